```python
import math
import jax
import jax.numpy as jnp
from jax import lax
import numpy as np

D_MODEL = 1024
BATCH = 8
SEQ = 2048
DEPTH = 1
DEC_BATCH = 128
DEC_SEQ = 1
PAST_LEN = 8192
PAGE_SIZE = 128

N_META = 16
GLA_HEADS = 4
GLA_DK = 64
GLA_DV = 128
GLA_LOWRANK = 16
GLA_GATE_TAU = 16.0
GLA_CHUNK = 64
SWA_HEADS = 8
SWA_KV_HEADS = 2
SWA_HEAD_DIM = 64
SWA_GROUP = SWA_HEADS // SWA_KV_HEADS
WINDOW = 128
SWA_BLOCK = 128
NUM_BUCKETS = 32
MAX_DISTANCE = 128
N_EXPERTS = 32
TOP_K = 4
D_FF = 1024
SWIGLU_ALPHA = 1.702
SWIGLU_LIMIT = 7.0
MOE_BLOCK = 128
RMS_EPS = 1e-6

GLA_QK = GLA_HEADS * GLA_DK
GLA_V = GLA_HEADS * GLA_DV
SWA_Q = SWA_HEADS * SWA_HEAD_DIM
SWA_KV = SWA_KV_HEADS * SWA_HEAD_DIM
MIX_WIDTH = GLA_V + SWA_Q
IN_SIZES = (GLA_QK, GLA_QK, GLA_V, GLA_V, GLA_LOWRANK, SWA_Q, SWA_KV, SWA_KV)
D_IN = GLA_QK * 2 + GLA_V * 2 + GLA_LOWRANK + SWA_Q + SWA_KV * 2

kernel_name = "hymba_gla_swa_sink_moe_step"


def _split_points(sizes):
    pts, acc = [], 0
    for s in sizes[:-1]:
        acc += s
        pts.append(acc)
    return pts


def rms_norm(x, g):
    xf = x.astype(jnp.float32)
    y = xf * lax.rsqrt(jnp.mean(xf * xf, axis=-1, keepdims=True) + RMS_EPS)
    return (y * g.astype(jnp.float32)).astype(x.dtype)


def t5_bucket(dist):
    n = jnp.maximum(dist, 0)
    max_exact = NUM_BUCKETS // 2
    nf = jnp.maximum(n, 1).astype(jnp.float32)
    large = max_exact + (jnp.log(nf / max_exact) / math.log(MAX_DISTANCE / max_exact)
                         * (NUM_BUCKETS - max_exact)).astype(jnp.int32)
    large = jnp.minimum(large, NUM_BUCKETS - 1)
    return jnp.where(n < max_exact, n, large)


def project_mixer_inputs(h, w_in, w_a_up, b_a):
    lead = h.shape[:-1]
    gq, gk, gv, gr, ga, sq, sk, sv = jnp.split(h @ w_in, _split_points(IN_SIZES), axis=-1)
    gq = gq.reshape(*lead, GLA_HEADS, GLA_DK) * (GLA_DK ** -0.5)
    gk = gk.reshape(*lead, GLA_HEADS, GLA_DK)
    gv = gv.reshape(*lead, GLA_HEADS, GLA_DV)
    log_a = jax.nn.log_sigmoid((ga @ w_a_up + b_a).astype(jnp.float32)) / GLA_GATE_TAU
    log_a = log_a.reshape(*lead, GLA_HEADS, GLA_DK)
    sq = sq.reshape(*lead, SWA_KV_HEADS, SWA_GROUP, SWA_HEAD_DIM)
    sk = sk.reshape(*lead, SWA_KV_HEADS, SWA_HEAD_DIM)
    sv = sv.reshape(*lead, SWA_KV_HEADS, SWA_HEAD_DIM)
    return gq, gk, gv, gr, log_a, sq, sk, sv


def gla_chunked(q, k, v, log_a, s0, chunk):
    B, L, H, dk = q.shape
    dv = v.shape[-1]
    nc = L // chunk
    def to_chunks(t):
        return t.astype(jnp.float32).reshape(B, nc, chunk, H, t.shape[-1]).transpose(1, 0, 3, 2, 4)
    qc, kc, vc = to_chunks(q), to_chunks(k), to_chunks(v)
    bc = jnp.cumsum(to_chunks(log_a), axis=-2)
    causal = jnp.tril(jnp.ones((chunk, chunk), jnp.float32))

    def step(S, inp):
        qi, ki, vi, bi = inp
        qt = qi * jnp.exp(bi)
        kt = ki * jnp.exp(-bi)
        A = jnp.einsum('bhtd,bhsd->bhts', qt, kt) * causal
        o = jnp.einsum('bhtd,bhde->bhte', qt, S) + jnp.einsum('bhts,bhse->bhte', A, vi)
        b_last = bi[..., -1:, :]
        S = jnp.exp(b_last[..., 0, :])[..., :, None] * S + jnp.einsum(
            'bhsd,bhse->bhde', ki * jnp.exp(b_last - bi), vi)
        return S, o

    S, o = lax.scan(step, s0.astype(jnp.float32), (qc, kc, vc, bc))
    o = o.transpose(1, 0, 3, 2, 4).reshape(B, L, H, dv)
    return o, S


def sink_attention(q, k, v, dist, valid, sinks, bias_table):
    s = jnp.einsum('...qhgd,...shd->...hgqs', q, k).astype(jnp.float32) * (SWA_HEAD_DIM ** -0.5)
    Qn, Kn = dist.shape
    bias = bias_table.astype(jnp.float32)[t5_bucket(dist)]
    bias = jnp.moveaxis(bias, -1, 0).reshape(SWA_KV_HEADS, SWA_GROUP, Qn, Kn)
    s = jnp.where(valid, s + bias, -jnp.inf)
    sink = sinks.astype(jnp.float32).reshape(SWA_KV_HEADS, SWA_GROUP, 1, 1)
    m = jnp.maximum(jnp.max(s, axis=-1, keepdims=True), sink)
    p = jnp.exp(s - m)
    p = p / (jnp.sum(p, axis=-1, keepdims=True) + jnp.exp(sink - m))
    return jnp.einsum('...hgqs,...shd->...qhgd', p.astype(v.dtype), v)


def swa_prompt(q, k, v, sinks, bias_table):
    B, L = q.shape[:2]
    nb = -(-L // SWA_BLOCK)
    pad = nb * SWA_BLOCK - L
    def pad_end(t):
        return jnp.pad(t, [(0, 0), (0, pad)] + [(0, 0)] * (t.ndim - 2))
    qb = pad_end(q).reshape(B, nb, SWA_BLOCK, SWA_KV_HEADS, SWA_GROUP, SWA_HEAD_DIM)
    kb = pad_end(k).reshape(B, nb, SWA_BLOCK, SWA_KV_HEADS, SWA_HEAD_DIM)
    vb = pad_end(v).reshape(B, nb, SWA_BLOCK, SWA_KV_HEADS, SWA_HEAD_DIM)
    def with_prev(t):
        prev = jnp.pad(t, [(0, 0), (1, 0), (0, 0), (0, 0), (0, 0)])[:, :-1]
        return jnp.concatenate([prev, t], axis=2)
    i = jnp.arange(SWA_BLOCK)[:, None]
    j = jnp.arange(2 * SWA_BLOCK)[None, :]
    dist = i - j + SWA_BLOCK
    blk = jnp.arange(nb)[:, None, None]
    key_exists = (blk * SWA_BLOCK - SWA_BLOCK + j[None]) >= 0
    valid = ((dist >= 0) & (dist < WINDOW))[None] & key_exists
    o = sink_attention(qb, with_prev(kb), with_prev(vb), dist, valid[:, None, None], sinks, bias_table)
    return o.reshape(B, nb * SWA_BLOCK, SWA_HEADS, SWA_HEAD_DIM)[:, :L]


def swa_sample(q, k, v, cache_k, cache_v, sinks, bias_table):
    S = q.shape[1]
    kk = jnp.concatenate([cache_k.astype(k.dtype), k], axis=1)
    vv = jnp.concatenate([cache_v.astype(v.dtype), v], axis=1)
    i = jnp.arange(S)[:, None]
    j = jnp.arange(WINDOW + S)[None, :]
    dist = i + WINDOW - j
    valid = (dist >= 0) & (dist < WINDOW)
    o = sink_attention(q, kk, vv, dist, valid, sinks, bias_table)
    return o.reshape(q.shape[0], S, SWA_HEADS, SWA_HEAD_DIM), kk[:, -WINDOW:], vv[:, -WINDOW:]


def combine_mixer_outputs(o_gla, gr, o_swa, g_gla_out, g_swa_out, w_out, dtype):
    lead = o_gla.shape[:-2]
    gate = jax.nn.silu(gr.astype(jnp.float32)).reshape(*lead, GLA_HEADS, GLA_DV)
    o_gla = (rms_norm(o_gla, g_gla_out) * gate).reshape(*lead, GLA_V)
    o_swa = rms_norm(o_swa.reshape(*lead, SWA_Q), g_swa_out)
    o = jnp.concatenate([o_gla.astype(dtype), o_swa.astype(dtype)], axis=-1)
    return o @ w_out


def mixer_prompt(h, mp, bias_table):
    w_in, w_a_up, b_a, g_gla_out, g_swa_out, sinks, w_out = mp
    gq, gk, gv, gr, log_a, sq, sk, sv = project_mixer_inputs(h, w_in, w_a_up, b_a)
    B, L = h.shape[:2]
    pad = (-L) % GLA_CHUNK
    def pad_front(t):
        return jnp.pad(t, [(0, 0), (pad, 0), (0, 0), (0, 0)])
    s0 = jnp.zeros((B, GLA_HEADS, GLA_DK, GLA_DV), jnp.float32)
    o_gla, s_new = gla_chunked(pad_front(gq), pad_front(gk), pad_front(gv), pad_front(log_a), s0, GLA_CHUNK)
    o_gla = o_gla[:, pad:]
    o_swa = swa_prompt(sq, sk, sv, sinks, bias_table)
    out = combine_mixer_outputs(o_gla, gr, o_swa, g_gla_out, g_swa_out, w_out, h.dtype)
    return out, s_new, sk[:, -WINDOW:], sv[:, -WINDOW:]


def mixer_sample(h, s0, cache_k, cache_v, mp, bias_table):
    w_in, w_a_up, b_a, g_gla_out, g_swa_out, sinks, w_out = mp
    gq, gk, gv, gr, log_a, sq, sk, sv = project_mixer_inputs(h, w_in, w_a_up, b_a)
    o_gla, s_new = gla_chunked(gq, gk, gv, log_a, s0, h.shape[1])
    o_swa, k_new, v_new = swa_sample(sq, sk, sv, cache_k, cache_v, sinks, bias_table)
    out = combine_mixer_outputs(o_gla, gr, o_swa, g_gla_out, g_swa_out, w_out, h.dtype)
    return out, s_new, k_new, v_new


def clamped_swiglu(hh):
    gate = jnp.minimum(hh[..., :D_FF], SWIGLU_LIMIT)
    lin = jnp.clip(hh[..., D_FF:], -SWIGLU_LIMIT, SWIGLU_LIMIT)
    return gate * jax.nn.sigmoid(SWIGLU_ALPHA * gate) * (lin + 1)


def moe_ffn(x2d, w_router, b_router, w_up, b_up, w_down, b_down):
    T, D = x2d.shape
    logits = (x2d @ w_router).astype(jnp.float32) + b_router.astype(jnp.float32)
    top_v, top_e = lax.top_k(logits, TOP_K)
    gates = jax.nn.softmax(top_v, axis=-1)
    n_slots = T * TOP_K
    slot_e = top_e.reshape(-1)
    slot_tok = jnp.repeat(jnp.arange(T, dtype=jnp.int32), TOP_K)
    slot_g = gates.reshape(-1)
    order = jnp.argsort(slot_e)
    e_sorted = slot_e[order]
    counts = jnp.bincount(slot_e, length=N_EXPERTS)
    padded = (counts + MOE_BLOCK - 1) // MOE_BLOCK * MOE_BLOCK
    pad_end = jnp.cumsum(padded)
    pad_start = pad_end - padded
    start = jnp.cumsum(counts) - counts
    dest = pad_start[e_sorted] + jnp.arange(n_slots) - start[e_sorted]
    n_blocks = -(-n_slots // MOE_BLOCK) + N_EXPERTS
    n_pad = n_blocks * MOE_BLOCK
    tok_of_slot = jnp.full((n_pad,), T, jnp.int32).at[dest].set(slot_tok[order])
    g_of_slot = jnp.zeros((n_pad,), jnp.float32).at[dest].set(slot_g[order])
    blk_e = jnp.minimum(jnp.searchsorted(pad_end, jnp.arange(n_blocks) * MOE_BLOCK, side='right'),
                        N_EXPERTS - 1)
    xs = jnp.concatenate([x2d, jnp.zeros((1, D), x2d.dtype)], axis=0)[tok_of_slot]
    xs = xs.reshape(n_blocks, MOE_BLOCK, D)

    def expert_block(args):
        xb, e = args
        hh = xb @ w_up[e] + b_up[e]
        return clamped_swiglu(hh) @ w_down[e] + b_down[e]

    ys = lax.map(expert_block, (xs, blk_e)).reshape(n_pad, D)
    ys = ys * g_of_slot[:, None].astype(ys.dtype)
    return jnp.zeros((T + 1, D), ys.dtype).at[tok_of_slot].add(ys)[:T]


def setup_inputs(seed: int = 0) -> dict:
    key = jax.random.key(seed)
    ks = iter(jax.random.split(key, 32))
    f32 = jnp.float32
    def nrm(shape, scale=1.0):
        return jax.random.normal(next(ks), shape, f32) * scale
    def gain(shape):
        return 1.0 + nrm(shape, 0.1)
    return {
        "x_prompt": nrm((BATCH, SEQ, D_MODEL)),
        "x_sample": nrm((DEC_BATCH, DEC_SEQ, D_MODEL)),
        "state_gla": nrm((DEPTH, DEC_BATCH, GLA_HEADS, GLA_DK, GLA_DV)),
        "cache_swa_k": nrm((DEPTH, DEC_BATCH, WINDOW, SWA_KV_HEADS, SWA_HEAD_DIM)),
        "cache_swa_v": nrm((DEPTH, DEC_BATCH, WINDOW, SWA_KV_HEADS, SWA_HEAD_DIM)),
        "meta_tokens": nrm((N_META, D_MODEL)),
        "rel_bias_table": nrm((NUM_BUCKETS, SWA_HEADS), 0.5),
        "g_mix": gain((DEPTH, D_MODEL)),
        "w_in": nrm((DEPTH, D_MODEL, D_IN), D_MODEL ** -0.5),
        "w_a_up": nrm((DEPTH, GLA_LOWRANK, GLA_QK), GLA_LOWRANK ** -0.5),
        "b_a": nrm((DEPTH, GLA_QK), 0.1),
        "g_gla_out": gain((DEPTH, GLA_DV)),
        "g_swa_out": gain((DEPTH, SWA_Q)),
        "attn_sinks": nrm((DEPTH, SWA_HEADS), 0.5),
        "w_out": nrm((DEPTH, MIX_WIDTH, D_MODEL), MIX_WIDTH ** -0.5),
        "g_ffn": gain((DEPTH, D_MODEL)),
        "w_router": nrm((DEPTH, D_MODEL, N_EXPERTS), D_MODEL ** -0.5),
        "b_router": nrm((DEPTH, N_EXPERTS), 0.01),
        "w_up": nrm((DEPTH, N_EXPERTS, D_MODEL, 2 * D_FF), D_MODEL ** -0.5),
        "b_up": nrm((DEPTH, N_EXPERTS, 2 * D_FF), 0.01),
        "w_down": nrm((DEPTH, N_EXPERTS, D_FF, D_MODEL), D_FF ** -0.5),
        "b_down": nrm((DEPTH, N_EXPERTS, D_MODEL), 0.01),
        "g_final": gain((D_MODEL,)),
    }


def reference(x_prompt, x_sample, state_gla, cache_swa_k, cache_swa_v, meta_tokens, rel_bias_table,
              g_mix, w_in, w_a_up, b_a, g_gla_out, g_swa_out, attn_sinks, w_out,
              g_ffn, w_router, b_router, w_up, b_up, w_down, b_down, g_final):
    B = x_prompt.shape[0]
    meta = jnp.broadcast_to(meta_tokens.astype(x_prompt.dtype)[None], (B, N_META, D_MODEL))
    xp = jnp.concatenate([meta, x_prompt], axis=1)
    xs = x_sample
    sp_list, kp_list, vp_list, ss_list, ks_list, vs_list = [], [], [], [], [], []
    for l in range(DEPTH):
        mp = (w_in[l], w_a_up[l], b_a[l], g_gla_out[l], g_swa_out[l], attn_sinks[l], w_out[l])
        mix_p, s_p, k_p, v_p = mixer_prompt(rms_norm(xp, g_mix[l]), mp, rel_bias_table)
        mix_s, s_s, k_s, v_s = mixer_sample(rms_norm(xs, g_mix[l]), state_gla[l], cache_swa_k[l],
                                            cache_swa_v[l], mp, rel_bias_table)
        xp = xp + mix_p
        xs = xs + mix_s
        if l == DEPTH - 1:
            xp = xp[:, N_META:]
        hp = rms_norm(xp, g_ffn[l])
        hs = rms_norm(xs, g_ffn[l])
        xp = xp + moe_ffn(hp.reshape(-1, D_MODEL), w_router[l], b_router[l], w_up[l], b_up[l],
                          w_down[l], b_down[l]).reshape(xp.shape)
        xs = xs + moe_ffn(hs.reshape(-1, D_MODEL), w_router[l], b_router[l], w_up[l], b_up[l],
                          w_down[l], b_down[l]).reshape(xs.shape)
        sp_list.append(s_p)
        kp_list.append(k_p)
        vp_list.append(v_p)
        ss_list.append(s_s)
        ks_list.append(k_s)
        vs_list.append(v_s)
    y_prompt = rms_norm(xp, g_final)
    y_sample = rms_norm(xs, g_final)
    new_state_gla_prompt = jnp.stack(sp_list)
    new_cache_swa_k_prompt = jnp.stack(kp_list)
    new_cache_swa_v_prompt = jnp.stack(vp_list)
    new_state_gla_sample = jnp.stack(ss_list)
    new_cache_swa_k_sample = jnp.stack(ks_list)
    new_cache_swa_v_sample = jnp.stack(vs_list)
    return (y_prompt, y_sample, new_state_gla_prompt, new_cache_swa_k_prompt, new_cache_swa_v_prompt,
            new_state_gla_sample, new_cache_swa_k_sample, new_cache_swa_v_sample)
```

```python
import functools
import math

import jax
import jax.numpy as jnp
from jax import lax
from jax.experimental import pallas as pl
from jax.experimental.pallas import tpu as pltpu

D_MODEL = 1024
N_META = 16
GLA_HEADS = 4
GLA_DK = 64
GLA_DV = 128
GLA_LOWRANK = 16
GLA_GATE_TAU = 16.0
GLA_CHUNK = 64
SWA_HEADS = 8
SWA_KV_HEADS = 2
SWA_HEAD_DIM = 64
SWA_GROUP = SWA_HEADS // SWA_KV_HEADS
WINDOW = 128
NUM_BUCKETS = 32
MAX_DISTANCE = 128
N_EXPERTS = 32
TOP_K = 4
D_FF = 1024
SWIGLU_ALPHA = 1.702
SWIGLU_LIMIT = 7.0
RMS_EPS = 1e-6

GLA_QK = GLA_HEADS * GLA_DK
GLA_V = GLA_HEADS * GLA_DV
SWA_Q = SWA_HEADS * SWA_HEAD_DIM
SWA_KV = SWA_KV_HEADS * SWA_HEAD_DIM
LANES = 128
C_GQ, C_GK, C_GV, C_GR = 0, GLA_QK, 2 * GLA_QK, 2 * GLA_QK + GLA_V
C_SQ = C_GR + GLA_V
C_SK = C_SQ + SWA_Q
C_SV = C_SK + SWA_KV
C_GA = C_SV + SWA_KV
D_PROJ = C_GA + LANES

MIX_TM = 256
MOE_TM = 256
DEC_SB = 16
VMEM_LIMIT = 56 * 1024 * 1024

F32 = jnp.float32
BF16 = jnp.bfloat16
NEG_INF = float("-inf")


def _dot(a, b):
    return jnp.dot(a, b, preferred_element_type=F32)


def _dot_nt(a, b):
    return lax.dot_general(a, b, (((1,), (1,)), ((), ())), preferred_element_type=F32)


def _split3(x):
    hi = x.astype(BF16)
    r1 = x - hi.astype(F32)
    mid = r1.astype(BF16)
    lo = (r1 - mid.astype(F32)).astype(BF16)
    return hi, mid, lo


def _rms(x, g):
    return x * lax.rsqrt(jnp.mean(x * x, axis=-1, keepdims=True) + RMS_EPS) * g


def _iota(shape, dim):
    return lax.broadcasted_iota(jnp.int32, shape, dim)


def _project(x, g_mix, w_in, w_a_up, b_a):
    h = _rms(x, g_mix).astype(BF16)
    proj = _dot(h, w_in)
    ga = proj[:, C_GA:C_GA + LANES].astype(BF16)
    z = _dot(ga, w_a_up) + b_a
    log_a = -(jnp.maximum(-z, 0.0) + jnp.log1p(jnp.exp(-jnp.abs(z)))) / GLA_GATE_TAU
    return dict(
        gq=proj[:, C_GQ:C_GQ + GLA_QK] * (GLA_DK ** -0.5),
        gk=proj[:, C_GK:C_GK + GLA_QK],
        gv=proj[:, C_GV:C_GV + GLA_V],
        gr=proj[:, C_GR:C_GR + GLA_V],
        sq=proj[:, C_SQ:C_SQ + SWA_Q],
        sk=proj[:, C_SK:C_SK + SWA_KV],
        sv=proj[:, C_SV:C_SV + SWA_KV],
        log_a=log_a,
    )


def _tail(x, o_gla, gr, o_swa, g_gla_out, g_swa_out, w_out, g_ffn, w_r, b_r, base):
    tm = x.shape[0]
    gate = gr * jax.nn.sigmoid(gr)
    parts = []
    for h in range(GLA_HEADS):
        sl = slice(h * GLA_DV, (h + 1) * GLA_DV)
        parts.append(_rms(o_gla[:, sl], g_gla_out) * gate[:, sl])
    parts.append(_rms(o_swa, g_swa_out))
    o = jnp.concatenate(parts, axis=1).astype(BF16)
    x_mid = x + _dot(o, w_out)
    hp = _rms(x_mid, g_ffn)

    h1 = hp.astype(BF16)
    h2 = (hp - h1.astype(F32)).astype(BF16)
    la = _dot_nt(w_r, h1)
    lb = _dot_nt(w_r[0:N_EXPERTS], h2)
    logits = la[0:N_EXPERTS] + la[N_EXPERTS:2 * N_EXPERTS] + lb + b_r

    eidx = _iota((N_EXPERTS, tm), 0)
    vals, idxs, onehots = [], [], []
    l = logits
    for _ in range(TOP_K):
        m = jnp.max(l, axis=0, keepdims=True)
        sel = jnp.min(jnp.where(l == m, eidx, N_EXPERTS), axis=0, keepdims=True)
        oh = eidx == sel
        l = jnp.where(oh, NEG_INF, l)
        vals.append(m)
        idxs.append(sel)
        onehots.append(oh)
    es = [jnp.exp(v - vals[0]) for v in vals]
    denom = es[0] + es[1] + es[2] + es[3]
    gates = [e / denom for e in es]

    ohf = jnp.concatenate([oh.astype(F32) for oh in onehots], axis=0)
    upper = (_iota((tm, tm), 0) < _iota((tm, tm), 1)).astype(BF16)
    prefix = _dot(ohf.astype(BF16), upper)
    ranks = []
    for k in range(TOP_K):
        sl = slice(k * N_EXPERTS, (k + 1) * N_EXPERTS)
        ohk = ohf[sl]
        base_t = jnp.concatenate([base] * (tm // LANES), axis=1)
        ranks.append(jnp.sum(ohk * (prefix[sl] + base_t), axis=0, keepdims=True))
        base = base + jnp.sum(ohk, axis=1, keepdims=True)
    zi = jnp.zeros((8 - TOP_K, tm), jnp.int32)
    zf = jnp.zeros((8 - TOP_K, tm), F32)
    topi = jnp.concatenate(idxs + [zi], axis=0)
    gate8 = jnp.concatenate(gates + [zf], axis=0)
    rank8 = jnp.concatenate([r.astype(jnp.int32) for r in ranks] + [zi], axis=0)
    return x_mid, hp, topi, gate8, rank8, base


def _gla_chunks(p, row0, s_bd, n_lead_pad):
    tm = p["gq"].shape[0]
    nch = tm // GLA_CHUNK
    log_a = p["log_a"]
    if n_lead_pad:
        rows = row0 + _iota((tm, GLA_QK), 0)
        log_a = jnp.where(rows >= n_lead_pad, log_a, 0.0)
    ri, ci = _iota((tm, tm), 0), _iota((tm, tm), 1)
    tril = ((ri >= ci) & (ri // GLA_CHUNK == ci // GLA_CHUNK)).astype(BF16)
    hi, mid, lo = _split3(log_a)
    b_all = _dot(tril, hi) + _dot(tril, mid) + _dot(tril, lo)

    c64 = GLA_CHUNK
    kk_mask = (_iota((GLA_QK, GLA_QK), 0) // c64) == (_iota((GLA_QK, GLA_QK), 1) // GLA_DK)
    vv_mask = (_iota((GLA_QK, GLA_V), 0) // c64) == (_iota((GLA_QK, GLA_V), 1) // GLA_DV)
    ss_mask = (_iota((GLA_QK, GLA_V), 0) // GLA_DK) == (_iota((GLA_QK, GLA_V), 1) // GLA_DV)
    causal = (_iota((c64, GLA_QK), 0) >= (_iota((c64, GLA_QK), 1) % c64)).astype(F32)
    zpad_k = jnp.zeros((LANES - c64, GLA_QK), F32)
    zpad_v = jnp.zeros((LANES - c64, GLA_V), BF16)

    outs = []
    for c in range(nch):
        rs = slice(c * c64, (c + 1) * c64)
        b = b_all[rs]
        q, k, v = p["gq"][rs], p["gk"][rs], p["gv"][rs]
        b_last = b[c64 - 1:c64]
        qt = (q * jnp.exp(b)).astype(BF16)
        kt = k * jnp.exp(-b)
        kd = k * jnp.exp(b_last - b)
        vb = v.astype(BF16)
        k_bd = jnp.where(kk_mask, jnp.concatenate([kt] * GLA_HEADS, axis=0), 0.0).astype(BF16)
        a = (_dot_nt(qt, k_bd) * causal).astype(BF16)
        v_bd = jnp.where(vv_mask, jnp.concatenate([vb] * GLA_HEADS, axis=0), jnp.zeros((), BF16))
        outs.append(_dot(qt, s_bd.astype(BF16)) + _dot(a, v_bd))
        kd_t = jnp.transpose(jnp.concatenate([kd, zpad_k], axis=0)).astype(BF16)
        upd = _dot(kd_t, jnp.concatenate([vb, zpad_v], axis=0))
        decay = jnp.exp(jnp.transpose(jnp.broadcast_to(b_last, (LANES, GLA_QK))))
        s_bd = s_bd * jnp.concatenate([decay] * GLA_HEADS, axis=1) + jnp.where(ss_mask, upd, 0.0)
    return jnp.concatenate(outs, axis=0), s_bd


def _swa_block(sq, kcat, vcat, bias_ref, sinks_ref, valid):
    half = _iota((1, LANES), 1) < SWA_HEAD_DIM
    k_roll = pltpu.roll(kcat, SWA_HEAD_DIM, 1)
    v_roll = pltpu.roll(vcat, SWA_HEAD_DIM, 1)
    cols = []
    for kv in range(SWA_KV_HEADS):
        if kv == 0:
            kk = jnp.where(half, kcat, k_roll)
            v_lo = jnp.where(half, vcat, 0.0)
            v_hi = jnp.where(half, 0.0, v_roll)
        else:
            kk = jnp.where(half, k_roll, kcat)
            v_lo = jnp.where(half, v_roll, 0.0)
            v_hi = jnp.where(half, 0.0, vcat)
        q_parts = []
        for c in (2 * kv, 2 * kv + 1):
            qc = sq[:, c * LANES:(c + 1) * LANES]
            q_parts.append(jnp.where(half, qc, 0.0))
            q_parts.append(jnp.where(half, 0.0, qc))
        q_st = jnp.concatenate(q_parts, axis=0).astype(BF16)
        s = _dot_nt(q_st, kk.astype(BF16)) * (SWA_HEAD_DIM ** -0.5) + bias_ref[kv]
        s = jnp.where(valid, s, NEG_INF)
        sink = jnp.concatenate(
            [jnp.full((WINDOW, 1), sinks_ref[kv * SWA_GROUP + g], F32) for g in range(SWA_GROUP)], axis=0)
        m = jnp.maximum(jnp.max(s, axis=1, keepdims=True), sink)
        pr = jnp.exp(s - m)
        inv = 1.0 / (jnp.sum(pr, axis=1, keepdims=True) + jnp.exp(sink - m))
        pb = pr.astype(BF16)
        p2 = jnp.concatenate([
            jnp.concatenate([pb[0:128], pb[128:256]], axis=1),
            jnp.concatenate([pb[256:384], pb[384:512]], axis=1)], axis=0)
        vv = jnp.concatenate([v_lo, v_hi], axis=0).astype(BF16)
        o2 = _dot(p2, vv)
        cols.append(o2[0:128] * jnp.where(half, inv[0:128], inv[128:256]))
        cols.append(o2[128:256] * jnp.where(half, inv[256:384], inv[384:512]))
    return jnp.concatenate(cols, axis=1)


def _mixer_kernel(sinks_ref, x_ref, s0_ref, k0_ref, v0_ref, base0_ref, bias_ref,
                  g_mix_ref, w_in_ref, w_a_up_ref, b_a_ref, g_gla_ref, g_swa_ref, w_out_ref,
                  g_ffn_ref, w_r_ref, b_r_ref,
                  xmid_ref, hp_ref, topi_ref, gate_ref, rank_ref, sout_ref, kout_ref, vout_ref, cnt_ref,
                  s_scr, k_scr, v_scr, base_scr, *, n_lead_pad, prev_valid_from):
    b_id, j = pl.program_id(0), pl.program_id(1)
    tm = x_ref.shape[1]

    @pl.when(j == 0)
    def _():
        s_scr[...] = s0_ref[...]
        k_scr[...] = k0_ref[...]
        v_scr[...] = v0_ref[...]

    @pl.when((j == 0) & (b_id == 0))
    def _():
        base_scr[...] = base0_ref[...]

    x = x_ref[0]
    p = _project(x, g_mix_ref[...], w_in_ref[...], w_a_up_ref[...], b_a_ref[...])

    o_gla, s_new = _gla_chunks(p, j * tm, s_scr[...], n_lead_pad)
    s_scr[...] = s_new

    qi = _iota((WINDOW, 2 * WINDOW), 0)
    kj = _iota((WINDOW, 2 * WINDOW), 1)
    band = (kj > qi) & (kj <= qi + WINDOW)
    o_parts = []
    for sb in range(tm // WINDOW):
        rs = slice(sb * WINDOW, (sb + 1) * WINDOW)
        k_blk, v_blk = p["sk"][rs], p["sv"][rs]
        k_prev = k_scr[...] if sb == 0 else p["sk"][(sb - 1) * WINDOW:sb * WINDOW]
        v_prev = v_scr[...] if sb == 0 else p["sv"][(sb - 1) * WINDOW:sb * WINDOW]
        valid = band
        if sb == 0 and prev_valid_from:
            first = jnp.where(j == 0, prev_valid_from, 0)
            valid = band & (kj >= first)
        valid = jnp.concatenate([valid] * SWA_GROUP, axis=0)
        o_parts.append(_swa_block(p["sq"][rs], jnp.concatenate([k_prev, k_blk], axis=0),
                                  jnp.concatenate([v_prev, v_blk], axis=0), bias_ref, sinks_ref, valid))
    o_swa = jnp.concatenate(o_parts, axis=0)
    k_scr[...] = p["sk"][tm - WINDOW:tm]
    v_scr[...] = p["sv"][tm - WINDOW:tm]

    x_mid, hp, topi, gate8, rank8, base = _tail(
        x, o_gla, p["gr"], o_swa, g_gla_ref[...], g_swa_ref[...], w_out_ref[...],
        g_ffn_ref[...], w_r_ref[...], b_r_ref[...], base_scr[...])
    base_scr[...] = base
    xmid_ref[0] = x_mid
    hp_ref[...] = hp
    topi_ref[...] = topi
    gate_ref[...] = gate8
    rank_ref[...] = rank8
    sout_ref[0] = s_new
    kout_ref[0] = p["sk"][tm - WINDOW:tm]
    vout_ref[0] = p["sv"][tm - WINDOW:tm]
    cnt_ref[...] = base


def _full_spec(shape):
    nd = len(shape)
    return pl.BlockSpec(shape, lambda *_: (0,) * nd)


def _mixer_call(x, s0, k0, v0, base0, wts, tm, n_lead_pad, prev_valid_from, hp_rows):
    B, L, _ = x.shape
    nj = L // tm
    T = B * L
    weight_args = (wts["bias"], wts["g_mix"], wts["w_in"], wts["w_a_up"], wts["b_a"], wts["g_gla"],
                   wts["g_swa"], wts["w_out"], wts["g_ffn"], wts["w_r"], wts["b_r"])
    in_specs = [
        pl.BlockSpec(memory_space=pltpu.SMEM),
        pl.BlockSpec((1, tm, D_MODEL), lambda b, j: (b, j, 0)),
        _full_spec(s0.shape), _full_spec(k0.shape), _full_spec(v0.shape), _full_spec(base0.shape),
    ] + [_full_spec(w.shape) for w in weight_args]
    tok_spec = pl.BlockSpec((8, tm), lambda b, j: (0, b * nj + j))
    out_specs = [
        pl.BlockSpec((1, tm, D_MODEL), lambda b, j: (b, j, 0)),
        pl.BlockSpec((tm, D_MODEL), lambda b, j: (b * nj + j, 0)),
        tok_spec, tok_spec, tok_spec,
        pl.BlockSpec((1, GLA_QK, GLA_V), lambda b, j: (b, 0, 0)),
        pl.BlockSpec((1, WINDOW, SWA_KV), lambda b, j: (b, 0, 0)),
        pl.BlockSpec((1, WINDOW, SWA_KV), lambda b, j: (b, 0, 0)),
        _full_spec((N_EXPERTS, LANES)),
    ]
    out_shape = [
        jax.ShapeDtypeStruct((B, L, D_MODEL), F32),
        jax.ShapeDtypeStruct((hp_rows, D_MODEL), F32),
        jax.ShapeDtypeStruct((8, T), jnp.int32),
        jax.ShapeDtypeStruct((8, T), F32),
        jax.ShapeDtypeStruct((8, T), jnp.int32),
        jax.ShapeDtypeStruct((B, GLA_QK, GLA_V), F32),
        jax.ShapeDtypeStruct((B, WINDOW, SWA_KV), F32),
        jax.ShapeDtypeStruct((B, WINDOW, SWA_KV), F32),
        jax.ShapeDtypeStruct((N_EXPERTS, LANES), F32),
    ]
    kern = functools.partial(_mixer_kernel, n_lead_pad=n_lead_pad, prev_valid_from=prev_valid_from)
    return pl.pallas_call(
        kern,
        grid=(B, nj),
        in_specs=in_specs,
        out_specs=out_specs,
        out_shape=out_shape,
        scratch_shapes=[pltpu.VMEM((GLA_QK, GLA_V), F32), pltpu.VMEM((WINDOW, SWA_KV), F32),
                        pltpu.VMEM((WINDOW, SWA_KV), F32), pltpu.VMEM((N_EXPERTS, LANES), F32)],
        compiler_params=pltpu.CompilerParams(dimension_semantics=("arbitrary", "arbitrary"),
                                             vmem_limit_bytes=VMEM_LIMIT),
        name="mixer",
    )(wts["sinks"], x, s0, k0, v0, base0, *weight_args)


def _decode_kernel(sinks_ref, x_ref, st_ref, ck_ref, cv_ref, base0_ref, bias_ref,
                   g_mix_ref, w_in_ref, w_a_up_ref, b_a_ref, g_gla_ref, g_swa_ref, w_out_ref,
                   g_ffn_ref, w_r_ref, b_r_ref,
                   xmid_ref, hp_ref, topi_ref, gate_ref, rank_ref, sto_ref, cko_ref, cvo_ref, cnt_ref,
                   at_scr, kt_scr, qt_scr, gv_scr, gr_scr, sq_scr, sk_scr, sv_scr, og_scr, os_scr):
    i = pl.program_id(0)
    n_seq = x_ref.shape[0]

    @pl.when(i == 0)
    def _():
        p = _project(x_ref[...], g_mix_ref[...], w_in_ref[...], w_a_up_ref[...], b_a_ref[...])
        at_scr[...] = jnp.transpose(jnp.exp(p["log_a"]))
        kt_scr[...] = jnp.transpose(p["gk"])
        qt_scr[...] = jnp.transpose(p["gq"])
        gv_scr[...] = p["gv"]
        gr_scr[...] = p["gr"]
        sq_scr[...] = p["sq"]
        sk_scr[...] = p["sk"]
        sv_scr[...] = p["sv"]

    lane_seq = _iota((GLA_QK, n_seq), 1)
    half = _iota((1, LANES), 1) < SWA_HEAD_DIM
    row_id = _iota((WINDOW, SWA_KV), 0)
    head_diag = (_iota((16, SWA_Q), 1) // SWA_HEAD_DIM) == _iota((16, SWA_Q), 0)
    sink_col = jnp.concatenate(
        [jnp.full((1, 1), sinks_ref[h], F32) for h in range(SWA_HEADS)] + [jnp.zeros((8, 1), F32)], axis=0)

    def per_seq(sl, carry):
        s = i * DEC_SB + sl
        sel = lane_seq == s
        a_c = jnp.sum(jnp.where(sel, at_scr[...], 0.0), axis=1, keepdims=True)
        k_c = jnp.sum(jnp.where(sel, kt_scr[...], 0.0), axis=1, keepdims=True)
        q_c = jnp.sum(jnp.where(sel, qt_scr[...], 0.0), axis=1, keepdims=True)
        st = st_ref[sl].reshape(GLA_QK, GLA_DV)
        v_row = gv_scr[pl.ds(s, 1), :]
        v_b = jnp.concatenate(
            [jnp.broadcast_to(v_row[:, h * GLA_DV:(h + 1) * GLA_DV], (GLA_DK, GLA_DV))
             for h in range(GLA_HEADS)], axis=0)
        st_new = a_c * st + k_c * v_b
        sto_ref[sl] = st_new.reshape(GLA_HEADS, GLA_DK, GLA_DV)
        t = q_c * st_new
        og_scr[pl.ds(s, 1), :] = jnp.concatenate(
            [jnp.sum(t[h * GLA_DK:(h + 1) * GLA_DK], axis=0, keepdims=True) for h in range(GLA_HEADS)],
            axis=1)

        k_new = sk_scr[pl.ds(s, 1), :]
        v_new = sv_scr[pl.ds(s, 1), :]
        kn = jnp.where(row_id == WINDOW - 1, k_new, pltpu.roll(ck_ref[sl], WINDOW - 1, 0))
        vn = jnp.where(row_id == WINDOW - 1, v_new, pltpu.roll(cv_ref[sl], WINDOW - 1, 0))
        cko_ref[sl] = kn
        cvo_ref[sl] = vn
        kr, vr = pltpu.roll(kn, SWA_HEAD_DIM, 1), pltpu.roll(vn, SWA_HEAD_DIM, 1)
        k0, k1 = jnp.where(half, kn, kr), jnp.where(half, kr, kn)
        v0, v1 = jnp.where(half, vn, vr), jnp.where(half, vr, vn)
        kw = jnp.concatenate([k0, k0, k1, k1], axis=1).astype(BF16)
        vw = jnp.concatenate([v0, v0, v1, v1], axis=1).astype(BF16)
        q_row = sq_scr[pl.ds(s, 1), :]
        qm = jnp.where(head_diag, jnp.broadcast_to(q_row, (16, SWA_Q)), 0.0).astype(BF16)
        sc = _dot_nt(qm, kw) * (SWA_HEAD_DIM ** -0.5) + bias_ref[...]
        m = jnp.maximum(jnp.max(sc, axis=1, keepdims=True), sink_col)
        pr = jnp.exp(sc - m)
        inv = 1.0 / (jnp.sum(pr, axis=1, keepdims=True) + jnp.exp(sink_col - m))
        ow = _dot(pr.astype(BF16), vw) * inv
        os_scr[pl.ds(s, 1), :] = jnp.sum(jnp.where(head_diag, ow, 0.0), axis=0, keepdims=True)
        return carry

    lax.fori_loop(0, DEC_SB, per_seq, 0)

    @pl.when(i == pl.num_programs(0) - 1)
    def _():
        x_mid, hp, topi, gate8, rank8, base = _tail(
            x_ref[...], og_scr[...], gr_scr[...], os_scr[...], g_gla_ref[...], g_swa_ref[...],
            w_out_ref[...], g_ffn_ref[...], w_r_ref[...], b_r_ref[...], base0_ref[...])
        xmid_ref[...] = x_mid
        hp_ref[...] = hp
        topi_ref[...] = topi
        gate_ref[...] = gate8
        rank_ref[...] = rank8
        cnt_ref[...] = base


def _decode_call(xs, state, ck, cv, base0, bias_dec, wts):
    n_seq = xs.shape[0]
    nb = n_seq // DEC_SB
    weight_args = (wts["g_mix"], wts["w_in"], wts["w_a_up"], wts["b_a"], wts["g_gla"],
                   wts["g_swa"], wts["w_out"], wts["g_ffn"], wts["w_r"], wts["b_r"])
    in_specs = [
        pl.BlockSpec(memory_space=pltpu.SMEM),
        _full_spec(xs.shape),
        pl.BlockSpec((DEC_SB, GLA_HEADS, GLA_DK, GLA_DV), lambda i: (i, 0, 0, 0)),
        pl.BlockSpec((DEC_SB, WINDOW, SWA_KV), lambda i: (i, 0, 0)),
        pl.BlockSpec((DEC_SB, WINDOW, SWA_KV), lambda i: (i, 0, 0)),
        _full_spec(base0.shape), _full_spec(bias_dec.shape),
    ] + [_full_spec(w.shape) for w in weight_args]
    out_specs = [
        _full_spec((n_seq, D_MODEL)),
        _full_spec((n_seq, D_MODEL)),
        _full_spec((8, n_seq)), _full_spec((8, n_seq)), _full_spec((8, n_seq)),
        pl.BlockSpec((DEC_SB, GLA_HEADS, GLA_DK, GLA_DV), lambda i: (i, 0, 0, 0)),
        pl.BlockSpec((DEC_SB, WINDOW, SWA_KV), lambda i: (i, 0, 0)),
        pl.BlockSpec((DEC_SB, WINDOW, SWA_KV), lambda i: (i, 0, 0)),
        _full_spec((N_EXPERTS, LANES)),
    ]
    out_shape = [
        jax.ShapeDtypeStruct((n_seq, D_MODEL), F32),
        jax.ShapeDtypeStruct((n_seq, D_MODEL), F32),
        jax.ShapeDtypeStruct((8, n_seq), jnp.int32),
        jax.ShapeDtypeStruct((8, n_seq), F32),
        jax.ShapeDtypeStruct((8, n_seq), jnp.int32),
        jax.ShapeDtypeStruct(state.shape, F32),
        jax.ShapeDtypeStruct(ck.shape, F32),
        jax.ShapeDtypeStruct(cv.shape, F32),
        jax.ShapeDtypeStruct((N_EXPERTS, LANES), F32),
    ]
    scratch = [pltpu.VMEM((GLA_QK, n_seq), F32)] * 3 + [
        pltpu.VMEM((n_seq, GLA_V), F32), pltpu.VMEM((n_seq, GLA_V), F32), pltpu.VMEM((n_seq, SWA_Q), F32),
        pltpu.VMEM((n_seq, SWA_KV), F32), pltpu.VMEM((n_seq, SWA_KV), F32),
        pltpu.VMEM((n_seq, GLA_V), F32), pltpu.VMEM((n_seq, SWA_Q), F32)]
    return pl.pallas_call(
        _decode_kernel,
        grid=(nb,),
        in_specs=in_specs,
        out_specs=out_specs,
        out_shape=out_shape,
        scratch_shapes=scratch,
        compiler_params=pltpu.CompilerParams(dimension_semantics=("arbitrary",),
                                             vmem_limit_bytes=VMEM_LIMIT),
        name="decode",
    )(wts["sinks"], xs, state, ck, cv, base0, bias_dec, *weight_args)


def _row_copy(src_hbm, row, dst, r, sem):
    return pltpu.make_async_copy(src_hbm.at[pl.ds(row, 1)], dst.at[pl.ds(r, 1)], sem)


def _expert_kernel(blk_e_ref, nused_ref, tok_ref, hp_hbm, hps_hbm, wu_ref, bu_ref, wd_ref, bd_ref, y_ref,
                   xbuf, wu_bf, wd_bf, sem):
    i = pl.program_id(0)
    tm = xbuf.shape[0]
    n_prompt = hp_hbm.shape[0]

    @pl.when(i < nused_ref[0])
    def _():
        def issue(r, c):
            tok = tok_ref[0, 0, r]

            @pl.when(tok < n_prompt)
            def _():
                _row_copy(hp_hbm, tok, xbuf, r, sem).start()

            @pl.when(tok >= n_prompt)
            def _():
                _row_copy(hps_hbm, tok - n_prompt, xbuf, r, sem).start()
            return c
        lax.fori_loop(0, tm, issue, 0)

        prev = blk_e_ref[jnp.maximum(i - 1, 0)]

        @pl.when((i == 0) | (blk_e_ref[i] != prev))
        def _():
            wu_bf[...] = wu_ref[0].astype(BF16)
            wd_bf[...] = wd_ref[0].astype(BF16)

        def drain(r, c):
            _row_copy(hp_hbm, 0, xbuf, r, sem).wait()
            return c
        lax.fori_loop(0, tm, drain, 0)

        x = xbuf[...].astype(BF16)
        hh = _dot(x, wu_bf[...]) + bu_ref[0]
        g = jnp.minimum(hh[:, :D_FF], SWIGLU_LIMIT)
        lin = jnp.clip(hh[:, D_FF:], -SWIGLU_LIMIT, SWIGLU_LIMIT)
        act = g * jax.nn.sigmoid(SWIGLU_ALPHA * g) * (lin + 1.0)
        y_ref[...] = _dot(act.astype(BF16), wd_bf[...]) + bd_ref[0]

    @pl.when(i >= nused_ref[0])
    def _():
        y_ref[...] = jnp.zeros_like(y_ref)


def _expert_call(blk_e, nused, tok_of_slot, hp_p, hp_s, w_up, b_up, w_down, b_down):
    n_blocks = blk_e.shape[0]
    tm = MOE_TM
    grid_spec = pltpu.PrefetchScalarGridSpec(
        num_scalar_prefetch=2,
        grid=(n_blocks,),
        in_specs=[
            pl.BlockSpec((1, 1, tm), lambda i, be, nu: (i, 0, 0), memory_space=pltpu.SMEM),
            pl.BlockSpec(memory_space=pl.ANY),
            pl.BlockSpec(memory_space=pl.ANY),
            pl.BlockSpec((1, D_MODEL, 2 * D_FF), lambda i, be, nu: (be[i], 0, 0)),
            pl.BlockSpec((1, 1, 2 * D_FF), lambda i, be, nu: (be[i], 0, 0)),
            pl.BlockSpec((1, D_FF, D_MODEL), lambda i, be, nu: (be[i], 0, 0)),
            pl.BlockSpec((1, 1, D_MODEL), lambda i, be, nu: (be[i], 0, 0)),
        ],
        out_specs=pl.BlockSpec((tm, D_MODEL), lambda i, be, nu: (i, 0)),
        scratch_shapes=[pltpu.VMEM((tm, D_MODEL), F32), pltpu.VMEM((D_MODEL, 2 * D_FF), BF16),
                        pltpu.VMEM((D_FF, D_MODEL), BF16), pltpu.SemaphoreType.DMA],
    )
    return pl.pallas_call(
        _expert_kernel,
        grid_spec=grid_spec,
        out_shape=jax.ShapeDtypeStruct((n_blocks * tm, D_MODEL), F32),
        compiler_params=pltpu.CompilerParams(dimension_semantics=("arbitrary",),
                                             vmem_limit_bytes=VMEM_LIMIT),
        name="experts",
    )(blk_e, nused, tok_of_slot.reshape(n_blocks, 1, tm), hp_p, hp_s,
      w_up, b_up.reshape(N_EXPERTS, 1, 2 * D_FF), w_down, b_down.reshape(N_EXPERTS, 1, D_MODEL))


def _combine_kernel(dest_ref, ys_hbm, xmid_ref, gate_ref, g_final_ref, y_ref, buf, sem):
    tm = xmid_ref.shape[0]

    def issue(t, c):
        for k in range(TOP_K):
            _row_copy(ys_hbm, dest_ref[0, k, t], buf.at[k], t, sem).start()
        return c
    lax.fori_loop(0, tm, issue, 0)

    def drain(t, c):
        for k in range(TOP_K):
            _row_copy(ys_hbm, 0, buf.at[k], t, sem).wait()
        return c
    lax.fori_loop(0, tm, drain, 0)

    acc = xmid_ref[...]
    gts = gate_ref[...]
    for k in range(TOP_K):
        acc = acc + buf[k] * gts[:, k:k + 1]
    y_ref[...] = _rms(acc, g_final_ref[...])


def _combine_call(dest, ys, x_mid, gates, g_final, tm):
    T = x_mid.shape[0]
    n = T // tm
    return pl.pallas_call(
        _combine_kernel,
        grid=(n,),
        in_specs=[
            pl.BlockSpec((1, TOP_K, tm), lambda i: (i, 0, 0), memory_space=pltpu.SMEM),
            pl.BlockSpec(memory_space=pl.ANY),
            pl.BlockSpec((tm, D_MODEL), lambda i: (i, 0)),
            pl.BlockSpec((tm, TOP_K), lambda i: (i, 0)),
            _full_spec((1, D_MODEL)),
        ],
        out_specs=pl.BlockSpec((tm, D_MODEL), lambda i: (i, 0)),
        out_shape=jax.ShapeDtypeStruct((T, D_MODEL), F32),
        scratch_shapes=[pltpu.VMEM((TOP_K, tm, D_MODEL), F32), pltpu.SemaphoreType.DMA],
        compiler_params=pltpu.CompilerParams(dimension_semantics=("arbitrary",),
                                             vmem_limit_bytes=VMEM_LIMIT),
        name="combine",
    )(dest.reshape(TOP_K, n, tm).transpose(1, 0, 2), ys, x_mid, gates, g_final)


def _t5_bucket(dist):
    n = jnp.maximum(dist, 0)
    max_exact = NUM_BUCKETS // 2
    nf = jnp.maximum(n, 1).astype(F32)
    large = max_exact + (jnp.log(nf / max_exact) / math.log(MAX_DISTANCE / max_exact)
                         * (NUM_BUCKETS - max_exact)).astype(jnp.int32)
    large = jnp.minimum(large, NUM_BUCKETS - 1)
    return jnp.where(n < max_exact, n, large)


def kernel(x_prompt, x_sample, state_gla, cache_swa_k, cache_swa_v, meta_tokens, rel_bias_table,
           g_mix, w_in, w_a_up, b_a, g_gla_out, g_swa_out, attn_sinks, w_out,
           g_ffn, w_router, b_router, w_up, b_up, w_down, b_down, g_final):
    assert g_mix.shape[0] == 1, "single-layer trunk"
    B, L, _ = x_prompt.shape
    n_seq = x_sample.shape[0]
    TP = B * L
    T_all = TP + n_seq

    wi = w_in[0]
    sizes = (GLA_QK, GLA_QK, GLA_V, GLA_V, GLA_LOWRANK, SWA_Q, SWA_KV, SWA_KV)
    offs = [0]
    for s in sizes:
        offs.append(offs[-1] + s)
    seg = [wi[:, offs[n]:offs[n + 1]] for n in range(8)]
    w_in_r = jnp.concatenate(
        seg[0:4] + seg[5:8] + [seg[4], jnp.zeros((D_MODEL, LANES - GLA_LOWRANK), F32)], axis=1).astype(BF16)
    w_a_pad = jnp.concatenate([w_a_up[0], jnp.zeros((LANES - GLA_LOWRANK, GLA_QK), F32)], axis=0).astype(BF16)
    wr_t = jnp.transpose(w_router[0])
    wr_hi = wr_t.astype(BF16)
    wr_lo = (wr_t - wr_hi.astype(F32)).astype(BF16)
    qi = jnp.arange(WINDOW)[:, None]
    kj = jnp.arange(2 * WINDOW)[None, :]
    bias_p = jnp.moveaxis(rel_bias_table.astype(F32)[_t5_bucket(qi - kj + WINDOW)], -1, 0)
    bias_p = bias_p.reshape(SWA_KV_HEADS, SWA_GROUP * WINDOW, 2 * WINDOW)
    bias_d = jnp.transpose(rel_bias_table.astype(F32)[_t5_bucket(WINDOW - 1 - jnp.arange(WINDOW))])
    bias_d = jnp.concatenate([bias_d, jnp.zeros((8, WINDOW), F32)], axis=0)
    wts = dict(
        sinks=attn_sinks[0].astype(F32), bias=bias_p,
        g_mix=g_mix[0][None], w_in=w_in_r, w_a_up=w_a_pad, b_a=b_a[0][None],
        g_gla=g_gla_out[0][None], g_swa=g_swa_out[0][None], w_out=w_out[0].astype(BF16),
        g_ffn=g_ffn[0][None], w_r=jnp.concatenate([wr_hi, wr_lo], axis=0), b_r=b_router[0][:, None],
    )

    x_pre = jnp.concatenate([jnp.zeros((WINDOW - N_META, D_MODEL), F32), meta_tokens.astype(F32)], axis=0)[None]
    zeros_s = jnp.zeros((GLA_QK, GLA_V), F32)
    zeros_kv = jnp.zeros((WINDOW, SWA_KV), F32)
    zeros_b = jnp.zeros((N_EXPERTS, LANES), F32)
    pre = _mixer_call(x_pre, zeros_s, zeros_kv, zeros_kv, zeros_b, wts, WINDOW, WINDOW - N_META, 0, WINDOW)
    s_meta, k_meta, v_meta = pre[5][0], pre[6][0], pre[7][0]

    (xmid_p, hp_p, topi_p, gate_p, rank_p, s_p, k_p, v_p, cnt_p) = _mixer_call(
        x_prompt, s_meta, k_meta, v_meta, zeros_b, wts, MIX_TM, 0, WINDOW - N_META, TP)

    (xmid_s, hp_s, topi_s, gate_s, rank_s, st_s, ck_s, cv_s, cnt_all) = _decode_call(
        x_sample[:, 0], state_gla[0], cache_swa_k[0].reshape(n_seq, WINDOW, SWA_KV),
        cache_swa_v[0].reshape(n_seq, WINDOW, SWA_KV), cnt_p, bias_d, wts)

    tm = MOE_TM
    n_slots = T_all * TOP_K
    n_blocks = -(-n_slots // tm) + N_EXPERTS
    top_e = jnp.concatenate([topi_p[:TOP_K], topi_s[:TOP_K]], axis=1)
    rank = jnp.concatenate([rank_p[:TOP_K], rank_s[:TOP_K]], axis=1)
    counts = cnt_all[:, 0].astype(jnp.int32)
    padded = (counts + tm - 1) // tm * tm
    pad_end = jnp.cumsum(padded)
    pad_start = pad_end - padded
    dest = pad_start[top_e] + rank
    tok = jnp.broadcast_to(jnp.arange(T_all, dtype=jnp.int32)[None], (TOP_K, T_all))
    tok_of_slot = jnp.zeros((n_blocks * tm,), jnp.int32).at[dest.reshape(-1)].set(tok.reshape(-1))
    blk_e = jnp.minimum(jnp.searchsorted(pad_end, jnp.arange(n_blocks) * tm, side="right"),
                        N_EXPERTS - 1).astype(jnp.int32)
    nused = (pad_end[-1] // tm).astype(jnp.int32).reshape(1)

    ys = _expert_call(blk_e, nused, tok_of_slot, hp_p, hp_s, w_up[0], b_up[0], w_down[0], b_down[0])

    gates = jnp.transpose(jnp.concatenate([gate_p[:TOP_K], gate_s[:TOP_K]], axis=1))
    gf = g_final[None]
    y_p = _combine_call(dest[:, :TP], ys, xmid_p.reshape(TP, D_MODEL), gates[:TP], gf, MIX_TM)
    y_s = _combine_call(dest[:, TP:], ys, xmid_s, gates[TP:], gf, n_seq)

    s_heads = jnp.stack([s_p[:, h * GLA_DK:(h + 1) * GLA_DK, h * GLA_DV:(h + 1) * GLA_DV]
                         for h in range(GLA_HEADS)], axis=1)
    return (y_p.reshape(B, L, D_MODEL), y_s.reshape(n_seq, 1, D_MODEL), s_heads[None],
            k_p.reshape(1, B, WINDOW, SWA_KV_HEADS, SWA_HEAD_DIM),
            v_p.reshape(1, B, WINDOW, SWA_KV_HEADS, SWA_HEAD_DIM),
            st_s[None], ck_s.reshape(1, n_seq, WINDOW, SWA_KV_HEADS, SWA_HEAD_DIM),
            cv_s.reshape(1, n_seq, WINDOW, SWA_KV_HEADS, SWA_HEAD_DIM))
```

```python
import functools
import math

import jax
import jax.numpy as jnp
from jax import lax
from jax.experimental import pallas as pl
from jax.experimental.pallas import tpu as pltpu

D_MODEL = 1024
N_META = 16
GLA_HEADS = 4
GLA_DK = 64
GLA_DV = 128
GLA_LOWRANK = 16
GLA_GATE_TAU = 16.0
GLA_CHUNK = 64
SWA_HEADS = 8
SWA_KV_HEADS = 2
SWA_HEAD_DIM = 64
SWA_GROUP = SWA_HEADS // SWA_KV_HEADS
WINDOW = 128
NUM_BUCKETS = 32
MAX_DISTANCE = 128
N_EXPERTS = 32
TOP_K = 4
D_FF = 1024
SWIGLU_ALPHA = 1.702
SWIGLU_LIMIT = 7.0
RMS_EPS = 1e-6

GLA_QK = GLA_HEADS * GLA_DK
GLA_V = GLA_HEADS * GLA_DV
SWA_Q = SWA_HEADS * SWA_HEAD_DIM
SWA_KV = SWA_KV_HEADS * SWA_HEAD_DIM
LANES = 128
C_GQ, C_GK, C_GV, C_GR = 0, GLA_QK, 2 * GLA_QK, 2 * GLA_QK + GLA_V
C_SQ = C_GR + GLA_V
C_SK = C_SQ + SWA_Q
C_SV = C_SK + SWA_KV
C_GA = C_SV + SWA_KV
D_PROJ = C_GA + LANES

MIX_TM = 256
MOE_TM = 256
DEC_SB = 16
VMEM_LIMIT = 56 * 1024 * 1024

F32 = jnp.float32
BF16 = jnp.bfloat16
NEG_INF = float("-inf")


def _dot(a, b):
    return jnp.dot(a, b, preferred_element_type=F32)


def _dot_nt(a, b):
    return lax.dot_general(a, b, (((1,), (1,)), ((), ())), preferred_element_type=F32)


def _split3(x):
    hi = x.astype(BF16)
    r1 = x - hi.astype(F32)
    mid = r1.astype(BF16)
    lo = (r1 - mid.astype(F32)).astype(BF16)
    return hi, mid, lo


def _rms(x, g):
    return x * lax.rsqrt(jnp.mean(x * x, axis=-1, keepdims=True) + RMS_EPS) * g


def _iota(shape, dim):
    return lax.broadcasted_iota(jnp.int32, shape, dim)


def _project(x, g_mix, w_in, w_a_up, b_a):
    h = _rms(x, g_mix).astype(BF16)
    proj = _dot(h, w_in)
    ga = proj[:, C_GA:C_GA + LANES].astype(BF16)
    z = _dot(ga, w_a_up) + b_a
    log_a = -(jnp.maximum(-z, 0.0) + jnp.log1p(jnp.exp(-jnp.abs(z)))) / GLA_GATE_TAU
    return dict(
        gq=proj[:, C_GQ:C_GQ + GLA_QK] * (GLA_DK ** -0.5),
        gk=proj[:, C_GK:C_GK + GLA_QK],
        gv=proj[:, C_GV:C_GV + GLA_V],
        gr=proj[:, C_GR:C_GR + GLA_V],
        sq=proj[:, C_SQ:C_SQ + SWA_Q],
        sk=proj[:, C_SK:C_SK + SWA_KV],
        sv=proj[:, C_SV:C_SV + SWA_KV],
        log_a=log_a,
    )


def _tail(x, o_gla, gr, o_swa, g_gla_out, g_swa_out, w_out, g_ffn, w_r, b_r, base):
    tm = x.shape[0]
    gate = gr * jax.nn.sigmoid(gr)
    parts = []
    for h in range(GLA_HEADS):
        sl = slice(h * GLA_DV, (h + 1) * GLA_DV)
        parts.append(_rms(o_gla[:, sl], g_gla_out) * gate[:, sl])
    parts.append(_rms(o_swa, g_swa_out))
    o = jnp.concatenate(parts, axis=1).astype(BF16)
    x_mid = x + _dot(o, w_out)
    hp = _rms(x_mid, g_ffn)

    h1 = hp.astype(BF16)
    h2 = (hp - h1.astype(F32)).astype(BF16)
    bits = lax.bitcast_convert_type(h1.astype(F32), jnp.uint32)
    half_d = D_MODEL // 2
    hp = bits[:, half_d:] | lax.shift_right_logical(bits[:, :half_d], jnp.uint32(16))
    la = _dot_nt(w_r, h1)
    lb = _dot_nt(w_r[0:N_EXPERTS], h2)
    logits = la[0:N_EXPERTS] + la[N_EXPERTS:2 * N_EXPERTS] + lb + b_r

    eidx = _iota((N_EXPERTS, tm), 0)
    vals, idxs, onehots = [], [], []
    l = logits
    for _ in range(TOP_K):
        m = jnp.max(l, axis=0, keepdims=True)
        sel = jnp.min(jnp.where(l == m, eidx, N_EXPERTS), axis=0, keepdims=True)
        oh = eidx == sel
        l = jnp.where(oh, NEG_INF, l)
        vals.append(m)
        idxs.append(sel)
        onehots.append(oh)
    es = [jnp.exp(v - vals[0]) for v in vals]
    denom = es[0] + es[1] + es[2] + es[3]
    gates = [e / denom for e in es]

    ohf = jnp.concatenate([oh.astype(F32) for oh in onehots], axis=0)
    upper = (_iota((tm, tm), 0) < _iota((tm, tm), 1)).astype(BF16)
    prefix = _dot(ohf.astype(BF16), upper)
    ranks = []
    for k in range(TOP_K):
        sl = slice(k * N_EXPERTS, (k + 1) * N_EXPERTS)
        ohk = ohf[sl]
        base_t = jnp.concatenate([base] * (tm // LANES), axis=1)
        ranks.append(jnp.sum(ohk * (prefix[sl] + base_t), axis=0, keepdims=True))
        base = base + jnp.sum(ohk, axis=1, keepdims=True)
    zi = jnp.zeros((8 - TOP_K, tm), jnp.int32)
    zf = jnp.zeros((8 - TOP_K, tm), F32)
    topi = jnp.concatenate(idxs + [zi], axis=0)
    gate8 = jnp.concatenate(gates + [zf], axis=0)
    rank8 = jnp.concatenate([r.astype(jnp.int32) for r in ranks] + [zi], axis=0)
    return x_mid, hp, topi, gate8, rank8, base


def _gla_chunks(p, row0, s_bd, n_lead_pad):
    tm = p["gq"].shape[0]
    nch = tm // GLA_CHUNK
    log_a = p["log_a"]
    if n_lead_pad:
        rows = row0 + _iota((tm, GLA_QK), 0)
        log_a = jnp.where(rows >= n_lead_pad, log_a, 0.0)
    ri, ci = _iota((tm, tm), 0), _iota((tm, tm), 1)
    tril = ((ri >= ci) & (ri // GLA_CHUNK == ci // GLA_CHUNK)).astype(BF16)
    hi, mid, lo = _split3(log_a)
    b_all = _dot(tril, hi) + _dot(tril, mid) + _dot(tril, lo)

    c64 = GLA_CHUNK
    kk_mask = (_iota((GLA_QK, GLA_QK), 0) // c64) == (_iota((GLA_QK, GLA_QK), 1) // GLA_DK)
    vv_mask = (_iota((GLA_QK, GLA_V), 0) // c64) == (_iota((GLA_QK, GLA_V), 1) // GLA_DV)
    ss_mask = (_iota((GLA_QK, GLA_V), 0) // GLA_DK) == (_iota((GLA_QK, GLA_V), 1) // GLA_DV)
    causal = (_iota((c64, GLA_QK), 0) >= (_iota((c64, GLA_QK), 1) % c64)).astype(F32)
    zpad_k = jnp.zeros((LANES - c64, GLA_QK), F32)
    zpad_v = jnp.zeros((LANES - c64, GLA_V), BF16)

    outs = []
    for c in range(nch):
        rs = slice(c * c64, (c + 1) * c64)
        b = b_all[rs]
        q, k, v = p["gq"][rs], p["gk"][rs], p["gv"][rs]
        b_last = b[c64 - 1:c64]
        qt = (q * jnp.exp(b)).astype(BF16)
        kt = k * jnp.exp(-b)
        kd = k * jnp.exp(b_last - b)
        vb = v.astype(BF16)
        k_bd = jnp.where(kk_mask, jnp.concatenate([kt] * GLA_HEADS, axis=0), 0.0).astype(BF16)
        a = (_dot_nt(qt, k_bd) * causal).astype(BF16)
        v_bd = jnp.where(vv_mask, jnp.concatenate([vb] * GLA_HEADS, axis=0), jnp.zeros((), BF16))
        outs.append(_dot(qt, s_bd.astype(BF16)) + _dot(a, v_bd))
        kd_t = jnp.transpose(jnp.concatenate([kd, zpad_k], axis=0)).astype(BF16)
        upd = _dot(kd_t, jnp.concatenate([vb, zpad_v], axis=0))
        decay = jnp.exp(jnp.transpose(jnp.broadcast_to(b_last, (LANES, GLA_QK))))
        s_bd = s_bd * jnp.concatenate([decay] * GLA_HEADS, axis=1) + jnp.where(ss_mask, upd, 0.0)
    return jnp.concatenate(outs, axis=0), s_bd


def _swa_block(sq, kcat, vcat, bias_ref, sinks_ref, valid):
    half = _iota((1, LANES), 1) < SWA_HEAD_DIM
    k_roll = pltpu.roll(kcat, SWA_HEAD_DIM, 1)
    v_roll = pltpu.roll(vcat, SWA_HEAD_DIM, 1)
    cols = []
    for kv in range(SWA_KV_HEADS):
        if kv == 0:
            kk = jnp.where(half, kcat, k_roll)
            v_lo = jnp.where(half, vcat, 0.0)
            v_hi = jnp.where(half, 0.0, v_roll)
        else:
            kk = jnp.where(half, k_roll, kcat)
            v_lo = jnp.where(half, v_roll, 0.0)
            v_hi = jnp.where(half, 0.0, vcat)
        q_parts = []
        for c in (2 * kv, 2 * kv + 1):
            qc = sq[:, c * LANES:(c + 1) * LANES]
            q_parts.append(jnp.where(half, qc, 0.0))
            q_parts.append(jnp.where(half, 0.0, qc))
        q_st = jnp.concatenate(q_parts, axis=0).astype(BF16)
        s = _dot_nt(q_st, kk.astype(BF16)) * (SWA_HEAD_DIM ** -0.5) + bias_ref[kv]
        s = jnp.where(valid, s, NEG_INF)
        sink = jnp.concatenate(
            [jnp.full((WINDOW, 1), sinks_ref[kv * SWA_GROUP + g], F32) for g in range(SWA_GROUP)], axis=0)
        m = jnp.maximum(jnp.max(s, axis=1, keepdims=True), sink)
        pr = jnp.exp(s - m)
        inv = 1.0 / (jnp.sum(pr, axis=1, keepdims=True) + jnp.exp(sink - m))
        pb = pr.astype(BF16)
        p2 = jnp.concatenate([
            jnp.concatenate([pb[0:128], pb[128:256]], axis=1),
            jnp.concatenate([pb[256:384], pb[384:512]], axis=1)], axis=0)
        vv = jnp.concatenate([v_lo, v_hi], axis=0).astype(BF16)
        o2 = _dot(p2, vv)
        cols.append(o2[0:128] * jnp.where(half, inv[0:128], inv[128:256]))
        cols.append(o2[128:256] * jnp.where(half, inv[256:384], inv[384:512]))
    return jnp.concatenate(cols, axis=1)


def _mixer_kernel(sinks_ref, x_ref, s0_ref, k0_ref, v0_ref, base0_ref, bias_ref,
                  g_mix_ref, w_in_ref, w_a_up_ref, b_a_ref, g_gla_ref, g_swa_ref, w_out_ref,
                  g_ffn_ref, w_r_ref, b_r_ref,
                  xmid_ref, hp_ref, topi_ref, gate_ref, rank_ref, sout_ref, kout_ref, vout_ref, cnt_ref,
                  s_scr, k_scr, v_scr, base_scr, *, n_lead_pad, prev_valid_from):
    b_id, j = pl.program_id(0), pl.program_id(1)
    tm = x_ref.shape[1]

    @pl.when(j == 0)
    def _():
        s_scr[...] = s0_ref[...]
        k_scr[...] = k0_ref[...]
        v_scr[...] = v0_ref[...]

    @pl.when((j == 0) & (b_id == 0))
    def _():
        base_scr[...] = base0_ref[...]

    x = x_ref[0]
    p = _project(x, g_mix_ref[...], w_in_ref[...], w_a_up_ref[...], b_a_ref[...])

    o_gla, s_new = _gla_chunks(p, j * tm, s_scr[...], n_lead_pad)
    s_scr[...] = s_new

    qi = _iota((WINDOW, 2 * WINDOW), 0)
    kj = _iota((WINDOW, 2 * WINDOW), 1)
    band = (kj > qi) & (kj <= qi + WINDOW)
    o_parts = []
    for sb in range(tm // WINDOW):
        rs = slice(sb * WINDOW, (sb + 1) * WINDOW)
        k_blk, v_blk = p["sk"][rs], p["sv"][rs]
        k_prev = k_scr[...] if sb == 0 else p["sk"][(sb - 1) * WINDOW:sb * WINDOW]
        v_prev = v_scr[...] if sb == 0 else p["sv"][(sb - 1) * WINDOW:sb * WINDOW]
        valid = band
        if sb == 0 and prev_valid_from:
            first = jnp.where(j == 0, prev_valid_from, 0)
            valid = band & (kj >= first)
        valid = jnp.concatenate([valid] * SWA_GROUP, axis=0)
        o_parts.append(_swa_block(p["sq"][rs], jnp.concatenate([k_prev, k_blk], axis=0),
                                  jnp.concatenate([v_prev, v_blk], axis=0), bias_ref, sinks_ref, valid))
    o_swa = jnp.concatenate(o_parts, axis=0)
    k_scr[...] = p["sk"][tm - WINDOW:tm]
    v_scr[...] = p["sv"][tm - WINDOW:tm]

    x_mid, hp, topi, gate8, rank8, base = _tail(
        x, o_gla, p["gr"], o_swa, g_gla_ref[...], g_swa_ref[...], w_out_ref[...],
        g_ffn_ref[...], w_r_ref[...], b_r_ref[...], base_scr[...])
    base_scr[...] = base
    xmid_ref[0] = x_mid
    hp_ref[...] = hp
    topi_ref[...] = topi
    gate_ref[...] = gate8
    rank_ref[...] = rank8
    sout_ref[0] = s_new
    kout_ref[0] = p["sk"][tm - WINDOW:tm]
    vout_ref[0] = p["sv"][tm - WINDOW:tm]
    cnt_ref[...] = base


def _full_spec(shape):
    nd = len(shape)
    return pl.BlockSpec(shape, lambda *_: (0,) * nd)


def _mixer_call(x, s0, k0, v0, base0, wts, tm, n_lead_pad, prev_valid_from, hp_rows):
    B, L, _ = x.shape
    nj = L // tm
    T = B * L
    weight_args = (wts["bias"], wts["g_mix"], wts["w_in"], wts["w_a_up"], wts["b_a"], wts["g_gla"],
                   wts["g_swa"], wts["w_out"], wts["g_ffn"], wts["w_r"], wts["b_r"])
    in_specs = [
        pl.BlockSpec(memory_space=pltpu.SMEM),
        pl.BlockSpec((1, tm, D_MODEL), lambda b, j: (b, j, 0)),
        _full_spec(s0.shape), _full_spec(k0.shape), _full_spec(v0.shape), _full_spec(base0.shape),
    ] + [_full_spec(w.shape) for w in weight_args]
    tok_spec = pl.BlockSpec((8, tm), lambda b, j: (0, b * nj + j))
    out_specs = [
        pl.BlockSpec((1, tm, D_MODEL), lambda b, j: (b, j, 0)),
        pl.BlockSpec((tm, D_MODEL // 2), lambda b, j: (b * nj + j, 0)),
        tok_spec, tok_spec, tok_spec,
        pl.BlockSpec((1, GLA_QK, GLA_V), lambda b, j: (b, 0, 0)),
        pl.BlockSpec((1, WINDOW, SWA_KV), lambda b, j: (b, 0, 0)),
        pl.BlockSpec((1, WINDOW, SWA_KV), lambda b, j: (b, 0, 0)),
        _full_spec((N_EXPERTS, LANES)),
    ]
    out_shape = [
        jax.ShapeDtypeStruct((B, L, D_MODEL), F32),
        jax.ShapeDtypeStruct((hp_rows, D_MODEL // 2), jnp.uint32),
        jax.ShapeDtypeStruct((8, T), jnp.int32),
        jax.ShapeDtypeStruct((8, T), F32),
        jax.ShapeDtypeStruct((8, T), jnp.int32),
        jax.ShapeDtypeStruct((B, GLA_QK, GLA_V), F32),
        jax.ShapeDtypeStruct((B, WINDOW, SWA_KV), F32),
        jax.ShapeDtypeStruct((B, WINDOW, SWA_KV), F32),
        jax.ShapeDtypeStruct((N_EXPERTS, LANES), F32),
    ]
    kern = functools.partial(_mixer_kernel, n_lead_pad=n_lead_pad, prev_valid_from=prev_valid_from)
    return pl.pallas_call(
        kern,
        grid=(B, nj),
        in_specs=in_specs,
        out_specs=out_specs,
        out_shape=out_shape,
        scratch_shapes=[pltpu.VMEM((GLA_QK, GLA_V), F32), pltpu.VMEM((WINDOW, SWA_KV), F32),
                        pltpu.VMEM((WINDOW, SWA_KV), F32), pltpu.VMEM((N_EXPERTS, LANES), F32)],
        compiler_params=pltpu.CompilerParams(dimension_semantics=("arbitrary", "arbitrary"),
                                             vmem_limit_bytes=VMEM_LIMIT),
        name="mixer",
    )(wts["sinks"], x, s0, k0, v0, base0, *weight_args)


def _decode_kernel(sinks_ref, x_ref, st_ref, ck_ref, cv_ref, base0_ref, bias_ref,
                   g_mix_ref, w_in_ref, w_a_up_ref, b_a_ref, g_gla_ref, g_swa_ref, w_out_ref,
                   g_ffn_ref, w_r_ref, b_r_ref,
                   xmid_ref, hp_ref, topi_ref, gate_ref, rank_ref, sto_ref, cko_ref, cvo_ref, cnt_ref,
                   at_scr, kt_scr, qt_scr, gv_scr, gr_scr, sq_scr, sk_scr, sv_scr, og_scr, os_scr):
    i = pl.program_id(0)
    n_seq = x_ref.shape[0]

    @pl.when(i == 0)
    def _():
        p = _project(x_ref[...], g_mix_ref[...], w_in_ref[...], w_a_up_ref[...], b_a_ref[...])
        at_scr[...] = jnp.transpose(jnp.exp(p["log_a"]))
        kt_scr[...] = jnp.transpose(p["gk"])
        qt_scr[...] = jnp.transpose(p["gq"])
        gv_scr[...] = p["gv"]
        gr_scr[...] = p["gr"]
        sq_scr[...] = p["sq"]
        sk_scr[...] = p["sk"]
        sv_scr[...] = p["sv"]

    lane_seq = _iota((GLA_QK, n_seq), 1)
    half = _iota((1, LANES), 1) < SWA_HEAD_DIM
    row_id = _iota((WINDOW, SWA_KV), 0)
    head_diag = (_iota((16, SWA_Q), 1) // SWA_HEAD_DIM) == _iota((16, SWA_Q), 0)
    sink_col = jnp.concatenate(
        [jnp.full((1, 1), sinks_ref[h], F32) for h in range(SWA_HEADS)] + [jnp.zeros((8, 1), F32)], axis=0)

    def per_seq(sl, carry):
        s = i * DEC_SB + sl
        sel = lane_seq == s
        a_c = jnp.sum(jnp.where(sel, at_scr[...], 0.0), axis=1, keepdims=True)
        k_c = jnp.sum(jnp.where(sel, kt_scr[...], 0.0), axis=1, keepdims=True)
        q_c = jnp.sum(jnp.where(sel, qt_scr[...], 0.0), axis=1, keepdims=True)
        st = st_ref[sl].reshape(GLA_QK, GLA_DV)
        v_row = gv_scr[pl.ds(s, 1), :]
        v_b = jnp.concatenate(
            [jnp.broadcast_to(v_row[:, h * GLA_DV:(h + 1) * GLA_DV], (GLA_DK, GLA_DV))
             for h in range(GLA_HEADS)], axis=0)
        st_new = a_c * st + k_c * v_b
        sto_ref[sl] = st_new.reshape(GLA_HEADS, GLA_DK, GLA_DV)
        t = q_c * st_new
        og_scr[pl.ds(s, 1), :] = jnp.concatenate(
            [jnp.sum(t[h * GLA_DK:(h + 1) * GLA_DK], axis=0, keepdims=True) for h in range(GLA_HEADS)],
            axis=1)

        k_new = sk_scr[pl.ds(s, 1), :]
        v_new = sv_scr[pl.ds(s, 1), :]
        kn = jnp.where(row_id == WINDOW - 1, k_new, pltpu.roll(ck_ref[sl], WINDOW - 1, 0))
        vn = jnp.where(row_id == WINDOW - 1, v_new, pltpu.roll(cv_ref[sl], WINDOW - 1, 0))
        cko_ref[sl] = kn
        cvo_ref[sl] = vn
        kr, vr = pltpu.roll(kn, SWA_HEAD_DIM, 1), pltpu.roll(vn, SWA_HEAD_DIM, 1)
        k0, k1 = jnp.where(half, kn, kr), jnp.where(half, kr, kn)
        v0, v1 = jnp.where(half, vn, vr), jnp.where(half, vr, vn)
        kw = jnp.concatenate([k0, k0, k1, k1], axis=1).astype(BF16)
        vw = jnp.concatenate([v0, v0, v1, v1], axis=1).astype(BF16)
        q_row = sq_scr[pl.ds(s, 1), :]
        qm = jnp.where(head_diag, jnp.broadcast_to(q_row, (16, SWA_Q)), 0.0).astype(BF16)
        sc = _dot_nt(qm, kw) * (SWA_HEAD_DIM ** -0.5) + bias_ref[...]
        m = jnp.maximum(jnp.max(sc, axis=1, keepdims=True), sink_col)
        pr = jnp.exp(sc - m)
        inv = 1.0 / (jnp.sum(pr, axis=1, keepdims=True) + jnp.exp(sink_col - m))
        ow = _dot(pr.astype(BF16), vw) * inv
        os_scr[pl.ds(s, 1), :] = jnp.sum(jnp.where(head_diag, ow, 0.0), axis=0, keepdims=True)
        return carry

    lax.fori_loop(0, DEC_SB, per_seq, 0)

    @pl.when(i == pl.num_programs(0) - 1)
    def _():
        x_mid, hp, topi, gate8, rank8, base = _tail(
            x_ref[...], og_scr[...], gr_scr[...], os_scr[...], g_gla_ref[...], g_swa_ref[...],
            w_out_ref[...], g_ffn_ref[...], w_r_ref[...], b_r_ref[...], base0_ref[...])
        xmid_ref[...] = x_mid
        hp_ref[...] = hp
        topi_ref[...] = topi
        gate_ref[...] = gate8
        rank_ref[...] = rank8
        cnt_ref[...] = base


def _decode_call(xs, state, ck, cv, base0, bias_dec, wts):
    n_seq = xs.shape[0]
    nb = n_seq // DEC_SB
    weight_args = (wts["g_mix"], wts["w_in"], wts["w_a_up"], wts["b_a"], wts["g_gla"],
                   wts["g_swa"], wts["w_out"], wts["g_ffn"], wts["w_r"], wts["b_r"])
    in_specs = [
        pl.BlockSpec(memory_space=pltpu.SMEM),
        _full_spec(xs.shape),
        pl.BlockSpec((DEC_SB, GLA_HEADS, GLA_DK, GLA_DV), lambda i: (i, 0, 0, 0)),
        pl.BlockSpec((DEC_SB, WINDOW, SWA_KV), lambda i: (i, 0, 0)),
        pl.BlockSpec((DEC_SB, WINDOW, SWA_KV), lambda i: (i, 0, 0)),
        _full_spec(base0.shape), _full_spec(bias_dec.shape),
    ] + [_full_spec(w.shape) for w in weight_args]
    out_specs = [
        _full_spec((n_seq, D_MODEL)),
        _full_spec((n_seq, D_MODEL // 2)),
        _full_spec((8, n_seq)), _full_spec((8, n_seq)), _full_spec((8, n_seq)),
        pl.BlockSpec((DEC_SB, GLA_HEADS, GLA_DK, GLA_DV), lambda i: (i, 0, 0, 0)),
        pl.BlockSpec((DEC_SB, WINDOW, SWA_KV), lambda i: (i, 0, 0)),
        pl.BlockSpec((DEC_SB, WINDOW, SWA_KV), lambda i: (i, 0, 0)),
        _full_spec((N_EXPERTS, LANES)),
    ]
    out_shape = [
        jax.ShapeDtypeStruct((n_seq, D_MODEL), F32),
        jax.ShapeDtypeStruct((n_seq, D_MODEL // 2), jnp.uint32),
        jax.ShapeDtypeStruct((8, n_seq), jnp.int32),
        jax.ShapeDtypeStruct((8, n_seq), F32),
        jax.ShapeDtypeStruct((8, n_seq), jnp.int32),
        jax.ShapeDtypeStruct(state.shape, F32),
        jax.ShapeDtypeStruct(ck.shape, F32),
        jax.ShapeDtypeStruct(cv.shape, F32),
        jax.ShapeDtypeStruct((N_EXPERTS, LANES), F32),
    ]
    scratch = [pltpu.VMEM((GLA_QK, n_seq), F32)] * 3 + [
        pltpu.VMEM((n_seq, GLA_V), F32), pltpu.VMEM((n_seq, GLA_V), F32), pltpu.VMEM((n_seq, SWA_Q), F32),
        pltpu.VMEM((n_seq, SWA_KV), F32), pltpu.VMEM((n_seq, SWA_KV), F32),
        pltpu.VMEM((n_seq, GLA_V), F32), pltpu.VMEM((n_seq, SWA_Q), F32)]
    return pl.pallas_call(
        _decode_kernel,
        grid=(nb,),
        in_specs=in_specs,
        out_specs=out_specs,
        out_shape=out_shape,
        scratch_shapes=scratch,
        compiler_params=pltpu.CompilerParams(dimension_semantics=("arbitrary",),
                                             vmem_limit_bytes=VMEM_LIMIT),
        name="decode",
    )(wts["sinks"], xs, state, ck, cv, base0, bias_dec, *weight_args)


FF_TILE = 256


def _expert_kernel(blk_e_ref, nused_ref, tok0_ref, tokn_ref, retp_ref, hp_hbm, wu_ref, bu_ref, wd_ref, bd_ref,
                   y_hbm, xbuf0, xbuf1, ybuf0, ybuf1, xbf, actbf, zbuf, wu_bf, wd_bf, gsem, ssem, zsem):
    i = pl.program_id(0)
    tm = MOE_TM
    nused = nused_ref[0]
    n_tiles = D_FF // FF_TILE
    per = tm // (2 * n_tiles)
    xbufs, ybufs = (xbuf0, xbuf1), (ybuf0, ybuf1)

    def gather(tok_ref, r, s):
        return pltpu.make_async_copy(hp_hbm.at[pl.ds(tok_ref[0, 0, r], 1)], xbufs[s].at[pl.ds(r, 1)], gsem.at[s])

    def scatter(r, s):
        return pltpu.make_async_copy(ybufs[s].at[pl.ds(r, 1)], y_hbm.at[pl.ds(retp_ref[0, 0, r], 1)], ssem.at[s])

    def wait_gather(s):
        pltpu.make_async_copy(hp_hbm.at[pl.ds(0, tm)], xbufs[s], gsem.at[s]).wait()

    def wait_scatter(s):
        pltpu.make_async_copy(ybufs[s], y_hbm.at[pl.ds(0, tm)], ssem.at[s]).wait()

    @pl.when(i == 0)
    def _():
        ybuf1[...] = jnp.zeros_like(ybuf1)
        zbuf[...] = jnp.zeros_like(zbuf)

        def issue(r, c):
            gather(tok0_ref, r, 0).start()
            return c
        lax.fori_loop(0, tm, issue, 0)

    def compute(s):
        o = 1 - s
        wait_gather(s)

        @pl.when(i >= 1)
        def _():
            wait_scatter(s)

        @pl.when((i == 0) | (blk_e_ref[i] != blk_e_ref[jnp.maximum(i - 1, 0)]))
        def _():
            wu_bf[...] = wu_ref[0].astype(BF16)
            wd_bf[...] = wd_ref[0].astype(BF16)

        w = xbufs[s][...]
        half_d = D_MODEL // 2
        xbf[:, :half_d] = lax.bitcast_convert_type(lax.shift_left(w, jnp.uint32(16)), F32).astype(BF16)
        xbf[:, half_d:] = lax.bitcast_convert_type(w & jnp.uint32(0xFFFF0000), F32).astype(BF16)
        def issue_rows(stage):
            for r in range(stage * per, (stage + 1) * per):
                gather(tokn_ref, r, o).start()
                scatter(r, o).start()

        for n in range(n_tiles):
            issue_rows(n)
            gc = slice(n * FF_TILE, (n + 1) * FF_TILE)
            lc = slice(D_FF + n * FF_TILE, D_FF + (n + 1) * FF_TILE)
            g = jnp.minimum(_dot(xbf[...], wu_bf[:, gc]) + bu_ref[0, :, gc], SWIGLU_LIMIT)
            lin = jnp.clip(_dot(xbf[...], wu_bf[:, lc]) + bu_ref[0, :, lc], -SWIGLU_LIMIT, SWIGLU_LIMIT)
            actbf[:, gc] = (g * jax.nn.sigmoid(SWIGLU_ALPHA * g) * (lin + 1.0)).astype(BF16)
        for n in range(n_tiles):
            issue_rows(n_tiles + n)
            yc = slice(n * FF_TILE, (n + 1) * FF_TILE)
            ybufs[s][:, yc] = _dot(actbf[...], wd_bf[:, yc]) + bd_ref[0, :, yc]

    def drain(s):
        o = 1 - s
        wait_gather(s)
        wait_scatter(s)

        def issue(r, c):
            scatter(r, o).start()
            return c
        lax.fori_loop(0, tm, issue, 0)
        wait_scatter(o)

    for s in (0, 1):
        @pl.when((i < nused) & (i % 2 == s))
        def _():
            compute(s)

        @pl.when((i == nused) & (i % 2 == s))
        def _():
            drain(s)

    @pl.when(i >= nused)
    def _():
        zc = pltpu.make_async_copy(zbuf, y_hbm.at[pl.ds(tm + i * tm, tm)], zsem)
        zc.start()
        zc.wait()


def _expert_call(blk_e, nused, tok_sorted, ret_sorted, hp_all, w_up, b_up, w_down, b_down, n_slots):
    n_blocks = blk_e.shape[0]
    tm = MOE_TM
    n_rows = n_blocks * tm + tm
    ret_tab = jnp.concatenate([n_slots + jnp.arange(tm, dtype=jnp.int32), ret_sorted]).reshape(n_blocks + 1, 1, tm)
    tok_tab = tok_sorted.reshape(n_blocks, 1, tm)
    smem_blk = functools.partial(pl.BlockSpec, (1, 1, tm), memory_space=pltpu.SMEM)
    grid_spec = pltpu.PrefetchScalarGridSpec(
        num_scalar_prefetch=2,
        grid=(n_blocks,),
        in_specs=[
            smem_blk(lambda i, be, nu: (0, 0, 0)),
            smem_blk(lambda i, be, nu: (jnp.minimum(i + 1, n_blocks - 1), 0, 0)),
            smem_blk(lambda i, be, nu: (i, 0, 0)),
            pl.BlockSpec(memory_space=pl.ANY),
            pl.BlockSpec((1, D_MODEL, 2 * D_FF), lambda i, be, nu: (be[i], 0, 0)),
            pl.BlockSpec((1, 1, 2 * D_FF), lambda i, be, nu: (be[i], 0, 0)),
            pl.BlockSpec((1, D_FF, D_MODEL), lambda i, be, nu: (be[i], 0, 0)),
            pl.BlockSpec((1, 1, D_MODEL), lambda i, be, nu: (be[i], 0, 0)),
        ],
        out_specs=pl.BlockSpec(memory_space=pl.ANY),
        scratch_shapes=[pltpu.VMEM((tm, D_MODEL // 2), jnp.uint32), pltpu.VMEM((tm, D_MODEL // 2), jnp.uint32),
                        pltpu.VMEM((tm, D_MODEL), F32), pltpu.VMEM((tm, D_MODEL), F32),
                        pltpu.VMEM((tm, D_MODEL), BF16), pltpu.VMEM((tm, D_FF), BF16),
                        pltpu.VMEM((tm, D_MODEL), F32),
                        pltpu.VMEM((D_MODEL, 2 * D_FF), BF16), pltpu.VMEM((D_FF, D_MODEL), BF16),
                        pltpu.SemaphoreType.DMA((2,)), pltpu.SemaphoreType.DMA((2,)), pltpu.SemaphoreType.DMA],
    )
    return pl.pallas_call(
        _expert_kernel,
        grid_spec=grid_spec,
        out_shape=jax.ShapeDtypeStruct((n_rows, D_MODEL), F32),
        compiler_params=pltpu.CompilerParams(dimension_semantics=("arbitrary",),
                                             vmem_limit_bytes=VMEM_LIMIT),
        name="experts",
    )(blk_e, nused, tok_tab, tok_tab, ret_tab, hp_all,
      w_up, b_up.reshape(N_EXPERTS, 1, 2 * D_FF), w_down, b_down.reshape(N_EXPERTS, 1, D_MODEL))


def _combine_kernel(ys_ref, xmid_ref, gate_ref, g_final_ref, y_ref):
    acc = xmid_ref[...]
    gts = gate_ref[...]
    for k in range(TOP_K):
        acc = acc + ys_ref[:, k * D_MODEL:(k + 1) * D_MODEL] * gts[:, k:k + 1]
    y_ref[...] = _rms(acc, g_final_ref[...])


def _combine_call(ys4, row0, x_mid, gates, g_final, tm):
    T = x_mid.shape[0]
    blk0 = row0 // tm
    return pl.pallas_call(
        _combine_kernel,
        grid=(T // tm,),
        in_specs=[
            pl.BlockSpec((tm, TOP_K * D_MODEL), lambda i: (blk0 + i, 0)),
            pl.BlockSpec((tm, D_MODEL), lambda i: (i, 0)),
            pl.BlockSpec((tm, TOP_K), lambda i: (i, 0)),
            _full_spec((1, D_MODEL)),
        ],
        out_specs=pl.BlockSpec((tm, D_MODEL), lambda i: (i, 0)),
        out_shape=jax.ShapeDtypeStruct((T, D_MODEL), F32),
        compiler_params=pltpu.CompilerParams(dimension_semantics=("arbitrary",),
                                             vmem_limit_bytes=VMEM_LIMIT),
        name="combine",
    )(ys4, x_mid, gates, g_final)


def _t5_bucket(dist):
    n = jnp.maximum(dist, 0)
    max_exact = NUM_BUCKETS // 2
    nf = jnp.maximum(n, 1).astype(F32)
    large = max_exact + (jnp.log(nf / max_exact) / math.log(MAX_DISTANCE / max_exact)
                         * (NUM_BUCKETS - max_exact)).astype(jnp.int32)
    large = jnp.minimum(large, NUM_BUCKETS - 1)
    return jnp.where(n < max_exact, n, large)


def kernel(x_prompt, x_sample, state_gla, cache_swa_k, cache_swa_v, meta_tokens, rel_bias_table,
           g_mix, w_in, w_a_up, b_a, g_gla_out, g_swa_out, attn_sinks, w_out,
           g_ffn, w_router, b_router, w_up, b_up, w_down, b_down, g_final):
    assert g_mix.shape[0] == 1, "single-layer trunk"
    B, L, _ = x_prompt.shape
    n_seq = x_sample.shape[0]
    TP = B * L
    T_all = TP + n_seq

    wi = w_in[0]
    sizes = (GLA_QK, GLA_QK, GLA_V, GLA_V, GLA_LOWRANK, SWA_Q, SWA_KV, SWA_KV)
    offs = [0]
    for s in sizes:
        offs.append(offs[-1] + s)
    seg = [wi[:, offs[n]:offs[n + 1]] for n in range(8)]
    w_in_r = jnp.concatenate(
        seg[0:4] + seg[5:8] + [seg[4], jnp.zeros((D_MODEL, LANES - GLA_LOWRANK), F32)], axis=1).astype(BF16)
    w_a_pad = jnp.concatenate([w_a_up[0], jnp.zeros((LANES - GLA_LOWRANK, GLA_QK), F32)], axis=0).astype(BF16)
    wr_t = jnp.transpose(w_router[0])
    wr_hi = wr_t.astype(BF16)
    wr_lo = (wr_t - wr_hi.astype(F32)).astype(BF16)
    qi = jnp.arange(WINDOW)[:, None]
    kj = jnp.arange(2 * WINDOW)[None, :]
    buckets = jnp.arange(NUM_BUCKETS)
    table = rel_bias_table.astype(F32)
    oh_p = (_t5_bucket(qi - kj + WINDOW)[..., None] == buckets).astype(F32)
    bias_p = jnp.einsum("qkb,bh->hqk", oh_p, table, precision=lax.Precision.HIGHEST)
    bias_p = bias_p.reshape(SWA_KV_HEADS, SWA_GROUP * WINDOW, 2 * WINDOW)
    oh_d = (_t5_bucket(WINDOW - 1 - jnp.arange(WINDOW))[:, None] == buckets).astype(F32)
    bias_d = jnp.einsum("rb,bh->hr", oh_d, table, precision=lax.Precision.HIGHEST)
    bias_d = jnp.concatenate([bias_d, jnp.zeros((8, WINDOW), F32)], axis=0)
    wts = dict(
        sinks=attn_sinks[0].astype(F32), bias=bias_p,
        g_mix=g_mix[0][None], w_in=w_in_r, w_a_up=w_a_pad, b_a=b_a[0][None],
        g_gla=g_gla_out[0][None], g_swa=g_swa_out[0][None], w_out=w_out[0].astype(BF16),
        g_ffn=g_ffn[0][None], w_r=jnp.concatenate([wr_hi, wr_lo], axis=0), b_r=b_router[0][:, None],
    )

    x_pre = jnp.concatenate([jnp.zeros((WINDOW - N_META, D_MODEL), F32), meta_tokens.astype(F32)], axis=0)[None]
    zeros_s = jnp.zeros((GLA_QK, GLA_V), F32)
    zeros_kv = jnp.zeros((WINDOW, SWA_KV), F32)
    zeros_b = jnp.zeros((N_EXPERTS, LANES), F32)
    pre = _mixer_call(x_pre, zeros_s, zeros_kv, zeros_kv, zeros_b, wts, WINDOW, WINDOW - N_META, 0, WINDOW)
    s_meta, k_meta, v_meta = pre[5][0], pre[6][0], pre[7][0]

    (xmid_p, hp_p, topi_p, gate_p, rank_p, s_p, k_p, v_p, cnt_p) = _mixer_call(
        x_prompt, s_meta, k_meta, v_meta, zeros_b, wts, MIX_TM, 0, WINDOW - N_META, TP)

    (xmid_s, hp_s, topi_s, gate_s, rank_s, st_s, ck_s, cv_s, cnt_all) = _decode_call(
        x_sample[:, 0], state_gla[0], cache_swa_k[0].reshape(n_seq, WINDOW, SWA_KV),
        cache_swa_v[0].reshape(n_seq, WINDOW, SWA_KV), cnt_p, bias_d, wts)

    tm = MOE_TM
    n_slots = T_all * TOP_K
    n_blocks = -(-n_slots // tm) + N_EXPERTS
    top_e = jnp.concatenate([topi_p[:TOP_K], topi_s[:TOP_K]], axis=1)
    rank = jnp.concatenate([rank_p[:TOP_K], rank_s[:TOP_K]], axis=1)
    counts = cnt_all[:, 0].astype(jnp.int32)
    padded = (counts + tm - 1) // tm * tm
    pad_end = jnp.cumsum(padded)
    pad_start = pad_end - padded
    e_ids = jnp.arange(N_EXPERTS, dtype=jnp.int32)
    dest = jnp.sum(jnp.where(top_e[..., None] == e_ids, pad_start, 0), axis=-1) + rank
    slot_id = jnp.arange(T_all, dtype=jnp.int32)[None] * TOP_K + jnp.arange(TOP_K, dtype=jnp.int32)[:, None]
    n_pad = n_blocks * tm
    ret_real = jnp.full((n_pad,), -1, jnp.int32).at[dest.reshape(-1)].set(slot_id.reshape(-1))
    pos = jnp.arange(n_pad, dtype=jnp.int32)
    real_before = jnp.sum(jnp.where(pad_start[None] <= pos[:, None], counts[None], 0), axis=1)
    is_real = ret_real >= 0
    ret_sorted = jnp.where(is_real, ret_real, n_slots + tm + pos - real_before)
    tok_sorted = jnp.where(is_real, ret_real // TOP_K, 0)
    blk_e = jnp.minimum(jnp.sum(pad_end[None] <= (jnp.arange(n_blocks, dtype=jnp.int32) * tm)[:, None], axis=1),
                        N_EXPERTS - 1).astype(jnp.int32)
    nused = (pad_end[-1] // tm).astype(jnp.int32).reshape(1)

    hp_all = jnp.concatenate([hp_p, hp_s], axis=0)
    ys = _expert_call(blk_e, nused, tok_sorted, ret_sorted, hp_all, w_up[0], b_up[0], w_down[0], b_down[0],
                      n_slots)
    ys4 = ys.reshape(-1, TOP_K * D_MODEL)

    gates = jnp.transpose(jnp.concatenate([gate_p[:TOP_K], gate_s[:TOP_K]], axis=1))
    gf = g_final[None]
    y_p = _combine_call(ys4, 0, xmid_p.reshape(TP, D_MODEL), gates[:TP], gf, MIX_TM)
    y_s = _combine_call(ys4, TP, xmid_s, gates[TP:], gf, n_seq)

    s_heads = jnp.stack([s_p[:, h * GLA_DK:(h + 1) * GLA_DK, h * GLA_DV:(h + 1) * GLA_DV]
                         for h in range(GLA_HEADS)], axis=1)
    return (y_p.reshape(B, L, D_MODEL), y_s.reshape(n_seq, 1, D_MODEL), s_heads[None],
            k_p.reshape(1, B, WINDOW, SWA_KV_HEADS, SWA_HEAD_DIM),
            v_p.reshape(1, B, WINDOW, SWA_KV_HEADS, SWA_HEAD_DIM),
            st_s[None], ck_s.reshape(1, n_seq, WINDOW, SWA_KV_HEADS, SWA_HEAD_DIM),
            cv_s.reshape(1, n_seq, WINDOW, SWA_KV_HEADS, SWA_HEAD_DIM))
```

```python
import functools
import math

import jax
import jax.numpy as jnp
from jax import lax
from jax.experimental import pallas as pl
from jax.experimental.pallas import tpu as pltpu

D_MODEL = 1024
N_META = 16
GLA_HEADS = 4
GLA_DK = 64
GLA_DV = 128
GLA_LOWRANK = 16
GLA_GATE_TAU = 16.0
GLA_CHUNK = 64
SWA_HEADS = 8
SWA_KV_HEADS = 2
SWA_HEAD_DIM = 64
SWA_GROUP = SWA_HEADS // SWA_KV_HEADS
WINDOW = 128
NUM_BUCKETS = 32
MAX_DISTANCE = 128
N_EXPERTS = 32
TOP_K = 4
D_FF = 1024
SWIGLU_ALPHA = 1.702
SWIGLU_LIMIT = 7.0
RMS_EPS = 1e-6

GLA_QK = GLA_HEADS * GLA_DK
GLA_V = GLA_HEADS * GLA_DV
SWA_Q = SWA_HEADS * SWA_HEAD_DIM
SWA_KV = SWA_KV_HEADS * SWA_HEAD_DIM
LANES = 128
C_GQ, C_GK, C_GV, C_GR = 0, GLA_QK, 2 * GLA_QK, 2 * GLA_QK + GLA_V
C_SQ = C_GR + GLA_V
C_SK = C_SQ + SWA_Q
C_SV = C_SK + SWA_KV
C_GA = C_SV + SWA_KV
D_PROJ = C_GA + LANES

MIX_TM = 256
MOE_TM = 256
DEC_SB = 16
VMEM_LIMIT = 56 * 1024 * 1024

F32 = jnp.float32
BF16 = jnp.bfloat16
NEG_INF = float("-inf")


def _dot(a, b):
    return jnp.dot(a, b, preferred_element_type=F32)


def _dot_nt(a, b):
    return lax.dot_general(a, b, (((1,), (1,)), ((), ())), preferred_element_type=F32)


def _split3(x):
    hi = x.astype(BF16)
    r1 = x - hi.astype(F32)
    mid = r1.astype(BF16)
    lo = (r1 - mid.astype(F32)).astype(BF16)
    return hi, mid, lo


def _rms(x, g):
    return x * lax.rsqrt(jnp.mean(x * x, axis=-1, keepdims=True) + RMS_EPS) * g


def _iota(shape, dim):
    return lax.broadcasted_iota(jnp.int32, shape, dim)


SLAB = D_MODEL // LANES


def _store_slabs(ref, x):
    rows = x.shape[0]
    for c in range(SLAB):
        ref[pl.ds(c, rows, stride=SLAB), :] = x[:, c * LANES:(c + 1) * LANES]


def _load_slab_chunk(ref, rows, first, stride):
    return ref[pl.ds(first, rows, stride=stride), :]


def _project(x, g_mix, w_in, w_a_up, b_a):
    h = _rms(x, g_mix).astype(BF16)
    proj = _dot(h, w_in)
    ga = proj[:, C_GA:C_GA + LANES].astype(BF16)
    z = _dot(ga, w_a_up) + b_a
    log_a = -(jnp.maximum(-z, 0.0) + jnp.log1p(jnp.exp(-jnp.abs(z)))) / GLA_GATE_TAU
    return dict(
        gq=proj[:, C_GQ:C_GQ + GLA_QK] * (GLA_DK ** -0.5),
        gk=proj[:, C_GK:C_GK + GLA_QK],
        gv=proj[:, C_GV:C_GV + GLA_V],
        gr=proj[:, C_GR:C_GR + GLA_V],
        sq=proj[:, C_SQ:C_SQ + SWA_Q],
        sk=proj[:, C_SK:C_SK + SWA_KV],
        sv=proj[:, C_SV:C_SV + SWA_KV],
        log_a=log_a,
    )


def _tail(x, o_gla, gr, o_swa, g_gla_out, g_swa_out, w_out, g_ffn, w_r, b_r, base):
    tm = x.shape[0]
    gate = gr * jax.nn.sigmoid(gr)
    parts = []
    for h in range(GLA_HEADS):
        sl = slice(h * GLA_DV, (h + 1) * GLA_DV)
        parts.append(_rms(o_gla[:, sl], g_gla_out) * gate[:, sl])
    parts.append(_rms(o_swa, g_swa_out))
    o = jnp.concatenate(parts, axis=1).astype(BF16)
    x_mid = x + _dot(o, w_out)
    hp = _rms(x_mid, g_ffn)

    h1 = hp.astype(BF16)
    h2 = (hp - h1.astype(F32)).astype(BF16)
    la = _dot_nt(w_r, h1)
    lb = _dot_nt(w_r[0:N_EXPERTS], h2)
    logits = la[0:N_EXPERTS] + la[N_EXPERTS:2 * N_EXPERTS] + lb + b_r

    eidx = _iota((N_EXPERTS, tm), 0)
    vals, idxs, onehots = [], [], []
    l = logits
    for _ in range(TOP_K):
        m = jnp.max(l, axis=0, keepdims=True)
        sel = jnp.min(jnp.where(l == m, eidx, N_EXPERTS), axis=0, keepdims=True)
        oh = eidx == sel
        l = jnp.where(oh, NEG_INF, l)
        vals.append(m)
        idxs.append(sel)
        onehots.append(oh)
    es = [jnp.exp(v - vals[0]) for v in vals]
    denom = es[0] + es[1] + es[2] + es[3]
    gates = [e / denom for e in es]

    ohf = jnp.concatenate([oh.astype(F32) for oh in onehots], axis=0)
    upper = (_iota((tm, tm), 0) < _iota((tm, tm), 1)).astype(BF16)
    prefix = _dot(ohf.astype(BF16), upper)
    ranks = []
    for k in range(TOP_K):
        sl = slice(k * N_EXPERTS, (k + 1) * N_EXPERTS)
        ohk = ohf[sl]
        base_t = jnp.concatenate([base] * (tm // LANES), axis=1)
        ranks.append(jnp.sum(ohk * (prefix[sl] + base_t), axis=0, keepdims=True))
        base = base + jnp.sum(ohk, axis=1, keepdims=True)
    zi = jnp.zeros((8 - TOP_K, tm), jnp.int32)
    zf = jnp.zeros((8 - TOP_K, tm), F32)
    topi = jnp.concatenate(idxs + [zi], axis=0)
    gate8 = jnp.concatenate(gates + [zf], axis=0)
    rank8 = jnp.concatenate([r.astype(jnp.int32) for r in ranks] + [zi], axis=0)
    return x_mid, hp, topi, gate8, rank8, base


def _gla_chunks(p, row0, s_bd, n_lead_pad):
    tm = p["gq"].shape[0]
    nch = tm // GLA_CHUNK
    log_a = p["log_a"]
    if n_lead_pad:
        rows = row0 + _iota((tm, GLA_QK), 0)
        log_a = jnp.where(rows >= n_lead_pad, log_a, 0.0)
    ri, ci = _iota((tm, tm), 0), _iota((tm, tm), 1)
    tril = ((ri >= ci) & (ri // GLA_CHUNK == ci // GLA_CHUNK)).astype(BF16)
    hi, mid, lo = _split3(log_a)
    b_all = _dot(tril, hi) + _dot(tril, mid) + _dot(tril, lo)

    c64 = GLA_CHUNK
    kk_mask = (_iota((GLA_QK, GLA_QK), 0) // c64) == (_iota((GLA_QK, GLA_QK), 1) // GLA_DK)
    vv_mask = (_iota((GLA_QK, GLA_V), 0) // c64) == (_iota((GLA_QK, GLA_V), 1) // GLA_DV)
    ss_mask = (_iota((GLA_QK, GLA_V), 0) // GLA_DK) == (_iota((GLA_QK, GLA_V), 1) // GLA_DV)
    causal = (_iota((c64, GLA_QK), 0) >= (_iota((c64, GLA_QK), 1) % c64)).astype(F32)
    zpad_k = jnp.zeros((LANES - c64, GLA_QK), F32)
    zpad_v = jnp.zeros((LANES - c64, GLA_V), BF16)

    outs = []
    for c in range(nch):
        rs = slice(c * c64, (c + 1) * c64)
        b = b_all[rs]
        q, k, v = p["gq"][rs], p["gk"][rs], p["gv"][rs]
        b_last = b[c64 - 1:c64]
        qt = (q * jnp.exp(b)).astype(BF16)
        kt = k * jnp.exp(-b)
        kd = k * jnp.exp(b_last - b)
        vb = v.astype(BF16)
        k_bd = jnp.where(kk_mask, jnp.concatenate([kt] * GLA_HEADS, axis=0), 0.0).astype(BF16)
        a = (_dot_nt(qt, k_bd) * causal).astype(BF16)
        v_bd = jnp.where(vv_mask, jnp.concatenate([vb] * GLA_HEADS, axis=0), jnp.zeros((), BF16))
        outs.append(_dot(qt, s_bd.astype(BF16)) + _dot(a, v_bd))
        kd_t = jnp.transpose(jnp.concatenate([kd, zpad_k], axis=0)).astype(BF16)
        upd = _dot(kd_t, jnp.concatenate([vb, zpad_v], axis=0))
        decay = jnp.exp(jnp.transpose(jnp.broadcast_to(b_last, (LANES, GLA_QK))))
        s_bd = s_bd * jnp.concatenate([decay] * GLA_HEADS, axis=1) + jnp.where(ss_mask, upd, 0.0)
    return jnp.concatenate(outs, axis=0), s_bd


def _swa_block(sq, kcat, vcat, bias_ref, sinks_ref, valid):
    half = _iota((1, LANES), 1) < SWA_HEAD_DIM
    k_roll = pltpu.roll(kcat, SWA_HEAD_DIM, 1)
    v_roll = pltpu.roll(vcat, SWA_HEAD_DIM, 1)
    cols = []
    for kv in range(SWA_KV_HEADS):
        if kv == 0:
            kk = jnp.where(half, kcat, k_roll)
            v_lo = jnp.where(half, vcat, 0.0)
            v_hi = jnp.where(half, 0.0, v_roll)
        else:
            kk = jnp.where(half, k_roll, kcat)
            v_lo = jnp.where(half, v_roll, 0.0)
            v_hi = jnp.where(half, 0.0, vcat)
        q_parts = []
        for c in (2 * kv, 2 * kv + 1):
            qc = sq[:, c * LANES:(c + 1) * LANES]
            q_parts.append(jnp.where(half, qc, 0.0))
            q_parts.append(jnp.where(half, 0.0, qc))
        q_st = jnp.concatenate(q_parts, axis=0).astype(BF16)
        s = _dot_nt(q_st, kk.astype(BF16)) * (SWA_HEAD_DIM ** -0.5) + bias_ref[kv]
        s = jnp.where(valid, s, NEG_INF)
        sink = jnp.concatenate(
            [jnp.full((WINDOW, 1), sinks_ref[kv * SWA_GROUP + g], F32) for g in range(SWA_GROUP)], axis=0)
        m = jnp.maximum(jnp.max(s, axis=1, keepdims=True), sink)
        pr = jnp.exp(s - m)
        inv = 1.0 / (jnp.sum(pr, axis=1, keepdims=True) + jnp.exp(sink - m))
        pb = pr.astype(BF16)
        p2 = jnp.concatenate([
            jnp.concatenate([pb[0:128], pb[128:256]], axis=1),
            jnp.concatenate([pb[256:384], pb[384:512]], axis=1)], axis=0)
        vv = jnp.concatenate([v_lo, v_hi], axis=0).astype(BF16)
        o2 = _dot(p2, vv)
        cols.append(o2[0:128] * jnp.where(half, inv[0:128], inv[128:256]))
        cols.append(o2[128:256] * jnp.where(half, inv[256:384], inv[384:512]))
    return jnp.concatenate(cols, axis=1)


def _mixer_kernel(sinks_ref, x_ref, s0_ref, k0_ref, v0_ref, base0_ref, bias_ref,
                  g_mix_ref, w_in_ref, w_a_up_ref, b_a_ref, g_gla_ref, g_swa_ref, w_out_ref,
                  g_ffn_ref, w_r_ref, b_r_ref,
                  xmid_ref, hp_ref, topi_ref, gate_ref, rank_ref, sout_ref, kout_ref, vout_ref, cnt_ref,
                  s_scr, k_scr, v_scr, base_scr, *, n_lead_pad, prev_valid_from):
    b_id, j = pl.program_id(0), pl.program_id(1)
    tm = x_ref.shape[1]

    @pl.when(j == 0)
    def _():
        s_scr[...] = s0_ref[...]
        k_scr[...] = k0_ref[...]
        v_scr[...] = v0_ref[...]

    @pl.when((j == 0) & (b_id == 0))
    def _():
        base_scr[...] = base0_ref[...]

    x = x_ref[0]
    p = _project(x, g_mix_ref[...], w_in_ref[...], w_a_up_ref[...], b_a_ref[...])

    o_gla, s_new = _gla_chunks(p, j * tm, s_scr[...], n_lead_pad)
    s_scr[...] = s_new

    qi = _iota((WINDOW, 2 * WINDOW), 0)
    kj = _iota((WINDOW, 2 * WINDOW), 1)
    band = (kj > qi) & (kj <= qi + WINDOW)
    o_parts = []
    for sb in range(tm // WINDOW):
        rs = slice(sb * WINDOW, (sb + 1) * WINDOW)
        k_blk, v_blk = p["sk"][rs], p["sv"][rs]
        k_prev = k_scr[...] if sb == 0 else p["sk"][(sb - 1) * WINDOW:sb * WINDOW]
        v_prev = v_scr[...] if sb == 0 else p["sv"][(sb - 1) * WINDOW:sb * WINDOW]
        valid = band
        if sb == 0 and prev_valid_from:
            first = jnp.where(j == 0, prev_valid_from, 0)
            valid = band & (kj >= first)
        valid = jnp.concatenate([valid] * SWA_GROUP, axis=0)
        o_parts.append(_swa_block(p["sq"][rs], jnp.concatenate([k_prev, k_blk], axis=0),
                                  jnp.concatenate([v_prev, v_blk], axis=0), bias_ref, sinks_ref, valid))
    o_swa = jnp.concatenate(o_parts, axis=0)
    k_scr[...] = p["sk"][tm - WINDOW:tm]
    v_scr[...] = p["sv"][tm - WINDOW:tm]

    x_mid, hp, topi, gate8, rank8, base = _tail(
        x, o_gla, p["gr"], o_swa, g_gla_ref[...], g_swa_ref[...], w_out_ref[...],
        g_ffn_ref[...], w_r_ref[...], b_r_ref[...], base_scr[...])
    base_scr[...] = base
    xmid_ref[0] = x_mid
    _store_slabs(hp_ref, hp)
    topi_ref[...] = topi
    gate_ref[...] = gate8
    rank_ref[...] = rank8
    sout_ref[0] = s_new
    kout_ref[0] = p["sk"][tm - WINDOW:tm]
    vout_ref[0] = p["sv"][tm - WINDOW:tm]
    cnt_ref[...] = base


def _full_spec(shape):
    nd = len(shape)
    return pl.BlockSpec(shape, lambda *_: (0,) * nd)


def _mixer_call(x, s0, k0, v0, base0, wts, tm, n_lead_pad, prev_valid_from, hp_rows):
    B, L, _ = x.shape
    nj = L // tm
    T = B * L
    weight_args = (wts["bias"], wts["g_mix"], wts["w_in"], wts["w_a_up"], wts["b_a"], wts["g_gla"],
                   wts["g_swa"], wts["w_out"], wts["g_ffn"], wts["w_r"], wts["b_r"])
    in_specs = [
        pl.BlockSpec(memory_space=pltpu.SMEM),
        pl.BlockSpec((1, tm, D_MODEL), lambda b, j: (b, j, 0)),
        _full_spec(s0.shape), _full_spec(k0.shape), _full_spec(v0.shape), _full_spec(base0.shape),
    ] + [_full_spec(w.shape) for w in weight_args]
    tok_spec = pl.BlockSpec((8, tm), lambda b, j: (0, b * nj + j))
    out_specs = [
        pl.BlockSpec((1, tm, D_MODEL), lambda b, j: (b, j, 0)),
        pl.BlockSpec((tm * SLAB, LANES), lambda b, j: (b * nj + j, 0)),
        tok_spec, tok_spec, tok_spec,
        pl.BlockSpec((1, GLA_QK, GLA_V), lambda b, j: (b, 0, 0)),
        pl.BlockSpec((1, WINDOW, SWA_KV), lambda b, j: (b, 0, 0)),
        pl.BlockSpec((1, WINDOW, SWA_KV), lambda b, j: (b, 0, 0)),
        _full_spec((N_EXPERTS, LANES)),
    ]
    out_shape = [
        jax.ShapeDtypeStruct((B, L, D_MODEL), F32),
        jax.ShapeDtypeStruct((hp_rows * SLAB, LANES), F32),
        jax.ShapeDtypeStruct((8, T), jnp.int32),
        jax.ShapeDtypeStruct((8, T), F32),
        jax.ShapeDtypeStruct((8, T), jnp.int32),
        jax.ShapeDtypeStruct((B, GLA_QK, GLA_V), F32),
        jax.ShapeDtypeStruct((B, WINDOW, SWA_KV), F32),
        jax.ShapeDtypeStruct((B, WINDOW, SWA_KV), F32),
        jax.ShapeDtypeStruct((N_EXPERTS, LANES), F32),
    ]
    kern = functools.partial(_mixer_kernel, n_lead_pad=n_lead_pad, prev_valid_from=prev_valid_from)
    return pl.pallas_call(
        kern,
        grid=(B, nj),
        in_specs=in_specs,
        out_specs=out_specs,
        out_shape=out_shape,
        scratch_shapes=[pltpu.VMEM((GLA_QK, GLA_V), F32), pltpu.VMEM((WINDOW, SWA_KV), F32),
                        pltpu.VMEM((WINDOW, SWA_KV), F32), pltpu.VMEM((N_EXPERTS, LANES), F32)],
        compiler_params=pltpu.CompilerParams(dimension_semantics=("arbitrary", "arbitrary"),
                                             vmem_limit_bytes=VMEM_LIMIT),
        name="mixer",
    )(wts["sinks"], x, s0, k0, v0, base0, *weight_args)


def _decode_kernel(sinks_ref, x_ref, st_ref, ck_ref, cv_ref, base0_ref, bias_ref,
                   g_mix_ref, w_in_ref, w_a_up_ref, b_a_ref, g_gla_ref, g_swa_ref, w_out_ref,
                   g_ffn_ref, w_r_ref, b_r_ref,
                   xmid_ref, hp_ref, topi_ref, gate_ref, rank_ref, sto_ref, cko_ref, cvo_ref, cnt_ref,
                   at_scr, kt_scr, qt_scr, gv_scr, gr_scr, sq_scr, sk_scr, sv_scr, og_scr, os_scr):
    i = pl.program_id(0)
    n_seq = x_ref.shape[0]

    @pl.when(i == 0)
    def _():
        p = _project(x_ref[...], g_mix_ref[...], w_in_ref[...], w_a_up_ref[...], b_a_ref[...])
        at_scr[...] = jnp.transpose(jnp.exp(p["log_a"]))
        kt_scr[...] = jnp.transpose(p["gk"])
        qt_scr[...] = jnp.transpose(p["gq"])
        gv_scr[...] = p["gv"]
        gr_scr[...] = p["gr"]
        sq_scr[...] = p["sq"]
        sk_scr[...] = p["sk"]
        sv_scr[...] = p["sv"]

    lane_seq = _iota((GLA_QK, n_seq), 1)
    half = _iota((1, LANES), 1) < SWA_HEAD_DIM
    row_id = _iota((WINDOW, SWA_KV), 0)
    head_diag = (_iota((16, SWA_Q), 1) // SWA_HEAD_DIM) == _iota((16, SWA_Q), 0)
    sink_col = jnp.concatenate(
        [jnp.full((1, 1), sinks_ref[h], F32) for h in range(SWA_HEADS)] + [jnp.zeros((8, 1), F32)], axis=0)

    def per_seq(sl, carry):
        s = i * DEC_SB + sl
        sel = lane_seq == s
        a_c = jnp.sum(jnp.where(sel, at_scr[...], 0.0), axis=1, keepdims=True)
        k_c = jnp.sum(jnp.where(sel, kt_scr[...], 0.0), axis=1, keepdims=True)
        q_c = jnp.sum(jnp.where(sel, qt_scr[...], 0.0), axis=1, keepdims=True)
        st = st_ref[sl].reshape(GLA_QK, GLA_DV)
        v_row = gv_scr[pl.ds(s, 1), :]
        v_b = jnp.concatenate(
            [jnp.broadcast_to(v_row[:, h * GLA_DV:(h + 1) * GLA_DV], (GLA_DK, GLA_DV))
             for h in range(GLA_HEADS)], axis=0)
        st_new = a_c * st + k_c * v_b
        sto_ref[sl] = st_new.reshape(GLA_HEADS, GLA_DK, GLA_DV)
        t = q_c * st_new
        og_scr[pl.ds(s, 1), :] = jnp.concatenate(
            [jnp.sum(t[h * GLA_DK:(h + 1) * GLA_DK], axis=0, keepdims=True) for h in range(GLA_HEADS)],
            axis=1)

        k_new = sk_scr[pl.ds(s, 1), :]
        v_new = sv_scr[pl.ds(s, 1), :]
        kn = jnp.where(row_id == WINDOW - 1, k_new, pltpu.roll(ck_ref[sl], WINDOW - 1, 0))
        vn = jnp.where(row_id == WINDOW - 1, v_new, pltpu.roll(cv_ref[sl], WINDOW - 1, 0))
        cko_ref[sl] = kn
        cvo_ref[sl] = vn
        kr, vr = pltpu.roll(kn, SWA_HEAD_DIM, 1), pltpu.roll(vn, SWA_HEAD_DIM, 1)
        k0, k1 = jnp.where(half, kn, kr), jnp.where(half, kr, kn)
        v0, v1 = jnp.where(half, vn, vr), jnp.where(half, vr, vn)
        kw = jnp.concatenate([k0, k0, k1, k1], axis=1).astype(BF16)
        vw = jnp.concatenate([v0, v0, v1, v1], axis=1).astype(BF16)
        q_row = sq_scr[pl.ds(s, 1), :]
        qm = jnp.where(head_diag, jnp.broadcast_to(q_row, (16, SWA_Q)), 0.0).astype(BF16)
        sc = _dot_nt(qm, kw) * (SWA_HEAD_DIM ** -0.5) + bias_ref[...]
        m = jnp.maximum(jnp.max(sc, axis=1, keepdims=True), sink_col)
        pr = jnp.exp(sc - m)
        inv = 1.0 / (jnp.sum(pr, axis=1, keepdims=True) + jnp.exp(sink_col - m))
        ow = _dot(pr.astype(BF16), vw) * inv
        os_scr[pl.ds(s, 1), :] = jnp.sum(jnp.where(head_diag, ow, 0.0), axis=0, keepdims=True)
        return carry

    lax.fori_loop(0, DEC_SB, per_seq, 0)

    @pl.when(i == pl.num_programs(0) - 1)
    def _():
        x_mid, hp, topi, gate8, rank8, base = _tail(
            x_ref[...], og_scr[...], gr_scr[...], os_scr[...], g_gla_ref[...], g_swa_ref[...],
            w_out_ref[...], g_ffn_ref[...], w_r_ref[...], b_r_ref[...], base0_ref[...])
        xmid_ref[...] = x_mid
        _store_slabs(hp_ref, hp)
        topi_ref[...] = topi
        gate_ref[...] = gate8
        rank_ref[...] = rank8
        cnt_ref[...] = base


def _decode_call(xs, state, ck, cv, base0, bias_dec, wts):
    n_seq = xs.shape[0]
    nb = n_seq // DEC_SB
    weight_args = (wts["g_mix"], wts["w_in"], wts["w_a_up"], wts["b_a"], wts["g_gla"],
                   wts["g_swa"], wts["w_out"], wts["g_ffn"], wts["w_r"], wts["b_r"])
    in_specs = [
        pl.BlockSpec(memory_space=pltpu.SMEM),
        _full_spec(xs.shape),
        pl.BlockSpec((DEC_SB, GLA_HEADS, GLA_DK, GLA_DV), lambda i: (i, 0, 0, 0)),
        pl.BlockSpec((DEC_SB, WINDOW, SWA_KV), lambda i: (i, 0, 0)),
        pl.BlockSpec((DEC_SB, WINDOW, SWA_KV), lambda i: (i, 0, 0)),
        _full_spec(base0.shape), _full_spec(bias_dec.shape),
    ] + [_full_spec(w.shape) for w in weight_args]
    out_specs = [
        _full_spec((n_seq, D_MODEL)),
        _full_spec((n_seq * SLAB, LANES)),
        _full_spec((8, n_seq)), _full_spec((8, n_seq)), _full_spec((8, n_seq)),
        pl.BlockSpec((DEC_SB, GLA_HEADS, GLA_DK, GLA_DV), lambda i: (i, 0, 0, 0)),
        pl.BlockSpec((DEC_SB, WINDOW, SWA_KV), lambda i: (i, 0, 0)),
        pl.BlockSpec((DEC_SB, WINDOW, SWA_KV), lambda i: (i, 0, 0)),
        _full_spec((N_EXPERTS, LANES)),
    ]
    out_shape = [
        jax.ShapeDtypeStruct((n_seq, D_MODEL), F32),
        jax.ShapeDtypeStruct((n_seq * SLAB, LANES), F32),
        jax.ShapeDtypeStruct((8, n_seq), jnp.int32),
        jax.ShapeDtypeStruct((8, n_seq), F32),
        jax.ShapeDtypeStruct((8, n_seq), jnp.int32),
        jax.ShapeDtypeStruct(state.shape, F32),
        jax.ShapeDtypeStruct(ck.shape, F32),
        jax.ShapeDtypeStruct(cv.shape, F32),
        jax.ShapeDtypeStruct((N_EXPERTS, LANES), F32),
    ]
    scratch = [pltpu.VMEM((GLA_QK, n_seq), F32)] * 3 + [
        pltpu.VMEM((n_seq, GLA_V), F32), pltpu.VMEM((n_seq, GLA_V), F32), pltpu.VMEM((n_seq, SWA_Q), F32),
        pltpu.VMEM((n_seq, SWA_KV), F32), pltpu.VMEM((n_seq, SWA_KV), F32),
        pltpu.VMEM((n_seq, GLA_V), F32), pltpu.VMEM((n_seq, SWA_Q), F32)]
    return pl.pallas_call(
        _decode_kernel,
        grid=(nb,),
        in_specs=in_specs,
        out_specs=out_specs,
        out_shape=out_shape,
        scratch_shapes=scratch,
        compiler_params=pltpu.CompilerParams(dimension_semantics=("arbitrary",),
                                             vmem_limit_bytes=VMEM_LIMIT),
        name="decode",
    )(wts["sinks"], xs, state, ck, cv, base0, bias_dec, *weight_args)


FF_TILE = 256


def _expert_kernel(blk_e_ref, nused_ref, tok0_ref, tokn_ref, retp_ref, hp_hbm, wu_ref, bu_ref, wd_ref, bd_ref,
                   y_hbm, xbuf0, xbuf1, ybuf0, ybuf1, xbf, actbf, zbuf, wu_bf, wd_bf, gsem, ssem, zsem):
    i = pl.program_id(0)
    tm = MOE_TM
    nused = nused_ref[0]
    n_tiles = D_FF // FF_TILE
    per = tm // (2 * n_tiles)
    xbufs, ybufs = (xbuf0, xbuf1), (ybuf0, ybuf1)

    def gather(tok_ref, r, s):
        src = hp_hbm.at[pl.ds(pl.multiple_of(tok_ref[0, 0, r], SLAB), SLAB)]
        return pltpu.make_async_copy(src, xbufs[s].at[pl.ds(r * SLAB, SLAB)], gsem.at[s])

    def scatter(r, s):
        dst = y_hbm.at[pl.ds(pl.multiple_of(retp_ref[0, 0, r], SLAB), SLAB)]
        return pltpu.make_async_copy(ybufs[s].at[pl.ds(r * SLAB, SLAB)], dst, ssem.at[s])

    def wait_gather(s):
        pltpu.make_async_copy(hp_hbm.at[pl.ds(0, tm * SLAB)], xbufs[s], gsem.at[s]).wait()

    def wait_scatter(s):
        pltpu.make_async_copy(ybufs[s], y_hbm.at[pl.ds(0, tm * SLAB)], ssem.at[s]).wait()

    @pl.when(i == 0)
    def _():
        ybuf1[...] = jnp.zeros_like(ybuf1)
        zbuf[...] = jnp.zeros_like(zbuf)

        def issue(r, c):
            gather(tok0_ref, r, 0).start()
            return c
        lax.fori_loop(0, tm, issue, 0)

    def compute(s):
        o = 1 - s
        wait_gather(s)

        @pl.when(i >= 1)
        def _():
            wait_scatter(s)

        @pl.when((i == 0) | (blk_e_ref[i] != blk_e_ref[jnp.maximum(i - 1, 0)]))
        def _():
            wu_bf[...] = wu_ref[0].astype(BF16)
            wd_bf[...] = wd_ref[0].astype(BF16)

        for c in range(SLAB):
            xbf[:, c * LANES:(c + 1) * LANES] = _load_slab_chunk(xbufs[s], tm, c, SLAB).astype(BF16)
        def issue_rows(stage):
            for r in range(stage * per, (stage + 1) * per):
                gather(tokn_ref, r, o).start()
                scatter(r, o).start()

        for n in range(n_tiles):
            issue_rows(n)
            gc = slice(n * FF_TILE, (n + 1) * FF_TILE)
            lc = slice(D_FF + n * FF_TILE, D_FF + (n + 1) * FF_TILE)
            g = jnp.minimum(_dot(xbf[...], wu_bf[:, gc]) + bu_ref[0, :, gc], SWIGLU_LIMIT)
            lin = jnp.clip(_dot(xbf[...], wu_bf[:, lc]) + bu_ref[0, :, lc], -SWIGLU_LIMIT, SWIGLU_LIMIT)
            actbf[:, gc] = (g * jax.nn.sigmoid(SWIGLU_ALPHA * g) * (lin + 1.0)).astype(BF16)
        for n in range(n_tiles):
            issue_rows(n_tiles + n)
            yc = slice(n * FF_TILE, (n + 1) * FF_TILE)
            y = _dot(actbf[...], wd_bf[:, yc]) + bd_ref[0, :, yc]
            for c in range(FF_TILE // LANES):
                ybufs[s][pl.ds(n * (FF_TILE // LANES) + c, tm, stride=SLAB), :] = y[:, c * LANES:(c + 1) * LANES]

    def drain(s):
        o = 1 - s
        wait_gather(s)
        wait_scatter(s)

        def issue(r, c):
            scatter(r, o).start()
            return c
        lax.fori_loop(0, tm, issue, 0)
        wait_scatter(o)

    for s in (0, 1):
        @pl.when((i < nused) & (i % 2 == s))
        def _():
            compute(s)

        @pl.when((i == nused) & (i % 2 == s))
        def _():
            drain(s)

    @pl.when(i >= nused)
    def _():
        zc = pltpu.make_async_copy(zbuf, y_hbm.at[pl.ds(pl.multiple_of((tm + i * tm) * SLAB, SLAB), tm * SLAB)], zsem)
        zc.start()
        zc.wait()


def _expert_call(blk_e, nused, tok_sorted, ret_sorted, hp_all, w_up, b_up, w_down, b_down, n_slots):
    n_blocks = blk_e.shape[0]
    tm = MOE_TM
    n_rows = n_blocks * tm + tm
    ret_tab = (jnp.concatenate([n_slots + jnp.arange(tm, dtype=jnp.int32), ret_sorted]) * SLAB
               ).reshape(n_blocks + 1, 1, tm)
    tok_tab = (tok_sorted * SLAB).reshape(n_blocks, 1, tm)
    smem_blk = functools.partial(pl.BlockSpec, (1, 1, tm), memory_space=pltpu.SMEM)
    grid_spec = pltpu.PrefetchScalarGridSpec(
        num_scalar_prefetch=2,
        grid=(n_blocks,),
        in_specs=[
            smem_blk(lambda i, be, nu: (0, 0, 0)),
            smem_blk(lambda i, be, nu: (jnp.minimum(i + 1, n_blocks - 1), 0, 0)),
            smem_blk(lambda i, be, nu: (i, 0, 0)),
            pl.BlockSpec(memory_space=pl.ANY),
            pl.BlockSpec((1, D_MODEL, 2 * D_FF), lambda i, be, nu: (be[i], 0, 0)),
            pl.BlockSpec((1, 1, 2 * D_FF), lambda i, be, nu: (be[i], 0, 0)),
            pl.BlockSpec((1, D_FF, D_MODEL), lambda i, be, nu: (be[i], 0, 0)),
            pl.BlockSpec((1, 1, D_MODEL), lambda i, be, nu: (be[i], 0, 0)),
        ],
        out_specs=pl.BlockSpec(memory_space=pl.ANY),
        scratch_shapes=[pltpu.VMEM((tm * SLAB, LANES), F32), pltpu.VMEM((tm * SLAB, LANES), F32),
                        pltpu.VMEM((tm * SLAB, LANES), F32), pltpu.VMEM((tm * SLAB, LANES), F32),
                        pltpu.VMEM((tm, D_MODEL), BF16), pltpu.VMEM((tm, D_FF), BF16),
                        pltpu.VMEM((tm * SLAB, LANES), F32),
                        pltpu.VMEM((D_MODEL, 2 * D_FF), BF16), pltpu.VMEM((D_FF, D_MODEL), BF16),
                        pltpu.SemaphoreType.DMA((2,)), pltpu.SemaphoreType.DMA((2,)), pltpu.SemaphoreType.DMA],
    )
    return pl.pallas_call(
        _expert_kernel,
        grid_spec=grid_spec,
        out_shape=jax.ShapeDtypeStruct((n_rows * SLAB, LANES), F32),
        compiler_params=pltpu.CompilerParams(dimension_semantics=("arbitrary",),
                                             vmem_limit_bytes=VMEM_LIMIT),
        name="experts",
    )(blk_e, nused, tok_tab, tok_tab, ret_tab, hp_all,
      w_up, b_up.reshape(N_EXPERTS, 1, 2 * D_FF), w_down, b_down.reshape(N_EXPERTS, 1, D_MODEL))


def _combine_kernel(ys_ref, xmid_ref, gate_ref, g_final_ref, y_ref):
    tm = xmid_ref.shape[0]
    gts = gate_ref[...]
    chunks = []
    for c in range(SLAB):
        acc = xmid_ref[:, c * LANES:(c + 1) * LANES]
        for k in range(TOP_K):
            acc = acc + _load_slab_chunk(ys_ref, tm, k * SLAB + c, TOP_K * SLAB) * gts[:, k:k + 1]
        chunks.append(acc)
    y_ref[...] = _rms(jnp.concatenate(chunks, axis=1), g_final_ref[...])


def _combine_call(ys4, row0, x_mid, gates, g_final, tm):
    T = x_mid.shape[0]
    blk0 = row0 // tm
    return pl.pallas_call(
        _combine_kernel,
        grid=(T // tm,),
        in_specs=[
            pl.BlockSpec((tm * TOP_K * SLAB, LANES), lambda i: (blk0 + i, 0)),
            pl.BlockSpec((tm, D_MODEL), lambda i: (i, 0)),
            pl.BlockSpec((tm, TOP_K), lambda i: (i, 0)),
            _full_spec((1, D_MODEL)),
        ],
        out_specs=pl.BlockSpec((tm, D_MODEL), lambda i: (i, 0)),
        out_shape=jax.ShapeDtypeStruct((T, D_MODEL), F32),
        compiler_params=pltpu.CompilerParams(dimension_semantics=("arbitrary",),
                                             vmem_limit_bytes=VMEM_LIMIT),
        name="combine",
    )(ys4, x_mid, gates, g_final)


def _t5_bucket(dist):
    n = jnp.maximum(dist, 0)
    max_exact = NUM_BUCKETS // 2
    nf = jnp.maximum(n, 1).astype(F32)
    large = max_exact + (jnp.log(nf / max_exact) / math.log(MAX_DISTANCE / max_exact)
                         * (NUM_BUCKETS - max_exact)).astype(jnp.int32)
    large = jnp.minimum(large, NUM_BUCKETS - 1)
    return jnp.where(n < max_exact, n, large)


def kernel(x_prompt, x_sample, state_gla, cache_swa_k, cache_swa_v, meta_tokens, rel_bias_table,
           g_mix, w_in, w_a_up, b_a, g_gla_out, g_swa_out, attn_sinks, w_out,
           g_ffn, w_router, b_router, w_up, b_up, w_down, b_down, g_final):
    assert g_mix.shape[0] == 1, "single-layer trunk"
    B, L, _ = x_prompt.shape
    n_seq = x_sample.shape[0]
    TP = B * L
    T_all = TP + n_seq

    wi = w_in[0]
    sizes = (GLA_QK, GLA_QK, GLA_V, GLA_V, GLA_LOWRANK, SWA_Q, SWA_KV, SWA_KV)
    offs = [0]
    for s in sizes:
        offs.append(offs[-1] + s)
    seg = [wi[:, offs[n]:offs[n + 1]] for n in range(8)]
    w_in_r = jnp.concatenate(
        seg[0:4] + seg[5:8] + [seg[4], jnp.zeros((D_MODEL, LANES - GLA_LOWRANK), F32)], axis=1).astype(BF16)
    w_a_pad = jnp.concatenate([w_a_up[0], jnp.zeros((LANES - GLA_LOWRANK, GLA_QK), F32)], axis=0).astype(BF16)
    wr_t = jnp.transpose(w_router[0])
    wr_hi = wr_t.astype(BF16)
    wr_lo = (wr_t - wr_hi.astype(F32)).astype(BF16)
    qi = jnp.arange(WINDOW)[:, None]
    kj = jnp.arange(2 * WINDOW)[None, :]
    buckets = jnp.arange(NUM_BUCKETS)
    table = rel_bias_table.astype(F32)
    oh_p = (_t5_bucket(qi - kj + WINDOW)[..., None] == buckets).astype(F32)
    bias_p = jnp.einsum("qkb,bh->hqk", oh_p, table, precision=lax.Precision.HIGHEST)
    bias_p = bias_p.reshape(SWA_KV_HEADS, SWA_GROUP * WINDOW, 2 * WINDOW)
    oh_d = (_t5_bucket(WINDOW - 1 - jnp.arange(WINDOW))[:, None] == buckets).astype(F32)
    bias_d = jnp.einsum("rb,bh->hr", oh_d, table, precision=lax.Precision.HIGHEST)
    bias_d = jnp.concatenate([bias_d, jnp.zeros((8, WINDOW), F32)], axis=0)
    wts = dict(
        sinks=attn_sinks[0].astype(F32), bias=bias_p,
        g_mix=g_mix[0][None], w_in=w_in_r, w_a_up=w_a_pad, b_a=b_a[0][None],
        g_gla=g_gla_out[0][None], g_swa=g_swa_out[0][None], w_out=w_out[0].astype(BF16),
        g_ffn=g_ffn[0][None], w_r=jnp.concatenate([wr_hi, wr_lo], axis=0), b_r=b_router[0][:, None],
    )

    x_pre = jnp.concatenate([jnp.zeros((WINDOW - N_META, D_MODEL), F32), meta_tokens.astype(F32)], axis=0)[None]
    zeros_s = jnp.zeros((GLA_QK, GLA_V), F32)
    zeros_kv = jnp.zeros((WINDOW, SWA_KV), F32)
    zeros_b = jnp.zeros((N_EXPERTS, LANES), F32)
    pre = _mixer_call(x_pre, zeros_s, zeros_kv, zeros_kv, zeros_b, wts, WINDOW, WINDOW - N_META, 0, WINDOW)
    s_meta, k_meta, v_meta = pre[5][0], pre[6][0], pre[7][0]

    (xmid_p, hp_p, topi_p, gate_p, rank_p, s_p, k_p, v_p, cnt_p) = _mixer_call(
        x_prompt, s_meta, k_meta, v_meta, zeros_b, wts, MIX_TM, 0, WINDOW - N_META, TP)

    (xmid_s, hp_s, topi_s, gate_s, rank_s, st_s, ck_s, cv_s, cnt_all) = _decode_call(
        x_sample[:, 0], state_gla[0], cache_swa_k[0].reshape(n_seq, WINDOW, SWA_KV),
        cache_swa_v[0].reshape(n_seq, WINDOW, SWA_KV), cnt_p, bias_d, wts)

    tm = MOE_TM
    n_slots = T_all * TOP_K
    n_blocks = -(-n_slots // tm) + N_EXPERTS
    top_e = jnp.concatenate([topi_p[:TOP_K], topi_s[:TOP_K]], axis=1)
    rank = jnp.concatenate([rank_p[:TOP_K], rank_s[:TOP_K]], axis=1)
    counts = cnt_all[:, 0].astype(jnp.int32)
    padded = (counts + tm - 1) // tm * tm
    pad_end = jnp.cumsum(padded)
    pad_start = pad_end - padded
    e_ids = jnp.arange(N_EXPERTS, dtype=jnp.int32)
    dest = jnp.sum(jnp.where(top_e[..., None] == e_ids, pad_start, 0), axis=-1) + rank
    slot_id = jnp.arange(T_all, dtype=jnp.int32)[None] * TOP_K + jnp.arange(TOP_K, dtype=jnp.int32)[:, None]
    n_pad = n_blocks * tm
    ret_real = jnp.full((n_pad,), -1, jnp.int32).at[dest.reshape(-1)].set(slot_id.reshape(-1))
    pos = jnp.arange(n_pad, dtype=jnp.int32)
    real_before = jnp.sum(jnp.where(pad_start[None] <= pos[:, None], counts[None], 0), axis=1)
    is_real = ret_real >= 0
    ret_sorted = jnp.where(is_real, ret_real, n_slots + tm + pos - real_before)
    tok_sorted = jnp.where(is_real, ret_real // TOP_K, 0)
    blk_e = jnp.minimum(jnp.sum(pad_end[None] <= (jnp.arange(n_blocks, dtype=jnp.int32) * tm)[:, None], axis=1),
                        N_EXPERTS - 1).astype(jnp.int32)
    nused = (pad_end[-1] // tm).astype(jnp.int32).reshape(1)

    hp_all = jnp.concatenate([hp_p, hp_s], axis=0)
    ys4 = _expert_call(blk_e, nused, tok_sorted, ret_sorted, hp_all, w_up[0], b_up[0], w_down[0], b_down[0],
                       n_slots)

    gates = jnp.transpose(jnp.concatenate([gate_p[:TOP_K], gate_s[:TOP_K]], axis=1))
    gf = g_final[None]
    y_p = _combine_call(ys4, 0, xmid_p.reshape(TP, D_MODEL), gates[:TP], gf, MIX_TM)
    y_s = _combine_call(ys4, TP, xmid_s, gates[TP:], gf, n_seq)

    s_heads = jnp.stack([s_p[:, h * GLA_DK:(h + 1) * GLA_DK, h * GLA_DV:(h + 1) * GLA_DV]
                         for h in range(GLA_HEADS)], axis=1)
    return (y_p.reshape(B, L, D_MODEL), y_s.reshape(n_seq, 1, D_MODEL), s_heads[None],
            k_p.reshape(1, B, WINDOW, SWA_KV_HEADS, SWA_HEAD_DIM),
            v_p.reshape(1, B, WINDOW, SWA_KV_HEADS, SWA_HEAD_DIM),
            st_s[None], ck_s.reshape(1, n_seq, WINDOW, SWA_KV_HEADS, SWA_HEAD_DIM),
            cv_s.reshape(1, n_seq, WINDOW, SWA_KV_HEADS, SWA_HEAD_DIM))
```

```python
import functools
import math

import jax
import jax.numpy as jnp
from jax import lax
from jax.experimental import pallas as pl
from jax.experimental.pallas import tpu as pltpu

D_MODEL = 1024
N_META = 16
GLA_HEADS = 4
GLA_DK = 64
GLA_DV = 128
GLA_LOWRANK = 16
GLA_GATE_TAU = 16.0
GLA_CHUNK = 64
SWA_HEADS = 8
SWA_KV_HEADS = 2
SWA_HEAD_DIM = 64
SWA_GROUP = SWA_HEADS // SWA_KV_HEADS
WINDOW = 128
NUM_BUCKETS = 32
MAX_DISTANCE = 128
N_EXPERTS = 32
TOP_K = 4
D_FF = 1024
SWIGLU_ALPHA = 1.702
SWIGLU_LIMIT = 7.0
RMS_EPS = 1e-6

GLA_QK = GLA_HEADS * GLA_DK
GLA_V = GLA_HEADS * GLA_DV
SWA_Q = SWA_HEADS * SWA_HEAD_DIM
SWA_KV = SWA_KV_HEADS * SWA_HEAD_DIM
LANES = 128
C_GQ, C_GK, C_GV, C_GR = 0, GLA_QK, 2 * GLA_QK, 2 * GLA_QK + GLA_V
C_SQ = C_GR + GLA_V
C_SK = C_SQ + SWA_Q
C_SV = C_SK + SWA_KV
C_GA = C_SV + SWA_KV
D_PROJ = C_GA + LANES

MIX_TM = 256
MOE_TM = 256
DEC_SB = 16
VMEM_LIMIT = 56 * 1024 * 1024

F32 = jnp.float32
BF16 = jnp.bfloat16
NEG_INF = float("-inf")


def _dot(a, b):
    return jnp.dot(a, b, preferred_element_type=F32)


def _dot_nt(a, b):
    return lax.dot_general(a, b, (((1,), (1,)), ((), ())), preferred_element_type=F32)


def _split3(x):
    hi = x.astype(BF16)
    r1 = x - hi.astype(F32)
    mid = r1.astype(BF16)
    lo = (r1 - mid.astype(F32)).astype(BF16)
    return hi, mid, lo


def _rms(x, g):
    return x * lax.rsqrt(jnp.mean(x * x, axis=-1, keepdims=True) + RMS_EPS) * g


def _iota(shape, dim):
    return lax.broadcasted_iota(jnp.int32, shape, dim)


SLAB = D_MODEL // LANES


def _store_slabs(ref, x):
    rows = x.shape[0]
    for c in range(SLAB):
        ref[pl.ds(c, rows, stride=SLAB), :] = x[:, c * LANES:(c + 1) * LANES]


def _load_slab_chunk(ref, rows, first, stride):
    return ref[pl.ds(first, rows, stride=stride), :]


def _project(x, g_mix, w_in, w_a_up, b_a):
    h = _rms(x, g_mix).astype(BF16)
    proj = _dot(h, w_in)
    ga = proj[:, C_GA:C_GA + LANES].astype(BF16)
    z = _dot(ga, w_a_up) + b_a
    log_a = -(jnp.maximum(-z, 0.0) + jnp.log1p(jnp.exp(-jnp.abs(z)))) / GLA_GATE_TAU
    return dict(
        gq=proj[:, C_GQ:C_GQ + GLA_QK] * (GLA_DK ** -0.5),
        gk=proj[:, C_GK:C_GK + GLA_QK],
        gv=proj[:, C_GV:C_GV + GLA_V],
        gr=proj[:, C_GR:C_GR + GLA_V],
        sq=proj[:, C_SQ:C_SQ + SWA_Q],
        sk=proj[:, C_SK:C_SK + SWA_KV],
        sv=proj[:, C_SV:C_SV + SWA_KV],
        log_a=log_a,
    )


def _tail(x, o_gla, gr, o_swa, g_gla_out, g_swa_out, w_out, g_ffn, w_r, b_r, base):
    tm = x.shape[0]
    gate = gr * jax.nn.sigmoid(gr)
    parts = []
    for h in range(GLA_HEADS):
        sl = slice(h * GLA_DV, (h + 1) * GLA_DV)
        parts.append(_rms(o_gla[:, sl], g_gla_out) * gate[:, sl])
    parts.append(_rms(o_swa, g_swa_out))
    o = jnp.concatenate(parts, axis=1).astype(BF16)
    x_mid = x + _dot(o, w_out)
    hp = _rms(x_mid, g_ffn)

    h1 = hp.astype(BF16)
    h2 = (hp - h1.astype(F32)).astype(BF16)
    la = _dot_nt(w_r, h1)
    lb = _dot_nt(w_r[0:N_EXPERTS], h2)
    logits = la[0:N_EXPERTS] + la[N_EXPERTS:2 * N_EXPERTS] + lb + b_r

    eidx = _iota((N_EXPERTS, tm), 0)
    vals, idxs, onehots = [], [], []
    l = logits
    for _ in range(TOP_K):
        m = jnp.max(l, axis=0, keepdims=True)
        sel = jnp.min(jnp.where(l == m, eidx, N_EXPERTS), axis=0, keepdims=True)
        oh = eidx == sel
        l = jnp.where(oh, NEG_INF, l)
        vals.append(m)
        idxs.append(sel)
        onehots.append(oh)
    es = [jnp.exp(v - vals[0]) for v in vals]
    denom = es[0] + es[1] + es[2] + es[3]
    gates = [e / denom for e in es]

    ohf = jnp.concatenate([oh.astype(F32) for oh in onehots], axis=0)
    upper = (_iota((tm, tm), 0) < _iota((tm, tm), 1)).astype(BF16)
    prefix = _dot(ohf.astype(BF16), upper)
    ranks = []
    for k in range(TOP_K):
        sl = slice(k * N_EXPERTS, (k + 1) * N_EXPERTS)
        ohk = ohf[sl]
        base_t = jnp.concatenate([base] * (tm // LANES), axis=1)
        ranks.append(jnp.sum(ohk * (prefix[sl] + base_t), axis=0, keepdims=True))
        base = base + jnp.sum(ohk, axis=1, keepdims=True)
    zi = jnp.zeros((8 - TOP_K, tm), jnp.int32)
    zf = jnp.zeros((8 - TOP_K, tm), F32)
    topi = jnp.concatenate(idxs + [zi], axis=0)
    gate8 = jnp.concatenate(gates + [zf], axis=0)
    rank8 = jnp.concatenate([r.astype(jnp.int32) for r in ranks] + [zi], axis=0)
    return x_mid, hp, topi, gate8, rank8, base


def _gla_chunks(p, row0, s_bd, n_lead_pad):
    tm = p["gq"].shape[0]
    nch = tm // GLA_CHUNK
    log_a = p["log_a"]
    if n_lead_pad:
        rows = row0 + _iota((tm, GLA_QK), 0)
        log_a = jnp.where(rows >= n_lead_pad, log_a, 0.0)
    ri, ci = _iota((tm, tm), 0), _iota((tm, tm), 1)
    tril = ((ri >= ci) & (ri // GLA_CHUNK == ci // GLA_CHUNK)).astype(BF16)
    hi, mid, lo = _split3(log_a)
    b_all = _dot(tril, hi) + _dot(tril, mid) + _dot(tril, lo)

    c64 = GLA_CHUNK
    kk_mask = (_iota((GLA_QK, GLA_QK), 0) // c64) == (_iota((GLA_QK, GLA_QK), 1) // GLA_DK)
    vv_mask = (_iota((GLA_QK, GLA_V), 0) // c64) == (_iota((GLA_QK, GLA_V), 1) // GLA_DV)
    ss_mask = (_iota((GLA_QK, GLA_V), 0) // GLA_DK) == (_iota((GLA_QK, GLA_V), 1) // GLA_DV)
    causal = (_iota((c64, GLA_QK), 0) >= (_iota((c64, GLA_QK), 1) % c64)).astype(F32)
    zpad_k = jnp.zeros((LANES - c64, GLA_QK), F32)
    zpad_v = jnp.zeros((LANES - c64, GLA_V), BF16)

    outs = []
    for c in range(nch):
        rs = slice(c * c64, (c + 1) * c64)
        b = b_all[rs]
        q, k, v = p["gq"][rs], p["gk"][rs], p["gv"][rs]
        b_last = b[c64 - 1:c64]
        qt = (q * jnp.exp(b)).astype(BF16)
        kt = k * jnp.exp(-b)
        kd = k * jnp.exp(b_last - b)
        vb = v.astype(BF16)
        k_bd = jnp.where(kk_mask, jnp.concatenate([kt] * GLA_HEADS, axis=0), 0.0).astype(BF16)
        a = (_dot_nt(qt, k_bd) * causal).astype(BF16)
        v_bd = jnp.where(vv_mask, jnp.concatenate([vb] * GLA_HEADS, axis=0), jnp.zeros((), BF16))
        outs.append(_dot(qt, s_bd.astype(BF16)) + _dot(a, v_bd))
        kd_t = jnp.transpose(jnp.concatenate([kd, zpad_k], axis=0)).astype(BF16)
        upd = _dot(kd_t, jnp.concatenate([vb, zpad_v], axis=0))
        decay = jnp.exp(jnp.transpose(jnp.broadcast_to(b_last, (LANES, GLA_QK))))
        s_bd = s_bd * jnp.concatenate([decay] * GLA_HEADS, axis=1) + jnp.where(ss_mask, upd, 0.0)
    return jnp.concatenate(outs, axis=0), s_bd


def _swa_block(sq, kcat, vcat, bias_ref, sinks_ref, valid):
    half = _iota((1, LANES), 1) < SWA_HEAD_DIM
    k_roll = pltpu.roll(kcat, SWA_HEAD_DIM, 1)
    v_roll = pltpu.roll(vcat, SWA_HEAD_DIM, 1)
    cols = []
    for kv in range(SWA_KV_HEADS):
        if kv == 0:
            kk = jnp.where(half, kcat, k_roll)
            v_lo = jnp.where(half, vcat, 0.0)
            v_hi = jnp.where(half, 0.0, v_roll)
        else:
            kk = jnp.where(half, k_roll, kcat)
            v_lo = jnp.where(half, v_roll, 0.0)
            v_hi = jnp.where(half, 0.0, vcat)
        q_parts = []
        for c in (2 * kv, 2 * kv + 1):
            qc = sq[:, c * LANES:(c + 1) * LANES]
            q_parts.append(jnp.where(half, qc, 0.0))
            q_parts.append(jnp.where(half, 0.0, qc))
        q_st = jnp.concatenate(q_parts, axis=0).astype(BF16)
        s = _dot_nt(q_st, kk.astype(BF16)) * (SWA_HEAD_DIM ** -0.5) + bias_ref[kv]
        s = jnp.where(valid, s, NEG_INF)
        sink = jnp.concatenate(
            [jnp.full((WINDOW, 1), sinks_ref[kv * SWA_GROUP + g], F32) for g in range(SWA_GROUP)], axis=0)
        m = jnp.maximum(jnp.max(s, axis=1, keepdims=True), sink)
        pr = jnp.exp(s - m)
        inv = 1.0 / (jnp.sum(pr, axis=1, keepdims=True) + jnp.exp(sink - m))
        pb = pr.astype(BF16)
        p2 = jnp.concatenate([
            jnp.concatenate([pb[0:128], pb[128:256]], axis=1),
            jnp.concatenate([pb[256:384], pb[384:512]], axis=1)], axis=0)
        vv = jnp.concatenate([v_lo, v_hi], axis=0).astype(BF16)
        o2 = _dot(p2, vv)
        cols.append(o2[0:128] * jnp.where(half, inv[0:128], inv[128:256]))
        cols.append(o2[128:256] * jnp.where(half, inv[256:384], inv[384:512]))
    return jnp.concatenate(cols, axis=1)


def _mixer_kernel(sinks_ref, x_ref, s0_ref, k0_ref, v0_ref, base0_ref, bias_ref,
                  g_mix_ref, w_in_ref, w_a_up_ref, b_a_ref, g_gla_ref, g_swa_ref, w_out_ref,
                  g_ffn_ref, w_r_ref, b_r_ref,
                  xmid_ref, hp_ref, topi_ref, gate_ref, rank_ref, sout_ref, kout_ref, vout_ref, cnt_ref,
                  s_scr, k_scr, v_scr, base_scr, *, n_lead_pad, prev_valid_from):
    b_id, j = pl.program_id(0), pl.program_id(1)
    tm = x_ref.shape[1]

    @pl.when(j == 0)
    def _():
        s_scr[...] = s0_ref[...]
        k_scr[...] = k0_ref[...]
        v_scr[...] = v0_ref[...]

    @pl.when((j == 0) & (b_id == 0))
    def _():
        base_scr[...] = base0_ref[...]

    x = x_ref[0]
    p = _project(x, g_mix_ref[...], w_in_ref[...], w_a_up_ref[...], b_a_ref[...])

    o_gla, s_new = _gla_chunks(p, j * tm, s_scr[...], n_lead_pad)
    s_scr[...] = s_new

    qi = _iota((WINDOW, 2 * WINDOW), 0)
    kj = _iota((WINDOW, 2 * WINDOW), 1)
    band = (kj > qi) & (kj <= qi + WINDOW)
    o_parts = []
    for sb in range(tm // WINDOW):
        rs = slice(sb * WINDOW, (sb + 1) * WINDOW)
        k_blk, v_blk = p["sk"][rs], p["sv"][rs]
        k_prev = k_scr[...] if sb == 0 else p["sk"][(sb - 1) * WINDOW:sb * WINDOW]
        v_prev = v_scr[...] if sb == 0 else p["sv"][(sb - 1) * WINDOW:sb * WINDOW]
        valid = band
        if sb == 0 and prev_valid_from:
            first = jnp.where(j == 0, prev_valid_from, 0)
            valid = band & (kj >= first)
        valid = jnp.concatenate([valid] * SWA_GROUP, axis=0)
        o_parts.append(_swa_block(p["sq"][rs], jnp.concatenate([k_prev, k_blk], axis=0),
                                  jnp.concatenate([v_prev, v_blk], axis=0), bias_ref, sinks_ref, valid))
    o_swa = jnp.concatenate(o_parts, axis=0)
    k_scr[...] = p["sk"][tm - WINDOW:tm]
    v_scr[...] = p["sv"][tm - WINDOW:tm]

    x_mid, hp, topi, gate8, rank8, base = _tail(
        x, o_gla, p["gr"], o_swa, g_gla_ref[...], g_swa_ref[...], w_out_ref[...],
        g_ffn_ref[...], w_r_ref[...], b_r_ref[...], base_scr[...])
    base_scr[...] = base
    xmid_ref[0] = x_mid
    _store_slabs(hp_ref, hp)
    topi_ref[...] = topi
    gate_ref[...] = gate8
    rank_ref[...] = rank8
    sout_ref[0] = s_new
    kout_ref[0] = p["sk"][tm - WINDOW:tm]
    vout_ref[0] = p["sv"][tm - WINDOW:tm]
    cnt_ref[...] = base


def _full_spec(shape):
    nd = len(shape)
    return pl.BlockSpec(shape, lambda *_: (0,) * nd)


def _mixer_call(x, s0, k0, v0, base0, wts, tm, n_lead_pad, prev_valid_from, hp_rows):
    B, L, _ = x.shape
    nj = L // tm
    T = B * L
    weight_args = (wts["bias"], wts["g_mix"], wts["w_in"], wts["w_a_up"], wts["b_a"], wts["g_gla"],
                   wts["g_swa"], wts["w_out"], wts["g_ffn"], wts["w_r"], wts["b_r"])
    in_specs = [
        pl.BlockSpec(memory_space=pltpu.SMEM),
        pl.BlockSpec((1, tm, D_MODEL), lambda b, j: (b, j, 0)),
        _full_spec(s0.shape), _full_spec(k0.shape), _full_spec(v0.shape), _full_spec(base0.shape),
    ] + [_full_spec(w.shape) for w in weight_args]
    tok_spec = pl.BlockSpec((8, tm), lambda b, j: (0, b * nj + j))
    out_specs = [
        pl.BlockSpec((1, tm, D_MODEL), lambda b, j: (b, j, 0)),
        pl.BlockSpec((tm * SLAB, LANES), lambda b, j: (b * nj + j, 0)),
        tok_spec, tok_spec, tok_spec,
        pl.BlockSpec((1, GLA_QK, GLA_V), lambda b, j: (b, 0, 0)),
        pl.BlockSpec((1, WINDOW, SWA_KV), lambda b, j: (b, 0, 0)),
        pl.BlockSpec((1, WINDOW, SWA_KV), lambda b, j: (b, 0, 0)),
        _full_spec((N_EXPERTS, LANES)),
    ]
    out_shape = [
        jax.ShapeDtypeStruct((B, L, D_MODEL), F32),
        jax.ShapeDtypeStruct((hp_rows * SLAB, LANES), F32),
        jax.ShapeDtypeStruct((8, T), jnp.int32),
        jax.ShapeDtypeStruct((8, T), F32),
        jax.ShapeDtypeStruct((8, T), jnp.int32),
        jax.ShapeDtypeStruct((B, GLA_QK, GLA_V), F32),
        jax.ShapeDtypeStruct((B, WINDOW, SWA_KV), F32),
        jax.ShapeDtypeStruct((B, WINDOW, SWA_KV), F32),
        jax.ShapeDtypeStruct((N_EXPERTS, LANES), F32),
    ]
    kern = functools.partial(_mixer_kernel, n_lead_pad=n_lead_pad, prev_valid_from=prev_valid_from)
    return pl.pallas_call(
        kern,
        grid=(B, nj),
        in_specs=in_specs,
        out_specs=out_specs,
        out_shape=out_shape,
        scratch_shapes=[pltpu.VMEM((GLA_QK, GLA_V), F32), pltpu.VMEM((WINDOW, SWA_KV), F32),
                        pltpu.VMEM((WINDOW, SWA_KV), F32), pltpu.VMEM((N_EXPERTS, LANES), F32)],
        compiler_params=pltpu.CompilerParams(dimension_semantics=("arbitrary", "arbitrary"),
                                             vmem_limit_bytes=VMEM_LIMIT),
        name="mixer",
    )(wts["sinks"], x, s0, k0, v0, base0, *weight_args)


def _decode_kernel(sinks_ref, x_ref, st_ref, ck_ref, cv_ref, base0_ref, bias_ref,
                   g_mix_ref, w_in_ref, w_a_up_ref, b_a_ref, g_gla_ref, g_swa_ref, w_out_ref,
                   g_ffn_ref, w_r_ref, b_r_ref,
                   xmid_ref, hp_ref, topi_ref, gate_ref, rank_ref, sto_ref, cko_ref, cvo_ref, cnt_ref,
                   at_scr, kt_scr, qt_scr, gv_scr, gr_scr, sq_scr, sk_scr, sv_scr, og_scr, os_scr):
    i = pl.program_id(0)
    n_seq = x_ref.shape[0]

    @pl.when(i == 0)
    def _():
        p = _project(x_ref[...], g_mix_ref[...], w_in_ref[...], w_a_up_ref[...], b_a_ref[...])
        at_scr[...] = jnp.transpose(jnp.exp(p["log_a"]))
        kt_scr[...] = jnp.transpose(p["gk"])
        qt_scr[...] = jnp.transpose(p["gq"])
        gv_scr[...] = p["gv"]
        gr_scr[...] = p["gr"]
        sq_scr[...] = p["sq"]
        sk_scr[...] = p["sk"]
        sv_scr[...] = p["sv"]

    lane_seq = _iota((GLA_QK, n_seq), 1)
    half = _iota((1, LANES), 1) < SWA_HEAD_DIM
    row_id = _iota((WINDOW, SWA_KV), 0)
    head_diag = (_iota((16, SWA_Q), 1) // SWA_HEAD_DIM) == _iota((16, SWA_Q), 0)
    sink_col = jnp.concatenate(
        [jnp.full((1, 1), sinks_ref[h], F32) for h in range(SWA_HEADS)] + [jnp.zeros((8, 1), F32)], axis=0)

    def per_seq(sl, carry):
        s = i * DEC_SB + sl
        sel = lane_seq == s
        a_c = jnp.sum(jnp.where(sel, at_scr[...], 0.0), axis=1, keepdims=True)
        k_c = jnp.sum(jnp.where(sel, kt_scr[...], 0.0), axis=1, keepdims=True)
        q_c = jnp.sum(jnp.where(sel, qt_scr[...], 0.0), axis=1, keepdims=True)
        st = st_ref[sl].reshape(GLA_QK, GLA_DV)
        v_row = gv_scr[pl.ds(s, 1), :]
        v_b = jnp.concatenate(
            [jnp.broadcast_to(v_row[:, h * GLA_DV:(h + 1) * GLA_DV], (GLA_DK, GLA_DV))
             for h in range(GLA_HEADS)], axis=0)
        st_new = a_c * st + k_c * v_b
        sto_ref[sl] = st_new.reshape(GLA_HEADS, GLA_DK, GLA_DV)
        t = q_c * st_new
        og_scr[pl.ds(s, 1), :] = jnp.concatenate(
            [jnp.sum(t[h * GLA_DK:(h + 1) * GLA_DK], axis=0, keepdims=True) for h in range(GLA_HEADS)],
            axis=1)

        k_new = sk_scr[pl.ds(s, 1), :]
        v_new = sv_scr[pl.ds(s, 1), :]
        kn = jnp.where(row_id == WINDOW - 1, k_new, pltpu.roll(ck_ref[sl], WINDOW - 1, 0))
        vn = jnp.where(row_id == WINDOW - 1, v_new, pltpu.roll(cv_ref[sl], WINDOW - 1, 0))
        cko_ref[sl] = kn
        cvo_ref[sl] = vn
        kr, vr = pltpu.roll(kn, SWA_HEAD_DIM, 1), pltpu.roll(vn, SWA_HEAD_DIM, 1)
        k0, k1 = jnp.where(half, kn, kr), jnp.where(half, kr, kn)
        v0, v1 = jnp.where(half, vn, vr), jnp.where(half, vr, vn)
        kw = jnp.concatenate([k0, k0, k1, k1], axis=1).astype(BF16)
        vw = jnp.concatenate([v0, v0, v1, v1], axis=1).astype(BF16)
        q_row = sq_scr[pl.ds(s, 1), :]
        qm = jnp.where(head_diag, jnp.broadcast_to(q_row, (16, SWA_Q)), 0.0).astype(BF16)
        sc = _dot_nt(qm, kw) * (SWA_HEAD_DIM ** -0.5) + bias_ref[...]
        m = jnp.maximum(jnp.max(sc, axis=1, keepdims=True), sink_col)
        pr = jnp.exp(sc - m)
        inv = 1.0 / (jnp.sum(pr, axis=1, keepdims=True) + jnp.exp(sink_col - m))
        ow = _dot(pr.astype(BF16), vw) * inv
        os_scr[pl.ds(s, 1), :] = jnp.sum(jnp.where(head_diag, ow, 0.0), axis=0, keepdims=True)
        return carry

    lax.fori_loop(0, DEC_SB, per_seq, 0)

    @pl.when(i == pl.num_programs(0) - 1)
    def _():
        x_mid, hp, topi, gate8, rank8, base = _tail(
            x_ref[...], og_scr[...], gr_scr[...], os_scr[...], g_gla_ref[...], g_swa_ref[...],
            w_out_ref[...], g_ffn_ref[...], w_r_ref[...], b_r_ref[...], base0_ref[...])
        xmid_ref[...] = x_mid
        _store_slabs(hp_ref, hp)
        topi_ref[...] = topi
        gate_ref[...] = gate8
        rank_ref[...] = rank8
        cnt_ref[...] = base


def _decode_call(xs, state, ck, cv, base0, bias_dec, wts):
    n_seq = xs.shape[0]
    nb = n_seq // DEC_SB
    weight_args = (wts["g_mix"], wts["w_in"], wts["w_a_up"], wts["b_a"], wts["g_gla"],
                   wts["g_swa"], wts["w_out"], wts["g_ffn"], wts["w_r"], wts["b_r"])
    in_specs = [
        pl.BlockSpec(memory_space=pltpu.SMEM),
        _full_spec(xs.shape),
        pl.BlockSpec((DEC_SB, GLA_HEADS, GLA_DK, GLA_DV), lambda i: (i, 0, 0, 0)),
        pl.BlockSpec((DEC_SB, WINDOW, SWA_KV), lambda i: (i, 0, 0)),
        pl.BlockSpec((DEC_SB, WINDOW, SWA_KV), lambda i: (i, 0, 0)),
        _full_spec(base0.shape), _full_spec(bias_dec.shape),
    ] + [_full_spec(w.shape) for w in weight_args]
    out_specs = [
        _full_spec((n_seq, D_MODEL)),
        _full_spec((n_seq * SLAB, LANES)),
        _full_spec((8, n_seq)), _full_spec((8, n_seq)), _full_spec((8, n_seq)),
        pl.BlockSpec((DEC_SB, GLA_HEADS, GLA_DK, GLA_DV), lambda i: (i, 0, 0, 0)),
        pl.BlockSpec((DEC_SB, WINDOW, SWA_KV), lambda i: (i, 0, 0)),
        pl.BlockSpec((DEC_SB, WINDOW, SWA_KV), lambda i: (i, 0, 0)),
        _full_spec((N_EXPERTS, LANES)),
    ]
    out_shape = [
        jax.ShapeDtypeStruct((n_seq, D_MODEL), F32),
        jax.ShapeDtypeStruct((n_seq * SLAB, LANES), F32),
        jax.ShapeDtypeStruct((8, n_seq), jnp.int32),
        jax.ShapeDtypeStruct((8, n_seq), F32),
        jax.ShapeDtypeStruct((8, n_seq), jnp.int32),
        jax.ShapeDtypeStruct(state.shape, F32),
        jax.ShapeDtypeStruct(ck.shape, F32),
        jax.ShapeDtypeStruct(cv.shape, F32),
        jax.ShapeDtypeStruct((N_EXPERTS, LANES), F32),
    ]
    scratch = [pltpu.VMEM((GLA_QK, n_seq), F32)] * 3 + [
        pltpu.VMEM((n_seq, GLA_V), F32), pltpu.VMEM((n_seq, GLA_V), F32), pltpu.VMEM((n_seq, SWA_Q), F32),
        pltpu.VMEM((n_seq, SWA_KV), F32), pltpu.VMEM((n_seq, SWA_KV), F32),
        pltpu.VMEM((n_seq, GLA_V), F32), pltpu.VMEM((n_seq, SWA_Q), F32)]
    return pl.pallas_call(
        _decode_kernel,
        grid=(nb,),
        in_specs=in_specs,
        out_specs=out_specs,
        out_shape=out_shape,
        scratch_shapes=scratch,
        compiler_params=pltpu.CompilerParams(dimension_semantics=("arbitrary",),
                                             vmem_limit_bytes=VMEM_LIMIT),
        name="decode",
    )(wts["sinks"], xs, state, ck, cv, base0, bias_dec, *weight_args)


FF_TILE = 256


RING = 3


def _expert_kernel(blk_e_ref, nused_ref, tok0_ref, tok1_ref, tokn_ref, retp_ref, hp_hbm, wu_ref, bu_ref,
                   wd_ref, bd_ref, y_hbm, xbuf0, xbuf1, xbuf2, ybuf0, ybuf1, ybuf2, xbf, actbf, zbuf,
                   wu_bf, wd_bf, gsem, ssem, zsem):
    i = pl.program_id(0)
    tm = MOE_TM
    nused = nused_ref[0]
    n_tiles = D_FF // FF_TILE
    xbufs, ybufs = (xbuf0, xbuf1, xbuf2), (ybuf0, ybuf1, ybuf2)

    def gather(tok_ref, r, s):
        src = hp_hbm.at[pl.ds(pl.multiple_of(tok_ref[0, 0, r], SLAB), SLAB)]
        return pltpu.make_async_copy(src, xbufs[s].at[pl.ds(r * SLAB, SLAB)], gsem.at[s])

    def scatter(r, s):
        dst = y_hbm.at[pl.ds(pl.multiple_of(retp_ref[0, 0, r], SLAB), SLAB)]
        return pltpu.make_async_copy(ybufs[s].at[pl.ds(r * SLAB, SLAB)], dst, ssem.at[s])

    def wait_gather(s):
        pltpu.make_async_copy(hp_hbm.at[pl.ds(0, tm * SLAB)], xbufs[s], gsem.at[s]).wait()

    def wait_scatter(s):
        pltpu.make_async_copy(ybufs[s], y_hbm.at[pl.ds(0, tm * SLAB)], ssem.at[s]).wait()

    @pl.when(i == 0)
    def _():
        ybuf2[...] = jnp.zeros_like(ybuf2)
        zbuf[...] = jnp.zeros_like(zbuf)

        def issue(r, c):
            gather(tok0_ref, r, 0).start()
            gather(tok1_ref, r, 1).start()
            return c
        lax.fori_loop(0, tm, issue, 0)

    def compute(s):
        nxt = (s + 2) % RING
        wait_gather(s)

        @pl.when(i >= 2)
        def _():
            wait_scatter(s)

        @pl.when((i == 0) | (blk_e_ref[i] != blk_e_ref[jnp.maximum(i - 1, 0)]))
        def _():
            wu_bf[...] = wu_ref[0].astype(BF16)
            wd_bf[...] = wd_ref[0].astype(BF16)

        for c in range(SLAB):
            xbf[:, c * LANES:(c + 1) * LANES] = _load_slab_chunk(xbufs[s], tm, c, SLAB).astype(BF16)
        for r in range(tm):
            gather(tokn_ref, r, nxt).start(priority=r % 2)
            scatter(r, nxt).start(priority=r % 2)
        for n in range(n_tiles):
            gc = slice(n * FF_TILE, (n + 1) * FF_TILE)
            lc = slice(D_FF + n * FF_TILE, D_FF + (n + 1) * FF_TILE)
            g = jnp.minimum(_dot(xbf[...], wu_bf[:, gc]) + bu_ref[0, :, gc], SWIGLU_LIMIT)
            lin = jnp.clip(_dot(xbf[...], wu_bf[:, lc]) + bu_ref[0, :, lc], -SWIGLU_LIMIT, SWIGLU_LIMIT)
            actbf[:, gc] = (g * jax.nn.sigmoid(SWIGLU_ALPHA * g) * (lin + 1.0)).astype(BF16)
        for n in range(n_tiles):
            yc = slice(n * FF_TILE, (n + 1) * FF_TILE)
            y = _dot(actbf[...], wd_bf[:, yc]) + bd_ref[0, :, yc]
            for c in range(FF_TILE // LANES):
                ybufs[s][pl.ds(n * (FF_TILE // LANES) + c, tm, stride=SLAB), :] = y[:, c * LANES:(c + 1) * LANES]

    def drain(s):
        wait_gather(s)
        wait_gather((s + 1) % RING)
        wait_scatter(s)
        wait_scatter((s + 1) % RING)
        last = (s + 2) % RING

        def issue(r, c):
            scatter(r, last).start()
            return c
        lax.fori_loop(0, tm, issue, 0)
        wait_scatter(last)

    for s in range(RING):
        @pl.when((i < nused) & (i % RING == s))
        def _():
            compute(s)

        @pl.when((i == nused) & (i % RING == s))
        def _():
            drain(s)

    @pl.when(i >= nused)
    def _():
        zc = pltpu.make_async_copy(zbuf, y_hbm.at[pl.ds(pl.multiple_of((tm + i * tm) * SLAB, SLAB), tm * SLAB)], zsem)
        zc.start()
        zc.wait()


def _expert_call(blk_e, nused, tok_sorted, ret_sorted, hp_all, w_up, b_up, w_down, b_down, n_slots):
    n_blocks = blk_e.shape[0]
    tm = MOE_TM
    n_rows = n_blocks * tm + tm
    ret_tab = (jnp.concatenate([n_slots + jnp.arange(tm, dtype=jnp.int32), ret_sorted]) * SLAB
               ).reshape(n_blocks + 1, 1, tm)
    tok_tab = (tok_sorted * SLAB).reshape(n_blocks, 1, tm)
    smem_blk = functools.partial(pl.BlockSpec, (1, 1, tm), memory_space=pltpu.SMEM)
    grid_spec = pltpu.PrefetchScalarGridSpec(
        num_scalar_prefetch=2,
        grid=(n_blocks,),
        in_specs=[
            smem_blk(lambda i, be, nu: (0, 0, 0)),
            smem_blk(lambda i, be, nu: (1, 0, 0)),
            smem_blk(lambda i, be, nu: (jnp.minimum(i + 2, n_blocks - 1), 0, 0)),
            smem_blk(lambda i, be, nu: (i, 0, 0)),
            pl.BlockSpec(memory_space=pl.ANY),
            pl.BlockSpec((1, D_MODEL, 2 * D_FF), lambda i, be, nu: (be[i], 0, 0)),
            pl.BlockSpec((1, 1, 2 * D_FF), lambda i, be, nu: (be[i], 0, 0)),
            pl.BlockSpec((1, D_FF, D_MODEL), lambda i, be, nu: (be[i], 0, 0)),
            pl.BlockSpec((1, 1, D_MODEL), lambda i, be, nu: (be[i], 0, 0)),
        ],
        out_specs=pl.BlockSpec(memory_space=pl.ANY),
        scratch_shapes=[pltpu.VMEM((tm * SLAB, LANES), F32)] * (2 * RING) + [
                        pltpu.VMEM((tm, D_MODEL), BF16), pltpu.VMEM((tm, D_FF), BF16),
                        pltpu.VMEM((tm * SLAB, LANES), F32),
                        pltpu.VMEM((D_MODEL, 2 * D_FF), BF16), pltpu.VMEM((D_FF, D_MODEL), BF16),
                        pltpu.SemaphoreType.DMA((RING,)), pltpu.SemaphoreType.DMA((RING,)),
                        pltpu.SemaphoreType.DMA],
    )
    return pl.pallas_call(
        _expert_kernel,
        grid_spec=grid_spec,
        out_shape=jax.ShapeDtypeStruct((n_rows * SLAB, LANES), F32),
        compiler_params=pltpu.CompilerParams(dimension_semantics=("arbitrary",),
                                             vmem_limit_bytes=VMEM_LIMIT),
        name="experts",
    )(blk_e, nused, tok_tab, tok_tab, tok_tab, ret_tab, hp_all,
      w_up, b_up.reshape(N_EXPERTS, 1, 2 * D_FF), w_down, b_down.reshape(N_EXPERTS, 1, D_MODEL))


def _combine_kernel(ys_ref, xmid_ref, gate_ref, g_final_ref, y_ref):
    tm = xmid_ref.shape[0]
    gts = gate_ref[...]
    chunks = []
    for c in range(SLAB):
        acc = xmid_ref[:, c * LANES:(c + 1) * LANES]
        for k in range(TOP_K):
            acc = acc + _load_slab_chunk(ys_ref, tm, k * SLAB + c, TOP_K * SLAB) * gts[:, k:k + 1]
        chunks.append(acc)
    y_ref[...] = _rms(jnp.concatenate(chunks, axis=1), g_final_ref[...])


def _combine_call(ys4, row0, x_mid, gates, g_final, tm):
    T = x_mid.shape[0]
    blk0 = row0 // tm
    return pl.pallas_call(
        _combine_kernel,
        grid=(T // tm,),
        in_specs=[
            pl.BlockSpec((tm * TOP_K * SLAB, LANES), lambda i: (blk0 + i, 0)),
            pl.BlockSpec((tm, D_MODEL), lambda i: (i, 0)),
            pl.BlockSpec((tm, TOP_K), lambda i: (i, 0)),
            _full_spec((1, D_MODEL)),
        ],
        out_specs=pl.BlockSpec((tm, D_MODEL), lambda i: (i, 0)),
        out_shape=jax.ShapeDtypeStruct((T, D_MODEL), F32),
        compiler_params=pltpu.CompilerParams(dimension_semantics=("arbitrary",),
                                             vmem_limit_bytes=VMEM_LIMIT),
        name="combine",
    )(ys4, x_mid, gates, g_final)


def _t5_bucket(dist):
    n = jnp.maximum(dist, 0)
    max_exact = NUM_BUCKETS // 2
    nf = jnp.maximum(n, 1).astype(F32)
    large = max_exact + (jnp.log(nf / max_exact) / math.log(MAX_DISTANCE / max_exact)
                         * (NUM_BUCKETS - max_exact)).astype(jnp.int32)
    large = jnp.minimum(large, NUM_BUCKETS - 1)
    return jnp.where(n < max_exact, n, large)


def kernel(x_prompt, x_sample, state_gla, cache_swa_k, cache_swa_v, meta_tokens, rel_bias_table,
           g_mix, w_in, w_a_up, b_a, g_gla_out, g_swa_out, attn_sinks, w_out,
           g_ffn, w_router, b_router, w_up, b_up, w_down, b_down, g_final):
    assert g_mix.shape[0] == 1, "single-layer trunk"
    B, L, _ = x_prompt.shape
    n_seq = x_sample.shape[0]
    TP = B * L
    T_all = TP + n_seq

    wi = w_in[0]
    sizes = (GLA_QK, GLA_QK, GLA_V, GLA_V, GLA_LOWRANK, SWA_Q, SWA_KV, SWA_KV)
    offs = [0]
    for s in sizes:
        offs.append(offs[-1] + s)
    seg = [wi[:, offs[n]:offs[n + 1]] for n in range(8)]
    w_in_r = jnp.concatenate(
        seg[0:4] + seg[5:8] + [seg[4], jnp.zeros((D_MODEL, LANES - GLA_LOWRANK), F32)], axis=1).astype(BF16)
    w_a_pad = jnp.concatenate([w_a_up[0], jnp.zeros((LANES - GLA_LOWRANK, GLA_QK), F32)], axis=0).astype(BF16)
    wr_t = jnp.transpose(w_router[0])
    wr_hi = wr_t.astype(BF16)
    wr_lo = (wr_t - wr_hi.astype(F32)).astype(BF16)
    qi = jnp.arange(WINDOW)[:, None]
    kj = jnp.arange(2 * WINDOW)[None, :]
    buckets = jnp.arange(NUM_BUCKETS)
    table = rel_bias_table.astype(F32)
    oh_p = (_t5_bucket(qi - kj + WINDOW)[..., None] == buckets).astype(F32)
    bias_p = jnp.einsum("qkb,bh->hqk", oh_p, table, precision=lax.Precision.HIGHEST)
    bias_p = bias_p.reshape(SWA_KV_HEADS, SWA_GROUP * WINDOW, 2 * WINDOW)
    oh_d = (_t5_bucket(WINDOW - 1 - jnp.arange(WINDOW))[:, None] == buckets).astype(F32)
    bias_d = jnp.einsum("rb,bh->hr", oh_d, table, precision=lax.Precision.HIGHEST)
    bias_d = jnp.concatenate([bias_d, jnp.zeros((8, WINDOW), F32)], axis=0)
    wts = dict(
        sinks=attn_sinks[0].astype(F32), bias=bias_p,
        g_mix=g_mix[0][None], w_in=w_in_r, w_a_up=w_a_pad, b_a=b_a[0][None],
        g_gla=g_gla_out[0][None], g_swa=g_swa_out[0][None], w_out=w_out[0].astype(BF16),
        g_ffn=g_ffn[0][None], w_r=jnp.concatenate([wr_hi, wr_lo], axis=0), b_r=b_router[0][:, None],
    )

    x_pre = jnp.concatenate([jnp.zeros((WINDOW - N_META, D_MODEL), F32), meta_tokens.astype(F32)], axis=0)[None]
    zeros_s = jnp.zeros((GLA_QK, GLA_V), F32)
    zeros_kv = jnp.zeros((WINDOW, SWA_KV), F32)
    zeros_b = jnp.zeros((N_EXPERTS, LANES), F32)
    pre = _mixer_call(x_pre, zeros_s, zeros_kv, zeros_kv, zeros_b, wts, WINDOW, WINDOW - N_META, 0, WINDOW)
    s_meta, k_meta, v_meta = pre[5][0], pre[6][0], pre[7][0]

    (xmid_p, hp_p, topi_p, gate_p, rank_p, s_p, k_p, v_p, cnt_p) = _mixer_call(
        x_prompt, s_meta, k_meta, v_meta, zeros_b, wts, MIX_TM, 0, WINDOW - N_META, TP)

    (xmid_s, hp_s, topi_s, gate_s, rank_s, st_s, ck_s, cv_s, cnt_all) = _decode_call(
        x_sample[:, 0], state_gla[0], cache_swa_k[0].reshape(n_seq, WINDOW, SWA_KV),
        cache_swa_v[0].reshape(n_seq, WINDOW, SWA_KV), cnt_p, bias_d, wts)

    tm = MOE_TM
    n_slots = T_all * TOP_K
    n_blocks = -(-n_slots // tm) + N_EXPERTS
    top_e = jnp.concatenate([topi_p[:TOP_K], topi_s[:TOP_K]], axis=1)
    rank = jnp.concatenate([rank_p[:TOP_K], rank_s[:TOP_K]], axis=1)
    counts = cnt_all[:, 0].astype(jnp.int32)
    padded = (counts + tm - 1) // tm * tm
    pad_end = jnp.cumsum(padded)
    pad_start = pad_end - padded
    e_ids = jnp.arange(N_EXPERTS, dtype=jnp.int32)
    dest = jnp.sum(jnp.where(top_e[..., None] == e_ids, pad_start, 0), axis=-1) + rank
    slot_id = jnp.arange(T_all, dtype=jnp.int32)[None] * TOP_K + jnp.arange(TOP_K, dtype=jnp.int32)[:, None]
    n_pad = n_blocks * tm
    ret_real = jnp.full((n_pad,), -1, jnp.int32).at[dest.reshape(-1)].set(slot_id.reshape(-1))
    pos = jnp.arange(n_pad, dtype=jnp.int32)
    real_before = jnp.sum(jnp.where(pad_start[None] <= pos[:, None], counts[None], 0), axis=1)
    is_real = ret_real >= 0
    ret_sorted = jnp.where(is_real, ret_real, n_slots + tm + pos - real_before)
    tok_sorted = jnp.where(is_real, ret_real // TOP_K, 0)
    blk_e = jnp.minimum(jnp.sum(pad_end[None] <= (jnp.arange(n_blocks, dtype=jnp.int32) * tm)[:, None], axis=1),
                        N_EXPERTS - 1).astype(jnp.int32)
    nused = (pad_end[-1] // tm).astype(jnp.int32).reshape(1)

    hp_all = jnp.concatenate([hp_p, hp_s], axis=0)
    ys4 = _expert_call(blk_e, nused, tok_sorted, ret_sorted, hp_all, w_up[0], b_up[0], w_down[0], b_down[0],
                       n_slots)

    gates = jnp.transpose(jnp.concatenate([gate_p[:TOP_K], gate_s[:TOP_K]], axis=1))
    gf = g_final[None]
    y_p = _combine_call(ys4, 0, xmid_p.reshape(TP, D_MODEL), gates[:TP], gf, MIX_TM)
    y_s = _combine_call(ys4, TP, xmid_s, gates[TP:], gf, n_seq)

    s_heads = jnp.stack([s_p[:, h * GLA_DK:(h + 1) * GLA_DK, h * GLA_DV:(h + 1) * GLA_DV]
                         for h in range(GLA_HEADS)], axis=1)
    return (y_p.reshape(B, L, D_MODEL), y_s.reshape(n_seq, 1, D_MODEL), s_heads[None],
            k_p.reshape(1, B, WINDOW, SWA_KV_HEADS, SWA_HEAD_DIM),
            v_p.reshape(1, B, WINDOW, SWA_KV_HEADS, SWA_HEAD_DIM),
            st_s[None], ck_s.reshape(1, n_seq, WINDOW, SWA_KV_HEADS, SWA_HEAD_DIM),
            cv_s.reshape(1, n_seq, WINDOW, SWA_KV_HEADS, SWA_HEAD_DIM))
```

```python
import functools
import math

import jax
import jax.numpy as jnp
from jax import lax
from jax.experimental import pallas as pl
from jax.experimental.pallas import tpu as pltpu
from jax.experimental.pallas import tpu_sc as plsc

D_MODEL = 1024
N_META = 16
GLA_HEADS = 4
GLA_DK = 64
GLA_DV = 128
GLA_LOWRANK = 16
GLA_GATE_TAU = 16.0
GLA_CHUNK = 64
SWA_HEADS = 8
SWA_KV_HEADS = 2
SWA_HEAD_DIM = 64
SWA_GROUP = SWA_HEADS // SWA_KV_HEADS
WINDOW = 128
NUM_BUCKETS = 32
MAX_DISTANCE = 128
N_EXPERTS = 32
TOP_K = 4
D_FF = 1024
SWIGLU_ALPHA = 1.702
SWIGLU_LIMIT = 7.0
RMS_EPS = 1e-6

GLA_QK = GLA_HEADS * GLA_DK
GLA_V = GLA_HEADS * GLA_DV
SWA_Q = SWA_HEADS * SWA_HEAD_DIM
SWA_KV = SWA_KV_HEADS * SWA_HEAD_DIM
LANES = 128
C_GQ, C_GK, C_GV, C_GR = 0, GLA_QK, 2 * GLA_QK, 2 * GLA_QK + GLA_V
C_SQ = C_GR + GLA_V
C_SK = C_SQ + SWA_Q
C_SV = C_SK + SWA_KV
C_GA = C_SV + SWA_KV
D_PROJ = C_GA + LANES

MIX_TM = 256
MOE_TM = 256
DEC_SB = 16
VMEM_LIMIT = 56 * 1024 * 1024

F32 = jnp.float32
BF16 = jnp.bfloat16
NEG_INF = float("-inf")


def _dot(a, b):
    return jnp.dot(a, b, preferred_element_type=F32)


def _dot_nt(a, b):
    return lax.dot_general(a, b, (((1,), (1,)), ((), ())), preferred_element_type=F32)


def _split3(x):
    hi = x.astype(BF16)
    r1 = x - hi.astype(F32)
    mid = r1.astype(BF16)
    lo = (r1 - mid.astype(F32)).astype(BF16)
    return hi, mid, lo


def _rms(x, g):
    return x * lax.rsqrt(jnp.mean(x * x, axis=-1, keepdims=True) + RMS_EPS) * g


def _iota(shape, dim):
    return lax.broadcasted_iota(jnp.int32, shape, dim)


SLAB = D_MODEL // LANES


def _store_slabs(ref, x):
    rows = x.shape[0]
    for c in range(SLAB):
        ref[pl.ds(c, rows, stride=SLAB), :] = x[:, c * LANES:(c + 1) * LANES]


def _load_slab_chunk(ref, rows, first, stride):
    return ref[pl.ds(first, rows, stride=stride), :]


def _project(x, g_mix, w_in, w_a_up, b_a):
    h = _rms(x, g_mix).astype(BF16)
    proj = _dot(h, w_in)
    ga = proj[:, C_GA:C_GA + LANES].astype(BF16)
    z = _dot(ga, w_a_up) + b_a
    log_a = -(jnp.maximum(-z, 0.0) + jnp.log1p(jnp.exp(-jnp.abs(z)))) / GLA_GATE_TAU
    return dict(
        gq=proj[:, C_GQ:C_GQ + GLA_QK] * (GLA_DK ** -0.5),
        gk=proj[:, C_GK:C_GK + GLA_QK],
        gv=proj[:, C_GV:C_GV + GLA_V],
        gr=proj[:, C_GR:C_GR + GLA_V],
        sq=proj[:, C_SQ:C_SQ + SWA_Q],
        sk=proj[:, C_SK:C_SK + SWA_KV],
        sv=proj[:, C_SV:C_SV + SWA_KV],
        log_a=log_a,
    )


def _tail(x, o_gla, gr, o_swa, g_gla_out, g_swa_out, w_out, g_ffn, w_r, b_r, base):
    tm = x.shape[0]
    gate = gr * jax.nn.sigmoid(gr)
    parts = []
    for h in range(GLA_HEADS):
        sl = slice(h * GLA_DV, (h + 1) * GLA_DV)
        parts.append(_rms(o_gla[:, sl], g_gla_out) * gate[:, sl])
    parts.append(_rms(o_swa, g_swa_out))
    o = jnp.concatenate(parts, axis=1).astype(BF16)
    x_mid = x + _dot(o, w_out)
    hp = _rms(x_mid, g_ffn)

    h1 = hp.astype(BF16)
    h2 = (hp - h1.astype(F32)).astype(BF16)
    la = _dot_nt(w_r, h1)
    lb = _dot_nt(w_r[0:N_EXPERTS], h2)
    logits = la[0:N_EXPERTS] + la[N_EXPERTS:2 * N_EXPERTS] + lb + b_r

    eidx = _iota((N_EXPERTS, tm), 0)
    vals, idxs, onehots = [], [], []
    l = logits
    for _ in range(TOP_K):
        m = jnp.max(l, axis=0, keepdims=True)
        sel = jnp.min(jnp.where(l == m, eidx, N_EXPERTS), axis=0, keepdims=True)
        oh = eidx == sel
        l = jnp.where(oh, NEG_INF, l)
        vals.append(m)
        idxs.append(sel)
        onehots.append(oh)
    es = [jnp.exp(v - vals[0]) for v in vals]
    denom = es[0] + es[1] + es[2] + es[3]
    gates = [e / denom for e in es]

    ohf = jnp.concatenate([oh.astype(F32) for oh in onehots], axis=0)
    upper = (_iota((tm, tm), 0) < _iota((tm, tm), 1)).astype(BF16)
    prefix = _dot(ohf.astype(BF16), upper)
    ranks = []
    for k in range(TOP_K):
        sl = slice(k * N_EXPERTS, (k + 1) * N_EXPERTS)
        ohk = ohf[sl]
        base_t = jnp.concatenate([base] * (tm // LANES), axis=1)
        ranks.append(jnp.sum(ohk * (prefix[sl] + base_t), axis=0, keepdims=True))
        base = base + jnp.sum(ohk, axis=1, keepdims=True)
    zi = jnp.zeros((8 - TOP_K, tm), jnp.int32)
    zf = jnp.zeros((8 - TOP_K, tm), F32)
    topi = jnp.concatenate(idxs + [zi], axis=0)
    gate8 = jnp.concatenate(gates + [zf], axis=0)
    rank8 = jnp.concatenate([r.astype(jnp.int32) for r in ranks] + [zi], axis=0)
    return x_mid, hp, topi, gate8, rank8, base


def _gla_chunks(p, row0, s_bd, n_lead_pad):
    tm = p["gq"].shape[0]
    nch = tm // GLA_CHUNK
    log_a = p["log_a"]
    if n_lead_pad:
        rows = row0 + _iota((tm, GLA_QK), 0)
        log_a = jnp.where(rows >= n_lead_pad, log_a, 0.0)
    ri, ci = _iota((tm, tm), 0), _iota((tm, tm), 1)
    tril = ((ri >= ci) & (ri // GLA_CHUNK == ci // GLA_CHUNK)).astype(BF16)
    hi, mid, lo = _split3(log_a)
    b_all = _dot(tril, hi) + _dot(tril, mid) + _dot(tril, lo)

    c64 = GLA_CHUNK
    kk_mask = (_iota((GLA_QK, GLA_QK), 0) // c64) == (_iota((GLA_QK, GLA_QK), 1) // GLA_DK)
    vv_mask = (_iota((GLA_QK, GLA_V), 0) // c64) == (_iota((GLA_QK, GLA_V), 1) // GLA_DV)
    ss_mask = (_iota((GLA_QK, GLA_V), 0) // GLA_DK) == (_iota((GLA_QK, GLA_V), 1) // GLA_DV)
    causal = (_iota((c64, GLA_QK), 0) >= (_iota((c64, GLA_QK), 1) % c64)).astype(F32)
    zpad_k = jnp.zeros((LANES - c64, GLA_QK), F32)
    zpad_v = jnp.zeros((LANES - c64, GLA_V), BF16)

    outs = []
    for c in range(nch):
        rs = slice(c * c64, (c + 1) * c64)
        b = b_all[rs]
        q, k, v = p["gq"][rs], p["gk"][rs], p["gv"][rs]
        b_last = b[c64 - 1:c64]
        qt = (q * jnp.exp(b)).astype(BF16)
        kt = k * jnp.exp(-b)
        kd = k * jnp.exp(b_last - b)
        vb = v.astype(BF16)
        k_bd = jnp.where(kk_mask, jnp.concatenate([kt] * GLA_HEADS, axis=0), 0.0).astype(BF16)
        a = (_dot_nt(qt, k_bd) * causal).astype(BF16)
        v_bd = jnp.where(vv_mask, jnp.concatenate([vb] * GLA_HEADS, axis=0), jnp.zeros((), BF16))
        outs.append(_dot(qt, s_bd.astype(BF16)) + _dot(a, v_bd))
        kd_t = jnp.transpose(jnp.concatenate([kd, zpad_k], axis=0)).astype(BF16)
        upd = _dot(kd_t, jnp.concatenate([vb, zpad_v], axis=0))
        decay = jnp.exp(jnp.transpose(jnp.broadcast_to(b_last, (LANES, GLA_QK))))
        s_bd = s_bd * jnp.concatenate([decay] * GLA_HEADS, axis=1) + jnp.where(ss_mask, upd, 0.0)
    return jnp.concatenate(outs, axis=0), s_bd


def _swa_block(sq, kcat, vcat, bias_ref, sinks_ref, valid):
    half = _iota((1, LANES), 1) < SWA_HEAD_DIM
    k_roll = pltpu.roll(kcat, SWA_HEAD_DIM, 1)
    v_roll = pltpu.roll(vcat, SWA_HEAD_DIM, 1)
    cols = []
    for kv in range(SWA_KV_HEADS):
        if kv == 0:
            kk = jnp.where(half, kcat, k_roll)
            v_lo = jnp.where(half, vcat, 0.0)
            v_hi = jnp.where(half, 0.0, v_roll)
        else:
            kk = jnp.where(half, k_roll, kcat)
            v_lo = jnp.where(half, v_roll, 0.0)
            v_hi = jnp.where(half, 0.0, vcat)
        q_parts = []
        for c in (2 * kv, 2 * kv + 1):
            qc = sq[:, c * LANES:(c + 1) * LANES]
            q_parts.append(jnp.where(half, qc, 0.0))
            q_parts.append(jnp.where(half, 0.0, qc))
        q_st = jnp.concatenate(q_parts, axis=0).astype(BF16)
        s = _dot_nt(q_st, kk.astype(BF16)) * (SWA_HEAD_DIM ** -0.5) + bias_ref[kv]
        s = jnp.where(valid, s, NEG_INF)
        sink = jnp.concatenate(
            [jnp.full((WINDOW, 1), sinks_ref[kv * SWA_GROUP + g], F32) for g in range(SWA_GROUP)], axis=0)
        m = jnp.maximum(jnp.max(s, axis=1, keepdims=True), sink)
        pr = jnp.exp(s - m)
        inv = 1.0 / (jnp.sum(pr, axis=1, keepdims=True) + jnp.exp(sink - m))
        pb = pr.astype(BF16)
        p2 = jnp.concatenate([
            jnp.concatenate([pb[0:128], pb[128:256]], axis=1),
            jnp.concatenate([pb[256:384], pb[384:512]], axis=1)], axis=0)
        vv = jnp.concatenate([v_lo, v_hi], axis=0).astype(BF16)
        o2 = _dot(p2, vv)
        cols.append(o2[0:128] * jnp.where(half, inv[0:128], inv[128:256]))
        cols.append(o2[128:256] * jnp.where(half, inv[256:384], inv[384:512]))
    return jnp.concatenate(cols, axis=1)


def _mixer_kernel(sinks_ref, x_ref, s0_ref, k0_ref, v0_ref, base0_ref, bias_ref,
                  g_mix_ref, w_in_ref, w_a_up_ref, b_a_ref, g_gla_ref, g_swa_ref, w_out_ref,
                  g_ffn_ref, w_r_ref, b_r_ref,
                  xmid_ref, hp_ref, topi_ref, gate_ref, rank_ref, sout_ref, kout_ref, vout_ref, cnt_ref,
                  s_scr, k_scr, v_scr, base_scr, *, n_lead_pad, prev_valid_from):
    b_id, j = pl.program_id(0), pl.program_id(1)
    tm = x_ref.shape[1]

    @pl.when(j == 0)
    def _():
        s_scr[...] = s0_ref[...]
        k_scr[...] = k0_ref[...]
        v_scr[...] = v0_ref[...]

    @pl.when((j == 0) & (b_id == 0))
    def _():
        base_scr[...] = base0_ref[...]

    x = x_ref[0]
    p = _project(x, g_mix_ref[...], w_in_ref[...], w_a_up_ref[...], b_a_ref[...])

    o_gla, s_new = _gla_chunks(p, j * tm, s_scr[...], n_lead_pad)
    s_scr[...] = s_new

    qi = _iota((WINDOW, 2 * WINDOW), 0)
    kj = _iota((WINDOW, 2 * WINDOW), 1)
    band = (kj > qi) & (kj <= qi + WINDOW)
    o_parts = []
    for sb in range(tm // WINDOW):
        rs = slice(sb * WINDOW, (sb + 1) * WINDOW)
        k_blk, v_blk = p["sk"][rs], p["sv"][rs]
        k_prev = k_scr[...] if sb == 0 else p["sk"][(sb - 1) * WINDOW:sb * WINDOW]
        v_prev = v_scr[...] if sb == 0 else p["sv"][(sb - 1) * WINDOW:sb * WINDOW]
        valid = band
        if sb == 0 and prev_valid_from:
            first = jnp.where(j == 0, prev_valid_from, 0)
            valid = band & (kj >= first)
        valid = jnp.concatenate([valid] * SWA_GROUP, axis=0)
        o_parts.append(_swa_block(p["sq"][rs], jnp.concatenate([k_prev, k_blk], axis=0),
                                  jnp.concatenate([v_prev, v_blk], axis=0), bias_ref, sinks_ref, valid))
    o_swa = jnp.concatenate(o_parts, axis=0)
    k_scr[...] = p["sk"][tm - WINDOW:tm]
    v_scr[...] = p["sv"][tm - WINDOW:tm]

    x_mid, hp, topi, gate8, rank8, base = _tail(
        x, o_gla, p["gr"], o_swa, g_gla_ref[...], g_swa_ref[...], w_out_ref[...],
        g_ffn_ref[...], w_r_ref[...], b_r_ref[...], base_scr[...])
    base_scr[...] = base
    xmid_ref[0] = x_mid
    _store_slabs(hp_ref, hp)
    topi_ref[...] = topi
    gate_ref[...] = gate8
    rank_ref[...] = rank8
    sout_ref[0] = s_new
    kout_ref[0] = p["sk"][tm - WINDOW:tm]
    vout_ref[0] = p["sv"][tm - WINDOW:tm]
    cnt_ref[...] = base


def _full_spec(shape):
    nd = len(shape)
    return pl.BlockSpec(shape, lambda *_: (0,) * nd)


def _mixer_call(x, s0, k0, v0, base0, wts, tm, n_lead_pad, prev_valid_from, hp_rows):
    B, L, _ = x.shape
    nj = L // tm
    T = B * L
    weight_args = (wts["bias"], wts["g_mix"], wts["w_in"], wts["w_a_up"], wts["b_a"], wts["g_gla"],
                   wts["g_swa"], wts["w_out"], wts["g_ffn"], wts["w_r"], wts["b_r"])
    in_specs = [
        pl.BlockSpec(memory_space=pltpu.SMEM),
        pl.BlockSpec((1, tm, D_MODEL), lambda b, j: (b, j, 0)),
        _full_spec(s0.shape), _full_spec(k0.shape), _full_spec(v0.shape), _full_spec(base0.shape),
    ] + [_full_spec(w.shape) for w in weight_args]
    tok_spec = pl.BlockSpec((8, tm), lambda b, j: (0, b * nj + j))
    out_specs = [
        pl.BlockSpec((1, tm, D_MODEL), lambda b, j: (b, j, 0)),
        pl.BlockSpec((tm * SLAB, LANES), lambda b, j: (b * nj + j, 0)),
        tok_spec, tok_spec, tok_spec,
        pl.BlockSpec((1, GLA_QK, GLA_V), lambda b, j: (b, 0, 0)),
        pl.BlockSpec((1, WINDOW, SWA_KV), lambda b, j: (b, 0, 0)),
        pl.BlockSpec((1, WINDOW, SWA_KV), lambda b, j: (b, 0, 0)),
        _full_spec((N_EXPERTS, LANES)),
    ]
    out_shape = [
        jax.ShapeDtypeStruct((B, L, D_MODEL), F32),
        jax.ShapeDtypeStruct((hp_rows * SLAB, LANES), F32),
        jax.ShapeDtypeStruct((8, T), jnp.int32),
        jax.ShapeDtypeStruct((8, T), F32),
        jax.ShapeDtypeStruct((8, T), jnp.int32),
        jax.ShapeDtypeStruct((B, GLA_QK, GLA_V), F32),
        jax.ShapeDtypeStruct((B, WINDOW, SWA_KV), F32),
        jax.ShapeDtypeStruct((B, WINDOW, SWA_KV), F32),
        jax.ShapeDtypeStruct((N_EXPERTS, LANES), F32),
    ]
    kern = functools.partial(_mixer_kernel, n_lead_pad=n_lead_pad, prev_valid_from=prev_valid_from)
    return pl.pallas_call(
        kern,
        grid=(B, nj),
        in_specs=in_specs,
        out_specs=out_specs,
        out_shape=out_shape,
        scratch_shapes=[pltpu.VMEM((GLA_QK, GLA_V), F32), pltpu.VMEM((WINDOW, SWA_KV), F32),
                        pltpu.VMEM((WINDOW, SWA_KV), F32), pltpu.VMEM((N_EXPERTS, LANES), F32)],
        compiler_params=pltpu.CompilerParams(dimension_semantics=("arbitrary", "arbitrary"),
                                             vmem_limit_bytes=VMEM_LIMIT),
        name="mixer",
    )(wts["sinks"], x, s0, k0, v0, base0, *weight_args)


def _decode_kernel(sinks_ref, x_ref, st_ref, ck_ref, cv_ref, base0_ref, bias_ref,
                   g_mix_ref, w_in_ref, w_a_up_ref, b_a_ref, g_gla_ref, g_swa_ref, w_out_ref,
                   g_ffn_ref, w_r_ref, b_r_ref,
                   xmid_ref, hp_ref, topi_ref, gate_ref, rank_ref, sto_ref, cko_ref, cvo_ref, cnt_ref,
                   at_scr, kt_scr, qt_scr, gv_scr, gr_scr, sq_scr, sk_scr, sv_scr, og_scr, os_scr):
    i = pl.program_id(0)
    n_seq = x_ref.shape[0]

    @pl.when(i == 0)
    def _():
        p = _project(x_ref[...], g_mix_ref[...], w_in_ref[...], w_a_up_ref[...], b_a_ref[...])
        at_scr[...] = jnp.transpose(jnp.exp(p["log_a"]))
        kt_scr[...] = jnp.transpose(p["gk"])
        qt_scr[...] = jnp.transpose(p["gq"])
        gv_scr[...] = p["gv"]
        gr_scr[...] = p["gr"]
        sq_scr[...] = p["sq"]
        sk_scr[...] = p["sk"]
        sv_scr[...] = p["sv"]

    lane_seq = _iota((GLA_QK, n_seq), 1)
    half = _iota((1, LANES), 1) < SWA_HEAD_DIM
    row_id = _iota((WINDOW, SWA_KV), 0)
    head_diag = (_iota((16, SWA_Q), 1) // SWA_HEAD_DIM) == _iota((16, SWA_Q), 0)
    sink_col = jnp.concatenate(
        [jnp.full((1, 1), sinks_ref[h], F32) for h in range(SWA_HEADS)] + [jnp.zeros((8, 1), F32)], axis=0)

    def per_seq(sl, carry):
        s = i * DEC_SB + sl
        sel = lane_seq == s
        a_c = jnp.sum(jnp.where(sel, at_scr[...], 0.0), axis=1, keepdims=True)
        k_c = jnp.sum(jnp.where(sel, kt_scr[...], 0.0), axis=1, keepdims=True)
        q_c = jnp.sum(jnp.where(sel, qt_scr[...], 0.0), axis=1, keepdims=True)
        st = st_ref[sl].reshape(GLA_QK, GLA_DV)
        v_row = gv_scr[pl.ds(s, 1), :]
        v_b = jnp.concatenate(
            [jnp.broadcast_to(v_row[:, h * GLA_DV:(h + 1) * GLA_DV], (GLA_DK, GLA_DV))
             for h in range(GLA_HEADS)], axis=0)
        st_new = a_c * st + k_c * v_b
        sto_ref[sl] = st_new.reshape(GLA_HEADS, GLA_DK, GLA_DV)
        t = q_c * st_new
        og_scr[pl.ds(s, 1), :] = jnp.concatenate(
            [jnp.sum(t[h * GLA_DK:(h + 1) * GLA_DK], axis=0, keepdims=True) for h in range(GLA_HEADS)],
            axis=1)

        k_new = sk_scr[pl.ds(s, 1), :]
        v_new = sv_scr[pl.ds(s, 1), :]
        kn = jnp.where(row_id == WINDOW - 1, k_new, pltpu.roll(ck_ref[sl], WINDOW - 1, 0))
        vn = jnp.where(row_id == WINDOW - 1, v_new, pltpu.roll(cv_ref[sl], WINDOW - 1, 0))
        cko_ref[sl] = kn
        cvo_ref[sl] = vn
        kr, vr = pltpu.roll(kn, SWA_HEAD_DIM, 1), pltpu.roll(vn, SWA_HEAD_DIM, 1)
        k0, k1 = jnp.where(half, kn, kr), jnp.where(half, kr, kn)
        v0, v1 = jnp.where(half, vn, vr), jnp.where(half, vr, vn)
        kw = jnp.concatenate([k0, k0, k1, k1], axis=1).astype(BF16)
        vw = jnp.concatenate([v0, v0, v1, v1], axis=1).astype(BF16)
        q_row = sq_scr[pl.ds(s, 1), :]
        qm = jnp.where(head_diag, jnp.broadcast_to(q_row, (16, SWA_Q)), 0.0).astype(BF16)
        sc = _dot_nt(qm, kw) * (SWA_HEAD_DIM ** -0.5) + bias_ref[...]
        m = jnp.maximum(jnp.max(sc, axis=1, keepdims=True), sink_col)
        pr = jnp.exp(sc - m)
        inv = 1.0 / (jnp.sum(pr, axis=1, keepdims=True) + jnp.exp(sink_col - m))
        ow = _dot(pr.astype(BF16), vw) * inv
        os_scr[pl.ds(s, 1), :] = jnp.sum(jnp.where(head_diag, ow, 0.0), axis=0, keepdims=True)
        return carry

    lax.fori_loop(0, DEC_SB, per_seq, 0)

    @pl.when(i == pl.num_programs(0) - 1)
    def _():
        x_mid, hp, topi, gate8, rank8, base = _tail(
            x_ref[...], og_scr[...], gr_scr[...], os_scr[...], g_gla_ref[...], g_swa_ref[...],
            w_out_ref[...], g_ffn_ref[...], w_r_ref[...], b_r_ref[...], base0_ref[...])
        xmid_ref[...] = x_mid
        _store_slabs(hp_ref, hp)
        topi_ref[...] = topi
        gate_ref[...] = gate8
        rank_ref[...] = rank8
        cnt_ref[...] = base


def _decode_call(xs, state, ck, cv, base0, bias_dec, wts):
    n_seq = xs.shape[0]
    nb = n_seq // DEC_SB
    weight_args = (wts["g_mix"], wts["w_in"], wts["w_a_up"], wts["b_a"], wts["g_gla"],
                   wts["g_swa"], wts["w_out"], wts["g_ffn"], wts["w_r"], wts["b_r"])
    in_specs = [
        pl.BlockSpec(memory_space=pltpu.SMEM),
        _full_spec(xs.shape),
        pl.BlockSpec((DEC_SB, GLA_HEADS, GLA_DK, GLA_DV), lambda i: (i, 0, 0, 0)),
        pl.BlockSpec((DEC_SB, WINDOW, SWA_KV), lambda i: (i, 0, 0)),
        pl.BlockSpec((DEC_SB, WINDOW, SWA_KV), lambda i: (i, 0, 0)),
        _full_spec(base0.shape), _full_spec(bias_dec.shape),
    ] + [_full_spec(w.shape) for w in weight_args]
    out_specs = [
        _full_spec((n_seq, D_MODEL)),
        _full_spec((n_seq * SLAB, LANES)),
        _full_spec((8, n_seq)), _full_spec((8, n_seq)), _full_spec((8, n_seq)),
        pl.BlockSpec((DEC_SB, GLA_HEADS, GLA_DK, GLA_DV), lambda i: (i, 0, 0, 0)),
        pl.BlockSpec((DEC_SB, WINDOW, SWA_KV), lambda i: (i, 0, 0)),
        pl.BlockSpec((DEC_SB, WINDOW, SWA_KV), lambda i: (i, 0, 0)),
        _full_spec((N_EXPERTS, LANES)),
    ]
    out_shape = [
        jax.ShapeDtypeStruct((n_seq, D_MODEL), F32),
        jax.ShapeDtypeStruct((n_seq * SLAB, LANES), F32),
        jax.ShapeDtypeStruct((8, n_seq), jnp.int32),
        jax.ShapeDtypeStruct((8, n_seq), F32),
        jax.ShapeDtypeStruct((8, n_seq), jnp.int32),
        jax.ShapeDtypeStruct(state.shape, F32),
        jax.ShapeDtypeStruct(ck.shape, F32),
        jax.ShapeDtypeStruct(cv.shape, F32),
        jax.ShapeDtypeStruct((N_EXPERTS, LANES), F32),
    ]
    scratch = [pltpu.VMEM((GLA_QK, n_seq), F32)] * 3 + [
        pltpu.VMEM((n_seq, GLA_V), F32), pltpu.VMEM((n_seq, GLA_V), F32), pltpu.VMEM((n_seq, SWA_Q), F32),
        pltpu.VMEM((n_seq, SWA_KV), F32), pltpu.VMEM((n_seq, SWA_KV), F32),
        pltpu.VMEM((n_seq, GLA_V), F32), pltpu.VMEM((n_seq, SWA_Q), F32)]
    return pl.pallas_call(
        _decode_kernel,
        grid=(nb,),
        in_specs=in_specs,
        out_specs=out_specs,
        out_shape=out_shape,
        scratch_shapes=scratch,
        compiler_params=pltpu.CompilerParams(dimension_semantics=("arbitrary",),
                                             vmem_limit_bytes=VMEM_LIMIT),
        name="decode",
    )(wts["sinks"], xs, state, ck, cv, base0, bias_dec, *weight_args)


SC_CORES = 2
SC_SUBCORES = 16
SC_WORKERS = SC_CORES * SC_SUBCORES
SC_CHUNK = 104


def _sc_mesh():
    return plsc.VectorSubcoreMesh(core_axis_name="c", subcore_axis_name="s")


def _sc_scatter_rows(src3, idx_flat, n_out):
    n = src3.shape[0]
    per_w = n // SC_WORKERS
    n_chunks = per_w // SC_CHUNK
    assert per_w * SC_WORKERS == n and n_chunks * SC_CHUNK == per_w

    @functools.partial(
        pl.kernel, mesh=_sc_mesh(),
        out_type=jax.ShapeDtypeStruct((n_out, SLAB, LANES), F32),
        scratch_types=[pltpu.VMEM((SC_CHUNK,), jnp.int32), pltpu.VMEM((SC_CHUNK, SLAB, LANES), F32)])
    def scatter_rows(src_hbm, idx_hbm, out_hbm, idx_v, rows_v):
        wid = lax.axis_index("s") * SC_CORES + lax.axis_index("c")

        @pl.loop(0, n_chunks)
        def _(c):
            base = pl.multiple_of(wid * per_w + c * SC_CHUNK, 8)
            pltpu.sync_copy(src_hbm.at[pl.ds(base, SC_CHUNK)], rows_v)
            for k in range(TOP_K):
                pltpu.sync_copy(idx_hbm.at[pl.ds(pl.multiple_of(k * n + base, 8), SC_CHUNK)], idx_v)
                pltpu.sync_copy(rows_v, out_hbm.at[idx_v])

    return scatter_rows(src3, idx_flat)


def _sc_gather_rows(src3, idx_flat):
    n = idx_flat.shape[0]
    per_w = n // SC_WORKERS
    n_chunks = per_w // SC_CHUNK
    assert per_w * SC_WORKERS == n and n_chunks * SC_CHUNK == per_w

    @functools.partial(
        pl.kernel, mesh=_sc_mesh(),
        out_type=jax.ShapeDtypeStruct((n, SLAB, LANES), F32),
        scratch_types=[pltpu.VMEM((SC_CHUNK,), jnp.int32), pltpu.VMEM((SC_CHUNK, SLAB, LANES), F32)])
    def gather_rows(src_hbm, idx_hbm, out_hbm, idx_v, rows_v):
        wid = lax.axis_index("s") * SC_CORES + lax.axis_index("c")

        @pl.loop(0, n_chunks)
        def _(c):
            base = pl.multiple_of(wid * per_w + c * SC_CHUNK, 8)
            pltpu.sync_copy(idx_hbm.at[pl.ds(base, SC_CHUNK)], idx_v)
            pltpu.sync_copy(src_hbm.at[idx_v], rows_v)
            pltpu.sync_copy(rows_v, out_hbm.at[pl.ds(base, SC_CHUNK)])

    return gather_rows(src3, idx_flat)


def _ffn_kernel(blk_e_ref, nused_ref, x_ref, wu_ref, bu_ref, wd_ref, bd_ref, y_ref, xbf, actbf, wu_bf, wd_bf):
    i = pl.program_id(0)
    tm = MOE_TM
    n_tiles = D_FF // FF_TILE

    @pl.when(i < nused_ref[0])
    def _():
        @pl.when((i == 0) | (blk_e_ref[i] != blk_e_ref[jnp.maximum(i - 1, 0)]))
        def _():
            wu_bf[...] = wu_ref[0].astype(BF16)
            wd_bf[...] = wd_ref[0].astype(BF16)

        for c in range(SLAB):
            xbf[:, c * LANES:(c + 1) * LANES] = _load_slab_chunk(x_ref, tm, c, SLAB).astype(BF16)
        for n in range(n_tiles):
            gc = slice(n * FF_TILE, (n + 1) * FF_TILE)
            lc = slice(D_FF + n * FF_TILE, D_FF + (n + 1) * FF_TILE)
            g = jnp.minimum(_dot(xbf[...], wu_bf[:, gc]) + bu_ref[0, :, gc], SWIGLU_LIMIT)
            lin = jnp.clip(_dot(xbf[...], wu_bf[:, lc]) + bu_ref[0, :, lc], -SWIGLU_LIMIT, SWIGLU_LIMIT)
            actbf[:, gc] = (g * jax.nn.sigmoid(SWIGLU_ALPHA * g) * (lin + 1.0)).astype(BF16)
        for n in range(n_tiles):
            yc = slice(n * FF_TILE, (n + 1) * FF_TILE)
            y = _dot(actbf[...], wd_bf[:, yc]) + bd_ref[0, :, yc]
            for c in range(FF_TILE // LANES):
                y_ref[pl.ds(n * (FF_TILE // LANES) + c, tm, stride=SLAB), :] = y[:, c * LANES:(c + 1) * LANES]

    @pl.when(i >= nused_ref[0])
    def _():
        y_ref[...] = jnp.zeros_like(y_ref)


def _ffn_call(blk_e, nused, xs2, w_up, b_up, w_down, b_down):
    n_blocks = blk_e.shape[0]
    tm = MOE_TM
    row_blk = pl.BlockSpec((tm * SLAB, LANES), lambda i, be, nu: (i, 0))
    grid_spec = pltpu.PrefetchScalarGridSpec(
        num_scalar_prefetch=2,
        grid=(n_blocks,),
        in_specs=[
            row_blk,
            pl.BlockSpec((1, D_MODEL, 2 * D_FF), lambda i, be, nu: (be[i], 0, 0)),
            pl.BlockSpec((1, 1, 2 * D_FF), lambda i, be, nu: (be[i], 0, 0)),
            pl.BlockSpec((1, D_FF, D_MODEL), lambda i, be, nu: (be[i], 0, 0)),
            pl.BlockSpec((1, 1, D_MODEL), lambda i, be, nu: (be[i], 0, 0)),
        ],
        out_specs=row_blk,
        scratch_shapes=[pltpu.VMEM((tm, D_MODEL), BF16), pltpu.VMEM((tm, D_FF), BF16),
                        pltpu.VMEM((D_MODEL, 2 * D_FF), BF16), pltpu.VMEM((D_FF, D_MODEL), BF16)],
    )
    return pl.pallas_call(
        _ffn_kernel,
        grid_spec=grid_spec,
        out_shape=jax.ShapeDtypeStruct((n_blocks * tm * SLAB, LANES), F32),
        compiler_params=pltpu.CompilerParams(dimension_semantics=("arbitrary",),
                                             vmem_limit_bytes=VMEM_LIMIT),
        name="experts",
    )(blk_e, nused, xs2, w_up, b_up.reshape(N_EXPERTS, 1, 2 * D_FF), w_down, b_down.reshape(N_EXPERTS, 1, D_MODEL))


FF_TILE = 256


RING = 3


def _expert_kernel(blk_e_ref, nused_ref, tok0_ref, tok1_ref, tokn_ref, retp_ref, hp_hbm, wu_ref, bu_ref,
                   wd_ref, bd_ref, y_hbm, xbuf0, xbuf1, xbuf2, ybuf0, ybuf1, ybuf2, xbf, actbf, zbuf,
                   wu_bf, wd_bf, gsem, ssem, zsem):
    i = pl.program_id(0)
    tm = MOE_TM
    nused = nused_ref[0]
    n_tiles = D_FF // FF_TILE
    xbufs, ybufs = (xbuf0, xbuf1, xbuf2), (ybuf0, ybuf1, ybuf2)

    def gather(tok_ref, r, s):
        src = hp_hbm.at[pl.ds(pl.multiple_of(tok_ref[0, 0, r], SLAB), SLAB)]
        return pltpu.make_async_copy(src, xbufs[s].at[pl.ds(r * SLAB, SLAB)], gsem.at[s])

    def scatter(r, s):
        dst = y_hbm.at[pl.ds(pl.multiple_of(retp_ref[0, 0, r], SLAB), SLAB)]
        return pltpu.make_async_copy(ybufs[s].at[pl.ds(r * SLAB, SLAB)], dst, ssem.at[s])

    def wait_gather(s):
        pltpu.make_async_copy(hp_hbm.at[pl.ds(0, tm * SLAB)], xbufs[s], gsem.at[s]).wait()

    def wait_scatter(s):
        pltpu.make_async_copy(ybufs[s], y_hbm.at[pl.ds(0, tm * SLAB)], ssem.at[s]).wait()

    @pl.when(i == 0)
    def _():
        ybuf2[...] = jnp.zeros_like(ybuf2)
        zbuf[...] = jnp.zeros_like(zbuf)

        def issue(r, c):
            gather(tok0_ref, r, 0).start()
            gather(tok1_ref, r, 1).start()
            return c
        lax.fori_loop(0, tm, issue, 0)

    def compute(s):
        nxt = (s + 2) % RING
        wait_gather(s)

        @pl.when(i >= 2)
        def _():
            wait_scatter(s)

        @pl.when((i == 0) | (blk_e_ref[i] != blk_e_ref[jnp.maximum(i - 1, 0)]))
        def _():
            wu_bf[...] = wu_ref[0].astype(BF16)
            wd_bf[...] = wd_ref[0].astype(BF16)

        for c in range(SLAB):
            xbf[:, c * LANES:(c + 1) * LANES] = _load_slab_chunk(xbufs[s], tm, c, SLAB).astype(BF16)
        for r in range(tm):
            gather(tokn_ref, r, nxt).start(priority=r % 2)
            scatter(r, nxt).start(priority=r % 2)
        for n in range(n_tiles):
            gc = slice(n * FF_TILE, (n + 1) * FF_TILE)
            lc = slice(D_FF + n * FF_TILE, D_FF + (n + 1) * FF_TILE)
            g = jnp.minimum(_dot(xbf[...], wu_bf[:, gc]) + bu_ref[0, :, gc], SWIGLU_LIMIT)
            lin = jnp.clip(_dot(xbf[...], wu_bf[:, lc]) + bu_ref[0, :, lc], -SWIGLU_LIMIT, SWIGLU_LIMIT)
            actbf[:, gc] = (g * jax.nn.sigmoid(SWIGLU_ALPHA * g) * (lin + 1.0)).astype(BF16)
        for n in range(n_tiles):
            yc = slice(n * FF_TILE, (n + 1) * FF_TILE)
            y = _dot(actbf[...], wd_bf[:, yc]) + bd_ref[0, :, yc]
            for c in range(FF_TILE // LANES):
                ybufs[s][pl.ds(n * (FF_TILE // LANES) + c, tm, stride=SLAB), :] = y[:, c * LANES:(c + 1) * LANES]

    def drain(s):
        wait_gather(s)
        wait_gather((s + 1) % RING)
        wait_scatter(s)
        wait_scatter((s + 1) % RING)
        last = (s + 2) % RING

        def issue(r, c):
            scatter(r, last).start()
            return c
        lax.fori_loop(0, tm, issue, 0)
        wait_scatter(last)

    for s in range(RING):
        @pl.when((i < nused) & (i % RING == s))
        def _():
            compute(s)

        @pl.when((i == nused) & (i % RING == s))
        def _():
            drain(s)

    @pl.when(i >= nused)
    def _():
        zc = pltpu.make_async_copy(zbuf, y_hbm.at[pl.ds(pl.multiple_of((tm + i * tm) * SLAB, SLAB), tm * SLAB)], zsem)
        zc.start()
        zc.wait()


def _expert_call(blk_e, nused, tok_sorted, ret_sorted, hp_all, w_up, b_up, w_down, b_down, n_slots):
    n_blocks = blk_e.shape[0]
    tm = MOE_TM
    n_rows = n_blocks * tm + tm
    ret_tab = (jnp.concatenate([n_slots + jnp.arange(tm, dtype=jnp.int32), ret_sorted]) * SLAB
               ).reshape(n_blocks + 1, 1, tm)
    tok_tab = (tok_sorted * SLAB).reshape(n_blocks, 1, tm)
    smem_blk = functools.partial(pl.BlockSpec, (1, 1, tm), memory_space=pltpu.SMEM)
    grid_spec = pltpu.PrefetchScalarGridSpec(
        num_scalar_prefetch=2,
        grid=(n_blocks,),
        in_specs=[
            smem_blk(lambda i, be, nu: (0, 0, 0)),
            smem_blk(lambda i, be, nu: (1, 0, 0)),
            smem_blk(lambda i, be, nu: (jnp.minimum(i + 2, n_blocks - 1), 0, 0)),
            smem_blk(lambda i, be, nu: (i, 0, 0)),
            pl.BlockSpec(memory_space=pl.ANY),
            pl.BlockSpec((1, D_MODEL, 2 * D_FF), lambda i, be, nu: (be[i], 0, 0)),
            pl.BlockSpec((1, 1, 2 * D_FF), lambda i, be, nu: (be[i], 0, 0)),
            pl.BlockSpec((1, D_FF, D_MODEL), lambda i, be, nu: (be[i], 0, 0)),
            pl.BlockSpec((1, 1, D_MODEL), lambda i, be, nu: (be[i], 0, 0)),
        ],
        out_specs=pl.BlockSpec(memory_space=pl.ANY),
        scratch_shapes=[pltpu.VMEM((tm * SLAB, LANES), F32)] * (2 * RING) + [
                        pltpu.VMEM((tm, D_MODEL), BF16), pltpu.VMEM((tm, D_FF), BF16),
                        pltpu.VMEM((tm * SLAB, LANES), F32),
                        pltpu.VMEM((D_MODEL, 2 * D_FF), BF16), pltpu.VMEM((D_FF, D_MODEL), BF16),
                        pltpu.SemaphoreType.DMA((RING,)), pltpu.SemaphoreType.DMA((RING,)),
                        pltpu.SemaphoreType.DMA],
    )
    return pl.pallas_call(
        _expert_kernel,
        grid_spec=grid_spec,
        out_shape=jax.ShapeDtypeStruct((n_rows * SLAB, LANES), F32),
        compiler_params=pltpu.CompilerParams(dimension_semantics=("arbitrary",),
                                             vmem_limit_bytes=VMEM_LIMIT),
        name="experts",
    )(blk_e, nused, tok_tab, tok_tab, tok_tab, ret_tab, hp_all,
      w_up, b_up.reshape(N_EXPERTS, 1, 2 * D_FF), w_down, b_down.reshape(N_EXPERTS, 1, D_MODEL))


def _combine_kernel(ys_ref, xmid_ref, gate_ref, g_final_ref, y_ref):
    tm = xmid_ref.shape[0]
    gts = gate_ref[...]
    chunks = []
    for c in range(SLAB):
        acc = xmid_ref[:, c * LANES:(c + 1) * LANES]
        for k in range(TOP_K):
            acc = acc + _load_slab_chunk(ys_ref, tm, k * SLAB + c, TOP_K * SLAB) * gts[:, k:k + 1]
        chunks.append(acc)
    y_ref[...] = _rms(jnp.concatenate(chunks, axis=1), g_final_ref[...])


def _combine_call(ys4, row0, x_mid, gates, g_final, tm):
    T = x_mid.shape[0]
    blk0 = row0 // tm
    return pl.pallas_call(
        _combine_kernel,
        grid=(T // tm,),
        in_specs=[
            pl.BlockSpec((tm * TOP_K * SLAB, LANES), lambda i: (blk0 + i, 0)),
            pl.BlockSpec((tm, D_MODEL), lambda i: (i, 0)),
            pl.BlockSpec((tm, TOP_K), lambda i: (i, 0)),
            _full_spec((1, D_MODEL)),
        ],
        out_specs=pl.BlockSpec((tm, D_MODEL), lambda i: (i, 0)),
        out_shape=jax.ShapeDtypeStruct((T, D_MODEL), F32),
        compiler_params=pltpu.CompilerParams(dimension_semantics=("arbitrary",),
                                             vmem_limit_bytes=VMEM_LIMIT),
        name="combine",
    )(ys4, x_mid, gates, g_final)


def _t5_bucket(dist):
    n = jnp.maximum(dist, 0)
    max_exact = NUM_BUCKETS // 2
    nf = jnp.maximum(n, 1).astype(F32)
    large = max_exact + (jnp.log(nf / max_exact) / math.log(MAX_DISTANCE / max_exact)
                         * (NUM_BUCKETS - max_exact)).astype(jnp.int32)
    large = jnp.minimum(large, NUM_BUCKETS - 1)
    return jnp.where(n < max_exact, n, large)


def kernel(x_prompt, x_sample, state_gla, cache_swa_k, cache_swa_v, meta_tokens, rel_bias_table,
           g_mix, w_in, w_a_up, b_a, g_gla_out, g_swa_out, attn_sinks, w_out,
           g_ffn, w_router, b_router, w_up, b_up, w_down, b_down, g_final):
    assert g_mix.shape[0] == 1, "single-layer trunk"
    B, L, _ = x_prompt.shape
    n_seq = x_sample.shape[0]
    TP = B * L
    T_all = TP + n_seq

    wi = w_in[0]
    sizes = (GLA_QK, GLA_QK, GLA_V, GLA_V, GLA_LOWRANK, SWA_Q, SWA_KV, SWA_KV)
    offs = [0]
    for s in sizes:
        offs.append(offs[-1] + s)
    seg = [wi[:, offs[n]:offs[n + 1]] for n in range(8)]
    w_in_r = jnp.concatenate(
        seg[0:4] + seg[5:8] + [seg[4], jnp.zeros((D_MODEL, LANES - GLA_LOWRANK), F32)], axis=1).astype(BF16)
    w_a_pad = jnp.concatenate([w_a_up[0], jnp.zeros((LANES - GLA_LOWRANK, GLA_QK), F32)], axis=0).astype(BF16)
    wr_t = jnp.transpose(w_router[0])
    wr_hi = wr_t.astype(BF16)
    wr_lo = (wr_t - wr_hi.astype(F32)).astype(BF16)
    qi = jnp.arange(WINDOW)[:, None]
    kj = jnp.arange(2 * WINDOW)[None, :]
    buckets = jnp.arange(NUM_BUCKETS)
    table = rel_bias_table.astype(F32)
    oh_p = (_t5_bucket(qi - kj + WINDOW)[..., None] == buckets).astype(F32)
    bias_p = jnp.einsum("qkb,bh->hqk", oh_p, table, precision=lax.Precision.HIGHEST)
    bias_p = bias_p.reshape(SWA_KV_HEADS, SWA_GROUP * WINDOW, 2 * WINDOW)
    oh_d = (_t5_bucket(WINDOW - 1 - jnp.arange(WINDOW))[:, None] == buckets).astype(F32)
    bias_d = jnp.einsum("rb,bh->hr", oh_d, table, precision=lax.Precision.HIGHEST)
    bias_d = jnp.concatenate([bias_d, jnp.zeros((8, WINDOW), F32)], axis=0)
    wts = dict(
        sinks=attn_sinks[0].astype(F32), bias=bias_p,
        g_mix=g_mix[0][None], w_in=w_in_r, w_a_up=w_a_pad, b_a=b_a[0][None],
        g_gla=g_gla_out[0][None], g_swa=g_swa_out[0][None], w_out=w_out[0].astype(BF16),
        g_ffn=g_ffn[0][None], w_r=jnp.concatenate([wr_hi, wr_lo], axis=0), b_r=b_router[0][:, None],
    )

    x_pre = jnp.concatenate([jnp.zeros((WINDOW - N_META, D_MODEL), F32), meta_tokens.astype(F32)], axis=0)[None]
    zeros_s = jnp.zeros((GLA_QK, GLA_V), F32)
    zeros_kv = jnp.zeros((WINDOW, SWA_KV), F32)
    zeros_b = jnp.zeros((N_EXPERTS, LANES), F32)
    pre = _mixer_call(x_pre, zeros_s, zeros_kv, zeros_kv, zeros_b, wts, WINDOW, WINDOW - N_META, 0, WINDOW)
    s_meta, k_meta, v_meta = pre[5][0], pre[6][0], pre[7][0]

    (xmid_p, hp_p, topi_p, gate_p, rank_p, s_p, k_p, v_p, cnt_p) = _mixer_call(
        x_prompt, s_meta, k_meta, v_meta, zeros_b, wts, MIX_TM, 0, WINDOW - N_META, TP)

    (xmid_s, hp_s, topi_s, gate_s, rank_s, st_s, ck_s, cv_s, cnt_all) = _decode_call(
        x_sample[:, 0], state_gla[0], cache_swa_k[0].reshape(n_seq, WINDOW, SWA_KV),
        cache_swa_v[0].reshape(n_seq, WINDOW, SWA_KV), cnt_p, bias_d, wts)

    tm = MOE_TM
    n_slots = T_all * TOP_K
    n_blocks = -(-n_slots // tm) + N_EXPERTS
    top_e = jnp.concatenate([topi_p[:TOP_K], topi_s[:TOP_K]], axis=1)
    rank = jnp.concatenate([rank_p[:TOP_K], rank_s[:TOP_K]], axis=1)
    counts = cnt_all[:, 0].astype(jnp.int32)
    padded = (counts + tm - 1) // tm * tm
    pad_end = jnp.cumsum(padded)
    pad_start = pad_end - padded
    e_ids = jnp.arange(N_EXPERTS, dtype=jnp.int32)
    dest = jnp.sum(jnp.where(top_e[..., None] == e_ids, pad_start, 0), axis=-1) + rank
    n_pad = n_blocks * tm
    blk_e = jnp.minimum(jnp.sum(pad_end[None] <= (jnp.arange(n_blocks, dtype=jnp.int32) * tm)[:, None], axis=1),
                        N_EXPERTS - 1).astype(jnp.int32)
    nused = (pad_end[-1] // tm).astype(jnp.int32).reshape(1)

    unit = SC_WORKERS * SC_CHUNK
    t_pad = -(-T_all // unit) * unit
    extra = t_pad - T_all
    hp3 = jnp.concatenate([hp_p, hp_s, jnp.zeros((extra * SLAB, LANES), F32)], axis=0).reshape(t_pad, SLAB, LANES)
    spill = n_pad + jnp.arange(TOP_K * extra, dtype=jnp.int32).reshape(TOP_K, extra)
    dest_pad = jnp.concatenate([dest, spill], axis=1).reshape(-1)
    xs3 = _sc_scatter_rows(hp3, dest_pad, n_pad + TOP_K * extra)
    ys2 = _ffn_call(blk_e, nused, xs3.reshape(-1, LANES), w_up[0], b_up[0], w_down[0], b_down[0])
    n_slots_pad = -(-n_slots // unit) * unit
    slot_src = jnp.concatenate([jnp.transpose(dest).reshape(-1), jnp.zeros((n_slots_pad - n_slots,), jnp.int32)])
    ys4 = _sc_gather_rows(ys2.reshape(-1, SLAB, LANES), slot_src).reshape(-1, LANES)

    gates = jnp.transpose(jnp.concatenate([gate_p[:TOP_K], gate_s[:TOP_K]], axis=1))
    gf = g_final[None]
    y_p = _combine_call(ys4, 0, xmid_p.reshape(TP, D_MODEL), gates[:TP], gf, MIX_TM)
    y_s = _combine_call(ys4, TP, xmid_s, gates[TP:], gf, n_seq)

    s_heads = jnp.stack([s_p[:, h * GLA_DK:(h + 1) * GLA_DK, h * GLA_DV:(h + 1) * GLA_DV]
                         for h in range(GLA_HEADS)], axis=1)
    return (y_p.reshape(B, L, D_MODEL), y_s.reshape(n_seq, 1, D_MODEL), s_heads[None],
            k_p.reshape(1, B, WINDOW, SWA_KV_HEADS, SWA_HEAD_DIM),
            v_p.reshape(1, B, WINDOW, SWA_KV_HEADS, SWA_HEAD_DIM),
            st_s[None], ck_s.reshape(1, n_seq, WINDOW, SWA_KV_HEADS, SWA_HEAD_DIM),
            cv_s.reshape(1, n_seq, WINDOW, SWA_KV_HEADS, SWA_HEAD_DIM))
```

```python
import functools
import math

import jax
import jax.numpy as jnp
from jax import lax
from jax.experimental import pallas as pl
from jax.experimental.pallas import tpu as pltpu
from jax.experimental.pallas import tpu_sc as plsc

D_MODEL = 1024
N_META = 16
GLA_HEADS = 4
GLA_DK = 64
GLA_DV = 128
GLA_LOWRANK = 16
GLA_GATE_TAU = 16.0
GLA_CHUNK = 64
SWA_HEADS = 8
SWA_KV_HEADS = 2
SWA_HEAD_DIM = 64
SWA_GROUP = SWA_HEADS // SWA_KV_HEADS
WINDOW = 128
NUM_BUCKETS = 32
MAX_DISTANCE = 128
N_EXPERTS = 32
TOP_K = 4
D_FF = 1024
SWIGLU_ALPHA = 1.702
SWIGLU_LIMIT = 7.0
RMS_EPS = 1e-6

GLA_QK = GLA_HEADS * GLA_DK
GLA_V = GLA_HEADS * GLA_DV
SWA_Q = SWA_HEADS * SWA_HEAD_DIM
SWA_KV = SWA_KV_HEADS * SWA_HEAD_DIM
LANES = 128
C_GQ, C_GK, C_GV, C_GR = 0, GLA_QK, 2 * GLA_QK, 2 * GLA_QK + GLA_V
C_SQ = C_GR + GLA_V
C_SK = C_SQ + SWA_Q
C_SV = C_SK + SWA_KV
C_GA = C_SV + SWA_KV
D_PROJ = C_GA + LANES

MIX_TM = 256
MOE_TM = 256
DEC_SB = 16
VMEM_LIMIT = 56 * 1024 * 1024

F32 = jnp.float32
BF16 = jnp.bfloat16
NEG_INF = float("-inf")


def _dot(a, b):
    return jnp.dot(a, b, preferred_element_type=F32)


def _dot_nt(a, b):
    return lax.dot_general(a, b, (((1,), (1,)), ((), ())), preferred_element_type=F32)


def _split3(x):
    hi = x.astype(BF16)
    r1 = x - hi.astype(F32)
    mid = r1.astype(BF16)
    lo = (r1 - mid.astype(F32)).astype(BF16)
    return hi, mid, lo


def _rms(x, g):
    return x * lax.rsqrt(jnp.mean(x * x, axis=-1, keepdims=True) + RMS_EPS) * g


def _iota(shape, dim):
    return lax.broadcasted_iota(jnp.int32, shape, dim)


SLAB = D_MODEL // LANES


def _store_slabs(ref, x):
    rows = x.shape[0]
    for c in range(SLAB):
        ref[pl.ds(c, rows, stride=SLAB), :] = x[:, c * LANES:(c + 1) * LANES]


def _load_slab_chunk(ref, rows, first, stride):
    return ref[pl.ds(first, rows, stride=stride), :]


def _project(x, g_mix, w_in, w_a_up, b_a):
    h = _rms(x, g_mix).astype(BF16)
    proj = _dot(h, w_in)
    ga = proj[:, C_GA:C_GA + LANES].astype(BF16)
    z = _dot(ga, w_a_up) + b_a
    log_a = -(jnp.maximum(-z, 0.0) + jnp.log1p(jnp.exp(-jnp.abs(z)))) / GLA_GATE_TAU
    return dict(
        gq=proj[:, C_GQ:C_GQ + GLA_QK] * (GLA_DK ** -0.5),
        gk=proj[:, C_GK:C_GK + GLA_QK],
        gv=proj[:, C_GV:C_GV + GLA_V],
        gr=proj[:, C_GR:C_GR + GLA_V],
        sq=proj[:, C_SQ:C_SQ + SWA_Q],
        sk=proj[:, C_SK:C_SK + SWA_KV],
        sv=proj[:, C_SV:C_SV + SWA_KV],
        log_a=log_a,
    )


def _tail(x, o_gla, gr, o_swa, g_gla_out, g_swa_out, w_out, g_ffn, w_r, b_r, base):
    tm = x.shape[0]
    gate = gr * jax.nn.sigmoid(gr)
    parts = []
    for h in range(GLA_HEADS):
        sl = slice(h * GLA_DV, (h + 1) * GLA_DV)
        parts.append(_rms(o_gla[:, sl], g_gla_out) * gate[:, sl])
    parts.append(_rms(o_swa, g_swa_out))
    o = jnp.concatenate(parts, axis=1).astype(BF16)
    x_mid = x + _dot(o, w_out)
    hp = _rms(x_mid, g_ffn)

    h1 = hp.astype(BF16)
    h2 = (hp - h1.astype(F32)).astype(BF16)
    la = _dot_nt(w_r, h1)
    lb = _dot_nt(w_r[0:N_EXPERTS], h2)
    logits = la[0:N_EXPERTS] + la[N_EXPERTS:2 * N_EXPERTS] + lb + b_r

    eidx = _iota((N_EXPERTS, tm), 0)
    vals, idxs, onehots = [], [], []
    l = logits
    for _ in range(TOP_K):
        m = jnp.max(l, axis=0, keepdims=True)
        sel = jnp.min(jnp.where(l == m, eidx, N_EXPERTS), axis=0, keepdims=True)
        oh = eidx == sel
        l = jnp.where(oh, NEG_INF, l)
        vals.append(m)
        idxs.append(sel)
        onehots.append(oh)
    es = [jnp.exp(v - vals[0]) for v in vals]
    denom = es[0] + es[1] + es[2] + es[3]
    gates = [e / denom for e in es]

    ohf = jnp.concatenate([oh.astype(F32) for oh in onehots], axis=0)
    upper = (_iota((tm, tm), 0) < _iota((tm, tm), 1)).astype(BF16)
    prefix = _dot(ohf.astype(BF16), upper)
    ranks = []
    for k in range(TOP_K):
        sl = slice(k * N_EXPERTS, (k + 1) * N_EXPERTS)
        ohk = ohf[sl]
        base_t = jnp.concatenate([base] * (tm // LANES), axis=1)
        ranks.append(jnp.sum(ohk * (prefix[sl] + base_t), axis=0, keepdims=True))
        base = base + jnp.sum(ohk, axis=1, keepdims=True)
    zi = jnp.zeros((8 - TOP_K, tm), jnp.int32)
    zf = jnp.zeros((8 - TOP_K, tm), F32)
    topi = jnp.concatenate(idxs + [zi], axis=0)
    gate8 = jnp.concatenate(gates + [zf], axis=0)
    rank8 = jnp.concatenate([r.astype(jnp.int32) for r in ranks] + [zi], axis=0)
    return x_mid, hp, topi, gate8, rank8, base


def _gla_chunks(p, row0, s_bd, n_lead_pad):
    tm = p["gq"].shape[0]
    nch = tm // GLA_CHUNK
    log_a = p["log_a"]
    if n_lead_pad:
        rows = row0 + _iota((tm, GLA_QK), 0)
        log_a = jnp.where(rows >= n_lead_pad, log_a, 0.0)
    ri, ci = _iota((tm, tm), 0), _iota((tm, tm), 1)
    tril = ((ri >= ci) & (ri // GLA_CHUNK == ci // GLA_CHUNK)).astype(BF16)
    hi, mid, lo = _split3(log_a)
    b_all = _dot(tril, hi) + _dot(tril, mid) + _dot(tril, lo)

    c64 = GLA_CHUNK
    kk_mask = (_iota((GLA_QK, GLA_QK), 0) // c64) == (_iota((GLA_QK, GLA_QK), 1) // GLA_DK)
    vv_mask = (_iota((GLA_QK, GLA_V), 0) // c64) == (_iota((GLA_QK, GLA_V), 1) // GLA_DV)
    ss_mask = (_iota((GLA_QK, GLA_V), 0) // GLA_DK) == (_iota((GLA_QK, GLA_V), 1) // GLA_DV)
    causal = (_iota((c64, GLA_QK), 0) >= (_iota((c64, GLA_QK), 1) % c64)).astype(F32)
    zpad_k = jnp.zeros((LANES - c64, GLA_QK), F32)
    zpad_v = jnp.zeros((LANES - c64, GLA_V), BF16)

    outs = []
    for c in range(nch):
        rs = slice(c * c64, (c + 1) * c64)
        b = b_all[rs]
        q, k, v = p["gq"][rs], p["gk"][rs], p["gv"][rs]
        b_last = b[c64 - 1:c64]
        qt = (q * jnp.exp(b)).astype(BF16)
        kt = k * jnp.exp(-b)
        kd = k * jnp.exp(b_last - b)
        vb = v.astype(BF16)
        k_bd = jnp.where(kk_mask, jnp.concatenate([kt] * GLA_HEADS, axis=0), 0.0).astype(BF16)
        a = (_dot_nt(qt, k_bd) * causal).astype(BF16)
        v_bd = jnp.where(vv_mask, jnp.concatenate([vb] * GLA_HEADS, axis=0), jnp.zeros((), BF16))
        outs.append(_dot(qt, s_bd.astype(BF16)) + _dot(a, v_bd))
        kd_t = jnp.transpose(jnp.concatenate([kd, zpad_k], axis=0)).astype(BF16)
        upd = _dot(kd_t, jnp.concatenate([vb, zpad_v], axis=0))
        decay = jnp.exp(jnp.transpose(jnp.broadcast_to(b_last, (LANES, GLA_QK))))
        s_bd = s_bd * jnp.concatenate([decay] * GLA_HEADS, axis=1) + jnp.where(ss_mask, upd, 0.0)
    return jnp.concatenate(outs, axis=0), s_bd


def _swa_block(sq, kcat, vcat, bias_ref, sinks_ref, valid):
    half = _iota((1, LANES), 1) < SWA_HEAD_DIM
    k_roll = pltpu.roll(kcat, SWA_HEAD_DIM, 1)
    v_roll = pltpu.roll(vcat, SWA_HEAD_DIM, 1)
    cols = []
    for kv in range(SWA_KV_HEADS):
        if kv == 0:
            kk = jnp.where(half, kcat, k_roll)
            v_lo = jnp.where(half, vcat, 0.0)
            v_hi = jnp.where(half, 0.0, v_roll)
        else:
            kk = jnp.where(half, k_roll, kcat)
            v_lo = jnp.where(half, v_roll, 0.0)
            v_hi = jnp.where(half, 0.0, vcat)
        q_parts = []
        for c in (2 * kv, 2 * kv + 1):
            qc = sq[:, c * LANES:(c + 1) * LANES]
            q_parts.append(jnp.where(half, qc, 0.0))
            q_parts.append(jnp.where(half, 0.0, qc))
        q_st = jnp.concatenate(q_parts, axis=0).astype(BF16)
        s = _dot_nt(q_st, kk.astype(BF16)) * (SWA_HEAD_DIM ** -0.5) + bias_ref[kv]
        s = jnp.where(valid, s, NEG_INF)
        sink = jnp.concatenate(
            [jnp.full((WINDOW, 1), sinks_ref[kv * SWA_GROUP + g], F32) for g in range(SWA_GROUP)], axis=0)
        m = jnp.maximum(jnp.max(s, axis=1, keepdims=True), sink)
        pr = jnp.exp(s - m)
        inv = 1.0 / (jnp.sum(pr, axis=1, keepdims=True) + jnp.exp(sink - m))
        pb = pr.astype(BF16)
        p2 = jnp.concatenate([
            jnp.concatenate([pb[0:128], pb[128:256]], axis=1),
            jnp.concatenate([pb[256:384], pb[384:512]], axis=1)], axis=0)
        vv = jnp.concatenate([v_lo, v_hi], axis=0).astype(BF16)
        o2 = _dot(p2, vv)
        cols.append(o2[0:128] * jnp.where(half, inv[0:128], inv[128:256]))
        cols.append(o2[128:256] * jnp.where(half, inv[256:384], inv[384:512]))
    return jnp.concatenate(cols, axis=1)


def _mixer_kernel(sinks_ref, x_ref, s0_ref, k0_ref, v0_ref, base0_ref, bias_ref,
                  g_mix_ref, w_in_ref, w_a_up_ref, b_a_ref, g_gla_ref, g_swa_ref, w_out_ref,
                  g_ffn_ref, w_r_ref, b_r_ref,
                  xmid_ref, hp_ref, topi_ref, gate_ref, rank_ref, sout_ref, kout_ref, vout_ref, cnt_ref,
                  s_scr, k_scr, v_scr, base_scr, *, n_lead_pad, prev_valid_from):
    b_id, j = pl.program_id(0), pl.program_id(1)
    tm = x_ref.shape[1]

    @pl.when(j == 0)
    def _():
        s_scr[...] = s0_ref[...]
        k_scr[...] = k0_ref[...]
        v_scr[...] = v0_ref[...]

    @pl.when((j == 0) & (b_id == 0))
    def _():
        base_scr[...] = base0_ref[...]

    x = x_ref[0]
    p = _project(x, g_mix_ref[...], w_in_ref[...], w_a_up_ref[...], b_a_ref[...])

    o_gla, s_new = _gla_chunks(p, j * tm, s_scr[...], n_lead_pad)
    s_scr[...] = s_new

    qi = _iota((WINDOW, 2 * WINDOW), 0)
    kj = _iota((WINDOW, 2 * WINDOW), 1)
    band = (kj > qi) & (kj <= qi + WINDOW)
    o_parts = []
    for sb in range(tm // WINDOW):
        rs = slice(sb * WINDOW, (sb + 1) * WINDOW)
        k_blk, v_blk = p["sk"][rs], p["sv"][rs]
        k_prev = k_scr[...] if sb == 0 else p["sk"][(sb - 1) * WINDOW:sb * WINDOW]
        v_prev = v_scr[...] if sb == 0 else p["sv"][(sb - 1) * WINDOW:sb * WINDOW]
        valid = band
        if sb == 0 and prev_valid_from:
            first = jnp.where(j == 0, prev_valid_from, 0)
            valid = band & (kj >= first)
        valid = jnp.concatenate([valid] * SWA_GROUP, axis=0)
        o_parts.append(_swa_block(p["sq"][rs], jnp.concatenate([k_prev, k_blk], axis=0),
                                  jnp.concatenate([v_prev, v_blk], axis=0), bias_ref, sinks_ref, valid))
    o_swa = jnp.concatenate(o_parts, axis=0)
    k_scr[...] = p["sk"][tm - WINDOW:tm]
    v_scr[...] = p["sv"][tm - WINDOW:tm]

    x_mid, hp, topi, gate8, rank8, base = _tail(
        x, o_gla, p["gr"], o_swa, g_gla_ref[...], g_swa_ref[...], w_out_ref[...],
        g_ffn_ref[...], w_r_ref[...], b_r_ref[...], base_scr[...])
    base_scr[...] = base
    xmid_ref[0] = x_mid
    _store_slabs(hp_ref, hp)
    topi_ref[...] = topi
    gate_ref[...] = gate8
    rank_ref[...] = rank8
    sout_ref[0] = s_new
    kout_ref[0] = p["sk"][tm - WINDOW:tm]
    vout_ref[0] = p["sv"][tm - WINDOW:tm]
    cnt_ref[...] = base


def _full_spec(shape):
    nd = len(shape)
    return pl.BlockSpec(shape, lambda *_: (0,) * nd)


def _mixer_call(x, s0, k0, v0, base0, wts, tm, n_lead_pad, prev_valid_from, hp_rows):
    B, L, _ = x.shape
    nj = L // tm
    T = B * L
    weight_args = (wts["bias"], wts["g_mix"], wts["w_in"], wts["w_a_up"], wts["b_a"], wts["g_gla"],
                   wts["g_swa"], wts["w_out"], wts["g_ffn"], wts["w_r"], wts["b_r"])
    in_specs = [
        pl.BlockSpec(memory_space=pltpu.SMEM),
        pl.BlockSpec((1, tm, D_MODEL), lambda b, j: (b, j, 0)),
        _full_spec(s0.shape), _full_spec(k0.shape), _full_spec(v0.shape), _full_spec(base0.shape),
    ] + [_full_spec(w.shape) for w in weight_args]
    tok_spec = pl.BlockSpec((8, tm), lambda b, j: (0, b * nj + j))
    out_specs = [
        pl.BlockSpec((1, tm, D_MODEL), lambda b, j: (b, j, 0)),
        pl.BlockSpec((tm * SLAB, LANES), lambda b, j: (b * nj + j, 0)),
        tok_spec, tok_spec, tok_spec,
        pl.BlockSpec((1, GLA_QK, GLA_V), lambda b, j: (b, 0, 0)),
        pl.BlockSpec((1, WINDOW, SWA_KV), lambda b, j: (b, 0, 0)),
        pl.BlockSpec((1, WINDOW, SWA_KV), lambda b, j: (b, 0, 0)),
        _full_spec((N_EXPERTS, LANES)),
    ]
    out_shape = [
        jax.ShapeDtypeStruct((B, L, D_MODEL), F32),
        jax.ShapeDtypeStruct((hp_rows * SLAB, LANES), F32),
        jax.ShapeDtypeStruct((8, T), jnp.int32),
        jax.ShapeDtypeStruct((8, T), F32),
        jax.ShapeDtypeStruct((8, T), jnp.int32),
        jax.ShapeDtypeStruct((B, GLA_QK, GLA_V), F32),
        jax.ShapeDtypeStruct((B, WINDOW, SWA_KV), F32),
        jax.ShapeDtypeStruct((B, WINDOW, SWA_KV), F32),
        jax.ShapeDtypeStruct((N_EXPERTS, LANES), F32),
    ]
    kern = functools.partial(_mixer_kernel, n_lead_pad=n_lead_pad, prev_valid_from=prev_valid_from)
    return pl.pallas_call(
        kern,
        grid=(B, nj),
        in_specs=in_specs,
        out_specs=out_specs,
        out_shape=out_shape,
        scratch_shapes=[pltpu.VMEM((GLA_QK, GLA_V), F32), pltpu.VMEM((WINDOW, SWA_KV), F32),
                        pltpu.VMEM((WINDOW, SWA_KV), F32), pltpu.VMEM((N_EXPERTS, LANES), F32)],
        compiler_params=pltpu.CompilerParams(dimension_semantics=("arbitrary", "arbitrary"),
                                             vmem_limit_bytes=VMEM_LIMIT),
        name="mixer",
    )(wts["sinks"], x, s0, k0, v0, base0, *weight_args)


def _decode_kernel(sinks_ref, x_ref, st_ref, ck_ref, cv_ref, base0_ref, bias_ref,
                   g_mix_ref, w_in_ref, w_a_up_ref, b_a_ref, g_gla_ref, g_swa_ref, w_out_ref,
                   g_ffn_ref, w_r_ref, b_r_ref,
                   xmid_ref, hp_ref, topi_ref, gate_ref, rank_ref, sto_ref, cko_ref, cvo_ref, cnt_ref,
                   at_scr, kt_scr, qt_scr, gv_scr, gr_scr, sq_scr, sk_scr, sv_scr, og_scr, os_scr):
    i = pl.program_id(0)
    n_seq = x_ref.shape[0]

    @pl.when(i == 0)
    def _():
        p = _project(x_ref[...], g_mix_ref[...], w_in_ref[...], w_a_up_ref[...], b_a_ref[...])
        at_scr[...] = jnp.transpose(jnp.exp(p["log_a"]))
        kt_scr[...] = jnp.transpose(p["gk"])
        qt_scr[...] = jnp.transpose(p["gq"])
        gv_scr[...] = p["gv"]
        gr_scr[...] = p["gr"]
        sq_scr[...] = p["sq"]
        sk_scr[...] = p["sk"]
        sv_scr[...] = p["sv"]

    lane_seq = _iota((GLA_QK, n_seq), 1)
    half = _iota((1, LANES), 1) < SWA_HEAD_DIM
    row_id = _iota((WINDOW, SWA_KV), 0)
    head_diag = (_iota((16, SWA_Q), 1) // SWA_HEAD_DIM) == _iota((16, SWA_Q), 0)
    sink_col = jnp.concatenate(
        [jnp.full((1, 1), sinks_ref[h], F32) for h in range(SWA_HEADS)] + [jnp.zeros((8, 1), F32)], axis=0)

    def per_seq(sl, carry):
        s = i * DEC_SB + sl
        sel = lane_seq == s
        a_c = jnp.sum(jnp.where(sel, at_scr[...], 0.0), axis=1, keepdims=True)
        k_c = jnp.sum(jnp.where(sel, kt_scr[...], 0.0), axis=1, keepdims=True)
        q_c = jnp.sum(jnp.where(sel, qt_scr[...], 0.0), axis=1, keepdims=True)
        st = st_ref[sl].reshape(GLA_QK, GLA_DV)
        v_row = gv_scr[pl.ds(s, 1), :]
        v_b = jnp.concatenate(
            [jnp.broadcast_to(v_row[:, h * GLA_DV:(h + 1) * GLA_DV], (GLA_DK, GLA_DV))
             for h in range(GLA_HEADS)], axis=0)
        st_new = a_c * st + k_c * v_b
        sto_ref[sl] = st_new.reshape(GLA_HEADS, GLA_DK, GLA_DV)
        t = q_c * st_new
        og_scr[pl.ds(s, 1), :] = jnp.concatenate(
            [jnp.sum(t[h * GLA_DK:(h + 1) * GLA_DK], axis=0, keepdims=True) for h in range(GLA_HEADS)],
            axis=1)

        k_new = sk_scr[pl.ds(s, 1), :]
        v_new = sv_scr[pl.ds(s, 1), :]
        kn = jnp.where(row_id == WINDOW - 1, k_new, pltpu.roll(ck_ref[sl], WINDOW - 1, 0))
        vn = jnp.where(row_id == WINDOW - 1, v_new, pltpu.roll(cv_ref[sl], WINDOW - 1, 0))
        cko_ref[sl] = kn
        cvo_ref[sl] = vn
        kr, vr = pltpu.roll(kn, SWA_HEAD_DIM, 1), pltpu.roll(vn, SWA_HEAD_DIM, 1)
        k0, k1 = jnp.where(half, kn, kr), jnp.where(half, kr, kn)
        v0, v1 = jnp.where(half, vn, vr), jnp.where(half, vr, vn)
        kw = jnp.concatenate([k0, k0, k1, k1], axis=1).astype(BF16)
        vw = jnp.concatenate([v0, v0, v1, v1], axis=1).astype(BF16)
        q_row = sq_scr[pl.ds(s, 1), :]
        qm = jnp.where(head_diag, jnp.broadcast_to(q_row, (16, SWA_Q)), 0.0).astype(BF16)
        sc = _dot_nt(qm, kw) * (SWA_HEAD_DIM ** -0.5) + bias_ref[...]
        m = jnp.maximum(jnp.max(sc, axis=1, keepdims=True), sink_col)
        pr = jnp.exp(sc - m)
        inv = 1.0 / (jnp.sum(pr, axis=1, keepdims=True) + jnp.exp(sink_col - m))
        ow = _dot(pr.astype(BF16), vw) * inv
        os_scr[pl.ds(s, 1), :] = jnp.sum(jnp.where(head_diag, ow, 0.0), axis=0, keepdims=True)
        return carry

    lax.fori_loop(0, DEC_SB, per_seq, 0)

    @pl.when(i == pl.num_programs(0) - 1)
    def _():
        x_mid, hp, topi, gate8, rank8, base = _tail(
            x_ref[...], og_scr[...], gr_scr[...], os_scr[...], g_gla_ref[...], g_swa_ref[...],
            w_out_ref[...], g_ffn_ref[...], w_r_ref[...], b_r_ref[...], base0_ref[...])
        xmid_ref[...] = x_mid
        _store_slabs(hp_ref, hp)
        topi_ref[...] = topi
        gate_ref[...] = gate8
        rank_ref[...] = rank8
        cnt_ref[...] = base


def _decode_call(xs, state, ck, cv, base0, bias_dec, wts):
    n_seq = xs.shape[0]
    nb = n_seq // DEC_SB
    weight_args = (wts["g_mix"], wts["w_in"], wts["w_a_up"], wts["b_a"], wts["g_gla"],
                   wts["g_swa"], wts["w_out"], wts["g_ffn"], wts["w_r"], wts["b_r"])
    in_specs = [
        pl.BlockSpec(memory_space=pltpu.SMEM),
        _full_spec(xs.shape),
        pl.BlockSpec((DEC_SB, GLA_HEADS, GLA_DK, GLA_DV), lambda i: (i, 0, 0, 0)),
        pl.BlockSpec((DEC_SB, WINDOW, SWA_KV), lambda i: (i, 0, 0)),
        pl.BlockSpec((DEC_SB, WINDOW, SWA_KV), lambda i: (i, 0, 0)),
        _full_spec(base0.shape), _full_spec(bias_dec.shape),
    ] + [_full_spec(w.shape) for w in weight_args]
    out_specs = [
        _full_spec((n_seq, D_MODEL)),
        _full_spec((n_seq * SLAB, LANES)),
        _full_spec((8, n_seq)), _full_spec((8, n_seq)), _full_spec((8, n_seq)),
        pl.BlockSpec((DEC_SB, GLA_HEADS, GLA_DK, GLA_DV), lambda i: (i, 0, 0, 0)),
        pl.BlockSpec((DEC_SB, WINDOW, SWA_KV), lambda i: (i, 0, 0)),
        pl.BlockSpec((DEC_SB, WINDOW, SWA_KV), lambda i: (i, 0, 0)),
        _full_spec((N_EXPERTS, LANES)),
    ]
    out_shape = [
        jax.ShapeDtypeStruct((n_seq, D_MODEL), F32),
        jax.ShapeDtypeStruct((n_seq * SLAB, LANES), F32),
        jax.ShapeDtypeStruct((8, n_seq), jnp.int32),
        jax.ShapeDtypeStruct((8, n_seq), F32),
        jax.ShapeDtypeStruct((8, n_seq), jnp.int32),
        jax.ShapeDtypeStruct(state.shape, F32),
        jax.ShapeDtypeStruct(ck.shape, F32),
        jax.ShapeDtypeStruct(cv.shape, F32),
        jax.ShapeDtypeStruct((N_EXPERTS, LANES), F32),
    ]
    scratch = [pltpu.VMEM((GLA_QK, n_seq), F32)] * 3 + [
        pltpu.VMEM((n_seq, GLA_V), F32), pltpu.VMEM((n_seq, GLA_V), F32), pltpu.VMEM((n_seq, SWA_Q), F32),
        pltpu.VMEM((n_seq, SWA_KV), F32), pltpu.VMEM((n_seq, SWA_KV), F32),
        pltpu.VMEM((n_seq, GLA_V), F32), pltpu.VMEM((n_seq, SWA_Q), F32)]
    return pl.pallas_call(
        _decode_kernel,
        grid=(nb,),
        in_specs=in_specs,
        out_specs=out_specs,
        out_shape=out_shape,
        scratch_shapes=scratch,
        compiler_params=pltpu.CompilerParams(dimension_semantics=("arbitrary",),
                                             vmem_limit_bytes=VMEM_LIMIT),
        name="decode",
    )(wts["sinks"], xs, state, ck, cv, base0, bias_dec, *weight_args)


SC_CORES = 2
SC_SUBCORES = 16
SC_WORKERS = SC_CORES * SC_SUBCORES
SC_SCATTER_ROWS = 32
SC_GATHER_ROWS = 24


def _sc_mesh():
    return plsc.VectorSubcoreMesh(core_axis_name="c", subcore_axis_name="s")


def _sc_worker_id():
    return lax.axis_index("s") * SC_CORES + lax.axis_index("c")


def _sc_scatter_rows(src_p, src_s, idx_p, idx_s, n_out):
    rows = SC_SCATTER_ROWS
    n_chunks = idx_p.shape[0] // SC_WORKERS
    n_s, _, rows_s = idx_s.shape
    assert n_chunks * SC_WORKERS == idx_p.shape[0] and n_chunks % 2 == 0 and n_s <= SC_WORKERS

    @functools.partial(
        pl.kernel, mesh=_sc_mesh(),
        out_type=jax.ShapeDtypeStruct((n_out, SLAB, LANES), F32),
        scratch_types=[pltpu.VMEM((2, TOP_K, rows), jnp.int32), pltpu.VMEM((2, rows, SLAB, LANES), F32),
                       pltpu.VMEM((TOP_K, rows_s), jnp.int32), pltpu.VMEM((rows_s, SLAB, LANES), F32),
                       pltpu.SemaphoreType.DMA((2,)), pltpu.SemaphoreType.DMA((2,))])
    def scatter_rows(srcp_hbm, srcs_hbm, idxp_hbm, idxs_hbm, out_hbm, idx_v, rows_v, idxs_v, rowss_v, lsem, ssem):
        wid = _sc_worker_id()

        def loads(c, b):
            g = wid * n_chunks + c
            return (pltpu.make_async_copy(idxp_hbm.at[g], idx_v.at[b], lsem.at[b]),
                    pltpu.make_async_copy(srcp_hbm.at[pl.ds(pl.multiple_of(g * rows, 8), rows)], rows_v.at[b],
                                          lsem.at[b]))

        def scatters(b):
            return [pltpu.make_async_copy(rows_v.at[b], out_hbm.at[idx_v.at[b, k]], ssem.at[b])
                    for k in range(TOP_K)]

        for d in loads(0, 0):
            d.start()

        @pl.loop(0, n_chunks, step=2)
        def _(c0):
            for b in range(2):
                c = c0 + b
                for d in loads(c, b):
                    d.wait()

                @pl.when(c >= 1)
                def _():
                    for d in scatters(1 - b):
                        d.wait()

                @pl.when(c + 1 < n_chunks)
                def _():
                    for d in loads(c + 1, 1 - b):
                        d.start()

                for d in scatters(b):
                    d.start()

        for d in scatters((n_chunks - 1) % 2):
            d.wait()

        @pl.when(wid < n_s)
        def _():
            pltpu.sync_copy(idxs_hbm.at[wid], idxs_v)
            pltpu.sync_copy(srcs_hbm.at[pl.ds(pl.multiple_of(wid * rows_s, 8), rows_s)], rowss_v)
            for k in range(TOP_K):
                pltpu.sync_copy(rowss_v, out_hbm.at[idxs_v.at[k]])

    return scatter_rows(src_p, src_s, idx_p, idx_s)


def _sc_gather_rows(src3, idx2):
    rows = SC_GATHER_ROWS
    n_chunks = idx2.shape[0] // SC_WORKERS
    assert n_chunks * SC_WORKERS == idx2.shape[0] and idx2.shape[1] == rows and n_chunks % 2 == 0

    @functools.partial(
        pl.kernel, mesh=_sc_mesh(),
        out_type=jax.ShapeDtypeStruct((idx2.shape[0] * rows, SLAB, LANES), F32),
        scratch_types=[pltpu.VMEM((2, rows), jnp.int32), pltpu.VMEM((2, rows, SLAB, LANES), F32),
                       pltpu.SemaphoreType.DMA((2,)), pltpu.SemaphoreType.DMA((2,))])
    def gather_rows(src_hbm, idx_hbm, out_hbm, idx_v, rows_v, gsem, wsem):
        wid = _sc_worker_id()

        def gather(b):
            return pltpu.make_async_copy(src_hbm.at[idx_v.at[b]], rows_v.at[b], gsem.at[b])

        def write(c, b):
            base = pl.multiple_of((wid * n_chunks + c) * rows, 8)
            return pltpu.make_async_copy(rows_v.at[b], out_hbm.at[pl.ds(base, rows)], wsem.at[b])

        pltpu.sync_copy(idx_hbm.at[wid * n_chunks], idx_v.at[0])
        gather(0).start()

        @pl.loop(0, n_chunks, step=2)
        def _(c0):
            for b in range(2):
                c = c0 + b

                @pl.when(c + 1 < n_chunks)
                def _():
                    @pl.when(c >= 1)
                    def _():
                        write(c - 1, 1 - b).wait()
                    pltpu.sync_copy(idx_hbm.at[wid * n_chunks + c + 1], idx_v.at[1 - b])
                    gather(1 - b).start()

                gather(b).wait()
                write(c, b).start()

        write(n_chunks - 2, 0).wait()
        write(n_chunks - 1, 1).wait()

    return gather_rows(src3, idx2)


def _ffn_kernel(blk_e_ref, nused_ref, x_ref, wu_ref, bu_ref, wd_ref, bd_ref, y_ref, xbf, actbf, wu_bf, wd_bf):
    i = pl.program_id(0)
    tm = MOE_TM
    n_tiles = D_FF // FF_TILE

    @pl.when(i < nused_ref[0])
    def _():
        @pl.when((i == 0) | (blk_e_ref[i] != blk_e_ref[jnp.maximum(i - 1, 0)]))
        def _():
            wu_bf[...] = wu_ref[0].astype(BF16)
            wd_bf[...] = wd_ref[0].astype(BF16)

        for c in range(SLAB):
            xbf[:, c * LANES:(c + 1) * LANES] = _load_slab_chunk(x_ref, tm, c, SLAB).astype(BF16)
        for n in range(n_tiles):
            gc = slice(n * FF_TILE, (n + 1) * FF_TILE)
            lc = slice(D_FF + n * FF_TILE, D_FF + (n + 1) * FF_TILE)
            g = jnp.minimum(_dot(xbf[...], wu_bf[:, gc]) + bu_ref[0, :, gc], SWIGLU_LIMIT)
            lin = jnp.clip(_dot(xbf[...], wu_bf[:, lc]) + bu_ref[0, :, lc], -SWIGLU_LIMIT, SWIGLU_LIMIT)
            actbf[:, gc] = (g * jax.nn.sigmoid(SWIGLU_ALPHA * g) * (lin + 1.0)).astype(BF16)
        for n in range(n_tiles):
            yc = slice(n * FF_TILE, (n + 1) * FF_TILE)
            y = _dot(actbf[...], wd_bf[:, yc]) + bd_ref[0, :, yc]
            for c in range(FF_TILE // LANES):
                y_ref[pl.ds(n * (FF_TILE // LANES) + c, tm, stride=SLAB), :] = y[:, c * LANES:(c + 1) * LANES]

    @pl.when(i >= nused_ref[0])
    def _():
        y_ref[...] = jnp.zeros_like(y_ref)


def _ffn_call(blk_e, nused, xs2, w_up, b_up, w_down, b_down):
    n_blocks = blk_e.shape[0]
    tm = MOE_TM
    row_blk = pl.BlockSpec((tm * SLAB, LANES), lambda i, be, nu: (i, 0))
    grid_spec = pltpu.PrefetchScalarGridSpec(
        num_scalar_prefetch=2,
        grid=(n_blocks,),
        in_specs=[
            row_blk,
            pl.BlockSpec((1, D_MODEL, 2 * D_FF), lambda i, be, nu: (be[i], 0, 0)),
            pl.BlockSpec((1, 1, 2 * D_FF), lambda i, be, nu: (be[i], 0, 0)),
            pl.BlockSpec((1, D_FF, D_MODEL), lambda i, be, nu: (be[i], 0, 0)),
            pl.BlockSpec((1, 1, D_MODEL), lambda i, be, nu: (be[i], 0, 0)),
        ],
        out_specs=row_blk,
        scratch_shapes=[pltpu.VMEM((tm, D_MODEL), BF16), pltpu.VMEM((tm, D_FF), BF16),
                        pltpu.VMEM((D_MODEL, 2 * D_FF), BF16), pltpu.VMEM((D_FF, D_MODEL), BF16)],
    )
    return pl.pallas_call(
        _ffn_kernel,
        grid_spec=grid_spec,
        out_shape=jax.ShapeDtypeStruct((n_blocks * tm * SLAB, LANES), F32),
        compiler_params=pltpu.CompilerParams(dimension_semantics=("arbitrary",),
                                             vmem_limit_bytes=VMEM_LIMIT),
        name="experts",
    )(blk_e, nused, xs2, w_up, b_up.reshape(N_EXPERTS, 1, 2 * D_FF), w_down, b_down.reshape(N_EXPERTS, 1, D_MODEL))


FF_TILE = 256


RING = 3


def _expert_kernel(blk_e_ref, nused_ref, tok0_ref, tok1_ref, tokn_ref, retp_ref, hp_hbm, wu_ref, bu_ref,
                   wd_ref, bd_ref, y_hbm, xbuf0, xbuf1, xbuf2, ybuf0, ybuf1, ybuf2, xbf, actbf, zbuf,
                   wu_bf, wd_bf, gsem, ssem, zsem):
    i = pl.program_id(0)
    tm = MOE_TM
    nused = nused_ref[0]
    n_tiles = D_FF // FF_TILE
    xbufs, ybufs = (xbuf0, xbuf1, xbuf2), (ybuf0, ybuf1, ybuf2)

    def gather(tok_ref, r, s):
        src = hp_hbm.at[pl.ds(pl.multiple_of(tok_ref[0, 0, r], SLAB), SLAB)]
        return pltpu.make_async_copy(src, xbufs[s].at[pl.ds(r * SLAB, SLAB)], gsem.at[s])

    def scatter(r, s):
        dst = y_hbm.at[pl.ds(pl.multiple_of(retp_ref[0, 0, r], SLAB), SLAB)]
        return pltpu.make_async_copy(ybufs[s].at[pl.ds(r * SLAB, SLAB)], dst, ssem.at[s])

    def wait_gather(s):
        pltpu.make_async_copy(hp_hbm.at[pl.ds(0, tm * SLAB)], xbufs[s], gsem.at[s]).wait()

    def wait_scatter(s):
        pltpu.make_async_copy(ybufs[s], y_hbm.at[pl.ds(0, tm * SLAB)], ssem.at[s]).wait()

    @pl.when(i == 0)
    def _():
        ybuf2[...] = jnp.zeros_like(ybuf2)
        zbuf[...] = jnp.zeros_like(zbuf)

        def issue(r, c):
            gather(tok0_ref, r, 0).start()
            gather(tok1_ref, r, 1).start()
            return c
        lax.fori_loop(0, tm, issue, 0)

    def compute(s):
        nxt = (s + 2) % RING
        wait_gather(s)

        @pl.when(i >= 2)
        def _():
            wait_scatter(s)

        @pl.when((i == 0) | (blk_e_ref[i] != blk_e_ref[jnp.maximum(i - 1, 0)]))
        def _():
            wu_bf[...] = wu_ref[0].astype(BF16)
            wd_bf[...] = wd_ref[0].astype(BF16)

        for c in range(SLAB):
            xbf[:, c * LANES:(c + 1) * LANES] = _load_slab_chunk(xbufs[s], tm, c, SLAB).astype(BF16)
        for r in range(tm):
            gather(tokn_ref, r, nxt).start(priority=r % 2)
            scatter(r, nxt).start(priority=r % 2)
        for n in range(n_tiles):
            gc = slice(n * FF_TILE, (n + 1) * FF_TILE)
            lc = slice(D_FF + n * FF_TILE, D_FF + (n + 1) * FF_TILE)
            g = jnp.minimum(_dot(xbf[...], wu_bf[:, gc]) + bu_ref[0, :, gc], SWIGLU_LIMIT)
            lin = jnp.clip(_dot(xbf[...], wu_bf[:, lc]) + bu_ref[0, :, lc], -SWIGLU_LIMIT, SWIGLU_LIMIT)
            actbf[:, gc] = (g * jax.nn.sigmoid(SWIGLU_ALPHA * g) * (lin + 1.0)).astype(BF16)
        for n in range(n_tiles):
            yc = slice(n * FF_TILE, (n + 1) * FF_TILE)
            y = _dot(actbf[...], wd_bf[:, yc]) + bd_ref[0, :, yc]
            for c in range(FF_TILE // LANES):
                ybufs[s][pl.ds(n * (FF_TILE // LANES) + c, tm, stride=SLAB), :] = y[:, c * LANES:(c + 1) * LANES]

    def drain(s):
        wait_gather(s)
        wait_gather((s + 1) % RING)
        wait_scatter(s)
        wait_scatter((s + 1) % RING)
        last = (s + 2) % RING

        def issue(r, c):
            scatter(r, last).start()
            return c
        lax.fori_loop(0, tm, issue, 0)
        wait_scatter(last)

    for s in range(RING):
        @pl.when((i < nused) & (i % RING == s))
        def _():
            compute(s)

        @pl.when((i == nused) & (i % RING == s))
        def _():
            drain(s)

    @pl.when(i >= nused)
    def _():
        zc = pltpu.make_async_copy(zbuf, y_hbm.at[pl.ds(pl.multiple_of((tm + i * tm) * SLAB, SLAB), tm * SLAB)], zsem)
        zc.start()
        zc.wait()


def _expert_call(blk_e, nused, tok_sorted, ret_sorted, hp_all, w_up, b_up, w_down, b_down, n_slots):
    n_blocks = blk_e.shape[0]
    tm = MOE_TM
    n_rows = n_blocks * tm + tm
    ret_tab = (jnp.concatenate([n_slots + jnp.arange(tm, dtype=jnp.int32), ret_sorted]) * SLAB
               ).reshape(n_blocks + 1, 1, tm)
    tok_tab = (tok_sorted * SLAB).reshape(n_blocks, 1, tm)
    smem_blk = functools.partial(pl.BlockSpec, (1, 1, tm), memory_space=pltpu.SMEM)
    grid_spec = pltpu.PrefetchScalarGridSpec(
        num_scalar_prefetch=2,
        grid=(n_blocks,),
        in_specs=[
            smem_blk(lambda i, be, nu: (0, 0, 0)),
            smem_blk(lambda i, be, nu: (1, 0, 0)),
            smem_blk(lambda i, be, nu: (jnp.minimum(i + 2, n_blocks - 1), 0, 0)),
            smem_blk(lambda i, be, nu: (i, 0, 0)),
            pl.BlockSpec(memory_space=pl.ANY),
            pl.BlockSpec((1, D_MODEL, 2 * D_FF), lambda i, be, nu: (be[i], 0, 0)),
            pl.BlockSpec((1, 1, 2 * D_FF), lambda i, be, nu: (be[i], 0, 0)),
            pl.BlockSpec((1, D_FF, D_MODEL), lambda i, be, nu: (be[i], 0, 0)),
            pl.BlockSpec((1, 1, D_MODEL), lambda i, be, nu: (be[i], 0, 0)),
        ],
        out_specs=pl.BlockSpec(memory_space=pl.ANY),
        scratch_shapes=[pltpu.VMEM((tm * SLAB, LANES), F32)] * (2 * RING) + [
                        pltpu.VMEM((tm, D_MODEL), BF16), pltpu.VMEM((tm, D_FF), BF16),
                        pltpu.VMEM((tm * SLAB, LANES), F32),
                        pltpu.VMEM((D_MODEL, 2 * D_FF), BF16), pltpu.VMEM((D_FF, D_MODEL), BF16),
                        pltpu.SemaphoreType.DMA((RING,)), pltpu.SemaphoreType.DMA((RING,)),
                        pltpu.SemaphoreType.DMA],
    )
    return pl.pallas_call(
        _expert_kernel,
        grid_spec=grid_spec,
        out_shape=jax.ShapeDtypeStruct((n_rows * SLAB, LANES), F32),
        compiler_params=pltpu.CompilerParams(dimension_semantics=("arbitrary",),
                                             vmem_limit_bytes=VMEM_LIMIT),
        name="experts",
    )(blk_e, nused, tok_tab, tok_tab, tok_tab, ret_tab, hp_all,
      w_up, b_up.reshape(N_EXPERTS, 1, 2 * D_FF), w_down, b_down.reshape(N_EXPERTS, 1, D_MODEL))


def _combine_kernel(ys_ref, xmid_ref, gate_ref, g_final_ref, y_ref):
    tm = xmid_ref.shape[0]
    gts = gate_ref[...]
    chunks = []
    for c in range(SLAB):
        acc = xmid_ref[:, c * LANES:(c + 1) * LANES]
        for k in range(TOP_K):
            acc = acc + _load_slab_chunk(ys_ref, tm, k * SLAB + c, TOP_K * SLAB) * gts[:, k:k + 1]
        chunks.append(acc)
    y_ref[...] = _rms(jnp.concatenate(chunks, axis=1), g_final_ref[...])


def _combine_call(ys4, row0, x_mid, gates, g_final, tm):
    T = x_mid.shape[0]
    blk0 = row0 // tm
    return pl.pallas_call(
        _combine_kernel,
        grid=(T // tm,),
        in_specs=[
            pl.BlockSpec((tm * TOP_K * SLAB, LANES), lambda i: (blk0 + i, 0)),
            pl.BlockSpec((tm, D_MODEL), lambda i: (i, 0)),
            pl.BlockSpec((tm, TOP_K), lambda i: (i, 0)),
            _full_spec((1, D_MODEL)),
        ],
        out_specs=pl.BlockSpec((tm, D_MODEL), lambda i: (i, 0)),
        out_shape=jax.ShapeDtypeStruct((T, D_MODEL), F32),
        compiler_params=pltpu.CompilerParams(dimension_semantics=("arbitrary",),
                                             vmem_limit_bytes=VMEM_LIMIT),
        name="combine",
    )(ys4, x_mid, gates, g_final)


def _t5_bucket(dist):
    n = jnp.maximum(dist, 0)
    max_exact = NUM_BUCKETS // 2
    nf = jnp.maximum(n, 1).astype(F32)
    large = max_exact + (jnp.log(nf / max_exact) / math.log(MAX_DISTANCE / max_exact)
                         * (NUM_BUCKETS - max_exact)).astype(jnp.int32)
    large = jnp.minimum(large, NUM_BUCKETS - 1)
    return jnp.where(n < max_exact, n, large)


def kernel(x_prompt, x_sample, state_gla, cache_swa_k, cache_swa_v, meta_tokens, rel_bias_table,
           g_mix, w_in, w_a_up, b_a, g_gla_out, g_swa_out, attn_sinks, w_out,
           g_ffn, w_router, b_router, w_up, b_up, w_down, b_down, g_final):
    assert g_mix.shape[0] == 1, "single-layer trunk"
    B, L, _ = x_prompt.shape
    n_seq = x_sample.shape[0]
    TP = B * L
    T_all = TP + n_seq

    wi = w_in[0]
    sizes = (GLA_QK, GLA_QK, GLA_V, GLA_V, GLA_LOWRANK, SWA_Q, SWA_KV, SWA_KV)
    offs = [0]
    for s in sizes:
        offs.append(offs[-1] + s)
    seg = [wi[:, offs[n]:offs[n + 1]] for n in range(8)]
    w_in_r = jnp.concatenate(
        seg[0:4] + seg[5:8] + [seg[4], jnp.zeros((D_MODEL, LANES - GLA_LOWRANK), F32)], axis=1).astype(BF16)
    w_a_pad = jnp.concatenate([w_a_up[0], jnp.zeros((LANES - GLA_LOWRANK, GLA_QK), F32)], axis=0).astype(BF16)
    wr_t = jnp.transpose(w_router[0])
    wr_hi = wr_t.astype(BF16)
    wr_lo = (wr_t - wr_hi.astype(F32)).astype(BF16)
    qi = jnp.arange(WINDOW)[:, None]
    kj = jnp.arange(2 * WINDOW)[None, :]
    buckets = jnp.arange(NUM_BUCKETS)
    table = rel_bias_table.astype(F32)
    oh_p = (_t5_bucket(qi - kj + WINDOW)[..., None] == buckets).astype(F32)
    bias_p = jnp.einsum("qkb,bh->hqk", oh_p, table, precision=lax.Precision.HIGHEST)
    bias_p = bias_p.reshape(SWA_KV_HEADS, SWA_GROUP * WINDOW, 2 * WINDOW)
    oh_d = (_t5_bucket(WINDOW - 1 - jnp.arange(WINDOW))[:, None] == buckets).astype(F32)
    bias_d = jnp.einsum("rb,bh->hr", oh_d, table, precision=lax.Precision.HIGHEST)
    bias_d = jnp.concatenate([bias_d, jnp.zeros((8, WINDOW), F32)], axis=0)
    wts = dict(
        sinks=attn_sinks[0].astype(F32), bias=bias_p,
        g_mix=g_mix[0][None], w_in=w_in_r, w_a_up=w_a_pad, b_a=b_a[0][None],
        g_gla=g_gla_out[0][None], g_swa=g_swa_out[0][None], w_out=w_out[0].astype(BF16),
        g_ffn=g_ffn[0][None], w_r=jnp.concatenate([wr_hi, wr_lo], axis=0), b_r=b_router[0][:, None],
    )

    x_pre = jnp.concatenate([jnp.zeros((WINDOW - N_META, D_MODEL), F32), meta_tokens.astype(F32)], axis=0)[None]
    zeros_s = jnp.zeros((GLA_QK, GLA_V), F32)
    zeros_kv = jnp.zeros((WINDOW, SWA_KV), F32)
    zeros_b = jnp.zeros((N_EXPERTS, LANES), F32)
    pre = _mixer_call(x_pre, zeros_s, zeros_kv, zeros_kv, zeros_b, wts, WINDOW, WINDOW - N_META, 0, WINDOW)
    s_meta, k_meta, v_meta = pre[5][0], pre[6][0], pre[7][0]

    (xmid_p, hp_p, topi_p, gate_p, rank_p, s_p, k_p, v_p, cnt_p) = _mixer_call(
        x_prompt, s_meta, k_meta, v_meta, zeros_b, wts, MIX_TM, 0, WINDOW - N_META, TP)

    (xmid_s, hp_s, topi_s, gate_s, rank_s, st_s, ck_s, cv_s, cnt_all) = _decode_call(
        x_sample[:, 0], state_gla[0], cache_swa_k[0].reshape(n_seq, WINDOW, SWA_KV),
        cache_swa_v[0].reshape(n_seq, WINDOW, SWA_KV), cnt_p, bias_d, wts)

    tm = MOE_TM
    n_slots = T_all * TOP_K
    n_blocks = -(-n_slots // tm) + N_EXPERTS
    top_e = jnp.concatenate([topi_p[:TOP_K], topi_s[:TOP_K]], axis=1)
    rank = jnp.concatenate([rank_p[:TOP_K], rank_s[:TOP_K]], axis=1)
    counts = cnt_all[:, 0].astype(jnp.int32)
    padded = (counts + tm - 1) // tm * tm
    pad_end = jnp.cumsum(padded)
    pad_start = pad_end - padded
    e_ids = jnp.arange(N_EXPERTS, dtype=jnp.int32)
    dest = jnp.sum(jnp.where(top_e[..., None] == e_ids, pad_start, 0), axis=-1) + rank
    n_pad = n_blocks * tm
    blk_e = jnp.minimum(jnp.sum(pad_end[None] <= (jnp.arange(n_blocks, dtype=jnp.int32) * tm)[:, None], axis=1),
                        N_EXPERTS - 1).astype(jnp.int32)
    nused = (pad_end[-1] // tm).astype(jnp.int32).reshape(1)

    sample_rows = 8
    idx_p = dest[:, :TP].reshape(TOP_K, TP // SC_SCATTER_ROWS, SC_SCATTER_ROWS).transpose(1, 0, 2)
    idx_s = dest[:, TP:].reshape(TOP_K, n_seq // sample_rows, sample_rows).transpose(1, 0, 2)
    xs3 = _sc_scatter_rows(hp_p.reshape(TP, SLAB, LANES), hp_s.reshape(n_seq, SLAB, LANES), idx_p, idx_s, n_pad)
    ys2 = _ffn_call(blk_e, nused, xs3.reshape(-1, LANES), w_up[0], b_up[0], w_down[0], b_down[0])
    slot_src = jnp.transpose(dest).reshape(n_slots // SC_GATHER_ROWS, SC_GATHER_ROWS)
    ys4 = _sc_gather_rows(ys2.reshape(-1, SLAB, LANES), slot_src).reshape(-1, LANES)

    gates = jnp.transpose(jnp.concatenate([gate_p[:TOP_K], gate_s[:TOP_K]], axis=1))
    gf = g_final[None]
    y_p = _combine_call(ys4, 0, xmid_p.reshape(TP, D_MODEL), gates[:TP], gf, MIX_TM)
    y_s = _combine_call(ys4, TP, xmid_s, gates[TP:], gf, n_seq)

    s_heads = jnp.stack([s_p[:, h * GLA_DK:(h + 1) * GLA_DK, h * GLA_DV:(h + 1) * GLA_DV]
                         for h in range(GLA_HEADS)], axis=1)
    return (y_p.reshape(B, L, D_MODEL), y_s.reshape(n_seq, 1, D_MODEL), s_heads[None],
            k_p.reshape(1, B, WINDOW, SWA_KV_HEADS, SWA_HEAD_DIM),
            v_p.reshape(1, B, WINDOW, SWA_KV_HEADS, SWA_HEAD_DIM),
            st_s[None], ck_s.reshape(1, n_seq, WINDOW, SWA_KV_HEADS, SWA_HEAD_DIM),
            cv_s.reshape(1, n_seq, WINDOW, SWA_KV_HEADS, SWA_HEAD_DIM))
```

```python
import functools
import math

import jax
import jax.numpy as jnp
from jax import lax
from jax.experimental import pallas as pl
from jax.experimental.pallas import tpu as pltpu
from jax.experimental.pallas import tpu_sc as plsc

D_MODEL = 1024
N_META = 16
GLA_HEADS = 4
GLA_DK = 64
GLA_DV = 128
GLA_LOWRANK = 16
GLA_GATE_TAU = 16.0
GLA_CHUNK = 64
SWA_HEADS = 8
SWA_KV_HEADS = 2
SWA_HEAD_DIM = 64
SWA_GROUP = SWA_HEADS // SWA_KV_HEADS
WINDOW = 128
NUM_BUCKETS = 32
MAX_DISTANCE = 128
N_EXPERTS = 32
TOP_K = 4
D_FF = 1024
SWIGLU_ALPHA = 1.702
SWIGLU_LIMIT = 7.0
RMS_EPS = 1e-6

GLA_QK = GLA_HEADS * GLA_DK
GLA_V = GLA_HEADS * GLA_DV
SWA_Q = SWA_HEADS * SWA_HEAD_DIM
SWA_KV = SWA_KV_HEADS * SWA_HEAD_DIM
LANES = 128
C_GQ, C_GK, C_GV, C_GR = 0, GLA_QK, 2 * GLA_QK, 2 * GLA_QK + GLA_V
C_SQ = C_GR + GLA_V
C_SK = C_SQ + SWA_Q
C_SV = C_SK + SWA_KV
C_GA = C_SV + SWA_KV
D_PROJ = C_GA + LANES

MIX_TM = 256
MOE_TM = 256
DEC_SB = 16
VMEM_LIMIT = 56 * 1024 * 1024

F32 = jnp.float32
BF16 = jnp.bfloat16
NEG_INF = float("-inf")


def _dot(a, b):
    return jnp.dot(a, b, preferred_element_type=F32)


def _dot_nt(a, b):
    return lax.dot_general(a, b, (((1,), (1,)), ((), ())), preferred_element_type=F32)


def _split3(x):
    hi = x.astype(BF16)
    r1 = x - hi.astype(F32)
    mid = r1.astype(BF16)
    lo = (r1 - mid.astype(F32)).astype(BF16)
    return hi, mid, lo


def _rms(x, g):
    return x * lax.rsqrt(jnp.mean(x * x, axis=-1, keepdims=True) + RMS_EPS) * g


def _iota(shape, dim):
    return lax.broadcasted_iota(jnp.int32, shape, dim)


SLAB = D_MODEL // LANES


def _store_slabs(ref, x):
    rows = x.shape[0]
    for c in range(SLAB):
        ref[pl.ds(c, rows, stride=SLAB), :] = x[:, c * LANES:(c + 1) * LANES]


def _load_slab_chunk(ref, rows, first, stride):
    return ref[pl.ds(first, rows, stride=stride), :]


def _project(x, g_mix, w_in, w_a_up, b_a):
    h = _rms(x, g_mix).astype(BF16)
    proj = _dot(h, w_in)
    ga = proj[:, C_GA:C_GA + LANES].astype(BF16)
    z = _dot(ga, w_a_up) + b_a
    log_a = -(jnp.maximum(-z, 0.0) + jnp.log1p(jnp.exp(-jnp.abs(z)))) / GLA_GATE_TAU
    return dict(
        gq=proj[:, C_GQ:C_GQ + GLA_QK] * (GLA_DK ** -0.5),
        gk=proj[:, C_GK:C_GK + GLA_QK],
        gv=proj[:, C_GV:C_GV + GLA_V],
        gr=proj[:, C_GR:C_GR + GLA_V],
        sq=proj[:, C_SQ:C_SQ + SWA_Q],
        sk=proj[:, C_SK:C_SK + SWA_KV],
        sv=proj[:, C_SV:C_SV + SWA_KV],
        log_a=log_a,
    )


def _tail(x, o_gla, gr, o_swa, g_gla_out, g_swa_out, w_out, g_ffn, w_r, b_r, base):
    tm = x.shape[0]
    gate = gr * jax.nn.sigmoid(gr)
    parts = []
    for h in range(GLA_HEADS):
        sl = slice(h * GLA_DV, (h + 1) * GLA_DV)
        parts.append(_rms(o_gla[:, sl], g_gla_out) * gate[:, sl])
    parts.append(_rms(o_swa, g_swa_out))
    o = jnp.concatenate(parts, axis=1).astype(BF16)
    x_mid = x + _dot(o, w_out)
    hp = _rms(x_mid, g_ffn)

    h1 = hp.astype(BF16)
    h2 = (hp - h1.astype(F32)).astype(BF16)
    la = _dot_nt(w_r, h1)
    lb = _dot_nt(w_r[0:N_EXPERTS], h2)
    logits = la[0:N_EXPERTS] + la[N_EXPERTS:2 * N_EXPERTS] + lb + b_r

    eidx = _iota((N_EXPERTS, tm), 0)
    vals, idxs, onehots = [], [], []
    l = logits
    for _ in range(TOP_K):
        m = jnp.max(l, axis=0, keepdims=True)
        sel = jnp.min(jnp.where(l == m, eidx, N_EXPERTS), axis=0, keepdims=True)
        oh = eidx == sel
        l = jnp.where(oh, NEG_INF, l)
        vals.append(m)
        idxs.append(sel)
        onehots.append(oh)
    es = [jnp.exp(v - vals[0]) for v in vals]
    denom = es[0] + es[1] + es[2] + es[3]
    gates = [e / denom for e in es]

    ohf = jnp.concatenate([oh.astype(F32) for oh in onehots], axis=0)
    upper = (_iota((tm, tm), 0) < _iota((tm, tm), 1)).astype(BF16)
    prefix = _dot(ohf.astype(BF16), upper)
    ranks = []
    for k in range(TOP_K):
        sl = slice(k * N_EXPERTS, (k + 1) * N_EXPERTS)
        ohk = ohf[sl]
        base_t = jnp.concatenate([base] * (tm // LANES), axis=1)
        ranks.append(jnp.sum(ohk * (prefix[sl] + base_t), axis=0, keepdims=True))
        base = base + jnp.sum(ohk, axis=1, keepdims=True)
    zi = jnp.zeros((8 - TOP_K, tm), jnp.int32)
    zf = jnp.zeros((8 - TOP_K, tm), F32)
    topi = jnp.concatenate(idxs + [zi], axis=0)
    gate8 = jnp.concatenate(gates + [zf], axis=0)
    rank8 = jnp.concatenate([r.astype(jnp.int32) for r in ranks] + [zi], axis=0)
    return x_mid, hp, topi, gate8, rank8, base


def _gla_chunks(p, row0, s_bd, n_lead_pad):
    tm = p["gq"].shape[0]
    nch = tm // GLA_CHUNK
    log_a = p["log_a"]
    if n_lead_pad:
        rows = row0 + _iota((tm, GLA_QK), 0)
        log_a = jnp.where(rows >= n_lead_pad, log_a, 0.0)
    ri, ci = _iota((tm, tm), 0), _iota((tm, tm), 1)
    tril = ((ri >= ci) & (ri // GLA_CHUNK == ci // GLA_CHUNK)).astype(BF16)
    hi, mid, lo = _split3(log_a)
    b_all = _dot(tril, hi) + _dot(tril, mid) + _dot(tril, lo)

    c64 = GLA_CHUNK
    kk_mask = (_iota((GLA_QK, GLA_QK), 0) // c64) == (_iota((GLA_QK, GLA_QK), 1) // GLA_DK)
    vv_mask = (_iota((GLA_QK, GLA_V), 0) // c64) == (_iota((GLA_QK, GLA_V), 1) // GLA_DV)
    ss_mask = (_iota((GLA_QK, GLA_V), 0) // GLA_DK) == (_iota((GLA_QK, GLA_V), 1) // GLA_DV)
    causal = (_iota((c64, GLA_QK), 0) >= (_iota((c64, GLA_QK), 1) % c64)).astype(F32)
    zpad_k = jnp.zeros((LANES - c64, GLA_QK), F32)
    zpad_v = jnp.zeros((LANES - c64, GLA_V), BF16)

    outs = []
    for c in range(nch):
        rs = slice(c * c64, (c + 1) * c64)
        b = b_all[rs]
        q, k, v = p["gq"][rs], p["gk"][rs], p["gv"][rs]
        b_last = b[c64 - 1:c64]
        qt = (q * jnp.exp(b)).astype(BF16)
        kt = k * jnp.exp(-b)
        kd = k * jnp.exp(b_last - b)
        vb = v.astype(BF16)
        k_bd = jnp.where(kk_mask, jnp.concatenate([kt] * GLA_HEADS, axis=0), 0.0).astype(BF16)
        a = (_dot_nt(qt, k_bd) * causal).astype(BF16)
        v_bd = jnp.where(vv_mask, jnp.concatenate([vb] * GLA_HEADS, axis=0), jnp.zeros((), BF16))
        outs.append(_dot(qt, s_bd.astype(BF16)) + _dot(a, v_bd))
        kd_t = jnp.transpose(jnp.concatenate([kd, zpad_k], axis=0)).astype(BF16)
        upd = _dot(kd_t, jnp.concatenate([vb, zpad_v], axis=0))
        decay = jnp.exp(jnp.transpose(jnp.broadcast_to(b_last, (LANES, GLA_QK))))
        s_bd = s_bd * jnp.concatenate([decay] * GLA_HEADS, axis=1) + jnp.where(ss_mask, upd, 0.0)
    return jnp.concatenate(outs, axis=0), s_bd


def _swa_block(sq, kcat, vcat, bias_ref, sinks_ref, valid_t):
    half = _iota((1, LANES), 1) < SWA_HEAD_DIM
    top_rows = _iota((LANES, 1), 0) < SWA_HEAD_DIM
    k_roll = pltpu.roll(kcat, SWA_HEAD_DIM, 1)
    v_t = jnp.transpose(vcat)
    zeros_v = jnp.zeros((SWA_HEAD_DIM, 2 * WINDOW), F32)
    cols = []
    for kv in range(SWA_KV_HEADS):
        kk = jnp.where(half, kcat, k_roll) if kv == 0 else jnp.where(half, k_roll, kcat)
        q_parts = []
        for c in (2 * kv, 2 * kv + 1):
            qc = sq[:, c * LANES:(c + 1) * LANES]
            q_parts.append(jnp.where(half, qc, 0.0))
            q_parts.append(jnp.where(half, 0.0, qc))
        q_st = jnp.concatenate(q_parts, axis=0).astype(BF16)
        s = _dot_nt(kk.astype(BF16), q_st) * (SWA_HEAD_DIM ** -0.5) + bias_ref[kv]
        s = jnp.where(valid_t, s, NEG_INF)
        sink = jnp.concatenate(
            [jnp.full((1, WINDOW), sinks_ref[kv * SWA_GROUP + g], F32) for g in range(SWA_GROUP)], axis=1)
        m = jnp.maximum(jnp.max(s, axis=0, keepdims=True), sink)
        pr = jnp.exp(s - m)
        inv = 1.0 / (jnp.sum(pr, axis=0, keepdims=True) + jnp.exp(sink - m))
        pb = pr.astype(BF16)
        vk = v_t[kv * SWA_HEAD_DIM:(kv + 1) * SWA_HEAD_DIM]
        vv_t = jnp.concatenate([jnp.concatenate([vk, zeros_v], axis=1),
                                jnp.concatenate([zeros_v, vk], axis=1)], axis=0).astype(BF16)
        for pair in range(SWA_GROUP // 2):
            ce = slice(2 * pair * WINDOW, (2 * pair + 1) * WINDOW)
            co = slice((2 * pair + 1) * WINDOW, (2 * pair + 2) * WINDOW)
            p2_t = jnp.concatenate([pb[:, ce], pb[:, co]], axis=0)
            o2_t = _dot(vv_t, p2_t)
            o2_t = o2_t * jnp.where(top_rows, inv[:, ce], inv[:, co])
            cols.append(jnp.transpose(o2_t))
    return jnp.concatenate(cols, axis=1)


def _mixer_kernel(sinks_ref, x_ref, s0_ref, k0_ref, v0_ref, base0_ref, bias_ref,
                  g_mix_ref, w_in_ref, w_a_up_ref, b_a_ref, g_gla_ref, g_swa_ref, w_out_ref,
                  g_ffn_ref, w_r_ref, b_r_ref,
                  xmid_ref, hp_ref, topi_ref, gate_ref, rank_ref, sout_ref, kout_ref, vout_ref, cnt_ref,
                  s_scr, k_scr, v_scr, base_scr, *, n_lead_pad, prev_valid_from):
    b_id, j = pl.program_id(0), pl.program_id(1)
    tm = x_ref.shape[1]

    @pl.when(j == 0)
    def _():
        s_scr[...] = s0_ref[...]
        k_scr[...] = k0_ref[...]
        v_scr[...] = v0_ref[...]

    @pl.when((j == 0) & (b_id == 0))
    def _():
        base_scr[...] = base0_ref[...]

    x = x_ref[0]
    p = _project(x, g_mix_ref[...], w_in_ref[...], w_a_up_ref[...], b_a_ref[...])

    o_gla, s_new = _gla_chunks(p, j * tm, s_scr[...], n_lead_pad)
    s_scr[...] = s_new

    kj = _iota((2 * WINDOW, WINDOW), 0)
    qi = _iota((2 * WINDOW, WINDOW), 1)
    band = (kj > qi) & (kj <= qi + WINDOW)
    o_parts = []
    for sb in range(tm // WINDOW):
        rs = slice(sb * WINDOW, (sb + 1) * WINDOW)
        k_blk, v_blk = p["sk"][rs], p["sv"][rs]
        k_prev = k_scr[...] if sb == 0 else p["sk"][(sb - 1) * WINDOW:sb * WINDOW]
        v_prev = v_scr[...] if sb == 0 else p["sv"][(sb - 1) * WINDOW:sb * WINDOW]
        valid = band
        if sb == 0 and prev_valid_from:
            first = jnp.where(j == 0, prev_valid_from, 0)
            valid = band & (kj >= first)
        valid = jnp.concatenate([valid] * SWA_GROUP, axis=1)
        o_parts.append(_swa_block(p["sq"][rs], jnp.concatenate([k_prev, k_blk], axis=0),
                                  jnp.concatenate([v_prev, v_blk], axis=0), bias_ref, sinks_ref, valid))
    o_swa = jnp.concatenate(o_parts, axis=0)
    k_scr[...] = p["sk"][tm - WINDOW:tm]
    v_scr[...] = p["sv"][tm - WINDOW:tm]

    x_mid, hp, topi, gate8, rank8, base = _tail(
        x, o_gla, p["gr"], o_swa, g_gla_ref[...], g_swa_ref[...], w_out_ref[...],
        g_ffn_ref[...], w_r_ref[...], b_r_ref[...], base_scr[...])
    base_scr[...] = base
    xmid_ref[0] = x_mid
    _store_slabs(hp_ref, hp)
    topi_ref[...] = topi
    gate_ref[...] = gate8
    rank_ref[...] = rank8
    sout_ref[0] = s_new
    kout_ref[0] = p["sk"][tm - WINDOW:tm]
    vout_ref[0] = p["sv"][tm - WINDOW:tm]
    cnt_ref[...] = base


def _full_spec(shape):
    nd = len(shape)
    return pl.BlockSpec(shape, lambda *_: (0,) * nd)


def _mixer_call(x, s0, k0, v0, base0, wts, tm, n_lead_pad, prev_valid_from, hp_rows):
    B, L, _ = x.shape
    nj = L // tm
    T = B * L
    weight_args = (wts["bias"], wts["g_mix"], wts["w_in"], wts["w_a_up"], wts["b_a"], wts["g_gla"],
                   wts["g_swa"], wts["w_out"], wts["g_ffn"], wts["w_r"], wts["b_r"])
    in_specs = [
        pl.BlockSpec(memory_space=pltpu.SMEM),
        pl.BlockSpec((1, tm, D_MODEL), lambda b, j: (b, j, 0)),
        _full_spec(s0.shape), _full_spec(k0.shape), _full_spec(v0.shape), _full_spec(base0.shape),
    ] + [_full_spec(w.shape) for w in weight_args]
    tok_spec = pl.BlockSpec((8, tm), lambda b, j: (0, b * nj + j))
    out_specs = [
        pl.BlockSpec((1, tm, D_MODEL), lambda b, j: (b, j, 0)),
        pl.BlockSpec((tm * SLAB, LANES), lambda b, j: (b * nj + j, 0)),
        tok_spec, tok_spec, tok_spec,
        pl.BlockSpec((1, GLA_QK, GLA_V), lambda b, j: (b, 0, 0)),
        pl.BlockSpec((1, WINDOW, SWA_KV), lambda b, j: (b, 0, 0)),
        pl.BlockSpec((1, WINDOW, SWA_KV), lambda b, j: (b, 0, 0)),
        _full_spec((N_EXPERTS, LANES)),
    ]
    out_shape = [
        jax.ShapeDtypeStruct((B, L, D_MODEL), F32),
        jax.ShapeDtypeStruct((hp_rows * SLAB, LANES), F32),
        jax.ShapeDtypeStruct((8, T), jnp.int32),
        jax.ShapeDtypeStruct((8, T), F32),
        jax.ShapeDtypeStruct((8, T), jnp.int32),
        jax.ShapeDtypeStruct((B, GLA_QK, GLA_V), F32),
        jax.ShapeDtypeStruct((B, WINDOW, SWA_KV), F32),
        jax.ShapeDtypeStruct((B, WINDOW, SWA_KV), F32),
        jax.ShapeDtypeStruct((N_EXPERTS, LANES), F32),
    ]
    kern = functools.partial(_mixer_kernel, n_lead_pad=n_lead_pad, prev_valid_from=prev_valid_from)
    return pl.pallas_call(
        kern,
        grid=(B, nj),
        in_specs=in_specs,
        out_specs=out_specs,
        out_shape=out_shape,
        scratch_shapes=[pltpu.VMEM((GLA_QK, GLA_V), F32), pltpu.VMEM((WINDOW, SWA_KV), F32),
                        pltpu.VMEM((WINDOW, SWA_KV), F32), pltpu.VMEM((N_EXPERTS, LANES), F32)],
        compiler_params=pltpu.CompilerParams(dimension_semantics=("arbitrary", "arbitrary"),
                                             vmem_limit_bytes=VMEM_LIMIT),
        name="mixer",
    )(wts["sinks"], x, s0, k0, v0, base0, *weight_args)


def _decode_kernel(sinks_ref, x_ref, st_ref, ck_ref, cv_ref, base0_ref, bias_ref,
                   g_mix_ref, w_in_ref, w_a_up_ref, b_a_ref, g_gla_ref, g_swa_ref, w_out_ref,
                   g_ffn_ref, w_r_ref, b_r_ref,
                   xmid_ref, hp_ref, topi_ref, gate_ref, rank_ref, sto_ref, cko_ref, cvo_ref, cnt_ref,
                   at_scr, kt_scr, qt_scr, gv_scr, gr_scr, sq_scr, sk_scr, sv_scr, og_scr, os_scr):
    i = pl.program_id(0)
    n_seq = x_ref.shape[0]

    @pl.when(i == 0)
    def _():
        p = _project(x_ref[...], g_mix_ref[...], w_in_ref[...], w_a_up_ref[...], b_a_ref[...])
        at_scr[...] = jnp.transpose(jnp.exp(p["log_a"]))
        kt_scr[...] = jnp.transpose(p["gk"])
        qt_scr[...] = jnp.transpose(p["gq"])
        gv_scr[...] = p["gv"]
        gr_scr[...] = p["gr"]
        sq_scr[...] = p["sq"]
        sk_scr[...] = p["sk"]
        sv_scr[...] = p["sv"]

    lane_seq = _iota((GLA_QK, n_seq), 1)
    half = _iota((1, LANES), 1) < SWA_HEAD_DIM
    row_id = _iota((WINDOW, SWA_KV), 0)
    head_diag = (_iota((16, SWA_Q), 1) // SWA_HEAD_DIM) == _iota((16, SWA_Q), 0)
    sink_col = jnp.concatenate(
        [jnp.full((1, 1), sinks_ref[h], F32) for h in range(SWA_HEADS)] + [jnp.zeros((8, 1), F32)], axis=0)

    def per_seq(sl, carry):
        s = i * DEC_SB + sl
        sel = lane_seq == s
        a_c = jnp.sum(jnp.where(sel, at_scr[...], 0.0), axis=1, keepdims=True)
        k_c = jnp.sum(jnp.where(sel, kt_scr[...], 0.0), axis=1, keepdims=True)
        q_c = jnp.sum(jnp.where(sel, qt_scr[...], 0.0), axis=1, keepdims=True)
        st = st_ref[sl].reshape(GLA_QK, GLA_DV)
        v_row = gv_scr[pl.ds(s, 1), :]
        v_b = jnp.concatenate(
            [jnp.broadcast_to(v_row[:, h * GLA_DV:(h + 1) * GLA_DV], (GLA_DK, GLA_DV))
             for h in range(GLA_HEADS)], axis=0)
        st_new = a_c * st + k_c * v_b
        sto_ref[sl] = st_new.reshape(GLA_HEADS, GLA_DK, GLA_DV)
        t = q_c * st_new
        og_scr[pl.ds(s, 1), :] = jnp.concatenate(
            [jnp.sum(t[h * GLA_DK:(h + 1) * GLA_DK], axis=0, keepdims=True) for h in range(GLA_HEADS)],
            axis=1)

        k_new = sk_scr[pl.ds(s, 1), :]
        v_new = sv_scr[pl.ds(s, 1), :]
        kn = jnp.where(row_id == WINDOW - 1, k_new, pltpu.roll(ck_ref[sl], WINDOW - 1, 0))
        vn = jnp.where(row_id == WINDOW - 1, v_new, pltpu.roll(cv_ref[sl], WINDOW - 1, 0))
        cko_ref[sl] = kn
        cvo_ref[sl] = vn
        kr, vr = pltpu.roll(kn, SWA_HEAD_DIM, 1), pltpu.roll(vn, SWA_HEAD_DIM, 1)
        k0, k1 = jnp.where(half, kn, kr), jnp.where(half, kr, kn)
        v0, v1 = jnp.where(half, vn, vr), jnp.where(half, vr, vn)
        kw = jnp.concatenate([k0, k0, k1, k1], axis=1).astype(BF16)
        vw = jnp.concatenate([v0, v0, v1, v1], axis=1).astype(BF16)
        q_row = sq_scr[pl.ds(s, 1), :]
        qm = jnp.where(head_diag, jnp.broadcast_to(q_row, (16, SWA_Q)), 0.0).astype(BF16)
        sc = _dot_nt(qm, kw) * (SWA_HEAD_DIM ** -0.5) + bias_ref[...]
        m = jnp.maximum(jnp.max(sc, axis=1, keepdims=True), sink_col)
        pr = jnp.exp(sc - m)
        inv = 1.0 / (jnp.sum(pr, axis=1, keepdims=True) + jnp.exp(sink_col - m))
        ow = _dot(pr.astype(BF16), vw) * inv
        os_scr[pl.ds(s, 1), :] = jnp.sum(jnp.where(head_diag, ow, 0.0), axis=0, keepdims=True)
        return carry

    lax.fori_loop(0, DEC_SB, per_seq, 0)

    @pl.when(i == pl.num_programs(0) - 1)
    def _():
        x_mid, hp, topi, gate8, rank8, base = _tail(
            x_ref[...], og_scr[...], gr_scr[...], os_scr[...], g_gla_ref[...], g_swa_ref[...],
            w_out_ref[...], g_ffn_ref[...], w_r_ref[...], b_r_ref[...], base0_ref[...])
        xmid_ref[...] = x_mid
        _store_slabs(hp_ref, hp)
        topi_ref[...] = topi
        gate_ref[...] = gate8
        rank_ref[...] = rank8
        cnt_ref[...] = base


def _decode_call(xs, state, ck, cv, base0, bias_dec, wts):
    n_seq = xs.shape[0]
    nb = n_seq // DEC_SB
    weight_args = (wts["g_mix"], wts["w_in"], wts["w_a_up"], wts["b_a"], wts["g_gla"],
                   wts["g_swa"], wts["w_out"], wts["g_ffn"], wts["w_r"], wts["b_r"])
    in_specs = [
        pl.BlockSpec(memory_space=pltpu.SMEM),
        _full_spec(xs.shape),
        pl.BlockSpec((DEC_SB, GLA_HEADS, GLA_DK, GLA_DV), lambda i: (i, 0, 0, 0)),
        pl.BlockSpec((DEC_SB, WINDOW, SWA_KV), lambda i: (i, 0, 0)),
        pl.BlockSpec((DEC_SB, WINDOW, SWA_KV), lambda i: (i, 0, 0)),
        _full_spec(base0.shape), _full_spec(bias_dec.shape),
    ] + [_full_spec(w.shape) for w in weight_args]
    out_specs = [
        _full_spec((n_seq, D_MODEL)),
        _full_spec((n_seq * SLAB, LANES)),
        _full_spec((8, n_seq)), _full_spec((8, n_seq)), _full_spec((8, n_seq)),
        pl.BlockSpec((DEC_SB, GLA_HEADS, GLA_DK, GLA_DV), lambda i: (i, 0, 0, 0)),
        pl.BlockSpec((DEC_SB, WINDOW, SWA_KV), lambda i: (i, 0, 0)),
        pl.BlockSpec((DEC_SB, WINDOW, SWA_KV), lambda i: (i, 0, 0)),
        _full_spec((N_EXPERTS, LANES)),
    ]
    out_shape = [
        jax.ShapeDtypeStruct((n_seq, D_MODEL), F32),
        jax.ShapeDtypeStruct((n_seq * SLAB, LANES), F32),
        jax.ShapeDtypeStruct((8, n_seq), jnp.int32),
        jax.ShapeDtypeStruct((8, n_seq), F32),
        jax.ShapeDtypeStruct((8, n_seq), jnp.int32),
        jax.ShapeDtypeStruct(state.shape, F32),
        jax.ShapeDtypeStruct(ck.shape, F32),
        jax.ShapeDtypeStruct(cv.shape, F32),
        jax.ShapeDtypeStruct((N_EXPERTS, LANES), F32),
    ]
    scratch = [pltpu.VMEM((GLA_QK, n_seq), F32)] * 3 + [
        pltpu.VMEM((n_seq, GLA_V), F32), pltpu.VMEM((n_seq, GLA_V), F32), pltpu.VMEM((n_seq, SWA_Q), F32),
        pltpu.VMEM((n_seq, SWA_KV), F32), pltpu.VMEM((n_seq, SWA_KV), F32),
        pltpu.VMEM((n_seq, GLA_V), F32), pltpu.VMEM((n_seq, SWA_Q), F32)]
    return pl.pallas_call(
        _decode_kernel,
        grid=(nb,),
        in_specs=in_specs,
        out_specs=out_specs,
        out_shape=out_shape,
        scratch_shapes=scratch,
        compiler_params=pltpu.CompilerParams(dimension_semantics=("arbitrary",),
                                             vmem_limit_bytes=VMEM_LIMIT),
        name="decode",
    )(wts["sinks"], xs, state, ck, cv, base0, bias_dec, *weight_args)


SC_CORES = 2
SC_SUBCORES = 16
SC_WORKERS = SC_CORES * SC_SUBCORES
SC_SCATTER_ROWS = 32
SC_GATHER_ROWS = 24


def _sc_mesh():
    return plsc.VectorSubcoreMesh(core_axis_name="c", subcore_axis_name="s")


def _sc_worker_id():
    return lax.axis_index("s") * SC_CORES + lax.axis_index("c")


def _sc_scatter_rows(src_p, src_s, idx_p, idx_s, n_out):
    rows = SC_SCATTER_ROWS
    n_chunks = idx_p.shape[0] // SC_WORKERS
    n_s, _, rows_s = idx_s.shape
    assert n_chunks * SC_WORKERS == idx_p.shape[0] and n_chunks % 2 == 0 and n_s <= SC_WORKERS

    @functools.partial(
        pl.kernel, mesh=_sc_mesh(),
        out_type=jax.ShapeDtypeStruct((n_out, SLAB, LANES), F32),
        scratch_types=[pltpu.VMEM((2, TOP_K, rows), jnp.int32), pltpu.VMEM((2, rows, SLAB, LANES), F32),
                       pltpu.VMEM((TOP_K, rows_s), jnp.int32), pltpu.VMEM((rows_s, SLAB, LANES), F32),
                       pltpu.SemaphoreType.DMA((2,)), pltpu.SemaphoreType.DMA((2,))])
    def scatter_rows(srcp_hbm, srcs_hbm, idxp_hbm, idxs_hbm, out_hbm, idx_v, rows_v, idxs_v, rowss_v, lsem, ssem):
        wid = _sc_worker_id()

        def loads(c, b):
            g = wid * n_chunks + c
            return (pltpu.make_async_copy(idxp_hbm.at[g], idx_v.at[b], lsem.at[b]),
                    pltpu.make_async_copy(srcp_hbm.at[pl.ds(pl.multiple_of(g * rows, 8), rows)], rows_v.at[b],
                                          lsem.at[b]))

        def scatters(b):
            return [pltpu.make_async_copy(rows_v.at[b], out_hbm.at[idx_v.at[b, k]], ssem.at[b])
                    for k in range(TOP_K)]

        for d in loads(0, 0):
            d.start()

        @pl.loop(0, n_chunks, step=2)
        def _(c0):
            for b in range(2):
                c = c0 + b
                for d in loads(c, b):
                    d.wait()

                @pl.when(c >= 1)
                def _():
                    for d in scatters(1 - b):
                        d.wait()

                @pl.when(c + 1 < n_chunks)
                def _():
                    for d in loads(c + 1, 1 - b):
                        d.start()

                for d in scatters(b):
                    d.start()

        for d in scatters((n_chunks - 1) % 2):
            d.wait()

        @pl.when(wid < n_s)
        def _():
            pltpu.sync_copy(idxs_hbm.at[wid], idxs_v)
            pltpu.sync_copy(srcs_hbm.at[pl.ds(pl.multiple_of(wid * rows_s, 8), rows_s)], rowss_v)
            for k in range(TOP_K):
                pltpu.sync_copy(rowss_v, out_hbm.at[idxs_v.at[k]])

    return scatter_rows(src_p, src_s, idx_p, idx_s)


def _sc_gather_rows(src3, idx2):
    rows = SC_GATHER_ROWS
    n_chunks = idx2.shape[0] // SC_WORKERS
    assert n_chunks * SC_WORKERS == idx2.shape[0] and idx2.shape[1] == rows and n_chunks % 2 == 0

    @functools.partial(
        pl.kernel, mesh=_sc_mesh(),
        out_type=jax.ShapeDtypeStruct((idx2.shape[0] * rows, SLAB, LANES), F32),
        scratch_types=[pltpu.VMEM((2, rows), jnp.int32), pltpu.VMEM((2, rows, SLAB, LANES), F32),
                       pltpu.SemaphoreType.DMA((2,)), pltpu.SemaphoreType.DMA((2,))])
    def gather_rows(src_hbm, idx_hbm, out_hbm, idx_v, rows_v, gsem, wsem):
        wid = _sc_worker_id()

        def gather(b):
            return pltpu.make_async_copy(src_hbm.at[idx_v.at[b]], rows_v.at[b], gsem.at[b])

        def write(c, b):
            base = pl.multiple_of((wid * n_chunks + c) * rows, 8)
            return pltpu.make_async_copy(rows_v.at[b], out_hbm.at[pl.ds(base, rows)], wsem.at[b])

        pltpu.sync_copy(idx_hbm.at[wid * n_chunks], idx_v.at[0])
        gather(0).start()

        @pl.loop(0, n_chunks, step=2)
        def _(c0):
            for b in range(2):
                c = c0 + b

                @pl.when(c + 1 < n_chunks)
                def _():
                    @pl.when(c >= 1)
                    def _():
                        write(c - 1, 1 - b).wait()
                    pltpu.sync_copy(idx_hbm.at[wid * n_chunks + c + 1], idx_v.at[1 - b])
                    gather(1 - b).start()

                gather(b).wait()
                write(c, b).start()

        write(n_chunks - 2, 0).wait()
        write(n_chunks - 1, 1).wait()

    return gather_rows(src3, idx2)


def _ffn_kernel(blk_e_ref, nused_ref, x_ref, wu_ref, bu_ref, wd_ref, bd_ref, y_ref, xbf, actbf, wu_bf, wd_bf):
    i = pl.program_id(0)
    tm = MOE_TM
    n_tiles = D_FF // FF_TILE

    @pl.when(i < nused_ref[0])
    def _():
        @pl.when((i == 0) | (blk_e_ref[i] != blk_e_ref[jnp.maximum(i - 1, 0)]))
        def _():
            wu_bf[...] = wu_ref[0].astype(BF16)
            wd_bf[...] = wd_ref[0].astype(BF16)

        for c in range(SLAB):
            xbf[:, c * LANES:(c + 1) * LANES] = _load_slab_chunk(x_ref, tm, c, SLAB).astype(BF16)
        for n in range(n_tiles):
            gc = slice(n * FF_TILE, (n + 1) * FF_TILE)
            lc = slice(D_FF + n * FF_TILE, D_FF + (n + 1) * FF_TILE)
            g = jnp.minimum(_dot(xbf[...], wu_bf[:, gc]) + bu_ref[0, :, gc], SWIGLU_LIMIT)
            lin = jnp.clip(_dot(xbf[...], wu_bf[:, lc]) + bu_ref[0, :, lc], -SWIGLU_LIMIT, SWIGLU_LIMIT)
            actbf[:, gc] = (g * jax.nn.sigmoid(SWIGLU_ALPHA * g) * (lin + 1.0)).astype(BF16)
        for n in range(n_tiles):
            yc = slice(n * FF_TILE, (n + 1) * FF_TILE)
            y = _dot(actbf[...], wd_bf[:, yc]) + bd_ref[0, :, yc]
            for c in range(FF_TILE // LANES):
                y_ref[pl.ds(n * (FF_TILE // LANES) + c, tm, stride=SLAB), :] = y[:, c * LANES:(c + 1) * LANES]

    @pl.when(i >= nused_ref[0])
    def _():
        y_ref[...] = jnp.zeros_like(y_ref)


def _ffn_call(blk_e, nused, xs2, w_up, b_up, w_down, b_down):
    n_blocks = blk_e.shape[0]
    tm = MOE_TM
    row_blk = pl.BlockSpec((tm * SLAB, LANES), lambda i, be, nu: (i, 0))
    grid_spec = pltpu.PrefetchScalarGridSpec(
        num_scalar_prefetch=2,
        grid=(n_blocks,),
        in_specs=[
            row_blk,
            pl.BlockSpec((1, D_MODEL, 2 * D_FF), lambda i, be, nu: (be[i], 0, 0)),
            pl.BlockSpec((1, 1, 2 * D_FF), lambda i, be, nu: (be[i], 0, 0)),
            pl.BlockSpec((1, D_FF, D_MODEL), lambda i, be, nu: (be[i], 0, 0)),
            pl.BlockSpec((1, 1, D_MODEL), lambda i, be, nu: (be[i], 0, 0)),
        ],
        out_specs=row_blk,
        scratch_shapes=[pltpu.VMEM((tm, D_MODEL), BF16), pltpu.VMEM((tm, D_FF), BF16),
                        pltpu.VMEM((D_MODEL, 2 * D_FF), BF16), pltpu.VMEM((D_FF, D_MODEL), BF16)],
    )
    return pl.pallas_call(
        _ffn_kernel,
        grid_spec=grid_spec,
        out_shape=jax.ShapeDtypeStruct((n_blocks * tm * SLAB, LANES), F32),
        compiler_params=pltpu.CompilerParams(dimension_semantics=("arbitrary",),
                                             vmem_limit_bytes=VMEM_LIMIT),
        name="experts",
    )(blk_e, nused, xs2, w_up, b_up.reshape(N_EXPERTS, 1, 2 * D_FF), w_down, b_down.reshape(N_EXPERTS, 1, D_MODEL))


FF_TILE = 256


RING = 3


def _expert_kernel(blk_e_ref, nused_ref, tok0_ref, tok1_ref, tokn_ref, retp_ref, hp_hbm, wu_ref, bu_ref,
                   wd_ref, bd_ref, y_hbm, xbuf0, xbuf1, xbuf2, ybuf0, ybuf1, ybuf2, xbf, actbf, zbuf,
                   wu_bf, wd_bf, gsem, ssem, zsem):
    i = pl.program_id(0)
    tm = MOE_TM
    nused = nused_ref[0]
    n_tiles = D_FF // FF_TILE
    xbufs, ybufs = (xbuf0, xbuf1, xbuf2), (ybuf0, ybuf1, ybuf2)

    def gather(tok_ref, r, s):
        src = hp_hbm.at[pl.ds(pl.multiple_of(tok_ref[0, 0, r], SLAB), SLAB)]
        return pltpu.make_async_copy(src, xbufs[s].at[pl.ds(r * SLAB, SLAB)], gsem.at[s])

    def scatter(r, s):
        dst = y_hbm.at[pl.ds(pl.multiple_of(retp_ref[0, 0, r], SLAB), SLAB)]
        return pltpu.make_async_copy(ybufs[s].at[pl.ds(r * SLAB, SLAB)], dst, ssem.at[s])

    def wait_gather(s):
        pltpu.make_async_copy(hp_hbm.at[pl.ds(0, tm * SLAB)], xbufs[s], gsem.at[s]).wait()

    def wait_scatter(s):
        pltpu.make_async_copy(ybufs[s], y_hbm.at[pl.ds(0, tm * SLAB)], ssem.at[s]).wait()

    @pl.when(i == 0)
    def _():
        ybuf2[...] = jnp.zeros_like(ybuf2)
        zbuf[...] = jnp.zeros_like(zbuf)

        def issue(r, c):
            gather(tok0_ref, r, 0).start()
            gather(tok1_ref, r, 1).start()
            return c
        lax.fori_loop(0, tm, issue, 0)

    def compute(s):
        nxt = (s + 2) % RING
        wait_gather(s)

        @pl.when(i >= 2)
        def _():
            wait_scatter(s)

        @pl.when((i == 0) | (blk_e_ref[i] != blk_e_ref[jnp.maximum(i - 1, 0)]))
        def _():
            wu_bf[...] = wu_ref[0].astype(BF16)
            wd_bf[...] = wd_ref[0].astype(BF16)

        for c in range(SLAB):
            xbf[:, c * LANES:(c + 1) * LANES] = _load_slab_chunk(xbufs[s], tm, c, SLAB).astype(BF16)
        for r in range(tm):
            gather(tokn_ref, r, nxt).start(priority=r % 2)
            scatter(r, nxt).start(priority=r % 2)
        for n in range(n_tiles):
            gc = slice(n * FF_TILE, (n + 1) * FF_TILE)
            lc = slice(D_FF + n * FF_TILE, D_FF + (n + 1) * FF_TILE)
            g = jnp.minimum(_dot(xbf[...], wu_bf[:, gc]) + bu_ref[0, :, gc], SWIGLU_LIMIT)
            lin = jnp.clip(_dot(xbf[...], wu_bf[:, lc]) + bu_ref[0, :, lc], -SWIGLU_LIMIT, SWIGLU_LIMIT)
            actbf[:, gc] = (g * jax.nn.sigmoid(SWIGLU_ALPHA * g) * (lin + 1.0)).astype(BF16)
        for n in range(n_tiles):
            yc = slice(n * FF_TILE, (n + 1) * FF_TILE)
            y = _dot(actbf[...], wd_bf[:, yc]) + bd_ref[0, :, yc]
            for c in range(FF_TILE // LANES):
                ybufs[s][pl.ds(n * (FF_TILE // LANES) + c, tm, stride=SLAB), :] = y[:, c * LANES:(c + 1) * LANES]

    def drain(s):
        wait_gather(s)
        wait_gather((s + 1) % RING)
        wait_scatter(s)
        wait_scatter((s + 1) % RING)
        last = (s + 2) % RING

        def issue(r, c):
            scatter(r, last).start()
            return c
        lax.fori_loop(0, tm, issue, 0)
        wait_scatter(last)

    for s in range(RING):
        @pl.when((i < nused) & (i % RING == s))
        def _():
            compute(s)

        @pl.when((i == nused) & (i % RING == s))
        def _():
            drain(s)

    @pl.when(i >= nused)
    def _():
        zc = pltpu.make_async_copy(zbuf, y_hbm.at[pl.ds(pl.multiple_of((tm + i * tm) * SLAB, SLAB), tm * SLAB)], zsem)
        zc.start()
        zc.wait()


def _expert_call(blk_e, nused, tok_sorted, ret_sorted, hp_all, w_up, b_up, w_down, b_down, n_slots):
    n_blocks = blk_e.shape[0]
    tm = MOE_TM
    n_rows = n_blocks * tm + tm
    ret_tab = (jnp.concatenate([n_slots + jnp.arange(tm, dtype=jnp.int32), ret_sorted]) * SLAB
               ).reshape(n_blocks + 1, 1, tm)
    tok_tab = (tok_sorted * SLAB).reshape(n_blocks, 1, tm)
    smem_blk = functools.partial(pl.BlockSpec, (1, 1, tm), memory_space=pltpu.SMEM)
    grid_spec = pltpu.PrefetchScalarGridSpec(
        num_scalar_prefetch=2,
        grid=(n_blocks,),
        in_specs=[
            smem_blk(lambda i, be, nu: (0, 0, 0)),
            smem_blk(lambda i, be, nu: (1, 0, 0)),
            smem_blk(lambda i, be, nu: (jnp.minimum(i + 2, n_blocks - 1), 0, 0)),
            smem_blk(lambda i, be, nu: (i, 0, 0)),
            pl.BlockSpec(memory_space=pl.ANY),
            pl.BlockSpec((1, D_MODEL, 2 * D_FF), lambda i, be, nu: (be[i], 0, 0)),
            pl.BlockSpec((1, 1, 2 * D_FF), lambda i, be, nu: (be[i], 0, 0)),
            pl.BlockSpec((1, D_FF, D_MODEL), lambda i, be, nu: (be[i], 0, 0)),
            pl.BlockSpec((1, 1, D_MODEL), lambda i, be, nu: (be[i], 0, 0)),
        ],
        out_specs=pl.BlockSpec(memory_space=pl.ANY),
        scratch_shapes=[pltpu.VMEM((tm * SLAB, LANES), F32)] * (2 * RING) + [
                        pltpu.VMEM((tm, D_MODEL), BF16), pltpu.VMEM((tm, D_FF), BF16),
                        pltpu.VMEM((tm * SLAB, LANES), F32),
                        pltpu.VMEM((D_MODEL, 2 * D_FF), BF16), pltpu.VMEM((D_FF, D_MODEL), BF16),
                        pltpu.SemaphoreType.DMA((RING,)), pltpu.SemaphoreType.DMA((RING,)),
                        pltpu.SemaphoreType.DMA],
    )
    return pl.pallas_call(
        _expert_kernel,
        grid_spec=grid_spec,
        out_shape=jax.ShapeDtypeStruct((n_rows * SLAB, LANES), F32),
        compiler_params=pltpu.CompilerParams(dimension_semantics=("arbitrary",),
                                             vmem_limit_bytes=VMEM_LIMIT),
        name="experts",
    )(blk_e, nused, tok_tab, tok_tab, tok_tab, ret_tab, hp_all,
      w_up, b_up.reshape(N_EXPERTS, 1, 2 * D_FF), w_down, b_down.reshape(N_EXPERTS, 1, D_MODEL))


def _combine_kernel(ys_ref, xmid_ref, gate_ref, g_final_ref, y_ref):
    tm = xmid_ref.shape[0]
    gts = gate_ref[...]
    chunks = []
    for c in range(SLAB):
        acc = xmid_ref[:, c * LANES:(c + 1) * LANES]
        for k in range(TOP_K):
            acc = acc + _load_slab_chunk(ys_ref, tm, k * SLAB + c, TOP_K * SLAB) * gts[:, k:k + 1]
        chunks.append(acc)
    y_ref[...] = _rms(jnp.concatenate(chunks, axis=1), g_final_ref[...])


def _combine_call(ys4, row0, x_mid, gates, g_final, tm):
    T = x_mid.shape[0]
    blk0 = row0 // tm
    return pl.pallas_call(
        _combine_kernel,
        grid=(T // tm,),
        in_specs=[
            pl.BlockSpec((tm * TOP_K * SLAB, LANES), lambda i: (blk0 + i, 0)),
            pl.BlockSpec((tm, D_MODEL), lambda i: (i, 0)),
            pl.BlockSpec((tm, TOP_K), lambda i: (i, 0)),
            _full_spec((1, D_MODEL)),
        ],
        out_specs=pl.BlockSpec((tm, D_MODEL), lambda i: (i, 0)),
        out_shape=jax.ShapeDtypeStruct((T, D_MODEL), F32),
        compiler_params=pltpu.CompilerParams(dimension_semantics=("arbitrary",),
                                             vmem_limit_bytes=VMEM_LIMIT),
        name="combine",
    )(ys4, x_mid, gates, g_final)


def _t5_bucket(dist):
    n = jnp.maximum(dist, 0)
    max_exact = NUM_BUCKETS // 2
    nf = jnp.maximum(n, 1).astype(F32)
    large = max_exact + (jnp.log(nf / max_exact) / math.log(MAX_DISTANCE / max_exact)
                         * (NUM_BUCKETS - max_exact)).astype(jnp.int32)
    large = jnp.minimum(large, NUM_BUCKETS - 1)
    return jnp.where(n < max_exact, n, large)


def kernel(x_prompt, x_sample, state_gla, cache_swa_k, cache_swa_v, meta_tokens, rel_bias_table,
           g_mix, w_in, w_a_up, b_a, g_gla_out, g_swa_out, attn_sinks, w_out,
           g_ffn, w_router, b_router, w_up, b_up, w_down, b_down, g_final):
    assert g_mix.shape[0] == 1, "single-layer trunk"
    B, L, _ = x_prompt.shape
    n_seq = x_sample.shape[0]
    TP = B * L
    T_all = TP + n_seq

    wi = w_in[0]
    sizes = (GLA_QK, GLA_QK, GLA_V, GLA_V, GLA_LOWRANK, SWA_Q, SWA_KV, SWA_KV)
    offs = [0]
    for s in sizes:
        offs.append(offs[-1] + s)
    seg = [wi[:, offs[n]:offs[n + 1]] for n in range(8)]
    w_in_r = jnp.concatenate(
        seg[0:4] + seg[5:8] + [seg[4], jnp.zeros((D_MODEL, LANES - GLA_LOWRANK), F32)], axis=1).astype(BF16)
    w_a_pad = jnp.concatenate([w_a_up[0], jnp.zeros((LANES - GLA_LOWRANK, GLA_QK), F32)], axis=0).astype(BF16)
    wr_t = jnp.transpose(w_router[0])
    wr_hi = wr_t.astype(BF16)
    wr_lo = (wr_t - wr_hi.astype(F32)).astype(BF16)
    qi = jnp.arange(WINDOW)[:, None]
    kj = jnp.arange(2 * WINDOW)[None, :]
    buckets = jnp.arange(NUM_BUCKETS)
    table = rel_bias_table.astype(F32)
    oh_p = (_t5_bucket(qi - kj + WINDOW)[..., None] == buckets).astype(F32)
    bias_p = jnp.einsum("qkb,bh->hkq", oh_p, table, precision=lax.Precision.HIGHEST)
    bias_p = bias_p.reshape(SWA_KV_HEADS, SWA_GROUP, 2 * WINDOW, WINDOW).transpose(0, 2, 1, 3)
    bias_p = bias_p.reshape(SWA_KV_HEADS, 2 * WINDOW, SWA_GROUP * WINDOW)
    oh_d = (_t5_bucket(WINDOW - 1 - jnp.arange(WINDOW))[:, None] == buckets).astype(F32)
    bias_d = jnp.einsum("rb,bh->hr", oh_d, table, precision=lax.Precision.HIGHEST)
    bias_d = jnp.concatenate([bias_d, jnp.zeros((8, WINDOW), F32)], axis=0)
    wts = dict(
        sinks=attn_sinks[0].astype(F32), bias=bias_p,
        g_mix=g_mix[0][None], w_in=w_in_r, w_a_up=w_a_pad, b_a=b_a[0][None],
        g_gla=g_gla_out[0][None], g_swa=g_swa_out[0][None], w_out=w_out[0].astype(BF16),
        g_ffn=g_ffn[0][None], w_r=jnp.concatenate([wr_hi, wr_lo], axis=0), b_r=b_router[0][:, None],
    )

    x_pre = jnp.concatenate([jnp.zeros((WINDOW - N_META, D_MODEL), F32), meta_tokens.astype(F32)], axis=0)[None]
    zeros_s = jnp.zeros((GLA_QK, GLA_V), F32)
    zeros_kv = jnp.zeros((WINDOW, SWA_KV), F32)
    zeros_b = jnp.zeros((N_EXPERTS, LANES), F32)
    pre = _mixer_call(x_pre, zeros_s, zeros_kv, zeros_kv, zeros_b, wts, WINDOW, WINDOW - N_META, 0, WINDOW)
    s_meta, k_meta, v_meta = pre[5][0], pre[6][0], pre[7][0]

    (xmid_p, hp_p, topi_p, gate_p, rank_p, s_p, k_p, v_p, cnt_p) = _mixer_call(
        x_prompt, s_meta, k_meta, v_meta, zeros_b, wts, MIX_TM, 0, WINDOW - N_META, TP)

    (xmid_s, hp_s, topi_s, gate_s, rank_s, st_s, ck_s, cv_s, cnt_all) = _decode_call(
        x_sample[:, 0], state_gla[0], cache_swa_k[0].reshape(n_seq, WINDOW, SWA_KV),
        cache_swa_v[0].reshape(n_seq, WINDOW, SWA_KV), cnt_p, bias_d, wts)

    tm = MOE_TM
    n_slots = T_all * TOP_K
    n_blocks = -(-n_slots // tm) + N_EXPERTS
    top_e = jnp.concatenate([topi_p[:TOP_K], topi_s[:TOP_K]], axis=1)
    rank = jnp.concatenate([rank_p[:TOP_K], rank_s[:TOP_K]], axis=1)
    counts = cnt_all[:, 0].astype(jnp.int32)
    padded = (counts + tm - 1) // tm * tm
    pad_end = jnp.cumsum(padded)
    pad_start = pad_end - padded
    e_ids = jnp.arange(N_EXPERTS, dtype=jnp.int32)
    dest = jnp.sum(jnp.where(top_e[..., None] == e_ids, pad_start, 0), axis=-1) + rank
    n_pad = n_blocks * tm
    blk_e = jnp.minimum(jnp.sum(pad_end[None] <= (jnp.arange(n_blocks, dtype=jnp.int32) * tm)[:, None], axis=1),
                        N_EXPERTS - 1).astype(jnp.int32)
    nused = (pad_end[-1] // tm).astype(jnp.int32).reshape(1)

    sample_rows = 8
    idx_p = dest[:, :TP].reshape(TOP_K, TP // SC_SCATTER_ROWS, SC_SCATTER_ROWS).transpose(1, 0, 2)
    idx_s = dest[:, TP:].reshape(TOP_K, n_seq // sample_rows, sample_rows).transpose(1, 0, 2)
    xs3 = _sc_scatter_rows(hp_p.reshape(TP, SLAB, LANES), hp_s.reshape(n_seq, SLAB, LANES), idx_p, idx_s, n_pad)
    ys2 = _ffn_call(blk_e, nused, xs3.reshape(-1, LANES), w_up[0], b_up[0], w_down[0], b_down[0])
    slot_src = jnp.transpose(dest).reshape(n_slots // SC_GATHER_ROWS, SC_GATHER_ROWS)
    ys4 = _sc_gather_rows(ys2.reshape(-1, SLAB, LANES), slot_src).reshape(-1, LANES)

    gates = jnp.transpose(jnp.concatenate([gate_p[:TOP_K], gate_s[:TOP_K]], axis=1))
    gf = g_final[None]
    y_p = _combine_call(ys4, 0, xmid_p.reshape(TP, D_MODEL), gates[:TP], gf, MIX_TM)
    y_s = _combine_call(ys4, TP, xmid_s, gates[TP:], gf, n_seq)

    s_heads = jnp.stack([s_p[:, h * GLA_DK:(h + 1) * GLA_DK, h * GLA_DV:(h + 1) * GLA_DV]
                         for h in range(GLA_HEADS)], axis=1)
    return (y_p.reshape(B, L, D_MODEL), y_s.reshape(n_seq, 1, D_MODEL), s_heads[None],
            k_p.reshape(1, B, WINDOW, SWA_KV_HEADS, SWA_HEAD_DIM),
            v_p.reshape(1, B, WINDOW, SWA_KV_HEADS, SWA_HEAD_DIM),
            st_s[None], ck_s.reshape(1, n_seq, WINDOW, SWA_KV_HEADS, SWA_HEAD_DIM),
            cv_s.reshape(1, n_seq, WINDOW, SWA_KV_HEADS, SWA_HEAD_DIM))
```

```python
import functools
import math

import jax
import jax.numpy as jnp
from jax import lax
from jax.experimental import pallas as pl
from jax.experimental.pallas import tpu as pltpu
from jax.experimental.pallas import tpu_sc as plsc

D_MODEL = 1024
N_META = 16
GLA_HEADS = 4
GLA_DK = 64
GLA_DV = 128
GLA_LOWRANK = 16
GLA_GATE_TAU = 16.0
GLA_CHUNK = 64
SWA_HEADS = 8
SWA_KV_HEADS = 2
SWA_HEAD_DIM = 64
SWA_GROUP = SWA_HEADS // SWA_KV_HEADS
WINDOW = 128
NUM_BUCKETS = 32
MAX_DISTANCE = 128
N_EXPERTS = 32
TOP_K = 4
D_FF = 1024
SWIGLU_ALPHA = 1.702
SWIGLU_LIMIT = 7.0
RMS_EPS = 1e-6

GLA_QK = GLA_HEADS * GLA_DK
GLA_V = GLA_HEADS * GLA_DV
SWA_Q = SWA_HEADS * SWA_HEAD_DIM
SWA_KV = SWA_KV_HEADS * SWA_HEAD_DIM
LANES = 128
C_GQ, C_GK, C_GV, C_GR = 0, GLA_QK, 2 * GLA_QK, 2 * GLA_QK + GLA_V
C_SQ = C_GR + GLA_V
C_SK = C_SQ + SWA_Q
C_SV = C_SK + SWA_KV
C_GA = C_SV + SWA_KV
D_PROJ = C_GA + LANES

MIX_TM = 256
MOE_TM = 512
DEC_SB = 16
VMEM_LIMIT = 56 * 1024 * 1024

F32 = jnp.float32
BF16 = jnp.bfloat16
NEG_INF = float("-inf")


def _dot(a, b):
    return jnp.dot(a, b, preferred_element_type=F32)


def _dot_nt(a, b):
    return lax.dot_general(a, b, (((1,), (1,)), ((), ())), preferred_element_type=F32)


def _split3(x):
    hi = x.astype(BF16)
    r1 = x - hi.astype(F32)
    mid = r1.astype(BF16)
    lo = (r1 - mid.astype(F32)).astype(BF16)
    return hi, mid, lo


def _rms(x, g):
    return x * lax.rsqrt(jnp.mean(x * x, axis=-1, keepdims=True) + RMS_EPS) * g


def _iota(shape, dim):
    return lax.broadcasted_iota(jnp.int32, shape, dim)


SLAB = D_MODEL // LANES


def _store_slabs(ref, x):
    rows = x.shape[0]
    for c in range(SLAB):
        ref[pl.ds(c, rows, stride=SLAB), :] = x[:, c * LANES:(c + 1) * LANES]


def _load_slab_chunk(ref, rows, first, stride):
    return ref[pl.ds(first, rows, stride=stride), :]


def _project(x, g_mix, w_in, w_a_up, b_a):
    h = _rms(x, g_mix).astype(BF16)
    proj = _dot(h, w_in)
    ga = proj[:, C_GA:C_GA + LANES].astype(BF16)
    z = _dot(ga, w_a_up) + b_a
    log_a = -(jnp.maximum(-z, 0.0) + jnp.log1p(jnp.exp(-jnp.abs(z)))) / GLA_GATE_TAU
    return dict(
        gq=proj[:, C_GQ:C_GQ + GLA_QK] * (GLA_DK ** -0.5),
        gk=proj[:, C_GK:C_GK + GLA_QK],
        gv=proj[:, C_GV:C_GV + GLA_V],
        gr=proj[:, C_GR:C_GR + GLA_V],
        sq=proj[:, C_SQ:C_SQ + SWA_Q],
        sk=proj[:, C_SK:C_SK + SWA_KV],
        sv=proj[:, C_SV:C_SV + SWA_KV],
        log_a=log_a,
    )


def _tail(x, o_gla, gr, o_swa, g_gla_out, g_swa_out, w_out, g_ffn, w_r, b_r, base):
    tm = x.shape[0]
    gate = gr * jax.nn.sigmoid(gr)
    parts = []
    for h in range(GLA_HEADS):
        sl = slice(h * GLA_DV, (h + 1) * GLA_DV)
        parts.append(_rms(o_gla[:, sl], g_gla_out) * gate[:, sl])
    parts.append(_rms(o_swa, g_swa_out))
    o = jnp.concatenate(parts, axis=1).astype(BF16)
    x_mid = x + _dot(o, w_out)
    hp = _rms(x_mid, g_ffn)

    h1 = hp.astype(BF16)
    h2 = (hp - h1.astype(F32)).astype(BF16)
    la = _dot_nt(w_r, h1)
    lb = _dot_nt(w_r[0:N_EXPERTS], h2)
    logits = la[0:N_EXPERTS] + la[N_EXPERTS:2 * N_EXPERTS] + lb + b_r

    eidx = _iota((N_EXPERTS, tm), 0)
    vals, idxs, onehots = [], [], []
    l = logits
    for _ in range(TOP_K):
        m = jnp.max(l, axis=0, keepdims=True)
        sel = jnp.min(jnp.where(l == m, eidx, N_EXPERTS), axis=0, keepdims=True)
        oh = eidx == sel
        l = jnp.where(oh, NEG_INF, l)
        vals.append(m)
        idxs.append(sel)
        onehots.append(oh)
    es = [jnp.exp(v - vals[0]) for v in vals]
    denom = es[0] + es[1] + es[2] + es[3]
    gates = [e / denom for e in es]

    ohf = jnp.concatenate([oh.astype(F32) for oh in onehots], axis=0)
    upper = (_iota((tm, tm), 0) < _iota((tm, tm), 1)).astype(BF16)
    prefix = _dot(ohf.astype(BF16), upper)
    ranks = []
    for k in range(TOP_K):
        sl = slice(k * N_EXPERTS, (k + 1) * N_EXPERTS)
        ohk = ohf[sl]
        base_t = jnp.concatenate([base] * (tm // LANES), axis=1)
        ranks.append(jnp.sum(ohk * (prefix[sl] + base_t), axis=0, keepdims=True))
        base = base + jnp.sum(ohk, axis=1, keepdims=True)
    zi = jnp.zeros((8 - TOP_K, tm), jnp.int32)
    zf = jnp.zeros((8 - TOP_K, tm), F32)
    topi = jnp.concatenate(idxs + [zi], axis=0)
    gate8 = jnp.concatenate(gates + [zf], axis=0)
    rank8 = jnp.concatenate([r.astype(jnp.int32) for r in ranks] + [zi], axis=0)
    return x_mid, hp, topi, gate8, rank8, base


def _gla_chunks(p, row0, s_bd, n_lead_pad):
    tm = p["gq"].shape[0]
    nch = tm // GLA_CHUNK
    log_a = p["log_a"]
    if n_lead_pad:
        rows = row0 + _iota((tm, GLA_QK), 0)
        log_a = jnp.where(rows >= n_lead_pad, log_a, 0.0)
    ri, ci = _iota((tm, tm), 0), _iota((tm, tm), 1)
    tril = ((ri >= ci) & (ri // GLA_CHUNK == ci // GLA_CHUNK)).astype(BF16)
    hi, mid, lo = _split3(log_a)
    b_all = _dot(tril, hi) + _dot(tril, mid) + _dot(tril, lo)

    c64 = GLA_CHUNK
    kk_mask = (_iota((GLA_QK, GLA_QK), 0) // c64) == (_iota((GLA_QK, GLA_QK), 1) // GLA_DK)
    vv_mask = (_iota((GLA_QK, GLA_V), 0) // c64) == (_iota((GLA_QK, GLA_V), 1) // GLA_DV)
    ss_mask = (_iota((GLA_QK, GLA_V), 0) // GLA_DK) == (_iota((GLA_QK, GLA_V), 1) // GLA_DV)
    causal = (_iota((c64, GLA_QK), 0) >= (_iota((c64, GLA_QK), 1) % c64)).astype(F32)
    zpad_k = jnp.zeros((LANES - c64, GLA_QK), F32)
    zpad_v = jnp.zeros((LANES - c64, GLA_V), BF16)

    outs = []
    for c in range(nch):
        rs = slice(c * c64, (c + 1) * c64)
        b = b_all[rs]
        q, k, v = p["gq"][rs], p["gk"][rs], p["gv"][rs]
        b_last = b[c64 - 1:c64]
        qt = (q * jnp.exp(b)).astype(BF16)
        kt = k * jnp.exp(-b)
        kd = k * jnp.exp(b_last - b)
        vb = v.astype(BF16)
        k_bd = jnp.where(kk_mask, jnp.concatenate([kt] * GLA_HEADS, axis=0), 0.0).astype(BF16)
        a = (_dot_nt(qt, k_bd) * causal).astype(BF16)
        v_bd = jnp.where(vv_mask, jnp.concatenate([vb] * GLA_HEADS, axis=0), jnp.zeros((), BF16))
        outs.append(_dot(qt, s_bd.astype(BF16)) + _dot(a, v_bd))
        kd_t = jnp.transpose(jnp.concatenate([kd, zpad_k], axis=0)).astype(BF16)
        upd = _dot(kd_t, jnp.concatenate([vb, zpad_v], axis=0))
        decay = jnp.exp(jnp.transpose(jnp.broadcast_to(b_last, (LANES, GLA_QK))))
        s_bd = s_bd * jnp.concatenate([decay] * GLA_HEADS, axis=1) + jnp.where(ss_mask, upd, 0.0)
    return jnp.concatenate(outs, axis=0), s_bd


def _swa_block(sq, kcat, vcat, bias_ref, sinks_ref, valid_t):
    half = _iota((1, LANES), 1) < SWA_HEAD_DIM
    top_rows = _iota((LANES, 1), 0) < SWA_HEAD_DIM
    k_roll = pltpu.roll(kcat, SWA_HEAD_DIM, 1)
    v_t = jnp.transpose(vcat)
    zeros_v = jnp.zeros((SWA_HEAD_DIM, 2 * WINDOW), F32)
    cols = []
    for kv in range(SWA_KV_HEADS):
        kk = jnp.where(half, kcat, k_roll) if kv == 0 else jnp.where(half, k_roll, kcat)
        q_parts = []
        for c in (2 * kv, 2 * kv + 1):
            qc = sq[:, c * LANES:(c + 1) * LANES]
            q_parts.append(jnp.where(half, qc, 0.0))
            q_parts.append(jnp.where(half, 0.0, qc))
        q_st = jnp.concatenate(q_parts, axis=0).astype(BF16)
        s = _dot_nt(kk.astype(BF16), q_st) * (SWA_HEAD_DIM ** -0.5) + bias_ref[kv]
        s = jnp.where(valid_t, s, NEG_INF)
        sink = jnp.concatenate(
            [jnp.full((1, WINDOW), sinks_ref[kv * SWA_GROUP + g], F32) for g in range(SWA_GROUP)], axis=1)
        m = jnp.maximum(jnp.max(s, axis=0, keepdims=True), sink)
        pr = jnp.exp(s - m)
        inv = 1.0 / (jnp.sum(pr, axis=0, keepdims=True) + jnp.exp(sink - m))
        pb = pr.astype(BF16)
        vk = v_t[kv * SWA_HEAD_DIM:(kv + 1) * SWA_HEAD_DIM]
        vv_t = jnp.concatenate([jnp.concatenate([vk, zeros_v], axis=1),
                                jnp.concatenate([zeros_v, vk], axis=1)], axis=0).astype(BF16)
        for pair in range(SWA_GROUP // 2):
            ce = slice(2 * pair * WINDOW, (2 * pair + 1) * WINDOW)
            co = slice((2 * pair + 1) * WINDOW, (2 * pair + 2) * WINDOW)
            p2_t = jnp.concatenate([pb[:, ce], pb[:, co]], axis=0)
            o2_t = _dot(vv_t, p2_t)
            o2_t = o2_t * jnp.where(top_rows, inv[:, ce], inv[:, co])
            cols.append(jnp.transpose(o2_t))
    return jnp.concatenate(cols, axis=1)


def _mixer_kernel(sinks_ref, x_ref, s0_ref, k0_ref, v0_ref, base0_ref, bias_ref,
                  g_mix_ref, w_in_ref, w_a_up_ref, b_a_ref, g_gla_ref, g_swa_ref, w_out_ref,
                  g_ffn_ref, w_r_ref, b_r_ref,
                  xmid_ref, hp_ref, topi_ref, gate_ref, rank_ref, sout_ref, kout_ref, vout_ref, cnt_ref,
                  s_scr, k_scr, v_scr, base_scr, *, n_lead_pad, prev_valid_from):
    b_id, j = pl.program_id(0), pl.program_id(1)
    tm = x_ref.shape[1]

    @pl.when(j == 0)
    def _():
        s_scr[...] = s0_ref[...]
        k_scr[...] = k0_ref[...]
        v_scr[...] = v0_ref[...]

    @pl.when((j == 0) & (b_id == 0))
    def _():
        base_scr[...] = base0_ref[...]

    x = x_ref[0]
    p = _project(x, g_mix_ref[...], w_in_ref[...], w_a_up_ref[...], b_a_ref[...])

    o_gla, s_new = _gla_chunks(p, j * tm, s_scr[...], n_lead_pad)
    s_scr[...] = s_new

    kj = _iota((2 * WINDOW, WINDOW), 0)
    qi = _iota((2 * WINDOW, WINDOW), 1)
    band = (kj > qi) & (kj <= qi + WINDOW)
    o_parts = []
    for sb in range(tm // WINDOW):
        rs = slice(sb * WINDOW, (sb + 1) * WINDOW)
        k_blk, v_blk = p["sk"][rs], p["sv"][rs]
        k_prev = k_scr[...] if sb == 0 else p["sk"][(sb - 1) * WINDOW:sb * WINDOW]
        v_prev = v_scr[...] if sb == 0 else p["sv"][(sb - 1) * WINDOW:sb * WINDOW]
        valid = band
        if sb == 0 and prev_valid_from:
            first = jnp.where(j == 0, prev_valid_from, 0)
            valid = band & (kj >= first)
        valid = jnp.concatenate([valid] * SWA_GROUP, axis=1)
        o_parts.append(_swa_block(p["sq"][rs], jnp.concatenate([k_prev, k_blk], axis=0),
                                  jnp.concatenate([v_prev, v_blk], axis=0), bias_ref, sinks_ref, valid))
    o_swa = jnp.concatenate(o_parts, axis=0)
    k_scr[...] = p["sk"][tm - WINDOW:tm]
    v_scr[...] = p["sv"][tm - WINDOW:tm]

    x_mid, hp, topi, gate8, rank8, base = _tail(
        x, o_gla, p["gr"], o_swa, g_gla_ref[...], g_swa_ref[...], w_out_ref[...],
        g_ffn_ref[...], w_r_ref[...], b_r_ref[...], base_scr[...])
    base_scr[...] = base
    xmid_ref[0] = x_mid
    _store_slabs(hp_ref, hp)
    topi_ref[...] = topi
    gate_ref[...] = gate8
    rank_ref[...] = rank8
    sout_ref[0] = s_new
    kout_ref[0] = p["sk"][tm - WINDOW:tm]
    vout_ref[0] = p["sv"][tm - WINDOW:tm]
    cnt_ref[...] = base


def _full_spec(shape):
    nd = len(shape)
    return pl.BlockSpec(shape, lambda *_: (0,) * nd)


def _mixer_call(x, s0, k0, v0, base0, wts, tm, n_lead_pad, prev_valid_from, hp_rows):
    B, L, _ = x.shape
    nj = L // tm
    T = B * L
    weight_args = (wts["bias"], wts["g_mix"], wts["w_in"], wts["w_a_up"], wts["b_a"], wts["g_gla"],
                   wts["g_swa"], wts["w_out"], wts["g_ffn"], wts["w_r"], wts["b_r"])
    in_specs = [
        pl.BlockSpec(memory_space=pltpu.SMEM),
        pl.BlockSpec((1, tm, D_MODEL), lambda b, j: (b, j, 0)),
        _full_spec(s0.shape), _full_spec(k0.shape), _full_spec(v0.shape), _full_spec(base0.shape),
    ] + [_full_spec(w.shape) for w in weight_args]
    tok_spec = pl.BlockSpec((8, tm), lambda b, j: (0, b * nj + j))
    out_specs = [
        pl.BlockSpec((1, tm, D_MODEL), lambda b, j: (b, j, 0)),
        pl.BlockSpec((tm * SLAB, LANES), lambda b, j: (b * nj + j, 0)),
        tok_spec, tok_spec, tok_spec,
        pl.BlockSpec((1, GLA_QK, GLA_V), lambda b, j: (b, 0, 0)),
        pl.BlockSpec((1, WINDOW, SWA_KV), lambda b, j: (b, 0, 0)),
        pl.BlockSpec((1, WINDOW, SWA_KV), lambda b, j: (b, 0, 0)),
        _full_spec((N_EXPERTS, LANES)),
    ]
    out_shape = [
        jax.ShapeDtypeStruct((B, L, D_MODEL), F32),
        jax.ShapeDtypeStruct((hp_rows * SLAB, LANES), F32),
        jax.ShapeDtypeStruct((8, T), jnp.int32),
        jax.ShapeDtypeStruct((8, T), F32),
        jax.ShapeDtypeStruct((8, T), jnp.int32),
        jax.ShapeDtypeStruct((B, GLA_QK, GLA_V), F32),
        jax.ShapeDtypeStruct((B, WINDOW, SWA_KV), F32),
        jax.ShapeDtypeStruct((B, WINDOW, SWA_KV), F32),
        jax.ShapeDtypeStruct((N_EXPERTS, LANES), F32),
    ]
    kern = functools.partial(_mixer_kernel, n_lead_pad=n_lead_pad, prev_valid_from=prev_valid_from)
    return pl.pallas_call(
        kern,
        grid=(B, nj),
        in_specs=in_specs,
        out_specs=out_specs,
        out_shape=out_shape,
        scratch_shapes=[pltpu.VMEM((GLA_QK, GLA_V), F32), pltpu.VMEM((WINDOW, SWA_KV), F32),
                        pltpu.VMEM((WINDOW, SWA_KV), F32), pltpu.VMEM((N_EXPERTS, LANES), F32)],
        compiler_params=pltpu.CompilerParams(dimension_semantics=("arbitrary", "arbitrary"),
                                             vmem_limit_bytes=VMEM_LIMIT),
        name="mixer",
    )(wts["sinks"], x, s0, k0, v0, base0, *weight_args)


def _decode_kernel(sinks_ref, x_ref, st_ref, ck_ref, cv_ref, base0_ref, bias_ref,
                   g_mix_ref, w_in_ref, w_a_up_ref, b_a_ref, g_gla_ref, g_swa_ref, w_out_ref,
                   g_ffn_ref, w_r_ref, b_r_ref,
                   xmid_ref, hp_ref, topi_ref, gate_ref, rank_ref, sto_ref, cko_ref, cvo_ref, cnt_ref,
                   at_scr, kt_scr, qt_scr, gv_scr, gr_scr, sq_scr, sk_scr, sv_scr, og_scr, os_scr):
    i = pl.program_id(0)
    n_seq = x_ref.shape[0]

    @pl.when(i == 0)
    def _():
        p = _project(x_ref[...], g_mix_ref[...], w_in_ref[...], w_a_up_ref[...], b_a_ref[...])
        at_scr[...] = jnp.transpose(jnp.exp(p["log_a"]))
        kt_scr[...] = jnp.transpose(p["gk"])
        qt_scr[...] = jnp.transpose(p["gq"])
        gv_scr[...] = p["gv"]
        gr_scr[...] = p["gr"]
        sq_scr[...] = p["sq"]
        sk_scr[...] = p["sk"]
        sv_scr[...] = p["sv"]

    lane_seq = _iota((GLA_QK, n_seq), 1)
    half = _iota((1, LANES), 1) < SWA_HEAD_DIM
    row_id = _iota((WINDOW, SWA_KV), 0)
    head_diag = (_iota((16, SWA_Q), 1) // SWA_HEAD_DIM) == _iota((16, SWA_Q), 0)
    sink_col = jnp.concatenate(
        [jnp.full((1, 1), sinks_ref[h], F32) for h in range(SWA_HEADS)] + [jnp.zeros((8, 1), F32)], axis=0)

    def per_seq(sl, carry):
        s = i * DEC_SB + sl
        sel = lane_seq == s
        a_c = jnp.sum(jnp.where(sel, at_scr[...], 0.0), axis=1, keepdims=True)
        k_c = jnp.sum(jnp.where(sel, kt_scr[...], 0.0), axis=1, keepdims=True)
        q_c = jnp.sum(jnp.where(sel, qt_scr[...], 0.0), axis=1, keepdims=True)
        st = st_ref[sl].reshape(GLA_QK, GLA_DV)
        v_row = gv_scr[pl.ds(s, 1), :]
        v_b = jnp.concatenate(
            [jnp.broadcast_to(v_row[:, h * GLA_DV:(h + 1) * GLA_DV], (GLA_DK, GLA_DV))
             for h in range(GLA_HEADS)], axis=0)
        st_new = a_c * st + k_c * v_b
        sto_ref[sl] = st_new.reshape(GLA_HEADS, GLA_DK, GLA_DV)
        t = q_c * st_new
        og_scr[pl.ds(s, 1), :] = jnp.concatenate(
            [jnp.sum(t[h * GLA_DK:(h + 1) * GLA_DK], axis=0, keepdims=True) for h in range(GLA_HEADS)],
            axis=1)

        k_new = sk_scr[pl.ds(s, 1), :]
        v_new = sv_scr[pl.ds(s, 1), :]
        kn = jnp.where(row_id == WINDOW - 1, k_new, pltpu.roll(ck_ref[sl], WINDOW - 1, 0))
        vn = jnp.where(row_id == WINDOW - 1, v_new, pltpu.roll(cv_ref[sl], WINDOW - 1, 0))
        cko_ref[sl] = kn
        cvo_ref[sl] = vn
        kr, vr = pltpu.roll(kn, SWA_HEAD_DIM, 1), pltpu.roll(vn, SWA_HEAD_DIM, 1)
        k0, k1 = jnp.where(half, kn, kr), jnp.where(half, kr, kn)
        v0, v1 = jnp.where(half, vn, vr), jnp.where(half, vr, vn)
        kw = jnp.concatenate([k0, k0, k1, k1], axis=1).astype(BF16)
        vw = jnp.concatenate([v0, v0, v1, v1], axis=1).astype(BF16)
        q_row = sq_scr[pl.ds(s, 1), :]
        qm = jnp.where(head_diag, jnp.broadcast_to(q_row, (16, SWA_Q)), 0.0).astype(BF16)
        sc = _dot_nt(qm, kw) * (SWA_HEAD_DIM ** -0.5) + bias_ref[...]
        m = jnp.maximum(jnp.max(sc, axis=1, keepdims=True), sink_col)
        pr = jnp.exp(sc - m)
        inv = 1.0 / (jnp.sum(pr, axis=1, keepdims=True) + jnp.exp(sink_col - m))
        ow = _dot(pr.astype(BF16), vw) * inv
        os_scr[pl.ds(s, 1), :] = jnp.sum(jnp.where(head_diag, ow, 0.0), axis=0, keepdims=True)
        return carry

    lax.fori_loop(0, DEC_SB, per_seq, 0)

    @pl.when(i == pl.num_programs(0) - 1)
    def _():
        x_mid, hp, topi, gate8, rank8, base = _tail(
            x_ref[...], og_scr[...], gr_scr[...], os_scr[...], g_gla_ref[...], g_swa_ref[...],
            w_out_ref[...], g_ffn_ref[...], w_r_ref[...], b_r_ref[...], base0_ref[...])
        xmid_ref[...] = x_mid
        _store_slabs(hp_ref, hp)
        topi_ref[...] = topi
        gate_ref[...] = gate8
        rank_ref[...] = rank8
        cnt_ref[...] = base


def _decode_call(xs, state, ck, cv, base0, bias_dec, wts):
    n_seq = xs.shape[0]
    nb = n_seq // DEC_SB
    weight_args = (wts["g_mix"], wts["w_in"], wts["w_a_up"], wts["b_a"], wts["g_gla"],
                   wts["g_swa"], wts["w_out"], wts["g_ffn"], wts["w_r"], wts["b_r"])
    in_specs = [
        pl.BlockSpec(memory_space=pltpu.SMEM),
        _full_spec(xs.shape),
        pl.BlockSpec((DEC_SB, GLA_HEADS, GLA_DK, GLA_DV), lambda i: (i, 0, 0, 0)),
        pl.BlockSpec((DEC_SB, WINDOW, SWA_KV), lambda i: (i, 0, 0)),
        pl.BlockSpec((DEC_SB, WINDOW, SWA_KV), lambda i: (i, 0, 0)),
        _full_spec(base0.shape), _full_spec(bias_dec.shape),
    ] + [_full_spec(w.shape) for w in weight_args]
    out_specs = [
        _full_spec((n_seq, D_MODEL)),
        _full_spec((n_seq * SLAB, LANES)),
        _full_spec((8, n_seq)), _full_spec((8, n_seq)), _full_spec((8, n_seq)),
        pl.BlockSpec((DEC_SB, GLA_HEADS, GLA_DK, GLA_DV), lambda i: (i, 0, 0, 0)),
        pl.BlockSpec((DEC_SB, WINDOW, SWA_KV), lambda i: (i, 0, 0)),
        pl.BlockSpec((DEC_SB, WINDOW, SWA_KV), lambda i: (i, 0, 0)),
        _full_spec((N_EXPERTS, LANES)),
    ]
    out_shape = [
        jax.ShapeDtypeStruct((n_seq, D_MODEL), F32),
        jax.ShapeDtypeStruct((n_seq * SLAB, LANES), F32),
        jax.ShapeDtypeStruct((8, n_seq), jnp.int32),
        jax.ShapeDtypeStruct((8, n_seq), F32),
        jax.ShapeDtypeStruct((8, n_seq), jnp.int32),
        jax.ShapeDtypeStruct(state.shape, F32),
        jax.ShapeDtypeStruct(ck.shape, F32),
        jax.ShapeDtypeStruct(cv.shape, F32),
        jax.ShapeDtypeStruct((N_EXPERTS, LANES), F32),
    ]
    scratch = [pltpu.VMEM((GLA_QK, n_seq), F32)] * 3 + [
        pltpu.VMEM((n_seq, GLA_V), F32), pltpu.VMEM((n_seq, GLA_V), F32), pltpu.VMEM((n_seq, SWA_Q), F32),
        pltpu.VMEM((n_seq, SWA_KV), F32), pltpu.VMEM((n_seq, SWA_KV), F32),
        pltpu.VMEM((n_seq, GLA_V), F32), pltpu.VMEM((n_seq, SWA_Q), F32)]
    return pl.pallas_call(
        _decode_kernel,
        grid=(nb,),
        in_specs=in_specs,
        out_specs=out_specs,
        out_shape=out_shape,
        scratch_shapes=scratch,
        compiler_params=pltpu.CompilerParams(dimension_semantics=("arbitrary",),
                                             vmem_limit_bytes=VMEM_LIMIT),
        name="decode",
    )(wts["sinks"], xs, state, ck, cv, base0, bias_dec, *weight_args)


SC_CORES = 2
SC_SUBCORES = 16
SC_WORKERS = SC_CORES * SC_SUBCORES
SC_SCATTER_ROWS = 32
SC_GATHER_ROWS = 24


def _sc_mesh():
    return plsc.VectorSubcoreMesh(core_axis_name="c", subcore_axis_name="s")


def _sc_worker_id():
    return lax.axis_index("s") * SC_CORES + lax.axis_index("c")


def _sc_scatter_rows(src_p, src_s, idx_p, idx_s, n_out):
    rows = SC_SCATTER_ROWS
    n_chunks = idx_p.shape[0] // SC_WORKERS
    n_s, _, rows_s = idx_s.shape
    assert n_chunks * SC_WORKERS == idx_p.shape[0] and n_chunks % 2 == 0 and n_s <= SC_WORKERS

    @functools.partial(
        pl.kernel, mesh=_sc_mesh(),
        out_type=jax.ShapeDtypeStruct((n_out, SLAB, LANES), F32),
        scratch_types=[pltpu.VMEM((2, TOP_K, rows), jnp.int32), pltpu.VMEM((2, rows, SLAB, LANES), F32),
                       pltpu.VMEM((TOP_K, rows_s), jnp.int32), pltpu.VMEM((rows_s, SLAB, LANES), F32),
                       pltpu.SemaphoreType.DMA((2,)), pltpu.SemaphoreType.DMA((2,))])
    def scatter_rows(srcp_hbm, srcs_hbm, idxp_hbm, idxs_hbm, out_hbm, idx_v, rows_v, idxs_v, rowss_v, lsem, ssem):
        wid = _sc_worker_id()

        def loads(c, b):
            g = wid * n_chunks + c
            return (pltpu.make_async_copy(idxp_hbm.at[g], idx_v.at[b], lsem.at[b]),
                    pltpu.make_async_copy(srcp_hbm.at[pl.ds(pl.multiple_of(g * rows, 8), rows)], rows_v.at[b],
                                          lsem.at[b]))

        def scatters(b):
            return [pltpu.make_async_copy(rows_v.at[b], out_hbm.at[idx_v.at[b, k]], ssem.at[b])
                    for k in range(TOP_K)]

        for d in loads(0, 0):
            d.start()

        @pl.loop(0, n_chunks, step=2)
        def _(c0):
            for b in range(2):
                c = c0 + b
                for d in loads(c, b):
                    d.wait()

                @pl.when(c >= 1)
                def _():
                    for d in scatters(1 - b):
                        d.wait()

                @pl.when(c + 1 < n_chunks)
                def _():
                    for d in loads(c + 1, 1 - b):
                        d.start()

                for d in scatters(b):
                    d.start()

        for d in scatters((n_chunks - 1) % 2):
            d.wait()

        @pl.when(wid < n_s)
        def _():
            pltpu.sync_copy(idxs_hbm.at[wid], idxs_v)
            pltpu.sync_copy(srcs_hbm.at[pl.ds(pl.multiple_of(wid * rows_s, 8), rows_s)], rowss_v)
            for k in range(TOP_K):
                pltpu.sync_copy(rowss_v, out_hbm.at[idxs_v.at[k]])

    return scatter_rows(src_p, src_s, idx_p, idx_s)


def _sc_gather_rows(src3, idx2):
    rows = SC_GATHER_ROWS
    n_chunks = idx2.shape[0] // SC_WORKERS
    assert n_chunks * SC_WORKERS == idx2.shape[0] and idx2.shape[1] == rows and n_chunks % 2 == 0

    @functools.partial(
        pl.kernel, mesh=_sc_mesh(),
        out_type=jax.ShapeDtypeStruct((idx2.shape[0] * rows, SLAB, LANES), F32),
        scratch_types=[pltpu.VMEM((2, rows), jnp.int32), pltpu.VMEM((2, rows, SLAB, LANES), F32),
                       pltpu.SemaphoreType.DMA((2,)), pltpu.SemaphoreType.DMA((2,))])
    def gather_rows(src_hbm, idx_hbm, out_hbm, idx_v, rows_v, gsem, wsem):
        wid = _sc_worker_id()

        def gather(b):
            return pltpu.make_async_copy(src_hbm.at[idx_v.at[b]], rows_v.at[b], gsem.at[b])

        def write(c, b):
            base = pl.multiple_of((wid * n_chunks + c) * rows, 8)
            return pltpu.make_async_copy(rows_v.at[b], out_hbm.at[pl.ds(base, rows)], wsem.at[b])

        pltpu.sync_copy(idx_hbm.at[wid * n_chunks], idx_v.at[0])
        gather(0).start()

        @pl.loop(0, n_chunks, step=2)
        def _(c0):
            for b in range(2):
                c = c0 + b

                @pl.when(c + 1 < n_chunks)
                def _():
                    @pl.when(c >= 1)
                    def _():
                        write(c - 1, 1 - b).wait()
                    pltpu.sync_copy(idx_hbm.at[wid * n_chunks + c + 1], idx_v.at[1 - b])
                    gather(1 - b).start()

                gather(b).wait()
                write(c, b).start()

        write(n_chunks - 2, 0).wait()
        write(n_chunks - 1, 1).wait()

    return gather_rows(src3, idx2)


def _ffn_kernel(blk_e_ref, nused_ref, x_ref, wu_ref, bu_ref, wd_ref, bd_ref, y_ref, xbf, actbf, wu_bf, wd_bf):
    i = pl.program_id(0)
    tm = MOE_TM
    n_tiles = D_FF // FF_TILE

    @pl.when(i < nused_ref[0])
    def _():
        @pl.when((i == 0) | (blk_e_ref[i] != blk_e_ref[jnp.maximum(i - 1, 0)]))
        def _():
            wu_bf[...] = wu_ref[0].astype(BF16)
            wd_bf[...] = wd_ref[0].astype(BF16)

        for c in range(SLAB):
            xbf[:, c * LANES:(c + 1) * LANES] = _load_slab_chunk(x_ref, tm, c, SLAB).astype(BF16)
        for n in range(n_tiles):
            gc = slice(n * FF_TILE, (n + 1) * FF_TILE)
            lc = slice(D_FF + n * FF_TILE, D_FF + (n + 1) * FF_TILE)
            g = jnp.minimum(_dot(xbf[...], wu_bf[:, gc]) + bu_ref[0, :, gc], SWIGLU_LIMIT)
            lin = jnp.clip(_dot(xbf[...], wu_bf[:, lc]) + bu_ref[0, :, lc], -SWIGLU_LIMIT, SWIGLU_LIMIT)
            actbf[:, gc] = (g * jax.nn.sigmoid(SWIGLU_ALPHA * g) * (lin + 1.0)).astype(BF16)
        for n in range(n_tiles):
            yc = slice(n * FF_TILE, (n + 1) * FF_TILE)
            y = _dot(actbf[...], wd_bf[:, yc]) + bd_ref[0, :, yc]
            for c in range(FF_TILE // LANES):
                y_ref[pl.ds(n * (FF_TILE // LANES) + c, tm, stride=SLAB), :] = y[:, c * LANES:(c + 1) * LANES]

    @pl.when(i >= nused_ref[0])
    def _():
        y_ref[...] = jnp.zeros_like(y_ref)


def _ffn_call(blk_e, nused, xs2, w_up, b_up, w_down, b_down):
    n_blocks = blk_e.shape[0]
    tm = MOE_TM
    row_blk = pl.BlockSpec((tm * SLAB, LANES), lambda i, be, nu: (i, 0))
    grid_spec = pltpu.PrefetchScalarGridSpec(
        num_scalar_prefetch=2,
        grid=(n_blocks,),
        in_specs=[
            row_blk,
            pl.BlockSpec((1, D_MODEL, 2 * D_FF), lambda i, be, nu: (be[i], 0, 0)),
            pl.BlockSpec((1, 1, 2 * D_FF), lambda i, be, nu: (be[i], 0, 0)),
            pl.BlockSpec((1, D_FF, D_MODEL), lambda i, be, nu: (be[i], 0, 0)),
            pl.BlockSpec((1, 1, D_MODEL), lambda i, be, nu: (be[i], 0, 0)),
        ],
        out_specs=row_blk,
        scratch_shapes=[pltpu.VMEM((tm, D_MODEL), BF16), pltpu.VMEM((tm, D_FF), BF16),
                        pltpu.VMEM((D_MODEL, 2 * D_FF), BF16), pltpu.VMEM((D_FF, D_MODEL), BF16)],
    )
    return pl.pallas_call(
        _ffn_kernel,
        grid_spec=grid_spec,
        out_shape=jax.ShapeDtypeStruct((n_blocks * tm * SLAB, LANES), F32),
        compiler_params=pltpu.CompilerParams(dimension_semantics=("arbitrary",),
                                             vmem_limit_bytes=VMEM_LIMIT),
        name="experts",
    )(blk_e, nused, xs2, w_up, b_up.reshape(N_EXPERTS, 1, 2 * D_FF), w_down, b_down.reshape(N_EXPERTS, 1, D_MODEL))


FF_TILE = 256


RING = 3


def _expert_kernel(blk_e_ref, nused_ref, tok0_ref, tok1_ref, tokn_ref, retp_ref, hp_hbm, wu_ref, bu_ref,
                   wd_ref, bd_ref, y_hbm, xbuf0, xbuf1, xbuf2, ybuf0, ybuf1, ybuf2, xbf, actbf, zbuf,
                   wu_bf, wd_bf, gsem, ssem, zsem):
    i = pl.program_id(0)
    tm = MOE_TM
    nused = nused_ref[0]
    n_tiles = D_FF // FF_TILE
    xbufs, ybufs = (xbuf0, xbuf1, xbuf2), (ybuf0, ybuf1, ybuf2)

    def gather(tok_ref, r, s):
        src = hp_hbm.at[pl.ds(pl.multiple_of(tok_ref[0, 0, r], SLAB), SLAB)]
        return pltpu.make_async_copy(src, xbufs[s].at[pl.ds(r * SLAB, SLAB)], gsem.at[s])

    def scatter(r, s):
        dst = y_hbm.at[pl.ds(pl.multiple_of(retp_ref[0, 0, r], SLAB), SLAB)]
        return pltpu.make_async_copy(ybufs[s].at[pl.ds(r * SLAB, SLAB)], dst, ssem.at[s])

    def wait_gather(s):
        pltpu.make_async_copy(hp_hbm.at[pl.ds(0, tm * SLAB)], xbufs[s], gsem.at[s]).wait()

    def wait_scatter(s):
        pltpu.make_async_copy(ybufs[s], y_hbm.at[pl.ds(0, tm * SLAB)], ssem.at[s]).wait()

    @pl.when(i == 0)
    def _():
        ybuf2[...] = jnp.zeros_like(ybuf2)
        zbuf[...] = jnp.zeros_like(zbuf)

        def issue(r, c):
            gather(tok0_ref, r, 0).start()
            gather(tok1_ref, r, 1).start()
            return c
        lax.fori_loop(0, tm, issue, 0)

    def compute(s):
        nxt = (s + 2) % RING
        wait_gather(s)

        @pl.when(i >= 2)
        def _():
            wait_scatter(s)

        @pl.when((i == 0) | (blk_e_ref[i] != blk_e_ref[jnp.maximum(i - 1, 0)]))
        def _():
            wu_bf[...] = wu_ref[0].astype(BF16)
            wd_bf[...] = wd_ref[0].astype(BF16)

        for c in range(SLAB):
            xbf[:, c * LANES:(c + 1) * LANES] = _load_slab_chunk(xbufs[s], tm, c, SLAB).astype(BF16)
        for r in range(tm):
            gather(tokn_ref, r, nxt).start(priority=r % 2)
            scatter(r, nxt).start(priority=r % 2)
        for n in range(n_tiles):
            gc = slice(n * FF_TILE, (n + 1) * FF_TILE)
            lc = slice(D_FF + n * FF_TILE, D_FF + (n + 1) * FF_TILE)
            g = jnp.minimum(_dot(xbf[...], wu_bf[:, gc]) + bu_ref[0, :, gc], SWIGLU_LIMIT)
            lin = jnp.clip(_dot(xbf[...], wu_bf[:, lc]) + bu_ref[0, :, lc], -SWIGLU_LIMIT, SWIGLU_LIMIT)
            actbf[:, gc] = (g * jax.nn.sigmoid(SWIGLU_ALPHA * g) * (lin + 1.0)).astype(BF16)
        for n in range(n_tiles):
            yc = slice(n * FF_TILE, (n + 1) * FF_TILE)
            y = _dot(actbf[...], wd_bf[:, yc]) + bd_ref[0, :, yc]
            for c in range(FF_TILE // LANES):
                ybufs[s][pl.ds(n * (FF_TILE // LANES) + c, tm, stride=SLAB), :] = y[:, c * LANES:(c + 1) * LANES]

    def drain(s):
        wait_gather(s)
        wait_gather((s + 1) % RING)
        wait_scatter(s)
        wait_scatter((s + 1) % RING)
        last = (s + 2) % RING

        def issue(r, c):
            scatter(r, last).start()
            return c
        lax.fori_loop(0, tm, issue, 0)
        wait_scatter(last)

    for s in range(RING):
        @pl.when((i < nused) & (i % RING == s))
        def _():
            compute(s)

        @pl.when((i == nused) & (i % RING == s))
        def _():
            drain(s)

    @pl.when(i >= nused)
    def _():
        zc = pltpu.make_async_copy(zbuf, y_hbm.at[pl.ds(pl.multiple_of((tm + i * tm) * SLAB, SLAB), tm * SLAB)], zsem)
        zc.start()
        zc.wait()


def _expert_call(blk_e, nused, tok_sorted, ret_sorted, hp_all, w_up, b_up, w_down, b_down, n_slots):
    n_blocks = blk_e.shape[0]
    tm = MOE_TM
    n_rows = n_blocks * tm + tm
    ret_tab = (jnp.concatenate([n_slots + jnp.arange(tm, dtype=jnp.int32), ret_sorted]) * SLAB
               ).reshape(n_blocks + 1, 1, tm)
    tok_tab = (tok_sorted * SLAB).reshape(n_blocks, 1, tm)
    smem_blk = functools.partial(pl.BlockSpec, (1, 1, tm), memory_space=pltpu.SMEM)
    grid_spec = pltpu.PrefetchScalarGridSpec(
        num_scalar_prefetch=2,
        grid=(n_blocks,),
        in_specs=[
            smem_blk(lambda i, be, nu: (0, 0, 0)),
            smem_blk(lambda i, be, nu: (1, 0, 0)),
            smem_blk(lambda i, be, nu: (jnp.minimum(i + 2, n_blocks - 1), 0, 0)),
            smem_blk(lambda i, be, nu: (i, 0, 0)),
            pl.BlockSpec(memory_space=pl.ANY),
            pl.BlockSpec((1, D_MODEL, 2 * D_FF), lambda i, be, nu: (be[i], 0, 0)),
            pl.BlockSpec((1, 1, 2 * D_FF), lambda i, be, nu: (be[i], 0, 0)),
            pl.BlockSpec((1, D_FF, D_MODEL), lambda i, be, nu: (be[i], 0, 0)),
            pl.BlockSpec((1, 1, D_MODEL), lambda i, be, nu: (be[i], 0, 0)),
        ],
        out_specs=pl.BlockSpec(memory_space=pl.ANY),
        scratch_shapes=[pltpu.VMEM((tm * SLAB, LANES), F32)] * (2 * RING) + [
                        pltpu.VMEM((tm, D_MODEL), BF16), pltpu.VMEM((tm, D_FF), BF16),
                        pltpu.VMEM((tm * SLAB, LANES), F32),
                        pltpu.VMEM((D_MODEL, 2 * D_FF), BF16), pltpu.VMEM((D_FF, D_MODEL), BF16),
                        pltpu.SemaphoreType.DMA((RING,)), pltpu.SemaphoreType.DMA((RING,)),
                        pltpu.SemaphoreType.DMA],
    )
    return pl.pallas_call(
        _expert_kernel,
        grid_spec=grid_spec,
        out_shape=jax.ShapeDtypeStruct((n_rows * SLAB, LANES), F32),
        compiler_params=pltpu.CompilerParams(dimension_semantics=("arbitrary",),
                                             vmem_limit_bytes=VMEM_LIMIT),
        name="experts",
    )(blk_e, nused, tok_tab, tok_tab, tok_tab, ret_tab, hp_all,
      w_up, b_up.reshape(N_EXPERTS, 1, 2 * D_FF), w_down, b_down.reshape(N_EXPERTS, 1, D_MODEL))


def _combine_kernel(ys_ref, xmid_ref, gate_ref, g_final_ref, y_ref):
    tm = xmid_ref.shape[0]
    gts = gate_ref[...]
    chunks = []
    for c in range(SLAB):
        acc = xmid_ref[:, c * LANES:(c + 1) * LANES]
        for k in range(TOP_K):
            acc = acc + _load_slab_chunk(ys_ref, tm, k * SLAB + c, TOP_K * SLAB) * gts[:, k:k + 1]
        chunks.append(acc)
    y_ref[...] = _rms(jnp.concatenate(chunks, axis=1), g_final_ref[...])


def _combine_call(ys4, row0, x_mid, gates, g_final, tm):
    T = x_mid.shape[0]
    blk0 = row0 // tm
    return pl.pallas_call(
        _combine_kernel,
        grid=(T // tm,),
        in_specs=[
            pl.BlockSpec((tm * TOP_K * SLAB, LANES), lambda i: (blk0 + i, 0)),
            pl.BlockSpec((tm, D_MODEL), lambda i: (i, 0)),
            pl.BlockSpec((tm, TOP_K), lambda i: (i, 0)),
            _full_spec((1, D_MODEL)),
        ],
        out_specs=pl.BlockSpec((tm, D_MODEL), lambda i: (i, 0)),
        out_shape=jax.ShapeDtypeStruct((T, D_MODEL), F32),
        compiler_params=pltpu.CompilerParams(dimension_semantics=("arbitrary",),
                                             vmem_limit_bytes=VMEM_LIMIT),
        name="combine",
    )(ys4, x_mid, gates, g_final)


def _t5_bucket(dist):
    n = jnp.maximum(dist, 0)
    max_exact = NUM_BUCKETS // 2
    nf = jnp.maximum(n, 1).astype(F32)
    large = max_exact + (jnp.log(nf / max_exact) / math.log(MAX_DISTANCE / max_exact)
                         * (NUM_BUCKETS - max_exact)).astype(jnp.int32)
    large = jnp.minimum(large, NUM_BUCKETS - 1)
    return jnp.where(n < max_exact, n, large)


def kernel(x_prompt, x_sample, state_gla, cache_swa_k, cache_swa_v, meta_tokens, rel_bias_table,
           g_mix, w_in, w_a_up, b_a, g_gla_out, g_swa_out, attn_sinks, w_out,
           g_ffn, w_router, b_router, w_up, b_up, w_down, b_down, g_final):
    assert g_mix.shape[0] == 1, "single-layer trunk"
    B, L, _ = x_prompt.shape
    n_seq = x_sample.shape[0]
    TP = B * L
    T_all = TP + n_seq

    wi = w_in[0]
    sizes = (GLA_QK, GLA_QK, GLA_V, GLA_V, GLA_LOWRANK, SWA_Q, SWA_KV, SWA_KV)
    offs = [0]
    for s in sizes:
        offs.append(offs[-1] + s)
    seg = [wi[:, offs[n]:offs[n + 1]] for n in range(8)]
    w_in_r = jnp.concatenate(
        seg[0:4] + seg[5:8] + [seg[4], jnp.zeros((D_MODEL, LANES - GLA_LOWRANK), F32)], axis=1).astype(BF16)
    w_a_pad = jnp.concatenate([w_a_up[0], jnp.zeros((LANES - GLA_LOWRANK, GLA_QK), F32)], axis=0).astype(BF16)
    wr_t = jnp.transpose(w_router[0])
    wr_hi = wr_t.astype(BF16)
    wr_lo = (wr_t - wr_hi.astype(F32)).astype(BF16)
    qi = jnp.arange(WINDOW)[:, None]
    kj = jnp.arange(2 * WINDOW)[None, :]
    buckets = jnp.arange(NUM_BUCKETS)
    table = rel_bias_table.astype(F32)
    oh_p = (_t5_bucket(qi - kj + WINDOW)[..., None] == buckets).astype(F32)
    bias_p = jnp.einsum("qkb,bh->hkq", oh_p, table, precision=lax.Precision.HIGHEST)
    bias_p = bias_p.reshape(SWA_KV_HEADS, SWA_GROUP, 2 * WINDOW, WINDOW).transpose(0, 2, 1, 3)
    bias_p = bias_p.reshape(SWA_KV_HEADS, 2 * WINDOW, SWA_GROUP * WINDOW)
    oh_d = (_t5_bucket(WINDOW - 1 - jnp.arange(WINDOW))[:, None] == buckets).astype(F32)
    bias_d = jnp.einsum("rb,bh->hr", oh_d, table, precision=lax.Precision.HIGHEST)
    bias_d = jnp.concatenate([bias_d, jnp.zeros((8, WINDOW), F32)], axis=0)
    wts = dict(
        sinks=attn_sinks[0].astype(F32), bias=bias_p,
        g_mix=g_mix[0][None], w_in=w_in_r, w_a_up=w_a_pad, b_a=b_a[0][None],
        g_gla=g_gla_out[0][None], g_swa=g_swa_out[0][None], w_out=w_out[0].astype(BF16),
        g_ffn=g_ffn[0][None], w_r=jnp.concatenate([wr_hi, wr_lo], axis=0), b_r=b_router[0][:, None],
    )

    x_pre = jnp.concatenate([jnp.zeros((WINDOW - N_META, D_MODEL), F32), meta_tokens.astype(F32)], axis=0)[None]
    zeros_s = jnp.zeros((GLA_QK, GLA_V), F32)
    zeros_kv = jnp.zeros((WINDOW, SWA_KV), F32)
    zeros_b = jnp.zeros((N_EXPERTS, LANES), F32)
    pre = _mixer_call(x_pre, zeros_s, zeros_kv, zeros_kv, zeros_b, wts, WINDOW, WINDOW - N_META, 0, WINDOW)
    s_meta, k_meta, v_meta = pre[5][0], pre[6][0], pre[7][0]

    (xmid_p, hp_p, topi_p, gate_p, rank_p, s_p, k_p, v_p, cnt_p) = _mixer_call(
        x_prompt, s_meta, k_meta, v_meta, zeros_b, wts, MIX_TM, 0, WINDOW - N_META, TP)

    (xmid_s, hp_s, topi_s, gate_s, rank_s, st_s, ck_s, cv_s, cnt_all) = _decode_call(
        x_sample[:, 0], state_gla[0], cache_swa_k[0].reshape(n_seq, WINDOW, SWA_KV),
        cache_swa_v[0].reshape(n_seq, WINDOW, SWA_KV), cnt_p, bias_d, wts)

    tm = MOE_TM
    n_slots = T_all * TOP_K
    n_blocks = -(-n_slots // tm) + N_EXPERTS
    top_e = jnp.concatenate([topi_p[:TOP_K], topi_s[:TOP_K]], axis=1)
    rank = jnp.concatenate([rank_p[:TOP_K], rank_s[:TOP_K]], axis=1)
    counts = cnt_all[:, 0].astype(jnp.int32)
    padded = (counts + tm - 1) // tm * tm
    pad_end = jnp.cumsum(padded)
    pad_start = pad_end - padded
    e_ids = jnp.arange(N_EXPERTS, dtype=jnp.int32)
    dest = jnp.sum(jnp.where(top_e[..., None] == e_ids, pad_start, 0), axis=-1) + rank
    n_pad = n_blocks * tm
    blk_e = jnp.minimum(jnp.sum(pad_end[None] <= (jnp.arange(n_blocks, dtype=jnp.int32) * tm)[:, None], axis=1),
                        N_EXPERTS - 1).astype(jnp.int32)
    nused = (pad_end[-1] // tm).astype(jnp.int32).reshape(1)

    sample_rows = 8
    idx_p = dest[:, :TP].reshape(TOP_K, TP // SC_SCATTER_ROWS, SC_SCATTER_ROWS).transpose(1, 0, 2)
    idx_s = dest[:, TP:].reshape(TOP_K, n_seq // sample_rows, sample_rows).transpose(1, 0, 2)
    xs3 = _sc_scatter_rows(hp_p.reshape(TP, SLAB, LANES), hp_s.reshape(n_seq, SLAB, LANES), idx_p, idx_s, n_pad)
    ys2 = _ffn_call(blk_e, nused, xs3.reshape(-1, LANES), w_up[0], b_up[0], w_down[0], b_down[0])
    slot_src = jnp.transpose(dest).reshape(n_slots // SC_GATHER_ROWS, SC_GATHER_ROWS)
    ys4 = _sc_gather_rows(ys2.reshape(-1, SLAB, LANES), slot_src).reshape(-1, LANES)

    gates = jnp.transpose(jnp.concatenate([gate_p[:TOP_K], gate_s[:TOP_K]], axis=1))
    gf = g_final[None]
    y_p = _combine_call(ys4, 0, xmid_p.reshape(TP, D_MODEL), gates[:TP], gf, MIX_TM)
    y_s = _combine_call(ys4, TP, xmid_s, gates[TP:], gf, n_seq)

    s_heads = jnp.stack([s_p[:, h * GLA_DK:(h + 1) * GLA_DK, h * GLA_DV:(h + 1) * GLA_DV]
                         for h in range(GLA_HEADS)], axis=1)
    return (y_p.reshape(B, L, D_MODEL), y_s.reshape(n_seq, 1, D_MODEL), s_heads[None],
            k_p.reshape(1, B, WINDOW, SWA_KV_HEADS, SWA_HEAD_DIM),
            v_p.reshape(1, B, WINDOW, SWA_KV_HEADS, SWA_HEAD_DIM),
            st_s[None], ck_s.reshape(1, n_seq, WINDOW, SWA_KV_HEADS, SWA_HEAD_DIM),
            cv_s.reshape(1, n_seq, WINDOW, SWA_KV_HEADS, SWA_HEAD_DIM))
```

```python
import functools
import math

import jax
import jax.numpy as jnp
from jax import lax
from jax.experimental import pallas as pl
from jax.experimental.pallas import tpu as pltpu
from jax.experimental.pallas import tpu_sc as plsc

D_MODEL = 1024
N_META = 16
GLA_HEADS = 4
GLA_DK = 64
GLA_DV = 128
GLA_LOWRANK = 16
GLA_GATE_TAU = 16.0
GLA_CHUNK = 64
SWA_HEADS = 8
SWA_KV_HEADS = 2
SWA_HEAD_DIM = 64
SWA_GROUP = SWA_HEADS // SWA_KV_HEADS
WINDOW = 128
NUM_BUCKETS = 32
MAX_DISTANCE = 128
N_EXPERTS = 32
TOP_K = 4
D_FF = 1024
SWIGLU_ALPHA = 1.702
SWIGLU_LIMIT = 7.0
RMS_EPS = 1e-6

GLA_QK = GLA_HEADS * GLA_DK
GLA_V = GLA_HEADS * GLA_DV
SWA_Q = SWA_HEADS * SWA_HEAD_DIM
SWA_KV = SWA_KV_HEADS * SWA_HEAD_DIM
LANES = 128
C_GQ, C_GK, C_GV, C_GR = 0, GLA_QK, 2 * GLA_QK, 2 * GLA_QK + GLA_V
C_SQ = C_GR + GLA_V
C_SK = C_SQ + SWA_Q
C_SV = C_SK + SWA_KV
C_GA = C_SV + SWA_KV
D_PROJ = C_GA + LANES

MIX_TM = 256
MOE_TM = 512
DEC_SB = 16
VMEM_LIMIT = 56 * 1024 * 1024

F32 = jnp.float32
BF16 = jnp.bfloat16
NEG_INF = float("-inf")


def _dot(a, b):
    return jnp.dot(a, b, preferred_element_type=F32)


def _dot_nt(a, b):
    return lax.dot_general(a, b, (((1,), (1,)), ((), ())), preferred_element_type=F32)


def _split3(x):
    hi = x.astype(BF16)
    r1 = x - hi.astype(F32)
    mid = r1.astype(BF16)
    lo = (r1 - mid.astype(F32)).astype(BF16)
    return hi, mid, lo


def _rms(x, g):
    return x * lax.rsqrt(jnp.mean(x * x, axis=-1, keepdims=True) + RMS_EPS) * g


def _iota(shape, dim):
    return lax.broadcasted_iota(jnp.int32, shape, dim)


SLAB = D_MODEL // LANES


def _store_slabs(ref, x):
    rows = x.shape[0]
    for c in range(SLAB):
        ref[pl.ds(c, rows, stride=SLAB), :] = x[:, c * LANES:(c + 1) * LANES]


def _load_slab_chunk(ref, rows, first, stride):
    return ref[pl.ds(first, rows, stride=stride), :]


def _project(x, g_mix, w_in, w_a_up, b_a):
    h = _rms(x, g_mix).astype(BF16)
    proj = _dot(h, w_in)
    ga = proj[:, C_GA:C_GA + LANES].astype(BF16)
    z = _dot(ga, w_a_up) + b_a
    log_a = -(jnp.maximum(-z, 0.0) + jnp.log1p(jnp.exp(-jnp.abs(z)))) / GLA_GATE_TAU
    return dict(
        gq=proj[:, C_GQ:C_GQ + GLA_QK] * (GLA_DK ** -0.5),
        gk=proj[:, C_GK:C_GK + GLA_QK],
        gv=proj[:, C_GV:C_GV + GLA_V],
        gr=proj[:, C_GR:C_GR + GLA_V],
        sq=proj[:, C_SQ:C_SQ + SWA_Q],
        sk=proj[:, C_SK:C_SK + SWA_KV],
        sv=proj[:, C_SV:C_SV + SWA_KV],
        log_a=log_a,
    )


def _tail(x, o_gla, gr, o_swa, g_gla_out, g_swa_out, w_out, g_ffn, w_r, b_r, base):
    tm = x.shape[0]
    gate = gr * jax.nn.sigmoid(gr)
    parts = []
    for h in range(GLA_HEADS):
        sl = slice(h * GLA_DV, (h + 1) * GLA_DV)
        parts.append(_rms(o_gla[:, sl], g_gla_out) * gate[:, sl])
    parts.append(_rms(o_swa, g_swa_out))
    o = jnp.concatenate(parts, axis=1).astype(BF16)
    x_mid = x + _dot(o, w_out)
    hp = _rms(x_mid, g_ffn)

    h1 = hp.astype(BF16)
    h2 = (hp - h1.astype(F32)).astype(BF16)
    la = _dot_nt(w_r, h1)
    lb = _dot_nt(w_r[0:N_EXPERTS], h2)
    logits = la[0:N_EXPERTS] + la[N_EXPERTS:2 * N_EXPERTS] + lb + b_r

    eidx = _iota((N_EXPERTS, tm), 0)
    vals, idxs, onehots = [], [], []
    l = logits
    for _ in range(TOP_K):
        m = jnp.max(l, axis=0, keepdims=True)
        sel = jnp.min(jnp.where(l == m, eidx, N_EXPERTS), axis=0, keepdims=True)
        oh = eidx == sel
        l = jnp.where(oh, NEG_INF, l)
        vals.append(m)
        idxs.append(sel)
        onehots.append(oh)
    es = [jnp.exp(v - vals[0]) for v in vals]
    denom = es[0] + es[1] + es[2] + es[3]
    gates = [e / denom for e in es]

    ohf = jnp.concatenate([oh.astype(F32) for oh in onehots], axis=0)
    upper = (_iota((tm, tm), 0) < _iota((tm, tm), 1)).astype(BF16)
    prefix = _dot(ohf.astype(BF16), upper)
    ranks = []
    for k in range(TOP_K):
        sl = slice(k * N_EXPERTS, (k + 1) * N_EXPERTS)
        ohk = ohf[sl]
        base_t = jnp.concatenate([base] * (tm // LANES), axis=1)
        ranks.append(jnp.sum(ohk * (prefix[sl] + base_t), axis=0, keepdims=True))
        base = base + jnp.sum(ohk, axis=1, keepdims=True)
    zi = jnp.zeros((8 - TOP_K, tm), jnp.int32)
    zf = jnp.zeros((8 - TOP_K, tm), F32)
    topi = jnp.concatenate(idxs + [zi], axis=0)
    gate8 = jnp.concatenate(gates + [zf], axis=0)
    rank8 = jnp.concatenate([r.astype(jnp.int32) for r in ranks] + [zi], axis=0)
    return x_mid, hp, topi, gate8, rank8, base


def _gla_chunks(p, row0, s_bd, n_lead_pad):
    tm = p["gq"].shape[0]
    nch = tm // GLA_CHUNK
    log_a = p["log_a"]
    if n_lead_pad:
        rows = row0 + _iota((tm, GLA_QK), 0)
        log_a = jnp.where(rows >= n_lead_pad, log_a, 0.0)
    ri, ci = _iota((tm, tm), 0), _iota((tm, tm), 1)
    tril = ((ri >= ci) & (ri // GLA_CHUNK == ci // GLA_CHUNK)).astype(BF16)
    hi, mid, lo = _split3(log_a)
    b_all = _dot(tril, hi) + _dot(tril, mid) + _dot(tril, lo)

    c64 = GLA_CHUNK
    kk_mask = (_iota((GLA_QK, GLA_QK), 0) // c64) == (_iota((GLA_QK, GLA_QK), 1) // GLA_DK)
    vv_mask = (_iota((GLA_QK, GLA_V), 0) // c64) == (_iota((GLA_QK, GLA_V), 1) // GLA_DV)
    ss_mask = (_iota((GLA_QK, GLA_V), 0) // GLA_DK) == (_iota((GLA_QK, GLA_V), 1) // GLA_DV)
    causal = (_iota((c64, GLA_QK), 0) >= (_iota((c64, GLA_QK), 1) % c64)).astype(F32)
    zpad_k = jnp.zeros((LANES - c64, GLA_QK), F32)
    zpad_v = jnp.zeros((LANES - c64, GLA_V), BF16)

    outs = []
    for c in range(nch):
        rs = slice(c * c64, (c + 1) * c64)
        b = b_all[rs]
        q, k, v = p["gq"][rs], p["gk"][rs], p["gv"][rs]
        b_last = b[c64 - 1:c64]
        qt = (q * jnp.exp(b)).astype(BF16)
        kt = k * jnp.exp(-b)
        kd = k * jnp.exp(b_last - b)
        vb = v.astype(BF16)
        k_bd = jnp.where(kk_mask, jnp.concatenate([kt] * GLA_HEADS, axis=0), 0.0).astype(BF16)
        a = (_dot_nt(qt, k_bd) * causal).astype(BF16)
        v_bd = jnp.where(vv_mask, jnp.concatenate([vb] * GLA_HEADS, axis=0), jnp.zeros((), BF16))
        outs.append(_dot(qt, s_bd.astype(BF16)) + _dot(a, v_bd))
        kd_t = jnp.transpose(jnp.concatenate([kd, zpad_k], axis=0)).astype(BF16)
        upd = _dot(kd_t, jnp.concatenate([vb, zpad_v], axis=0))
        decay = jnp.exp(jnp.transpose(jnp.broadcast_to(b_last, (LANES, GLA_QK))))
        s_bd = s_bd * jnp.concatenate([decay] * GLA_HEADS, axis=1) + jnp.where(ss_mask, upd, 0.0)
    return jnp.concatenate(outs, axis=0), s_bd


def _swa_block(sq, kcat, vcat, bias_ref, sinks_ref, valid_t):
    half = _iota((1, LANES), 1) < SWA_HEAD_DIM
    top_rows = _iota((LANES, 1), 0) < SWA_HEAD_DIM
    k_roll = pltpu.roll(kcat, SWA_HEAD_DIM, 1)
    v_t = jnp.transpose(vcat)
    zeros_v = jnp.zeros((SWA_HEAD_DIM, 2 * WINDOW), F32)
    cols = []
    for kv in range(SWA_KV_HEADS):
        kk = jnp.where(half, kcat, k_roll) if kv == 0 else jnp.where(half, k_roll, kcat)
        q_parts = []
        for c in (2 * kv, 2 * kv + 1):
            qc = sq[:, c * LANES:(c + 1) * LANES]
            q_parts.append(jnp.where(half, qc, 0.0))
            q_parts.append(jnp.where(half, 0.0, qc))
        q_st = jnp.concatenate(q_parts, axis=0).astype(BF16)
        s = _dot_nt(kk.astype(BF16), q_st) * (SWA_HEAD_DIM ** -0.5) + bias_ref[kv]
        s = jnp.where(valid_t, s, NEG_INF)
        sink = jnp.concatenate(
            [jnp.full((1, WINDOW), sinks_ref[kv * SWA_GROUP + g], F32) for g in range(SWA_GROUP)], axis=1)
        m = jnp.maximum(jnp.max(s, axis=0, keepdims=True), sink)
        pr = jnp.exp(s - m)
        inv = 1.0 / (jnp.sum(pr, axis=0, keepdims=True) + jnp.exp(sink - m))
        pb = pr.astype(BF16)
        vk = v_t[kv * SWA_HEAD_DIM:(kv + 1) * SWA_HEAD_DIM]
        vv_t = jnp.concatenate([jnp.concatenate([vk, zeros_v], axis=1),
                                jnp.concatenate([zeros_v, vk], axis=1)], axis=0).astype(BF16)
        for pair in range(SWA_GROUP // 2):
            ce = slice(2 * pair * WINDOW, (2 * pair + 1) * WINDOW)
            co = slice((2 * pair + 1) * WINDOW, (2 * pair + 2) * WINDOW)
            p2_t = jnp.concatenate([pb[:, ce], pb[:, co]], axis=0)
            o2_t = _dot(vv_t, p2_t)
            o2_t = o2_t * jnp.where(top_rows, inv[:, ce], inv[:, co])
            cols.append(jnp.transpose(o2_t))
    return jnp.concatenate(cols, axis=1)


def _mixer_kernel(sinks_ref, x_ref, s0_ref, k0_ref, v0_ref, base0_ref, bias_ref,
                  g_mix_ref, w_in_ref, w_a_up_ref, b_a_ref, g_gla_ref, g_swa_ref, w_out_ref,
                  g_ffn_ref, w_r_ref, b_r_ref,
                  xmid_ref, hp_ref, topi_ref, gate_ref, rank_ref, sout_ref, kout_ref, vout_ref, cnt_ref,
                  s_scr, k_scr, v_scr, base_scr, *, n_lead_pad, prev_valid_from):
    b_id, j = pl.program_id(0), pl.program_id(1)
    tm = x_ref.shape[1]

    @pl.when(j == 0)
    def _():
        s_scr[...] = s0_ref[...]
        k_scr[...] = k0_ref[...]
        v_scr[...] = v0_ref[...]

    @pl.when((j == 0) & (b_id == 0))
    def _():
        base_scr[...] = base0_ref[...]

    x = x_ref[0]
    p = _project(x, g_mix_ref[...], w_in_ref[...], w_a_up_ref[...], b_a_ref[...])

    o_gla, s_new = _gla_chunks(p, j * tm, s_scr[...], n_lead_pad)
    s_scr[...] = s_new

    kj = _iota((2 * WINDOW, WINDOW), 0)
    qi = _iota((2 * WINDOW, WINDOW), 1)
    band = (kj > qi) & (kj <= qi + WINDOW)
    o_parts = []
    for sb in range(tm // WINDOW):
        rs = slice(sb * WINDOW, (sb + 1) * WINDOW)
        k_blk, v_blk = p["sk"][rs], p["sv"][rs]
        k_prev = k_scr[...] if sb == 0 else p["sk"][(sb - 1) * WINDOW:sb * WINDOW]
        v_prev = v_scr[...] if sb == 0 else p["sv"][(sb - 1) * WINDOW:sb * WINDOW]
        valid = band
        if sb == 0 and prev_valid_from:
            first = jnp.where(j == 0, prev_valid_from, 0)
            valid = band & (kj >= first)
        valid = jnp.concatenate([valid] * SWA_GROUP, axis=1)
        o_parts.append(_swa_block(p["sq"][rs], jnp.concatenate([k_prev, k_blk], axis=0),
                                  jnp.concatenate([v_prev, v_blk], axis=0), bias_ref, sinks_ref, valid))
    o_swa = jnp.concatenate(o_parts, axis=0)
    k_scr[...] = p["sk"][tm - WINDOW:tm]
    v_scr[...] = p["sv"][tm - WINDOW:tm]

    x_mid, hp, topi, gate8, rank8, base = _tail(
        x, o_gla, p["gr"], o_swa, g_gla_ref[...], g_swa_ref[...], w_out_ref[...],
        g_ffn_ref[...], w_r_ref[...], b_r_ref[...], base_scr[...])
    base_scr[...] = base
    xmid_ref[0] = x_mid
    _store_slabs(hp_ref, hp)
    topi_ref[...] = topi
    gate_ref[...] = gate8
    rank_ref[...] = rank8
    sout_ref[0] = s_new
    kout_ref[0] = p["sk"][tm - WINDOW:tm]
    vout_ref[0] = p["sv"][tm - WINDOW:tm]
    cnt_ref[...] = base


def _full_spec(shape):
    nd = len(shape)
    return pl.BlockSpec(shape, lambda *_: (0,) * nd)


def _mixer_call(x, s0, k0, v0, base0, wts, tm, n_lead_pad, prev_valid_from, hp_rows):
    B, L, _ = x.shape
    nj = L // tm
    T = B * L
    weight_args = (wts["bias"], wts["g_mix"], wts["w_in"], wts["w_a_up"], wts["b_a"], wts["g_gla"],
                   wts["g_swa"], wts["w_out"], wts["g_ffn"], wts["w_r"], wts["b_r"])
    in_specs = [
        pl.BlockSpec(memory_space=pltpu.SMEM),
        pl.BlockSpec((1, tm, D_MODEL), lambda b, j: (b, j, 0)),
        _full_spec(s0.shape), _full_spec(k0.shape), _full_spec(v0.shape), _full_spec(base0.shape),
    ] + [_full_spec(w.shape) for w in weight_args]
    tok_spec = pl.BlockSpec((8, tm), lambda b, j: (0, b * nj + j))
    out_specs = [
        pl.BlockSpec((1, tm, D_MODEL), lambda b, j: (b, j, 0)),
        pl.BlockSpec((tm * SLAB, LANES), lambda b, j: (b * nj + j, 0)),
        tok_spec, tok_spec, tok_spec,
        pl.BlockSpec((1, GLA_QK, GLA_V), lambda b, j: (b, 0, 0)),
        pl.BlockSpec((1, WINDOW, SWA_KV), lambda b, j: (b, 0, 0)),
        pl.BlockSpec((1, WINDOW, SWA_KV), lambda b, j: (b, 0, 0)),
        _full_spec((N_EXPERTS, LANES)),
    ]
    out_shape = [
        jax.ShapeDtypeStruct((B, L, D_MODEL), F32),
        jax.ShapeDtypeStruct((hp_rows * SLAB, LANES), F32),
        jax.ShapeDtypeStruct((8, T), jnp.int32),
        jax.ShapeDtypeStruct((8, T), F32),
        jax.ShapeDtypeStruct((8, T), jnp.int32),
        jax.ShapeDtypeStruct((B, GLA_QK, GLA_V), F32),
        jax.ShapeDtypeStruct((B, WINDOW, SWA_KV), F32),
        jax.ShapeDtypeStruct((B, WINDOW, SWA_KV), F32),
        jax.ShapeDtypeStruct((N_EXPERTS, LANES), F32),
    ]
    kern = functools.partial(_mixer_kernel, n_lead_pad=n_lead_pad, prev_valid_from=prev_valid_from)
    return pl.pallas_call(
        kern,
        grid=(B, nj),
        in_specs=in_specs,
        out_specs=out_specs,
        out_shape=out_shape,
        scratch_shapes=[pltpu.VMEM((GLA_QK, GLA_V), F32), pltpu.VMEM((WINDOW, SWA_KV), F32),
                        pltpu.VMEM((WINDOW, SWA_KV), F32), pltpu.VMEM((N_EXPERTS, LANES), F32)],
        compiler_params=pltpu.CompilerParams(dimension_semantics=("arbitrary", "arbitrary"),
                                             vmem_limit_bytes=VMEM_LIMIT),
        name="mixer",
    )(wts["sinks"], x, s0, k0, v0, base0, *weight_args)


def _decode_kernel(sinks_ref, x_ref, st_ref, ck_ref, cv_ref, base0_ref, bias_ref,
                   g_mix_ref, w_in_ref, w_a_up_ref, b_a_ref, g_gla_ref, g_swa_ref, w_out_ref,
                   g_ffn_ref, w_r_ref, b_r_ref,
                   xmid_ref, hp_ref, topi_ref, gate_ref, rank_ref, sto_ref, cko_ref, cvo_ref, cnt_ref,
                   at_scr, kt_scr, qt_scr, gv_scr, gr_scr, sq_scr, sk_scr, sv_scr, og_scr, os_scr):
    i = pl.program_id(0)
    n_seq = x_ref.shape[0]

    @pl.when(i == 0)
    def _():
        p = _project(x_ref[...], g_mix_ref[...], w_in_ref[...], w_a_up_ref[...], b_a_ref[...])
        at_scr[...] = jnp.transpose(jnp.exp(p["log_a"]))
        kt_scr[...] = jnp.transpose(p["gk"])
        qt_scr[...] = jnp.transpose(p["gq"])
        gv_scr[...] = p["gv"]
        gr_scr[...] = p["gr"]
        sq_scr[...] = p["sq"]
        sk_scr[...] = p["sk"]
        sv_scr[...] = p["sv"]

    lane_seq = _iota((GLA_QK, n_seq), 1)
    half = _iota((1, LANES), 1) < SWA_HEAD_DIM
    row_id = _iota((WINDOW, SWA_KV), 0)
    head_diag = (_iota((16, SWA_Q), 1) // SWA_HEAD_DIM) == _iota((16, SWA_Q), 0)
    sink_col = jnp.concatenate(
        [jnp.full((1, 1), sinks_ref[h], F32) for h in range(SWA_HEADS)] + [jnp.zeros((8, 1), F32)], axis=0)

    def per_seq(sl, carry):
        s = i * DEC_SB + sl
        sel = lane_seq == s
        a_c = jnp.sum(jnp.where(sel, at_scr[...], 0.0), axis=1, keepdims=True)
        k_c = jnp.sum(jnp.where(sel, kt_scr[...], 0.0), axis=1, keepdims=True)
        q_c = jnp.sum(jnp.where(sel, qt_scr[...], 0.0), axis=1, keepdims=True)
        st = st_ref[sl].reshape(GLA_QK, GLA_DV)
        v_row = gv_scr[pl.ds(s, 1), :]
        v_b = jnp.concatenate(
            [jnp.broadcast_to(v_row[:, h * GLA_DV:(h + 1) * GLA_DV], (GLA_DK, GLA_DV))
             for h in range(GLA_HEADS)], axis=0)
        st_new = a_c * st + k_c * v_b
        sto_ref[sl] = st_new.reshape(GLA_HEADS, GLA_DK, GLA_DV)
        t = q_c * st_new
        og_scr[pl.ds(s, 1), :] = jnp.concatenate(
            [jnp.sum(t[h * GLA_DK:(h + 1) * GLA_DK], axis=0, keepdims=True) for h in range(GLA_HEADS)],
            axis=1)

        k_new = sk_scr[pl.ds(s, 1), :]
        v_new = sv_scr[pl.ds(s, 1), :]
        kn = jnp.where(row_id == WINDOW - 1, k_new, pltpu.roll(ck_ref[sl], WINDOW - 1, 0))
        vn = jnp.where(row_id == WINDOW - 1, v_new, pltpu.roll(cv_ref[sl], WINDOW - 1, 0))
        cko_ref[sl] = kn
        cvo_ref[sl] = vn
        kr, vr = pltpu.roll(kn, SWA_HEAD_DIM, 1), pltpu.roll(vn, SWA_HEAD_DIM, 1)
        k0, k1 = jnp.where(half, kn, kr), jnp.where(half, kr, kn)
        v0, v1 = jnp.where(half, vn, vr), jnp.where(half, vr, vn)
        kw = jnp.concatenate([k0, k0, k1, k1], axis=1).astype(BF16)
        vw = jnp.concatenate([v0, v0, v1, v1], axis=1).astype(BF16)
        q_row = sq_scr[pl.ds(s, 1), :]
        qm = jnp.where(head_diag, jnp.broadcast_to(q_row, (16, SWA_Q)), 0.0).astype(BF16)
        sc = _dot_nt(qm, kw) * (SWA_HEAD_DIM ** -0.5) + bias_ref[...]
        m = jnp.maximum(jnp.max(sc, axis=1, keepdims=True), sink_col)
        pr = jnp.exp(sc - m)
        inv = 1.0 / (jnp.sum(pr, axis=1, keepdims=True) + jnp.exp(sink_col - m))
        ow = _dot(pr.astype(BF16), vw) * inv
        os_scr[pl.ds(s, 1), :] = jnp.sum(jnp.where(head_diag, ow, 0.0), axis=0, keepdims=True)
        return carry

    lax.fori_loop(0, DEC_SB, per_seq, 0, unroll=2)

    @pl.when(i == pl.num_programs(0) - 1)
    def _():
        x_mid, hp, topi, gate8, rank8, base = _tail(
            x_ref[...], og_scr[...], gr_scr[...], os_scr[...], g_gla_ref[...], g_swa_ref[...],
            w_out_ref[...], g_ffn_ref[...], w_r_ref[...], b_r_ref[...], base0_ref[...])
        xmid_ref[...] = x_mid
        _store_slabs(hp_ref, hp)
        topi_ref[...] = topi
        gate_ref[...] = gate8
        rank_ref[...] = rank8
        cnt_ref[...] = base


def _decode_call(xs, state, ck, cv, base0, bias_dec, wts):
    n_seq = xs.shape[0]
    nb = n_seq // DEC_SB
    weight_args = (wts["g_mix"], wts["w_in"], wts["w_a_up"], wts["b_a"], wts["g_gla"],
                   wts["g_swa"], wts["w_out"], wts["g_ffn"], wts["w_r"], wts["b_r"])
    in_specs = [
        pl.BlockSpec(memory_space=pltpu.SMEM),
        _full_spec(xs.shape),
        pl.BlockSpec((DEC_SB, GLA_HEADS, GLA_DK, GLA_DV), lambda i: (i, 0, 0, 0)),
        pl.BlockSpec((DEC_SB, WINDOW, SWA_KV), lambda i: (i, 0, 0)),
        pl.BlockSpec((DEC_SB, WINDOW, SWA_KV), lambda i: (i, 0, 0)),
        _full_spec(base0.shape), _full_spec(bias_dec.shape),
    ] + [_full_spec(w.shape) for w in weight_args]
    out_specs = [
        _full_spec((n_seq, D_MODEL)),
        _full_spec((n_seq * SLAB, LANES)),
        _full_spec((8, n_seq)), _full_spec((8, n_seq)), _full_spec((8, n_seq)),
        pl.BlockSpec((DEC_SB, GLA_HEADS, GLA_DK, GLA_DV), lambda i: (i, 0, 0, 0)),
        pl.BlockSpec((DEC_SB, WINDOW, SWA_KV), lambda i: (i, 0, 0)),
        pl.BlockSpec((DEC_SB, WINDOW, SWA_KV), lambda i: (i, 0, 0)),
        _full_spec((N_EXPERTS, LANES)),
    ]
    out_shape = [
        jax.ShapeDtypeStruct((n_seq, D_MODEL), F32),
        jax.ShapeDtypeStruct((n_seq * SLAB, LANES), F32),
        jax.ShapeDtypeStruct((8, n_seq), jnp.int32),
        jax.ShapeDtypeStruct((8, n_seq), F32),
        jax.ShapeDtypeStruct((8, n_seq), jnp.int32),
        jax.ShapeDtypeStruct(state.shape, F32),
        jax.ShapeDtypeStruct(ck.shape, F32),
        jax.ShapeDtypeStruct(cv.shape, F32),
        jax.ShapeDtypeStruct((N_EXPERTS, LANES), F32),
    ]
    scratch = [pltpu.VMEM((GLA_QK, n_seq), F32)] * 3 + [
        pltpu.VMEM((n_seq, GLA_V), F32), pltpu.VMEM((n_seq, GLA_V), F32), pltpu.VMEM((n_seq, SWA_Q), F32),
        pltpu.VMEM((n_seq, SWA_KV), F32), pltpu.VMEM((n_seq, SWA_KV), F32),
        pltpu.VMEM((n_seq, GLA_V), F32), pltpu.VMEM((n_seq, SWA_Q), F32)]
    return pl.pallas_call(
        _decode_kernel,
        grid=(nb,),
        in_specs=in_specs,
        out_specs=out_specs,
        out_shape=out_shape,
        scratch_shapes=scratch,
        compiler_params=pltpu.CompilerParams(dimension_semantics=("arbitrary",),
                                             vmem_limit_bytes=VMEM_LIMIT),
        name="decode",
    )(wts["sinks"], xs, state, ck, cv, base0, bias_dec, *weight_args)


SC_CORES = 2
SC_SUBCORES = 16
SC_WORKERS = SC_CORES * SC_SUBCORES
SC_SCATTER_ROWS = 32
SC_GATHER_ROWS = 32


def _sc_mesh():
    return plsc.VectorSubcoreMesh(core_axis_name="c", subcore_axis_name="s")


def _sc_worker_id():
    return lax.axis_index("s") * SC_CORES + lax.axis_index("c")


def _sc_scatter_rows(src_p, src_s, idx_p, idx_s, n_out):
    rows = SC_SCATTER_ROWS
    n_chunks = idx_p.shape[0] // SC_WORKERS
    n_s, _, rows_s = idx_s.shape
    assert n_chunks * SC_WORKERS == idx_p.shape[0] and n_chunks % 2 == 0 and n_s <= SC_WORKERS

    @functools.partial(
        pl.kernel, mesh=_sc_mesh(),
        out_type=jax.ShapeDtypeStruct((n_out, SLAB, LANES), F32),
        scratch_types=[pltpu.VMEM((2, TOP_K, rows), jnp.int32), pltpu.VMEM((2, rows, SLAB, LANES), F32),
                       pltpu.VMEM((TOP_K, rows_s), jnp.int32), pltpu.VMEM((rows_s, SLAB, LANES), F32),
                       pltpu.SemaphoreType.DMA((2,)), pltpu.SemaphoreType.DMA((2,))])
    def scatter_rows(srcp_hbm, srcs_hbm, idxp_hbm, idxs_hbm, out_hbm, idx_v, rows_v, idxs_v, rowss_v, lsem, ssem):
        wid = _sc_worker_id()

        def loads(c, b):
            g = wid * n_chunks + c
            return (pltpu.make_async_copy(idxp_hbm.at[g], idx_v.at[b], lsem.at[b]),
                    pltpu.make_async_copy(srcp_hbm.at[pl.ds(pl.multiple_of(g * rows, 8), rows)], rows_v.at[b],
                                          lsem.at[b]))

        def scatters(b):
            return [pltpu.make_async_copy(rows_v.at[b], out_hbm.at[idx_v.at[b, k]], ssem.at[b])
                    for k in range(TOP_K)]

        for d in loads(0, 0):
            d.start()

        @pl.loop(0, n_chunks, step=2)
        def _(c0):
            for b in range(2):
                c = c0 + b
                for d in loads(c, b):
                    d.wait()

                @pl.when(c >= 1)
                def _():
                    for d in scatters(1 - b):
                        d.wait()

                @pl.when(c + 1 < n_chunks)
                def _():
                    for d in loads(c + 1, 1 - b):
                        d.start()

                for d in scatters(b):
                    d.start()

        for d in scatters((n_chunks - 1) % 2):
            d.wait()

        @pl.when(wid < n_s)
        def _():
            pltpu.sync_copy(idxs_hbm.at[wid], idxs_v)
            pltpu.sync_copy(srcs_hbm.at[pl.ds(pl.multiple_of(wid * rows_s, 8), rows_s)], rowss_v)
            for k in range(TOP_K):
                pltpu.sync_copy(rowss_v, out_hbm.at[idxs_v.at[k]])

    return scatter_rows(src_p, src_s, idx_p, idx_s)


def _sc_gather_rows(src3, idx2):
    rows = SC_GATHER_ROWS
    n_chunks = idx2.shape[0] // SC_WORKERS
    assert n_chunks * SC_WORKERS == idx2.shape[0] and idx2.shape[1] == rows and n_chunks % 2 == 0

    @functools.partial(
        pl.kernel, mesh=_sc_mesh(),
        out_type=jax.ShapeDtypeStruct((idx2.shape[0] * rows, SLAB, LANES), F32),
        scratch_types=[pltpu.VMEM((2, rows), jnp.int32), pltpu.VMEM((2, rows, SLAB, LANES), F32),
                       pltpu.SemaphoreType.DMA((2,)), pltpu.SemaphoreType.DMA((2,))])
    def gather_rows(src_hbm, idx_hbm, out_hbm, idx_v, rows_v, gsem, wsem):
        wid = _sc_worker_id()

        def gather(b):
            return pltpu.make_async_copy(src_hbm.at[idx_v.at[b]], rows_v.at[b], gsem.at[b])

        def write(c, b):
            base = pl.multiple_of((wid * n_chunks + c) * rows, 8)
            return pltpu.make_async_copy(rows_v.at[b], out_hbm.at[pl.ds(base, rows)], wsem.at[b])

        pltpu.sync_copy(idx_hbm.at[wid * n_chunks], idx_v.at[0])
        gather(0).start()

        @pl.loop(0, n_chunks, step=2)
        def _(c0):
            for b in range(2):
                c = c0 + b

                @pl.when(c + 1 < n_chunks)
                def _():
                    @pl.when(c >= 1)
                    def _():
                        write(c - 1, 1 - b).wait()
                    pltpu.sync_copy(idx_hbm.at[wid * n_chunks + c + 1], idx_v.at[1 - b])
                    gather(1 - b).start()

                gather(b).wait()
                write(c, b).start()

        write(n_chunks - 2, 0).wait()
        write(n_chunks - 1, 1).wait()

    return gather_rows(src3, idx2)


def _ffn_kernel(blk_e_ref, nused_ref, x_ref, wu_ref, bu_ref, wd_ref, bd_ref, y_ref, xbf, actbf, wu_bf, wd_bf):
    i = pl.program_id(0)
    tm = MOE_TM
    n_tiles = D_FF // FF_TILE

    @pl.when(i < nused_ref[0])
    def _():
        @pl.when((i == 0) | (blk_e_ref[i] != blk_e_ref[jnp.maximum(i - 1, 0)]))
        def _():
            wu_bf[...] = wu_ref[0].astype(BF16)
            wd_bf[...] = wd_ref[0].astype(BF16)

        for c in range(SLAB):
            xbf[:, c * LANES:(c + 1) * LANES] = _load_slab_chunk(x_ref, tm, c, SLAB).astype(BF16)
        for n in range(n_tiles):
            gc = slice(n * FF_TILE, (n + 1) * FF_TILE)
            lc = slice(D_FF + n * FF_TILE, D_FF + (n + 1) * FF_TILE)
            g = jnp.minimum(_dot(xbf[...], wu_bf[:, gc]) + bu_ref[0, :, gc], SWIGLU_LIMIT)
            lin = jnp.clip(_dot(xbf[...], wu_bf[:, lc]) + bu_ref[0, :, lc], -SWIGLU_LIMIT, SWIGLU_LIMIT)
            actbf[:, gc] = (g * jax.nn.sigmoid(SWIGLU_ALPHA * g) * (lin + 1.0)).astype(BF16)
        for n in range(n_tiles):
            yc = slice(n * FF_TILE, (n + 1) * FF_TILE)
            y = _dot(actbf[...], wd_bf[:, yc]) + bd_ref[0, :, yc]
            for c in range(FF_TILE // LANES):
                y_ref[pl.ds(n * (FF_TILE // LANES) + c, tm, stride=SLAB), :] = y[:, c * LANES:(c + 1) * LANES]

    @pl.when(i >= nused_ref[0])
    def _():
        y_ref[...] = jnp.zeros_like(y_ref)


def _ffn_call(blk_e, nused, xs2, w_up, b_up, w_down, b_down):
    n_blocks = blk_e.shape[0]
    tm = MOE_TM
    row_blk = pl.BlockSpec((tm * SLAB, LANES), lambda i, be, nu: (i, 0))
    grid_spec = pltpu.PrefetchScalarGridSpec(
        num_scalar_prefetch=2,
        grid=(n_blocks,),
        in_specs=[
            row_blk,
            pl.BlockSpec((1, D_MODEL, 2 * D_FF), lambda i, be, nu: (be[i], 0, 0)),
            pl.BlockSpec((1, 1, 2 * D_FF), lambda i, be, nu: (be[i], 0, 0)),
            pl.BlockSpec((1, D_FF, D_MODEL), lambda i, be, nu: (be[i], 0, 0)),
            pl.BlockSpec((1, 1, D_MODEL), lambda i, be, nu: (be[i], 0, 0)),
        ],
        out_specs=row_blk,
        scratch_shapes=[pltpu.VMEM((tm, D_MODEL), BF16), pltpu.VMEM((tm, D_FF), BF16),
                        pltpu.VMEM((D_MODEL, 2 * D_FF), BF16), pltpu.VMEM((D_FF, D_MODEL), BF16)],
    )
    return pl.pallas_call(
        _ffn_kernel,
        grid_spec=grid_spec,
        out_shape=jax.ShapeDtypeStruct((n_blocks * tm * SLAB, LANES), F32),
        compiler_params=pltpu.CompilerParams(dimension_semantics=("arbitrary",),
                                             vmem_limit_bytes=VMEM_LIMIT),
        name="experts",
    )(blk_e, nused, xs2, w_up, b_up.reshape(N_EXPERTS, 1, 2 * D_FF), w_down, b_down.reshape(N_EXPERTS, 1, D_MODEL))


FF_TILE = 256


RING = 3


def _expert_kernel(blk_e_ref, nused_ref, tok0_ref, tok1_ref, tokn_ref, retp_ref, hp_hbm, wu_ref, bu_ref,
                   wd_ref, bd_ref, y_hbm, xbuf0, xbuf1, xbuf2, ybuf0, ybuf1, ybuf2, xbf, actbf, zbuf,
                   wu_bf, wd_bf, gsem, ssem, zsem):
    i = pl.program_id(0)
    tm = MOE_TM
    nused = nused_ref[0]
    n_tiles = D_FF // FF_TILE
    xbufs, ybufs = (xbuf0, xbuf1, xbuf2), (ybuf0, ybuf1, ybuf2)

    def gather(tok_ref, r, s):
        src = hp_hbm.at[pl.ds(pl.multiple_of(tok_ref[0, 0, r], SLAB), SLAB)]
        return pltpu.make_async_copy(src, xbufs[s].at[pl.ds(r * SLAB, SLAB)], gsem.at[s])

    def scatter(r, s):
        dst = y_hbm.at[pl.ds(pl.multiple_of(retp_ref[0, 0, r], SLAB), SLAB)]
        return pltpu.make_async_copy(ybufs[s].at[pl.ds(r * SLAB, SLAB)], dst, ssem.at[s])

    def wait_gather(s):
        pltpu.make_async_copy(hp_hbm.at[pl.ds(0, tm * SLAB)], xbufs[s], gsem.at[s]).wait()

    def wait_scatter(s):
        pltpu.make_async_copy(ybufs[s], y_hbm.at[pl.ds(0, tm * SLAB)], ssem.at[s]).wait()

    @pl.when(i == 0)
    def _():
        ybuf2[...] = jnp.zeros_like(ybuf2)
        zbuf[...] = jnp.zeros_like(zbuf)

        def issue(r, c):
            gather(tok0_ref, r, 0).start()
            gather(tok1_ref, r, 1).start()
            return c
        lax.fori_loop(0, tm, issue, 0)

    def compute(s):
        nxt = (s + 2) % RING
        wait_gather(s)

        @pl.when(i >= 2)
        def _():
            wait_scatter(s)

        @pl.when((i == 0) | (blk_e_ref[i] != blk_e_ref[jnp.maximum(i - 1, 0)]))
        def _():
            wu_bf[...] = wu_ref[0].astype(BF16)
            wd_bf[...] = wd_ref[0].astype(BF16)

        for c in range(SLAB):
            xbf[:, c * LANES:(c + 1) * LANES] = _load_slab_chunk(xbufs[s], tm, c, SLAB).astype(BF16)
        for r in range(tm):
            gather(tokn_ref, r, nxt).start(priority=r % 2)
            scatter(r, nxt).start(priority=r % 2)
        for n in range(n_tiles):
            gc = slice(n * FF_TILE, (n + 1) * FF_TILE)
            lc = slice(D_FF + n * FF_TILE, D_FF + (n + 1) * FF_TILE)
            g = jnp.minimum(_dot(xbf[...], wu_bf[:, gc]) + bu_ref[0, :, gc], SWIGLU_LIMIT)
            lin = jnp.clip(_dot(xbf[...], wu_bf[:, lc]) + bu_ref[0, :, lc], -SWIGLU_LIMIT, SWIGLU_LIMIT)
            actbf[:, gc] = (g * jax.nn.sigmoid(SWIGLU_ALPHA * g) * (lin + 1.0)).astype(BF16)
        for n in range(n_tiles):
            yc = slice(n * FF_TILE, (n + 1) * FF_TILE)
            y = _dot(actbf[...], wd_bf[:, yc]) + bd_ref[0, :, yc]
            for c in range(FF_TILE // LANES):
                ybufs[s][pl.ds(n * (FF_TILE // LANES) + c, tm, stride=SLAB), :] = y[:, c * LANES:(c + 1) * LANES]

    def drain(s):
        wait_gather(s)
        wait_gather((s + 1) % RING)
        wait_scatter(s)
        wait_scatter((s + 1) % RING)
        last = (s + 2) % RING

        def issue(r, c):
            scatter(r, last).start()
            return c
        lax.fori_loop(0, tm, issue, 0)
        wait_scatter(last)

    for s in range(RING):
        @pl.when((i < nused) & (i % RING == s))
        def _():
            compute(s)

        @pl.when((i == nused) & (i % RING == s))
        def _():
            drain(s)

    @pl.when(i >= nused)
    def _():
        zc = pltpu.make_async_copy(zbuf, y_hbm.at[pl.ds(pl.multiple_of((tm + i * tm) * SLAB, SLAB), tm * SLAB)], zsem)
        zc.start()
        zc.wait()


def _expert_call(blk_e, nused, tok_sorted, ret_sorted, hp_all, w_up, b_up, w_down, b_down, n_slots):
    n_blocks = blk_e.shape[0]
    tm = MOE_TM
    n_rows = n_blocks * tm + tm
    ret_tab = (jnp.concatenate([n_slots + jnp.arange(tm, dtype=jnp.int32), ret_sorted]) * SLAB
               ).reshape(n_blocks + 1, 1, tm)
    tok_tab = (tok_sorted * SLAB).reshape(n_blocks, 1, tm)
    smem_blk = functools.partial(pl.BlockSpec, (1, 1, tm), memory_space=pltpu.SMEM)
    grid_spec = pltpu.PrefetchScalarGridSpec(
        num_scalar_prefetch=2,
        grid=(n_blocks,),
        in_specs=[
            smem_blk(lambda i, be, nu: (0, 0, 0)),
            smem_blk(lambda i, be, nu: (1, 0, 0)),
            smem_blk(lambda i, be, nu: (jnp.minimum(i + 2, n_blocks - 1), 0, 0)),
            smem_blk(lambda i, be, nu: (i, 0, 0)),
            pl.BlockSpec(memory_space=pl.ANY),
            pl.BlockSpec((1, D_MODEL, 2 * D_FF), lambda i, be, nu: (be[i], 0, 0)),
            pl.BlockSpec((1, 1, 2 * D_FF), lambda i, be, nu: (be[i], 0, 0)),
            pl.BlockSpec((1, D_FF, D_MODEL), lambda i, be, nu: (be[i], 0, 0)),
            pl.BlockSpec((1, 1, D_MODEL), lambda i, be, nu: (be[i], 0, 0)),
        ],
        out_specs=pl.BlockSpec(memory_space=pl.ANY),
        scratch_shapes=[pltpu.VMEM((tm * SLAB, LANES), F32)] * (2 * RING) + [
                        pltpu.VMEM((tm, D_MODEL), BF16), pltpu.VMEM((tm, D_FF), BF16),
                        pltpu.VMEM((tm * SLAB, LANES), F32),
                        pltpu.VMEM((D_MODEL, 2 * D_FF), BF16), pltpu.VMEM((D_FF, D_MODEL), BF16),
                        pltpu.SemaphoreType.DMA((RING,)), pltpu.SemaphoreType.DMA((RING,)),
                        pltpu.SemaphoreType.DMA],
    )
    return pl.pallas_call(
        _expert_kernel,
        grid_spec=grid_spec,
        out_shape=jax.ShapeDtypeStruct((n_rows * SLAB, LANES), F32),
        compiler_params=pltpu.CompilerParams(dimension_semantics=("arbitrary",),
                                             vmem_limit_bytes=VMEM_LIMIT),
        name="experts",
    )(blk_e, nused, tok_tab, tok_tab, tok_tab, ret_tab, hp_all,
      w_up, b_up.reshape(N_EXPERTS, 1, 2 * D_FF), w_down, b_down.reshape(N_EXPERTS, 1, D_MODEL))


def _combine_kernel(ys0_ref, ys1_ref, ys2_ref, ys3_ref, xmid_ref, gate_ref, g_final_ref, y_ref):
    tm = xmid_ref.shape[0]
    gts = gate_ref[...]
    chunks = []
    for c in range(SLAB):
        acc = xmid_ref[:, c * LANES:(c + 1) * LANES]
        for k, ys_ref in enumerate((ys0_ref, ys1_ref, ys2_ref, ys3_ref)):
            acc = acc + _load_slab_chunk(ys_ref, tm, c, SLAB) * gts[:, k:k + 1]
        chunks.append(acc)
    y_ref[...] = _rms(jnp.concatenate(chunks, axis=1), g_final_ref[...])


def _combine_call(ys4, t_stride, row0, x_mid, gates, g_final, tm):
    T = x_mid.shape[0]
    blk0 = row0 // tm
    per_k = t_stride // tm
    assert per_k * tm == t_stride and blk0 * tm == row0

    def ys_spec(k):
        return pl.BlockSpec((tm * SLAB, LANES), lambda i: (k * per_k + blk0 + i, 0))

    return pl.pallas_call(
        _combine_kernel,
        grid=(T // tm,),
        in_specs=[
            ys_spec(0), ys_spec(1), ys_spec(2), ys_spec(3),
            pl.BlockSpec((tm, D_MODEL), lambda i: (i, 0)),
            pl.BlockSpec((tm, TOP_K), lambda i: (i, 0)),
            _full_spec((1, D_MODEL)),
        ],
        out_specs=pl.BlockSpec((tm, D_MODEL), lambda i: (i, 0)),
        out_shape=jax.ShapeDtypeStruct((T, D_MODEL), F32),
        compiler_params=pltpu.CompilerParams(dimension_semantics=("arbitrary",),
                                             vmem_limit_bytes=VMEM_LIMIT),
        name="combine",
    )(ys4, ys4, ys4, ys4, x_mid, gates, g_final)


def _t5_bucket(dist):
    n = jnp.maximum(dist, 0)
    max_exact = NUM_BUCKETS // 2
    nf = jnp.maximum(n, 1).astype(F32)
    large = max_exact + (jnp.log(nf / max_exact) / math.log(MAX_DISTANCE / max_exact)
                         * (NUM_BUCKETS - max_exact)).astype(jnp.int32)
    large = jnp.minimum(large, NUM_BUCKETS - 1)
    return jnp.where(n < max_exact, n, large)


def kernel(x_prompt, x_sample, state_gla, cache_swa_k, cache_swa_v, meta_tokens, rel_bias_table,
           g_mix, w_in, w_a_up, b_a, g_gla_out, g_swa_out, attn_sinks, w_out,
           g_ffn, w_router, b_router, w_up, b_up, w_down, b_down, g_final):
    assert g_mix.shape[0] == 1, "single-layer trunk"
    B, L, _ = x_prompt.shape
    n_seq = x_sample.shape[0]
    TP = B * L
    T_all = TP + n_seq

    wi = w_in[0]
    sizes = (GLA_QK, GLA_QK, GLA_V, GLA_V, GLA_LOWRANK, SWA_Q, SWA_KV, SWA_KV)
    offs = [0]
    for s in sizes:
        offs.append(offs[-1] + s)
    seg = [wi[:, offs[n]:offs[n + 1]] for n in range(8)]
    w_in_r = jnp.concatenate(
        seg[0:4] + seg[5:8] + [seg[4], jnp.zeros((D_MODEL, LANES - GLA_LOWRANK), F32)], axis=1).astype(BF16)
    w_a_pad = jnp.concatenate([w_a_up[0], jnp.zeros((LANES - GLA_LOWRANK, GLA_QK), F32)], axis=0).astype(BF16)
    wr_t = jnp.transpose(w_router[0])
    wr_hi = wr_t.astype(BF16)
    wr_lo = (wr_t - wr_hi.astype(F32)).astype(BF16)
    qi = jnp.arange(WINDOW)[:, None]
    kj = jnp.arange(2 * WINDOW)[None, :]
    buckets = jnp.arange(NUM_BUCKETS)
    table = rel_bias_table.astype(F32)
    oh_p = (_t5_bucket(qi - kj + WINDOW)[..., None] == buckets).astype(F32)
    bias_p = jnp.einsum("qkb,bh->hkq", oh_p, table, precision=lax.Precision.HIGHEST)
    bias_p = bias_p.reshape(SWA_KV_HEADS, SWA_GROUP, 2 * WINDOW, WINDOW).transpose(0, 2, 1, 3)
    bias_p = bias_p.reshape(SWA_KV_HEADS, 2 * WINDOW, SWA_GROUP * WINDOW)
    oh_d = (_t5_bucket(WINDOW - 1 - jnp.arange(WINDOW))[:, None] == buckets).astype(F32)
    bias_d = jnp.einsum("rb,bh->hr", oh_d, table, precision=lax.Precision.HIGHEST)
    bias_d = jnp.concatenate([bias_d, jnp.zeros((8, WINDOW), F32)], axis=0)
    wts = dict(
        sinks=attn_sinks[0].astype(F32), bias=bias_p,
        g_mix=g_mix[0][None], w_in=w_in_r, w_a_up=w_a_pad, b_a=b_a[0][None],
        g_gla=g_gla_out[0][None], g_swa=g_swa_out[0][None], w_out=w_out[0].astype(BF16),
        g_ffn=g_ffn[0][None], w_r=jnp.concatenate([wr_hi, wr_lo], axis=0), b_r=b_router[0][:, None],
    )

    x_pre = jnp.concatenate([jnp.zeros((WINDOW - N_META, D_MODEL), F32), meta_tokens.astype(F32)], axis=0)[None]
    zeros_s = jnp.zeros((GLA_QK, GLA_V), F32)
    zeros_kv = jnp.zeros((WINDOW, SWA_KV), F32)
    zeros_b = jnp.zeros((N_EXPERTS, LANES), F32)
    pre = _mixer_call(x_pre, zeros_s, zeros_kv, zeros_kv, zeros_b, wts, WINDOW, WINDOW - N_META, 0, WINDOW)
    s_meta, k_meta, v_meta = pre[5][0], pre[6][0], pre[7][0]

    (xmid_p, hp_p, topi_p, gate_p, rank_p, s_p, k_p, v_p, cnt_p) = _mixer_call(
        x_prompt, s_meta, k_meta, v_meta, zeros_b, wts, MIX_TM, 0, WINDOW - N_META, TP)

    (xmid_s, hp_s, topi_s, gate_s, rank_s, st_s, ck_s, cv_s, cnt_all) = _decode_call(
        x_sample[:, 0], state_gla[0], cache_swa_k[0].reshape(n_seq, WINDOW, SWA_KV),
        cache_swa_v[0].reshape(n_seq, WINDOW, SWA_KV), cnt_p, bias_d, wts)

    tm = MOE_TM
    n_slots = T_all * TOP_K
    n_blocks = -(-n_slots // tm) + N_EXPERTS
    top_e = jnp.concatenate([topi_p[:TOP_K], topi_s[:TOP_K]], axis=1)
    rank = jnp.concatenate([rank_p[:TOP_K], rank_s[:TOP_K]], axis=1)
    counts = cnt_all[:, 0].astype(jnp.int32)
    padded = (counts + tm - 1) // tm * tm
    pad_end = jnp.cumsum(padded)
    pad_start = pad_end - padded
    e_ids = jnp.arange(N_EXPERTS, dtype=jnp.int32)
    dest = jnp.sum(jnp.where(top_e[..., None] == e_ids, pad_start, 0), axis=-1) + rank
    n_pad = n_blocks * tm
    blk_e = jnp.minimum(jnp.sum(pad_end[None] <= (jnp.arange(n_blocks, dtype=jnp.int32) * tm)[:, None], axis=1),
                        N_EXPERTS - 1).astype(jnp.int32)
    nused = (pad_end[-1] // tm).astype(jnp.int32).reshape(1)

    sample_rows = 8
    idx_p = dest[:, :TP].reshape(TOP_K, TP // SC_SCATTER_ROWS, SC_SCATTER_ROWS).transpose(1, 0, 2)
    idx_s = dest[:, TP:].reshape(TOP_K, n_seq // sample_rows, sample_rows).transpose(1, 0, 2)
    xs3 = _sc_scatter_rows(hp_p.reshape(TP, SLAB, LANES), hp_s.reshape(n_seq, SLAB, LANES), idx_p, idx_s, n_pad)
    ys2 = _ffn_call(blk_e, nused, xs3.reshape(-1, LANES), w_up[0], b_up[0], w_down[0], b_down[0])
    unit = 2 * SC_WORKERS * SC_GATHER_ROWS // TOP_K
    t_stride = -(-T_all // unit) * unit
    slot_src = jnp.concatenate([dest, jnp.zeros((TOP_K, t_stride - T_all), jnp.int32)], axis=1)
    slot_src = slot_src.reshape(TOP_K * t_stride // SC_GATHER_ROWS, SC_GATHER_ROWS)
    ys4 = _sc_gather_rows(ys2.reshape(-1, SLAB, LANES), slot_src).reshape(-1, LANES)

    gates = jnp.transpose(jnp.concatenate([gate_p[:TOP_K], gate_s[:TOP_K]], axis=1))
    gf = g_final[None]
    y_p = _combine_call(ys4, t_stride, 0, xmid_p.reshape(TP, D_MODEL), gates[:TP], gf, MIX_TM)
    y_s = _combine_call(ys4, t_stride, TP, xmid_s, gates[TP:], gf, n_seq)

    s_heads = jnp.stack([s_p[:, h * GLA_DK:(h + 1) * GLA_DK, h * GLA_DV:(h + 1) * GLA_DV]
                         for h in range(GLA_HEADS)], axis=1)
    return (y_p.reshape(B, L, D_MODEL), y_s.reshape(n_seq, 1, D_MODEL), s_heads[None],
            k_p.reshape(1, B, WINDOW, SWA_KV_HEADS, SWA_HEAD_DIM),
            v_p.reshape(1, B, WINDOW, SWA_KV_HEADS, SWA_HEAD_DIM),
            st_s[None], ck_s.reshape(1, n_seq, WINDOW, SWA_KV_HEADS, SWA_HEAD_DIM),
            cv_s.reshape(1, n_seq, WINDOW, SWA_KV_HEADS, SWA_HEAD_DIM))
```

```python
import functools
import math

import jax
import jax.numpy as jnp
from jax import lax
from jax.experimental import pallas as pl
from jax.experimental.pallas import tpu as pltpu
from jax.experimental.pallas import tpu_sc as plsc

D_MODEL = 1024
N_META = 16
GLA_HEADS = 4
GLA_DK = 64
GLA_DV = 128
GLA_LOWRANK = 16
GLA_GATE_TAU = 16.0
GLA_CHUNK = 64
SWA_HEADS = 8
SWA_KV_HEADS = 2
SWA_HEAD_DIM = 64
SWA_GROUP = SWA_HEADS // SWA_KV_HEADS
WINDOW = 128
NUM_BUCKETS = 32
MAX_DISTANCE = 128
N_EXPERTS = 32
TOP_K = 4
D_FF = 1024
SWIGLU_ALPHA = 1.702
SWIGLU_LIMIT = 7.0
RMS_EPS = 1e-6

GLA_QK = GLA_HEADS * GLA_DK
GLA_V = GLA_HEADS * GLA_DV
SWA_Q = SWA_HEADS * SWA_HEAD_DIM
SWA_KV = SWA_KV_HEADS * SWA_HEAD_DIM
LANES = 128
C_GQ, C_GK, C_GV, C_GR = 0, GLA_QK, 2 * GLA_QK, 2 * GLA_QK + GLA_V
C_SQ = C_GR + GLA_V
C_SK = C_SQ + SWA_Q
C_SV = C_SK + SWA_KV
C_GA = C_SV + SWA_KV
D_PROJ = C_GA + LANES

MIX_TM = 256
MOE_TM = 512
DEC_SB = 16
VMEM_LIMIT = 56 * 1024 * 1024

F32 = jnp.float32
BF16 = jnp.bfloat16
NEG_INF = float("-inf")


def _dot(a, b):
    return jnp.dot(a, b, preferred_element_type=F32)


def _dot_nt(a, b):
    return lax.dot_general(a, b, (((1,), (1,)), ((), ())), preferred_element_type=F32)


def _split3(x):
    hi = x.astype(BF16)
    r1 = x - hi.astype(F32)
    mid = r1.astype(BF16)
    lo = (r1 - mid.astype(F32)).astype(BF16)
    return hi, mid, lo


def _rms(x, g):
    return x * lax.rsqrt(jnp.mean(x * x, axis=-1, keepdims=True) + RMS_EPS) * g


def _iota(shape, dim):
    return lax.broadcasted_iota(jnp.int32, shape, dim)


SLAB = D_MODEL // LANES


def _store_slabs(ref, x):
    rows = x.shape[0]
    for c in range(SLAB):
        ref[pl.ds(c, rows, stride=SLAB), :] = x[:, c * LANES:(c + 1) * LANES]


def _load_slab_chunk(ref, rows, first, stride):
    return ref[pl.ds(first, rows, stride=stride), :]


def _project(x, g_mix, w_in, w_a_up, b_a):
    h = _rms(x, g_mix).astype(BF16)
    proj = _dot(h, w_in)
    ga = proj[:, C_GA:C_GA + LANES].astype(BF16)
    z = _dot(ga, w_a_up) + b_a
    log_a = -(jnp.maximum(-z, 0.0) + jnp.log1p(jnp.exp(-jnp.abs(z)))) / GLA_GATE_TAU
    return dict(
        gq=proj[:, C_GQ:C_GQ + GLA_QK] * (GLA_DK ** -0.5),
        gk=proj[:, C_GK:C_GK + GLA_QK],
        gv=proj[:, C_GV:C_GV + GLA_V],
        gr=proj[:, C_GR:C_GR + GLA_V],
        sq=proj[:, C_SQ:C_SQ + SWA_Q],
        sk=proj[:, C_SK:C_SK + SWA_KV],
        sv=proj[:, C_SV:C_SV + SWA_KV],
        log_a=log_a,
    )


def _tail(x, o_gla, gr, o_swa, g_gla_out, g_swa_out, w_out, g_ffn, w_r, b_r, base):
    tm = x.shape[0]
    gate = gr * jax.nn.sigmoid(gr)
    parts = []
    for h in range(GLA_HEADS):
        sl = slice(h * GLA_DV, (h + 1) * GLA_DV)
        parts.append(_rms(o_gla[:, sl], g_gla_out) * gate[:, sl])
    parts.append(_rms(o_swa, g_swa_out))
    o = jnp.concatenate(parts, axis=1).astype(BF16)
    x_mid = x + _dot(o, w_out)
    hp = _rms(x_mid, g_ffn)

    h1 = hp.astype(BF16)
    h2 = (hp - h1.astype(F32)).astype(BF16)
    la = _dot_nt(w_r, h1)
    lb = _dot_nt(w_r[0:N_EXPERTS], h2)
    logits = la[0:N_EXPERTS] + la[N_EXPERTS:2 * N_EXPERTS] + lb + b_r

    eidx = _iota((N_EXPERTS, tm), 0)
    vals, idxs, onehots = [], [], []
    l = logits
    for _ in range(TOP_K):
        m = jnp.max(l, axis=0, keepdims=True)
        sel = jnp.min(jnp.where(l == m, eidx, N_EXPERTS), axis=0, keepdims=True)
        oh = eidx == sel
        l = jnp.where(oh, NEG_INF, l)
        vals.append(m)
        idxs.append(sel)
        onehots.append(oh)
    es = [jnp.exp(v - vals[0]) for v in vals]
    denom = es[0] + es[1] + es[2] + es[3]
    gates = [e / denom for e in es]

    ohf = jnp.concatenate([oh.astype(F32) for oh in onehots], axis=0)
    upper = (_iota((tm, tm), 0) < _iota((tm, tm), 1)).astype(BF16)
    prefix = _dot(ohf.astype(BF16), upper)
    ranks = []
    for k in range(TOP_K):
        sl = slice(k * N_EXPERTS, (k + 1) * N_EXPERTS)
        ohk = ohf[sl]
        base_t = jnp.concatenate([base] * (tm // LANES), axis=1)
        ranks.append(jnp.sum(ohk * (prefix[sl] + base_t), axis=0, keepdims=True))
        base = base + jnp.sum(ohk, axis=1, keepdims=True)
    zi = jnp.zeros((8 - TOP_K, tm), jnp.int32)
    zf = jnp.zeros((8 - TOP_K, tm), F32)
    topi = jnp.concatenate(idxs + [zi], axis=0)
    gate8 = jnp.concatenate(gates + [zf], axis=0)
    rank8 = jnp.concatenate([r.astype(jnp.int32) for r in ranks] + [zi], axis=0)
    return x_mid, hp, topi, gate8, rank8, base


def _gla_chunks(p, row0, s_bd, n_lead_pad):
    tm = p["gq"].shape[0]
    nch = tm // GLA_CHUNK
    log_a = p["log_a"]
    if n_lead_pad:
        rows = row0 + _iota((tm, GLA_QK), 0)
        log_a = jnp.where(rows >= n_lead_pad, log_a, 0.0)
    ri, ci = _iota((tm, tm), 0), _iota((tm, tm), 1)
    tril = ((ri >= ci) & (ri // GLA_CHUNK == ci // GLA_CHUNK)).astype(BF16)
    hi, mid, lo = _split3(log_a)
    b_all = _dot(tril, hi) + _dot(tril, mid) + _dot(tril, lo)

    c64 = GLA_CHUNK
    kk_mask = (_iota((GLA_QK, GLA_QK), 0) // c64) == (_iota((GLA_QK, GLA_QK), 1) // GLA_DK)
    vv_mask = (_iota((GLA_QK, GLA_V), 0) // c64) == (_iota((GLA_QK, GLA_V), 1) // GLA_DV)
    ss_mask = (_iota((GLA_QK, GLA_V), 0) // GLA_DK) == (_iota((GLA_QK, GLA_V), 1) // GLA_DV)
    causal = (_iota((c64, GLA_QK), 0) >= (_iota((c64, GLA_QK), 1) % c64)).astype(F32)
    zpad_k = jnp.zeros((LANES - c64, GLA_QK), F32)
    zpad_v = jnp.zeros((LANES - c64, GLA_V), BF16)

    outs = []
    for c in range(nch):
        rs = slice(c * c64, (c + 1) * c64)
        b = b_all[rs]
        q, k, v = p["gq"][rs], p["gk"][rs], p["gv"][rs]
        b_last = b[c64 - 1:c64]
        qt = (q * jnp.exp(b)).astype(BF16)
        kt = k * jnp.exp(-b)
        kd = k * jnp.exp(b_last - b)
        vb = v.astype(BF16)
        k_bd = jnp.where(kk_mask, jnp.concatenate([kt] * GLA_HEADS, axis=0), 0.0).astype(BF16)
        a = (_dot_nt(qt, k_bd) * causal).astype(BF16)
        v_bd = jnp.where(vv_mask, jnp.concatenate([vb] * GLA_HEADS, axis=0), jnp.zeros((), BF16))
        outs.append(_dot(qt, s_bd.astype(BF16)) + _dot(a, v_bd))
        kd_t = jnp.transpose(jnp.concatenate([kd, zpad_k], axis=0)).astype(BF16)
        upd = _dot(kd_t, jnp.concatenate([vb, zpad_v], axis=0))
        decay = jnp.exp(jnp.transpose(jnp.broadcast_to(b_last, (LANES, GLA_QK))))
        s_bd = s_bd * jnp.concatenate([decay] * GLA_HEADS, axis=1) + jnp.where(ss_mask, upd, 0.0)
    return jnp.concatenate(outs, axis=0), s_bd


def _swa_block(sq, kcat, vcat, bias_ref, sinks_ref, valid_t):
    half = _iota((1, LANES), 1) < SWA_HEAD_DIM
    top_rows = _iota((LANES, 1), 0) < SWA_HEAD_DIM
    k_roll = pltpu.roll(kcat, SWA_HEAD_DIM, 1)
    v_t = jnp.transpose(vcat)
    zeros_v = jnp.zeros((SWA_HEAD_DIM, 2 * WINDOW), F32)
    cols = []
    for kv in range(SWA_KV_HEADS):
        kk = jnp.where(half, kcat, k_roll) if kv == 0 else jnp.where(half, k_roll, kcat)
        q_parts = []
        for c in (2 * kv, 2 * kv + 1):
            qc = sq[:, c * LANES:(c + 1) * LANES]
            q_parts.append(jnp.where(half, qc, 0.0))
            q_parts.append(jnp.where(half, 0.0, qc))
        q_st = jnp.concatenate(q_parts, axis=0).astype(BF16)
        s = _dot_nt(kk.astype(BF16), q_st) * (SWA_HEAD_DIM ** -0.5) + bias_ref[kv]
        s = jnp.where(valid_t, s, NEG_INF)
        sink = jnp.concatenate(
            [jnp.full((1, WINDOW), sinks_ref[kv * SWA_GROUP + g], F32) for g in range(SWA_GROUP)], axis=1)
        m = jnp.maximum(jnp.max(s, axis=0, keepdims=True), sink)
        pr = jnp.exp(s - m)
        inv = 1.0 / (jnp.sum(pr, axis=0, keepdims=True) + jnp.exp(sink - m))
        pb = pr.astype(BF16)
        vk = v_t[kv * SWA_HEAD_DIM:(kv + 1) * SWA_HEAD_DIM]
        vv_t = jnp.concatenate([jnp.concatenate([vk, zeros_v], axis=1),
                                jnp.concatenate([zeros_v, vk], axis=1)], axis=0).astype(BF16)
        for pair in range(SWA_GROUP // 2):
            ce = slice(2 * pair * WINDOW, (2 * pair + 1) * WINDOW)
            co = slice((2 * pair + 1) * WINDOW, (2 * pair + 2) * WINDOW)
            p2_t = jnp.concatenate([pb[:, ce], pb[:, co]], axis=0)
            o2_t = _dot(vv_t, p2_t)
            o2_t = o2_t * jnp.where(top_rows, inv[:, ce], inv[:, co])
            cols.append(jnp.transpose(o2_t))
    return jnp.concatenate(cols, axis=1)


def _mixer_kernel(sinks_ref, x_ref, s0_ref, k0_ref, v0_ref, base0_ref, bias_ref,
                  g_mix_ref, w_in_ref, w_a_up_ref, b_a_ref, g_gla_ref, g_swa_ref, w_out_ref,
                  g_ffn_ref, w_r_ref, b_r_ref,
                  xmid_ref, hp_ref, topi_ref, gate_ref, rank_ref, sout_ref, kout_ref, vout_ref, cnt_ref,
                  s_scr, k_scr, v_scr, base_scr, *, n_lead_pad, prev_valid_from):
    b_id, j = pl.program_id(0), pl.program_id(1)
    tm = x_ref.shape[1]

    @pl.when(j == 0)
    def _():
        s_scr[...] = s0_ref[...]
        k_scr[...] = k0_ref[...]
        v_scr[...] = v0_ref[...]

    @pl.when((j == 0) & (b_id == 0))
    def _():
        base_scr[...] = base0_ref[...]

    x = x_ref[0]
    p = _project(x, g_mix_ref[...], w_in_ref[...], w_a_up_ref[...], b_a_ref[...])

    o_gla, s_new = _gla_chunks(p, j * tm, s_scr[...], n_lead_pad)
    s_scr[...] = s_new

    kj = _iota((2 * WINDOW, WINDOW), 0)
    qi = _iota((2 * WINDOW, WINDOW), 1)
    band = (kj > qi) & (kj <= qi + WINDOW)
    o_parts = []
    for sb in range(tm // WINDOW):
        rs = slice(sb * WINDOW, (sb + 1) * WINDOW)
        k_blk, v_blk = p["sk"][rs], p["sv"][rs]
        k_prev = k_scr[...] if sb == 0 else p["sk"][(sb - 1) * WINDOW:sb * WINDOW]
        v_prev = v_scr[...] if sb == 0 else p["sv"][(sb - 1) * WINDOW:sb * WINDOW]
        valid = band
        if sb == 0 and prev_valid_from:
            first = jnp.where(j == 0, prev_valid_from, 0)
            valid = band & (kj >= first)
        valid = jnp.concatenate([valid] * SWA_GROUP, axis=1)
        o_parts.append(_swa_block(p["sq"][rs], jnp.concatenate([k_prev, k_blk], axis=0),
                                  jnp.concatenate([v_prev, v_blk], axis=0), bias_ref, sinks_ref, valid))
    o_swa = jnp.concatenate(o_parts, axis=0)
    k_scr[...] = p["sk"][tm - WINDOW:tm]
    v_scr[...] = p["sv"][tm - WINDOW:tm]

    x_mid, hp, topi, gate8, rank8, base = _tail(
        x, o_gla, p["gr"], o_swa, g_gla_ref[...], g_swa_ref[...], w_out_ref[...],
        g_ffn_ref[...], w_r_ref[...], b_r_ref[...], base_scr[...])
    base_scr[...] = base
    xmid_ref[0] = x_mid
    _store_slabs(hp_ref, hp)
    topi_ref[...] = topi
    gate_ref[...] = gate8
    rank_ref[...] = rank8
    sout_ref[0] = s_new
    kout_ref[0] = p["sk"][tm - WINDOW:tm]
    vout_ref[0] = p["sv"][tm - WINDOW:tm]
    cnt_ref[...] = base


def _full_spec(shape):
    nd = len(shape)
    return pl.BlockSpec(shape, lambda *_: (0,) * nd)


def _mixer_call(x, s0, k0, v0, base0, wts, tm, n_lead_pad, prev_valid_from, hp_rows):
    B, L, _ = x.shape
    nj = L // tm
    T = B * L
    weight_args = (wts["bias"], wts["g_mix"], wts["w_in"], wts["w_a_up"], wts["b_a"], wts["g_gla"],
                   wts["g_swa"], wts["w_out"], wts["g_ffn"], wts["w_r"], wts["b_r"])
    in_specs = [
        pl.BlockSpec(memory_space=pltpu.SMEM),
        pl.BlockSpec((1, tm, D_MODEL), lambda b, j: (b, j, 0)),
        _full_spec(s0.shape), _full_spec(k0.shape), _full_spec(v0.shape), _full_spec(base0.shape),
    ] + [_full_spec(w.shape) for w in weight_args]
    tok_spec = pl.BlockSpec((8, tm), lambda b, j: (0, b * nj + j))
    out_specs = [
        pl.BlockSpec((1, tm, D_MODEL), lambda b, j: (b, j, 0)),
        pl.BlockSpec((tm * SLAB, LANES), lambda b, j: (b * nj + j, 0)),
        tok_spec, tok_spec, tok_spec,
        pl.BlockSpec((1, GLA_QK, GLA_V), lambda b, j: (b, 0, 0)),
        pl.BlockSpec((1, WINDOW, SWA_KV), lambda b, j: (b, 0, 0)),
        pl.BlockSpec((1, WINDOW, SWA_KV), lambda b, j: (b, 0, 0)),
        _full_spec((N_EXPERTS, LANES)),
    ]
    out_shape = [
        jax.ShapeDtypeStruct((B, L, D_MODEL), F32),
        jax.ShapeDtypeStruct((hp_rows * SLAB, LANES), F32),
        jax.ShapeDtypeStruct((8, T), jnp.int32),
        jax.ShapeDtypeStruct((8, T), F32),
        jax.ShapeDtypeStruct((8, T), jnp.int32),
        jax.ShapeDtypeStruct((B, GLA_QK, GLA_V), F32),
        jax.ShapeDtypeStruct((B, WINDOW, SWA_KV), F32),
        jax.ShapeDtypeStruct((B, WINDOW, SWA_KV), F32),
        jax.ShapeDtypeStruct((N_EXPERTS, LANES), F32),
    ]
    kern = functools.partial(_mixer_kernel, n_lead_pad=n_lead_pad, prev_valid_from=prev_valid_from)
    return pl.pallas_call(
        kern,
        grid=(B, nj),
        in_specs=in_specs,
        out_specs=out_specs,
        out_shape=out_shape,
        scratch_shapes=[pltpu.VMEM((GLA_QK, GLA_V), F32), pltpu.VMEM((WINDOW, SWA_KV), F32),
                        pltpu.VMEM((WINDOW, SWA_KV), F32), pltpu.VMEM((N_EXPERTS, LANES), F32)],
        compiler_params=pltpu.CompilerParams(dimension_semantics=("arbitrary", "arbitrary"),
                                             vmem_limit_bytes=VMEM_LIMIT),
        name="mixer",
    )(wts["sinks"], x, s0, k0, v0, base0, *weight_args)


def _decode_kernel(sinks_ref, x_ref, st_ref, ck_ref, cv_ref, base0_ref, bias_ref,
                   g_mix_ref, w_in_ref, w_a_up_ref, b_a_ref, g_gla_ref, g_swa_ref, w_out_ref,
                   g_ffn_ref, w_r_ref, b_r_ref,
                   xmid_ref, hp_ref, topi_ref, gate_ref, rank_ref, sto_ref, cko_ref, cvo_ref, cnt_ref,
                   at_scr, kt_scr, qt_scr, gv_scr, gr_scr, sq_scr, sk_scr, sv_scr, og_scr, os_scr):
    i = pl.program_id(0)
    n_seq = x_ref.shape[0]

    @pl.when(i == 0)
    def _():
        p = _project(x_ref[...], g_mix_ref[...], w_in_ref[...], w_a_up_ref[...], b_a_ref[...])
        at_scr[...] = jnp.transpose(jnp.exp(p["log_a"]))
        kt_scr[...] = jnp.transpose(p["gk"])
        qt_scr[...] = jnp.transpose(p["gq"])
        gv_scr[...] = p["gv"]
        gr_scr[...] = p["gr"]
        sq_scr[...] = p["sq"]
        sk_scr[...] = p["sk"]
        sv_scr[...] = p["sv"]

    lane_seq = _iota((GLA_QK, n_seq), 1)
    half = _iota((1, LANES), 1) < SWA_HEAD_DIM
    row_id = _iota((WINDOW, SWA_KV), 0)
    head_diag = (_iota((16, SWA_Q), 1) // SWA_HEAD_DIM) == _iota((16, SWA_Q), 0)
    sink_col = jnp.concatenate(
        [jnp.full((1, 1), sinks_ref[h], F32) for h in range(SWA_HEADS)] + [jnp.zeros((8, 1), F32)], axis=0)

    def per_seq(sl, carry):
        s = i * DEC_SB + sl
        sel = lane_seq == s
        a_c = jnp.sum(jnp.where(sel, at_scr[...], 0.0), axis=1, keepdims=True)
        k_c = jnp.sum(jnp.where(sel, kt_scr[...], 0.0), axis=1, keepdims=True)
        q_c = jnp.sum(jnp.where(sel, qt_scr[...], 0.0), axis=1, keepdims=True)
        st = st_ref[sl].reshape(GLA_QK, GLA_DV)
        v_row = gv_scr[pl.ds(s, 1), :]
        v_b = jnp.concatenate(
            [jnp.broadcast_to(v_row[:, h * GLA_DV:(h + 1) * GLA_DV], (GLA_DK, GLA_DV))
             for h in range(GLA_HEADS)], axis=0)
        st_new = a_c * st + k_c * v_b
        sto_ref[sl] = st_new.reshape(GLA_HEADS, GLA_DK, GLA_DV)
        t = q_c * st_new
        og_scr[pl.ds(s, 1), :] = jnp.concatenate(
            [jnp.sum(t[h * GLA_DK:(h + 1) * GLA_DK], axis=0, keepdims=True) for h in range(GLA_HEADS)],
            axis=1)

        k_new = sk_scr[pl.ds(s, 1), :]
        v_new = sv_scr[pl.ds(s, 1), :]
        kn = jnp.where(row_id == WINDOW - 1, k_new, pltpu.roll(ck_ref[sl], WINDOW - 1, 0))
        vn = jnp.where(row_id == WINDOW - 1, v_new, pltpu.roll(cv_ref[sl], WINDOW - 1, 0))
        cko_ref[sl] = kn
        cvo_ref[sl] = vn
        kr, vr = pltpu.roll(kn, SWA_HEAD_DIM, 1), pltpu.roll(vn, SWA_HEAD_DIM, 1)
        k0, k1 = jnp.where(half, kn, kr), jnp.where(half, kr, kn)
        v0, v1 = jnp.where(half, vn, vr), jnp.where(half, vr, vn)
        kw = jnp.concatenate([k0, k0, k1, k1], axis=1).astype(BF16)
        vw = jnp.concatenate([v0, v0, v1, v1], axis=1).astype(BF16)
        q_row = sq_scr[pl.ds(s, 1), :]
        qm = jnp.where(head_diag, jnp.broadcast_to(q_row, (16, SWA_Q)), 0.0).astype(BF16)
        sc = _dot_nt(qm, kw) * (SWA_HEAD_DIM ** -0.5) + bias_ref[...]
        m = jnp.maximum(jnp.max(sc, axis=1, keepdims=True), sink_col)
        pr = jnp.exp(sc - m)
        inv = 1.0 / (jnp.sum(pr, axis=1, keepdims=True) + jnp.exp(sink_col - m))
        ow = _dot(pr.astype(BF16), vw) * inv
        os_scr[pl.ds(s, 1), :] = jnp.sum(jnp.where(head_diag, ow, 0.0), axis=0, keepdims=True)
        return carry

    lax.fori_loop(0, DEC_SB, per_seq, 0, unroll=2)

    @pl.when(i == pl.num_programs(0) - 1)
    def _():
        x_mid, hp, topi, gate8, rank8, base = _tail(
            x_ref[...], og_scr[...], gr_scr[...], os_scr[...], g_gla_ref[...], g_swa_ref[...],
            w_out_ref[...], g_ffn_ref[...], w_r_ref[...], b_r_ref[...], base0_ref[...])
        xmid_ref[...] = x_mid
        _store_slabs(hp_ref, hp)
        topi_ref[...] = topi
        gate_ref[...] = gate8
        rank_ref[...] = rank8
        cnt_ref[...] = base


def _decode_call(xs, state, ck, cv, base0, bias_dec, wts):
    n_seq = xs.shape[0]
    nb = n_seq // DEC_SB
    weight_args = (wts["g_mix"], wts["w_in"], wts["w_a_up"], wts["b_a"], wts["g_gla"],
                   wts["g_swa"], wts["w_out"], wts["g_ffn"], wts["w_r"], wts["b_r"])
    in_specs = [
        pl.BlockSpec(memory_space=pltpu.SMEM),
        _full_spec(xs.shape),
        pl.BlockSpec((DEC_SB, GLA_HEADS, GLA_DK, GLA_DV), lambda i: (i, 0, 0, 0)),
        pl.BlockSpec((DEC_SB, WINDOW, SWA_KV), lambda i: (i, 0, 0)),
        pl.BlockSpec((DEC_SB, WINDOW, SWA_KV), lambda i: (i, 0, 0)),
        _full_spec(base0.shape), _full_spec(bias_dec.shape),
    ] + [_full_spec(w.shape) for w in weight_args]
    out_specs = [
        _full_spec((n_seq, D_MODEL)),
        _full_spec((n_seq * SLAB, LANES)),
        _full_spec((8, n_seq)), _full_spec((8, n_seq)), _full_spec((8, n_seq)),
        pl.BlockSpec((DEC_SB, GLA_HEADS, GLA_DK, GLA_DV), lambda i: (i, 0, 0, 0)),
        pl.BlockSpec((DEC_SB, WINDOW, SWA_KV), lambda i: (i, 0, 0)),
        pl.BlockSpec((DEC_SB, WINDOW, SWA_KV), lambda i: (i, 0, 0)),
        _full_spec((N_EXPERTS, LANES)),
    ]
    out_shape = [
        jax.ShapeDtypeStruct((n_seq, D_MODEL), F32),
        jax.ShapeDtypeStruct((n_seq * SLAB, LANES), F32),
        jax.ShapeDtypeStruct((8, n_seq), jnp.int32),
        jax.ShapeDtypeStruct((8, n_seq), F32),
        jax.ShapeDtypeStruct((8, n_seq), jnp.int32),
        jax.ShapeDtypeStruct(state.shape, F32),
        jax.ShapeDtypeStruct(ck.shape, F32),
        jax.ShapeDtypeStruct(cv.shape, F32),
        jax.ShapeDtypeStruct((N_EXPERTS, LANES), F32),
    ]
    scratch = [pltpu.VMEM((GLA_QK, n_seq), F32)] * 3 + [
        pltpu.VMEM((n_seq, GLA_V), F32), pltpu.VMEM((n_seq, GLA_V), F32), pltpu.VMEM((n_seq, SWA_Q), F32),
        pltpu.VMEM((n_seq, SWA_KV), F32), pltpu.VMEM((n_seq, SWA_KV), F32),
        pltpu.VMEM((n_seq, GLA_V), F32), pltpu.VMEM((n_seq, SWA_Q), F32)]
    return pl.pallas_call(
        _decode_kernel,
        grid=(nb,),
        in_specs=in_specs,
        out_specs=out_specs,
        out_shape=out_shape,
        scratch_shapes=scratch,
        compiler_params=pltpu.CompilerParams(dimension_semantics=("arbitrary",),
                                             vmem_limit_bytes=VMEM_LIMIT),
        name="decode",
    )(wts["sinks"], xs, state, ck, cv, base0, bias_dec, *weight_args)


SC_CORES = 2
SC_SUBCORES = 16
SC_WORKERS = SC_CORES * SC_SUBCORES
SC_SCATTER_ROWS = 32
SC_GATHER_ROWS = 24


def _sc_mesh():
    return plsc.VectorSubcoreMesh(core_axis_name="c", subcore_axis_name="s")


def _sc_worker_id():
    return lax.axis_index("s") * SC_CORES + lax.axis_index("c")


def _sc_scatter_rows(src_p, src_s, idx_p, idx_s, n_out):
    rows = SC_SCATTER_ROWS
    n_chunks = idx_p.shape[0] // SC_WORKERS
    n_s, _, rows_s = idx_s.shape
    assert n_chunks * SC_WORKERS == idx_p.shape[0] and n_chunks % 2 == 0 and n_s <= SC_WORKERS

    @functools.partial(
        pl.kernel, mesh=_sc_mesh(),
        out_type=jax.ShapeDtypeStruct((n_out, SLAB, LANES), F32),
        scratch_types=[pltpu.VMEM((2, TOP_K, rows), jnp.int32), pltpu.VMEM((2, rows, SLAB, LANES), F32),
                       pltpu.VMEM((TOP_K, rows_s), jnp.int32), pltpu.VMEM((rows_s, SLAB, LANES), F32),
                       pltpu.SemaphoreType.DMA((2,)), pltpu.SemaphoreType.DMA((2,))])
    def scatter_rows(srcp_hbm, srcs_hbm, idxp_hbm, idxs_hbm, out_hbm, idx_v, rows_v, idxs_v, rowss_v, lsem, ssem):
        wid = _sc_worker_id()

        def loads(c, b):
            g = wid * n_chunks + c
            return (pltpu.make_async_copy(idxp_hbm.at[g], idx_v.at[b], lsem.at[b]),
                    pltpu.make_async_copy(srcp_hbm.at[pl.ds(pl.multiple_of(g * rows, 8), rows)], rows_v.at[b],
                                          lsem.at[b]))

        def scatters(b):
            return [pltpu.make_async_copy(rows_v.at[b], out_hbm.at[idx_v.at[b, k]], ssem.at[b])
                    for k in range(TOP_K)]

        for d in loads(0, 0):
            d.start()

        @pl.loop(0, n_chunks, step=2)
        def _(c0):
            for b in range(2):
                c = c0 + b
                for d in loads(c, b):
                    d.wait()

                @pl.when(c >= 1)
                def _():
                    for d in scatters(1 - b):
                        d.wait()

                @pl.when(c + 1 < n_chunks)
                def _():
                    for d in loads(c + 1, 1 - b):
                        d.start()

                for d in scatters(b):
                    d.start()

        for d in scatters((n_chunks - 1) % 2):
            d.wait()

        @pl.when(wid < n_s)
        def _():
            pltpu.sync_copy(idxs_hbm.at[wid], idxs_v)
            pltpu.sync_copy(srcs_hbm.at[pl.ds(pl.multiple_of(wid * rows_s, 8), rows_s)], rowss_v)
            for k in range(TOP_K):
                pltpu.sync_copy(rowss_v, out_hbm.at[idxs_v.at[k]])

    return scatter_rows(src_p, src_s, idx_p, idx_s)


def _sc_gather_rows(src3, idx2):
    rows = SC_GATHER_ROWS
    n_chunks = idx2.shape[0] // SC_WORKERS
    assert n_chunks * SC_WORKERS == idx2.shape[0] and idx2.shape[1] == rows and n_chunks % 2 == 0

    @functools.partial(
        pl.kernel, mesh=_sc_mesh(),
        out_type=jax.ShapeDtypeStruct((idx2.shape[0] * rows, SLAB, LANES), F32),
        scratch_types=[pltpu.VMEM((2, rows), jnp.int32), pltpu.VMEM((2, rows, SLAB, LANES), F32),
                       pltpu.SemaphoreType.DMA((2,)), pltpu.SemaphoreType.DMA((2,))])
    def gather_rows(src_hbm, idx_hbm, out_hbm, idx_v, rows_v, gsem, wsem):
        wid = _sc_worker_id()

        def gather(b):
            return pltpu.make_async_copy(src_hbm.at[idx_v.at[b]], rows_v.at[b], gsem.at[b])

        def write(c, b):
            base = pl.multiple_of((wid * n_chunks + c) * rows, 8)
            return pltpu.make_async_copy(rows_v.at[b], out_hbm.at[pl.ds(base, rows)], wsem.at[b])

        pltpu.sync_copy(idx_hbm.at[wid * n_chunks], idx_v.at[0])
        gather(0).start()

        @pl.loop(0, n_chunks, step=2)
        def _(c0):
            for b in range(2):
                c = c0 + b

                @pl.when(c + 1 < n_chunks)
                def _():
                    @pl.when(c >= 1)
                    def _():
                        write(c - 1, 1 - b).wait()
                    pltpu.sync_copy(idx_hbm.at[wid * n_chunks + c + 1], idx_v.at[1 - b])
                    gather(1 - b).start()

                gather(b).wait()
                write(c, b).start()

        write(n_chunks - 2, 0).wait()
        write(n_chunks - 1, 1).wait()

    return gather_rows(src3, idx2)


def _ffn_kernel(blk_e_ref, nused_ref, x_ref, wu_ref, bu_ref, wd_ref, bd_ref, y_ref, xbf, actbf, wu_bf, wd_bf):
    i = pl.program_id(0)
    tm = MOE_TM
    n_tiles = D_FF // FF_TILE

    @pl.when(i < nused_ref[0])
    def _():
        @pl.when((i == 0) | (blk_e_ref[i] != blk_e_ref[jnp.maximum(i - 1, 0)]))
        def _():
            wu_bf[...] = wu_ref[0].astype(BF16)
            wd_bf[...] = wd_ref[0].astype(BF16)

        for c in range(SLAB):
            xbf[:, c * LANES:(c + 1) * LANES] = _load_slab_chunk(x_ref, tm, c, SLAB).astype(BF16)
        for n in range(n_tiles):
            gc = slice(n * FF_TILE, (n + 1) * FF_TILE)
            lc = slice(D_FF + n * FF_TILE, D_FF + (n + 1) * FF_TILE)
            g = jnp.minimum(_dot(xbf[...], wu_bf[:, gc]) + bu_ref[0, :, gc], SWIGLU_LIMIT)
            lin = jnp.clip(_dot(xbf[...], wu_bf[:, lc]) + bu_ref[0, :, lc], -SWIGLU_LIMIT, SWIGLU_LIMIT)
            actbf[:, gc] = (g * jax.nn.sigmoid(SWIGLU_ALPHA * g) * (lin + 1.0)).astype(BF16)
        for n in range(n_tiles):
            yc = slice(n * FF_TILE, (n + 1) * FF_TILE)
            y = _dot(actbf[...], wd_bf[:, yc]) + bd_ref[0, :, yc]
            for c in range(FF_TILE // LANES):
                y_ref[pl.ds(n * (FF_TILE // LANES) + c, tm, stride=SLAB), :] = y[:, c * LANES:(c + 1) * LANES]

    @pl.when(i >= nused_ref[0])
    def _():
        y_ref[...] = jnp.zeros_like(y_ref)


def _ffn_call(blk_e, nused, xs2, w_up, b_up, w_down, b_down):
    n_blocks = blk_e.shape[0]
    tm = MOE_TM
    row_blk = pl.BlockSpec((tm * SLAB, LANES), lambda i, be, nu: (i, 0))
    grid_spec = pltpu.PrefetchScalarGridSpec(
        num_scalar_prefetch=2,
        grid=(n_blocks,),
        in_specs=[
            row_blk,
            pl.BlockSpec((1, D_MODEL, 2 * D_FF), lambda i, be, nu: (be[i], 0, 0)),
            pl.BlockSpec((1, 1, 2 * D_FF), lambda i, be, nu: (be[i], 0, 0)),
            pl.BlockSpec((1, D_FF, D_MODEL), lambda i, be, nu: (be[i], 0, 0)),
            pl.BlockSpec((1, 1, D_MODEL), lambda i, be, nu: (be[i], 0, 0)),
        ],
        out_specs=row_blk,
        scratch_shapes=[pltpu.VMEM((tm, D_MODEL), BF16), pltpu.VMEM((tm, D_FF), BF16),
                        pltpu.VMEM((D_MODEL, 2 * D_FF), BF16), pltpu.VMEM((D_FF, D_MODEL), BF16)],
    )
    return pl.pallas_call(
        _ffn_kernel,
        grid_spec=grid_spec,
        out_shape=jax.ShapeDtypeStruct((n_blocks * tm * SLAB, LANES), F32),
        compiler_params=pltpu.CompilerParams(dimension_semantics=("arbitrary",),
                                             vmem_limit_bytes=VMEM_LIMIT),
        name="experts",
    )(blk_e, nused, xs2, w_up, b_up.reshape(N_EXPERTS, 1, 2 * D_FF), w_down, b_down.reshape(N_EXPERTS, 1, D_MODEL))


FF_TILE = 256


RING = 3


def _expert_kernel(blk_e_ref, nused_ref, tok0_ref, tok1_ref, tokn_ref, retp_ref, hp_hbm, wu_ref, bu_ref,
                   wd_ref, bd_ref, y_hbm, xbuf0, xbuf1, xbuf2, ybuf0, ybuf1, ybuf2, xbf, actbf, zbuf,
                   wu_bf, wd_bf, gsem, ssem, zsem):
    i = pl.program_id(0)
    tm = MOE_TM
    nused = nused_ref[0]
    n_tiles = D_FF // FF_TILE
    xbufs, ybufs = (xbuf0, xbuf1, xbuf2), (ybuf0, ybuf1, ybuf2)

    def gather(tok_ref, r, s):
        src = hp_hbm.at[pl.ds(pl.multiple_of(tok_ref[0, 0, r], SLAB), SLAB)]
        return pltpu.make_async_copy(src, xbufs[s].at[pl.ds(r * SLAB, SLAB)], gsem.at[s])

    def scatter(r, s):
        dst = y_hbm.at[pl.ds(pl.multiple_of(retp_ref[0, 0, r], SLAB), SLAB)]
        return pltpu.make_async_copy(ybufs[s].at[pl.ds(r * SLAB, SLAB)], dst, ssem.at[s])

    def wait_gather(s):
        pltpu.make_async_copy(hp_hbm.at[pl.ds(0, tm * SLAB)], xbufs[s], gsem.at[s]).wait()

    def wait_scatter(s):
        pltpu.make_async_copy(ybufs[s], y_hbm.at[pl.ds(0, tm * SLAB)], ssem.at[s]).wait()

    @pl.when(i == 0)
    def _():
        ybuf2[...] = jnp.zeros_like(ybuf2)
        zbuf[...] = jnp.zeros_like(zbuf)

        def issue(r, c):
            gather(tok0_ref, r, 0).start()
            gather(tok1_ref, r, 1).start()
            return c
        lax.fori_loop(0, tm, issue, 0)

    def compute(s):
        nxt = (s + 2) % RING
        wait_gather(s)

        @pl.when(i >= 2)
        def _():
            wait_scatter(s)

        @pl.when((i == 0) | (blk_e_ref[i] != blk_e_ref[jnp.maximum(i - 1, 0)]))
        def _():
            wu_bf[...] = wu_ref[0].astype(BF16)
            wd_bf[...] = wd_ref[0].astype(BF16)

        for c in range(SLAB):
            xbf[:, c * LANES:(c + 1) * LANES] = _load_slab_chunk(xbufs[s], tm, c, SLAB).astype(BF16)
        for r in range(tm):
            gather(tokn_ref, r, nxt).start(priority=r % 2)
            scatter(r, nxt).start(priority=r % 2)
        for n in range(n_tiles):
            gc = slice(n * FF_TILE, (n + 1) * FF_TILE)
            lc = slice(D_FF + n * FF_TILE, D_FF + (n + 1) * FF_TILE)
            g = jnp.minimum(_dot(xbf[...], wu_bf[:, gc]) + bu_ref[0, :, gc], SWIGLU_LIMIT)
            lin = jnp.clip(_dot(xbf[...], wu_bf[:, lc]) + bu_ref[0, :, lc], -SWIGLU_LIMIT, SWIGLU_LIMIT)
            actbf[:, gc] = (g * jax.nn.sigmoid(SWIGLU_ALPHA * g) * (lin + 1.0)).astype(BF16)
        for n in range(n_tiles):
            yc = slice(n * FF_TILE, (n + 1) * FF_TILE)
            y = _dot(actbf[...], wd_bf[:, yc]) + bd_ref[0, :, yc]
            for c in range(FF_TILE // LANES):
                ybufs[s][pl.ds(n * (FF_TILE // LANES) + c, tm, stride=SLAB), :] = y[:, c * LANES:(c + 1) * LANES]

    def drain(s):
        wait_gather(s)
        wait_gather((s + 1) % RING)
        wait_scatter(s)
        wait_scatter((s + 1) % RING)
        last = (s + 2) % RING

        def issue(r, c):
            scatter(r, last).start()
            return c
        lax.fori_loop(0, tm, issue, 0)
        wait_scatter(last)

    for s in range(RING):
        @pl.when((i < nused) & (i % RING == s))
        def _():
            compute(s)

        @pl.when((i == nused) & (i % RING == s))
        def _():
            drain(s)

    @pl.when(i >= nused)
    def _():
        zc = pltpu.make_async_copy(zbuf, y_hbm.at[pl.ds(pl.multiple_of((tm + i * tm) * SLAB, SLAB), tm * SLAB)], zsem)
        zc.start()
        zc.wait()


def _expert_call(blk_e, nused, tok_sorted, ret_sorted, hp_all, w_up, b_up, w_down, b_down, n_slots):
    n_blocks = blk_e.shape[0]
    tm = MOE_TM
    n_rows = n_blocks * tm + tm
    ret_tab = (jnp.concatenate([n_slots + jnp.arange(tm, dtype=jnp.int32), ret_sorted]) * SLAB
               ).reshape(n_blocks + 1, 1, tm)
    tok_tab = (tok_sorted * SLAB).reshape(n_blocks, 1, tm)
    smem_blk = functools.partial(pl.BlockSpec, (1, 1, tm), memory_space=pltpu.SMEM)
    grid_spec = pltpu.PrefetchScalarGridSpec(
        num_scalar_prefetch=2,
        grid=(n_blocks,),
        in_specs=[
            smem_blk(lambda i, be, nu: (0, 0, 0)),
            smem_blk(lambda i, be, nu: (1, 0, 0)),
            smem_blk(lambda i, be, nu: (jnp.minimum(i + 2, n_blocks - 1), 0, 0)),
            smem_blk(lambda i, be, nu: (i, 0, 0)),
            pl.BlockSpec(memory_space=pl.ANY),
            pl.BlockSpec((1, D_MODEL, 2 * D_FF), lambda i, be, nu: (be[i], 0, 0)),
            pl.BlockSpec((1, 1, 2 * D_FF), lambda i, be, nu: (be[i], 0, 0)),
            pl.BlockSpec((1, D_FF, D_MODEL), lambda i, be, nu: (be[i], 0, 0)),
            pl.BlockSpec((1, 1, D_MODEL), lambda i, be, nu: (be[i], 0, 0)),
        ],
        out_specs=pl.BlockSpec(memory_space=pl.ANY),
        scratch_shapes=[pltpu.VMEM((tm * SLAB, LANES), F32)] * (2 * RING) + [
                        pltpu.VMEM((tm, D_MODEL), BF16), pltpu.VMEM((tm, D_FF), BF16),
                        pltpu.VMEM((tm * SLAB, LANES), F32),
                        pltpu.VMEM((D_MODEL, 2 * D_FF), BF16), pltpu.VMEM((D_FF, D_MODEL), BF16),
                        pltpu.SemaphoreType.DMA((RING,)), pltpu.SemaphoreType.DMA((RING,)),
                        pltpu.SemaphoreType.DMA],
    )
    return pl.pallas_call(
        _expert_kernel,
        grid_spec=grid_spec,
        out_shape=jax.ShapeDtypeStruct((n_rows * SLAB, LANES), F32),
        compiler_params=pltpu.CompilerParams(dimension_semantics=("arbitrary",),
                                             vmem_limit_bytes=VMEM_LIMIT),
        name="experts",
    )(blk_e, nused, tok_tab, tok_tab, tok_tab, ret_tab, hp_all,
      w_up, b_up.reshape(N_EXPERTS, 1, 2 * D_FF), w_down, b_down.reshape(N_EXPERTS, 1, D_MODEL))


def _combine_kernel(ys0_ref, ys1_ref, ys2_ref, ys3_ref, xmid_ref, gate_ref, g_final_ref, y_ref):
    tm = xmid_ref.shape[0]
    gts = gate_ref[...]
    chunks = []
    for c in range(SLAB):
        acc = xmid_ref[:, c * LANES:(c + 1) * LANES]
        for k, ys_ref in enumerate((ys0_ref, ys1_ref, ys2_ref, ys3_ref)):
            acc = acc + _load_slab_chunk(ys_ref, tm, c, SLAB) * gts[:, k:k + 1]
        chunks.append(acc)
    y_ref[...] = _rms(jnp.concatenate(chunks, axis=1), g_final_ref[...])


def _combine_call(ys4, t_stride, row0, x_mid, gates, g_final, tm):
    T = x_mid.shape[0]
    blk0 = row0 // tm
    per_k = t_stride // tm
    assert per_k * tm == t_stride and blk0 * tm == row0

    def ys_spec(k):
        return pl.BlockSpec((tm * SLAB, LANES), lambda i: (k * per_k + blk0 + i, 0))

    return pl.pallas_call(
        _combine_kernel,
        grid=(T // tm,),
        in_specs=[
            ys_spec(0), ys_spec(1), ys_spec(2), ys_spec(3),
            pl.BlockSpec((tm, D_MODEL), lambda i: (i, 0)),
            pl.BlockSpec((tm, TOP_K), lambda i: (i, 0)),
            _full_spec((1, D_MODEL)),
        ],
        out_specs=pl.BlockSpec((tm, D_MODEL), lambda i: (i, 0)),
        out_shape=jax.ShapeDtypeStruct((T, D_MODEL), F32),
        compiler_params=pltpu.CompilerParams(dimension_semantics=("arbitrary",),
                                             vmem_limit_bytes=VMEM_LIMIT),
        name="combine",
    )(ys4, ys4, ys4, ys4, x_mid, gates, g_final)


def _t5_bucket(dist):
    n = jnp.maximum(dist, 0)
    max_exact = NUM_BUCKETS // 2
    nf = jnp.maximum(n, 1).astype(F32)
    large = max_exact + (jnp.log(nf / max_exact) / math.log(MAX_DISTANCE / max_exact)
                         * (NUM_BUCKETS - max_exact)).astype(jnp.int32)
    large = jnp.minimum(large, NUM_BUCKETS - 1)
    return jnp.where(n < max_exact, n, large)


def kernel(x_prompt, x_sample, state_gla, cache_swa_k, cache_swa_v, meta_tokens, rel_bias_table,
           g_mix, w_in, w_a_up, b_a, g_gla_out, g_swa_out, attn_sinks, w_out,
           g_ffn, w_router, b_router, w_up, b_up, w_down, b_down, g_final):
    assert g_mix.shape[0] == 1, "single-layer trunk"
    B, L, _ = x_prompt.shape
    n_seq = x_sample.shape[0]
    TP = B * L
    T_all = TP + n_seq

    wi = w_in[0]
    sizes = (GLA_QK, GLA_QK, GLA_V, GLA_V, GLA_LOWRANK, SWA_Q, SWA_KV, SWA_KV)
    offs = [0]
    for s in sizes:
        offs.append(offs[-1] + s)
    seg = [wi[:, offs[n]:offs[n + 1]] for n in range(8)]
    w_in_r = jnp.concatenate(
        seg[0:4] + seg[5:8] + [seg[4], jnp.zeros((D_MODEL, LANES - GLA_LOWRANK), F32)], axis=1).astype(BF16)
    w_a_pad = jnp.concatenate([w_a_up[0], jnp.zeros((LANES - GLA_LOWRANK, GLA_QK), F32)], axis=0).astype(BF16)
    wr_t = jnp.transpose(w_router[0])
    wr_hi = wr_t.astype(BF16)
    wr_lo = (wr_t - wr_hi.astype(F32)).astype(BF16)
    qi = jnp.arange(WINDOW)[:, None]
    kj = jnp.arange(2 * WINDOW)[None, :]
    buckets = jnp.arange(NUM_BUCKETS)
    table = rel_bias_table.astype(F32)
    oh_p = (_t5_bucket(qi - kj + WINDOW)[..., None] == buckets).astype(F32)
    bias_p = jnp.einsum("qkb,bh->hkq", oh_p, table, precision=lax.Precision.HIGHEST)
    bias_p = bias_p.reshape(SWA_KV_HEADS, SWA_GROUP, 2 * WINDOW, WINDOW).transpose(0, 2, 1, 3)
    bias_p = bias_p.reshape(SWA_KV_HEADS, 2 * WINDOW, SWA_GROUP * WINDOW)
    oh_d = (_t5_bucket(WINDOW - 1 - jnp.arange(WINDOW))[:, None] == buckets).astype(F32)
    bias_d = jnp.einsum("rb,bh->hr", oh_d, table, precision=lax.Precision.HIGHEST)
    bias_d = jnp.concatenate([bias_d, jnp.zeros((8, WINDOW), F32)], axis=0)
    wts = dict(
        sinks=attn_sinks[0].astype(F32), bias=bias_p,
        g_mix=g_mix[0][None], w_in=w_in_r, w_a_up=w_a_pad, b_a=b_a[0][None],
        g_gla=g_gla_out[0][None], g_swa=g_swa_out[0][None], w_out=w_out[0].astype(BF16),
        g_ffn=g_ffn[0][None], w_r=jnp.concatenate([wr_hi, wr_lo], axis=0), b_r=b_router[0][:, None],
    )

    x_pre = jnp.concatenate([jnp.zeros((WINDOW - N_META, D_MODEL), F32), meta_tokens.astype(F32)], axis=0)[None]
    zeros_s = jnp.zeros((GLA_QK, GLA_V), F32)
    zeros_kv = jnp.zeros((WINDOW, SWA_KV), F32)
    zeros_b = jnp.zeros((N_EXPERTS, LANES), F32)
    pre = _mixer_call(x_pre, zeros_s, zeros_kv, zeros_kv, zeros_b, wts, WINDOW, WINDOW - N_META, 0, WINDOW)
    s_meta, k_meta, v_meta = pre[5][0], pre[6][0], pre[7][0]

    (xmid_p, hp_p, topi_p, gate_p, rank_p, s_p, k_p, v_p, cnt_p) = _mixer_call(
        x_prompt, s_meta, k_meta, v_meta, zeros_b, wts, MIX_TM, 0, WINDOW - N_META, TP)

    (xmid_s, hp_s, topi_s, gate_s, rank_s, st_s, ck_s, cv_s, cnt_all) = _decode_call(
        x_sample[:, 0], state_gla[0], cache_swa_k[0].reshape(n_seq, WINDOW, SWA_KV),
        cache_swa_v[0].reshape(n_seq, WINDOW, SWA_KV), cnt_p, bias_d, wts)

    tm = MOE_TM
    n_slots = T_all * TOP_K
    n_blocks = -(-n_slots // tm) + N_EXPERTS
    top_e = jnp.concatenate([topi_p[:TOP_K], topi_s[:TOP_K]], axis=1)
    rank = jnp.concatenate([rank_p[:TOP_K], rank_s[:TOP_K]], axis=1)
    counts = cnt_all[:, 0].astype(jnp.int32)
    padded = (counts + tm - 1) // tm * tm
    pad_end = jnp.cumsum(padded)
    pad_start = pad_end - padded
    e_ids = jnp.arange(N_EXPERTS, dtype=jnp.int32)
    dest = jnp.sum(jnp.where(top_e[..., None] == e_ids, pad_start, 0), axis=-1) + rank
    n_pad = n_blocks * tm
    blk_e = jnp.minimum(jnp.sum(pad_end[None] <= (jnp.arange(n_blocks, dtype=jnp.int32) * tm)[:, None], axis=1),
                        N_EXPERTS - 1).astype(jnp.int32)
    nused = (pad_end[-1] // tm).astype(jnp.int32).reshape(1)

    sample_rows = 8
    idx_p = dest[:, :TP].reshape(TOP_K, TP // SC_SCATTER_ROWS, SC_SCATTER_ROWS).transpose(1, 0, 2)
    idx_s = dest[:, TP:].reshape(TOP_K, n_seq // sample_rows, sample_rows).transpose(1, 0, 2)
    xs3 = _sc_scatter_rows(hp_p.reshape(TP, SLAB, LANES), hp_s.reshape(n_seq, SLAB, LANES), idx_p, idx_s, n_pad)
    ys2 = _ffn_call(blk_e, nused, xs3.reshape(-1, LANES), w_up[0], b_up[0], w_down[0], b_down[0])
    unit = math.lcm(2 * SC_WORKERS * SC_GATHER_ROWS // TOP_K, MIX_TM)
    t_stride = -(-T_all // unit) * unit
    filler = jnp.arange(TOP_K * (t_stride - T_all), dtype=jnp.int32).reshape(TOP_K, t_stride - T_all)
    slot_src = jnp.concatenate([dest, filler], axis=1)
    slot_src = slot_src.reshape(TOP_K * t_stride // SC_GATHER_ROWS, SC_GATHER_ROWS)
    ys4 = _sc_gather_rows(ys2.reshape(-1, SLAB, LANES), slot_src).reshape(-1, LANES)

    gates = jnp.transpose(jnp.concatenate([gate_p[:TOP_K], gate_s[:TOP_K]], axis=1))
    gf = g_final[None]
    y_p = _combine_call(ys4, t_stride, 0, xmid_p.reshape(TP, D_MODEL), gates[:TP], gf, MIX_TM)
    y_s = _combine_call(ys4, t_stride, TP, xmid_s, gates[TP:], gf, n_seq)

    s_heads = jnp.stack([s_p[:, h * GLA_DK:(h + 1) * GLA_DK, h * GLA_DV:(h + 1) * GLA_DV]
                         for h in range(GLA_HEADS)], axis=1)
    return (y_p.reshape(B, L, D_MODEL), y_s.reshape(n_seq, 1, D_MODEL), s_heads[None],
            k_p.reshape(1, B, WINDOW, SWA_KV_HEADS, SWA_HEAD_DIM),
            v_p.reshape(1, B, WINDOW, SWA_KV_HEADS, SWA_HEAD_DIM),
            st_s[None], ck_s.reshape(1, n_seq, WINDOW, SWA_KV_HEADS, SWA_HEAD_DIM),
            cv_s.reshape(1, n_seq, WINDOW, SWA_KV_HEADS, SWA_HEAD_DIM))
```

```python
import functools
import math

import jax
import jax.numpy as jnp
from jax import lax
from jax.experimental import pallas as pl
from jax.experimental.pallas import tpu as pltpu
from jax.experimental.pallas import tpu_sc as plsc

D_MODEL = 1024
N_META = 16
GLA_HEADS = 4
GLA_DK = 64
GLA_DV = 128
GLA_LOWRANK = 16
GLA_GATE_TAU = 16.0
GLA_CHUNK = 64
SWA_HEADS = 8
SWA_KV_HEADS = 2
SWA_HEAD_DIM = 64
SWA_GROUP = SWA_HEADS // SWA_KV_HEADS
WINDOW = 128
NUM_BUCKETS = 32
MAX_DISTANCE = 128
N_EXPERTS = 32
TOP_K = 4
D_FF = 1024
SWIGLU_ALPHA = 1.702
SWIGLU_LIMIT = 7.0
RMS_EPS = 1e-6

GLA_QK = GLA_HEADS * GLA_DK
GLA_V = GLA_HEADS * GLA_DV
SWA_Q = SWA_HEADS * SWA_HEAD_DIM
SWA_KV = SWA_KV_HEADS * SWA_HEAD_DIM
LANES = 128
C_GQ, C_GK, C_GV, C_GR = 0, GLA_QK, 2 * GLA_QK, 2 * GLA_QK + GLA_V
C_SQ = C_GR + GLA_V
C_SK = C_SQ + SWA_Q
C_SV = C_SK + SWA_KV
C_GA = C_SV + SWA_KV
D_PROJ = C_GA + LANES

MIX_TM = 512
MOE_TM = 512
DEC_SB = 16
VMEM_LIMIT = 56 * 1024 * 1024

F32 = jnp.float32
BF16 = jnp.bfloat16
NEG_INF = float("-inf")


def _dot(a, b):
    return jnp.dot(a, b, preferred_element_type=F32)


def _dot_nt(a, b):
    return lax.dot_general(a, b, (((1,), (1,)), ((), ())), preferred_element_type=F32)


def _split3(x):
    hi = x.astype(BF16)
    r1 = x - hi.astype(F32)
    mid = r1.astype(BF16)
    lo = (r1 - mid.astype(F32)).astype(BF16)
    return hi, mid, lo


def _rms(x, g):
    return x * lax.rsqrt(jnp.mean(x * x, axis=-1, keepdims=True) + RMS_EPS) * g


def _iota(shape, dim):
    return lax.broadcasted_iota(jnp.int32, shape, dim)


SLAB = D_MODEL // LANES


def _store_slabs(ref, x):
    rows = x.shape[0]
    for c in range(SLAB):
        ref[pl.ds(c, rows, stride=SLAB), :] = x[:, c * LANES:(c + 1) * LANES]


def _load_slab_chunk(ref, rows, first, stride):
    return ref[pl.ds(first, rows, stride=stride), :]


def _project(x, g_mix, w_in, w_a_up, b_a):
    h = _rms(x, g_mix).astype(BF16)
    proj = _dot(h, w_in)
    ga = proj[:, C_GA:C_GA + LANES].astype(BF16)
    z = _dot(ga, w_a_up) + b_a
    log_a = -(jnp.maximum(-z, 0.0) + jnp.log1p(jnp.exp(-jnp.abs(z)))) / GLA_GATE_TAU
    return dict(
        gq=proj[:, C_GQ:C_GQ + GLA_QK] * (GLA_DK ** -0.5),
        gk=proj[:, C_GK:C_GK + GLA_QK],
        gv=proj[:, C_GV:C_GV + GLA_V],
        gr=proj[:, C_GR:C_GR + GLA_V],
        sq=proj[:, C_SQ:C_SQ + SWA_Q],
        sk=proj[:, C_SK:C_SK + SWA_KV],
        sv=proj[:, C_SV:C_SV + SWA_KV],
        log_a=log_a,
    )


def _tail(x, o_gla, gr, o_swa, g_gla_out, g_swa_out, w_out, g_ffn, w_r, b_r, base):
    tm = x.shape[0]
    gate = gr * jax.nn.sigmoid(gr)
    parts = []
    for h in range(GLA_HEADS):
        sl = slice(h * GLA_DV, (h + 1) * GLA_DV)
        parts.append(_rms(o_gla[:, sl], g_gla_out) * gate[:, sl])
    parts.append(_rms(o_swa, g_swa_out))
    o = jnp.concatenate(parts, axis=1).astype(BF16)
    x_mid = x + _dot(o, w_out)
    hp = _rms(x_mid, g_ffn)

    h1 = hp.astype(BF16)
    h2 = (hp - h1.astype(F32)).astype(BF16)
    la = _dot_nt(w_r, h1)
    lb = _dot_nt(w_r[0:N_EXPERTS], h2)
    logits = la[0:N_EXPERTS] + la[N_EXPERTS:2 * N_EXPERTS] + lb + b_r

    eidx = _iota((N_EXPERTS, tm), 0)
    vals, idxs, onehots = [], [], []
    l = logits
    for _ in range(TOP_K):
        m = jnp.max(l, axis=0, keepdims=True)
        sel = jnp.min(jnp.where(l == m, eidx, N_EXPERTS), axis=0, keepdims=True)
        oh = eidx == sel
        l = jnp.where(oh, NEG_INF, l)
        vals.append(m)
        idxs.append(sel)
        onehots.append(oh)
    es = [jnp.exp(v - vals[0]) for v in vals]
    denom = es[0] + es[1] + es[2] + es[3]
    gates = [e / denom for e in es]

    ohf = jnp.concatenate([oh.astype(F32) for oh in onehots], axis=0)
    upper = (_iota((tm, tm), 0) < _iota((tm, tm), 1)).astype(BF16)
    prefix = _dot(ohf.astype(BF16), upper)
    ranks = []
    for k in range(TOP_K):
        sl = slice(k * N_EXPERTS, (k + 1) * N_EXPERTS)
        ohk = ohf[sl]
        base_t = jnp.concatenate([base] * (tm // LANES), axis=1)
        ranks.append(jnp.sum(ohk * (prefix[sl] + base_t), axis=0, keepdims=True))
        base = base + jnp.sum(ohk, axis=1, keepdims=True)
    zi = jnp.zeros((8 - TOP_K, tm), jnp.int32)
    zf = jnp.zeros((8 - TOP_K, tm), F32)
    topi = jnp.concatenate(idxs + [zi], axis=0)
    gate8 = jnp.concatenate(gates + [zf], axis=0)
    rank8 = jnp.concatenate([r.astype(jnp.int32) for r in ranks] + [zi], axis=0)
    return x_mid, hp, topi, gate8, rank8, base


def _gla_chunks(p, row0, s_bd, n_lead_pad):
    tm = p["gq"].shape[0]
    nch = tm // GLA_CHUNK
    log_a = p["log_a"]
    if n_lead_pad:
        rows = row0 + _iota((tm, GLA_QK), 0)
        log_a = jnp.where(rows >= n_lead_pad, log_a, 0.0)
    ri, ci = _iota((tm, tm), 0), _iota((tm, tm), 1)
    tril = ((ri >= ci) & (ri // GLA_CHUNK == ci // GLA_CHUNK)).astype(BF16)
    hi, mid, lo = _split3(log_a)
    b_all = _dot(tril, hi) + _dot(tril, mid) + _dot(tril, lo)

    c64 = GLA_CHUNK
    kk_mask = (_iota((GLA_QK, GLA_QK), 0) // c64) == (_iota((GLA_QK, GLA_QK), 1) // GLA_DK)
    vv_mask = (_iota((GLA_QK, GLA_V), 0) // c64) == (_iota((GLA_QK, GLA_V), 1) // GLA_DV)
    ss_mask = (_iota((GLA_QK, GLA_V), 0) // GLA_DK) == (_iota((GLA_QK, GLA_V), 1) // GLA_DV)
    causal = (_iota((c64, GLA_QK), 0) >= (_iota((c64, GLA_QK), 1) % c64)).astype(F32)
    zpad_k = jnp.zeros((LANES - c64, GLA_QK), F32)
    zpad_v = jnp.zeros((LANES - c64, GLA_V), BF16)

    outs = []
    for c in range(nch):
        rs = slice(c * c64, (c + 1) * c64)
        b = b_all[rs]
        q, k, v = p["gq"][rs], p["gk"][rs], p["gv"][rs]
        b_last = b[c64 - 1:c64]
        qt = (q * jnp.exp(b)).astype(BF16)
        kt = k * jnp.exp(-b)
        kd = k * jnp.exp(b_last - b)
        vb = v.astype(BF16)
        k_bd = jnp.where(kk_mask, jnp.concatenate([kt] * GLA_HEADS, axis=0), 0.0).astype(BF16)
        a = (_dot_nt(qt, k_bd) * causal).astype(BF16)
        v_bd = jnp.where(vv_mask, jnp.concatenate([vb] * GLA_HEADS, axis=0), jnp.zeros((), BF16))
        outs.append(_dot(qt, s_bd.astype(BF16)) + _dot(a, v_bd))
        kd_t = jnp.transpose(jnp.concatenate([kd, zpad_k], axis=0)).astype(BF16)
        upd = _dot(kd_t, jnp.concatenate([vb, zpad_v], axis=0))
        decay = jnp.exp(jnp.transpose(jnp.broadcast_to(b_last, (LANES, GLA_QK))))
        s_bd = s_bd * jnp.concatenate([decay] * GLA_HEADS, axis=1) + jnp.where(ss_mask, upd, 0.0)
    return jnp.concatenate(outs, axis=0), s_bd


def _swa_block(sq, kcat, vcat, bias_ref, sinks_ref, valid_t):
    half = _iota((1, LANES), 1) < SWA_HEAD_DIM
    top_rows = _iota((LANES, 1), 0) < SWA_HEAD_DIM
    k_roll = pltpu.roll(kcat, SWA_HEAD_DIM, 1)
    v_t = jnp.transpose(vcat)
    zeros_v = jnp.zeros((SWA_HEAD_DIM, 2 * WINDOW), F32)
    cols = []
    for kv in range(SWA_KV_HEADS):
        kk = jnp.where(half, kcat, k_roll) if kv == 0 else jnp.where(half, k_roll, kcat)
        q_parts = []
        for c in (2 * kv, 2 * kv + 1):
            qc = sq[:, c * LANES:(c + 1) * LANES]
            q_parts.append(jnp.where(half, qc, 0.0))
            q_parts.append(jnp.where(half, 0.0, qc))
        q_st = jnp.concatenate(q_parts, axis=0).astype(BF16)
        s = _dot_nt(kk.astype(BF16), q_st) * (SWA_HEAD_DIM ** -0.5) + bias_ref[kv]
        s = jnp.where(valid_t, s, NEG_INF)
        sink = jnp.concatenate(
            [jnp.full((1, WINDOW), sinks_ref[kv * SWA_GROUP + g], F32) for g in range(SWA_GROUP)], axis=1)
        m = jnp.maximum(jnp.max(s, axis=0, keepdims=True), sink)
        pr = jnp.exp(s - m)
        inv = 1.0 / (jnp.sum(pr, axis=0, keepdims=True) + jnp.exp(sink - m))
        pb = pr.astype(BF16)
        vk = v_t[kv * SWA_HEAD_DIM:(kv + 1) * SWA_HEAD_DIM]
        vv_t = jnp.concatenate([jnp.concatenate([vk, zeros_v], axis=1),
                                jnp.concatenate([zeros_v, vk], axis=1)], axis=0).astype(BF16)
        for pair in range(SWA_GROUP // 2):
            ce = slice(2 * pair * WINDOW, (2 * pair + 1) * WINDOW)
            co = slice((2 * pair + 1) * WINDOW, (2 * pair + 2) * WINDOW)
            p2_t = jnp.concatenate([pb[:, ce], pb[:, co]], axis=0)
            o2_t = _dot(vv_t, p2_t)
            o2_t = o2_t * jnp.where(top_rows, inv[:, ce], inv[:, co])
            cols.append(jnp.transpose(o2_t))
    return jnp.concatenate(cols, axis=1)


def _mixer_kernel(sinks_ref, x_ref, s0_ref, k0_ref, v0_ref, base0_ref, bias_ref,
                  g_mix_ref, w_in_ref, w_a_up_ref, b_a_ref, g_gla_ref, g_swa_ref, w_out_ref,
                  g_ffn_ref, w_r_ref, b_r_ref,
                  xmid_ref, hp_ref, topi_ref, gate_ref, rank_ref, sout_ref, kout_ref, vout_ref, cnt_ref,
                  s_scr, k_scr, v_scr, base_scr, *, n_lead_pad, prev_valid_from):
    b_id, j = pl.program_id(0), pl.program_id(1)
    tm = x_ref.shape[1]

    @pl.when(j == 0)
    def _():
        s_scr[...] = s0_ref[...]
        k_scr[...] = k0_ref[...]
        v_scr[...] = v0_ref[...]

    @pl.when((j == 0) & (b_id == 0))
    def _():
        base_scr[...] = base0_ref[...]

    x = x_ref[0]
    p = _project(x, g_mix_ref[...], w_in_ref[...], w_a_up_ref[...], b_a_ref[...])

    o_gla, s_new = _gla_chunks(p, j * tm, s_scr[...], n_lead_pad)
    s_scr[...] = s_new

    kj = _iota((2 * WINDOW, WINDOW), 0)
    qi = _iota((2 * WINDOW, WINDOW), 1)
    band = (kj > qi) & (kj <= qi + WINDOW)
    o_parts = []
    for sb in range(tm // WINDOW):
        rs = slice(sb * WINDOW, (sb + 1) * WINDOW)
        k_blk, v_blk = p["sk"][rs], p["sv"][rs]
        k_prev = k_scr[...] if sb == 0 else p["sk"][(sb - 1) * WINDOW:sb * WINDOW]
        v_prev = v_scr[...] if sb == 0 else p["sv"][(sb - 1) * WINDOW:sb * WINDOW]
        valid = band
        if sb == 0 and prev_valid_from:
            first = jnp.where(j == 0, prev_valid_from, 0)
            valid = band & (kj >= first)
        valid = jnp.concatenate([valid] * SWA_GROUP, axis=1)
        o_parts.append(_swa_block(p["sq"][rs], jnp.concatenate([k_prev, k_blk], axis=0),
                                  jnp.concatenate([v_prev, v_blk], axis=0), bias_ref, sinks_ref, valid))
    o_swa = jnp.concatenate(o_parts, axis=0)
    k_scr[...] = p["sk"][tm - WINDOW:tm]
    v_scr[...] = p["sv"][tm - WINDOW:tm]

    x_mid, hp, topi, gate8, rank8, base = _tail(
        x, o_gla, p["gr"], o_swa, g_gla_ref[...], g_swa_ref[...], w_out_ref[...],
        g_ffn_ref[...], w_r_ref[...], b_r_ref[...], base_scr[...])
    base_scr[...] = base
    xmid_ref[0] = x_mid
    _store_slabs(hp_ref, hp)
    topi_ref[...] = topi
    gate_ref[...] = gate8
    rank_ref[...] = rank8
    sout_ref[0] = s_new
    kout_ref[0] = p["sk"][tm - WINDOW:tm]
    vout_ref[0] = p["sv"][tm - WINDOW:tm]
    cnt_ref[...] = base


def _full_spec(shape):
    nd = len(shape)
    return pl.BlockSpec(shape, lambda *_: (0,) * nd)


def _mixer_call(x, s0, k0, v0, base0, wts, tm, n_lead_pad, prev_valid_from, hp_rows):
    B, L, _ = x.shape
    nj = L // tm
    T = B * L
    weight_args = (wts["bias"], wts["g_mix"], wts["w_in"], wts["w_a_up"], wts["b_a"], wts["g_gla"],
                   wts["g_swa"], wts["w_out"], wts["g_ffn"], wts["w_r"], wts["b_r"])
    in_specs = [
        pl.BlockSpec(memory_space=pltpu.SMEM),
        pl.BlockSpec((1, tm, D_MODEL), lambda b, j: (b, j, 0)),
        _full_spec(s0.shape), _full_spec(k0.shape), _full_spec(v0.shape), _full_spec(base0.shape),
    ] + [_full_spec(w.shape) for w in weight_args]
    tok_spec = pl.BlockSpec((8, tm), lambda b, j: (0, b * nj + j))
    out_specs = [
        pl.BlockSpec((1, tm, D_MODEL), lambda b, j: (b, j, 0)),
        pl.BlockSpec((tm * SLAB, LANES), lambda b, j: (b * nj + j, 0)),
        tok_spec, tok_spec, tok_spec,
        pl.BlockSpec((1, GLA_QK, GLA_V), lambda b, j: (b, 0, 0)),
        pl.BlockSpec((1, WINDOW, SWA_KV), lambda b, j: (b, 0, 0)),
        pl.BlockSpec((1, WINDOW, SWA_KV), lambda b, j: (b, 0, 0)),
        _full_spec((N_EXPERTS, LANES)),
    ]
    out_shape = [
        jax.ShapeDtypeStruct((B, L, D_MODEL), F32),
        jax.ShapeDtypeStruct((hp_rows * SLAB, LANES), F32),
        jax.ShapeDtypeStruct((8, T), jnp.int32),
        jax.ShapeDtypeStruct((8, T), F32),
        jax.ShapeDtypeStruct((8, T), jnp.int32),
        jax.ShapeDtypeStruct((B, GLA_QK, GLA_V), F32),
        jax.ShapeDtypeStruct((B, WINDOW, SWA_KV), F32),
        jax.ShapeDtypeStruct((B, WINDOW, SWA_KV), F32),
        jax.ShapeDtypeStruct((N_EXPERTS, LANES), F32),
    ]
    kern = functools.partial(_mixer_kernel, n_lead_pad=n_lead_pad, prev_valid_from=prev_valid_from)
    return pl.pallas_call(
        kern,
        grid=(B, nj),
        in_specs=in_specs,
        out_specs=out_specs,
        out_shape=out_shape,
        scratch_shapes=[pltpu.VMEM((GLA_QK, GLA_V), F32), pltpu.VMEM((WINDOW, SWA_KV), F32),
                        pltpu.VMEM((WINDOW, SWA_KV), F32), pltpu.VMEM((N_EXPERTS, LANES), F32)],
        compiler_params=pltpu.CompilerParams(dimension_semantics=("arbitrary", "arbitrary"),
                                             vmem_limit_bytes=VMEM_LIMIT),
        name="mixer",
    )(wts["sinks"], x, s0, k0, v0, base0, *weight_args)


def _decode_kernel(sinks_ref, x_ref, st_ref, ck_ref, cv_ref, base0_ref, bias_ref,
                   g_mix_ref, w_in_ref, w_a_up_ref, b_a_ref, g_gla_ref, g_swa_ref, w_out_ref,
                   g_ffn_ref, w_r_ref, b_r_ref,
                   xmid_ref, hp_ref, topi_ref, gate_ref, rank_ref, sto_ref, cko_ref, cvo_ref, cnt_ref,
                   at_scr, kt_scr, qt_scr, gv_scr, gr_scr, sq_scr, sk_scr, sv_scr, og_scr, os_scr):
    i = pl.program_id(0)
    n_seq = x_ref.shape[0]

    @pl.when(i == 0)
    def _():
        p = _project(x_ref[...], g_mix_ref[...], w_in_ref[...], w_a_up_ref[...], b_a_ref[...])
        at_scr[...] = jnp.transpose(jnp.exp(p["log_a"]))
        kt_scr[...] = jnp.transpose(p["gk"])
        qt_scr[...] = jnp.transpose(p["gq"])
        gv_scr[...] = p["gv"]
        gr_scr[...] = p["gr"]
        sq_scr[...] = p["sq"]
        sk_scr[...] = p["sk"]
        sv_scr[...] = p["sv"]

    lane_seq = _iota((GLA_QK, n_seq), 1)
    half = _iota((1, LANES), 1) < SWA_HEAD_DIM
    row_id = _iota((WINDOW, SWA_KV), 0)
    head_diag = (_iota((16, SWA_Q), 1) // SWA_HEAD_DIM) == _iota((16, SWA_Q), 0)
    sink_col = jnp.concatenate(
        [jnp.full((1, 1), sinks_ref[h], F32) for h in range(SWA_HEADS)] + [jnp.zeros((8, 1), F32)], axis=0)

    def per_seq(sl, carry):
        s = i * DEC_SB + sl
        sel = lane_seq == s
        a_c = jnp.sum(jnp.where(sel, at_scr[...], 0.0), axis=1, keepdims=True)
        k_c = jnp.sum(jnp.where(sel, kt_scr[...], 0.0), axis=1, keepdims=True)
        q_c = jnp.sum(jnp.where(sel, qt_scr[...], 0.0), axis=1, keepdims=True)
        st = st_ref[sl].reshape(GLA_QK, GLA_DV)
        v_row = gv_scr[pl.ds(s, 1), :]
        v_b = jnp.concatenate(
            [jnp.broadcast_to(v_row[:, h * GLA_DV:(h + 1) * GLA_DV], (GLA_DK, GLA_DV))
             for h in range(GLA_HEADS)], axis=0)
        st_new = a_c * st + k_c * v_b
        sto_ref[sl] = st_new.reshape(GLA_HEADS, GLA_DK, GLA_DV)
        t = q_c * st_new
        og_scr[pl.ds(s, 1), :] = jnp.concatenate(
            [jnp.sum(t[h * GLA_DK:(h + 1) * GLA_DK], axis=0, keepdims=True) for h in range(GLA_HEADS)],
            axis=1)

        k_new = sk_scr[pl.ds(s, 1), :]
        v_new = sv_scr[pl.ds(s, 1), :]
        kn = jnp.where(row_id == WINDOW - 1, k_new, pltpu.roll(ck_ref[sl], WINDOW - 1, 0))
        vn = jnp.where(row_id == WINDOW - 1, v_new, pltpu.roll(cv_ref[sl], WINDOW - 1, 0))
        cko_ref[sl] = kn
        cvo_ref[sl] = vn
        kr, vr = pltpu.roll(kn, SWA_HEAD_DIM, 1), pltpu.roll(vn, SWA_HEAD_DIM, 1)
        k0, k1 = jnp.where(half, kn, kr), jnp.where(half, kr, kn)
        v0, v1 = jnp.where(half, vn, vr), jnp.where(half, vr, vn)
        kw = jnp.concatenate([k0, k0, k1, k1], axis=1).astype(BF16)
        vw = jnp.concatenate([v0, v0, v1, v1], axis=1).astype(BF16)
        q_row = sq_scr[pl.ds(s, 1), :]
        qm = jnp.where(head_diag, jnp.broadcast_to(q_row, (16, SWA_Q)), 0.0).astype(BF16)
        sc = _dot_nt(qm, kw) * (SWA_HEAD_DIM ** -0.5) + bias_ref[...]
        m = jnp.maximum(jnp.max(sc, axis=1, keepdims=True), sink_col)
        pr = jnp.exp(sc - m)
        inv = 1.0 / (jnp.sum(pr, axis=1, keepdims=True) + jnp.exp(sink_col - m))
        ow = _dot(pr.astype(BF16), vw) * inv
        os_scr[pl.ds(s, 1), :] = jnp.sum(jnp.where(head_diag, ow, 0.0), axis=0, keepdims=True)
        return carry

    lax.fori_loop(0, DEC_SB, per_seq, 0, unroll=2)

    @pl.when(i == pl.num_programs(0) - 1)
    def _():
        x_mid, hp, topi, gate8, rank8, base = _tail(
            x_ref[...], og_scr[...], gr_scr[...], os_scr[...], g_gla_ref[...], g_swa_ref[...],
            w_out_ref[...], g_ffn_ref[...], w_r_ref[...], b_r_ref[...], base0_ref[...])
        xmid_ref[...] = x_mid
        _store_slabs(hp_ref, hp)
        topi_ref[...] = topi
        gate_ref[...] = gate8
        rank_ref[...] = rank8
        cnt_ref[...] = base


def _decode_call(xs, state, ck, cv, base0, bias_dec, wts):
    n_seq = xs.shape[0]
    nb = n_seq // DEC_SB
    weight_args = (wts["g_mix"], wts["w_in"], wts["w_a_up"], wts["b_a"], wts["g_gla"],
                   wts["g_swa"], wts["w_out"], wts["g_ffn"], wts["w_r"], wts["b_r"])
    in_specs = [
        pl.BlockSpec(memory_space=pltpu.SMEM),
        _full_spec(xs.shape),
        pl.BlockSpec((DEC_SB, GLA_HEADS, GLA_DK, GLA_DV), lambda i: (i, 0, 0, 0)),
        pl.BlockSpec((DEC_SB, WINDOW, SWA_KV), lambda i: (i, 0, 0)),
        pl.BlockSpec((DEC_SB, WINDOW, SWA_KV), lambda i: (i, 0, 0)),
        _full_spec(base0.shape), _full_spec(bias_dec.shape),
    ] + [_full_spec(w.shape) for w in weight_args]
    out_specs = [
        _full_spec((n_seq, D_MODEL)),
        _full_spec((n_seq * SLAB, LANES)),
        _full_spec((8, n_seq)), _full_spec((8, n_seq)), _full_spec((8, n_seq)),
        pl.BlockSpec((DEC_SB, GLA_HEADS, GLA_DK, GLA_DV), lambda i: (i, 0, 0, 0)),
        pl.BlockSpec((DEC_SB, WINDOW, SWA_KV), lambda i: (i, 0, 0)),
        pl.BlockSpec((DEC_SB, WINDOW, SWA_KV), lambda i: (i, 0, 0)),
        _full_spec((N_EXPERTS, LANES)),
    ]
    out_shape = [
        jax.ShapeDtypeStruct((n_seq, D_MODEL), F32),
        jax.ShapeDtypeStruct((n_seq * SLAB, LANES), F32),
        jax.ShapeDtypeStruct((8, n_seq), jnp.int32),
        jax.ShapeDtypeStruct((8, n_seq), F32),
        jax.ShapeDtypeStruct((8, n_seq), jnp.int32),
        jax.ShapeDtypeStruct(state.shape, F32),
        jax.ShapeDtypeStruct(ck.shape, F32),
        jax.ShapeDtypeStruct(cv.shape, F32),
        jax.ShapeDtypeStruct((N_EXPERTS, LANES), F32),
    ]
    scratch = [pltpu.VMEM((GLA_QK, n_seq), F32)] * 3 + [
        pltpu.VMEM((n_seq, GLA_V), F32), pltpu.VMEM((n_seq, GLA_V), F32), pltpu.VMEM((n_seq, SWA_Q), F32),
        pltpu.VMEM((n_seq, SWA_KV), F32), pltpu.VMEM((n_seq, SWA_KV), F32),
        pltpu.VMEM((n_seq, GLA_V), F32), pltpu.VMEM((n_seq, SWA_Q), F32)]
    return pl.pallas_call(
        _decode_kernel,
        grid=(nb,),
        in_specs=in_specs,
        out_specs=out_specs,
        out_shape=out_shape,
        scratch_shapes=scratch,
        compiler_params=pltpu.CompilerParams(dimension_semantics=("arbitrary",),
                                             vmem_limit_bytes=VMEM_LIMIT),
        name="decode",
    )(wts["sinks"], xs, state, ck, cv, base0, bias_dec, *weight_args)


SC_CORES = 2
SC_SUBCORES = 16
SC_WORKERS = SC_CORES * SC_SUBCORES
SC_SCATTER_ROWS = 32
SC_GATHER_ROWS = 24


def _sc_mesh():
    return plsc.VectorSubcoreMesh(core_axis_name="c", subcore_axis_name="s")


def _sc_worker_id():
    return lax.axis_index("s") * SC_CORES + lax.axis_index("c")


def _sc_scatter_rows(src_p, src_s, idx_p, idx_s, n_out):
    rows = SC_SCATTER_ROWS
    n_chunks = idx_p.shape[0] // SC_WORKERS
    n_s, _, rows_s = idx_s.shape
    assert n_chunks * SC_WORKERS == idx_p.shape[0] and n_chunks % 2 == 0 and n_s <= SC_WORKERS

    @functools.partial(
        pl.kernel, mesh=_sc_mesh(),
        out_type=jax.ShapeDtypeStruct((n_out, SLAB, LANES), F32),
        scratch_types=[pltpu.VMEM((2, TOP_K, rows), jnp.int32), pltpu.VMEM((2, rows, SLAB, LANES), F32),
                       pltpu.VMEM((TOP_K, rows_s), jnp.int32), pltpu.VMEM((rows_s, SLAB, LANES), F32),
                       pltpu.SemaphoreType.DMA((2,)), pltpu.SemaphoreType.DMA((2,))])
    def scatter_rows(srcp_hbm, srcs_hbm, idxp_hbm, idxs_hbm, out_hbm, idx_v, rows_v, idxs_v, rowss_v, lsem, ssem):
        wid = _sc_worker_id()

        def loads(c, b):
            g = wid * n_chunks + c
            return (pltpu.make_async_copy(idxp_hbm.at[g], idx_v.at[b], lsem.at[b]),
                    pltpu.make_async_copy(srcp_hbm.at[pl.ds(pl.multiple_of(g * rows, 8), rows)], rows_v.at[b],
                                          lsem.at[b]))

        def scatters(b):
            return [pltpu.make_async_copy(rows_v.at[b], out_hbm.at[idx_v.at[b, k]], ssem.at[b])
                    for k in range(TOP_K)]

        for d in loads(0, 0):
            d.start()

        @pl.loop(0, n_chunks, step=2)
        def _(c0):
            for b in range(2):
                c = c0 + b
                for d in loads(c, b):
                    d.wait()

                @pl.when(c >= 1)
                def _():
                    for d in scatters(1 - b):
                        d.wait()

                @pl.when(c + 1 < n_chunks)
                def _():
                    for d in loads(c + 1, 1 - b):
                        d.start()

                for d in scatters(b):
                    d.start()

        for d in scatters((n_chunks - 1) % 2):
            d.wait()

        @pl.when(wid < n_s)
        def _():
            pltpu.sync_copy(idxs_hbm.at[wid], idxs_v)
            pltpu.sync_copy(srcs_hbm.at[pl.ds(pl.multiple_of(wid * rows_s, 8), rows_s)], rowss_v)
            for k in range(TOP_K):
                pltpu.sync_copy(rowss_v, out_hbm.at[idxs_v.at[k]])

    return scatter_rows(src_p, src_s, idx_p, idx_s)


def _sc_gather_rows(src3, idx2):
    rows = SC_GATHER_ROWS
    n_chunks = idx2.shape[0] // SC_WORKERS
    assert n_chunks * SC_WORKERS == idx2.shape[0] and idx2.shape[1] == rows and n_chunks % 2 == 0

    @functools.partial(
        pl.kernel, mesh=_sc_mesh(),
        out_type=jax.ShapeDtypeStruct((idx2.shape[0] * rows, SLAB, LANES), F32),
        scratch_types=[pltpu.VMEM((2, rows), jnp.int32), pltpu.VMEM((2, rows, SLAB, LANES), F32),
                       pltpu.SemaphoreType.DMA((2,)), pltpu.SemaphoreType.DMA((2,))])
    def gather_rows(src_hbm, idx_hbm, out_hbm, idx_v, rows_v, gsem, wsem):
        wid = _sc_worker_id()

        def gather(b):
            return pltpu.make_async_copy(src_hbm.at[idx_v.at[b]], rows_v.at[b], gsem.at[b])

        def write(c, b):
            base = pl.multiple_of((wid * n_chunks + c) * rows, 8)
            return pltpu.make_async_copy(rows_v.at[b], out_hbm.at[pl.ds(base, rows)], wsem.at[b])

        pltpu.sync_copy(idx_hbm.at[wid * n_chunks], idx_v.at[0])
        gather(0).start()

        @pl.loop(0, n_chunks, step=2)
        def _(c0):
            for b in range(2):
                c = c0 + b

                @pl.when(c + 1 < n_chunks)
                def _():
                    @pl.when(c >= 1)
                    def _():
                        write(c - 1, 1 - b).wait()
                    pltpu.sync_copy(idx_hbm.at[wid * n_chunks + c + 1], idx_v.at[1 - b])
                    gather(1 - b).start()

                gather(b).wait()
                write(c, b).start()

        write(n_chunks - 2, 0).wait()
        write(n_chunks - 1, 1).wait()

    return gather_rows(src3, idx2)


FF_TILE = 256


def _ffn_kernel(blk_e_ref, nused_ref, x_ref, wu_ref, bu_ref, wd_ref, bd_ref, y_ref, xbf, actbf, wu_bf, wd_bf):
    i = pl.program_id(0)
    tm = MOE_TM
    n_tiles = D_FF // FF_TILE

    @pl.when(i < nused_ref[0])
    def _():
        @pl.when((i == 0) | (blk_e_ref[i] != blk_e_ref[jnp.maximum(i - 1, 0)]))
        def _():
            wu_bf[...] = wu_ref[0].astype(BF16)
            wd_bf[...] = wd_ref[0].astype(BF16)

        for c in range(SLAB):
            xbf[:, c * LANES:(c + 1) * LANES] = _load_slab_chunk(x_ref, tm, c, SLAB).astype(BF16)
        for n in range(n_tiles):
            gc = slice(n * FF_TILE, (n + 1) * FF_TILE)
            lc = slice(D_FF + n * FF_TILE, D_FF + (n + 1) * FF_TILE)
            g = jnp.minimum(_dot(xbf[...], wu_bf[:, gc]) + bu_ref[0, :, gc], SWIGLU_LIMIT)
            lin = jnp.clip(_dot(xbf[...], wu_bf[:, lc]) + bu_ref[0, :, lc], -SWIGLU_LIMIT, SWIGLU_LIMIT)
            actbf[:, gc] = (g * jax.nn.sigmoid(SWIGLU_ALPHA * g) * (lin + 1.0)).astype(BF16)
        for n in range(n_tiles):
            yc = slice(n * FF_TILE, (n + 1) * FF_TILE)
            y = _dot(actbf[...], wd_bf[:, yc]) + bd_ref[0, :, yc]
            for c in range(FF_TILE // LANES):
                y_ref[pl.ds(n * (FF_TILE // LANES) + c, tm, stride=SLAB), :] = y[:, c * LANES:(c + 1) * LANES]

    @pl.when(i >= nused_ref[0])
    def _():
        y_ref[...] = jnp.zeros_like(y_ref)


def _ffn_call(blk_e, nused, xs2, w_up, b_up, w_down, b_down):
    n_blocks = blk_e.shape[0]
    tm = MOE_TM
    row_blk = pl.BlockSpec((tm * SLAB, LANES), lambda i, be, nu: (i, 0))
    grid_spec = pltpu.PrefetchScalarGridSpec(
        num_scalar_prefetch=2,
        grid=(n_blocks,),
        in_specs=[
            row_blk,
            pl.BlockSpec((1, D_MODEL, 2 * D_FF), lambda i, be, nu: (be[i], 0, 0)),
            pl.BlockSpec((1, 1, 2 * D_FF), lambda i, be, nu: (be[i], 0, 0)),
            pl.BlockSpec((1, D_FF, D_MODEL), lambda i, be, nu: (be[i], 0, 0)),
            pl.BlockSpec((1, 1, D_MODEL), lambda i, be, nu: (be[i], 0, 0)),
        ],
        out_specs=row_blk,
        scratch_shapes=[pltpu.VMEM((tm, D_MODEL), BF16), pltpu.VMEM((tm, D_FF), BF16),
                        pltpu.VMEM((D_MODEL, 2 * D_FF), BF16), pltpu.VMEM((D_FF, D_MODEL), BF16)],
    )
    return pl.pallas_call(
        _ffn_kernel,
        grid_spec=grid_spec,
        out_shape=jax.ShapeDtypeStruct((n_blocks * tm * SLAB, LANES), F32),
        compiler_params=pltpu.CompilerParams(dimension_semantics=("arbitrary",),
                                             vmem_limit_bytes=VMEM_LIMIT),
        name="experts",
    )(blk_e, nused, xs2, w_up, b_up.reshape(N_EXPERTS, 1, 2 * D_FF), w_down, b_down.reshape(N_EXPERTS, 1, D_MODEL))


def _combine_kernel(ys0_ref, ys1_ref, ys2_ref, ys3_ref, xmid_ref, gate_ref, g_final_ref, y_ref):
    tm = xmid_ref.shape[0]
    gts = gate_ref[...]
    chunks = []
    for c in range(SLAB):
        acc = xmid_ref[:, c * LANES:(c + 1) * LANES]
        for k, ys_ref in enumerate((ys0_ref, ys1_ref, ys2_ref, ys3_ref)):
            acc = acc + _load_slab_chunk(ys_ref, tm, c, SLAB) * gts[:, k:k + 1]
        chunks.append(acc)
    y_ref[...] = _rms(jnp.concatenate(chunks, axis=1), g_final_ref[...])


def _combine_call(ys4, t_stride, row0, x_mid, gates, g_final, tm):
    T = x_mid.shape[0]
    blk0 = row0 // tm
    per_k = t_stride // tm
    assert per_k * tm == t_stride and blk0 * tm == row0

    def ys_spec(k):
        return pl.BlockSpec((tm * SLAB, LANES), lambda i: (k * per_k + blk0 + i, 0))

    return pl.pallas_call(
        _combine_kernel,
        grid=(T // tm,),
        in_specs=[
            ys_spec(0), ys_spec(1), ys_spec(2), ys_spec(3),
            pl.BlockSpec((tm, D_MODEL), lambda i: (i, 0)),
            pl.BlockSpec((tm, TOP_K), lambda i: (i, 0)),
            _full_spec((1, D_MODEL)),
        ],
        out_specs=pl.BlockSpec((tm, D_MODEL), lambda i: (i, 0)),
        out_shape=jax.ShapeDtypeStruct((T, D_MODEL), F32),
        compiler_params=pltpu.CompilerParams(dimension_semantics=("arbitrary",),
                                             vmem_limit_bytes=VMEM_LIMIT),
        name="combine",
    )(ys4, ys4, ys4, ys4, x_mid, gates, g_final)


def _t5_bucket(dist):
    n = jnp.maximum(dist, 0)
    max_exact = NUM_BUCKETS // 2
    nf = jnp.maximum(n, 1).astype(F32)
    large = max_exact + (jnp.log(nf / max_exact) / math.log(MAX_DISTANCE / max_exact)
                         * (NUM_BUCKETS - max_exact)).astype(jnp.int32)
    large = jnp.minimum(large, NUM_BUCKETS - 1)
    return jnp.where(n < max_exact, n, large)


def kernel(x_prompt, x_sample, state_gla, cache_swa_k, cache_swa_v, meta_tokens, rel_bias_table,
           g_mix, w_in, w_a_up, b_a, g_gla_out, g_swa_out, attn_sinks, w_out,
           g_ffn, w_router, b_router, w_up, b_up, w_down, b_down, g_final):
    assert g_mix.shape[0] == 1, "single-layer trunk"
    B, L, _ = x_prompt.shape
    n_seq = x_sample.shape[0]
    TP = B * L
    T_all = TP + n_seq

    wi = w_in[0]
    sizes = (GLA_QK, GLA_QK, GLA_V, GLA_V, GLA_LOWRANK, SWA_Q, SWA_KV, SWA_KV)
    offs = [0]
    for s in sizes:
        offs.append(offs[-1] + s)
    seg = [wi[:, offs[n]:offs[n + 1]] for n in range(8)]
    w_in_r = jnp.concatenate(
        seg[0:4] + seg[5:8] + [seg[4], jnp.zeros((D_MODEL, LANES - GLA_LOWRANK), F32)], axis=1).astype(BF16)
    w_a_pad = jnp.concatenate([w_a_up[0], jnp.zeros((LANES - GLA_LOWRANK, GLA_QK), F32)], axis=0).astype(BF16)
    wr_t = jnp.transpose(w_router[0])
    wr_hi = wr_t.astype(BF16)
    wr_lo = (wr_t - wr_hi.astype(F32)).astype(BF16)
    qi = jnp.arange(WINDOW)[:, None]
    kj = jnp.arange(2 * WINDOW)[None, :]
    buckets = jnp.arange(NUM_BUCKETS)
    table = rel_bias_table.astype(F32)
    oh_p = (_t5_bucket(qi - kj + WINDOW)[..., None] == buckets).astype(F32)
    bias_p = jnp.einsum("qkb,bh->hkq", oh_p, table, precision=lax.Precision.HIGHEST)
    bias_p = bias_p.reshape(SWA_KV_HEADS, SWA_GROUP, 2 * WINDOW, WINDOW).transpose(0, 2, 1, 3)
    bias_p = bias_p.reshape(SWA_KV_HEADS, 2 * WINDOW, SWA_GROUP * WINDOW)
    oh_d = (_t5_bucket(WINDOW - 1 - jnp.arange(WINDOW))[:, None] == buckets).astype(F32)
    bias_d = jnp.einsum("rb,bh->hr", oh_d, table, precision=lax.Precision.HIGHEST)
    bias_d = jnp.concatenate([bias_d, jnp.zeros((8, WINDOW), F32)], axis=0)
    wts = dict(
        sinks=attn_sinks[0].astype(F32), bias=bias_p,
        g_mix=g_mix[0][None], w_in=w_in_r, w_a_up=w_a_pad, b_a=b_a[0][None],
        g_gla=g_gla_out[0][None], g_swa=g_swa_out[0][None], w_out=w_out[0].astype(BF16),
        g_ffn=g_ffn[0][None], w_r=jnp.concatenate([wr_hi, wr_lo], axis=0), b_r=b_router[0][:, None],
    )

    x_pre = jnp.concatenate([jnp.zeros((WINDOW - N_META, D_MODEL), F32), meta_tokens.astype(F32)], axis=0)[None]
    zeros_s = jnp.zeros((GLA_QK, GLA_V), F32)
    zeros_kv = jnp.zeros((WINDOW, SWA_KV), F32)
    zeros_b = jnp.zeros((N_EXPERTS, LANES), F32)
    pre = _mixer_call(x_pre, zeros_s, zeros_kv, zeros_kv, zeros_b, wts, WINDOW, WINDOW - N_META, 0, WINDOW)
    s_meta, k_meta, v_meta = pre[5][0], pre[6][0], pre[7][0]

    (xmid_p, hp_p, topi_p, gate_p, rank_p, s_p, k_p, v_p, cnt_p) = _mixer_call(
        x_prompt, s_meta, k_meta, v_meta, zeros_b, wts, MIX_TM, 0, WINDOW - N_META, TP)

    (xmid_s, hp_s, topi_s, gate_s, rank_s, st_s, ck_s, cv_s, cnt_all) = _decode_call(
        x_sample[:, 0], state_gla[0], cache_swa_k[0].reshape(n_seq, WINDOW, SWA_KV),
        cache_swa_v[0].reshape(n_seq, WINDOW, SWA_KV), cnt_p, bias_d, wts)

    tm = MOE_TM
    n_slots = T_all * TOP_K
    n_blocks = -(-n_slots // tm) + N_EXPERTS
    top_e = jnp.concatenate([topi_p[:TOP_K], topi_s[:TOP_K]], axis=1)
    rank = jnp.concatenate([rank_p[:TOP_K], rank_s[:TOP_K]], axis=1)
    counts = cnt_all[:, 0].astype(jnp.int32)
    padded = (counts + tm - 1) // tm * tm
    pad_end = jnp.cumsum(padded)
    pad_start = pad_end - padded
    e_ids = jnp.arange(N_EXPERTS, dtype=jnp.int32)
    dest = jnp.sum(jnp.where(top_e[..., None] == e_ids, pad_start, 0), axis=-1) + rank
    n_pad = n_blocks * tm
    blk_e = jnp.minimum(jnp.sum(pad_end[None] <= (jnp.arange(n_blocks, dtype=jnp.int32) * tm)[:, None], axis=1),
                        N_EXPERTS - 1).astype(jnp.int32)
    nused = (pad_end[-1] // tm).astype(jnp.int32).reshape(1)

    sample_rows = 8
    idx_p = dest[:, :TP].reshape(TOP_K, TP // SC_SCATTER_ROWS, SC_SCATTER_ROWS).transpose(1, 0, 2)
    idx_s = dest[:, TP:].reshape(TOP_K, n_seq // sample_rows, sample_rows).transpose(1, 0, 2)
    xs3 = _sc_scatter_rows(hp_p.reshape(TP, SLAB, LANES), hp_s.reshape(n_seq, SLAB, LANES), idx_p, idx_s, n_pad)
    ys2 = _ffn_call(blk_e, nused, xs3.reshape(-1, LANES), w_up[0], b_up[0], w_down[0], b_down[0])
    unit = math.lcm(2 * SC_WORKERS * SC_GATHER_ROWS // TOP_K, MIX_TM)
    t_stride = -(-T_all // unit) * unit
    filler = jnp.arange(TOP_K * (t_stride - T_all), dtype=jnp.int32).reshape(TOP_K, t_stride - T_all)
    slot_src = jnp.concatenate([dest, filler], axis=1)
    slot_src = slot_src.reshape(TOP_K * t_stride // SC_GATHER_ROWS, SC_GATHER_ROWS)
    ys4 = _sc_gather_rows(ys2.reshape(-1, SLAB, LANES), slot_src).reshape(-1, LANES)

    gates = jnp.transpose(jnp.concatenate([gate_p[:TOP_K], gate_s[:TOP_K]], axis=1))
    gf = g_final[None]
    y_p = _combine_call(ys4, t_stride, 0, xmid_p.reshape(TP, D_MODEL), gates[:TP], gf, MIX_TM)
    y_s = _combine_call(ys4, t_stride, TP, xmid_s, gates[TP:], gf, n_seq)

    s_heads = jnp.stack([s_p[:, h * GLA_DK:(h + 1) * GLA_DK, h * GLA_DV:(h + 1) * GLA_DV]
                         for h in range(GLA_HEADS)], axis=1)
    return (y_p.reshape(B, L, D_MODEL), y_s.reshape(n_seq, 1, D_MODEL), s_heads[None],
            k_p.reshape(1, B, WINDOW, SWA_KV_HEADS, SWA_HEAD_DIM),
            v_p.reshape(1, B, WINDOW, SWA_KV_HEADS, SWA_HEAD_DIM),
            st_s[None], ck_s.reshape(1, n_seq, WINDOW, SWA_KV_HEADS, SWA_HEAD_DIM),
            cv_s.reshape(1, n_seq, WINDOW, SWA_KV_HEADS, SWA_HEAD_DIM))
```

```python
import functools
import math

import jax
import jax.numpy as jnp
from jax import lax
from jax.experimental import pallas as pl
from jax.experimental.pallas import tpu as pltpu
from jax.experimental.pallas import tpu_sc as plsc

D_MODEL = 1024
N_META = 16
GLA_HEADS = 4
GLA_DK = 64
GLA_DV = 128
GLA_LOWRANK = 16
GLA_GATE_TAU = 16.0
GLA_CHUNK = 64
SWA_HEADS = 8
SWA_KV_HEADS = 2
SWA_HEAD_DIM = 64
SWA_GROUP = SWA_HEADS // SWA_KV_HEADS
WINDOW = 128
NUM_BUCKETS = 32
MAX_DISTANCE = 128
N_EXPERTS = 32
TOP_K = 4
D_FF = 1024
SWIGLU_ALPHA = 1.702
SWIGLU_LIMIT = 7.0
RMS_EPS = 1e-6

GLA_QK = GLA_HEADS * GLA_DK
GLA_V = GLA_HEADS * GLA_DV
SWA_Q = SWA_HEADS * SWA_HEAD_DIM
SWA_KV = SWA_KV_HEADS * SWA_HEAD_DIM
LANES = 128
C_GQ, C_GK, C_GV, C_GR = 0, GLA_QK, 2 * GLA_QK, 2 * GLA_QK + GLA_V
C_SQ = C_GR + GLA_V
C_SK = C_SQ + SWA_Q
C_SV = C_SK + SWA_KV
C_GA = C_SV + SWA_KV
D_PROJ = C_GA + LANES

MIX_TM = 512
MOE_TM = 512
DEC_SB = 16
VMEM_LIMIT = 56 * 1024 * 1024

F32 = jnp.float32
BF16 = jnp.bfloat16
NEG_INF = float("-inf")


def _dot(a, b):
    return jnp.dot(a, b, preferred_element_type=F32)


def _dot_nt(a, b):
    return lax.dot_general(a, b, (((1,), (1,)), ((), ())), preferred_element_type=F32)


def _split3(x):
    hi = x.astype(BF16)
    r1 = x - hi.astype(F32)
    mid = r1.astype(BF16)
    lo = (r1 - mid.astype(F32)).astype(BF16)
    return hi, mid, lo


def _rms(x, g):
    return x * lax.rsqrt(jnp.mean(x * x, axis=-1, keepdims=True) + RMS_EPS) * g


def _iota(shape, dim):
    return lax.broadcasted_iota(jnp.int32, shape, dim)


SLAB = D_MODEL // LANES


def _store_slabs(ref, x):
    rows = x.shape[0]
    for c in range(SLAB):
        ref[pl.ds(c, rows, stride=SLAB), :] = x[:, c * LANES:(c + 1) * LANES]


def _load_slab_chunk(ref, rows, first, stride):
    return ref[pl.ds(first, rows, stride=stride), :]


def _project(x, g_mix, w_in_ref, w_a_up, b_a):
    h = _rms(x, g_mix).astype(BF16)

    def cols(lo, width):
        return _dot(h, w_in_ref[:, lo:lo + width])

    ga = cols(C_GA, LANES).astype(BF16)
    z = _dot(ga, w_a_up) + b_a
    log_a = -(jnp.maximum(-z, 0.0) + jnp.log1p(jnp.exp(-jnp.abs(z)))) / GLA_GATE_TAU
    gqk = cols(C_GQ, 2 * GLA_QK)
    gv = cols(C_GV, GLA_V)
    swa = cols(C_SQ, SWA_Q + 2 * SWA_KV)
    gr = cols(C_GR, GLA_V)
    return dict(
        gq=gqk[:, :GLA_QK] * (GLA_DK ** -0.5),
        gk=gqk[:, GLA_QK:],
        gv=gv,
        gr=gr,
        sq=swa[:, :SWA_Q],
        sk=swa[:, SWA_Q:SWA_Q + SWA_KV],
        sv=swa[:, SWA_Q + SWA_KV:],
        log_a=log_a,
    )


def _tail(x, o_gla, gr, o_swa, g_gla_out, g_swa_out, w_out_ref, g_ffn, w_r, b_r, base):
    tm = x.shape[0]
    gate = gr * jax.nn.sigmoid(gr)
    parts = []
    for h in range(GLA_HEADS):
        sl = slice(h * GLA_DV, (h + 1) * GLA_DV)
        parts.append(_rms(o_gla[:, sl], g_gla_out) * gate[:, sl])
    og = jnp.concatenate(parts, axis=1).astype(BF16)
    os_ = _rms(o_swa, g_swa_out).astype(BF16)
    x_mid = x + _dot(og, w_out_ref[0:GLA_V]) + _dot(os_, w_out_ref[GLA_V:GLA_V + SWA_Q])
    hp = _rms(x_mid, g_ffn)

    h1 = hp.astype(BF16)
    h2 = (hp - h1.astype(F32)).astype(BF16)
    la = _dot_nt(w_r, h1)
    lb = _dot_nt(w_r[0:N_EXPERTS], h2)
    logits = la[0:N_EXPERTS] + la[N_EXPERTS:2 * N_EXPERTS] + lb + b_r

    eidx = _iota((N_EXPERTS, tm), 0)
    vals, idxs, onehots = [], [], []
    l = logits
    for _ in range(TOP_K):
        m = jnp.max(l, axis=0, keepdims=True)
        sel = jnp.min(jnp.where(l == m, eidx, N_EXPERTS), axis=0, keepdims=True)
        oh = eidx == sel
        l = jnp.where(oh, NEG_INF, l)
        vals.append(m)
        idxs.append(sel)
        onehots.append(oh)
    es = [jnp.exp(v - vals[0]) for v in vals]
    denom = es[0] + es[1] + es[2] + es[3]
    gates = [e / denom for e in es]

    ohf = jnp.concatenate([oh.astype(F32) for oh in onehots], axis=0)
    upper = (_iota((tm, tm), 0) < _iota((tm, tm), 1)).astype(BF16)
    prefix = _dot(ohf.astype(BF16), upper)
    ranks = []
    for k in range(TOP_K):
        sl = slice(k * N_EXPERTS, (k + 1) * N_EXPERTS)
        ohk = ohf[sl]
        base_t = jnp.concatenate([base] * (tm // LANES), axis=1)
        ranks.append(jnp.sum(ohk * (prefix[sl] + base_t), axis=0, keepdims=True))
        base = base + jnp.sum(ohk, axis=1, keepdims=True)
    zi = jnp.zeros((8 - TOP_K, tm), jnp.int32)
    zf = jnp.zeros((8 - TOP_K, tm), F32)
    topi = jnp.concatenate(idxs + [zi], axis=0)
    gate8 = jnp.concatenate(gates + [zf], axis=0)
    rank8 = jnp.concatenate([r.astype(jnp.int32) for r in ranks] + [zi], axis=0)
    return x_mid, hp, topi, gate8, rank8, base


def _gla_chunks(p, row0, s_bd, n_lead_pad):
    tm = p["gq"].shape[0]
    nch = tm // GLA_CHUNK
    log_a = p["log_a"]
    if n_lead_pad:
        rows = row0 + _iota((tm, GLA_QK), 0)
        log_a = jnp.where(rows >= n_lead_pad, log_a, 0.0)
    ri, ci = _iota((tm, tm), 0), _iota((tm, tm), 1)
    tril = ((ri >= ci) & (ri // GLA_CHUNK == ci // GLA_CHUNK)).astype(BF16)
    hi, mid, lo = _split3(log_a)
    b_all = _dot(tril, hi) + _dot(tril, mid) + _dot(tril, lo)

    c64 = GLA_CHUNK
    kk_mask = (_iota((GLA_QK, GLA_QK), 0) // c64) == (_iota((GLA_QK, GLA_QK), 1) // GLA_DK)
    vv_mask = (_iota((GLA_QK, GLA_V), 0) // c64) == (_iota((GLA_QK, GLA_V), 1) // GLA_DV)
    ss_mask = (_iota((GLA_QK, GLA_V), 0) // GLA_DK) == (_iota((GLA_QK, GLA_V), 1) // GLA_DV)
    causal = (_iota((c64, GLA_QK), 0) >= (_iota((c64, GLA_QK), 1) % c64)).astype(F32)
    zpad_k = jnp.zeros((LANES - c64, GLA_QK), F32)
    zpad_v = jnp.zeros((LANES - c64, GLA_V), BF16)

    outs = []
    for c in range(nch):
        rs = slice(c * c64, (c + 1) * c64)
        b = b_all[rs]
        q, k, v = p["gq"][rs], p["gk"][rs], p["gv"][rs]
        b_last = b[c64 - 1:c64]
        qt = (q * jnp.exp(b)).astype(BF16)
        kt = k * jnp.exp(-b)
        kd = k * jnp.exp(b_last - b)
        vb = v.astype(BF16)
        k_bd = jnp.where(kk_mask, jnp.concatenate([kt] * GLA_HEADS, axis=0), 0.0).astype(BF16)
        a = (_dot_nt(qt, k_bd) * causal).astype(BF16)
        v_bd = jnp.where(vv_mask, jnp.concatenate([vb] * GLA_HEADS, axis=0), jnp.zeros((), BF16))
        outs.append(_dot(qt, s_bd.astype(BF16)) + _dot(a, v_bd))
        kd_t = jnp.transpose(jnp.concatenate([kd, zpad_k], axis=0)).astype(BF16)
        upd = _dot(kd_t, jnp.concatenate([vb, zpad_v], axis=0))
        decay = jnp.exp(jnp.transpose(jnp.broadcast_to(b_last, (LANES, GLA_QK))))
        s_bd = s_bd * jnp.concatenate([decay] * GLA_HEADS, axis=1) + jnp.where(ss_mask, upd, 0.0)
    return jnp.concatenate(outs, axis=0), s_bd


def _swa_block(sq, kcat, vcat, bias_ref, sinks_ref, valid_t):
    half = _iota((1, LANES), 1) < SWA_HEAD_DIM
    top_rows = _iota((LANES, 1), 0) < SWA_HEAD_DIM
    k_roll = pltpu.roll(kcat, SWA_HEAD_DIM, 1)
    v_t = jnp.transpose(vcat)
    zeros_v = jnp.zeros((SWA_HEAD_DIM, 2 * WINDOW), F32)
    cols = []
    for kv in range(SWA_KV_HEADS):
        kk = jnp.where(half, kcat, k_roll) if kv == 0 else jnp.where(half, k_roll, kcat)
        q_parts = []
        for c in (2 * kv, 2 * kv + 1):
            qc = sq[:, c * LANES:(c + 1) * LANES]
            q_parts.append(jnp.where(half, qc, 0.0))
            q_parts.append(jnp.where(half, 0.0, qc))
        q_st = jnp.concatenate(q_parts, axis=0).astype(BF16)
        s = _dot_nt(kk.astype(BF16), q_st) * (SWA_HEAD_DIM ** -0.5) + bias_ref[kv]
        s = jnp.where(valid_t, s, NEG_INF)
        sink = jnp.concatenate(
            [jnp.full((1, WINDOW), sinks_ref[kv * SWA_GROUP + g], F32) for g in range(SWA_GROUP)], axis=1)
        m = jnp.maximum(jnp.max(s, axis=0, keepdims=True), sink)
        pr = jnp.exp(s - m)
        inv = 1.0 / (jnp.sum(pr, axis=0, keepdims=True) + jnp.exp(sink - m))
        pb = pr.astype(BF16)
        vk = v_t[kv * SWA_HEAD_DIM:(kv + 1) * SWA_HEAD_DIM]
        vv_t = jnp.concatenate([jnp.concatenate([vk, zeros_v], axis=1),
                                jnp.concatenate([zeros_v, vk], axis=1)], axis=0).astype(BF16)
        for pair in range(SWA_GROUP // 2):
            ce = slice(2 * pair * WINDOW, (2 * pair + 1) * WINDOW)
            co = slice((2 * pair + 1) * WINDOW, (2 * pair + 2) * WINDOW)
            p2_t = jnp.concatenate([pb[:, ce], pb[:, co]], axis=0)
            o2_t = _dot(vv_t, p2_t)
            o2_t = o2_t * jnp.where(top_rows, inv[:, ce], inv[:, co])
            cols.append(jnp.transpose(o2_t))
    return jnp.concatenate(cols, axis=1)


def _mixer_kernel(sinks_ref, x_ref, s0_ref, k0_ref, v0_ref, base0_ref, bias_ref,
                  g_mix_ref, w_in_ref, w_a_up_ref, b_a_ref, g_gla_ref, g_swa_ref, w_out_ref,
                  g_ffn_ref, w_r_ref, b_r_ref,
                  xmid_ref, hp_ref, topi_ref, gate_ref, rank_ref, sout_ref, kout_ref, vout_ref, cnt_ref,
                  s_scr, k_scr, v_scr, base_scr, *, n_lead_pad, prev_valid_from):
    b_id, j = pl.program_id(0), pl.program_id(1)
    tm = x_ref.shape[1]

    @pl.when(j == 0)
    def _():
        s_scr[...] = s0_ref[...]
        k_scr[...] = k0_ref[...]
        v_scr[...] = v0_ref[...]

    @pl.when((j == 0) & (b_id == 0))
    def _():
        base_scr[...] = base0_ref[...]

    x = x_ref[0]
    p = _project(x, g_mix_ref[...], w_in_ref, w_a_up_ref[...], b_a_ref[...])

    o_gla, s_new = _gla_chunks(p, j * tm, s_scr[...], n_lead_pad)
    s_scr[...] = s_new

    kj = _iota((2 * WINDOW, WINDOW), 0)
    qi = _iota((2 * WINDOW, WINDOW), 1)
    band = (kj > qi) & (kj <= qi + WINDOW)
    o_parts = []
    for sb in range(tm // WINDOW):
        rs = slice(sb * WINDOW, (sb + 1) * WINDOW)
        k_blk, v_blk = p["sk"][rs], p["sv"][rs]
        k_prev = k_scr[...] if sb == 0 else p["sk"][(sb - 1) * WINDOW:sb * WINDOW]
        v_prev = v_scr[...] if sb == 0 else p["sv"][(sb - 1) * WINDOW:sb * WINDOW]
        valid = band
        if sb == 0 and prev_valid_from:
            first = jnp.where(j == 0, prev_valid_from, 0)
            valid = band & (kj >= first)
        valid = jnp.concatenate([valid] * SWA_GROUP, axis=1)
        o_parts.append(_swa_block(p["sq"][rs], jnp.concatenate([k_prev, k_blk], axis=0),
                                  jnp.concatenate([v_prev, v_blk], axis=0), bias_ref, sinks_ref, valid))
    o_swa = jnp.concatenate(o_parts, axis=0)
    k_scr[...] = p["sk"][tm - WINDOW:tm]
    v_scr[...] = p["sv"][tm - WINDOW:tm]

    x_mid, hp, topi, gate8, rank8, base = _tail(
        x, o_gla, p["gr"], o_swa, g_gla_ref[...], g_swa_ref[...], w_out_ref,
        g_ffn_ref[...], w_r_ref[...], b_r_ref[...], base_scr[...])
    base_scr[...] = base
    xmid_ref[0] = x_mid
    _store_slabs(hp_ref, hp)
    topi_ref[...] = topi
    gate_ref[...] = gate8
    rank_ref[...] = rank8
    sout_ref[0] = s_new
    kout_ref[0] = p["sk"][tm - WINDOW:tm]
    vout_ref[0] = p["sv"][tm - WINDOW:tm]
    cnt_ref[...] = base


def _full_spec(shape):
    nd = len(shape)
    return pl.BlockSpec(shape, lambda *_: (0,) * nd)


def _mixer_call(x, s0, k0, v0, base0, wts, tm, n_lead_pad, prev_valid_from, hp_rows):
    B, L, _ = x.shape
    nj = L // tm
    T = B * L
    weight_args = (wts["bias"], wts["g_mix"], wts["w_in"], wts["w_a_up"], wts["b_a"], wts["g_gla"],
                   wts["g_swa"], wts["w_out"], wts["g_ffn"], wts["w_r"], wts["b_r"])
    in_specs = [
        pl.BlockSpec(memory_space=pltpu.SMEM),
        pl.BlockSpec((1, tm, D_MODEL), lambda b, j: (b, j, 0)),
        _full_spec(s0.shape), _full_spec(k0.shape), _full_spec(v0.shape), _full_spec(base0.shape),
    ] + [_full_spec(w.shape) for w in weight_args]
    tok_spec = pl.BlockSpec((8, tm), lambda b, j: (0, b * nj + j))
    out_specs = [
        pl.BlockSpec((1, tm, D_MODEL), lambda b, j: (b, j, 0)),
        pl.BlockSpec((tm * SLAB, LANES), lambda b, j: (b * nj + j, 0)),
        tok_spec, tok_spec, tok_spec,
        pl.BlockSpec((1, GLA_QK, GLA_V), lambda b, j: (b, 0, 0)),
        pl.BlockSpec((1, WINDOW, SWA_KV), lambda b, j: (b, 0, 0)),
        pl.BlockSpec((1, WINDOW, SWA_KV), lambda b, j: (b, 0, 0)),
        _full_spec((N_EXPERTS, LANES)),
    ]
    out_shape = [
        jax.ShapeDtypeStruct((B, L, D_MODEL), F32),
        jax.ShapeDtypeStruct((hp_rows * SLAB, LANES), F32),
        jax.ShapeDtypeStruct((8, T), jnp.int32),
        jax.ShapeDtypeStruct((8, T), F32),
        jax.ShapeDtypeStruct((8, T), jnp.int32),
        jax.ShapeDtypeStruct((B, GLA_QK, GLA_V), F32),
        jax.ShapeDtypeStruct((B, WINDOW, SWA_KV), F32),
        jax.ShapeDtypeStruct((B, WINDOW, SWA_KV), F32),
        jax.ShapeDtypeStruct((N_EXPERTS, LANES), F32),
    ]
    kern = functools.partial(_mixer_kernel, n_lead_pad=n_lead_pad, prev_valid_from=prev_valid_from)
    return pl.pallas_call(
        kern,
        grid=(B, nj),
        in_specs=in_specs,
        out_specs=out_specs,
        out_shape=out_shape,
        scratch_shapes=[pltpu.VMEM((GLA_QK, GLA_V), F32), pltpu.VMEM((WINDOW, SWA_KV), F32),
                        pltpu.VMEM((WINDOW, SWA_KV), F32), pltpu.VMEM((N_EXPERTS, LANES), F32)],
        compiler_params=pltpu.CompilerParams(dimension_semantics=("arbitrary", "arbitrary"),
                                             vmem_limit_bytes=VMEM_LIMIT),
        name="mixer",
    )(wts["sinks"], x, s0, k0, v0, base0, *weight_args)


def _decode_kernel(sinks_ref, x_ref, st_ref, ck_ref, cv_ref, base0_ref, bias_ref,
                   g_mix_ref, w_in_ref, w_a_up_ref, b_a_ref, g_gla_ref, g_swa_ref, w_out_ref,
                   g_ffn_ref, w_r_ref, b_r_ref,
                   xmid_ref, hp_ref, topi_ref, gate_ref, rank_ref, sto_ref, cko_ref, cvo_ref, cnt_ref,
                   at_scr, kt_scr, qt_scr, gv_scr, gr_scr, sq_scr, sk_scr, sv_scr, og_scr, os_scr):
    i = pl.program_id(0)
    n_seq = x_ref.shape[0]

    @pl.when(i == 0)
    def _():
        p = _project(x_ref[...], g_mix_ref[...], w_in_ref, w_a_up_ref[...], b_a_ref[...])
        at_scr[...] = jnp.transpose(jnp.exp(p["log_a"]))
        kt_scr[...] = jnp.transpose(p["gk"])
        qt_scr[...] = jnp.transpose(p["gq"])
        gv_scr[...] = p["gv"]
        gr_scr[...] = p["gr"]
        sq_scr[...] = p["sq"]
        sk_scr[...] = p["sk"]
        sv_scr[...] = p["sv"]

    lane_seq = _iota((GLA_QK, n_seq), 1)
    half = _iota((1, LANES), 1) < SWA_HEAD_DIM
    row_id = _iota((WINDOW, SWA_KV), 0)
    head_diag = (_iota((16, SWA_Q), 1) // SWA_HEAD_DIM) == _iota((16, SWA_Q), 0)
    sink_col = jnp.concatenate(
        [jnp.full((1, 1), sinks_ref[h], F32) for h in range(SWA_HEADS)] + [jnp.zeros((8, 1), F32)], axis=0)

    def per_seq(sl, carry):
        s = i * DEC_SB + sl
        sel = lane_seq == s
        a_c = jnp.sum(jnp.where(sel, at_scr[...], 0.0), axis=1, keepdims=True)
        k_c = jnp.sum(jnp.where(sel, kt_scr[...], 0.0), axis=1, keepdims=True)
        q_c = jnp.sum(jnp.where(sel, qt_scr[...], 0.0), axis=1, keepdims=True)
        st = st_ref[sl].reshape(GLA_QK, GLA_DV)
        v_row = gv_scr[pl.ds(s, 1), :]
        v_b = jnp.concatenate(
            [jnp.broadcast_to(v_row[:, h * GLA_DV:(h + 1) * GLA_DV], (GLA_DK, GLA_DV))
             for h in range(GLA_HEADS)], axis=0)
        st_new = a_c * st + k_c * v_b
        sto_ref[sl] = st_new.reshape(GLA_HEADS, GLA_DK, GLA_DV)
        t = q_c * st_new
        og_scr[pl.ds(s, 1), :] = jnp.concatenate(
            [jnp.sum(t[h * GLA_DK:(h + 1) * GLA_DK], axis=0, keepdims=True) for h in range(GLA_HEADS)],
            axis=1)

        k_new = sk_scr[pl.ds(s, 1), :]
        v_new = sv_scr[pl.ds(s, 1), :]
        kn = jnp.where(row_id == WINDOW - 1, k_new, pltpu.roll(ck_ref[sl], WINDOW - 1, 0))
        vn = jnp.where(row_id == WINDOW - 1, v_new, pltpu.roll(cv_ref[sl], WINDOW - 1, 0))
        cko_ref[sl] = kn
        cvo_ref[sl] = vn
        kr, vr = pltpu.roll(kn, SWA_HEAD_DIM, 1), pltpu.roll(vn, SWA_HEAD_DIM, 1)
        k0, k1 = jnp.where(half, kn, kr), jnp.where(half, kr, kn)
        v0, v1 = jnp.where(half, vn, vr), jnp.where(half, vr, vn)
        kw = jnp.concatenate([k0, k0, k1, k1], axis=1).astype(BF16)
        vw = jnp.concatenate([v0, v0, v1, v1], axis=1).astype(BF16)
        q_row = sq_scr[pl.ds(s, 1), :]
        qm = jnp.where(head_diag, jnp.broadcast_to(q_row, (16, SWA_Q)), 0.0).astype(BF16)
        sc = _dot_nt(qm, kw) * (SWA_HEAD_DIM ** -0.5) + bias_ref[...]
        m = jnp.maximum(jnp.max(sc, axis=1, keepdims=True), sink_col)
        pr = jnp.exp(sc - m)
        inv = 1.0 / (jnp.sum(pr, axis=1, keepdims=True) + jnp.exp(sink_col - m))
        ow = _dot(pr.astype(BF16), vw) * inv
        os_scr[pl.ds(s, 1), :] = jnp.sum(jnp.where(head_diag, ow, 0.0), axis=0, keepdims=True)
        return carry

    lax.fori_loop(0, DEC_SB, per_seq, 0, unroll=2)

    @pl.when(i == pl.num_programs(0) - 1)
    def _():
        x_mid, hp, topi, gate8, rank8, base = _tail(
            x_ref[...], og_scr[...], gr_scr[...], os_scr[...], g_gla_ref[...], g_swa_ref[...],
            w_out_ref, g_ffn_ref[...], w_r_ref[...], b_r_ref[...], base0_ref[...])
        xmid_ref[...] = x_mid
        _store_slabs(hp_ref, hp)
        topi_ref[...] = topi
        gate_ref[...] = gate8
        rank_ref[...] = rank8
        cnt_ref[...] = base


def _decode_call(xs, state, ck, cv, base0, bias_dec, wts):
    n_seq = xs.shape[0]
    nb = n_seq // DEC_SB
    weight_args = (wts["g_mix"], wts["w_in"], wts["w_a_up"], wts["b_a"], wts["g_gla"],
                   wts["g_swa"], wts["w_out"], wts["g_ffn"], wts["w_r"], wts["b_r"])
    in_specs = [
        pl.BlockSpec(memory_space=pltpu.SMEM),
        _full_spec(xs.shape),
        pl.BlockSpec((DEC_SB, GLA_HEADS, GLA_DK, GLA_DV), lambda i: (i, 0, 0, 0)),
        pl.BlockSpec((DEC_SB, WINDOW, SWA_KV), lambda i: (i, 0, 0)),
        pl.BlockSpec((DEC_SB, WINDOW, SWA_KV), lambda i: (i, 0, 0)),
        _full_spec(base0.shape), _full_spec(bias_dec.shape),
    ] + [_full_spec(w.shape) for w in weight_args]
    out_specs = [
        _full_spec((n_seq, D_MODEL)),
        _full_spec((n_seq * SLAB, LANES)),
        _full_spec((8, n_seq)), _full_spec((8, n_seq)), _full_spec((8, n_seq)),
        pl.BlockSpec((DEC_SB, GLA_HEADS, GLA_DK, GLA_DV), lambda i: (i, 0, 0, 0)),
        pl.BlockSpec((DEC_SB, WINDOW, SWA_KV), lambda i: (i, 0, 0)),
        pl.BlockSpec((DEC_SB, WINDOW, SWA_KV), lambda i: (i, 0, 0)),
        _full_spec((N_EXPERTS, LANES)),
    ]
    out_shape = [
        jax.ShapeDtypeStruct((n_seq, D_MODEL), F32),
        jax.ShapeDtypeStruct((n_seq * SLAB, LANES), F32),
        jax.ShapeDtypeStruct((8, n_seq), jnp.int32),
        jax.ShapeDtypeStruct((8, n_seq), F32),
        jax.ShapeDtypeStruct((8, n_seq), jnp.int32),
        jax.ShapeDtypeStruct(state.shape, F32),
        jax.ShapeDtypeStruct(ck.shape, F32),
        jax.ShapeDtypeStruct(cv.shape, F32),
        jax.ShapeDtypeStruct((N_EXPERTS, LANES), F32),
    ]
    scratch = [pltpu.VMEM((GLA_QK, n_seq), F32)] * 3 + [
        pltpu.VMEM((n_seq, GLA_V), F32), pltpu.VMEM((n_seq, GLA_V), F32), pltpu.VMEM((n_seq, SWA_Q), F32),
        pltpu.VMEM((n_seq, SWA_KV), F32), pltpu.VMEM((n_seq, SWA_KV), F32),
        pltpu.VMEM((n_seq, GLA_V), F32), pltpu.VMEM((n_seq, SWA_Q), F32)]
    return pl.pallas_call(
        _decode_kernel,
        grid=(nb,),
        in_specs=in_specs,
        out_specs=out_specs,
        out_shape=out_shape,
        scratch_shapes=scratch,
        compiler_params=pltpu.CompilerParams(dimension_semantics=("arbitrary",),
                                             vmem_limit_bytes=VMEM_LIMIT),
        name="decode",
    )(wts["sinks"], xs, state, ck, cv, base0, bias_dec, *weight_args)


SC_CORES = 2
SC_SUBCORES = 16
SC_WORKERS = SC_CORES * SC_SUBCORES
SC_SCATTER_ROWS = 32
SC_GATHER_ROWS = 24


def _sc_mesh():
    return plsc.VectorSubcoreMesh(core_axis_name="c", subcore_axis_name="s")


def _sc_worker_id():
    return lax.axis_index("s") * SC_CORES + lax.axis_index("c")


def _sc_scatter_rows(src_p, src_s, idx_p, idx_s, n_out):
    rows = SC_SCATTER_ROWS
    n_chunks = idx_p.shape[0] // SC_WORKERS
    n_s, _, rows_s = idx_s.shape
    assert n_chunks * SC_WORKERS == idx_p.shape[0] and n_chunks % 2 == 0 and n_s <= SC_WORKERS

    @functools.partial(
        pl.kernel, mesh=_sc_mesh(),
        out_type=jax.ShapeDtypeStruct((n_out, SLAB, LANES), F32),
        scratch_types=[pltpu.VMEM((2, TOP_K, rows), jnp.int32), pltpu.VMEM((2, rows, SLAB, LANES), F32),
                       pltpu.VMEM((TOP_K, rows_s), jnp.int32), pltpu.VMEM((rows_s, SLAB, LANES), F32),
                       pltpu.SemaphoreType.DMA((2,)), pltpu.SemaphoreType.DMA((2,))])
    def scatter_rows(srcp_hbm, srcs_hbm, idxp_hbm, idxs_hbm, out_hbm, idx_v, rows_v, idxs_v, rowss_v, lsem, ssem):
        wid = _sc_worker_id()

        def loads(c, b):
            g = wid * n_chunks + c
            return (pltpu.make_async_copy(idxp_hbm.at[g], idx_v.at[b], lsem.at[b]),
                    pltpu.make_async_copy(srcp_hbm.at[pl.ds(pl.multiple_of(g * rows, 8), rows)], rows_v.at[b],
                                          lsem.at[b]))

        def scatters(b):
            return [pltpu.make_async_copy(rows_v.at[b], out_hbm.at[idx_v.at[b, k]], ssem.at[b])
                    for k in range(TOP_K)]

        for d in loads(0, 0):
            d.start()

        @pl.loop(0, n_chunks, step=2)
        def _(c0):
            for b in range(2):
                c = c0 + b
                for d in loads(c, b):
                    d.wait()

                @pl.when(c >= 1)
                def _():
                    for d in scatters(1 - b):
                        d.wait()

                @pl.when(c + 1 < n_chunks)
                def _():
                    for d in loads(c + 1, 1 - b):
                        d.start()

                for d in scatters(b):
                    d.start()

        for d in scatters((n_chunks - 1) % 2):
            d.wait()

        @pl.when(wid < n_s)
        def _():
            pltpu.sync_copy(idxs_hbm.at[wid], idxs_v)
            pltpu.sync_copy(srcs_hbm.at[pl.ds(pl.multiple_of(wid * rows_s, 8), rows_s)], rowss_v)
            for k in range(TOP_K):
                pltpu.sync_copy(rowss_v, out_hbm.at[idxs_v.at[k]])

    return scatter_rows(src_p, src_s, idx_p, idx_s)


def _sc_gather_rows(src3, idx2):
    rows = SC_GATHER_ROWS
    n_chunks = idx2.shape[0] // SC_WORKERS
    assert n_chunks * SC_WORKERS == idx2.shape[0] and idx2.shape[1] == rows and n_chunks % 2 == 0

    @functools.partial(
        pl.kernel, mesh=_sc_mesh(),
        out_type=jax.ShapeDtypeStruct((idx2.shape[0] * rows, SLAB, LANES), F32),
        scratch_types=[pltpu.VMEM((2, rows), jnp.int32), pltpu.VMEM((2, rows, SLAB, LANES), F32),
                       pltpu.SemaphoreType.DMA((2,)), pltpu.SemaphoreType.DMA((2,))])
    def gather_rows(src_hbm, idx_hbm, out_hbm, idx_v, rows_v, gsem, wsem):
        wid = _sc_worker_id()

        def gather(b):
            return pltpu.make_async_copy(src_hbm.at[idx_v.at[b]], rows_v.at[b], gsem.at[b])

        def write(c, b):
            base = pl.multiple_of((wid * n_chunks + c) * rows, 8)
            return pltpu.make_async_copy(rows_v.at[b], out_hbm.at[pl.ds(base, rows)], wsem.at[b])

        pltpu.sync_copy(idx_hbm.at[wid * n_chunks], idx_v.at[0])
        gather(0).start()

        @pl.loop(0, n_chunks, step=2)
        def _(c0):
            for b in range(2):
                c = c0 + b

                @pl.when(c + 1 < n_chunks)
                def _():
                    @pl.when(c >= 1)
                    def _():
                        write(c - 1, 1 - b).wait()
                    pltpu.sync_copy(idx_hbm.at[wid * n_chunks + c + 1], idx_v.at[1 - b])
                    gather(1 - b).start()

                gather(b).wait()
                write(c, b).start()

        write(n_chunks - 2, 0).wait()
        write(n_chunks - 1, 1).wait()

    return gather_rows(src3, idx2)


FF_TILE = 256


def _ffn_kernel(blk_e_ref, nused_ref, x_ref, wu_ref, bu_ref, wd_ref, bd_ref, y_ref, xbf, actbf, wu_bf, wd_bf):
    i = pl.program_id(0)
    tm = MOE_TM
    n_tiles = D_FF // FF_TILE

    @pl.when(i < nused_ref[0])
    def _():
        @pl.when((i == 0) | (blk_e_ref[i] != blk_e_ref[jnp.maximum(i - 1, 0)]))
        def _():
            wu_bf[...] = wu_ref[0].astype(BF16)
            wd_bf[...] = wd_ref[0].astype(BF16)

        for c in range(SLAB):
            xbf[:, c * LANES:(c + 1) * LANES] = _load_slab_chunk(x_ref, tm, c, SLAB).astype(BF16)
        for n in range(n_tiles):
            gc = slice(n * FF_TILE, (n + 1) * FF_TILE)
            lc = slice(D_FF + n * FF_TILE, D_FF + (n + 1) * FF_TILE)
            g = jnp.minimum(_dot(xbf[...], wu_bf[:, gc]) + bu_ref[0, :, gc], SWIGLU_LIMIT)
            lin = jnp.clip(_dot(xbf[...], wu_bf[:, lc]) + bu_ref[0, :, lc], -SWIGLU_LIMIT, SWIGLU_LIMIT)
            actbf[:, gc] = (g * jax.nn.sigmoid(SWIGLU_ALPHA * g) * (lin + 1.0)).astype(BF16)
        for n in range(n_tiles):
            yc = slice(n * FF_TILE, (n + 1) * FF_TILE)
            y = _dot(actbf[...], wd_bf[:, yc]) + bd_ref[0, :, yc]
            for c in range(FF_TILE // LANES):
                y_ref[pl.ds(n * (FF_TILE // LANES) + c, tm, stride=SLAB), :] = y[:, c * LANES:(c + 1) * LANES]

    @pl.when(i >= nused_ref[0])
    def _():
        y_ref[...] = jnp.zeros_like(y_ref)


def _ffn_call(blk_e, nused, xs2, w_up, b_up, w_down, b_down):
    n_blocks = blk_e.shape[0]
    tm = MOE_TM
    row_blk = pl.BlockSpec((tm * SLAB, LANES), lambda i, be, nu: (i, 0))
    grid_spec = pltpu.PrefetchScalarGridSpec(
        num_scalar_prefetch=2,
        grid=(n_blocks,),
        in_specs=[
            row_blk,
            pl.BlockSpec((1, D_MODEL, 2 * D_FF), lambda i, be, nu: (be[i], 0, 0)),
            pl.BlockSpec((1, 1, 2 * D_FF), lambda i, be, nu: (be[i], 0, 0)),
            pl.BlockSpec((1, D_FF, D_MODEL), lambda i, be, nu: (be[i], 0, 0)),
            pl.BlockSpec((1, 1, D_MODEL), lambda i, be, nu: (be[i], 0, 0)),
        ],
        out_specs=row_blk,
        scratch_shapes=[pltpu.VMEM((tm, D_MODEL), BF16), pltpu.VMEM((tm, D_FF), BF16),
                        pltpu.VMEM((D_MODEL, 2 * D_FF), BF16), pltpu.VMEM((D_FF, D_MODEL), BF16)],
    )
    return pl.pallas_call(
        _ffn_kernel,
        grid_spec=grid_spec,
        out_shape=jax.ShapeDtypeStruct((n_blocks * tm * SLAB, LANES), F32),
        compiler_params=pltpu.CompilerParams(dimension_semantics=("arbitrary",),
                                             vmem_limit_bytes=VMEM_LIMIT),
        name="experts",
    )(blk_e, nused, xs2, w_up, b_up.reshape(N_EXPERTS, 1, 2 * D_FF), w_down, b_down.reshape(N_EXPERTS, 1, D_MODEL))


def _combine_kernel(ys0_ref, ys1_ref, ys2_ref, ys3_ref, xmid_ref, gate_ref, g_final_ref, y_ref):
    tm = xmid_ref.shape[0]
    gts = gate_ref[...]
    chunks = []
    for c in range(SLAB):
        acc = xmid_ref[:, c * LANES:(c + 1) * LANES]
        for k, ys_ref in enumerate((ys0_ref, ys1_ref, ys2_ref, ys3_ref)):
            acc = acc + _load_slab_chunk(ys_ref, tm, c, SLAB) * gts[:, k:k + 1]
        chunks.append(acc)
    y_ref[...] = _rms(jnp.concatenate(chunks, axis=1), g_final_ref[...])


def _combine_call(ys4, t_stride, row0, x_mid, gates, g_final, tm):
    T = x_mid.shape[0]
    blk0 = row0 // tm
    per_k = t_stride // tm
    assert per_k * tm == t_stride and blk0 * tm == row0

    def ys_spec(k):
        return pl.BlockSpec((tm * SLAB, LANES), lambda i: (k * per_k + blk0 + i, 0))

    return pl.pallas_call(
        _combine_kernel,
        grid=(T // tm,),
        in_specs=[
            ys_spec(0), ys_spec(1), ys_spec(2), ys_spec(3),
            pl.BlockSpec((tm, D_MODEL), lambda i: (i, 0)),
            pl.BlockSpec((tm, TOP_K), lambda i: (i, 0)),
            _full_spec((1, D_MODEL)),
        ],
        out_specs=pl.BlockSpec((tm, D_MODEL), lambda i: (i, 0)),
        out_shape=jax.ShapeDtypeStruct((T, D_MODEL), F32),
        compiler_params=pltpu.CompilerParams(dimension_semantics=("arbitrary",),
                                             vmem_limit_bytes=VMEM_LIMIT),
        name="combine",
    )(ys4, ys4, ys4, ys4, x_mid, gates, g_final)


def _t5_bucket(dist):
    n = jnp.maximum(dist, 0)
    max_exact = NUM_BUCKETS // 2
    nf = jnp.maximum(n, 1).astype(F32)
    large = max_exact + (jnp.log(nf / max_exact) / math.log(MAX_DISTANCE / max_exact)
                         * (NUM_BUCKETS - max_exact)).astype(jnp.int32)
    large = jnp.minimum(large, NUM_BUCKETS - 1)
    return jnp.where(n < max_exact, n, large)


def kernel(x_prompt, x_sample, state_gla, cache_swa_k, cache_swa_v, meta_tokens, rel_bias_table,
           g_mix, w_in, w_a_up, b_a, g_gla_out, g_swa_out, attn_sinks, w_out,
           g_ffn, w_router, b_router, w_up, b_up, w_down, b_down, g_final):
    assert g_mix.shape[0] == 1, "single-layer trunk"
    B, L, _ = x_prompt.shape
    n_seq = x_sample.shape[0]
    TP = B * L
    T_all = TP + n_seq

    wi = w_in[0]
    sizes = (GLA_QK, GLA_QK, GLA_V, GLA_V, GLA_LOWRANK, SWA_Q, SWA_KV, SWA_KV)
    offs = [0]
    for s in sizes:
        offs.append(offs[-1] + s)
    seg = [wi[:, offs[n]:offs[n + 1]] for n in range(8)]
    w_in_r = jnp.concatenate(
        seg[0:4] + seg[5:8] + [seg[4], jnp.zeros((D_MODEL, LANES - GLA_LOWRANK), F32)], axis=1).astype(BF16)
    w_a_pad = jnp.concatenate([w_a_up[0], jnp.zeros((LANES - GLA_LOWRANK, GLA_QK), F32)], axis=0).astype(BF16)
    wr_t = jnp.transpose(w_router[0])
    wr_hi = wr_t.astype(BF16)
    wr_lo = (wr_t - wr_hi.astype(F32)).astype(BF16)
    qi = jnp.arange(WINDOW)[:, None]
    kj = jnp.arange(2 * WINDOW)[None, :]
    buckets = jnp.arange(NUM_BUCKETS)
    table = rel_bias_table.astype(F32)
    oh_p = (_t5_bucket(qi - kj + WINDOW)[..., None] == buckets).astype(F32)
    bias_p = jnp.einsum("qkb,bh->hkq", oh_p, table, precision=lax.Precision.HIGHEST)
    bias_p = bias_p.reshape(SWA_KV_HEADS, SWA_GROUP, 2 * WINDOW, WINDOW).transpose(0, 2, 1, 3)
    bias_p = bias_p.reshape(SWA_KV_HEADS, 2 * WINDOW, SWA_GROUP * WINDOW)
    oh_d = (_t5_bucket(WINDOW - 1 - jnp.arange(WINDOW))[:, None] == buckets).astype(F32)
    bias_d = jnp.einsum("rb,bh->hr", oh_d, table, precision=lax.Precision.HIGHEST)
    bias_d = jnp.concatenate([bias_d, jnp.zeros((8, WINDOW), F32)], axis=0)
    wts = dict(
        sinks=attn_sinks[0].astype(F32), bias=bias_p,
        g_mix=g_mix[0][None], w_in=w_in_r, w_a_up=w_a_pad, b_a=b_a[0][None],
        g_gla=g_gla_out[0][None], g_swa=g_swa_out[0][None], w_out=w_out[0].astype(BF16),
        g_ffn=g_ffn[0][None], w_r=jnp.concatenate([wr_hi, wr_lo], axis=0), b_r=b_router[0][:, None],
    )

    x_pre = jnp.concatenate([jnp.zeros((WINDOW - N_META, D_MODEL), F32), meta_tokens.astype(F32)], axis=0)[None]
    zeros_s = jnp.zeros((GLA_QK, GLA_V), F32)
    zeros_kv = jnp.zeros((WINDOW, SWA_KV), F32)
    zeros_b = jnp.zeros((N_EXPERTS, LANES), F32)
    pre = _mixer_call(x_pre, zeros_s, zeros_kv, zeros_kv, zeros_b, wts, WINDOW, WINDOW - N_META, 0, WINDOW)
    s_meta, k_meta, v_meta = pre[5][0], pre[6][0], pre[7][0]

    (xmid_p, hp_p, topi_p, gate_p, rank_p, s_p, k_p, v_p, cnt_p) = _mixer_call(
        x_prompt, s_meta, k_meta, v_meta, zeros_b, wts, MIX_TM, 0, WINDOW - N_META, TP)

    (xmid_s, hp_s, topi_s, gate_s, rank_s, st_s, ck_s, cv_s, cnt_all) = _decode_call(
        x_sample[:, 0], state_gla[0], cache_swa_k[0].reshape(n_seq, WINDOW, SWA_KV),
        cache_swa_v[0].reshape(n_seq, WINDOW, SWA_KV), cnt_p, bias_d, wts)

    tm = MOE_TM
    n_slots = T_all * TOP_K
    n_blocks = -(-n_slots // tm) + N_EXPERTS
    top_e = jnp.concatenate([topi_p[:TOP_K], topi_s[:TOP_K]], axis=1)
    rank = jnp.concatenate([rank_p[:TOP_K], rank_s[:TOP_K]], axis=1)
    counts = cnt_all[:, 0].astype(jnp.int32)
    padded = (counts + tm - 1) // tm * tm
    pad_end = jnp.cumsum(padded)
    pad_start = pad_end - padded
    e_ids = jnp.arange(N_EXPERTS, dtype=jnp.int32)
    dest = jnp.sum(jnp.where(top_e[..., None] == e_ids, pad_start, 0), axis=-1) + rank
    n_pad = n_blocks * tm
    blk_e = jnp.minimum(jnp.sum(pad_end[None] <= (jnp.arange(n_blocks, dtype=jnp.int32) * tm)[:, None], axis=1),
                        N_EXPERTS - 1).astype(jnp.int32)
    nused = (pad_end[-1] // tm).astype(jnp.int32).reshape(1)

    sample_rows = 8
    idx_p = dest[:, :TP].reshape(TOP_K, TP // SC_SCATTER_ROWS, SC_SCATTER_ROWS).transpose(1, 0, 2)
    idx_s = dest[:, TP:].reshape(TOP_K, n_seq // sample_rows, sample_rows).transpose(1, 0, 2)
    xs3 = _sc_scatter_rows(hp_p.reshape(TP, SLAB, LANES), hp_s.reshape(n_seq, SLAB, LANES), idx_p, idx_s, n_pad)
    ys2 = _ffn_call(blk_e, nused, xs3.reshape(-1, LANES), w_up[0], b_up[0], w_down[0], b_down[0])
    unit = math.lcm(2 * SC_WORKERS * SC_GATHER_ROWS // TOP_K, MIX_TM)
    t_stride = -(-T_all // unit) * unit
    filler = jnp.arange(TOP_K * (t_stride - T_all), dtype=jnp.int32).reshape(TOP_K, t_stride - T_all)
    slot_src = jnp.concatenate([dest, filler], axis=1)
    slot_src = slot_src.reshape(TOP_K * t_stride // SC_GATHER_ROWS, SC_GATHER_ROWS)
    ys4 = _sc_gather_rows(ys2.reshape(-1, SLAB, LANES), slot_src).reshape(-1, LANES)

    gates = jnp.transpose(jnp.concatenate([gate_p[:TOP_K], gate_s[:TOP_K]], axis=1))
    gf = g_final[None]
    y_p = _combine_call(ys4, t_stride, 0, xmid_p.reshape(TP, D_MODEL), gates[:TP], gf, MIX_TM)
    y_s = _combine_call(ys4, t_stride, TP, xmid_s, gates[TP:], gf, n_seq)

    s_heads = jnp.stack([s_p[:, h * GLA_DK:(h + 1) * GLA_DK, h * GLA_DV:(h + 1) * GLA_DV]
                         for h in range(GLA_HEADS)], axis=1)
    return (y_p.reshape(B, L, D_MODEL), y_s.reshape(n_seq, 1, D_MODEL), s_heads[None],
            k_p.reshape(1, B, WINDOW, SWA_KV_HEADS, SWA_HEAD_DIM),
            v_p.reshape(1, B, WINDOW, SWA_KV_HEADS, SWA_HEAD_DIM),
            st_s[None], ck_s.reshape(1, n_seq, WINDOW, SWA_KV_HEADS, SWA_HEAD_DIM),
            cv_s.reshape(1, n_seq, WINDOW, SWA_KV_HEADS, SWA_HEAD_DIM))
```

```python
import functools
import math

import jax
import jax.numpy as jnp
from jax import lax
from jax.experimental import pallas as pl
from jax.experimental.pallas import tpu as pltpu
from jax.experimental.pallas import tpu_sc as plsc

D_MODEL = 1024
N_META = 16
GLA_HEADS = 4
GLA_DK = 64
GLA_DV = 128
GLA_LOWRANK = 16
GLA_GATE_TAU = 16.0
GLA_CHUNK = 64
SWA_HEADS = 8
SWA_KV_HEADS = 2
SWA_HEAD_DIM = 64
SWA_GROUP = SWA_HEADS // SWA_KV_HEADS
WINDOW = 128
NUM_BUCKETS = 32
MAX_DISTANCE = 128
N_EXPERTS = 32
TOP_K = 4
D_FF = 1024
SWIGLU_ALPHA = 1.702
SWIGLU_LIMIT = 7.0
RMS_EPS = 1e-6

GLA_QK = GLA_HEADS * GLA_DK
GLA_V = GLA_HEADS * GLA_DV
SWA_Q = SWA_HEADS * SWA_HEAD_DIM
SWA_KV = SWA_KV_HEADS * SWA_HEAD_DIM
LANES = 128
C_GQ, C_GK, C_GV, C_GR = 0, GLA_QK, 2 * GLA_QK, 2 * GLA_QK + GLA_V
C_SQ = C_GR + GLA_V
C_SK = C_SQ + SWA_Q
C_SV = C_SK + SWA_KV
C_GA = C_SV + SWA_KV
D_PROJ = C_GA + LANES

MIX_TM = 512
MOE_TM = 512
DEC_SB = 16
VMEM_LIMIT = 56 * 1024 * 1024

F32 = jnp.float32
BF16 = jnp.bfloat16
NEG_INF = float("-inf")


def _dot(a, b):
    return jnp.dot(a, b, preferred_element_type=F32)


def _dot_nt(a, b):
    return lax.dot_general(a, b, (((1,), (1,)), ((), ())), preferred_element_type=F32)


def _split3(x):
    hi = x.astype(BF16)
    r1 = x - hi.astype(F32)
    mid = r1.astype(BF16)
    lo = (r1 - mid.astype(F32)).astype(BF16)
    return hi, mid, lo


def _rms(x, g):
    return x * lax.rsqrt(jnp.mean(x * x, axis=-1, keepdims=True) + RMS_EPS) * g


def _iota(shape, dim):
    return lax.broadcasted_iota(jnp.int32, shape, dim)


HALF_D = D_MODEL // 2
SLAB = HALF_D // LANES


def _pack_pair(lo, hi):
    bl = lax.bitcast_convert_type(lo.astype(BF16).astype(F32), jnp.uint32)
    bh = lax.bitcast_convert_type(hi.astype(BF16).astype(F32), jnp.uint32)
    return lax.bitcast_convert_type(bh | lax.shift_right_logical(bl, jnp.uint32(16)), jnp.int32)


def _unpack_pair(w):
    u = lax.bitcast_convert_type(w, jnp.uint32)
    lo = lax.bitcast_convert_type(lax.shift_left(u, jnp.uint32(16)), F32)
    hi = lax.bitcast_convert_type(u & jnp.uint32(0xFFFF0000), F32)
    return lo, hi


def _store_slabs(ref, x):
    rows = x.shape[0]
    for c in range(SLAB):
        sl = slice(c * LANES, (c + 1) * LANES)
        ref[pl.ds(c, rows, stride=SLAB), :] = _pack_pair(x[:, sl], x[:, HALF_D + c * LANES:HALF_D + (c + 1) * LANES])


def _load_slab_chunk(ref, rows, c):
    return _unpack_pair(ref[pl.ds(c, rows, stride=SLAB), :])


def _project(x, g_mix, w_in_ref, w_a_up, b_a):
    h = _rms(x, g_mix).astype(BF16)

    def cols(lo, width):
        return _dot(h, w_in_ref[:, lo:lo + width])

    ga = cols(C_GA, LANES).astype(BF16)
    z = _dot(ga, w_a_up) + b_a
    log_a = -(jnp.maximum(-z, 0.0) + jnp.log1p(jnp.exp(-jnp.abs(z)))) / GLA_GATE_TAU
    gqk = cols(C_GQ, 2 * GLA_QK)
    gv = cols(C_GV, GLA_V)
    swa = cols(C_SQ, SWA_Q + 2 * SWA_KV)
    gr = cols(C_GR, GLA_V)
    return dict(
        gq=gqk[:, :GLA_QK] * (GLA_DK ** -0.5),
        gk=gqk[:, GLA_QK:],
        gv=gv,
        gr=gr,
        sq=swa[:, :SWA_Q],
        sk=swa[:, SWA_Q:SWA_Q + SWA_KV],
        sv=swa[:, SWA_Q + SWA_KV:],
        log_a=log_a,
    )


def _tail(x, o_gla, gr, o_swa, g_gla_out, g_swa_out, w_out_ref, g_ffn, w_r, b_r, base):
    tm = x.shape[0]
    gate = gr * jax.nn.sigmoid(gr)
    parts = []
    for h in range(GLA_HEADS):
        sl = slice(h * GLA_DV, (h + 1) * GLA_DV)
        parts.append(_rms(o_gla[:, sl], g_gla_out) * gate[:, sl])
    og = jnp.concatenate(parts, axis=1).astype(BF16)
    os_ = _rms(o_swa, g_swa_out).astype(BF16)
    x_mid = x + _dot(og, w_out_ref[0:GLA_V]) + _dot(os_, w_out_ref[GLA_V:GLA_V + SWA_Q])
    hp = _rms(x_mid, g_ffn)

    h1 = hp.astype(BF16)
    h2 = (hp - h1.astype(F32)).astype(BF16)
    la = _dot_nt(w_r, h1)
    lb = _dot_nt(w_r[0:N_EXPERTS], h2)
    logits = la[0:N_EXPERTS] + la[N_EXPERTS:2 * N_EXPERTS] + lb + b_r

    eidx = _iota((N_EXPERTS, tm), 0)
    vals, idxs, onehots = [], [], []
    l = logits
    for _ in range(TOP_K):
        m = jnp.max(l, axis=0, keepdims=True)
        sel = jnp.min(jnp.where(l == m, eidx, N_EXPERTS), axis=0, keepdims=True)
        oh = eidx == sel
        l = jnp.where(oh, NEG_INF, l)
        vals.append(m)
        idxs.append(sel)
        onehots.append(oh)
    es = [jnp.exp(v - vals[0]) for v in vals]
    denom = es[0] + es[1] + es[2] + es[3]
    gates = [e / denom for e in es]

    ohf = jnp.concatenate([oh.astype(F32) for oh in onehots], axis=0)
    upper = (_iota((tm, tm), 0) < _iota((tm, tm), 1)).astype(BF16)
    prefix = _dot(ohf.astype(BF16), upper)
    ranks = []
    for k in range(TOP_K):
        sl = slice(k * N_EXPERTS, (k + 1) * N_EXPERTS)
        ohk = ohf[sl]
        base_t = jnp.concatenate([base] * (tm // LANES), axis=1)
        ranks.append(jnp.sum(ohk * (prefix[sl] + base_t), axis=0, keepdims=True))
        base = base + jnp.sum(ohk, axis=1, keepdims=True)
    zi = jnp.zeros((8 - TOP_K, tm), jnp.int32)
    zf = jnp.zeros((8 - TOP_K, tm), F32)
    topi = jnp.concatenate(idxs + [zi], axis=0)
    gate8 = jnp.concatenate(gates + [zf], axis=0)
    rank8 = jnp.concatenate([r.astype(jnp.int32) for r in ranks] + [zi], axis=0)
    return x_mid, hp, topi, gate8, rank8, base


def _gla_chunks(p, row0, s_bd, n_lead_pad):
    tm = p["gq"].shape[0]
    nch = tm // GLA_CHUNK
    log_a = p["log_a"]
    if n_lead_pad:
        rows = row0 + _iota((tm, GLA_QK), 0)
        log_a = jnp.where(rows >= n_lead_pad, log_a, 0.0)
    ri, ci = _iota((tm, tm), 0), _iota((tm, tm), 1)
    tril = ((ri >= ci) & (ri // GLA_CHUNK == ci // GLA_CHUNK)).astype(BF16)
    hi, mid, lo = _split3(log_a)
    b_all = _dot(tril, hi) + _dot(tril, mid) + _dot(tril, lo)

    c64 = GLA_CHUNK
    kk_mask = (_iota((GLA_QK, GLA_QK), 0) // c64) == (_iota((GLA_QK, GLA_QK), 1) // GLA_DK)
    vv_mask = (_iota((GLA_QK, GLA_V), 0) // c64) == (_iota((GLA_QK, GLA_V), 1) // GLA_DV)
    ss_mask = (_iota((GLA_QK, GLA_V), 0) // GLA_DK) == (_iota((GLA_QK, GLA_V), 1) // GLA_DV)
    causal = (_iota((c64, GLA_QK), 0) >= (_iota((c64, GLA_QK), 1) % c64)).astype(F32)
    zpad_k = jnp.zeros((LANES - c64, GLA_QK), F32)
    zpad_v = jnp.zeros((LANES - c64, GLA_V), BF16)

    outs = []
    for c in range(nch):
        rs = slice(c * c64, (c + 1) * c64)
        b = b_all[rs]
        q, k, v = p["gq"][rs], p["gk"][rs], p["gv"][rs]
        b_last = b[c64 - 1:c64]
        qt = (q * jnp.exp(b)).astype(BF16)
        kt = k * jnp.exp(-b)
        kd = k * jnp.exp(b_last - b)
        vb = v.astype(BF16)
        k_bd = jnp.where(kk_mask, jnp.concatenate([kt] * GLA_HEADS, axis=0), 0.0).astype(BF16)
        a = (_dot_nt(qt, k_bd) * causal).astype(BF16)
        v_bd = jnp.where(vv_mask, jnp.concatenate([vb] * GLA_HEADS, axis=0), jnp.zeros((), BF16))
        outs.append(_dot(qt, s_bd.astype(BF16)) + _dot(a, v_bd))
        kd_t = jnp.transpose(jnp.concatenate([kd, zpad_k], axis=0)).astype(BF16)
        upd = _dot(kd_t, jnp.concatenate([vb, zpad_v], axis=0))
        decay = jnp.exp(jnp.transpose(jnp.broadcast_to(b_last, (LANES, GLA_QK))))
        s_bd = s_bd * jnp.concatenate([decay] * GLA_HEADS, axis=1) + jnp.where(ss_mask, upd, 0.0)
    return jnp.concatenate(outs, axis=0), s_bd


def _swa_block(sq, kcat, vcat, bias_ref, sinks_ref, valid_t):
    half = _iota((1, LANES), 1) < SWA_HEAD_DIM
    top_rows = _iota((LANES, 1), 0) < SWA_HEAD_DIM
    k_roll = pltpu.roll(kcat, SWA_HEAD_DIM, 1)
    v_t = jnp.transpose(vcat)
    zeros_v = jnp.zeros((SWA_HEAD_DIM, 2 * WINDOW), F32)
    cols = []
    for kv in range(SWA_KV_HEADS):
        kk = jnp.where(half, kcat, k_roll) if kv == 0 else jnp.where(half, k_roll, kcat)
        q_parts = []
        for c in (2 * kv, 2 * kv + 1):
            qc = sq[:, c * LANES:(c + 1) * LANES]
            q_parts.append(jnp.where(half, qc, 0.0))
            q_parts.append(jnp.where(half, 0.0, qc))
        q_st = jnp.concatenate(q_parts, axis=0).astype(BF16)
        s = _dot_nt(kk.astype(BF16), q_st) * (SWA_HEAD_DIM ** -0.5) + bias_ref[kv]
        s = jnp.where(valid_t, s, NEG_INF)
        sink = jnp.concatenate(
            [jnp.full((1, WINDOW), sinks_ref[kv * SWA_GROUP + g], F32) for g in range(SWA_GROUP)], axis=1)
        m = jnp.maximum(jnp.max(s, axis=0, keepdims=True), sink)
        pr = jnp.exp(s - m)
        inv = 1.0 / (jnp.sum(pr, axis=0, keepdims=True) + jnp.exp(sink - m))
        pb = pr.astype(BF16)
        vk = v_t[kv * SWA_HEAD_DIM:(kv + 1) * SWA_HEAD_DIM]
        vv_t = jnp.concatenate([jnp.concatenate([vk, zeros_v], axis=1),
                                jnp.concatenate([zeros_v, vk], axis=1)], axis=0).astype(BF16)
        for pair in range(SWA_GROUP // 2):
            ce = slice(2 * pair * WINDOW, (2 * pair + 1) * WINDOW)
            co = slice((2 * pair + 1) * WINDOW, (2 * pair + 2) * WINDOW)
            p2_t = jnp.concatenate([pb[:, ce], pb[:, co]], axis=0)
            o2_t = _dot(vv_t, p2_t)
            o2_t = o2_t * jnp.where(top_rows, inv[:, ce], inv[:, co])
            cols.append(jnp.transpose(o2_t))
    return jnp.concatenate(cols, axis=1)


def _mixer_kernel(sinks_ref, x_ref, s0_ref, k0_ref, v0_ref, base0_ref, bias_ref,
                  g_mix_ref, w_in_ref, w_a_up_ref, b_a_ref, g_gla_ref, g_swa_ref, w_out_ref,
                  g_ffn_ref, w_r_ref, b_r_ref,
                  xmid_ref, hp_ref, topi_ref, gate_ref, rank_ref, sout_ref, kout_ref, vout_ref, cnt_ref,
                  s_scr, k_scr, v_scr, base_scr, *, n_lead_pad, prev_valid_from):
    b_id, j = pl.program_id(0), pl.program_id(1)
    tm = x_ref.shape[1]

    @pl.when(j == 0)
    def _():
        s_scr[...] = s0_ref[...]
        k_scr[...] = k0_ref[...]
        v_scr[...] = v0_ref[...]

    @pl.when((j == 0) & (b_id == 0))
    def _():
        base_scr[...] = base0_ref[...]

    x = x_ref[0]
    p = _project(x, g_mix_ref[...], w_in_ref, w_a_up_ref[...], b_a_ref[...])

    o_gla, s_new = _gla_chunks(p, j * tm, s_scr[...], n_lead_pad)
    s_scr[...] = s_new

    kj = _iota((2 * WINDOW, WINDOW), 0)
    qi = _iota((2 * WINDOW, WINDOW), 1)
    band = (kj > qi) & (kj <= qi + WINDOW)
    o_parts = []
    for sb in range(tm // WINDOW):
        rs = slice(sb * WINDOW, (sb + 1) * WINDOW)
        k_blk, v_blk = p["sk"][rs], p["sv"][rs]
        k_prev = k_scr[...] if sb == 0 else p["sk"][(sb - 1) * WINDOW:sb * WINDOW]
        v_prev = v_scr[...] if sb == 0 else p["sv"][(sb - 1) * WINDOW:sb * WINDOW]
        valid = band
        if sb == 0 and prev_valid_from:
            first = jnp.where(j == 0, prev_valid_from, 0)
            valid = band & (kj >= first)
        valid = jnp.concatenate([valid] * SWA_GROUP, axis=1)
        o_parts.append(_swa_block(p["sq"][rs], jnp.concatenate([k_prev, k_blk], axis=0),
                                  jnp.concatenate([v_prev, v_blk], axis=0), bias_ref, sinks_ref, valid))
    o_swa = jnp.concatenate(o_parts, axis=0)
    k_scr[...] = p["sk"][tm - WINDOW:tm]
    v_scr[...] = p["sv"][tm - WINDOW:tm]

    x_mid, hp, topi, gate8, rank8, base = _tail(
        x, o_gla, p["gr"], o_swa, g_gla_ref[...], g_swa_ref[...], w_out_ref,
        g_ffn_ref[...], w_r_ref[...], b_r_ref[...], base_scr[...])
    base_scr[...] = base
    xmid_ref[0] = x_mid
    _store_slabs(hp_ref, hp)
    topi_ref[...] = topi
    gate_ref[...] = gate8
    rank_ref[...] = rank8
    sout_ref[0] = s_new
    kout_ref[0] = p["sk"][tm - WINDOW:tm]
    vout_ref[0] = p["sv"][tm - WINDOW:tm]
    cnt_ref[...] = base


def _full_spec(shape):
    nd = len(shape)
    return pl.BlockSpec(shape, lambda *_: (0,) * nd)


def _mixer_call(x, s0, k0, v0, base0, wts, tm, n_lead_pad, prev_valid_from, hp_rows):
    B, L, _ = x.shape
    nj = L // tm
    T = B * L
    weight_args = (wts["bias"], wts["g_mix"], wts["w_in"], wts["w_a_up"], wts["b_a"], wts["g_gla"],
                   wts["g_swa"], wts["w_out"], wts["g_ffn"], wts["w_r"], wts["b_r"])
    in_specs = [
        pl.BlockSpec(memory_space=pltpu.SMEM),
        pl.BlockSpec((1, tm, D_MODEL), lambda b, j: (b, j, 0)),
        _full_spec(s0.shape), _full_spec(k0.shape), _full_spec(v0.shape), _full_spec(base0.shape),
    ] + [_full_spec(w.shape) for w in weight_args]
    tok_spec = pl.BlockSpec((8, tm), lambda b, j: (0, b * nj + j))
    out_specs = [
        pl.BlockSpec((1, tm, D_MODEL), lambda b, j: (b, j, 0)),
        pl.BlockSpec((tm * SLAB, LANES), lambda b, j: (b * nj + j, 0)),
        tok_spec, tok_spec, tok_spec,
        pl.BlockSpec((1, GLA_QK, GLA_V), lambda b, j: (b, 0, 0)),
        pl.BlockSpec((1, WINDOW, SWA_KV), lambda b, j: (b, 0, 0)),
        pl.BlockSpec((1, WINDOW, SWA_KV), lambda b, j: (b, 0, 0)),
        _full_spec((N_EXPERTS, LANES)),
    ]
    out_shape = [
        jax.ShapeDtypeStruct((B, L, D_MODEL), F32),
        jax.ShapeDtypeStruct((hp_rows * SLAB, LANES), jnp.int32),
        jax.ShapeDtypeStruct((8, T), jnp.int32),
        jax.ShapeDtypeStruct((8, T), F32),
        jax.ShapeDtypeStruct((8, T), jnp.int32),
        jax.ShapeDtypeStruct((B, GLA_QK, GLA_V), F32),
        jax.ShapeDtypeStruct((B, WINDOW, SWA_KV), F32),
        jax.ShapeDtypeStruct((B, WINDOW, SWA_KV), F32),
        jax.ShapeDtypeStruct((N_EXPERTS, LANES), F32),
    ]
    kern = functools.partial(_mixer_kernel, n_lead_pad=n_lead_pad, prev_valid_from=prev_valid_from)
    return pl.pallas_call(
        kern,
        grid=(B, nj),
        in_specs=in_specs,
        out_specs=out_specs,
        out_shape=out_shape,
        scratch_shapes=[pltpu.VMEM((GLA_QK, GLA_V), F32), pltpu.VMEM((WINDOW, SWA_KV), F32),
                        pltpu.VMEM((WINDOW, SWA_KV), F32), pltpu.VMEM((N_EXPERTS, LANES), F32)],
        compiler_params=pltpu.CompilerParams(dimension_semantics=("arbitrary", "arbitrary"),
                                             vmem_limit_bytes=VMEM_LIMIT),
        name="mixer",
    )(wts["sinks"], x, s0, k0, v0, base0, *weight_args)


def _decode_kernel(sinks_ref, x_ref, st_ref, ck_ref, cv_ref, base0_ref, bias_ref,
                   g_mix_ref, w_in_ref, w_a_up_ref, b_a_ref, g_gla_ref, g_swa_ref, w_out_ref,
                   g_ffn_ref, w_r_ref, b_r_ref,
                   xmid_ref, hp_ref, topi_ref, gate_ref, rank_ref, sto_ref, cko_ref, cvo_ref, cnt_ref,
                   at_scr, kt_scr, qt_scr, gv_scr, gr_scr, sq_scr, sk_scr, sv_scr, og_scr, os_scr):
    i = pl.program_id(0)
    n_seq = x_ref.shape[0]

    @pl.when(i == 0)
    def _():
        p = _project(x_ref[...], g_mix_ref[...], w_in_ref, w_a_up_ref[...], b_a_ref[...])
        at_scr[...] = jnp.transpose(jnp.exp(p["log_a"]))
        kt_scr[...] = jnp.transpose(p["gk"])
        qt_scr[...] = jnp.transpose(p["gq"])
        gv_scr[...] = p["gv"]
        gr_scr[...] = p["gr"]
        sq_scr[...] = p["sq"]
        sk_scr[...] = p["sk"]
        sv_scr[...] = p["sv"]

    lane_seq = _iota((GLA_QK, n_seq), 1)
    half = _iota((1, LANES), 1) < SWA_HEAD_DIM
    row_id = _iota((WINDOW, SWA_KV), 0)
    head_diag = (_iota((16, SWA_Q), 1) // SWA_HEAD_DIM) == _iota((16, SWA_Q), 0)
    sink_col = jnp.concatenate(
        [jnp.full((1, 1), sinks_ref[h], F32) for h in range(SWA_HEADS)] + [jnp.zeros((8, 1), F32)], axis=0)

    def per_seq(sl, carry):
        s = i * DEC_SB + sl
        sel = lane_seq == s
        a_c = jnp.sum(jnp.where(sel, at_scr[...], 0.0), axis=1, keepdims=True)
        k_c = jnp.sum(jnp.where(sel, kt_scr[...], 0.0), axis=1, keepdims=True)
        q_c = jnp.sum(jnp.where(sel, qt_scr[...], 0.0), axis=1, keepdims=True)
        st = st_ref[sl].reshape(GLA_QK, GLA_DV)
        v_row = gv_scr[pl.ds(s, 1), :]
        v_b = jnp.concatenate(
            [jnp.broadcast_to(v_row[:, h * GLA_DV:(h + 1) * GLA_DV], (GLA_DK, GLA_DV))
             for h in range(GLA_HEADS)], axis=0)
        st_new = a_c * st + k_c * v_b
        sto_ref[sl] = st_new.reshape(GLA_HEADS, GLA_DK, GLA_DV)
        t = q_c * st_new
        og_scr[pl.ds(s, 1), :] = jnp.concatenate(
            [jnp.sum(t[h * GLA_DK:(h + 1) * GLA_DK], axis=0, keepdims=True) for h in range(GLA_HEADS)],
            axis=1)

        k_new = sk_scr[pl.ds(s, 1), :]
        v_new = sv_scr[pl.ds(s, 1), :]
        kn = jnp.where(row_id == WINDOW - 1, k_new, pltpu.roll(ck_ref[sl], WINDOW - 1, 0))
        vn = jnp.where(row_id == WINDOW - 1, v_new, pltpu.roll(cv_ref[sl], WINDOW - 1, 0))
        cko_ref[sl] = kn
        cvo_ref[sl] = vn
        kr, vr = pltpu.roll(kn, SWA_HEAD_DIM, 1), pltpu.roll(vn, SWA_HEAD_DIM, 1)
        k0, k1 = jnp.where(half, kn, kr), jnp.where(half, kr, kn)
        v0, v1 = jnp.where(half, vn, vr), jnp.where(half, vr, vn)
        kw = jnp.concatenate([k0, k0, k1, k1], axis=1).astype(BF16)
        vw = jnp.concatenate([v0, v0, v1, v1], axis=1).astype(BF16)
        q_row = sq_scr[pl.ds(s, 1), :]
        qm = jnp.where(head_diag, jnp.broadcast_to(q_row, (16, SWA_Q)), 0.0).astype(BF16)
        sc = _dot_nt(qm, kw) * (SWA_HEAD_DIM ** -0.5) + bias_ref[...]
        m = jnp.maximum(jnp.max(sc, axis=1, keepdims=True), sink_col)
        pr = jnp.exp(sc - m)
        inv = 1.0 / (jnp.sum(pr, axis=1, keepdims=True) + jnp.exp(sink_col - m))
        ow = _dot(pr.astype(BF16), vw) * inv
        os_scr[pl.ds(s, 1), :] = jnp.sum(jnp.where(head_diag, ow, 0.0), axis=0, keepdims=True)
        return carry

    lax.fori_loop(0, DEC_SB, per_seq, 0, unroll=2)

    @pl.when(i == pl.num_programs(0) - 1)
    def _():
        x_mid, hp, topi, gate8, rank8, base = _tail(
            x_ref[...], og_scr[...], gr_scr[...], os_scr[...], g_gla_ref[...], g_swa_ref[...],
            w_out_ref, g_ffn_ref[...], w_r_ref[...], b_r_ref[...], base0_ref[...])
        xmid_ref[...] = x_mid
        _store_slabs(hp_ref, hp)
        topi_ref[...] = topi
        gate_ref[...] = gate8
        rank_ref[...] = rank8
        cnt_ref[...] = base


def _decode_call(xs, state, ck, cv, base0, bias_dec, wts):
    n_seq = xs.shape[0]
    nb = n_seq // DEC_SB
    weight_args = (wts["g_mix"], wts["w_in"], wts["w_a_up"], wts["b_a"], wts["g_gla"],
                   wts["g_swa"], wts["w_out"], wts["g_ffn"], wts["w_r"], wts["b_r"])
    in_specs = [
        pl.BlockSpec(memory_space=pltpu.SMEM),
        _full_spec(xs.shape),
        pl.BlockSpec((DEC_SB, GLA_HEADS, GLA_DK, GLA_DV), lambda i: (i, 0, 0, 0)),
        pl.BlockSpec((DEC_SB, WINDOW, SWA_KV), lambda i: (i, 0, 0)),
        pl.BlockSpec((DEC_SB, WINDOW, SWA_KV), lambda i: (i, 0, 0)),
        _full_spec(base0.shape), _full_spec(bias_dec.shape),
    ] + [_full_spec(w.shape) for w in weight_args]
    out_specs = [
        _full_spec((n_seq, D_MODEL)),
        _full_spec((n_seq * SLAB, LANES)),
        _full_spec((8, n_seq)), _full_spec((8, n_seq)), _full_spec((8, n_seq)),
        pl.BlockSpec((DEC_SB, GLA_HEADS, GLA_DK, GLA_DV), lambda i: (i, 0, 0, 0)),
        pl.BlockSpec((DEC_SB, WINDOW, SWA_KV), lambda i: (i, 0, 0)),
        pl.BlockSpec((DEC_SB, WINDOW, SWA_KV), lambda i: (i, 0, 0)),
        _full_spec((N_EXPERTS, LANES)),
    ]
    out_shape = [
        jax.ShapeDtypeStruct((n_seq, D_MODEL), F32),
        jax.ShapeDtypeStruct((n_seq * SLAB, LANES), jnp.int32),
        jax.ShapeDtypeStruct((8, n_seq), jnp.int32),
        jax.ShapeDtypeStruct((8, n_seq), F32),
        jax.ShapeDtypeStruct((8, n_seq), jnp.int32),
        jax.ShapeDtypeStruct(state.shape, F32),
        jax.ShapeDtypeStruct(ck.shape, F32),
        jax.ShapeDtypeStruct(cv.shape, F32),
        jax.ShapeDtypeStruct((N_EXPERTS, LANES), F32),
    ]
    scratch = [pltpu.VMEM((GLA_QK, n_seq), F32)] * 3 + [
        pltpu.VMEM((n_seq, GLA_V), F32), pltpu.VMEM((n_seq, GLA_V), F32), pltpu.VMEM((n_seq, SWA_Q), F32),
        pltpu.VMEM((n_seq, SWA_KV), F32), pltpu.VMEM((n_seq, SWA_KV), F32),
        pltpu.VMEM((n_seq, GLA_V), F32), pltpu.VMEM((n_seq, SWA_Q), F32)]
    return pl.pallas_call(
        _decode_kernel,
        grid=(nb,),
        in_specs=in_specs,
        out_specs=out_specs,
        out_shape=out_shape,
        scratch_shapes=scratch,
        compiler_params=pltpu.CompilerParams(dimension_semantics=("arbitrary",),
                                             vmem_limit_bytes=VMEM_LIMIT),
        name="decode",
    )(wts["sinks"], xs, state, ck, cv, base0, bias_dec, *weight_args)


SC_CORES = 2
SC_SUBCORES = 16
SC_WORKERS = SC_CORES * SC_SUBCORES
SC_SCATTER_ROWS = 32
SC_GATHER_ROWS = 24


def _sc_mesh():
    return plsc.VectorSubcoreMesh(core_axis_name="c", subcore_axis_name="s")


def _sc_worker_id():
    return lax.axis_index("s") * SC_CORES + lax.axis_index("c")


def _sc_scatter_rows(src_p, src_s, idx_p, idx_s, n_out):
    rows = SC_SCATTER_ROWS
    n_chunks = idx_p.shape[0] // SC_WORKERS
    n_s, _, rows_s = idx_s.shape
    assert n_chunks * SC_WORKERS == idx_p.shape[0] and n_chunks % 2 == 0 and n_s <= SC_WORKERS

    @functools.partial(
        pl.kernel, mesh=_sc_mesh(),
        out_type=jax.ShapeDtypeStruct((n_out, SLAB, LANES), jnp.int32),
        scratch_types=[pltpu.VMEM((2, TOP_K, rows), jnp.int32), pltpu.VMEM((2, rows, SLAB, LANES), jnp.int32),
                       pltpu.VMEM((TOP_K, rows_s), jnp.int32), pltpu.VMEM((rows_s, SLAB, LANES), jnp.int32),
                       pltpu.SemaphoreType.DMA((2,)), pltpu.SemaphoreType.DMA((2,))])
    def scatter_rows(srcp_hbm, srcs_hbm, idxp_hbm, idxs_hbm, out_hbm, idx_v, rows_v, idxs_v, rowss_v, lsem, ssem):
        wid = _sc_worker_id()

        def loads(c, b):
            g = wid * n_chunks + c
            return (pltpu.make_async_copy(idxp_hbm.at[g], idx_v.at[b], lsem.at[b]),
                    pltpu.make_async_copy(srcp_hbm.at[pl.ds(pl.multiple_of(g * rows, 8), rows)], rows_v.at[b],
                                          lsem.at[b]))

        def scatters(b):
            return [pltpu.make_async_copy(rows_v.at[b], out_hbm.at[idx_v.at[b, k]], ssem.at[b])
                    for k in range(TOP_K)]

        for d in loads(0, 0):
            d.start()

        @pl.loop(0, n_chunks, step=2)
        def _(c0):
            for b in range(2):
                c = c0 + b
                for d in loads(c, b):
                    d.wait()

                @pl.when(c >= 1)
                def _():
                    for d in scatters(1 - b):
                        d.wait()

                @pl.when(c + 1 < n_chunks)
                def _():
                    for d in loads(c + 1, 1 - b):
                        d.start()

                for d in scatters(b):
                    d.start()

        for d in scatters((n_chunks - 1) % 2):
            d.wait()

        @pl.when(wid < n_s)
        def _():
            pltpu.sync_copy(idxs_hbm.at[wid], idxs_v)
            pltpu.sync_copy(srcs_hbm.at[pl.ds(pl.multiple_of(wid * rows_s, 8), rows_s)], rowss_v)
            for k in range(TOP_K):
                pltpu.sync_copy(rowss_v, out_hbm.at[idxs_v.at[k]])

    return scatter_rows(src_p, src_s, idx_p, idx_s)


def _sc_gather_rows(src3, idx2):
    rows = SC_GATHER_ROWS
    n_chunks = idx2.shape[0] // SC_WORKERS
    assert n_chunks * SC_WORKERS == idx2.shape[0] and idx2.shape[1] == rows and n_chunks % 2 == 0

    @functools.partial(
        pl.kernel, mesh=_sc_mesh(),
        out_type=jax.ShapeDtypeStruct((idx2.shape[0] * rows, SLAB, LANES), jnp.int32),
        scratch_types=[pltpu.VMEM((2, rows), jnp.int32), pltpu.VMEM((2, rows, SLAB, LANES), jnp.int32),
                       pltpu.SemaphoreType.DMA((2,)), pltpu.SemaphoreType.DMA((2,))])
    def gather_rows(src_hbm, idx_hbm, out_hbm, idx_v, rows_v, gsem, wsem):
        wid = _sc_worker_id()

        def gather(b):
            return pltpu.make_async_copy(src_hbm.at[idx_v.at[b]], rows_v.at[b], gsem.at[b])

        def write(c, b):
            base = pl.multiple_of((wid * n_chunks + c) * rows, 8)
            return pltpu.make_async_copy(rows_v.at[b], out_hbm.at[pl.ds(base, rows)], wsem.at[b])

        pltpu.sync_copy(idx_hbm.at[wid * n_chunks], idx_v.at[0])
        gather(0).start()

        @pl.loop(0, n_chunks, step=2)
        def _(c0):
            for b in range(2):
                c = c0 + b

                @pl.when(c + 1 < n_chunks)
                def _():
                    @pl.when(c >= 1)
                    def _():
                        write(c - 1, 1 - b).wait()
                    pltpu.sync_copy(idx_hbm.at[wid * n_chunks + c + 1], idx_v.at[1 - b])
                    gather(1 - b).start()

                gather(b).wait()
                write(c, b).start()

        write(n_chunks - 2, 0).wait()
        write(n_chunks - 1, 1).wait()

    return gather_rows(src3, idx2)


FF_TILE = 256


def _ffn_kernel(blk_e_ref, nused_ref, x_ref, wu_ref, bu_ref, wd_ref, bd_ref, y_ref, xbf, actbf, wu_bf, wd_bf):
    i = pl.program_id(0)
    tm = MOE_TM
    n_tiles = D_FF // FF_TILE

    @pl.when(i < nused_ref[0])
    def _():
        @pl.when((i == 0) | (blk_e_ref[i] != blk_e_ref[jnp.maximum(i - 1, 0)]))
        def _():
            wu_bf[...] = wu_ref[0].astype(BF16)
            wd_bf[...] = wd_ref[0].astype(BF16)

        for c in range(SLAB):
            lo, hi = _load_slab_chunk(x_ref, tm, c)
            xbf[:, c * LANES:(c + 1) * LANES] = lo.astype(BF16)
            xbf[:, HALF_D + c * LANES:HALF_D + (c + 1) * LANES] = hi.astype(BF16)
        for n in range(n_tiles):
            gc = slice(n * FF_TILE, (n + 1) * FF_TILE)
            lc = slice(D_FF + n * FF_TILE, D_FF + (n + 1) * FF_TILE)
            g = jnp.minimum(_dot(xbf[...], wu_bf[:, gc]) + bu_ref[0, :, gc], SWIGLU_LIMIT)
            lin = jnp.clip(_dot(xbf[...], wu_bf[:, lc]) + bu_ref[0, :, lc], -SWIGLU_LIMIT, SWIGLU_LIMIT)
            actbf[:, gc] = (g * jax.nn.sigmoid(SWIGLU_ALPHA * g) * (lin + 1.0)).astype(BF16)
        per_tile = FF_TILE // LANES
        for n in range(n_tiles // 2):
            yl = slice(n * FF_TILE, (n + 1) * FF_TILE)
            yh = slice(HALF_D + n * FF_TILE, HALF_D + (n + 1) * FF_TILE)
            y_lo = _dot(actbf[...], wd_bf[:, yl]) + bd_ref[0, :, yl]
            y_hi = _dot(actbf[...], wd_bf[:, yh]) + bd_ref[0, :, yh]
            for c in range(per_tile):
                sl = slice(c * LANES, (c + 1) * LANES)
                y_ref[pl.ds(n * per_tile + c, tm, stride=SLAB), :] = _pack_pair(y_lo[:, sl], y_hi[:, sl])

    @pl.when(i >= nused_ref[0])
    def _():
        y_ref[...] = jnp.zeros_like(y_ref)


def _ffn_call(blk_e, nused, xs2, w_up, b_up, w_down, b_down):
    n_blocks = blk_e.shape[0]
    tm = MOE_TM
    row_blk = pl.BlockSpec((tm * SLAB, LANES), lambda i, be, nu: (i, 0))
    grid_spec = pltpu.PrefetchScalarGridSpec(
        num_scalar_prefetch=2,
        grid=(n_blocks,),
        in_specs=[
            row_blk,
            pl.BlockSpec((1, D_MODEL, 2 * D_FF), lambda i, be, nu: (be[i], 0, 0)),
            pl.BlockSpec((1, 1, 2 * D_FF), lambda i, be, nu: (be[i], 0, 0)),
            pl.BlockSpec((1, D_FF, D_MODEL), lambda i, be, nu: (be[i], 0, 0)),
            pl.BlockSpec((1, 1, D_MODEL), lambda i, be, nu: (be[i], 0, 0)),
        ],
        out_specs=row_blk,
        scratch_shapes=[pltpu.VMEM((tm, D_MODEL), BF16), pltpu.VMEM((tm, D_FF), BF16),
                        pltpu.VMEM((D_MODEL, 2 * D_FF), BF16), pltpu.VMEM((D_FF, D_MODEL), BF16)],
    )
    return pl.pallas_call(
        _ffn_kernel,
        grid_spec=grid_spec,
        out_shape=jax.ShapeDtypeStruct((n_blocks * tm * SLAB, LANES), jnp.int32),
        compiler_params=pltpu.CompilerParams(dimension_semantics=("arbitrary",),
                                             vmem_limit_bytes=VMEM_LIMIT),
        name="experts",
    )(blk_e, nused, xs2, w_up, b_up.reshape(N_EXPERTS, 1, 2 * D_FF), w_down, b_down.reshape(N_EXPERTS, 1, D_MODEL))


def _combine_kernel(ys0_ref, ys1_ref, ys2_ref, ys3_ref, xmid_ref, gate_ref, g_final_ref, y_ref):
    tm = xmid_ref.shape[0]
    gts = gate_ref[...]
    lows, highs = [], []
    for c in range(SLAB):
        acc_lo = xmid_ref[:, c * LANES:(c + 1) * LANES]
        acc_hi = xmid_ref[:, HALF_D + c * LANES:HALF_D + (c + 1) * LANES]
        for k, ys_ref in enumerate((ys0_ref, ys1_ref, ys2_ref, ys3_ref)):
            lo, hi = _load_slab_chunk(ys_ref, tm, c)
            acc_lo = acc_lo + lo * gts[:, k:k + 1]
            acc_hi = acc_hi + hi * gts[:, k:k + 1]
        lows.append(acc_lo)
        highs.append(acc_hi)
    y_ref[...] = _rms(jnp.concatenate(lows + highs, axis=1), g_final_ref[...])


def _combine_call(ys4, t_stride, row0, x_mid, gates, g_final, tm):
    T = x_mid.shape[0]
    blk0 = row0 // tm
    per_k = t_stride // tm
    assert per_k * tm == t_stride and blk0 * tm == row0

    def ys_spec(k):
        return pl.BlockSpec((tm * SLAB, LANES), lambda i: (k * per_k + blk0 + i, 0))

    return pl.pallas_call(
        _combine_kernel,
        grid=(T // tm,),
        in_specs=[
            ys_spec(0), ys_spec(1), ys_spec(2), ys_spec(3),
            pl.BlockSpec((tm, D_MODEL), lambda i: (i, 0)),
            pl.BlockSpec((tm, TOP_K), lambda i: (i, 0)),
            _full_spec((1, D_MODEL)),
        ],
        out_specs=pl.BlockSpec((tm, D_MODEL), lambda i: (i, 0)),
        out_shape=jax.ShapeDtypeStruct((T, D_MODEL), F32),
        compiler_params=pltpu.CompilerParams(dimension_semantics=("arbitrary",),
                                             vmem_limit_bytes=VMEM_LIMIT),
        name="combine",
    )(ys4, ys4, ys4, ys4, x_mid, gates, g_final)


def _t5_bucket(dist):
    n = jnp.maximum(dist, 0)
    max_exact = NUM_BUCKETS // 2
    nf = jnp.maximum(n, 1).astype(F32)
    large = max_exact + (jnp.log(nf / max_exact) / math.log(MAX_DISTANCE / max_exact)
                         * (NUM_BUCKETS - max_exact)).astype(jnp.int32)
    large = jnp.minimum(large, NUM_BUCKETS - 1)
    return jnp.where(n < max_exact, n, large)


def kernel(x_prompt, x_sample, state_gla, cache_swa_k, cache_swa_v, meta_tokens, rel_bias_table,
           g_mix, w_in, w_a_up, b_a, g_gla_out, g_swa_out, attn_sinks, w_out,
           g_ffn, w_router, b_router, w_up, b_up, w_down, b_down, g_final):
    assert g_mix.shape[0] == 1, "single-layer trunk"
    B, L, _ = x_prompt.shape
    n_seq = x_sample.shape[0]
    TP = B * L
    T_all = TP + n_seq

    wi = w_in[0]
    sizes = (GLA_QK, GLA_QK, GLA_V, GLA_V, GLA_LOWRANK, SWA_Q, SWA_KV, SWA_KV)
    offs = [0]
    for s in sizes:
        offs.append(offs[-1] + s)
    seg = [wi[:, offs[n]:offs[n + 1]] for n in range(8)]
    w_in_r = jnp.concatenate(
        seg[0:4] + seg[5:8] + [seg[4], jnp.zeros((D_MODEL, LANES - GLA_LOWRANK), F32)], axis=1).astype(BF16)
    w_a_pad = jnp.concatenate([w_a_up[0], jnp.zeros((LANES - GLA_LOWRANK, GLA_QK), F32)], axis=0).astype(BF16)
    wr_t = jnp.transpose(w_router[0])
    wr_hi = wr_t.astype(BF16)
    wr_lo = (wr_t - wr_hi.astype(F32)).astype(BF16)
    qi = jnp.arange(WINDOW)[:, None]
    kj = jnp.arange(2 * WINDOW)[None, :]
    buckets = jnp.arange(NUM_BUCKETS)
    table = rel_bias_table.astype(F32)
    oh_p = (_t5_bucket(qi - kj + WINDOW)[..., None] == buckets).astype(F32)
    bias_p = jnp.einsum("qkb,bh->hkq", oh_p, table, precision=lax.Precision.HIGHEST)
    bias_p = bias_p.reshape(SWA_KV_HEADS, SWA_GROUP, 2 * WINDOW, WINDOW).transpose(0, 2, 1, 3)
    bias_p = bias_p.reshape(SWA_KV_HEADS, 2 * WINDOW, SWA_GROUP * WINDOW)
    oh_d = (_t5_bucket(WINDOW - 1 - jnp.arange(WINDOW))[:, None] == buckets).astype(F32)
    bias_d = jnp.einsum("rb,bh->hr", oh_d, table, precision=lax.Precision.HIGHEST)
    bias_d = jnp.concatenate([bias_d, jnp.zeros((8, WINDOW), F32)], axis=0)
    wts = dict(
        sinks=attn_sinks[0].astype(F32), bias=bias_p,
        g_mix=g_mix[0][None], w_in=w_in_r, w_a_up=w_a_pad, b_a=b_a[0][None],
        g_gla=g_gla_out[0][None], g_swa=g_swa_out[0][None], w_out=w_out[0].astype(BF16),
        g_ffn=g_ffn[0][None], w_r=jnp.concatenate([wr_hi, wr_lo], axis=0), b_r=b_router[0][:, None],
    )

    x_pre = jnp.concatenate([jnp.zeros((WINDOW - N_META, D_MODEL), F32), meta_tokens.astype(F32)], axis=0)[None]
    zeros_s = jnp.zeros((GLA_QK, GLA_V), F32)
    zeros_kv = jnp.zeros((WINDOW, SWA_KV), F32)
    zeros_b = jnp.zeros((N_EXPERTS, LANES), F32)
    pre = _mixer_call(x_pre, zeros_s, zeros_kv, zeros_kv, zeros_b, wts, WINDOW, WINDOW - N_META, 0, WINDOW)
    s_meta, k_meta, v_meta = pre[5][0], pre[6][0], pre[7][0]

    (xmid_p, hp_p, topi_p, gate_p, rank_p, s_p, k_p, v_p, cnt_p) = _mixer_call(
        x_prompt, s_meta, k_meta, v_meta, zeros_b, wts, MIX_TM, 0, WINDOW - N_META, TP)

    (xmid_s, hp_s, topi_s, gate_s, rank_s, st_s, ck_s, cv_s, cnt_all) = _decode_call(
        x_sample[:, 0], state_gla[0], cache_swa_k[0].reshape(n_seq, WINDOW, SWA_KV),
        cache_swa_v[0].reshape(n_seq, WINDOW, SWA_KV), cnt_p, bias_d, wts)

    tm = MOE_TM
    n_slots = T_all * TOP_K
    n_blocks = -(-n_slots // tm) + N_EXPERTS
    top_e = jnp.concatenate([topi_p[:TOP_K], topi_s[:TOP_K]], axis=1)
    rank = jnp.concatenate([rank_p[:TOP_K], rank_s[:TOP_K]], axis=1)
    counts = cnt_all[:, 0].astype(jnp.int32)
    padded = (counts + tm - 1) // tm * tm
    pad_end = jnp.cumsum(padded)
    pad_start = pad_end - padded
    e_ids = jnp.arange(N_EXPERTS, dtype=jnp.int32)
    dest = jnp.sum(jnp.where(top_e[..., None] == e_ids, pad_start, 0), axis=-1) + rank
    n_pad = n_blocks * tm
    blk_e = jnp.minimum(jnp.sum(pad_end[None] <= (jnp.arange(n_blocks, dtype=jnp.int32) * tm)[:, None], axis=1),
                        N_EXPERTS - 1).astype(jnp.int32)
    nused = (pad_end[-1] // tm).astype(jnp.int32).reshape(1)

    sample_rows = 8
    idx_p = dest[:, :TP].reshape(TOP_K, TP // SC_SCATTER_ROWS, SC_SCATTER_ROWS).transpose(1, 0, 2)
    idx_s = dest[:, TP:].reshape(TOP_K, n_seq // sample_rows, sample_rows).transpose(1, 0, 2)
    xs3 = _sc_scatter_rows(hp_p.reshape(TP, SLAB, LANES), hp_s.reshape(n_seq, SLAB, LANES), idx_p, idx_s, n_pad)
    ys2 = _ffn_call(blk_e, nused, xs3.reshape(-1, LANES), w_up[0], b_up[0], w_down[0], b_down[0])
    unit = math.lcm(2 * SC_WORKERS * SC_GATHER_ROWS // TOP_K, MIX_TM)
    t_stride = -(-T_all // unit) * unit
    filler = jnp.arange(TOP_K * (t_stride - T_all), dtype=jnp.int32).reshape(TOP_K, t_stride - T_all)
    slot_src = jnp.concatenate([dest, filler], axis=1)
    slot_src = slot_src.reshape(TOP_K * t_stride // SC_GATHER_ROWS, SC_GATHER_ROWS)
    ys4 = _sc_gather_rows(ys2.reshape(-1, SLAB, LANES), slot_src).reshape(-1, LANES)

    gates = jnp.transpose(jnp.concatenate([gate_p[:TOP_K], gate_s[:TOP_K]], axis=1))
    gf = g_final[None]
    y_p = _combine_call(ys4, t_stride, 0, xmid_p.reshape(TP, D_MODEL), gates[:TP], gf, MIX_TM)
    y_s = _combine_call(ys4, t_stride, TP, xmid_s, gates[TP:], gf, n_seq)

    s_heads = jnp.stack([s_p[:, h * GLA_DK:(h + 1) * GLA_DK, h * GLA_DV:(h + 1) * GLA_DV]
                         for h in range(GLA_HEADS)], axis=1)
    return (y_p.reshape(B, L, D_MODEL), y_s.reshape(n_seq, 1, D_MODEL), s_heads[None],
            k_p.reshape(1, B, WINDOW, SWA_KV_HEADS, SWA_HEAD_DIM),
            v_p.reshape(1, B, WINDOW, SWA_KV_HEADS, SWA_HEAD_DIM),
            st_s[None], ck_s.reshape(1, n_seq, WINDOW, SWA_KV_HEADS, SWA_HEAD_DIM),
            cv_s.reshape(1, n_seq, WINDOW, SWA_KV_HEADS, SWA_HEAD_DIM))
```

```python
import functools
import math

import jax
import jax.numpy as jnp
from jax import lax
from jax.experimental import pallas as pl
from jax.experimental.pallas import tpu as pltpu
from jax.experimental.pallas import tpu_sc as plsc

D_MODEL = 1024
N_META = 16
GLA_HEADS = 4
GLA_DK = 64
GLA_DV = 128
GLA_LOWRANK = 16
GLA_GATE_TAU = 16.0
GLA_CHUNK = 64
SWA_HEADS = 8
SWA_KV_HEADS = 2
SWA_HEAD_DIM = 64
SWA_GROUP = SWA_HEADS // SWA_KV_HEADS
WINDOW = 128
NUM_BUCKETS = 32
MAX_DISTANCE = 128
N_EXPERTS = 32
TOP_K = 4
D_FF = 1024
SWIGLU_ALPHA = 1.702
SWIGLU_LIMIT = 7.0
RMS_EPS = 1e-6

GLA_QK = GLA_HEADS * GLA_DK
GLA_V = GLA_HEADS * GLA_DV
SWA_Q = SWA_HEADS * SWA_HEAD_DIM
SWA_KV = SWA_KV_HEADS * SWA_HEAD_DIM
LANES = 128
C_GQ, C_GK, C_GV, C_GR = 0, GLA_QK, 2 * GLA_QK, 2 * GLA_QK + GLA_V
C_SQ = C_GR + GLA_V
C_SK = C_SQ + SWA_Q
C_SV = C_SK + SWA_KV
C_GA = C_SV + SWA_KV
D_PROJ = C_GA + LANES

MIX_TM = 512
MOE_TM = 512
DEC_SB = 16
VMEM_LIMIT = 56 * 1024 * 1024

F32 = jnp.float32
BF16 = jnp.bfloat16
NEG_INF = float("-inf")


def _dot(a, b):
    return jnp.dot(a, b, preferred_element_type=F32)


def _dot_nt(a, b):
    return lax.dot_general(a, b, (((1,), (1,)), ((), ())), preferred_element_type=F32)


def _split3(x):
    hi = x.astype(BF16)
    r1 = x - hi.astype(F32)
    mid = r1.astype(BF16)
    lo = (r1 - mid.astype(F32)).astype(BF16)
    return hi, mid, lo


def _rms(x, g):
    return x * lax.rsqrt(jnp.mean(x * x, axis=-1, keepdims=True) + RMS_EPS) * g


def _iota(shape, dim):
    return lax.broadcasted_iota(jnp.int32, shape, dim)


HALF_D = D_MODEL // 2
SLAB = HALF_D // LANES


def _pack_pair(lo, hi):
    bl = lax.bitcast_convert_type(lo.astype(BF16).astype(F32), jnp.uint32)
    bh = lax.bitcast_convert_type(hi.astype(BF16).astype(F32), jnp.uint32)
    return lax.bitcast_convert_type(bh | lax.shift_right_logical(bl, jnp.uint32(16)), jnp.int32)


def _unpack_pair(w):
    u = lax.bitcast_convert_type(w, jnp.uint32)
    lo = lax.bitcast_convert_type(lax.shift_left(u, jnp.uint32(16)), F32)
    hi = lax.bitcast_convert_type(u & jnp.uint32(0xFFFF0000), F32)
    return lo, hi


def _store_slabs(ref, x):
    rows = x.shape[0]
    for c in range(SLAB):
        sl = slice(c * LANES, (c + 1) * LANES)
        ref[pl.ds(c, rows, stride=SLAB), :] = _pack_pair(x[:, sl], x[:, HALF_D + c * LANES:HALF_D + (c + 1) * LANES])


def _load_slab_chunk(ref, rows, c):
    return _unpack_pair(ref[pl.ds(c, rows, stride=SLAB), :])


def _project(x, g_mix, w_in_ref, w_a_up, b_a):
    h = _rms(x, g_mix).astype(BF16)

    def cols(lo, width):
        return _dot(h, w_in_ref[:, lo:lo + width])

    ga = cols(C_GA, LANES).astype(BF16)
    z = _dot(ga, w_a_up) + b_a
    log_a = -(jnp.maximum(-z, 0.0) + jnp.log1p(jnp.exp(-jnp.abs(z)))) / GLA_GATE_TAU
    gqk = cols(C_GQ, 2 * GLA_QK)
    gv = cols(C_GV, GLA_V)
    swa = cols(C_SQ, SWA_Q + 2 * SWA_KV)
    gr = cols(C_GR, GLA_V)
    return dict(
        gq=gqk[:, :GLA_QK] * (GLA_DK ** -0.5),
        gk=gqk[:, GLA_QK:],
        gv=gv,
        gr=gr,
        sq=swa[:, :SWA_Q],
        sk=swa[:, SWA_Q:SWA_Q + SWA_KV],
        sv=swa[:, SWA_Q + SWA_KV:],
        log_a=log_a,
    )


def _tail(x, o_gla, gr, o_swa, g_gla_out, g_swa_out, w_out_ref, g_ffn, w_r, b_r, base):
    tm = x.shape[0]
    gate = gr * jax.nn.sigmoid(gr)
    parts = []
    for h in range(GLA_HEADS):
        sl = slice(h * GLA_DV, (h + 1) * GLA_DV)
        parts.append(_rms(o_gla[:, sl], g_gla_out) * gate[:, sl])
    og = jnp.concatenate(parts, axis=1).astype(BF16)
    os_ = _rms(o_swa, g_swa_out).astype(BF16)
    x_mid = x + _dot(og, w_out_ref[0:GLA_V]) + _dot(os_, w_out_ref[GLA_V:GLA_V + SWA_Q])
    hp = _rms(x_mid, g_ffn)

    h1 = hp.astype(BF16)
    h2 = (hp - h1.astype(F32)).astype(BF16)
    la = _dot_nt(w_r, h1)
    lb = _dot_nt(w_r[0:N_EXPERTS], h2)
    logits = la[0:N_EXPERTS] + la[N_EXPERTS:2 * N_EXPERTS] + lb + b_r

    eidx = _iota((N_EXPERTS, tm), 0)
    vals, idxs, onehots = [], [], []
    l = logits
    for _ in range(TOP_K):
        m = jnp.max(l, axis=0, keepdims=True)
        sel = jnp.min(jnp.where(l == m, eidx, N_EXPERTS), axis=0, keepdims=True)
        oh = eidx == sel
        l = jnp.where(oh, NEG_INF, l)
        vals.append(m)
        idxs.append(sel)
        onehots.append(oh)
    es = [jnp.exp(v - vals[0]) for v in vals]
    denom = es[0] + es[1] + es[2] + es[3]
    gates = [e / denom for e in es]

    ohf = jnp.concatenate([oh.astype(F32) for oh in onehots], axis=0)
    upper = (_iota((tm, tm), 0) < _iota((tm, tm), 1)).astype(BF16)
    prefix = _dot(ohf.astype(BF16), upper)
    ranks = []
    for k in range(TOP_K):
        sl = slice(k * N_EXPERTS, (k + 1) * N_EXPERTS)
        ohk = ohf[sl]
        base_t = jnp.concatenate([base] * (tm // LANES), axis=1)
        ranks.append(jnp.sum(ohk * (prefix[sl] + base_t), axis=0, keepdims=True))
        base = base + jnp.sum(ohk, axis=1, keepdims=True)
    zi = jnp.zeros((8 - TOP_K, tm), jnp.int32)
    zf = jnp.zeros((8 - TOP_K, tm), F32)
    topi = jnp.concatenate(idxs + [zi], axis=0)
    gate8 = jnp.concatenate(gates + [zf], axis=0)
    rank8 = jnp.concatenate([r.astype(jnp.int32) for r in ranks] + [zi], axis=0)
    return x_mid, hp, topi, gate8, rank8, base


def _gla_chunks(p, row0, s_bd, n_lead_pad):
    tm = p["gq"].shape[0]
    nch = tm // GLA_CHUNK
    log_a = p["log_a"]
    if n_lead_pad:
        rows = row0 + _iota((tm, GLA_QK), 0)
        log_a = jnp.where(rows >= n_lead_pad, log_a, 0.0)
    ri, ci = _iota((tm, tm), 0), _iota((tm, tm), 1)
    tril = ((ri >= ci) & (ri // GLA_CHUNK == ci // GLA_CHUNK)).astype(BF16)
    hi, mid, lo = _split3(log_a)
    b_all = _dot(tril, hi) + _dot(tril, mid) + _dot(tril, lo)

    c64 = GLA_CHUNK
    kk_mask = (_iota((GLA_QK, GLA_QK), 0) // c64) == (_iota((GLA_QK, GLA_QK), 1) // GLA_DK)
    vv_mask = (_iota((GLA_QK, GLA_V), 0) // c64) == (_iota((GLA_QK, GLA_V), 1) // GLA_DV)
    ss_mask = (_iota((GLA_QK, GLA_V), 0) // GLA_DK) == (_iota((GLA_QK, GLA_V), 1) // GLA_DV)
    causal = (_iota((c64, GLA_QK), 0) >= (_iota((c64, GLA_QK), 1) % c64)).astype(F32)
    zpad_k = jnp.zeros((LANES - c64, GLA_QK), F32)
    zpad_v = jnp.zeros((LANES - c64, GLA_V), BF16)

    outs = []
    for c in range(nch):
        rs = slice(c * c64, (c + 1) * c64)
        b = b_all[rs]
        q, k, v = p["gq"][rs], p["gk"][rs], p["gv"][rs]
        b_last = b[c64 - 1:c64]
        qt = (q * jnp.exp(b)).astype(BF16)
        kt = k * jnp.exp(-b)
        kd = k * jnp.exp(b_last - b)
        vb = v.astype(BF16)
        k_bd = jnp.where(kk_mask, jnp.concatenate([kt] * GLA_HEADS, axis=0), 0.0).astype(BF16)
        a = (_dot_nt(qt, k_bd) * causal).astype(BF16)
        v_bd = jnp.where(vv_mask, jnp.concatenate([vb] * GLA_HEADS, axis=0), jnp.zeros((), BF16))
        outs.append(_dot(qt, s_bd.astype(BF16)) + _dot(a, v_bd))
        kd_t = jnp.transpose(jnp.concatenate([kd, zpad_k], axis=0)).astype(BF16)
        upd = _dot(kd_t, jnp.concatenate([vb, zpad_v], axis=0))
        decay = jnp.exp(jnp.transpose(jnp.broadcast_to(b_last, (LANES, GLA_QK))))
        s_bd = s_bd * jnp.concatenate([decay] * GLA_HEADS, axis=1) + jnp.where(ss_mask, upd, 0.0)
    return jnp.concatenate(outs, axis=0), s_bd


def _swa_block(sq, kcat, vcat, bias_ref, sinks_ref, valid_t):
    half = _iota((1, LANES), 1) < SWA_HEAD_DIM
    top_rows = _iota((LANES, 1), 0) < SWA_HEAD_DIM
    k_roll = pltpu.roll(kcat, SWA_HEAD_DIM, 1)
    v_t = jnp.transpose(vcat)
    zeros_v = jnp.zeros((SWA_HEAD_DIM, 2 * WINDOW), F32)
    cols = []
    for kv in range(SWA_KV_HEADS):
        kk = jnp.where(half, kcat, k_roll) if kv == 0 else jnp.where(half, k_roll, kcat)
        q_parts = []
        for c in (2 * kv, 2 * kv + 1):
            qc = sq[:, c * LANES:(c + 1) * LANES]
            q_parts.append(jnp.where(half, qc, 0.0))
            q_parts.append(jnp.where(half, 0.0, qc))
        q_st = jnp.concatenate(q_parts, axis=0).astype(BF16)
        s = _dot_nt(kk.astype(BF16), q_st) * (SWA_HEAD_DIM ** -0.5) + bias_ref[kv]
        s = jnp.where(valid_t, s, NEG_INF)
        sink = jnp.concatenate(
            [jnp.full((1, WINDOW), sinks_ref[kv * SWA_GROUP + g], F32) for g in range(SWA_GROUP)], axis=1)
        m = jnp.maximum(jnp.max(s, axis=0, keepdims=True), sink)
        pr = jnp.exp(s - m)
        inv = 1.0 / (jnp.sum(pr, axis=0, keepdims=True) + jnp.exp(sink - m))
        pb = pr.astype(BF16)
        vk = v_t[kv * SWA_HEAD_DIM:(kv + 1) * SWA_HEAD_DIM]
        vv_t = jnp.concatenate([jnp.concatenate([vk, zeros_v], axis=1),
                                jnp.concatenate([zeros_v, vk], axis=1)], axis=0).astype(BF16)
        for pair in range(SWA_GROUP // 2):
            ce = slice(2 * pair * WINDOW, (2 * pair + 1) * WINDOW)
            co = slice((2 * pair + 1) * WINDOW, (2 * pair + 2) * WINDOW)
            p2_t = jnp.concatenate([pb[:, ce], pb[:, co]], axis=0)
            o2_t = _dot(vv_t, p2_t)
            o2_t = o2_t * jnp.where(top_rows, inv[:, ce], inv[:, co])
            cols.append(jnp.transpose(o2_t))
    return jnp.concatenate(cols, axis=1)


def _mixer_kernel(sinks_ref, x_ref, s0_ref, k0_ref, v0_ref, base0_ref, bias_ref,
                  g_mix_ref, w_in_ref, w_a_up_ref, b_a_ref, g_gla_ref, g_swa_ref, w_out_ref,
                  g_ffn_ref, w_r_ref, b_r_ref,
                  xmid_ref, hp_ref, topi_ref, gate_ref, rank_ref, sout_ref, kout_ref, vout_ref, cnt_ref,
                  s_scr, k_scr, v_scr, base_scr, *, n_lead_pad, prev_valid_from):
    b_id, j = pl.program_id(0), pl.program_id(1)
    tm = x_ref.shape[1]

    @pl.when(j == 0)
    def _():
        s_scr[...] = s0_ref[...]
        k_scr[...] = k0_ref[...]
        v_scr[...] = v0_ref[...]

    @pl.when((j == 0) & (b_id == 0))
    def _():
        base_scr[...] = base0_ref[...]

    x = x_ref[0]
    p = _project(x, g_mix_ref[...], w_in_ref, w_a_up_ref[...], b_a_ref[...])

    o_gla, s_new = _gla_chunks(p, j * tm, s_scr[...], n_lead_pad)
    s_scr[...] = s_new

    kj = _iota((2 * WINDOW, WINDOW), 0)
    qi = _iota((2 * WINDOW, WINDOW), 1)
    band = (kj > qi) & (kj <= qi + WINDOW)
    o_parts = []
    for sb in range(tm // WINDOW):
        rs = slice(sb * WINDOW, (sb + 1) * WINDOW)
        k_blk, v_blk = p["sk"][rs], p["sv"][rs]
        k_prev = k_scr[...] if sb == 0 else p["sk"][(sb - 1) * WINDOW:sb * WINDOW]
        v_prev = v_scr[...] if sb == 0 else p["sv"][(sb - 1) * WINDOW:sb * WINDOW]
        valid = band
        if sb == 0 and prev_valid_from:
            first = jnp.where(j == 0, prev_valid_from, 0)
            valid = band & (kj >= first)
        valid = jnp.concatenate([valid] * SWA_GROUP, axis=1)
        o_parts.append(_swa_block(p["sq"][rs], jnp.concatenate([k_prev, k_blk], axis=0),
                                  jnp.concatenate([v_prev, v_blk], axis=0), bias_ref, sinks_ref, valid))
    o_swa = jnp.concatenate(o_parts, axis=0)
    k_scr[...] = p["sk"][tm - WINDOW:tm]
    v_scr[...] = p["sv"][tm - WINDOW:tm]

    x_mid, hp, topi, gate8, rank8, base = _tail(
        x, o_gla, p["gr"], o_swa, g_gla_ref[...], g_swa_ref[...], w_out_ref,
        g_ffn_ref[...], w_r_ref[...], b_r_ref[...], base_scr[...])
    base_scr[...] = base
    xmid_ref[0] = x_mid
    _store_slabs(hp_ref, hp)
    topi_ref[...] = topi
    gate_ref[...] = gate8
    rank_ref[...] = rank8
    sout_ref[0] = s_new
    kout_ref[0] = p["sk"][tm - WINDOW:tm]
    vout_ref[0] = p["sv"][tm - WINDOW:tm]
    cnt_ref[...] = base


def _full_spec(shape):
    nd = len(shape)
    return pl.BlockSpec(shape, lambda *_: (0,) * nd)


def _mixer_call(x, s0, k0, v0, base0, wts, tm, n_lead_pad, prev_valid_from, hp_rows):
    B, L, _ = x.shape
    nj = L // tm
    T = B * L
    weight_args = (wts["bias"], wts["g_mix"], wts["w_in"], wts["w_a_up"], wts["b_a"], wts["g_gla"],
                   wts["g_swa"], wts["w_out"], wts["g_ffn"], wts["w_r"], wts["b_r"])
    in_specs = [
        pl.BlockSpec(memory_space=pltpu.SMEM),
        pl.BlockSpec((1, tm, D_MODEL), lambda b, j: (b, j, 0)),
        _full_spec(s0.shape), _full_spec(k0.shape), _full_spec(v0.shape), _full_spec(base0.shape),
    ] + [_full_spec(w.shape) for w in weight_args]
    tok_spec = pl.BlockSpec((8, tm), lambda b, j: (0, b * nj + j))
    out_specs = [
        pl.BlockSpec((1, tm, D_MODEL), lambda b, j: (b, j, 0)),
        pl.BlockSpec((tm * SLAB, LANES), lambda b, j: (b * nj + j, 0)),
        tok_spec, tok_spec, tok_spec,
        pl.BlockSpec((1, GLA_QK, GLA_V), lambda b, j: (b, 0, 0)),
        pl.BlockSpec((1, WINDOW, SWA_KV), lambda b, j: (b, 0, 0)),
        pl.BlockSpec((1, WINDOW, SWA_KV), lambda b, j: (b, 0, 0)),
        _full_spec((N_EXPERTS, LANES)),
    ]
    out_shape = [
        jax.ShapeDtypeStruct((B, L, D_MODEL), F32),
        jax.ShapeDtypeStruct((hp_rows * SLAB, LANES), jnp.int32),
        jax.ShapeDtypeStruct((8, T), jnp.int32),
        jax.ShapeDtypeStruct((8, T), F32),
        jax.ShapeDtypeStruct((8, T), jnp.int32),
        jax.ShapeDtypeStruct((B, GLA_QK, GLA_V), F32),
        jax.ShapeDtypeStruct((B, WINDOW, SWA_KV), F32),
        jax.ShapeDtypeStruct((B, WINDOW, SWA_KV), F32),
        jax.ShapeDtypeStruct((N_EXPERTS, LANES), F32),
    ]
    kern = functools.partial(_mixer_kernel, n_lead_pad=n_lead_pad, prev_valid_from=prev_valid_from)
    return pl.pallas_call(
        kern,
        grid=(B, nj),
        in_specs=in_specs,
        out_specs=out_specs,
        out_shape=out_shape,
        scratch_shapes=[pltpu.VMEM((GLA_QK, GLA_V), F32), pltpu.VMEM((WINDOW, SWA_KV), F32),
                        pltpu.VMEM((WINDOW, SWA_KV), F32), pltpu.VMEM((N_EXPERTS, LANES), F32)],
        compiler_params=pltpu.CompilerParams(dimension_semantics=("arbitrary", "arbitrary"),
                                             vmem_limit_bytes=VMEM_LIMIT),
        name="mixer",
    )(wts["sinks"], x, s0, k0, v0, base0, *weight_args)


def _decode_kernel(sinks_ref, x_ref, st_ref, ck_ref, cv_ref, base0_ref, bias_ref,
                   g_mix_ref, w_in_ref, w_a_up_ref, b_a_ref, g_gla_ref, g_swa_ref, w_out_ref,
                   g_ffn_ref, w_r_ref, b_r_ref,
                   xmid_ref, hp_ref, topi_ref, gate_ref, rank_ref, sto_ref, cko_ref, cvo_ref, cnt_ref,
                   at_scr, kt_scr, qt_scr, gv_scr, gr_scr, sq_scr, sk_scr, sv_scr, og_scr, os_scr):
    i = pl.program_id(0)
    n_seq = x_ref.shape[0]

    @pl.when(i == 0)
    def _():
        p = _project(x_ref[...], g_mix_ref[...], w_in_ref, w_a_up_ref[...], b_a_ref[...])
        at_scr[...] = jnp.transpose(jnp.exp(p["log_a"]))
        kt_scr[...] = jnp.transpose(p["gk"])
        qt_scr[...] = jnp.transpose(p["gq"])
        gv_scr[...] = p["gv"]
        gr_scr[...] = p["gr"]
        sq_scr[...] = p["sq"]
        sk_scr[...] = p["sk"]
        sv_scr[...] = p["sv"]

    lane_seq = _iota((GLA_QK, n_seq), 1)
    half = _iota((1, LANES), 1) < SWA_HEAD_DIM
    row_id = _iota((WINDOW, SWA_KV), 0)
    head_diag = (_iota((16, SWA_Q), 1) // SWA_HEAD_DIM) == _iota((16, SWA_Q), 0)
    sink_col = jnp.concatenate(
        [jnp.full((1, 1), sinks_ref[h], F32) for h in range(SWA_HEADS)] + [jnp.zeros((8, 1), F32)], axis=0)

    def per_seq(sl, carry):
        s = i * DEC_SB + sl
        sel = lane_seq == s
        a_c = jnp.sum(jnp.where(sel, at_scr[...], 0.0), axis=1, keepdims=True)
        k_c = jnp.sum(jnp.where(sel, kt_scr[...], 0.0), axis=1, keepdims=True)
        q_c = jnp.sum(jnp.where(sel, qt_scr[...], 0.0), axis=1, keepdims=True)
        st = st_ref[sl].reshape(GLA_QK, GLA_DV)
        v_row = gv_scr[pl.ds(s, 1), :]
        v_b = jnp.concatenate(
            [jnp.broadcast_to(v_row[:, h * GLA_DV:(h + 1) * GLA_DV], (GLA_DK, GLA_DV))
             for h in range(GLA_HEADS)], axis=0)
        st_new = a_c * st + k_c * v_b
        sto_ref[sl] = st_new.reshape(GLA_HEADS, GLA_DK, GLA_DV)
        t = q_c * st_new
        og_scr[pl.ds(s, 1), :] = jnp.concatenate(
            [jnp.sum(t[h * GLA_DK:(h + 1) * GLA_DK], axis=0, keepdims=True) for h in range(GLA_HEADS)],
            axis=1)

        k_new = sk_scr[pl.ds(s, 1), :]
        v_new = sv_scr[pl.ds(s, 1), :]
        kn = jnp.where(row_id == WINDOW - 1, k_new, pltpu.roll(ck_ref[sl], WINDOW - 1, 0))
        vn = jnp.where(row_id == WINDOW - 1, v_new, pltpu.roll(cv_ref[sl], WINDOW - 1, 0))
        cko_ref[sl] = kn
        cvo_ref[sl] = vn
        kr, vr = pltpu.roll(kn, SWA_HEAD_DIM, 1), pltpu.roll(vn, SWA_HEAD_DIM, 1)
        k0, k1 = jnp.where(half, kn, kr), jnp.where(half, kr, kn)
        v0, v1 = jnp.where(half, vn, vr), jnp.where(half, vr, vn)
        kw = jnp.concatenate([k0, k0, k1, k1], axis=1).astype(BF16)
        vw = jnp.concatenate([v0, v0, v1, v1], axis=1).astype(BF16)
        q_row = sq_scr[pl.ds(s, 1), :]
        qm = jnp.where(head_diag, jnp.broadcast_to(q_row, (16, SWA_Q)), 0.0).astype(BF16)
        sc = _dot_nt(qm, kw) * (SWA_HEAD_DIM ** -0.5) + bias_ref[...]
        m = jnp.maximum(jnp.max(sc, axis=1, keepdims=True), sink_col)
        pr = jnp.exp(sc - m)
        inv = 1.0 / (jnp.sum(pr, axis=1, keepdims=True) + jnp.exp(sink_col - m))
        ow = _dot(pr.astype(BF16), vw) * inv
        os_scr[pl.ds(s, 1), :] = jnp.sum(jnp.where(head_diag, ow, 0.0), axis=0, keepdims=True)
        return carry

    lax.fori_loop(0, DEC_SB, per_seq, 0, unroll=4)

    @pl.when(i == pl.num_programs(0) - 1)
    def _():
        x_mid, hp, topi, gate8, rank8, base = _tail(
            x_ref[...], og_scr[...], gr_scr[...], os_scr[...], g_gla_ref[...], g_swa_ref[...],
            w_out_ref, g_ffn_ref[...], w_r_ref[...], b_r_ref[...], base0_ref[...])
        xmid_ref[...] = x_mid
        _store_slabs(hp_ref, hp)
        topi_ref[...] = topi
        gate_ref[...] = gate8
        rank_ref[...] = rank8
        cnt_ref[...] = base


def _decode_call(xs, state, ck, cv, base0, bias_dec, wts):
    n_seq = xs.shape[0]
    nb = n_seq // DEC_SB
    weight_args = (wts["g_mix"], wts["w_in"], wts["w_a_up"], wts["b_a"], wts["g_gla"],
                   wts["g_swa"], wts["w_out"], wts["g_ffn"], wts["w_r"], wts["b_r"])
    in_specs = [
        pl.BlockSpec(memory_space=pltpu.SMEM),
        _full_spec(xs.shape),
        pl.BlockSpec((DEC_SB, GLA_HEADS, GLA_DK, GLA_DV), lambda i: (i, 0, 0, 0)),
        pl.BlockSpec((DEC_SB, WINDOW, SWA_KV), lambda i: (i, 0, 0)),
        pl.BlockSpec((DEC_SB, WINDOW, SWA_KV), lambda i: (i, 0, 0)),
        _full_spec(base0.shape), _full_spec(bias_dec.shape),
    ] + [_full_spec(w.shape) for w in weight_args]
    out_specs = [
        _full_spec((n_seq, D_MODEL)),
        _full_spec((n_seq * SLAB, LANES)),
        _full_spec((8, n_seq)), _full_spec((8, n_seq)), _full_spec((8, n_seq)),
        pl.BlockSpec((DEC_SB, GLA_HEADS, GLA_DK, GLA_DV), lambda i: (i, 0, 0, 0)),
        pl.BlockSpec((DEC_SB, WINDOW, SWA_KV), lambda i: (i, 0, 0)),
        pl.BlockSpec((DEC_SB, WINDOW, SWA_KV), lambda i: (i, 0, 0)),
        _full_spec((N_EXPERTS, LANES)),
    ]
    out_shape = [
        jax.ShapeDtypeStruct((n_seq, D_MODEL), F32),
        jax.ShapeDtypeStruct((n_seq * SLAB, LANES), jnp.int32),
        jax.ShapeDtypeStruct((8, n_seq), jnp.int32),
        jax.ShapeDtypeStruct((8, n_seq), F32),
        jax.ShapeDtypeStruct((8, n_seq), jnp.int32),
        jax.ShapeDtypeStruct(state.shape, F32),
        jax.ShapeDtypeStruct(ck.shape, F32),
        jax.ShapeDtypeStruct(cv.shape, F32),
        jax.ShapeDtypeStruct((N_EXPERTS, LANES), F32),
    ]
    scratch = [pltpu.VMEM((GLA_QK, n_seq), F32)] * 3 + [
        pltpu.VMEM((n_seq, GLA_V), F32), pltpu.VMEM((n_seq, GLA_V), F32), pltpu.VMEM((n_seq, SWA_Q), F32),
        pltpu.VMEM((n_seq, SWA_KV), F32), pltpu.VMEM((n_seq, SWA_KV), F32),
        pltpu.VMEM((n_seq, GLA_V), F32), pltpu.VMEM((n_seq, SWA_Q), F32)]
    return pl.pallas_call(
        _decode_kernel,
        grid=(nb,),
        in_specs=in_specs,
        out_specs=out_specs,
        out_shape=out_shape,
        scratch_shapes=scratch,
        compiler_params=pltpu.CompilerParams(dimension_semantics=("arbitrary",),
                                             vmem_limit_bytes=VMEM_LIMIT),
        name="decode",
    )(wts["sinks"], xs, state, ck, cv, base0, bias_dec, *weight_args)


SC_CORES = 2
SC_SUBCORES = 16
SC_WORKERS = SC_CORES * SC_SUBCORES
SC_SCATTER_ROWS = 64
SC_GATHER_ROWS = 48


def _sc_mesh():
    return plsc.VectorSubcoreMesh(core_axis_name="c", subcore_axis_name="s")


def _sc_worker_id():
    return lax.axis_index("s") * SC_CORES + lax.axis_index("c")


def _sc_scatter_rows(src_p, src_s, idx_p, idx_s, n_out):
    rows = SC_SCATTER_ROWS
    n_chunks = idx_p.shape[0] // SC_WORKERS
    n_s, _, rows_s = idx_s.shape
    assert n_chunks * SC_WORKERS == idx_p.shape[0] and n_chunks % 2 == 0 and n_s <= SC_WORKERS

    @functools.partial(
        pl.kernel, mesh=_sc_mesh(),
        out_type=jax.ShapeDtypeStruct((n_out, SLAB, LANES), jnp.int32),
        scratch_types=[pltpu.VMEM((2, TOP_K, rows), jnp.int32), pltpu.VMEM((2, rows, SLAB, LANES), jnp.int32),
                       pltpu.VMEM((TOP_K, rows_s), jnp.int32), pltpu.VMEM((rows_s, SLAB, LANES), jnp.int32),
                       pltpu.SemaphoreType.DMA((2,)), pltpu.SemaphoreType.DMA((2,))])
    def scatter_rows(srcp_hbm, srcs_hbm, idxp_hbm, idxs_hbm, out_hbm, idx_v, rows_v, idxs_v, rowss_v, lsem, ssem):
        wid = _sc_worker_id()

        def loads(c, b):
            g = wid * n_chunks + c
            return (pltpu.make_async_copy(idxp_hbm.at[g], idx_v.at[b], lsem.at[b]),
                    pltpu.make_async_copy(srcp_hbm.at[pl.ds(pl.multiple_of(g * rows, 8), rows)], rows_v.at[b],
                                          lsem.at[b]))

        def scatters(b):
            return [pltpu.make_async_copy(rows_v.at[b], out_hbm.at[idx_v.at[b, k]], ssem.at[b])
                    for k in range(TOP_K)]

        for d in loads(0, 0):
            d.start()

        @pl.loop(0, n_chunks, step=2)
        def _(c0):
            for b in range(2):
                c = c0 + b
                for d in loads(c, b):
                    d.wait()

                @pl.when(c >= 1)
                def _():
                    for d in scatters(1 - b):
                        d.wait()

                @pl.when(c + 1 < n_chunks)
                def _():
                    for d in loads(c + 1, 1 - b):
                        d.start()

                for d in scatters(b):
                    d.start()

        for d in scatters((n_chunks - 1) % 2):
            d.wait()

        @pl.when(wid < n_s)
        def _():
            pltpu.sync_copy(idxs_hbm.at[wid], idxs_v)
            pltpu.sync_copy(srcs_hbm.at[pl.ds(pl.multiple_of(wid * rows_s, 8), rows_s)], rowss_v)
            for k in range(TOP_K):
                pltpu.sync_copy(rowss_v, out_hbm.at[idxs_v.at[k]])

    return scatter_rows(src_p, src_s, idx_p, idx_s)


def _sc_gather_rows(src3, idx2):
    rows = SC_GATHER_ROWS
    n_chunks = idx2.shape[0] // SC_WORKERS
    assert n_chunks * SC_WORKERS == idx2.shape[0] and idx2.shape[1] == rows and n_chunks % 2 == 0

    @functools.partial(
        pl.kernel, mesh=_sc_mesh(),
        out_type=jax.ShapeDtypeStruct((idx2.shape[0] * rows, SLAB, LANES), jnp.int32),
        scratch_types=[pltpu.VMEM((2, rows), jnp.int32), pltpu.VMEM((2, rows, SLAB, LANES), jnp.int32),
                       pltpu.SemaphoreType.DMA((2,)), pltpu.SemaphoreType.DMA((2,))])
    def gather_rows(src_hbm, idx_hbm, out_hbm, idx_v, rows_v, gsem, wsem):
        wid = _sc_worker_id()

        def gather(b):
            return pltpu.make_async_copy(src_hbm.at[idx_v.at[b]], rows_v.at[b], gsem.at[b])

        def write(c, b):
            base = pl.multiple_of((wid * n_chunks + c) * rows, 8)
            return pltpu.make_async_copy(rows_v.at[b], out_hbm.at[pl.ds(base, rows)], wsem.at[b])

        pltpu.sync_copy(idx_hbm.at[wid * n_chunks], idx_v.at[0])
        gather(0).start()

        @pl.loop(0, n_chunks, step=2)
        def _(c0):
            for b in range(2):
                c = c0 + b

                @pl.when(c + 1 < n_chunks)
                def _():
                    @pl.when(c >= 1)
                    def _():
                        write(c - 1, 1 - b).wait()
                    pltpu.sync_copy(idx_hbm.at[wid * n_chunks + c + 1], idx_v.at[1 - b])
                    gather(1 - b).start()

                gather(b).wait()
                write(c, b).start()

        write(n_chunks - 2, 0).wait()
        write(n_chunks - 1, 1).wait()

    return gather_rows(src3, idx2)


FF_TILE = 256


def _ffn_kernel(blk_e_ref, nused_ref, x_ref, wu_ref, bu_ref, wd_ref, bd_ref, y_ref, xbf, actbf, wu_bf, wd_bf):
    i = pl.program_id(0)
    tm = MOE_TM
    n_tiles = D_FF // FF_TILE

    @pl.when(i < nused_ref[0])
    def _():
        @pl.when((i == 0) | (blk_e_ref[i] != blk_e_ref[jnp.maximum(i - 1, 0)]))
        def _():
            wu_bf[...] = wu_ref[0].astype(BF16)
            wd_bf[...] = wd_ref[0].astype(BF16)

        for c in range(SLAB):
            lo, hi = _load_slab_chunk(x_ref, tm, c)
            xbf[:, c * LANES:(c + 1) * LANES] = lo.astype(BF16)
            xbf[:, HALF_D + c * LANES:HALF_D + (c + 1) * LANES] = hi.astype(BF16)
        for n in range(n_tiles):
            gc = slice(n * FF_TILE, (n + 1) * FF_TILE)
            lc = slice(D_FF + n * FF_TILE, D_FF + (n + 1) * FF_TILE)
            g = jnp.minimum(_dot(xbf[...], wu_bf[:, gc]) + bu_ref[0, :, gc], SWIGLU_LIMIT)
            lin = jnp.clip(_dot(xbf[...], wu_bf[:, lc]) + bu_ref[0, :, lc], -SWIGLU_LIMIT, SWIGLU_LIMIT)
            actbf[:, gc] = (g * jax.nn.sigmoid(SWIGLU_ALPHA * g) * (lin + 1.0)).astype(BF16)
        per_tile = FF_TILE // LANES
        for n in range(n_tiles // 2):
            yl = slice(n * FF_TILE, (n + 1) * FF_TILE)
            yh = slice(HALF_D + n * FF_TILE, HALF_D + (n + 1) * FF_TILE)
            y_lo = _dot(actbf[...], wd_bf[:, yl]) + bd_ref[0, :, yl]
            y_hi = _dot(actbf[...], wd_bf[:, yh]) + bd_ref[0, :, yh]
            for c in range(per_tile):
                sl = slice(c * LANES, (c + 1) * LANES)
                y_ref[pl.ds(n * per_tile + c, tm, stride=SLAB), :] = _pack_pair(y_lo[:, sl], y_hi[:, sl])

    @pl.when(i >= nused_ref[0])
    def _():
        y_ref[...] = jnp.zeros_like(y_ref)


def _ffn_call(blk_e, nused, xs2, w_up, b_up, w_down, b_down):
    n_blocks = blk_e.shape[0]
    tm = MOE_TM
    row_blk = pl.BlockSpec((tm * SLAB, LANES), lambda i, be, nu: (i, 0))
    grid_spec = pltpu.PrefetchScalarGridSpec(
        num_scalar_prefetch=2,
        grid=(n_blocks,),
        in_specs=[
            row_blk,
            pl.BlockSpec((1, D_MODEL, 2 * D_FF), lambda i, be, nu: (be[i], 0, 0)),
            pl.BlockSpec((1, 1, 2 * D_FF), lambda i, be, nu: (be[i], 0, 0)),
            pl.BlockSpec((1, D_FF, D_MODEL), lambda i, be, nu: (be[i], 0, 0)),
            pl.BlockSpec((1, 1, D_MODEL), lambda i, be, nu: (be[i], 0, 0)),
        ],
        out_specs=row_blk,
        scratch_shapes=[pltpu.VMEM((tm, D_MODEL), BF16), pltpu.VMEM((tm, D_FF), BF16),
                        pltpu.VMEM((D_MODEL, 2 * D_FF), BF16), pltpu.VMEM((D_FF, D_MODEL), BF16)],
    )
    return pl.pallas_call(
        _ffn_kernel,
        grid_spec=grid_spec,
        out_shape=jax.ShapeDtypeStruct((n_blocks * tm * SLAB, LANES), jnp.int32),
        compiler_params=pltpu.CompilerParams(dimension_semantics=("arbitrary",),
                                             vmem_limit_bytes=VMEM_LIMIT),
        name="experts",
    )(blk_e, nused, xs2, w_up, b_up.reshape(N_EXPERTS, 1, 2 * D_FF), w_down, b_down.reshape(N_EXPERTS, 1, D_MODEL))


def _combine_kernel(ys0_ref, ys1_ref, ys2_ref, ys3_ref, xmid_ref, gate_ref, g_final_ref, y_ref):
    tm = xmid_ref.shape[0]
    gts = gate_ref[...]
    lows, highs = [], []
    for c in range(SLAB):
        acc_lo = xmid_ref[:, c * LANES:(c + 1) * LANES]
        acc_hi = xmid_ref[:, HALF_D + c * LANES:HALF_D + (c + 1) * LANES]
        for k, ys_ref in enumerate((ys0_ref, ys1_ref, ys2_ref, ys3_ref)):
            lo, hi = _load_slab_chunk(ys_ref, tm, c)
            acc_lo = acc_lo + lo * gts[:, k:k + 1]
            acc_hi = acc_hi + hi * gts[:, k:k + 1]
        lows.append(acc_lo)
        highs.append(acc_hi)
    y_ref[...] = _rms(jnp.concatenate(lows + highs, axis=1), g_final_ref[...])


def _combine_call(ys4, t_stride, row0, x_mid, gates, g_final, tm):
    T = x_mid.shape[0]
    blk0 = row0 // tm
    per_k = t_stride // tm
    assert per_k * tm == t_stride and blk0 * tm == row0

    def ys_spec(k):
        return pl.BlockSpec((tm * SLAB, LANES), lambda i: (k * per_k + blk0 + i, 0))

    return pl.pallas_call(
        _combine_kernel,
        grid=(T // tm,),
        in_specs=[
            ys_spec(0), ys_spec(1), ys_spec(2), ys_spec(3),
            pl.BlockSpec((tm, D_MODEL), lambda i: (i, 0)),
            pl.BlockSpec((tm, TOP_K), lambda i: (i, 0)),
            _full_spec((1, D_MODEL)),
        ],
        out_specs=pl.BlockSpec((tm, D_MODEL), lambda i: (i, 0)),
        out_shape=jax.ShapeDtypeStruct((T, D_MODEL), F32),
        compiler_params=pltpu.CompilerParams(dimension_semantics=("arbitrary",),
                                             vmem_limit_bytes=VMEM_LIMIT),
        name="combine",
    )(ys4, ys4, ys4, ys4, x_mid, gates, g_final)


def _t5_bucket(dist):
    n = jnp.maximum(dist, 0)
    max_exact = NUM_BUCKETS // 2
    nf = jnp.maximum(n, 1).astype(F32)
    large = max_exact + (jnp.log(nf / max_exact) / math.log(MAX_DISTANCE / max_exact)
                         * (NUM_BUCKETS - max_exact)).astype(jnp.int32)
    large = jnp.minimum(large, NUM_BUCKETS - 1)
    return jnp.where(n < max_exact, n, large)


def kernel(x_prompt, x_sample, state_gla, cache_swa_k, cache_swa_v, meta_tokens, rel_bias_table,
           g_mix, w_in, w_a_up, b_a, g_gla_out, g_swa_out, attn_sinks, w_out,
           g_ffn, w_router, b_router, w_up, b_up, w_down, b_down, g_final):
    assert g_mix.shape[0] == 1, "single-layer trunk"
    B, L, _ = x_prompt.shape
    n_seq = x_sample.shape[0]
    TP = B * L
    T_all = TP + n_seq

    wi = w_in[0]
    sizes = (GLA_QK, GLA_QK, GLA_V, GLA_V, GLA_LOWRANK, SWA_Q, SWA_KV, SWA_KV)
    offs = [0]
    for s in sizes:
        offs.append(offs[-1] + s)
    seg = [wi[:, offs[n]:offs[n + 1]] for n in range(8)]
    w_in_r = jnp.concatenate(
        seg[0:4] + seg[5:8] + [seg[4], jnp.zeros((D_MODEL, LANES - GLA_LOWRANK), F32)], axis=1).astype(BF16)
    w_a_pad = jnp.concatenate([w_a_up[0], jnp.zeros((LANES - GLA_LOWRANK, GLA_QK), F32)], axis=0).astype(BF16)
    wr_t = jnp.transpose(w_router[0])
    wr_hi = wr_t.astype(BF16)
    wr_lo = (wr_t - wr_hi.astype(F32)).astype(BF16)
    qi = jnp.arange(WINDOW)[:, None]
    kj = jnp.arange(2 * WINDOW)[None, :]
    buckets = jnp.arange(NUM_BUCKETS)
    table = rel_bias_table.astype(F32)
    oh_p = (_t5_bucket(qi - kj + WINDOW)[..., None] == buckets).astype(F32)
    bias_p = jnp.einsum("qkb,bh->hkq", oh_p, table, precision=lax.Precision.HIGHEST)
    bias_p = bias_p.reshape(SWA_KV_HEADS, SWA_GROUP, 2 * WINDOW, WINDOW).transpose(0, 2, 1, 3)
    bias_p = bias_p.reshape(SWA_KV_HEADS, 2 * WINDOW, SWA_GROUP * WINDOW)
    oh_d = (_t5_bucket(WINDOW - 1 - jnp.arange(WINDOW))[:, None] == buckets).astype(F32)
    bias_d = jnp.einsum("rb,bh->hr", oh_d, table, precision=lax.Precision.HIGHEST)
    bias_d = jnp.concatenate([bias_d, jnp.zeros((8, WINDOW), F32)], axis=0)
    wts = dict(
        sinks=attn_sinks[0].astype(F32), bias=bias_p,
        g_mix=g_mix[0][None], w_in=w_in_r, w_a_up=w_a_pad, b_a=b_a[0][None],
        g_gla=g_gla_out[0][None], g_swa=g_swa_out[0][None], w_out=w_out[0].astype(BF16),
        g_ffn=g_ffn[0][None], w_r=jnp.concatenate([wr_hi, wr_lo], axis=0), b_r=b_router[0][:, None],
    )

    x_pre = jnp.concatenate([jnp.zeros((WINDOW - N_META, D_MODEL), F32), meta_tokens.astype(F32)], axis=0)[None]
    zeros_s = jnp.zeros((GLA_QK, GLA_V), F32)
    zeros_kv = jnp.zeros((WINDOW, SWA_KV), F32)
    zeros_b = jnp.zeros((N_EXPERTS, LANES), F32)
    pre = _mixer_call(x_pre, zeros_s, zeros_kv, zeros_kv, zeros_b, wts, WINDOW, WINDOW - N_META, 0, WINDOW)
    s_meta, k_meta, v_meta = pre[5][0], pre[6][0], pre[7][0]

    (xmid_p, hp_p, topi_p, gate_p, rank_p, s_p, k_p, v_p, cnt_p) = _mixer_call(
        x_prompt, s_meta, k_meta, v_meta, zeros_b, wts, MIX_TM, 0, WINDOW - N_META, TP)

    (xmid_s, hp_s, topi_s, gate_s, rank_s, st_s, ck_s, cv_s, cnt_all) = _decode_call(
        x_sample[:, 0], state_gla[0], cache_swa_k[0].reshape(n_seq, WINDOW, SWA_KV),
        cache_swa_v[0].reshape(n_seq, WINDOW, SWA_KV), cnt_p, bias_d, wts)

    tm = MOE_TM
    n_slots = T_all * TOP_K
    n_blocks = -(-n_slots // tm) + N_EXPERTS
    top_e = jnp.concatenate([topi_p[:TOP_K], topi_s[:TOP_K]], axis=1)
    rank = jnp.concatenate([rank_p[:TOP_K], rank_s[:TOP_K]], axis=1)
    counts = cnt_all[:, 0].astype(jnp.int32)
    padded = (counts + tm - 1) // tm * tm
    pad_end = jnp.cumsum(padded)
    pad_start = pad_end - padded
    e_ids = jnp.arange(N_EXPERTS, dtype=jnp.int32)
    dest = jnp.sum(jnp.where(top_e[..., None] == e_ids, pad_start, 0), axis=-1) + rank
    n_pad = n_blocks * tm
    blk_e = jnp.minimum(jnp.sum(pad_end[None] <= (jnp.arange(n_blocks, dtype=jnp.int32) * tm)[:, None], axis=1),
                        N_EXPERTS - 1).astype(jnp.int32)
    nused = (pad_end[-1] // tm).astype(jnp.int32).reshape(1)

    sample_rows = 8
    idx_p = dest[:, :TP].reshape(TOP_K, TP // SC_SCATTER_ROWS, SC_SCATTER_ROWS).transpose(1, 0, 2)
    idx_s = dest[:, TP:].reshape(TOP_K, n_seq // sample_rows, sample_rows).transpose(1, 0, 2)
    xs3 = _sc_scatter_rows(hp_p.reshape(TP, SLAB, LANES), hp_s.reshape(n_seq, SLAB, LANES), idx_p, idx_s, n_pad)
    ys2 = _ffn_call(blk_e, nused, xs3.reshape(-1, LANES), w_up[0], b_up[0], w_down[0], b_down[0])
    unit = math.lcm(2 * SC_WORKERS * SC_GATHER_ROWS // TOP_K, MIX_TM)
    t_stride = -(-T_all // unit) * unit
    filler = jnp.arange(TOP_K * (t_stride - T_all), dtype=jnp.int32).reshape(TOP_K, t_stride - T_all)
    slot_src = jnp.concatenate([dest, filler], axis=1)
    slot_src = slot_src.reshape(TOP_K * t_stride // SC_GATHER_ROWS, SC_GATHER_ROWS)
    ys4 = _sc_gather_rows(ys2.reshape(-1, SLAB, LANES), slot_src).reshape(-1, LANES)

    gates = jnp.transpose(jnp.concatenate([gate_p[:TOP_K], gate_s[:TOP_K]], axis=1))
    gf = g_final[None]
    y_p = _combine_call(ys4, t_stride, 0, xmid_p.reshape(TP, D_MODEL), gates[:TP], gf, MIX_TM)
    y_s = _combine_call(ys4, t_stride, TP, xmid_s, gates[TP:], gf, n_seq)

    s_heads = jnp.stack([s_p[:, h * GLA_DK:(h + 1) * GLA_DK, h * GLA_DV:(h + 1) * GLA_DV]
                         for h in range(GLA_HEADS)], axis=1)
    return (y_p.reshape(B, L, D_MODEL), y_s.reshape(n_seq, 1, D_MODEL), s_heads[None],
            k_p.reshape(1, B, WINDOW, SWA_KV_HEADS, SWA_HEAD_DIM),
            v_p.reshape(1, B, WINDOW, SWA_KV_HEADS, SWA_HEAD_DIM),
            st_s[None], ck_s.reshape(1, n_seq, WINDOW, SWA_KV_HEADS, SWA_HEAD_DIM),
            cv_s.reshape(1, n_seq, WINDOW, SWA_KV_HEADS, SWA_HEAD_DIM))
```

```python
import functools
import math

import jax
import jax.numpy as jnp
from jax import lax
from jax.experimental import pallas as pl
from jax.experimental.pallas import tpu as pltpu
from jax.experimental.pallas import tpu_sc as plsc

D_MODEL = 1024
N_META = 16
GLA_HEADS = 4
GLA_DK = 64
GLA_DV = 128
GLA_LOWRANK = 16
GLA_GATE_TAU = 16.0
GLA_CHUNK = 64
SWA_HEADS = 8
SWA_KV_HEADS = 2
SWA_HEAD_DIM = 64
SWA_GROUP = SWA_HEADS // SWA_KV_HEADS
WINDOW = 128
NUM_BUCKETS = 32
MAX_DISTANCE = 128
N_EXPERTS = 32
TOP_K = 4
D_FF = 1024
SWIGLU_ALPHA = 1.702
SWIGLU_LIMIT = 7.0
RMS_EPS = 1e-6

GLA_QK = GLA_HEADS * GLA_DK
GLA_V = GLA_HEADS * GLA_DV
SWA_Q = SWA_HEADS * SWA_HEAD_DIM
SWA_KV = SWA_KV_HEADS * SWA_HEAD_DIM
LANES = 128
C_GQ, C_GK, C_GV, C_GR = 0, GLA_QK, 2 * GLA_QK, 2 * GLA_QK + GLA_V
C_SQ = C_GR + GLA_V
C_SK = C_SQ + SWA_Q
C_SV = C_SK + SWA_KV
C_GA = C_SV + SWA_KV
D_PROJ = C_GA + LANES

MIX_TM = 512
MOE_TM = 512
DEC_SB = 16
VMEM_LIMIT = 56 * 1024 * 1024

F32 = jnp.float32
BF16 = jnp.bfloat16
NEG_INF = float("-inf")


def _dot(a, b):
    return jnp.dot(a, b, preferred_element_type=F32)


def _dot_nt(a, b):
    return lax.dot_general(a, b, (((1,), (1,)), ((), ())), preferred_element_type=F32)


def _split3(x):
    hi = x.astype(BF16)
    r1 = x - hi.astype(F32)
    mid = r1.astype(BF16)
    lo = (r1 - mid.astype(F32)).astype(BF16)
    return hi, mid, lo


def _rms(x, g):
    return x * lax.rsqrt(jnp.mean(x * x, axis=-1, keepdims=True) + RMS_EPS) * g


def _iota(shape, dim):
    return lax.broadcasted_iota(jnp.int32, shape, dim)


HALF_D = D_MODEL // 2
SLAB = HALF_D // LANES


def _pack_pair(lo, hi):
    bl = lax.bitcast_convert_type(lo.astype(BF16).astype(F32), jnp.uint32)
    bh = lax.bitcast_convert_type(hi.astype(BF16).astype(F32), jnp.uint32)
    return lax.bitcast_convert_type(bh | lax.shift_right_logical(bl, jnp.uint32(16)), jnp.int32)


def _unpack_pair(w):
    u = lax.bitcast_convert_type(w, jnp.uint32)
    lo = lax.bitcast_convert_type(lax.shift_left(u, jnp.uint32(16)), F32)
    hi = lax.bitcast_convert_type(u & jnp.uint32(0xFFFF0000), F32)
    return lo, hi


def _store_slabs(ref, x):
    rows = x.shape[0]
    for c in range(SLAB):
        sl = slice(c * LANES, (c + 1) * LANES)
        ref[pl.ds(c, rows, stride=SLAB), :] = _pack_pair(x[:, sl], x[:, HALF_D + c * LANES:HALF_D + (c + 1) * LANES])


def _load_slab_chunk(ref, rows, c):
    return _unpack_pair(ref[pl.ds(c, rows, stride=SLAB), :])


def _project(x, g_mix, w_in_ref, w_a_up, b_a):
    h = _rms(x, g_mix).astype(BF16)

    def cols(lo, width):
        return _dot(h, w_in_ref[:, lo:lo + width])

    ga = cols(C_GA, LANES).astype(BF16)
    z = _dot(ga, w_a_up) + b_a
    log_a = -(jnp.maximum(-z, 0.0) + jnp.log1p(jnp.exp(-jnp.abs(z)))) / GLA_GATE_TAU
    gqk = cols(C_GQ, 2 * GLA_QK)
    gv = cols(C_GV, GLA_V)
    swa = cols(C_SQ, SWA_Q + 2 * SWA_KV)
    gr = cols(C_GR, GLA_V)
    return dict(
        gq=gqk[:, :GLA_QK] * (GLA_DK ** -0.5),
        gk=gqk[:, GLA_QK:],
        gv=gv,
        gr=gr,
        sq=swa[:, :SWA_Q],
        sk=swa[:, SWA_Q:SWA_Q + SWA_KV],
        sv=swa[:, SWA_Q + SWA_KV:],
        log_a=log_a,
    )


def _tail(x, o_gla, gr, o_swa, g_gla_out, g_swa_out, w_out_ref, g_ffn, w_r, b_r, base):
    tm = x.shape[0]
    gate = gr * jax.nn.sigmoid(gr)
    parts = []
    for h in range(GLA_HEADS):
        sl = slice(h * GLA_DV, (h + 1) * GLA_DV)
        parts.append(_rms(o_gla[:, sl], g_gla_out) * gate[:, sl])
    og = jnp.concatenate(parts, axis=1).astype(BF16)
    os_ = _rms(o_swa, g_swa_out).astype(BF16)
    x_mid = x + _dot(og, w_out_ref[0:GLA_V]) + _dot(os_, w_out_ref[GLA_V:GLA_V + SWA_Q])
    hp = _rms(x_mid, g_ffn)

    h1 = hp.astype(BF16)
    h2 = (hp - h1.astype(F32)).astype(BF16)
    la = _dot_nt(w_r, h1)
    lb = _dot_nt(w_r[0:N_EXPERTS], h2)
    logits = la[0:N_EXPERTS] + la[N_EXPERTS:2 * N_EXPERTS] + lb + b_r

    eidx = _iota((N_EXPERTS, tm), 0)
    vals, idxs, onehots = [], [], []
    l = logits
    for _ in range(TOP_K):
        m = jnp.max(l, axis=0, keepdims=True)
        sel = jnp.min(jnp.where(l == m, eidx, N_EXPERTS), axis=0, keepdims=True)
        oh = eidx == sel
        l = jnp.where(oh, NEG_INF, l)
        vals.append(m)
        idxs.append(sel)
        onehots.append(oh)
    es = [jnp.exp(v - vals[0]) for v in vals]
    denom = es[0] + es[1] + es[2] + es[3]
    gates = [e / denom for e in es]

    ohf = jnp.concatenate([oh.astype(F32) for oh in onehots], axis=0)
    upper = (_iota((tm, tm), 0) < _iota((tm, tm), 1)).astype(BF16)
    prefix = _dot(ohf.astype(BF16), upper)
    ranks = []
    for k in range(TOP_K):
        sl = slice(k * N_EXPERTS, (k + 1) * N_EXPERTS)
        ohk = ohf[sl]
        base_t = jnp.concatenate([base] * (tm // LANES), axis=1)
        ranks.append(jnp.sum(ohk * (prefix[sl] + base_t), axis=0, keepdims=True))
        base = base + jnp.sum(ohk, axis=1, keepdims=True)
    zi = jnp.zeros((8 - TOP_K, tm), jnp.int32)
    zf = jnp.zeros((8 - TOP_K, tm), F32)
    topi = jnp.concatenate(idxs + [zi], axis=0)
    gate8 = jnp.concatenate(gates + [zf], axis=0)
    rank8 = jnp.concatenate([r.astype(jnp.int32) for r in ranks] + [zi], axis=0)
    return x_mid, hp, topi, gate8, rank8, base


def _gla_chunks(p, row0, s_bd, n_lead_pad):
    tm = p["gq"].shape[0]
    nch = tm // GLA_CHUNK
    log_a = p["log_a"]
    if n_lead_pad:
        rows = row0 + _iota((tm, GLA_QK), 0)
        log_a = jnp.where(rows >= n_lead_pad, log_a, 0.0)
    ri, ci = _iota((tm, tm), 0), _iota((tm, tm), 1)
    tril = ((ri >= ci) & (ri // GLA_CHUNK == ci // GLA_CHUNK)).astype(BF16)
    hi, mid, lo = _split3(log_a)
    b_all = _dot(tril, hi) + _dot(tril, mid) + _dot(tril, lo)

    c64 = GLA_CHUNK
    kk_mask = (_iota((GLA_QK, GLA_QK), 0) // c64) == (_iota((GLA_QK, GLA_QK), 1) // GLA_DK)
    vv_mask = (_iota((GLA_QK, GLA_V), 0) // c64) == (_iota((GLA_QK, GLA_V), 1) // GLA_DV)
    ss_mask = (_iota((GLA_QK, GLA_V), 0) // GLA_DK) == (_iota((GLA_QK, GLA_V), 1) // GLA_DV)
    causal = (_iota((c64, GLA_QK), 0) >= (_iota((c64, GLA_QK), 1) % c64)).astype(F32)
    zpad_k = jnp.zeros((LANES - c64, GLA_QK), F32)
    zpad_v = jnp.zeros((LANES - c64, GLA_V), BF16)

    outs = []
    for c in range(nch):
        rs = slice(c * c64, (c + 1) * c64)
        b = b_all[rs]
        q, k, v = p["gq"][rs], p["gk"][rs], p["gv"][rs]
        b_last = b[c64 - 1:c64]
        qt = (q * jnp.exp(b)).astype(BF16)
        kt = k * jnp.exp(-b)
        kd = k * jnp.exp(b_last - b)
        vb = v.astype(BF16)
        k_bd = jnp.where(kk_mask, jnp.concatenate([kt] * GLA_HEADS, axis=0), 0.0).astype(BF16)
        a = (_dot_nt(qt, k_bd) * causal).astype(BF16)
        v_bd = jnp.where(vv_mask, jnp.concatenate([vb] * GLA_HEADS, axis=0), jnp.zeros((), BF16))
        outs.append(_dot(qt, s_bd.astype(BF16)) + _dot(a, v_bd))
        kd_t = jnp.transpose(jnp.concatenate([kd, zpad_k], axis=0)).astype(BF16)
        upd = _dot(kd_t, jnp.concatenate([vb, zpad_v], axis=0))
        decay = jnp.exp(jnp.transpose(jnp.broadcast_to(b_last, (LANES, GLA_QK))))
        s_bd = s_bd * jnp.concatenate([decay] * GLA_HEADS, axis=1) + jnp.where(ss_mask, upd, 0.0)
    return jnp.concatenate(outs, axis=0), s_bd


def _swa_block(sq, kcat, vcat, bias_ref, sinks_ref, valid_t):
    half = _iota((1, LANES), 1) < SWA_HEAD_DIM
    top_rows = _iota((LANES, 1), 0) < SWA_HEAD_DIM
    k_roll = pltpu.roll(kcat, SWA_HEAD_DIM, 1)
    v_t = jnp.transpose(vcat)
    zeros_v = jnp.zeros((SWA_HEAD_DIM, 2 * WINDOW), F32)
    cols = []
    for kv in range(SWA_KV_HEADS):
        kk = jnp.where(half, kcat, k_roll) if kv == 0 else jnp.where(half, k_roll, kcat)
        q_parts = []
        for c in (2 * kv, 2 * kv + 1):
            qc = sq[:, c * LANES:(c + 1) * LANES]
            q_parts.append(jnp.where(half, qc, 0.0))
            q_parts.append(jnp.where(half, 0.0, qc))
        q_st = jnp.concatenate(q_parts, axis=0).astype(BF16)
        s = _dot_nt(kk.astype(BF16), q_st) * (SWA_HEAD_DIM ** -0.5) + bias_ref[kv]
        s = jnp.where(valid_t, s, NEG_INF)
        sink = jnp.concatenate(
            [jnp.full((1, WINDOW), sinks_ref[kv * SWA_GROUP + g], F32) for g in range(SWA_GROUP)], axis=1)
        m = jnp.maximum(jnp.max(s, axis=0, keepdims=True), sink)
        pr = jnp.exp(s - m)
        inv = 1.0 / (jnp.sum(pr, axis=0, keepdims=True) + jnp.exp(sink - m))
        pb = pr.astype(BF16)
        vk = v_t[kv * SWA_HEAD_DIM:(kv + 1) * SWA_HEAD_DIM]
        vv_t = jnp.concatenate([jnp.concatenate([vk, zeros_v], axis=1),
                                jnp.concatenate([zeros_v, vk], axis=1)], axis=0).astype(BF16)
        for pair in range(SWA_GROUP // 2):
            ce = slice(2 * pair * WINDOW, (2 * pair + 1) * WINDOW)
            co = slice((2 * pair + 1) * WINDOW, (2 * pair + 2) * WINDOW)
            p2_t = jnp.concatenate([pb[:, ce], pb[:, co]], axis=0)
            o2_t = _dot(vv_t, p2_t)
            o2_t = o2_t * jnp.where(top_rows, inv[:, ce], inv[:, co])
            cols.append(jnp.transpose(o2_t))
    return jnp.concatenate(cols, axis=1)


def _mixer_kernel(sinks_ref, x_ref, s0_ref, k0_ref, v0_ref, base0_ref, bias_ref,
                  g_mix_ref, w_in_ref, w_a_up_ref, b_a_ref, g_gla_ref, g_swa_ref, w_out_ref,
                  g_ffn_ref, w_r_ref, b_r_ref,
                  xmid_ref, hp_ref, topi_ref, gate_ref, rank_ref, sout_ref, kout_ref, vout_ref, cnt_ref,
                  s_scr, k_scr, v_scr, base_scr, *, n_lead_pad, prev_valid_from):
    b_id, j = pl.program_id(0), pl.program_id(1)
    tm = x_ref.shape[1]

    @pl.when(j == 0)
    def _():
        s_scr[...] = s0_ref[...]
        k_scr[...] = k0_ref[...]
        v_scr[...] = v0_ref[...]

    @pl.when((j == 0) & (b_id == 0))
    def _():
        base_scr[...] = base0_ref[...]

    x = x_ref[0]
    p = _project(x, g_mix_ref[...], w_in_ref, w_a_up_ref[...], b_a_ref[...])

    o_gla, s_new = _gla_chunks(p, j * tm, s_scr[...], n_lead_pad)
    s_scr[...] = s_new

    kj = _iota((2 * WINDOW, WINDOW), 0)
    qi = _iota((2 * WINDOW, WINDOW), 1)
    band = (kj > qi) & (kj <= qi + WINDOW)
    o_parts = []
    for sb in range(tm // WINDOW):
        rs = slice(sb * WINDOW, (sb + 1) * WINDOW)
        k_blk, v_blk = p["sk"][rs], p["sv"][rs]
        k_prev = k_scr[...] if sb == 0 else p["sk"][(sb - 1) * WINDOW:sb * WINDOW]
        v_prev = v_scr[...] if sb == 0 else p["sv"][(sb - 1) * WINDOW:sb * WINDOW]
        valid = band
        if sb == 0 and prev_valid_from:
            first = jnp.where(j == 0, prev_valid_from, 0)
            valid = band & (kj >= first)
        valid = jnp.concatenate([valid] * SWA_GROUP, axis=1)
        o_parts.append(_swa_block(p["sq"][rs], jnp.concatenate([k_prev, k_blk], axis=0),
                                  jnp.concatenate([v_prev, v_blk], axis=0), bias_ref, sinks_ref, valid))
    o_swa = jnp.concatenate(o_parts, axis=0)
    k_scr[...] = p["sk"][tm - WINDOW:tm]
    v_scr[...] = p["sv"][tm - WINDOW:tm]

    x_mid, hp, topi, gate8, rank8, base = _tail(
        x, o_gla, p["gr"], o_swa, g_gla_ref[...], g_swa_ref[...], w_out_ref,
        g_ffn_ref[...], w_r_ref[...], b_r_ref[...], base_scr[...])
    base_scr[...] = base
    xmid_ref[0] = x_mid
    _store_slabs(hp_ref, hp)
    topi_ref[...] = topi
    gate_ref[...] = gate8
    rank_ref[...] = rank8
    sout_ref[0] = s_new
    kout_ref[0] = p["sk"][tm - WINDOW:tm]
    vout_ref[0] = p["sv"][tm - WINDOW:tm]
    cnt_ref[...] = base


def _full_spec(shape):
    nd = len(shape)
    return pl.BlockSpec(shape, lambda *_: (0,) * nd)


def _mixer_call(x, s0, k0, v0, base0, wts, tm, n_lead_pad, prev_valid_from, hp_rows):
    B, L, _ = x.shape
    nj = L // tm
    T = B * L
    weight_args = (wts["bias"], wts["g_mix"], wts["w_in"], wts["w_a_up"], wts["b_a"], wts["g_gla"],
                   wts["g_swa"], wts["w_out"], wts["g_ffn"], wts["w_r"], wts["b_r"])
    in_specs = [
        pl.BlockSpec(memory_space=pltpu.SMEM),
        pl.BlockSpec((1, tm, D_MODEL), lambda b, j: (b, j, 0)),
        _full_spec(s0.shape), _full_spec(k0.shape), _full_spec(v0.shape), _full_spec(base0.shape),
    ] + [_full_spec(w.shape) for w in weight_args]
    tok_spec = pl.BlockSpec((8, tm), lambda b, j: (0, b * nj + j))
    out_specs = [
        pl.BlockSpec((1, tm, D_MODEL), lambda b, j: (b, j, 0)),
        pl.BlockSpec((tm * SLAB, LANES), lambda b, j: (b * nj + j, 0)),
        tok_spec, tok_spec, tok_spec,
        pl.BlockSpec((1, GLA_QK, GLA_V), lambda b, j: (b, 0, 0)),
        pl.BlockSpec((1, WINDOW, SWA_KV), lambda b, j: (b, 0, 0)),
        pl.BlockSpec((1, WINDOW, SWA_KV), lambda b, j: (b, 0, 0)),
        _full_spec((N_EXPERTS, LANES)),
    ]
    out_shape = [
        jax.ShapeDtypeStruct((B, L, D_MODEL), F32),
        jax.ShapeDtypeStruct((hp_rows * SLAB, LANES), jnp.int32),
        jax.ShapeDtypeStruct((8, T), jnp.int32),
        jax.ShapeDtypeStruct((8, T), F32),
        jax.ShapeDtypeStruct((8, T), jnp.int32),
        jax.ShapeDtypeStruct((B, GLA_QK, GLA_V), F32),
        jax.ShapeDtypeStruct((B, WINDOW, SWA_KV), F32),
        jax.ShapeDtypeStruct((B, WINDOW, SWA_KV), F32),
        jax.ShapeDtypeStruct((N_EXPERTS, LANES), F32),
    ]
    kern = functools.partial(_mixer_kernel, n_lead_pad=n_lead_pad, prev_valid_from=prev_valid_from)
    return pl.pallas_call(
        kern,
        grid=(B, nj),
        in_specs=in_specs,
        out_specs=out_specs,
        out_shape=out_shape,
        scratch_shapes=[pltpu.VMEM((GLA_QK, GLA_V), F32), pltpu.VMEM((WINDOW, SWA_KV), F32),
                        pltpu.VMEM((WINDOW, SWA_KV), F32), pltpu.VMEM((N_EXPERTS, LANES), F32)],
        compiler_params=pltpu.CompilerParams(dimension_semantics=("arbitrary", "arbitrary"),
                                             vmem_limit_bytes=VMEM_LIMIT),
        name="mixer",
    )(wts["sinks"], x, s0, k0, v0, base0, *weight_args)


def _decode_kernel(sinks_ref, x_ref, st_ref, ck_ref, cv_ref, base0_ref, bias_ref,
                   g_mix_ref, w_in_ref, w_a_up_ref, b_a_ref, g_gla_ref, g_swa_ref, w_out_ref,
                   g_ffn_ref, w_r_ref, b_r_ref,
                   xmid_ref, hp_ref, topi_ref, gate_ref, rank_ref, sto_ref, cko_ref, cvo_ref, cnt_ref,
                   at_scr, kt_scr, qt_scr, gv_scr, gr_scr, sq_scr, sk_scr, sv_scr, og_scr, os_scr):
    i = pl.program_id(0)
    n_seq = x_ref.shape[0]

    @pl.when(i == 0)
    def _():
        p = _project(x_ref[...], g_mix_ref[...], w_in_ref, w_a_up_ref[...], b_a_ref[...])
        at_scr[...] = jnp.transpose(jnp.exp(p["log_a"]))
        kt_scr[...] = jnp.transpose(p["gk"])
        qt_scr[...] = jnp.transpose(p["gq"])
        gv_scr[...] = p["gv"]
        gr_scr[...] = p["gr"]
        sq_scr[...] = p["sq"]
        sk_scr[...] = p["sk"]
        sv_scr[...] = p["sv"]

    lane_seq = _iota((GLA_QK, n_seq), 1)
    half = _iota((1, LANES), 1) < SWA_HEAD_DIM
    row_id = _iota((WINDOW, SWA_KV), 0)
    head_diag = (_iota((16, SWA_Q), 1) // SWA_HEAD_DIM) == _iota((16, SWA_Q), 0)
    sink_col = jnp.concatenate(
        [jnp.full((1, 1), sinks_ref[h], F32) for h in range(SWA_HEADS)] + [jnp.zeros((8, 1), F32)], axis=0)

    def per_seq(sl, carry):
        s = i * DEC_SB + sl
        sel = lane_seq == s
        a_c = jnp.sum(jnp.where(sel, at_scr[...], 0.0), axis=1, keepdims=True)
        k_c = jnp.sum(jnp.where(sel, kt_scr[...], 0.0), axis=1, keepdims=True)
        q_c = jnp.sum(jnp.where(sel, qt_scr[...], 0.0), axis=1, keepdims=True)
        st = st_ref[sl].reshape(GLA_QK, GLA_DV)
        v_row = gv_scr[pl.ds(s, 1), :]
        v_b = jnp.concatenate(
            [jnp.broadcast_to(v_row[:, h * GLA_DV:(h + 1) * GLA_DV], (GLA_DK, GLA_DV))
             for h in range(GLA_HEADS)], axis=0)
        st_new = a_c * st + k_c * v_b
        sto_ref[sl] = st_new.reshape(GLA_HEADS, GLA_DK, GLA_DV)
        t = q_c * st_new
        og_scr[pl.ds(s, 1), :] = jnp.concatenate(
            [jnp.sum(t[h * GLA_DK:(h + 1) * GLA_DK], axis=0, keepdims=True) for h in range(GLA_HEADS)],
            axis=1)

        k_new = sk_scr[pl.ds(s, 1), :]
        v_new = sv_scr[pl.ds(s, 1), :]
        kn = jnp.where(row_id == WINDOW - 1, k_new, pltpu.roll(ck_ref[sl], WINDOW - 1, 0))
        vn = jnp.where(row_id == WINDOW - 1, v_new, pltpu.roll(cv_ref[sl], WINDOW - 1, 0))
        cko_ref[sl] = kn
        cvo_ref[sl] = vn
        kr, vr = pltpu.roll(kn, SWA_HEAD_DIM, 1), pltpu.roll(vn, SWA_HEAD_DIM, 1)
        k0, k1 = jnp.where(half, kn, kr), jnp.where(half, kr, kn)
        v0, v1 = jnp.where(half, vn, vr), jnp.where(half, vr, vn)
        kw = jnp.concatenate([k0, k0, k1, k1], axis=1).astype(BF16)
        vw = jnp.concatenate([v0, v0, v1, v1], axis=1).astype(BF16)
        q_row = sq_scr[pl.ds(s, 1), :]
        qm = jnp.where(head_diag, jnp.broadcast_to(q_row, (16, SWA_Q)), 0.0).astype(BF16)
        sc = _dot_nt(qm, kw) * (SWA_HEAD_DIM ** -0.5) + bias_ref[...]
        m = jnp.maximum(jnp.max(sc, axis=1, keepdims=True), sink_col)
        pr = jnp.exp(sc - m)
        inv = 1.0 / (jnp.sum(pr, axis=1, keepdims=True) + jnp.exp(sink_col - m))
        ow = _dot(pr.astype(BF16), vw) * inv
        os_scr[pl.ds(s, 1), :] = jnp.sum(jnp.where(head_diag, ow, 0.0), axis=0, keepdims=True)
        return carry

    lax.fori_loop(0, DEC_SB, per_seq, 0, unroll=4)

    @pl.when(i == pl.num_programs(0) - 1)
    def _():
        x_mid, hp, topi, gate8, rank8, base = _tail(
            x_ref[...], og_scr[...], gr_scr[...], os_scr[...], g_gla_ref[...], g_swa_ref[...],
            w_out_ref, g_ffn_ref[...], w_r_ref[...], b_r_ref[...], base0_ref[...])
        xmid_ref[...] = x_mid
        _store_slabs(hp_ref, hp)
        topi_ref[...] = topi
        gate_ref[...] = gate8
        rank_ref[...] = rank8
        cnt_ref[...] = base


def _decode_call(xs, state, ck, cv, base0, bias_dec, wts):
    n_seq = xs.shape[0]
    nb = n_seq // DEC_SB
    weight_args = (wts["g_mix"], wts["w_in"], wts["w_a_up"], wts["b_a"], wts["g_gla"],
                   wts["g_swa"], wts["w_out"], wts["g_ffn"], wts["w_r"], wts["b_r"])
    in_specs = [
        pl.BlockSpec(memory_space=pltpu.SMEM),
        _full_spec(xs.shape),
        pl.BlockSpec((DEC_SB, GLA_HEADS, GLA_DK, GLA_DV), lambda i: (i, 0, 0, 0)),
        pl.BlockSpec((DEC_SB, WINDOW, SWA_KV), lambda i: (i, 0, 0)),
        pl.BlockSpec((DEC_SB, WINDOW, SWA_KV), lambda i: (i, 0, 0)),
        _full_spec(base0.shape), _full_spec(bias_dec.shape),
    ] + [_full_spec(w.shape) for w in weight_args]
    out_specs = [
        _full_spec((n_seq, D_MODEL)),
        _full_spec((n_seq * SLAB, LANES)),
        _full_spec((8, n_seq)), _full_spec((8, n_seq)), _full_spec((8, n_seq)),
        pl.BlockSpec((DEC_SB, GLA_HEADS, GLA_DK, GLA_DV), lambda i: (i, 0, 0, 0)),
        pl.BlockSpec((DEC_SB, WINDOW, SWA_KV), lambda i: (i, 0, 0)),
        pl.BlockSpec((DEC_SB, WINDOW, SWA_KV), lambda i: (i, 0, 0)),
        _full_spec((N_EXPERTS, LANES)),
    ]
    out_shape = [
        jax.ShapeDtypeStruct((n_seq, D_MODEL), F32),
        jax.ShapeDtypeStruct((n_seq * SLAB, LANES), jnp.int32),
        jax.ShapeDtypeStruct((8, n_seq), jnp.int32),
        jax.ShapeDtypeStruct((8, n_seq), F32),
        jax.ShapeDtypeStruct((8, n_seq), jnp.int32),
        jax.ShapeDtypeStruct(state.shape, F32),
        jax.ShapeDtypeStruct(ck.shape, F32),
        jax.ShapeDtypeStruct(cv.shape, F32),
        jax.ShapeDtypeStruct((N_EXPERTS, LANES), F32),
    ]
    scratch = [pltpu.VMEM((GLA_QK, n_seq), F32)] * 3 + [
        pltpu.VMEM((n_seq, GLA_V), F32), pltpu.VMEM((n_seq, GLA_V), F32), pltpu.VMEM((n_seq, SWA_Q), F32),
        pltpu.VMEM((n_seq, SWA_KV), F32), pltpu.VMEM((n_seq, SWA_KV), F32),
        pltpu.VMEM((n_seq, GLA_V), F32), pltpu.VMEM((n_seq, SWA_Q), F32)]
    return pl.pallas_call(
        _decode_kernel,
        grid=(nb,),
        in_specs=in_specs,
        out_specs=out_specs,
        out_shape=out_shape,
        scratch_shapes=scratch,
        compiler_params=pltpu.CompilerParams(dimension_semantics=("arbitrary",),
                                             vmem_limit_bytes=VMEM_LIMIT),
        name="decode",
    )(wts["sinks"], xs, state, ck, cv, base0, bias_dec, *weight_args)


SC_CORES = 2
SC_SUBCORES = 16
SC_WORKERS = SC_CORES * SC_SUBCORES
SC_SCATTER_ROWS = 64
SC_GATHER_ROWS = 48


def _sc_mesh():
    return plsc.VectorSubcoreMesh(core_axis_name="c", subcore_axis_name="s")


def _sc_worker_id():
    return lax.axis_index("s") * SC_CORES + lax.axis_index("c")


def _sc_scatter_rows(src_p, src_s, idx_p, idx_s, n_out):
    rows = SC_SCATTER_ROWS
    n_chunks = idx_p.shape[0] // SC_WORKERS
    n_s, _, rows_s = idx_s.shape
    assert n_chunks * SC_WORKERS == idx_p.shape[0] and n_chunks % 2 == 0 and n_s <= SC_WORKERS

    @functools.partial(
        pl.kernel, mesh=_sc_mesh(),
        out_type=jax.ShapeDtypeStruct((n_out, SLAB, LANES), jnp.int32),
        scratch_types=[pltpu.VMEM((2, TOP_K, rows), jnp.int32), pltpu.VMEM((2, rows, SLAB, LANES), jnp.int32),
                       pltpu.VMEM((TOP_K, rows_s), jnp.int32), pltpu.VMEM((rows_s, SLAB, LANES), jnp.int32),
                       pltpu.SemaphoreType.DMA((2,)), pltpu.SemaphoreType.DMA((2,))])
    def scatter_rows(srcp_hbm, srcs_hbm, idxp_hbm, idxs_hbm, out_hbm, idx_v, rows_v, idxs_v, rowss_v, lsem, ssem):
        wid = _sc_worker_id()

        def loads(c, b):
            g = wid * n_chunks + c
            return (pltpu.make_async_copy(idxp_hbm.at[g], idx_v.at[b], lsem.at[b]),
                    pltpu.make_async_copy(srcp_hbm.at[pl.ds(pl.multiple_of(g * rows, 8), rows)], rows_v.at[b],
                                          lsem.at[b]))

        def scatters(b):
            return [pltpu.make_async_copy(rows_v.at[b], out_hbm.at[idx_v.at[b, k]], ssem.at[b])
                    for k in range(TOP_K)]

        for d in loads(0, 0):
            d.start()

        @pl.loop(0, n_chunks, step=2)
        def _(c0):
            for b in range(2):
                c = c0 + b
                for d in loads(c, b):
                    d.wait()

                @pl.when(c >= 1)
                def _():
                    for d in scatters(1 - b):
                        d.wait()

                @pl.when(c + 1 < n_chunks)
                def _():
                    for d in loads(c + 1, 1 - b):
                        d.start()

                for d in scatters(b):
                    d.start()

        for d in scatters((n_chunks - 1) % 2):
            d.wait()

        @pl.when(wid < n_s)
        def _():
            pltpu.sync_copy(idxs_hbm.at[wid], idxs_v)
            pltpu.sync_copy(srcs_hbm.at[pl.ds(pl.multiple_of(wid * rows_s, 8), rows_s)], rowss_v)
            for k in range(TOP_K):
                pltpu.sync_copy(rowss_v, out_hbm.at[idxs_v.at[k]])

    return scatter_rows(src_p, src_s, idx_p, idx_s)


def _sc_gather_rows(src3, idx2):
    rows = SC_GATHER_ROWS
    n_chunks = idx2.shape[0] // SC_WORKERS
    assert n_chunks * SC_WORKERS == idx2.shape[0] and idx2.shape[1] == rows and n_chunks % 2 == 0

    @functools.partial(
        pl.kernel, mesh=_sc_mesh(),
        out_type=jax.ShapeDtypeStruct((idx2.shape[0] * rows, SLAB, LANES), jnp.int32),
        scratch_types=[pltpu.VMEM((2, rows), jnp.int32), pltpu.VMEM((2, rows, SLAB, LANES), jnp.int32),
                       pltpu.SemaphoreType.DMA((2,)), pltpu.SemaphoreType.DMA((2,))])
    def gather_rows(src_hbm, idx_hbm, out_hbm, idx_v, rows_v, gsem, wsem):
        wid = _sc_worker_id()

        def gather(b):
            return pltpu.make_async_copy(src_hbm.at[idx_v.at[b]], rows_v.at[b], gsem.at[b])

        def write(c, b):
            base = pl.multiple_of((wid * n_chunks + c) * rows, 8)
            return pltpu.make_async_copy(rows_v.at[b], out_hbm.at[pl.ds(base, rows)], wsem.at[b])

        pltpu.sync_copy(idx_hbm.at[wid * n_chunks], idx_v.at[0])
        gather(0).start()

        @pl.loop(0, n_chunks, step=2)
        def _(c0):
            for b in range(2):
                c = c0 + b

                @pl.when(c + 1 < n_chunks)
                def _():
                    @pl.when(c >= 1)
                    def _():
                        write(c - 1, 1 - b).wait()
                    pltpu.sync_copy(idx_hbm.at[wid * n_chunks + c + 1], idx_v.at[1 - b])
                    gather(1 - b).start()

                gather(b).wait()
                write(c, b).start()

        write(n_chunks - 2, 0).wait()
        write(n_chunks - 1, 1).wait()

    return gather_rows(src3, idx2)


FF_TILE = 256


def _ffn_kernel(blk_e_ref, nused_ref, x_ref, wu_ref, bu_ref, wd_ref, bd_ref, y_ref, xbf, actbf, wu_bf, wd_bf):
    i = pl.program_id(0)
    tm = MOE_TM
    n_tiles = D_FF // FF_TILE

    @pl.when(i < nused_ref[0])
    def _():
        @pl.when((i == 0) | (blk_e_ref[i] != blk_e_ref[jnp.maximum(i - 1, 0)]))
        def _():
            wu_bf[...] = wu_ref[0].astype(BF16)
            wd_bf[...] = wd_ref[0].astype(BF16)

        for c in range(SLAB):
            lo, hi = _load_slab_chunk(x_ref, tm, c)
            xbf[:, c * LANES:(c + 1) * LANES] = lo.astype(BF16)
            xbf[:, HALF_D + c * LANES:HALF_D + (c + 1) * LANES] = hi.astype(BF16)
        for n in range(n_tiles):
            gc = slice(n * FF_TILE, (n + 1) * FF_TILE)
            lc = slice(D_FF + n * FF_TILE, D_FF + (n + 1) * FF_TILE)
            g = jnp.minimum(_dot(xbf[...], wu_bf[:, gc]) + bu_ref[0, :, gc], SWIGLU_LIMIT)
            lin = jnp.clip(_dot(xbf[...], wu_bf[:, lc]) + bu_ref[0, :, lc], -SWIGLU_LIMIT, SWIGLU_LIMIT)
            actbf[:, gc] = (g * jax.nn.sigmoid(SWIGLU_ALPHA * g) * (lin + 1.0)).astype(BF16)
        per_tile = FF_TILE // LANES
        for n in range(n_tiles // 2):
            yl = slice(n * FF_TILE, (n + 1) * FF_TILE)
            yh = slice(HALF_D + n * FF_TILE, HALF_D + (n + 1) * FF_TILE)
            y_lo = _dot(actbf[...], wd_bf[:, yl]) + bd_ref[0, :, yl]
            y_hi = _dot(actbf[...], wd_bf[:, yh]) + bd_ref[0, :, yh]
            for c in range(per_tile):
                sl = slice(c * LANES, (c + 1) * LANES)
                y_ref[pl.ds(n * per_tile + c, tm, stride=SLAB), :] = _pack_pair(y_lo[:, sl], y_hi[:, sl])

    @pl.when(i >= nused_ref[0])
    def _():
        y_ref[...] = jnp.zeros_like(y_ref)


def _ffn_call(blk_e, nused, xs2, w_up, b_up, w_down, b_down):
    n_blocks = blk_e.shape[0]
    tm = MOE_TM
    row_blk = pl.BlockSpec((tm * SLAB, LANES), lambda i, be, nu: (i, 0))
    grid_spec = pltpu.PrefetchScalarGridSpec(
        num_scalar_prefetch=2,
        grid=(n_blocks,),
        in_specs=[
            row_blk,
            pl.BlockSpec((1, D_MODEL, 2 * D_FF), lambda i, be, nu: (be[i], 0, 0)),
            pl.BlockSpec((1, 1, 2 * D_FF), lambda i, be, nu: (be[i], 0, 0)),
            pl.BlockSpec((1, D_FF, D_MODEL), lambda i, be, nu: (be[i], 0, 0)),
            pl.BlockSpec((1, 1, D_MODEL), lambda i, be, nu: (be[i], 0, 0)),
        ],
        out_specs=row_blk,
        scratch_shapes=[pltpu.VMEM((tm, D_MODEL), BF16), pltpu.VMEM((tm, D_FF), BF16),
                        pltpu.VMEM((D_MODEL, 2 * D_FF), BF16), pltpu.VMEM((D_FF, D_MODEL), BF16)],
    )
    return pl.pallas_call(
        _ffn_kernel,
        grid_spec=grid_spec,
        out_shape=jax.ShapeDtypeStruct((n_blocks * tm * SLAB, LANES), jnp.int32),
        compiler_params=pltpu.CompilerParams(dimension_semantics=("arbitrary",),
                                             vmem_limit_bytes=VMEM_LIMIT),
        name="experts",
    )(blk_e, nused, xs2, w_up, b_up.reshape(N_EXPERTS, 1, 2 * D_FF), w_down, b_down.reshape(N_EXPERTS, 1, D_MODEL))


def _combine_kernel(ys0_ref, ys1_ref, ys2_ref, ys3_ref, xmid_ref, gate_ref, g_final_ref, y_ref):
    tm = xmid_ref.shape[0]
    gts = jnp.transpose(jnp.concatenate([gate_ref[...], jnp.zeros((LANES - 8, tm), F32)], axis=0))
    lows, highs = [], []
    for c in range(SLAB):
        acc_lo = xmid_ref[:, c * LANES:(c + 1) * LANES]
        acc_hi = xmid_ref[:, HALF_D + c * LANES:HALF_D + (c + 1) * LANES]
        for k, ys_ref in enumerate((ys0_ref, ys1_ref, ys2_ref, ys3_ref)):
            lo, hi = _load_slab_chunk(ys_ref, tm, c)
            acc_lo = acc_lo + lo * gts[:, k:k + 1]
            acc_hi = acc_hi + hi * gts[:, k:k + 1]
        lows.append(acc_lo)
        highs.append(acc_hi)
    y_ref[...] = _rms(jnp.concatenate(lows + highs, axis=1), g_final_ref[...])


def _combine_call(ys4, t_stride, row0, x_mid, gates, g_final, tm):
    T = x_mid.shape[0]
    blk0 = row0 // tm
    per_k = t_stride // tm
    assert per_k * tm == t_stride and blk0 * tm == row0

    def ys_spec(k):
        return pl.BlockSpec((tm * SLAB, LANES), lambda i: (k * per_k + blk0 + i, 0))

    return pl.pallas_call(
        _combine_kernel,
        grid=(T // tm,),
        in_specs=[
            ys_spec(0), ys_spec(1), ys_spec(2), ys_spec(3),
            pl.BlockSpec((tm, D_MODEL), lambda i: (i, 0)),
            pl.BlockSpec((8, tm), lambda i: (0, i)),
            _full_spec((1, D_MODEL)),
        ],
        out_specs=pl.BlockSpec((tm, D_MODEL), lambda i: (i, 0)),
        out_shape=jax.ShapeDtypeStruct((T, D_MODEL), F32),
        compiler_params=pltpu.CompilerParams(dimension_semantics=("arbitrary",),
                                             vmem_limit_bytes=VMEM_LIMIT),
        name="combine",
    )(ys4, ys4, ys4, ys4, x_mid, gates, g_final)


def _t5_bucket(dist):
    n = jnp.maximum(dist, 0)
    max_exact = NUM_BUCKETS // 2
    nf = jnp.maximum(n, 1).astype(F32)
    large = max_exact + (jnp.log(nf / max_exact) / math.log(MAX_DISTANCE / max_exact)
                         * (NUM_BUCKETS - max_exact)).astype(jnp.int32)
    large = jnp.minimum(large, NUM_BUCKETS - 1)
    return jnp.where(n < max_exact, n, large)


def kernel(x_prompt, x_sample, state_gla, cache_swa_k, cache_swa_v, meta_tokens, rel_bias_table,
           g_mix, w_in, w_a_up, b_a, g_gla_out, g_swa_out, attn_sinks, w_out,
           g_ffn, w_router, b_router, w_up, b_up, w_down, b_down, g_final):
    assert g_mix.shape[0] == 1, "single-layer trunk"
    B, L, _ = x_prompt.shape
    n_seq = x_sample.shape[0]
    TP = B * L
    T_all = TP + n_seq

    wi = w_in[0]
    sizes = (GLA_QK, GLA_QK, GLA_V, GLA_V, GLA_LOWRANK, SWA_Q, SWA_KV, SWA_KV)
    offs = [0]
    for s in sizes:
        offs.append(offs[-1] + s)
    seg = [wi[:, offs[n]:offs[n + 1]] for n in range(8)]
    w_in_r = jnp.concatenate(
        seg[0:4] + seg[5:8] + [seg[4], jnp.zeros((D_MODEL, LANES - GLA_LOWRANK), F32)], axis=1).astype(BF16)
    w_a_pad = jnp.concatenate([w_a_up[0], jnp.zeros((LANES - GLA_LOWRANK, GLA_QK), F32)], axis=0).astype(BF16)
    wr_t = jnp.transpose(w_router[0])
    wr_hi = wr_t.astype(BF16)
    wr_lo = (wr_t - wr_hi.astype(F32)).astype(BF16)
    qi = jnp.arange(WINDOW)[:, None]
    kj = jnp.arange(2 * WINDOW)[None, :]
    buckets = jnp.arange(NUM_BUCKETS)
    table = rel_bias_table.astype(F32)
    oh_p = (_t5_bucket(qi - kj + WINDOW)[..., None] == buckets).astype(F32)
    bias_p = jnp.einsum("qkb,bh->hkq", oh_p, table, precision=lax.Precision.HIGHEST)
    bias_p = bias_p.reshape(SWA_KV_HEADS, SWA_GROUP, 2 * WINDOW, WINDOW).transpose(0, 2, 1, 3)
    bias_p = bias_p.reshape(SWA_KV_HEADS, 2 * WINDOW, SWA_GROUP * WINDOW)
    oh_d = (_t5_bucket(WINDOW - 1 - jnp.arange(WINDOW))[:, None] == buckets).astype(F32)
    bias_d = jnp.einsum("rb,bh->hr", oh_d, table, precision=lax.Precision.HIGHEST)
    bias_d = jnp.concatenate([bias_d, jnp.zeros((8, WINDOW), F32)], axis=0)
    wts = dict(
        sinks=attn_sinks[0].astype(F32), bias=bias_p,
        g_mix=g_mix[0][None], w_in=w_in_r, w_a_up=w_a_pad, b_a=b_a[0][None],
        g_gla=g_gla_out[0][None], g_swa=g_swa_out[0][None], w_out=w_out[0].astype(BF16),
        g_ffn=g_ffn[0][None], w_r=jnp.concatenate([wr_hi, wr_lo], axis=0), b_r=b_router[0][:, None],
    )

    x_pre = jnp.concatenate([jnp.zeros((WINDOW - N_META, D_MODEL), F32), meta_tokens.astype(F32)], axis=0)[None]
    zeros_s = jnp.zeros((GLA_QK, GLA_V), F32)
    zeros_kv = jnp.zeros((WINDOW, SWA_KV), F32)
    zeros_b = jnp.zeros((N_EXPERTS, LANES), F32)
    pre = _mixer_call(x_pre, zeros_s, zeros_kv, zeros_kv, zeros_b, wts, WINDOW, WINDOW - N_META, 0, WINDOW)
    s_meta, k_meta, v_meta = pre[5][0], pre[6][0], pre[7][0]

    (xmid_p, hp_p, topi_p, gate_p, rank_p, s_p, k_p, v_p, cnt_p) = _mixer_call(
        x_prompt, s_meta, k_meta, v_meta, zeros_b, wts, MIX_TM, 0, WINDOW - N_META, TP)

    (xmid_s, hp_s, topi_s, gate_s, rank_s, st_s, ck_s, cv_s, cnt_all) = _decode_call(
        x_sample[:, 0], state_gla[0], cache_swa_k[0].reshape(n_seq, WINDOW, SWA_KV),
        cache_swa_v[0].reshape(n_seq, WINDOW, SWA_KV), cnt_p, bias_d, wts)

    tm = MOE_TM
    n_slots = T_all * TOP_K
    n_blocks = -(-n_slots // tm) + N_EXPERTS
    top_e = jnp.concatenate([topi_p[:TOP_K], topi_s[:TOP_K]], axis=1)
    rank = jnp.concatenate([rank_p[:TOP_K], rank_s[:TOP_K]], axis=1)
    counts = cnt_all[:, 0].astype(jnp.int32)
    padded = (counts + tm - 1) // tm * tm
    pad_end = jnp.cumsum(padded)
    pad_start = pad_end - padded
    e_ids = jnp.arange(N_EXPERTS, dtype=jnp.int32)
    dest = jnp.sum(jnp.where(top_e[..., None] == e_ids, pad_start, 0), axis=-1) + rank
    n_pad = n_blocks * tm
    blk_e = jnp.minimum(jnp.sum(pad_end[None] <= (jnp.arange(n_blocks, dtype=jnp.int32) * tm)[:, None], axis=1),
                        N_EXPERTS - 1).astype(jnp.int32)
    nused = (pad_end[-1] // tm).astype(jnp.int32).reshape(1)

    sample_rows = 8
    idx_p = dest[:, :TP].reshape(TOP_K, TP // SC_SCATTER_ROWS, SC_SCATTER_ROWS).transpose(1, 0, 2)
    idx_s = dest[:, TP:].reshape(TOP_K, n_seq // sample_rows, sample_rows).transpose(1, 0, 2)
    xs3 = _sc_scatter_rows(hp_p.reshape(TP, SLAB, LANES), hp_s.reshape(n_seq, SLAB, LANES), idx_p, idx_s, n_pad)
    ys2 = _ffn_call(blk_e, nused, xs3.reshape(-1, LANES), w_up[0], b_up[0], w_down[0], b_down[0])
    unit = math.lcm(2 * SC_WORKERS * SC_GATHER_ROWS // TOP_K, MIX_TM)
    t_stride = -(-T_all // unit) * unit
    filler = jnp.arange(TOP_K * (t_stride - T_all), dtype=jnp.int32).reshape(TOP_K, t_stride - T_all)
    slot_src = jnp.concatenate([dest, filler], axis=1)
    slot_src = slot_src.reshape(TOP_K * t_stride // SC_GATHER_ROWS, SC_GATHER_ROWS)
    ys4 = _sc_gather_rows(ys2.reshape(-1, SLAB, LANES), slot_src).reshape(-1, LANES)

    gf = g_final[None]
    y_p = _combine_call(ys4, t_stride, 0, xmid_p.reshape(TP, D_MODEL), gate_p, gf, MIX_TM)
    y_s = _combine_call(ys4, t_stride, TP, xmid_s, gate_s, gf, n_seq)

    s_heads = jnp.stack([s_p[:, h * GLA_DK:(h + 1) * GLA_DK, h * GLA_DV:(h + 1) * GLA_DV]
                         for h in range(GLA_HEADS)], axis=1)
    return (y_p.reshape(B, L, D_MODEL), y_s.reshape(n_seq, 1, D_MODEL), s_heads[None],
            k_p.reshape(1, B, WINDOW, SWA_KV_HEADS, SWA_HEAD_DIM),
            v_p.reshape(1, B, WINDOW, SWA_KV_HEADS, SWA_HEAD_DIM),
            st_s[None], ck_s.reshape(1, n_seq, WINDOW, SWA_KV_HEADS, SWA_HEAD_DIM),
            cv_s.reshape(1, n_seq, WINDOW, SWA_KV_HEADS, SWA_HEAD_DIM))
```

```python
import functools
import math

import jax
import jax.numpy as jnp
from jax import lax
from jax.experimental import pallas as pl
from jax.experimental.pallas import tpu as pltpu
from jax.experimental.pallas import tpu_sc as plsc

D_MODEL = 1024
N_META = 16
GLA_HEADS = 4
GLA_DK = 64
GLA_DV = 128
GLA_LOWRANK = 16
GLA_GATE_TAU = 16.0
GLA_CHUNK = 64
SWA_HEADS = 8
SWA_KV_HEADS = 2
SWA_HEAD_DIM = 64
SWA_GROUP = SWA_HEADS // SWA_KV_HEADS
WINDOW = 128
NUM_BUCKETS = 32
MAX_DISTANCE = 128
N_EXPERTS = 32
TOP_K = 4
D_FF = 1024
SWIGLU_ALPHA = 1.702
SWIGLU_LIMIT = 7.0
RMS_EPS = 1e-6

GLA_QK = GLA_HEADS * GLA_DK
GLA_V = GLA_HEADS * GLA_DV
SWA_Q = SWA_HEADS * SWA_HEAD_DIM
SWA_KV = SWA_KV_HEADS * SWA_HEAD_DIM
LANES = 128
C_GQ, C_GK, C_GV, C_GR = 0, GLA_QK, 2 * GLA_QK, 2 * GLA_QK + GLA_V
C_SQ = C_GR + GLA_V
C_SK = C_SQ + SWA_Q
C_SV = C_SK + SWA_KV
C_GA = C_SV + SWA_KV
D_PROJ = C_GA + LANES

MIX_TM = 512
MOE_TM = 512
DEC_SB = 16
VMEM_LIMIT = 56 * 1024 * 1024

F32 = jnp.float32
BF16 = jnp.bfloat16
NEG_INF = float("-inf")


def _dot(a, b):
    return jnp.dot(a, b, preferred_element_type=F32)


def _dot_nt(a, b):
    return lax.dot_general(a, b, (((1,), (1,)), ((), ())), preferred_element_type=F32)


def _split3(x):
    hi = x.astype(BF16)
    r1 = x - hi.astype(F32)
    mid = r1.astype(BF16)
    lo = (r1 - mid.astype(F32)).astype(BF16)
    return hi, mid, lo


def _rms(x, g):
    return x * lax.rsqrt(jnp.mean(x * x, axis=-1, keepdims=True) + RMS_EPS) * g


def _iota(shape, dim):
    return lax.broadcasted_iota(jnp.int32, shape, dim)


HALF_D = D_MODEL // 2
SLAB = HALF_D // LANES


def _pack_pair(lo, hi):
    bl = lax.bitcast_convert_type(lo.astype(BF16).astype(F32), jnp.uint32)
    bh = lax.bitcast_convert_type(hi.astype(BF16).astype(F32), jnp.uint32)
    return lax.bitcast_convert_type(bh | lax.shift_right_logical(bl, jnp.uint32(16)), jnp.int32)


def _unpack_pair(w):
    u = lax.bitcast_convert_type(w, jnp.uint32)
    lo = lax.bitcast_convert_type(lax.shift_left(u, jnp.uint32(16)), F32)
    hi = lax.bitcast_convert_type(u & jnp.uint32(0xFFFF0000), F32)
    return lo, hi


def _store_slabs(ref, x):
    rows = x.shape[0]
    for c in range(SLAB):
        sl = slice(c * LANES, (c + 1) * LANES)
        ref[pl.ds(c, rows, stride=SLAB), :] = _pack_pair(x[:, sl], x[:, HALF_D + c * LANES:HALF_D + (c + 1) * LANES])


def _load_slab_chunk(ref, rows, c):
    return _unpack_pair(ref[pl.ds(c, rows, stride=SLAB), :])


def _project(x, g_mix, w_in_ref, w_a_up, b_a):
    h = _rms(x, g_mix).astype(BF16)

    def cols(lo, width):
        return _dot(h, w_in_ref[:, lo:lo + width])

    ga = cols(C_GA, LANES).astype(BF16)
    z = _dot(ga, w_a_up) + b_a
    log_a = -(jnp.maximum(-z, 0.0) + jnp.log(1.0 + jnp.exp(-jnp.abs(z)))) / GLA_GATE_TAU
    gqk = cols(C_GQ, 2 * GLA_QK)
    gv = cols(C_GV, GLA_V)
    swa = cols(C_SQ, SWA_Q + 2 * SWA_KV)
    gr = cols(C_GR, GLA_V)
    return dict(
        gq=gqk[:, :GLA_QK],
        gk=gqk[:, GLA_QK:],
        gv=gv,
        gr=gr,
        sq=swa[:, :SWA_Q],
        sk=swa[:, SWA_Q:SWA_Q + SWA_KV],
        sv=swa[:, SWA_Q + SWA_KV:],
        log_a=log_a,
    )


def _tail(x, o_gla, gr, o_swa, g_gla_out, g_swa_out, w_out_ref, g_ffn, w_r, b_r, base):
    tm = x.shape[0]
    gate = gr * jax.nn.sigmoid(gr)
    parts = []
    for h in range(GLA_HEADS):
        sl = slice(h * GLA_DV, (h + 1) * GLA_DV)
        parts.append(_rms(o_gla[:, sl], g_gla_out) * gate[:, sl])
    og = jnp.concatenate(parts, axis=1).astype(BF16)
    os_ = _rms(o_swa, g_swa_out).astype(BF16)
    x_mid = x + _dot(og, w_out_ref[0:GLA_V]) + _dot(os_, w_out_ref[GLA_V:GLA_V + SWA_Q])
    hp = _rms(x_mid, g_ffn)

    h1 = hp.astype(BF16)
    h2 = (hp - h1.astype(F32)).astype(BF16)
    la = _dot_nt(w_r, h1)
    lb = _dot_nt(w_r[0:N_EXPERTS], h2)
    logits = la[0:N_EXPERTS] + la[N_EXPERTS:2 * N_EXPERTS] + lb + b_r

    eidx = _iota((N_EXPERTS, tm), 0)
    vals, idxs, onehots = [], [], []
    l = logits
    for _ in range(TOP_K):
        m = jnp.max(l, axis=0, keepdims=True)
        sel = jnp.min(jnp.where(l == m, eidx, N_EXPERTS), axis=0, keepdims=True)
        oh = eidx == sel
        l = jnp.where(oh, NEG_INF, l)
        vals.append(m)
        idxs.append(sel)
        onehots.append(oh)
    es = [jnp.exp(v - vals[0]) for v in vals]
    denom = es[0] + es[1] + es[2] + es[3]
    gates = [e / denom for e in es]

    ohf = jnp.concatenate([oh.astype(F32) for oh in onehots], axis=0)
    upper = (_iota((tm, tm), 0) < _iota((tm, tm), 1)).astype(BF16)
    prefix = _dot(ohf.astype(BF16), upper)
    ranks = []
    for k in range(TOP_K):
        sl = slice(k * N_EXPERTS, (k + 1) * N_EXPERTS)
        ohk = ohf[sl]
        base_t = jnp.concatenate([base] * (tm // LANES), axis=1)
        ranks.append(jnp.sum(ohk * (prefix[sl] + base_t), axis=0, keepdims=True))
        base = base + jnp.sum(ohk, axis=1, keepdims=True)
    zi = jnp.zeros((8 - TOP_K, tm), jnp.int32)
    zf = jnp.zeros((8 - TOP_K, tm), F32)
    topi = jnp.concatenate(idxs + [zi], axis=0)
    gate8 = jnp.concatenate(gates + [zf], axis=0)
    rank8 = jnp.concatenate([r.astype(jnp.int32) for r in ranks] + [zi], axis=0)
    return x_mid, hp, topi, gate8, rank8, base


def _gla_chunks(p, row0, s_blocks, n_lead_pad):
    tm = p["gq"].shape[0]
    nch = tm // GLA_CHUNK
    log_a = p["log_a"]
    if n_lead_pad:
        rows = row0 + _iota((tm, GLA_QK), 0)
        log_a = jnp.where(rows >= n_lead_pad, log_a, 0.0)
    ri, ci = _iota((tm, tm), 0), _iota((tm, tm), 1)
    tril = ((ri >= ci) & (ri // GLA_CHUNK == ci // GLA_CHUNK)).astype(BF16)
    hi, mid, lo = _split3(log_a)
    b_all = _dot(tril, hi) + _dot(tril, mid) + _dot(tril, lo)

    c64 = GLA_CHUNK
    kk_mask = (_iota((GLA_QK, GLA_QK), 0) // c64) == (_iota((GLA_QK, GLA_QK), 1) // GLA_DK)
    vv_mask = (_iota((GLA_QK, GLA_V), 0) // c64) == (_iota((GLA_QK, GLA_V), 1) // GLA_DV)
    zero_blk = jnp.zeros((GLA_DK, GLA_DV), BF16)
    causal = (_iota((c64, GLA_QK), 0) >= (_iota((c64, GLA_QK), 1) % c64)).astype(F32)
    zpad_k = jnp.zeros((LANES - c64, GLA_QK), F32)
    zpad_v = jnp.zeros((LANES - c64, GLA_V), BF16)

    outs = []
    for c in range(nch):
        rs = slice(c * c64, (c + 1) * c64)
        b = b_all[rs]
        q, k, v = p["gq"][rs], p["gk"][rs], p["gv"][rs]
        b_last = b[c64 - 1:c64]
        qt = (q * jnp.exp(b)).astype(BF16)
        kt = k * jnp.exp(-b)
        kd = k * jnp.exp(b_last - b)
        vb = v.astype(BF16)
        k_bd = jnp.where(kk_mask, jnp.concatenate([kt] * GLA_HEADS, axis=0), 0.0).astype(BF16)
        a = (_dot_nt(qt, k_bd) * causal).astype(BF16)
        v_bd = jnp.where(vv_mask, jnp.concatenate([vb] * GLA_HEADS, axis=0), jnp.zeros((), BF16))
        s_bd = jnp.concatenate(
            [jnp.concatenate([s_blocks[h].astype(BF16) if g == h else zero_blk for g in range(GLA_HEADS)], axis=1)
             for h in range(GLA_HEADS)], axis=0)
        outs.append(_dot(qt, s_bd) + _dot(a, v_bd))
        kd_t = jnp.transpose(jnp.concatenate([kd, zpad_k], axis=0)).astype(BF16)
        upd = _dot(kd_t, jnp.concatenate([vb, zpad_v], axis=0))
        decay = jnp.exp(jnp.transpose(jnp.broadcast_to(b_last, (LANES, GLA_QK))))
        s_blocks = [s_blocks[h] * decay[h * GLA_DK:(h + 1) * GLA_DK]
                    + upd[h * GLA_DK:(h + 1) * GLA_DK, h * GLA_DV:(h + 1) * GLA_DV] for h in range(GLA_HEADS)]
    return jnp.concatenate(outs, axis=0), s_blocks


def _swa_block(sq, kcat, vcat, bias_ref, sinks_ref, valid_t):
    half = _iota((1, LANES), 1) < SWA_HEAD_DIM
    top_rows = _iota((LANES, 1), 0) < SWA_HEAD_DIM
    k_roll = pltpu.roll(kcat, SWA_HEAD_DIM, 1)
    v_t = jnp.transpose(vcat)
    zeros_v = jnp.zeros((SWA_HEAD_DIM, 2 * WINDOW), F32)
    cols = []
    for kv in range(SWA_KV_HEADS):
        kk = jnp.where(half, kcat, k_roll) if kv == 0 else jnp.where(half, k_roll, kcat)
        q_parts = []
        for c in (2 * kv, 2 * kv + 1):
            qc = sq[:, c * LANES:(c + 1) * LANES]
            q_parts.append(jnp.where(half, qc, 0.0))
            q_parts.append(jnp.where(half, 0.0, qc))
        q_st = jnp.concatenate(q_parts, axis=0).astype(BF16)
        s = _dot_nt(kk.astype(BF16), q_st) + bias_ref[kv]
        if valid_t is not None:
            s = jnp.where(valid_t, s, NEG_INF)
        sink = jnp.concatenate(
            [jnp.full((1, WINDOW), sinks_ref[kv * SWA_GROUP + g], F32) for g in range(SWA_GROUP)], axis=1)
        m = jnp.maximum(jnp.max(s, axis=0, keepdims=True), sink)
        pr = jnp.exp(s - m)
        inv = 1.0 / (jnp.sum(pr, axis=0, keepdims=True) + jnp.exp(sink - m))
        pb = pr.astype(BF16)
        vk = v_t[kv * SWA_HEAD_DIM:(kv + 1) * SWA_HEAD_DIM]
        vv_t = jnp.concatenate([jnp.concatenate([vk, zeros_v], axis=1),
                                jnp.concatenate([zeros_v, vk], axis=1)], axis=0).astype(BF16)
        for pair in range(SWA_GROUP // 2):
            ce = slice(2 * pair * WINDOW, (2 * pair + 1) * WINDOW)
            co = slice((2 * pair + 1) * WINDOW, (2 * pair + 2) * WINDOW)
            p2_t = jnp.concatenate([pb[:, ce], pb[:, co]], axis=0)
            o2_t = _dot(vv_t, p2_t)
            o2_t = o2_t * jnp.where(top_rows, inv[:, ce], inv[:, co])
            cols.append(jnp.transpose(o2_t))
    return jnp.concatenate(cols, axis=1)


def _mixer_kernel(sinks_ref, x_ref, s0_ref, k0_ref, v0_ref, base0_ref, bias_ref,
                  g_mix_ref, w_in_ref, w_a_up_ref, b_a_ref, g_gla_ref, g_swa_ref, w_out_ref,
                  g_ffn_ref, w_r_ref, b_r_ref,
                  xmid_ref, hp_ref, topi_ref, gate_ref, rank_ref, sout_ref, kout_ref, vout_ref, cnt_ref,
                  s_scr, k_scr, v_scr, base_scr, *, n_lead_pad, prev_valid_from):
    b_id, j = pl.program_id(0), pl.program_id(1)
    tm = x_ref.shape[1]

    @pl.when(j == 0)
    def _():
        s_scr[...] = s0_ref[...]
        k_scr[...] = k0_ref[...]
        v_scr[...] = v0_ref[...]

    @pl.when((j == 0) & (b_id == 0))
    def _():
        base_scr[...] = base0_ref[...]

    x = x_ref[0]
    p = _project(x, g_mix_ref[...], w_in_ref, w_a_up_ref[...], b_a_ref[...])

    diag = [(slice(h * GLA_DK, (h + 1) * GLA_DK), slice(h * GLA_DV, (h + 1) * GLA_DV)) for h in range(GLA_HEADS)]
    o_gla, s_blocks = _gla_chunks(p, j * tm, [s_scr[r, c] for r, c in diag], n_lead_pad)
    for (r, c), blk in zip(diag, s_blocks):
        s_scr[r, c] = blk

    o_parts = []
    for sb in range(tm // WINDOW):
        rs = slice(sb * WINDOW, (sb + 1) * WINDOW)
        k_blk, v_blk = p["sk"][rs], p["sv"][rs]
        k_prev = k_scr[...] if sb == 0 else p["sk"][(sb - 1) * WINDOW:sb * WINDOW]
        v_prev = v_scr[...] if sb == 0 else p["sv"][(sb - 1) * WINDOW:sb * WINDOW]
        valid = None
        if sb == 0 and prev_valid_from:
            first = jnp.where(j == 0, prev_valid_from, 0)
            valid = _iota((2 * WINDOW, SWA_GROUP * WINDOW), 0) >= first
        o_parts.append(_swa_block(p["sq"][rs], jnp.concatenate([k_prev, k_blk], axis=0),
                                  jnp.concatenate([v_prev, v_blk], axis=0), bias_ref, sinks_ref, valid))
    o_swa = jnp.concatenate(o_parts, axis=0)
    k_scr[...] = p["sk"][tm - WINDOW:tm]
    v_scr[...] = p["sv"][tm - WINDOW:tm]

    x_mid, hp, topi, gate8, rank8, base = _tail(
        x, o_gla, p["gr"], o_swa, g_gla_ref[...], g_swa_ref[...], w_out_ref,
        g_ffn_ref[...], w_r_ref[...], b_r_ref[...], base_scr[...])
    base_scr[...] = base
    xmid_ref[0] = x_mid
    _store_slabs(hp_ref, hp)
    topi_ref[...] = topi
    gate_ref[...] = gate8
    rank_ref[...] = rank8
    sout_ref[0] = s_scr[...]
    kout_ref[0] = p["sk"][tm - WINDOW:tm]
    vout_ref[0] = p["sv"][tm - WINDOW:tm]
    cnt_ref[...] = base


def _full_spec(shape):
    nd = len(shape)
    return pl.BlockSpec(shape, lambda *_: (0,) * nd)


def _mixer_call(x, s0, k0, v0, base0, wts, tm, n_lead_pad, prev_valid_from, hp_rows):
    B, L, _ = x.shape
    nj = L // tm
    T = B * L
    weight_args = (wts["bias"], wts["g_mix"], wts["w_in"], wts["w_a_up"], wts["b_a"], wts["g_gla"],
                   wts["g_swa"], wts["w_out"], wts["g_ffn"], wts["w_r"], wts["b_r"])
    in_specs = [
        pl.BlockSpec(memory_space=pltpu.SMEM),
        pl.BlockSpec((1, tm, D_MODEL), lambda b, j: (b, j, 0)),
        _full_spec(s0.shape), _full_spec(k0.shape), _full_spec(v0.shape), _full_spec(base0.shape),
    ] + [_full_spec(w.shape) for w in weight_args]
    tok_spec = pl.BlockSpec((8, tm), lambda b, j: (0, b * nj + j))
    out_specs = [
        pl.BlockSpec((1, tm, D_MODEL), lambda b, j: (b, j, 0)),
        pl.BlockSpec((tm * SLAB, LANES), lambda b, j: (b * nj + j, 0)),
        tok_spec, tok_spec, tok_spec,
        pl.BlockSpec((1, GLA_QK, GLA_V), lambda b, j: (b, 0, 0)),
        pl.BlockSpec((1, WINDOW, SWA_KV), lambda b, j: (b, 0, 0)),
        pl.BlockSpec((1, WINDOW, SWA_KV), lambda b, j: (b, 0, 0)),
        _full_spec((N_EXPERTS, LANES)),
    ]
    out_shape = [
        jax.ShapeDtypeStruct((B, L, D_MODEL), F32),
        jax.ShapeDtypeStruct((hp_rows * SLAB, LANES), jnp.int32),
        jax.ShapeDtypeStruct((8, T), jnp.int32),
        jax.ShapeDtypeStruct((8, T), F32),
        jax.ShapeDtypeStruct((8, T), jnp.int32),
        jax.ShapeDtypeStruct((B, GLA_QK, GLA_V), F32),
        jax.ShapeDtypeStruct((B, WINDOW, SWA_KV), F32),
        jax.ShapeDtypeStruct((B, WINDOW, SWA_KV), F32),
        jax.ShapeDtypeStruct((N_EXPERTS, LANES), F32),
    ]
    kern = functools.partial(_mixer_kernel, n_lead_pad=n_lead_pad, prev_valid_from=prev_valid_from)
    return pl.pallas_call(
        kern,
        grid=(B, nj),
        in_specs=in_specs,
        out_specs=out_specs,
        out_shape=out_shape,
        scratch_shapes=[pltpu.VMEM((GLA_QK, GLA_V), F32), pltpu.VMEM((WINDOW, SWA_KV), F32),
                        pltpu.VMEM((WINDOW, SWA_KV), F32), pltpu.VMEM((N_EXPERTS, LANES), F32)],
        compiler_params=pltpu.CompilerParams(dimension_semantics=("arbitrary", "arbitrary"),
                                             vmem_limit_bytes=VMEM_LIMIT),
        name="mixer",
    )(wts["sinks"], x, s0, k0, v0, base0, *weight_args)


def _decode_kernel(sinks_ref, x_ref, st_ref, ck_ref, cv_ref, base0_ref, bias_ref,
                   g_mix_ref, w_in_ref, w_a_up_ref, b_a_ref, g_gla_ref, g_swa_ref, w_out_ref,
                   g_ffn_ref, w_r_ref, b_r_ref,
                   xmid_ref, hp_ref, topi_ref, gate_ref, rank_ref, sto_ref, cko_ref, cvo_ref, cnt_ref,
                   at_scr, kt_scr, qt_scr, gv_scr, gr_scr, sq_scr, sk_scr, sv_scr, og_scr, os_scr):
    i = pl.program_id(0)
    n_seq = x_ref.shape[0]

    @pl.when(i == 0)
    def _():
        p = _project(x_ref[...], g_mix_ref[...], w_in_ref, w_a_up_ref[...], b_a_ref[...])
        at_scr[...] = jnp.transpose(jnp.exp(p["log_a"]))
        kt_scr[...] = jnp.transpose(p["gk"])
        qt_scr[...] = jnp.transpose(p["gq"])
        gv_scr[...] = p["gv"]
        gr_scr[...] = p["gr"]
        sq_scr[...] = p["sq"]
        sk_scr[...] = p["sk"]
        sv_scr[...] = p["sv"]

    lane_seq = _iota((GLA_QK, n_seq), 1)
    half = _iota((1, LANES), 1) < SWA_HEAD_DIM
    row_id = _iota((WINDOW, SWA_KV), 0)
    head_diag = (_iota((16, SWA_Q), 1) // SWA_HEAD_DIM) == _iota((16, SWA_Q), 0)
    sink_col = jnp.concatenate(
        [jnp.full((1, 1), sinks_ref[h], F32) for h in range(SWA_HEADS)] + [jnp.zeros((8, 1), F32)], axis=0)

    def per_seq(sl, carry):
        s = i * DEC_SB + sl
        sel = lane_seq == s
        a_c = jnp.sum(jnp.where(sel, at_scr[...], 0.0), axis=1, keepdims=True)
        k_c = jnp.sum(jnp.where(sel, kt_scr[...], 0.0), axis=1, keepdims=True)
        q_c = jnp.sum(jnp.where(sel, qt_scr[...], 0.0), axis=1, keepdims=True)
        st = st_ref[sl].reshape(GLA_QK, GLA_DV)
        v_row = gv_scr[pl.ds(s, 1), :]
        v_b = jnp.concatenate(
            [jnp.broadcast_to(v_row[:, h * GLA_DV:(h + 1) * GLA_DV], (GLA_DK, GLA_DV))
             for h in range(GLA_HEADS)], axis=0)
        st_new = a_c * st + k_c * v_b
        sto_ref[sl] = st_new.reshape(GLA_HEADS, GLA_DK, GLA_DV)
        t = q_c * st_new
        og_scr[pl.ds(s, 1), :] = jnp.concatenate(
            [jnp.sum(t[h * GLA_DK:(h + 1) * GLA_DK], axis=0, keepdims=True) for h in range(GLA_HEADS)],
            axis=1)

        k_new = sk_scr[pl.ds(s, 1), :]
        v_new = sv_scr[pl.ds(s, 1), :]
        kn = jnp.where(row_id == WINDOW - 1, k_new, pltpu.roll(ck_ref[sl], WINDOW - 1, 0))
        vn = jnp.where(row_id == WINDOW - 1, v_new, pltpu.roll(cv_ref[sl], WINDOW - 1, 0))
        cko_ref[sl] = kn
        cvo_ref[sl] = vn
        kr, vr = pltpu.roll(kn, SWA_HEAD_DIM, 1), pltpu.roll(vn, SWA_HEAD_DIM, 1)
        k0, k1 = jnp.where(half, kn, kr), jnp.where(half, kr, kn)
        v0, v1 = jnp.where(half, vn, vr), jnp.where(half, vr, vn)
        kw = jnp.concatenate([k0, k0, k1, k1], axis=1).astype(BF16)
        vw = jnp.concatenate([v0, v0, v1, v1], axis=1).astype(BF16)
        q_row = sq_scr[pl.ds(s, 1), :]
        qm = jnp.where(head_diag, jnp.broadcast_to(q_row, (16, SWA_Q)), 0.0).astype(BF16)
        sc = _dot_nt(qm, kw) + bias_ref[...]
        m = jnp.maximum(jnp.max(sc, axis=1, keepdims=True), sink_col)
        pr = jnp.exp(sc - m)
        inv = 1.0 / (jnp.sum(pr, axis=1, keepdims=True) + jnp.exp(sink_col - m))
        ow = _dot(pr.astype(BF16), vw) * inv
        os_scr[pl.ds(s, 1), :] = jnp.sum(jnp.where(head_diag, ow, 0.0), axis=0, keepdims=True)
        return carry

    lax.fori_loop(0, DEC_SB, per_seq, 0, unroll=4)

    @pl.when(i == pl.num_programs(0) - 1)
    def _():
        x_mid, hp, topi, gate8, rank8, base = _tail(
            x_ref[...], og_scr[...], gr_scr[...], os_scr[...], g_gla_ref[...], g_swa_ref[...],
            w_out_ref, g_ffn_ref[...], w_r_ref[...], b_r_ref[...], base0_ref[...])
        xmid_ref[...] = x_mid
        _store_slabs(hp_ref, hp)
        topi_ref[...] = topi
        gate_ref[...] = gate8
        rank_ref[...] = rank8
        cnt_ref[...] = base


def _decode_call(xs, state, ck, cv, base0, bias_dec, wts):
    n_seq = xs.shape[0]
    nb = n_seq // DEC_SB
    weight_args = (wts["g_mix"], wts["w_in"], wts["w_a_up"], wts["b_a"], wts["g_gla"],
                   wts["g_swa"], wts["w_out"], wts["g_ffn"], wts["w_r"], wts["b_r"])
    in_specs = [
        pl.BlockSpec(memory_space=pltpu.SMEM),
        _full_spec(xs.shape),
        pl.BlockSpec((DEC_SB, GLA_HEADS, GLA_DK, GLA_DV), lambda i: (i, 0, 0, 0)),
        pl.BlockSpec((DEC_SB, WINDOW, SWA_KV), lambda i: (i, 0, 0)),
        pl.BlockSpec((DEC_SB, WINDOW, SWA_KV), lambda i: (i, 0, 0)),
        _full_spec(base0.shape), _full_spec(bias_dec.shape),
    ] + [_full_spec(w.shape) for w in weight_args]
    out_specs = [
        _full_spec((n_seq, D_MODEL)),
        _full_spec((n_seq * SLAB, LANES)),
        _full_spec((8, n_seq)), _full_spec((8, n_seq)), _full_spec((8, n_seq)),
        pl.BlockSpec((DEC_SB, GLA_HEADS, GLA_DK, GLA_DV), lambda i: (i, 0, 0, 0)),
        pl.BlockSpec((DEC_SB, WINDOW, SWA_KV), lambda i: (i, 0, 0)),
        pl.BlockSpec((DEC_SB, WINDOW, SWA_KV), lambda i: (i, 0, 0)),
        _full_spec((N_EXPERTS, LANES)),
    ]
    out_shape = [
        jax.ShapeDtypeStruct((n_seq, D_MODEL), F32),
        jax.ShapeDtypeStruct((n_seq * SLAB, LANES), jnp.int32),
        jax.ShapeDtypeStruct((8, n_seq), jnp.int32),
        jax.ShapeDtypeStruct((8, n_seq), F32),
        jax.ShapeDtypeStruct((8, n_seq), jnp.int32),
        jax.ShapeDtypeStruct(state.shape, F32),
        jax.ShapeDtypeStruct(ck.shape, F32),
        jax.ShapeDtypeStruct(cv.shape, F32),
        jax.ShapeDtypeStruct((N_EXPERTS, LANES), F32),
    ]
    scratch = [pltpu.VMEM((GLA_QK, n_seq), F32)] * 3 + [
        pltpu.VMEM((n_seq, GLA_V), F32), pltpu.VMEM((n_seq, GLA_V), F32), pltpu.VMEM((n_seq, SWA_Q), F32),
        pltpu.VMEM((n_seq, SWA_KV), F32), pltpu.VMEM((n_seq, SWA_KV), F32),
        pltpu.VMEM((n_seq, GLA_V), F32), pltpu.VMEM((n_seq, SWA_Q), F32)]
    return pl.pallas_call(
        _decode_kernel,
        grid=(nb,),
        in_specs=in_specs,
        out_specs=out_specs,
        out_shape=out_shape,
        scratch_shapes=scratch,
        compiler_params=pltpu.CompilerParams(dimension_semantics=("arbitrary",),
                                             vmem_limit_bytes=VMEM_LIMIT),
        name="decode",
    )(wts["sinks"], xs, state, ck, cv, base0, bias_dec, *weight_args)


SC_CORES = 2
SC_SUBCORES = 16
SC_WORKERS = SC_CORES * SC_SUBCORES
SC_SCATTER_ROWS = 64
SC_GATHER_ROWS = 48


def _sc_mesh():
    return plsc.VectorSubcoreMesh(core_axis_name="c", subcore_axis_name="s")


def _sc_worker_id():
    return lax.axis_index("s") * SC_CORES + lax.axis_index("c")


def _sc_scatter_rows(src_p, src_s, idx_p, idx_s, n_out):
    rows = SC_SCATTER_ROWS
    n_chunks = idx_p.shape[0] // SC_WORKERS
    n_s, _, rows_s = idx_s.shape
    assert n_chunks * SC_WORKERS == idx_p.shape[0] and n_chunks % 2 == 0 and n_s <= SC_WORKERS

    @functools.partial(
        pl.kernel, mesh=_sc_mesh(),
        out_type=jax.ShapeDtypeStruct((n_out, SLAB, LANES), jnp.int32),
        scratch_types=[pltpu.VMEM((2, TOP_K, rows), jnp.int32), pltpu.VMEM((2, rows, SLAB, LANES), jnp.int32),
                       pltpu.VMEM((TOP_K, rows_s), jnp.int32), pltpu.VMEM((rows_s, SLAB, LANES), jnp.int32),
                       pltpu.SemaphoreType.DMA((2,)), pltpu.SemaphoreType.DMA((2,))])
    def scatter_rows(srcp_hbm, srcs_hbm, idxp_hbm, idxs_hbm, out_hbm, idx_v, rows_v, idxs_v, rowss_v, lsem, ssem):
        wid = _sc_worker_id()

        def loads(c, b):
            g = wid * n_chunks + c
            return (pltpu.make_async_copy(idxp_hbm.at[g], idx_v.at[b], lsem.at[b]),
                    pltpu.make_async_copy(srcp_hbm.at[pl.ds(pl.multiple_of(g * rows, 8), rows)], rows_v.at[b],
                                          lsem.at[b]))

        def scatters(b):
            return [pltpu.make_async_copy(rows_v.at[b], out_hbm.at[idx_v.at[b, k]], ssem.at[b])
                    for k in range(TOP_K)]

        for d in loads(0, 0):
            d.start()

        @pl.loop(0, n_chunks, step=2)
        def _(c0):
            for b in range(2):
                c = c0 + b
                for d in loads(c, b):
                    d.wait()

                @pl.when(c >= 1)
                def _():
                    for d in scatters(1 - b):
                        d.wait()

                @pl.when(c + 1 < n_chunks)
                def _():
                    for d in loads(c + 1, 1 - b):
                        d.start()

                for d in scatters(b):
                    d.start()

        for d in scatters((n_chunks - 1) % 2):
            d.wait()

        @pl.when(wid < n_s)
        def _():
            pltpu.sync_copy(idxs_hbm.at[wid], idxs_v)
            pltpu.sync_copy(srcs_hbm.at[pl.ds(pl.multiple_of(wid * rows_s, 8), rows_s)], rowss_v)
            for k in range(TOP_K):
                pltpu.sync_copy(rowss_v, out_hbm.at[idxs_v.at[k]])

    return scatter_rows(src_p, src_s, idx_p, idx_s)


def _sc_gather_rows(src3, idx2):
    rows = SC_GATHER_ROWS
    n_chunks = idx2.shape[0] // SC_WORKERS
    assert n_chunks * SC_WORKERS == idx2.shape[0] and idx2.shape[1] == rows and n_chunks % 2 == 0

    @functools.partial(
        pl.kernel, mesh=_sc_mesh(),
        out_type=jax.ShapeDtypeStruct((idx2.shape[0] * rows, SLAB, LANES), jnp.int32),
        scratch_types=[pltpu.VMEM((2, rows), jnp.int32), pltpu.VMEM((2, rows, SLAB, LANES), jnp.int32),
                       pltpu.SemaphoreType.DMA((2,)), pltpu.SemaphoreType.DMA((2,))])
    def gather_rows(src_hbm, idx_hbm, out_hbm, idx_v, rows_v, gsem, wsem):
        wid = _sc_worker_id()

        def gather(b):
            return pltpu.make_async_copy(src_hbm.at[idx_v.at[b]], rows_v.at[b], gsem.at[b])

        def write(c, b):
            base = pl.multiple_of((wid * n_chunks + c) * rows, 8)
            return pltpu.make_async_copy(rows_v.at[b], out_hbm.at[pl.ds(base, rows)], wsem.at[b])

        pltpu.sync_copy(idx_hbm.at[wid * n_chunks], idx_v.at[0])
        gather(0).start()

        @pl.loop(0, n_chunks, step=2)
        def _(c0):
            for b in range(2):
                c = c0 + b

                @pl.when(c + 1 < n_chunks)
                def _():
                    @pl.when(c >= 1)
                    def _():
                        write(c - 1, 1 - b).wait()
                    pltpu.sync_copy(idx_hbm.at[wid * n_chunks + c + 1], idx_v.at[1 - b])
                    gather(1 - b).start()

                gather(b).wait()
                write(c, b).start()

        write(n_chunks - 2, 0).wait()
        write(n_chunks - 1, 1).wait()

    return gather_rows(src3, idx2)


FF_TILE = 256


def _ffn_kernel(blk_e_ref, nused_ref, x_ref, wu_ref, bu_ref, wd_ref, bd_ref, y_ref, xbf, actbf, wu_bf, wd_bf):
    i = pl.program_id(0)
    tm = MOE_TM
    n_tiles = D_FF // FF_TILE

    @pl.when(i < nused_ref[0])
    def _():
        @pl.when((i == 0) | (blk_e_ref[i] != blk_e_ref[jnp.maximum(i - 1, 0)]))
        def _():
            wu_bf[...] = wu_ref[0].astype(BF16)
            wd_bf[...] = wd_ref[0].astype(BF16)

        for c in range(SLAB):
            lo, hi = _load_slab_chunk(x_ref, tm, c)
            xbf[:, c * LANES:(c + 1) * LANES] = lo.astype(BF16)
            xbf[:, HALF_D + c * LANES:HALF_D + (c + 1) * LANES] = hi.astype(BF16)
        for n in range(n_tiles):
            gc = slice(n * FF_TILE, (n + 1) * FF_TILE)
            lc = slice(D_FF + n * FF_TILE, D_FF + (n + 1) * FF_TILE)
            g = jnp.minimum(_dot(xbf[...], wu_bf[:, gc]) + bu_ref[0, :, gc], SWIGLU_LIMIT)
            lin = jnp.clip(_dot(xbf[...], wu_bf[:, lc]) + bu_ref[0, :, lc], -SWIGLU_LIMIT, SWIGLU_LIMIT)
            actbf[:, gc] = (g * jax.nn.sigmoid(SWIGLU_ALPHA * g) * (lin + 1.0)).astype(BF16)
        per_tile = FF_TILE // LANES
        for n in range(n_tiles // 2):
            yl = slice(n * FF_TILE, (n + 1) * FF_TILE)
            yh = slice(HALF_D + n * FF_TILE, HALF_D + (n + 1) * FF_TILE)
            y_lo = _dot(actbf[...], wd_bf[:, yl]) + bd_ref[0, :, yl]
            y_hi = _dot(actbf[...], wd_bf[:, yh]) + bd_ref[0, :, yh]
            for c in range(per_tile):
                sl = slice(c * LANES, (c + 1) * LANES)
                y_ref[pl.ds(n * per_tile + c, tm, stride=SLAB), :] = _pack_pair(y_lo[:, sl], y_hi[:, sl])

    @pl.when(i >= nused_ref[0])
    def _():
        y_ref[...] = jnp.zeros_like(y_ref)


def _ffn_call(blk_e, nused, xs2, w_up, b_up, w_down, b_down):
    n_blocks = blk_e.shape[0]
    tm = MOE_TM
    row_blk = pl.BlockSpec((tm * SLAB, LANES), lambda i, be, nu: (i, 0))
    grid_spec = pltpu.PrefetchScalarGridSpec(
        num_scalar_prefetch=2,
        grid=(n_blocks,),
        in_specs=[
            row_blk,
            pl.BlockSpec((1, D_MODEL, 2 * D_FF), lambda i, be, nu: (be[i], 0, 0)),
            pl.BlockSpec((1, 1, 2 * D_FF), lambda i, be, nu: (be[i], 0, 0)),
            pl.BlockSpec((1, D_FF, D_MODEL), lambda i, be, nu: (be[i], 0, 0)),
            pl.BlockSpec((1, 1, D_MODEL), lambda i, be, nu: (be[i], 0, 0)),
        ],
        out_specs=row_blk,
        scratch_shapes=[pltpu.VMEM((tm, D_MODEL), BF16), pltpu.VMEM((tm, D_FF), BF16),
                        pltpu.VMEM((D_MODEL, 2 * D_FF), BF16), pltpu.VMEM((D_FF, D_MODEL), BF16)],
    )
    return pl.pallas_call(
        _ffn_kernel,
        grid_spec=grid_spec,
        out_shape=jax.ShapeDtypeStruct((n_blocks * tm * SLAB, LANES), jnp.int32),
        compiler_params=pltpu.CompilerParams(dimension_semantics=("arbitrary",),
                                             vmem_limit_bytes=VMEM_LIMIT),
        name="experts",
    )(blk_e, nused, xs2, w_up, b_up.reshape(N_EXPERTS, 1, 2 * D_FF), w_down, b_down.reshape(N_EXPERTS, 1, D_MODEL))


def _combine_kernel(ys0_ref, ys1_ref, ys2_ref, ys3_ref, xmid_ref, gate_ref, g_final_ref, y_ref):
    tm = xmid_ref.shape[0]
    gts = jnp.transpose(jnp.concatenate([gate_ref[...], jnp.zeros((LANES - 8, tm), F32)], axis=0))
    lows, highs = [], []
    for c in range(SLAB):
        acc_lo = xmid_ref[:, c * LANES:(c + 1) * LANES]
        acc_hi = xmid_ref[:, HALF_D + c * LANES:HALF_D + (c + 1) * LANES]
        for k, ys_ref in enumerate((ys0_ref, ys1_ref, ys2_ref, ys3_ref)):
            lo, hi = _load_slab_chunk(ys_ref, tm, c)
            acc_lo = acc_lo + lo * gts[:, k:k + 1]
            acc_hi = acc_hi + hi * gts[:, k:k + 1]
        lows.append(acc_lo)
        highs.append(acc_hi)
    y_ref[...] = _rms(jnp.concatenate(lows + highs, axis=1), g_final_ref[...])


def _combine_call(ys4, t_stride, row0, x_mid, gates, g_final, tm):
    T = x_mid.shape[0]
    blk0 = row0 // tm
    per_k = t_stride // tm
    assert per_k * tm == t_stride and blk0 * tm == row0

    def ys_spec(k):
        return pl.BlockSpec((tm * SLAB, LANES), lambda i: (k * per_k + blk0 + i, 0))

    return pl.pallas_call(
        _combine_kernel,
        grid=(T // tm,),
        in_specs=[
            ys_spec(0), ys_spec(1), ys_spec(2), ys_spec(3),
            pl.BlockSpec((tm, D_MODEL), lambda i: (i, 0)),
            pl.BlockSpec((8, tm), lambda i: (0, i)),
            _full_spec((1, D_MODEL)),
        ],
        out_specs=pl.BlockSpec((tm, D_MODEL), lambda i: (i, 0)),
        out_shape=jax.ShapeDtypeStruct((T, D_MODEL), F32),
        compiler_params=pltpu.CompilerParams(dimension_semantics=("arbitrary",),
                                             vmem_limit_bytes=VMEM_LIMIT),
        name="combine",
    )(ys4, ys4, ys4, ys4, x_mid, gates, g_final)


def _t5_bucket(dist):
    n = jnp.maximum(dist, 0)
    max_exact = NUM_BUCKETS // 2
    nf = jnp.maximum(n, 1).astype(F32)
    large = max_exact + (jnp.log(nf / max_exact) / math.log(MAX_DISTANCE / max_exact)
                         * (NUM_BUCKETS - max_exact)).astype(jnp.int32)
    large = jnp.minimum(large, NUM_BUCKETS - 1)
    return jnp.where(n < max_exact, n, large)


def kernel(x_prompt, x_sample, state_gla, cache_swa_k, cache_swa_v, meta_tokens, rel_bias_table,
           g_mix, w_in, w_a_up, b_a, g_gla_out, g_swa_out, attn_sinks, w_out,
           g_ffn, w_router, b_router, w_up, b_up, w_down, b_down, g_final):
    assert g_mix.shape[0] == 1, "single-layer trunk"
    B, L, _ = x_prompt.shape
    n_seq = x_sample.shape[0]
    TP = B * L
    T_all = TP + n_seq

    wi = w_in[0]
    sizes = (GLA_QK, GLA_QK, GLA_V, GLA_V, GLA_LOWRANK, SWA_Q, SWA_KV, SWA_KV)
    offs = [0]
    for s in sizes:
        offs.append(offs[-1] + s)
    seg = [wi[:, offs[n]:offs[n + 1]] for n in range(8)]
    seg[0] = seg[0] * (GLA_DK ** -0.5)
    seg[5] = seg[5] * (SWA_HEAD_DIM ** -0.5)
    w_in_r = jnp.concatenate(
        seg[0:4] + seg[5:8] + [seg[4], jnp.zeros((D_MODEL, LANES - GLA_LOWRANK), F32)], axis=1).astype(BF16)
    w_a_pad = jnp.concatenate([w_a_up[0], jnp.zeros((LANES - GLA_LOWRANK, GLA_QK), F32)], axis=0).astype(BF16)
    wr_t = jnp.transpose(w_router[0])
    wr_hi = wr_t.astype(BF16)
    wr_lo = (wr_t - wr_hi.astype(F32)).astype(BF16)
    qi = jnp.arange(WINDOW)[:, None]
    kj = jnp.arange(2 * WINDOW)[None, :]
    buckets = jnp.arange(NUM_BUCKETS)
    table = rel_bias_table.astype(F32)
    oh_p = (_t5_bucket(qi - kj + WINDOW)[..., None] == buckets).astype(F32)
    bias_p = jnp.einsum("qkb,bh->hkq", oh_p, table, precision=lax.Precision.HIGHEST)
    in_window = jnp.transpose((kj > qi) & (kj <= qi + WINDOW))
    bias_p = jnp.where(in_window[None], bias_p, NEG_INF)
    bias_p = bias_p.reshape(SWA_KV_HEADS, SWA_GROUP, 2 * WINDOW, WINDOW).transpose(0, 2, 1, 3)
    bias_p = bias_p.reshape(SWA_KV_HEADS, 2 * WINDOW, SWA_GROUP * WINDOW)
    oh_d = (_t5_bucket(WINDOW - 1 - jnp.arange(WINDOW))[:, None] == buckets).astype(F32)
    bias_d = jnp.einsum("rb,bh->hr", oh_d, table, precision=lax.Precision.HIGHEST)
    bias_d = jnp.concatenate([bias_d, jnp.zeros((8, WINDOW), F32)], axis=0)
    wts = dict(
        sinks=attn_sinks[0].astype(F32), bias=bias_p,
        g_mix=g_mix[0][None], w_in=w_in_r, w_a_up=w_a_pad, b_a=b_a[0][None],
        g_gla=g_gla_out[0][None], g_swa=g_swa_out[0][None], w_out=w_out[0].astype(BF16),
        g_ffn=g_ffn[0][None], w_r=jnp.concatenate([wr_hi, wr_lo], axis=0), b_r=b_router[0][:, None],
    )

    x_pre = jnp.concatenate([jnp.zeros((WINDOW - N_META, D_MODEL), F32), meta_tokens.astype(F32)], axis=0)[None]
    zeros_s = jnp.zeros((GLA_QK, GLA_V), F32)
    zeros_kv = jnp.zeros((WINDOW, SWA_KV), F32)
    zeros_b = jnp.zeros((N_EXPERTS, LANES), F32)
    pre = _mixer_call(x_pre, zeros_s, zeros_kv, zeros_kv, zeros_b, wts, WINDOW, WINDOW - N_META, 0, WINDOW)
    s_meta, k_meta, v_meta = pre[5][0], pre[6][0], pre[7][0]

    (xmid_p, hp_p, topi_p, gate_p, rank_p, s_p, k_p, v_p, cnt_p) = _mixer_call(
        x_prompt, s_meta, k_meta, v_meta, zeros_b, wts, MIX_TM, 0, WINDOW - N_META, TP)

    (xmid_s, hp_s, topi_s, gate_s, rank_s, st_s, ck_s, cv_s, cnt_all) = _decode_call(
        x_sample[:, 0], state_gla[0], cache_swa_k[0].reshape(n_seq, WINDOW, SWA_KV),
        cache_swa_v[0].reshape(n_seq, WINDOW, SWA_KV), cnt_p, bias_d, wts)

    tm = MOE_TM
    n_slots = T_all * TOP_K
    n_blocks = -(-n_slots // tm) + N_EXPERTS
    top_e = jnp.concatenate([topi_p[:TOP_K], topi_s[:TOP_K]], axis=1)
    rank = jnp.concatenate([rank_p[:TOP_K], rank_s[:TOP_K]], axis=1)
    counts = cnt_all[:, 0].astype(jnp.int32)
    padded = (counts + tm - 1) // tm * tm
    pad_end = jnp.cumsum(padded)
    pad_start = pad_end - padded
    e_ids = jnp.arange(N_EXPERTS, dtype=jnp.int32)
    dest = jnp.sum(jnp.where(top_e[..., None] == e_ids, pad_start, 0), axis=-1) + rank
    n_pad = n_blocks * tm
    blk_e = jnp.minimum(jnp.sum(pad_end[None] <= (jnp.arange(n_blocks, dtype=jnp.int32) * tm)[:, None], axis=1),
                        N_EXPERTS - 1).astype(jnp.int32)
    nused = (pad_end[-1] // tm).astype(jnp.int32).reshape(1)

    sample_rows = 8
    idx_p = dest[:, :TP].reshape(TOP_K, TP // SC_SCATTER_ROWS, SC_SCATTER_ROWS).transpose(1, 0, 2)
    idx_s = dest[:, TP:].reshape(TOP_K, n_seq // sample_rows, sample_rows).transpose(1, 0, 2)
    xs3 = _sc_scatter_rows(hp_p.reshape(TP, SLAB, LANES), hp_s.reshape(n_seq, SLAB, LANES), idx_p, idx_s, n_pad)
    ys2 = _ffn_call(blk_e, nused, xs3.reshape(-1, LANES), w_up[0], b_up[0], w_down[0], b_down[0])
    unit = math.lcm(2 * SC_WORKERS * SC_GATHER_ROWS // TOP_K, MIX_TM)
    t_stride = -(-T_all // unit) * unit
    filler = jnp.arange(TOP_K * (t_stride - T_all), dtype=jnp.int32).reshape(TOP_K, t_stride - T_all)
    slot_src = jnp.concatenate([dest, filler], axis=1)
    slot_src = slot_src.reshape(TOP_K * t_stride // SC_GATHER_ROWS, SC_GATHER_ROWS)
    ys4 = _sc_gather_rows(ys2.reshape(-1, SLAB, LANES), slot_src).reshape(-1, LANES)

    gf = g_final[None]
    y_p = _combine_call(ys4, t_stride, 0, xmid_p.reshape(TP, D_MODEL), gate_p, gf, MIX_TM)
    y_s = _combine_call(ys4, t_stride, TP, xmid_s, gate_s, gf, n_seq)

    s_heads = jnp.stack([s_p[:, h * GLA_DK:(h + 1) * GLA_DK, h * GLA_DV:(h + 1) * GLA_DV]
                         for h in range(GLA_HEADS)], axis=1)
    return (y_p.reshape(B, L, D_MODEL), y_s.reshape(n_seq, 1, D_MODEL), s_heads[None],
            k_p.reshape(1, B, WINDOW, SWA_KV_HEADS, SWA_HEAD_DIM),
            v_p.reshape(1, B, WINDOW, SWA_KV_HEADS, SWA_HEAD_DIM),
            st_s[None], ck_s.reshape(1, n_seq, WINDOW, SWA_KV_HEADS, SWA_HEAD_DIM),
            cv_s.reshape(1, n_seq, WINDOW, SWA_KV_HEADS, SWA_HEAD_DIM))
```

```python
import functools
import math

import jax
import jax.numpy as jnp
from jax import lax
from jax.experimental import pallas as pl
from jax.experimental.pallas import tpu as pltpu
from jax.experimental.pallas import tpu_sc as plsc

D_MODEL = 1024
N_META = 16
GLA_HEADS = 4
GLA_DK = 64
GLA_DV = 128
GLA_LOWRANK = 16
GLA_GATE_TAU = 16.0
GLA_CHUNK = 64
SWA_HEADS = 8
SWA_KV_HEADS = 2
SWA_HEAD_DIM = 64
SWA_GROUP = SWA_HEADS // SWA_KV_HEADS
WINDOW = 128
NUM_BUCKETS = 32
MAX_DISTANCE = 128
N_EXPERTS = 32
TOP_K = 4
D_FF = 1024
SWIGLU_ALPHA = 1.702
SWIGLU_LIMIT = 7.0
RMS_EPS = 1e-6

GLA_QK = GLA_HEADS * GLA_DK
GLA_V = GLA_HEADS * GLA_DV
SWA_Q = SWA_HEADS * SWA_HEAD_DIM
SWA_KV = SWA_KV_HEADS * SWA_HEAD_DIM
LANES = 128
C_GQ, C_GK, C_GV, C_GR = 0, GLA_QK, 2 * GLA_QK, 2 * GLA_QK + GLA_V
C_SQ = C_GR + GLA_V
C_SK = C_SQ + SWA_Q
C_SV = C_SK + SWA_KV
C_GA = C_SV + SWA_KV
D_PROJ = C_GA + LANES

MIX_TM = 512
MOE_TM = 512
DEC_SB = 16
VMEM_LIMIT = 56 * 1024 * 1024

F32 = jnp.float32
BF16 = jnp.bfloat16
NEG_INF = float("-inf")


def _dot(a, b):
    return jnp.dot(a, b, preferred_element_type=F32)


def _dot_nt(a, b):
    return lax.dot_general(a, b, (((1,), (1,)), ((), ())), preferred_element_type=F32)


def _split3(x):
    hi = x.astype(BF16)
    r1 = x - hi.astype(F32)
    mid = r1.astype(BF16)
    lo = (r1 - mid.astype(F32)).astype(BF16)
    return hi, mid, lo


def _rms(x, g):
    return x * lax.rsqrt(jnp.mean(x * x, axis=-1, keepdims=True) + RMS_EPS) * g


def _iota(shape, dim):
    return lax.broadcasted_iota(jnp.int32, shape, dim)


HALF_D = D_MODEL // 2
SLAB = HALF_D // LANES


def _pack_pair(lo, hi):
    bl = lax.bitcast_convert_type(lo.astype(BF16).astype(F32), jnp.uint32)
    bh = lax.bitcast_convert_type(hi.astype(BF16).astype(F32), jnp.uint32)
    return lax.bitcast_convert_type(bh | lax.shift_right_logical(bl, jnp.uint32(16)), jnp.int32)


def _unpack_pair(w):
    u = lax.bitcast_convert_type(w, jnp.uint32)
    lo = lax.bitcast_convert_type(lax.shift_left(u, jnp.uint32(16)), F32)
    hi = lax.bitcast_convert_type(u & jnp.uint32(0xFFFF0000), F32)
    return lo, hi


def _store_slabs(ref, x):
    rows = x.shape[0]
    for c in range(SLAB):
        sl = slice(c * LANES, (c + 1) * LANES)
        ref[pl.ds(c, rows, stride=SLAB), :] = _pack_pair(x[:, sl], x[:, HALF_D + c * LANES:HALF_D + (c + 1) * LANES])


def _load_slab_chunk(ref, rows, c):
    return _unpack_pair(ref[pl.ds(c, rows, stride=SLAB), :])


def _project(x, g_mix, w_in_ref, w_a_up, b_a):
    h = _rms(x, g_mix).astype(BF16)

    def cols(lo, width):
        return _dot(h, w_in_ref[:, lo:lo + width])

    ga = cols(C_GA, LANES).astype(BF16)
    z = _dot(ga, w_a_up) + b_a
    log_a = -(jnp.maximum(-z, 0.0) + jnp.log(1.0 + jnp.exp(-jnp.abs(z)))) / GLA_GATE_TAU
    gqk = cols(C_GQ, 2 * GLA_QK)
    gv = cols(C_GV, GLA_V)
    swa = cols(C_SQ, SWA_Q + 2 * SWA_KV)
    gr = cols(C_GR, GLA_V)
    return dict(
        gq=gqk[:, :GLA_QK],
        gk=gqk[:, GLA_QK:],
        gv=gv,
        gr=gr,
        sq=swa[:, :SWA_Q],
        sk=swa[:, SWA_Q:SWA_Q + SWA_KV],
        sv=swa[:, SWA_Q + SWA_KV:],
        log_a=log_a,
    )


def _tail(x, o_gla, gr, o_swa, g_gla_out, g_swa_out, w_out_ref, g_ffn, w_r, b_r, base):
    tm = x.shape[0]
    gate = gr * jax.nn.sigmoid(gr)
    parts = []
    for h in range(GLA_HEADS):
        sl = slice(h * GLA_DV, (h + 1) * GLA_DV)
        parts.append(_rms(o_gla[:, sl], g_gla_out) * gate[:, sl])
    og = jnp.concatenate(parts, axis=1).astype(BF16)
    os_ = _rms(o_swa, g_swa_out).astype(BF16)
    x_mid = x + _dot(og, w_out_ref[0:GLA_V]) + _dot(os_, w_out_ref[GLA_V:GLA_V + SWA_Q])
    hp = _rms(x_mid, g_ffn)

    h1 = hp.astype(BF16)
    h2 = (hp - h1.astype(F32)).astype(BF16)
    la = _dot_nt(w_r, h1)
    lb = _dot_nt(w_r[0:N_EXPERTS], h2)
    logits = la[0:N_EXPERTS] + la[N_EXPERTS:2 * N_EXPERTS] + lb + b_r

    eidx = _iota((N_EXPERTS, tm), 0)
    vals, idxs, onehots = [], [], []
    l = logits
    for _ in range(TOP_K):
        m = jnp.max(l, axis=0, keepdims=True)
        sel = jnp.min(jnp.where(l == m, eidx, N_EXPERTS), axis=0, keepdims=True)
        oh = eidx == sel
        l = jnp.where(oh, NEG_INF, l)
        vals.append(m)
        idxs.append(sel)
        onehots.append(oh)
    es = [jnp.exp(v - vals[0]) for v in vals]
    denom = es[0] + es[1] + es[2] + es[3]
    gates = [e / denom for e in es]

    ohf = jnp.concatenate([oh.astype(F32) for oh in onehots], axis=0)
    upper = (_iota((tm, tm), 0) < _iota((tm, tm), 1)).astype(BF16)
    prefix = _dot(ohf.astype(BF16), upper)
    ranks = []
    for k in range(TOP_K):
        sl = slice(k * N_EXPERTS, (k + 1) * N_EXPERTS)
        ohk = ohf[sl]
        base_t = jnp.concatenate([base] * (tm // LANES), axis=1)
        ranks.append(jnp.sum(ohk * (prefix[sl] + base_t), axis=0, keepdims=True))
        base = base + jnp.sum(ohk, axis=1, keepdims=True)
    zi = jnp.zeros((8 - TOP_K, tm), jnp.int32)
    zf = jnp.zeros((8 - TOP_K, tm), F32)
    topi = jnp.concatenate(idxs + [zi], axis=0)
    gate8 = jnp.concatenate(gates + [zf], axis=0)
    rank8 = jnp.concatenate([r.astype(jnp.int32) for r in ranks] + [zi], axis=0)
    return x_mid, hp, topi, gate8, rank8, base


def _gla_chunks(p, row0, s_blocks, n_lead_pad):
    tm = p["gq"].shape[0]
    nch = tm // GLA_CHUNK
    log_a = p["log_a"]
    if n_lead_pad:
        rows = row0 + _iota((tm, GLA_QK), 0)
        log_a = jnp.where(rows >= n_lead_pad, log_a, 0.0)
    ri, ci = _iota((tm, tm), 0), _iota((tm, tm), 1)
    tril = ((ri >= ci) & (ri // GLA_CHUNK == ci // GLA_CHUNK)).astype(BF16)
    hi, mid, lo = _split3(log_a)
    b_all = _dot(tril, hi) + _dot(tril, mid) + _dot(tril, lo)

    c64 = GLA_CHUNK
    kk_mask = (_iota((GLA_QK, GLA_QK), 0) // c64) == (_iota((GLA_QK, GLA_QK), 1) // GLA_DK)
    vv_mask = (_iota((GLA_QK, GLA_V), 0) // c64) == (_iota((GLA_QK, GLA_V), 1) // GLA_DV)
    zero_blk = jnp.zeros((GLA_DK, GLA_DV), BF16)
    causal = (_iota((c64, GLA_QK), 0) >= (_iota((c64, GLA_QK), 1) % c64)).astype(F32)
    zpad_k = jnp.zeros((LANES - c64, GLA_QK), F32)
    zpad_v = jnp.zeros((LANES - c64, GLA_V), BF16)

    outs = []
    for c in range(nch):
        rs = slice(c * c64, (c + 1) * c64)
        b = b_all[rs]
        q, k, v = p["gq"][rs], p["gk"][rs], p["gv"][rs]
        b_last = b[c64 - 1:c64]
        qt = (q * jnp.exp(b)).astype(BF16)
        kt = k * jnp.exp(-b)
        kd = k * jnp.exp(b_last - b)
        vb = v.astype(BF16)
        k_bd = jnp.where(kk_mask, jnp.concatenate([kt] * GLA_HEADS, axis=0), 0.0).astype(BF16)
        a = (_dot_nt(qt, k_bd) * causal).astype(BF16)
        v_bd = jnp.where(vv_mask, jnp.concatenate([vb] * GLA_HEADS, axis=0), jnp.zeros((), BF16))
        s_bd = jnp.concatenate(
            [jnp.concatenate([s_blocks[h].astype(BF16) if g == h else zero_blk for g in range(GLA_HEADS)], axis=1)
             for h in range(GLA_HEADS)], axis=0)
        outs.append(_dot(qt, s_bd) + _dot(a, v_bd))
        kd_t = jnp.transpose(jnp.concatenate([kd, zpad_k], axis=0)).astype(BF16)
        upd = _dot(kd_t, jnp.concatenate([vb, zpad_v], axis=0))
        decay = jnp.exp(jnp.transpose(jnp.broadcast_to(b_last, (LANES, GLA_QK))))
        s_blocks = [s_blocks[h] * decay[h * GLA_DK:(h + 1) * GLA_DK]
                    + upd[h * GLA_DK:(h + 1) * GLA_DK, h * GLA_DV:(h + 1) * GLA_DV] for h in range(GLA_HEADS)]
    return jnp.concatenate(outs, axis=0), s_blocks


def _swa_block(sq, kcat, vcat, bias_ref, sinks_ref, valid_t):
    half = _iota((1, LANES), 1) < SWA_HEAD_DIM
    top_rows = _iota((LANES, 1), 0) < SWA_HEAD_DIM
    k_roll = pltpu.roll(kcat, SWA_HEAD_DIM, 1)
    v_t = jnp.transpose(vcat)
    zeros_v = jnp.zeros((SWA_HEAD_DIM, 2 * WINDOW), F32)
    cols = []
    for kv in range(SWA_KV_HEADS):
        kk = jnp.where(half, kcat, k_roll) if kv == 0 else jnp.where(half, k_roll, kcat)
        q_parts = []
        for c in (2 * kv, 2 * kv + 1):
            qc = sq[:, c * LANES:(c + 1) * LANES]
            q_parts.append(jnp.where(half, qc, 0.0))
            q_parts.append(jnp.where(half, 0.0, qc))
        q_st = jnp.concatenate(q_parts, axis=0).astype(BF16)
        s = _dot_nt(kk.astype(BF16), q_st) + bias_ref[kv]
        if valid_t is not None:
            s = jnp.where(valid_t, s, NEG_INF)
        sink = jnp.concatenate(
            [jnp.full((1, WINDOW), sinks_ref[kv * SWA_GROUP + g], F32) for g in range(SWA_GROUP)], axis=1)
        m = jnp.maximum(jnp.max(s, axis=0, keepdims=True), sink)
        pr = jnp.exp(s - m)
        inv = 1.0 / (jnp.sum(pr, axis=0, keepdims=True) + jnp.exp(sink - m))
        pb = pr.astype(BF16)
        vk = v_t[kv * SWA_HEAD_DIM:(kv + 1) * SWA_HEAD_DIM]
        vv_t = jnp.concatenate([jnp.concatenate([vk, zeros_v], axis=1),
                                jnp.concatenate([zeros_v, vk], axis=1)], axis=0).astype(BF16)
        for pair in range(SWA_GROUP // 2):
            ce = slice(2 * pair * WINDOW, (2 * pair + 1) * WINDOW)
            co = slice((2 * pair + 1) * WINDOW, (2 * pair + 2) * WINDOW)
            p2_t = jnp.concatenate([pb[:, ce], pb[:, co]], axis=0)
            o2_t = _dot(vv_t, p2_t)
            o2_t = o2_t * jnp.where(top_rows, inv[:, ce], inv[:, co])
            cols.append(jnp.transpose(o2_t))
    return jnp.concatenate(cols, axis=1)


def _mixer_kernel(sinks_ref, x_ref, s0_ref, k0_ref, v0_ref, base0_ref, bias_ref,
                  g_mix_ref, w_in_ref, w_a_up_ref, b_a_ref, g_gla_ref, g_swa_ref, w_out_ref,
                  g_ffn_ref, w_r_ref, b_r_ref,
                  xmid_ref, hp_ref, topi_ref, gate_ref, rank_ref, sout_ref, kout_ref, vout_ref, cnt_ref,
                  s_scr, k_scr, v_scr, base_scr, *, n_lead_pad, prev_valid_from):
    b_id, j = pl.program_id(0), pl.program_id(1)
    tm = x_ref.shape[1]

    @pl.when(j == 0)
    def _():
        s_scr[...] = s0_ref[...]
        k_scr[...] = k0_ref[...]
        v_scr[...] = v0_ref[...]

    @pl.when((j == 0) & (b_id == 0))
    def _():
        base_scr[...] = base0_ref[...]

    x = x_ref[0]
    p = _project(x, g_mix_ref[...], w_in_ref, w_a_up_ref[...], b_a_ref[...])

    diag = [(slice(h * GLA_DK, (h + 1) * GLA_DK), slice(h * GLA_DV, (h + 1) * GLA_DV)) for h in range(GLA_HEADS)]
    o_gla, s_blocks = _gla_chunks(p, j * tm, [s_scr[r, c] for r, c in diag], n_lead_pad)
    for (r, c), blk in zip(diag, s_blocks):
        s_scr[r, c] = blk

    o_parts = []
    for sb in range(tm // WINDOW):
        rs = slice(sb * WINDOW, (sb + 1) * WINDOW)
        k_blk, v_blk = p["sk"][rs], p["sv"][rs]
        k_prev = k_scr[...] if sb == 0 else p["sk"][(sb - 1) * WINDOW:sb * WINDOW]
        v_prev = v_scr[...] if sb == 0 else p["sv"][(sb - 1) * WINDOW:sb * WINDOW]
        valid = None
        if sb == 0 and prev_valid_from:
            first = jnp.where(j == 0, prev_valid_from, 0)
            valid = _iota((2 * WINDOW, SWA_GROUP * WINDOW), 0) >= first
        o_parts.append(_swa_block(p["sq"][rs], jnp.concatenate([k_prev, k_blk], axis=0),
                                  jnp.concatenate([v_prev, v_blk], axis=0), bias_ref, sinks_ref, valid))
    o_swa = jnp.concatenate(o_parts, axis=0)
    k_scr[...] = p["sk"][tm - WINDOW:tm]
    v_scr[...] = p["sv"][tm - WINDOW:tm]

    x_mid, hp, topi, gate8, rank8, base = _tail(
        x, o_gla, p["gr"], o_swa, g_gla_ref[...], g_swa_ref[...], w_out_ref,
        g_ffn_ref[...], w_r_ref[...], b_r_ref[...], base_scr[...])
    base_scr[...] = base
    xmid_ref[0] = x_mid
    _store_slabs(hp_ref, hp)
    topi_ref[...] = topi
    gate_ref[...] = gate8
    rank_ref[...] = rank8
    sout_ref[0] = s_scr[...]
    kout_ref[0] = p["sk"][tm - WINDOW:tm]
    vout_ref[0] = p["sv"][tm - WINDOW:tm]
    cnt_ref[...] = base


def _full_spec(shape):
    nd = len(shape)
    return pl.BlockSpec(shape, lambda *_: (0,) * nd)


def _mixer_call(x, s0, k0, v0, base0, wts, tm, n_lead_pad, prev_valid_from, hp_rows):
    B, L, _ = x.shape
    nj = L // tm
    T = B * L
    weight_args = (wts["bias"], wts["g_mix"], wts["w_in"], wts["w_a_up"], wts["b_a"], wts["g_gla"],
                   wts["g_swa"], wts["w_out"], wts["g_ffn"], wts["w_r"], wts["b_r"])
    in_specs = [
        pl.BlockSpec(memory_space=pltpu.SMEM),
        pl.BlockSpec((1, tm, D_MODEL), lambda b, j: (b, j, 0)),
        _full_spec(s0.shape), _full_spec(k0.shape), _full_spec(v0.shape), _full_spec(base0.shape),
    ] + [_full_spec(w.shape) for w in weight_args]
    tok_spec = pl.BlockSpec((8, tm), lambda b, j: (0, b * nj + j))
    out_specs = [
        pl.BlockSpec((1, tm, D_MODEL), lambda b, j: (b, j, 0)),
        pl.BlockSpec((tm * SLAB, LANES), lambda b, j: (b * nj + j, 0)),
        tok_spec, tok_spec, tok_spec,
        pl.BlockSpec((1, GLA_QK, GLA_V), lambda b, j: (b, 0, 0)),
        pl.BlockSpec((1, WINDOW, SWA_KV), lambda b, j: (b, 0, 0)),
        pl.BlockSpec((1, WINDOW, SWA_KV), lambda b, j: (b, 0, 0)),
        _full_spec((N_EXPERTS, LANES)),
    ]
    out_shape = [
        jax.ShapeDtypeStruct((B, L, D_MODEL), F32),
        jax.ShapeDtypeStruct((hp_rows * SLAB, LANES), jnp.int32),
        jax.ShapeDtypeStruct((8, T), jnp.int32),
        jax.ShapeDtypeStruct((8, T), F32),
        jax.ShapeDtypeStruct((8, T), jnp.int32),
        jax.ShapeDtypeStruct((B, GLA_QK, GLA_V), F32),
        jax.ShapeDtypeStruct((B, WINDOW, SWA_KV), F32),
        jax.ShapeDtypeStruct((B, WINDOW, SWA_KV), F32),
        jax.ShapeDtypeStruct((N_EXPERTS, LANES), F32),
    ]
    kern = functools.partial(_mixer_kernel, n_lead_pad=n_lead_pad, prev_valid_from=prev_valid_from)
    return pl.pallas_call(
        kern,
        grid=(B, nj),
        in_specs=in_specs,
        out_specs=out_specs,
        out_shape=out_shape,
        scratch_shapes=[pltpu.VMEM((GLA_QK, GLA_V), F32), pltpu.VMEM((WINDOW, SWA_KV), F32),
                        pltpu.VMEM((WINDOW, SWA_KV), F32), pltpu.VMEM((N_EXPERTS, LANES), F32)],
        compiler_params=pltpu.CompilerParams(dimension_semantics=("arbitrary", "arbitrary"),
                                             vmem_limit_bytes=VMEM_LIMIT),
        name="mixer",
    )(wts["sinks"], x, s0, k0, v0, base0, *weight_args)


def _decode_kernel(sinks_ref, x_ref, st_ref, ck_ref, cv_ref, base0_ref, bias_ref,
                   g_mix_ref, w_in_ref, w_a_up_ref, b_a_ref, g_gla_ref, g_swa_ref, w_out_ref,
                   g_ffn_ref, w_r_ref, b_r_ref,
                   xmid_ref, hp_ref, topi_ref, gate_ref, rank_ref, sto_ref, cko_ref, cvo_ref, cnt_ref,
                   tq_scr, gv_scr, gr_scr, sq_scr, sk_scr, sv_scr, og_scr, os_scr):
    i = pl.program_id(0)
    n_seq = x_ref.shape[0]

    @pl.when(i == 0)
    def _():
        p = _project(x_ref[...], g_mix_ref[...], w_in_ref, w_a_up_ref[...], b_a_ref[...])
        a_hi, a_mid, a_lo = _split3(jnp.transpose(jnp.exp(p["log_a"])))
        tq_scr[...] = jnp.concatenate(
            [a_hi, a_mid, a_lo, jnp.transpose(p["gk"]).astype(BF16), jnp.transpose(p["gq"]).astype(BF16)], axis=0)
        gv_scr[...] = p["gv"]
        gr_scr[...] = p["gr"]
        sq_scr[...] = p["sq"]
        sk_scr[...] = p["sk"]
        sv_scr[...] = p["sv"]

    seq_row = _iota((n_seq, LANES), 0)
    half = _iota((1, LANES), 1) < SWA_HEAD_DIM
    row_id = _iota((WINDOW, SWA_KV), 0)
    head_diag = (_iota((16, SWA_Q), 1) // SWA_HEAD_DIM) == _iota((16, SWA_Q), 0)
    sink_col = jnp.concatenate(
        [jnp.full((1, 1), sinks_ref[h], F32) for h in range(SWA_HEADS)] + [jnp.zeros((8, 1), F32)], axis=0)

    def per_seq(sl, carry):
        s = i * DEC_SB + sl
        pick = (seq_row == s).astype(BF16)
        cols = _dot(tq_scr[...], pick)
        a_c = cols[0:GLA_QK] + cols[GLA_QK:2 * GLA_QK] + cols[2 * GLA_QK:3 * GLA_QK]
        k_c = cols[3 * GLA_QK:4 * GLA_QK]
        q_c = cols[4 * GLA_QK:5 * GLA_QK]
        st = st_ref[sl].reshape(GLA_QK, GLA_DV)
        v_row = gv_scr[pl.ds(s, 1), :]
        v_b = jnp.concatenate(
            [jnp.broadcast_to(v_row[:, h * GLA_DV:(h + 1) * GLA_DV], (GLA_DK, GLA_DV))
             for h in range(GLA_HEADS)], axis=0)
        st_new = a_c * st + k_c * v_b
        sto_ref[sl] = st_new.reshape(GLA_HEADS, GLA_DK, GLA_DV)
        t = q_c * st_new
        og_scr[pl.ds(s, 1), :] = jnp.concatenate(
            [jnp.sum(t[h * GLA_DK:(h + 1) * GLA_DK], axis=0, keepdims=True) for h in range(GLA_HEADS)],
            axis=1)

        k_new = sk_scr[pl.ds(s, 1), :]
        v_new = sv_scr[pl.ds(s, 1), :]
        kn = jnp.where(row_id == WINDOW - 1, k_new, pltpu.roll(ck_ref[sl], WINDOW - 1, 0))
        vn = jnp.where(row_id == WINDOW - 1, v_new, pltpu.roll(cv_ref[sl], WINDOW - 1, 0))
        cko_ref[sl] = kn
        cvo_ref[sl] = vn
        kr, vr = pltpu.roll(kn, SWA_HEAD_DIM, 1), pltpu.roll(vn, SWA_HEAD_DIM, 1)
        k0, k1 = jnp.where(half, kn, kr), jnp.where(half, kr, kn)
        v0, v1 = jnp.where(half, vn, vr), jnp.where(half, vr, vn)
        kw = jnp.concatenate([k0, k0, k1, k1], axis=1).astype(BF16)
        vw = jnp.concatenate([v0, v0, v1, v1], axis=1).astype(BF16)
        q_row = sq_scr[pl.ds(s, 1), :]
        qm = jnp.where(head_diag, jnp.broadcast_to(q_row, (16, SWA_Q)), 0.0).astype(BF16)
        sc = _dot_nt(qm, kw) + bias_ref[...]
        m = jnp.maximum(jnp.max(sc, axis=1, keepdims=True), sink_col)
        pr = jnp.exp(sc - m)
        inv = 1.0 / (jnp.sum(pr, axis=1, keepdims=True) + jnp.exp(sink_col - m))
        ow = _dot(pr.astype(BF16), vw) * inv
        os_scr[pl.ds(s, 1), :] = jnp.sum(jnp.where(head_diag, ow, 0.0), axis=0, keepdims=True)
        return carry

    lax.fori_loop(0, DEC_SB, per_seq, 0, unroll=8)

    @pl.when(i == pl.num_programs(0) - 1)
    def _():
        x_mid, hp, topi, gate8, rank8, base = _tail(
            x_ref[...], og_scr[...], gr_scr[...], os_scr[...], g_gla_ref[...], g_swa_ref[...],
            w_out_ref, g_ffn_ref[...], w_r_ref[...], b_r_ref[...], base0_ref[...])
        xmid_ref[...] = x_mid
        _store_slabs(hp_ref, hp)
        topi_ref[...] = topi
        gate_ref[...] = gate8
        rank_ref[...] = rank8
        cnt_ref[...] = base


def _decode_call(xs, state, ck, cv, base0, bias_dec, wts):
    n_seq = xs.shape[0]
    nb = n_seq // DEC_SB
    weight_args = (wts["g_mix"], wts["w_in"], wts["w_a_up"], wts["b_a"], wts["g_gla"],
                   wts["g_swa"], wts["w_out"], wts["g_ffn"], wts["w_r"], wts["b_r"])
    in_specs = [
        pl.BlockSpec(memory_space=pltpu.SMEM),
        _full_spec(xs.shape),
        pl.BlockSpec((DEC_SB, GLA_HEADS, GLA_DK, GLA_DV), lambda i: (i, 0, 0, 0)),
        pl.BlockSpec((DEC_SB, WINDOW, SWA_KV), lambda i: (i, 0, 0)),
        pl.BlockSpec((DEC_SB, WINDOW, SWA_KV), lambda i: (i, 0, 0)),
        _full_spec(base0.shape), _full_spec(bias_dec.shape),
    ] + [_full_spec(w.shape) for w in weight_args]
    out_specs = [
        _full_spec((n_seq, D_MODEL)),
        _full_spec((n_seq * SLAB, LANES)),
        _full_spec((8, n_seq)), _full_spec((8, n_seq)), _full_spec((8, n_seq)),
        pl.BlockSpec((DEC_SB, GLA_HEADS, GLA_DK, GLA_DV), lambda i: (i, 0, 0, 0)),
        pl.BlockSpec((DEC_SB, WINDOW, SWA_KV), lambda i: (i, 0, 0)),
        pl.BlockSpec((DEC_SB, WINDOW, SWA_KV), lambda i: (i, 0, 0)),
        _full_spec((N_EXPERTS, LANES)),
    ]
    out_shape = [
        jax.ShapeDtypeStruct((n_seq, D_MODEL), F32),
        jax.ShapeDtypeStruct((n_seq * SLAB, LANES), jnp.int32),
        jax.ShapeDtypeStruct((8, n_seq), jnp.int32),
        jax.ShapeDtypeStruct((8, n_seq), F32),
        jax.ShapeDtypeStruct((8, n_seq), jnp.int32),
        jax.ShapeDtypeStruct(state.shape, F32),
        jax.ShapeDtypeStruct(ck.shape, F32),
        jax.ShapeDtypeStruct(cv.shape, F32),
        jax.ShapeDtypeStruct((N_EXPERTS, LANES), F32),
    ]
    scratch = [pltpu.VMEM((5 * GLA_QK, n_seq), BF16)] + [
        pltpu.VMEM((n_seq, GLA_V), F32), pltpu.VMEM((n_seq, GLA_V), F32), pltpu.VMEM((n_seq, SWA_Q), F32),
        pltpu.VMEM((n_seq, SWA_KV), F32), pltpu.VMEM((n_seq, SWA_KV), F32),
        pltpu.VMEM((n_seq, GLA_V), F32), pltpu.VMEM((n_seq, SWA_Q), F32)]
    return pl.pallas_call(
        _decode_kernel,
        grid=(nb,),
        in_specs=in_specs,
        out_specs=out_specs,
        out_shape=out_shape,
        scratch_shapes=scratch,
        compiler_params=pltpu.CompilerParams(dimension_semantics=("arbitrary",),
                                             vmem_limit_bytes=VMEM_LIMIT),
        name="decode",
    )(wts["sinks"], xs, state, ck, cv, base0, bias_dec, *weight_args)


SC_CORES = 2
SC_SUBCORES = 16
SC_WORKERS = SC_CORES * SC_SUBCORES
SC_SCATTER_ROWS = 64
SC_GATHER_ROWS = 48


def _sc_mesh():
    return plsc.VectorSubcoreMesh(core_axis_name="c", subcore_axis_name="s")


def _sc_worker_id():
    return lax.axis_index("s") * SC_CORES + lax.axis_index("c")


def _sc_scatter_rows(src_p, src_s, idx_p, idx_s, n_out):
    rows = SC_SCATTER_ROWS
    n_chunks = idx_p.shape[0] // SC_WORKERS
    n_s, _, rows_s = idx_s.shape
    assert n_chunks * SC_WORKERS == idx_p.shape[0] and n_chunks % 2 == 0 and n_s <= SC_WORKERS

    @functools.partial(
        pl.kernel, mesh=_sc_mesh(),
        out_type=jax.ShapeDtypeStruct((n_out, SLAB, LANES), jnp.int32),
        scratch_types=[pltpu.VMEM((2, TOP_K, rows), jnp.int32), pltpu.VMEM((2, rows, SLAB, LANES), jnp.int32),
                       pltpu.VMEM((TOP_K, rows_s), jnp.int32), pltpu.VMEM((rows_s, SLAB, LANES), jnp.int32),
                       pltpu.SemaphoreType.DMA((2,)), pltpu.SemaphoreType.DMA((2,))])
    def scatter_rows(srcp_hbm, srcs_hbm, idxp_hbm, idxs_hbm, out_hbm, idx_v, rows_v, idxs_v, rowss_v, lsem, ssem):
        wid = _sc_worker_id()

        def loads(c, b):
            g = wid * n_chunks + c
            return (pltpu.make_async_copy(idxp_hbm.at[g], idx_v.at[b], lsem.at[b]),
                    pltpu.make_async_copy(srcp_hbm.at[pl.ds(pl.multiple_of(g * rows, 8), rows)], rows_v.at[b],
                                          lsem.at[b]))

        def scatters(b):
            return [pltpu.make_async_copy(rows_v.at[b], out_hbm.at[idx_v.at[b, k]], ssem.at[b])
                    for k in range(TOP_K)]

        for d in loads(0, 0):
            d.start()

        @pl.loop(0, n_chunks, step=2)
        def _(c0):
            for b in range(2):
                c = c0 + b
                for d in loads(c, b):
                    d.wait()

                @pl.when(c >= 1)
                def _():
                    for d in scatters(1 - b):
                        d.wait()

                @pl.when(c + 1 < n_chunks)
                def _():
                    for d in loads(c + 1, 1 - b):
                        d.start()

                for d in scatters(b):
                    d.start()

        for d in scatters((n_chunks - 1) % 2):
            d.wait()

        @pl.when(wid < n_s)
        def _():
            pltpu.sync_copy(idxs_hbm.at[wid], idxs_v)
            pltpu.sync_copy(srcs_hbm.at[pl.ds(pl.multiple_of(wid * rows_s, 8), rows_s)], rowss_v)
            for k in range(TOP_K):
                pltpu.sync_copy(rowss_v, out_hbm.at[idxs_v.at[k]])

    return scatter_rows(src_p, src_s, idx_p, idx_s)


def _sc_gather_rows(src3, idx2):
    rows = SC_GATHER_ROWS
    n_chunks = idx2.shape[0] // SC_WORKERS
    assert n_chunks * SC_WORKERS == idx2.shape[0] and idx2.shape[1] == rows and n_chunks % 2 == 0

    @functools.partial(
        pl.kernel, mesh=_sc_mesh(),
        out_type=jax.ShapeDtypeStruct((idx2.shape[0] * rows, SLAB, LANES), jnp.int32),
        scratch_types=[pltpu.VMEM((2, rows), jnp.int32), pltpu.VMEM((2, rows, SLAB, LANES), jnp.int32),
                       pltpu.SemaphoreType.DMA((2,)), pltpu.SemaphoreType.DMA((2,))])
    def gather_rows(src_hbm, idx_hbm, out_hbm, idx_v, rows_v, gsem, wsem):
        wid = _sc_worker_id()

        def gather(b):
            return pltpu.make_async_copy(src_hbm.at[idx_v.at[b]], rows_v.at[b], gsem.at[b])

        def write(c, b):
            base = pl.multiple_of((wid * n_chunks + c) * rows, 8)
            return pltpu.make_async_copy(rows_v.at[b], out_hbm.at[pl.ds(base, rows)], wsem.at[b])

        pltpu.sync_copy(idx_hbm.at[wid * n_chunks], idx_v.at[0])
        gather(0).start()

        @pl.loop(0, n_chunks, step=2)
        def _(c0):
            for b in range(2):
                c = c0 + b

                @pl.when(c + 1 < n_chunks)
                def _():
                    @pl.when(c >= 1)
                    def _():
                        write(c - 1, 1 - b).wait()
                    pltpu.sync_copy(idx_hbm.at[wid * n_chunks + c + 1], idx_v.at[1 - b])
                    gather(1 - b).start()

                gather(b).wait()
                write(c, b).start()

        write(n_chunks - 2, 0).wait()
        write(n_chunks - 1, 1).wait()

    return gather_rows(src3, idx2)


FF_TILE = 256


def _ffn_kernel(blk_e_ref, nused_ref, x_ref, wu_ref, bu_ref, wd_ref, bd_ref, y_ref, xbf, actbf, wu_bf, wd_bf):
    i = pl.program_id(0)
    tm = MOE_TM
    n_tiles = D_FF // FF_TILE

    @pl.when(i < nused_ref[0])
    def _():
        @pl.when((i == 0) | (blk_e_ref[i] != blk_e_ref[jnp.maximum(i - 1, 0)]))
        def _():
            wu_bf[...] = wu_ref[0].astype(BF16)
            wd_bf[...] = wd_ref[0].astype(BF16)

        for c in range(SLAB):
            lo, hi = _load_slab_chunk(x_ref, tm, c)
            xbf[:, c * LANES:(c + 1) * LANES] = lo.astype(BF16)
            xbf[:, HALF_D + c * LANES:HALF_D + (c + 1) * LANES] = hi.astype(BF16)
        for n in range(n_tiles):
            gc = slice(n * FF_TILE, (n + 1) * FF_TILE)
            lc = slice(D_FF + n * FF_TILE, D_FF + (n + 1) * FF_TILE)
            g = jnp.minimum(_dot(xbf[...], wu_bf[:, gc]) + bu_ref[0, :, gc], SWIGLU_LIMIT)
            lin = jnp.clip(_dot(xbf[...], wu_bf[:, lc]) + bu_ref[0, :, lc], -SWIGLU_LIMIT, SWIGLU_LIMIT)
            actbf[:, gc] = (g * jax.nn.sigmoid(SWIGLU_ALPHA * g) * (lin + 1.0)).astype(BF16)
        per_tile = FF_TILE // LANES
        for n in range(n_tiles // 2):
            yl = slice(n * FF_TILE, (n + 1) * FF_TILE)
            yh = slice(HALF_D + n * FF_TILE, HALF_D + (n + 1) * FF_TILE)
            y_lo = _dot(actbf[...], wd_bf[:, yl]) + bd_ref[0, :, yl]
            y_hi = _dot(actbf[...], wd_bf[:, yh]) + bd_ref[0, :, yh]
            for c in range(per_tile):
                sl = slice(c * LANES, (c + 1) * LANES)
                y_ref[pl.ds(n * per_tile + c, tm, stride=SLAB), :] = _pack_pair(y_lo[:, sl], y_hi[:, sl])

    @pl.when(i >= nused_ref[0])
    def _():
        y_ref[...] = jnp.zeros_like(y_ref)


def _ffn_call(blk_e, nused, xs2, w_up, b_up, w_down, b_down):
    n_blocks = blk_e.shape[0]
    tm = MOE_TM
    row_blk = pl.BlockSpec((tm * SLAB, LANES), lambda i, be, nu: (i, 0))
    grid_spec = pltpu.PrefetchScalarGridSpec(
        num_scalar_prefetch=2,
        grid=(n_blocks,),
        in_specs=[
            row_blk,
            pl.BlockSpec((1, D_MODEL, 2 * D_FF), lambda i, be, nu: (be[i], 0, 0)),
            pl.BlockSpec((1, 1, 2 * D_FF), lambda i, be, nu: (be[i], 0, 0)),
            pl.BlockSpec((1, D_FF, D_MODEL), lambda i, be, nu: (be[i], 0, 0)),
            pl.BlockSpec((1, 1, D_MODEL), lambda i, be, nu: (be[i], 0, 0)),
        ],
        out_specs=row_blk,
        scratch_shapes=[pltpu.VMEM((tm, D_MODEL), BF16), pltpu.VMEM((tm, D_FF), BF16),
                        pltpu.VMEM((D_MODEL, 2 * D_FF), BF16), pltpu.VMEM((D_FF, D_MODEL), BF16)],
    )
    return pl.pallas_call(
        _ffn_kernel,
        grid_spec=grid_spec,
        out_shape=jax.ShapeDtypeStruct((n_blocks * tm * SLAB, LANES), jnp.int32),
        compiler_params=pltpu.CompilerParams(dimension_semantics=("arbitrary",),
                                             vmem_limit_bytes=VMEM_LIMIT),
        name="experts",
    )(blk_e, nused, xs2, w_up, b_up.reshape(N_EXPERTS, 1, 2 * D_FF), w_down, b_down.reshape(N_EXPERTS, 1, D_MODEL))


def _combine_kernel(ys0_ref, ys1_ref, ys2_ref, ys3_ref, xmid_ref, gate_ref, g_final_ref, y_ref):
    tm = xmid_ref.shape[0]
    gts = jnp.transpose(jnp.concatenate([gate_ref[...], jnp.zeros((LANES - 8, tm), F32)], axis=0))
    lows, highs = [], []
    for c in range(SLAB):
        acc_lo = xmid_ref[:, c * LANES:(c + 1) * LANES]
        acc_hi = xmid_ref[:, HALF_D + c * LANES:HALF_D + (c + 1) * LANES]
        for k, ys_ref in enumerate((ys0_ref, ys1_ref, ys2_ref, ys3_ref)):
            lo, hi = _load_slab_chunk(ys_ref, tm, c)
            acc_lo = acc_lo + lo * gts[:, k:k + 1]
            acc_hi = acc_hi + hi * gts[:, k:k + 1]
        lows.append(acc_lo)
        highs.append(acc_hi)
    y_ref[...] = _rms(jnp.concatenate(lows + highs, axis=1), g_final_ref[...])


def _combine_call(ys4, t_stride, row0, x_mid, gates, g_final, tm):
    T = x_mid.shape[0]
    blk0 = row0 // tm
    per_k = t_stride // tm
    assert per_k * tm == t_stride and blk0 * tm == row0

    def ys_spec(k):
        return pl.BlockSpec((tm * SLAB, LANES), lambda i: (k * per_k + blk0 + i, 0))

    return pl.pallas_call(
        _combine_kernel,
        grid=(T // tm,),
        in_specs=[
            ys_spec(0), ys_spec(1), ys_spec(2), ys_spec(3),
            pl.BlockSpec((tm, D_MODEL), lambda i: (i, 0)),
            pl.BlockSpec((8, tm), lambda i: (0, i)),
            _full_spec((1, D_MODEL)),
        ],
        out_specs=pl.BlockSpec((tm, D_MODEL), lambda i: (i, 0)),
        out_shape=jax.ShapeDtypeStruct((T, D_MODEL), F32),
        compiler_params=pltpu.CompilerParams(dimension_semantics=("arbitrary",),
                                             vmem_limit_bytes=VMEM_LIMIT),
        name="combine",
    )(ys4, ys4, ys4, ys4, x_mid, gates, g_final)


def _t5_bucket(dist):
    n = jnp.maximum(dist, 0)
    max_exact = NUM_BUCKETS // 2
    nf = jnp.maximum(n, 1).astype(F32)
    large = max_exact + (jnp.log(nf / max_exact) / math.log(MAX_DISTANCE / max_exact)
                         * (NUM_BUCKETS - max_exact)).astype(jnp.int32)
    large = jnp.minimum(large, NUM_BUCKETS - 1)
    return jnp.where(n < max_exact, n, large)


def kernel(x_prompt, x_sample, state_gla, cache_swa_k, cache_swa_v, meta_tokens, rel_bias_table,
           g_mix, w_in, w_a_up, b_a, g_gla_out, g_swa_out, attn_sinks, w_out,
           g_ffn, w_router, b_router, w_up, b_up, w_down, b_down, g_final):
    assert g_mix.shape[0] == 1, "single-layer trunk"
    B, L, _ = x_prompt.shape
    n_seq = x_sample.shape[0]
    TP = B * L
    T_all = TP + n_seq

    wi = w_in[0]
    sizes = (GLA_QK, GLA_QK, GLA_V, GLA_V, GLA_LOWRANK, SWA_Q, SWA_KV, SWA_KV)
    offs = [0]
    for s in sizes:
        offs.append(offs[-1] + s)
    seg = [wi[:, offs[n]:offs[n + 1]] for n in range(8)]
    seg[0] = seg[0] * (GLA_DK ** -0.5)
    seg[5] = seg[5] * (SWA_HEAD_DIM ** -0.5)
    w_in_r = jnp.concatenate(
        seg[0:4] + seg[5:8] + [seg[4], jnp.zeros((D_MODEL, LANES - GLA_LOWRANK), F32)], axis=1).astype(BF16)
    w_a_pad = jnp.concatenate([w_a_up[0], jnp.zeros((LANES - GLA_LOWRANK, GLA_QK), F32)], axis=0).astype(BF16)
    wr_t = jnp.transpose(w_router[0])
    wr_hi = wr_t.astype(BF16)
    wr_lo = (wr_t - wr_hi.astype(F32)).astype(BF16)
    qi = jnp.arange(WINDOW)[:, None]
    kj = jnp.arange(2 * WINDOW)[None, :]
    buckets = jnp.arange(NUM_BUCKETS)
    table = rel_bias_table.astype(F32)
    oh_p = (_t5_bucket(qi - kj + WINDOW)[..., None] == buckets).astype(F32)
    bias_p = jnp.einsum("qkb,bh->hkq", oh_p, table, precision=lax.Precision.HIGHEST)
    in_window = jnp.transpose((kj > qi) & (kj <= qi + WINDOW))
    bias_p = jnp.where(in_window[None], bias_p, NEG_INF)
    bias_p = bias_p.reshape(SWA_KV_HEADS, SWA_GROUP, 2 * WINDOW, WINDOW).transpose(0, 2, 1, 3)
    bias_p = bias_p.reshape(SWA_KV_HEADS, 2 * WINDOW, SWA_GROUP * WINDOW)
    oh_d = (_t5_bucket(WINDOW - 1 - jnp.arange(WINDOW))[:, None] == buckets).astype(F32)
    bias_d = jnp.einsum("rb,bh->hr", oh_d, table, precision=lax.Precision.HIGHEST)
    bias_d = jnp.concatenate([bias_d, jnp.zeros((8, WINDOW), F32)], axis=0)
    wts = dict(
        sinks=attn_sinks[0].astype(F32), bias=bias_p,
        g_mix=g_mix[0][None], w_in=w_in_r, w_a_up=w_a_pad, b_a=b_a[0][None],
        g_gla=g_gla_out[0][None], g_swa=g_swa_out[0][None], w_out=w_out[0].astype(BF16),
        g_ffn=g_ffn[0][None], w_r=jnp.concatenate([wr_hi, wr_lo], axis=0), b_r=b_router[0][:, None],
    )

    x_pre = jnp.concatenate([jnp.zeros((WINDOW - N_META, D_MODEL), F32), meta_tokens.astype(F32)], axis=0)[None]
    zeros_s = jnp.zeros((GLA_QK, GLA_V), F32)
    zeros_kv = jnp.zeros((WINDOW, SWA_KV), F32)
    zeros_b = jnp.zeros((N_EXPERTS, LANES), F32)
    pre = _mixer_call(x_pre, zeros_s, zeros_kv, zeros_kv, zeros_b, wts, WINDOW, WINDOW - N_META, 0, WINDOW)
    s_meta, k_meta, v_meta = pre[5][0], pre[6][0], pre[7][0]

    (xmid_p, hp_p, topi_p, gate_p, rank_p, s_p, k_p, v_p, cnt_p) = _mixer_call(
        x_prompt, s_meta, k_meta, v_meta, zeros_b, wts, MIX_TM, 0, WINDOW - N_META, TP)

    (xmid_s, hp_s, topi_s, gate_s, rank_s, st_s, ck_s, cv_s, cnt_all) = _decode_call(
        x_sample[:, 0], state_gla[0], cache_swa_k[0].reshape(n_seq, WINDOW, SWA_KV),
        cache_swa_v[0].reshape(n_seq, WINDOW, SWA_KV), cnt_p, bias_d, wts)

    tm = MOE_TM
    n_slots = T_all * TOP_K
    n_blocks = -(-n_slots // tm) + N_EXPERTS
    top_e = jnp.concatenate([topi_p[:TOP_K], topi_s[:TOP_K]], axis=1)
    rank = jnp.concatenate([rank_p[:TOP_K], rank_s[:TOP_K]], axis=1)
    counts = cnt_all[:, 0].astype(jnp.int32)
    padded = (counts + tm - 1) // tm * tm
    pad_end = jnp.cumsum(padded)
    pad_start = pad_end - padded
    e_ids = jnp.arange(N_EXPERTS, dtype=jnp.int32)
    dest = jnp.sum(jnp.where(top_e[..., None] == e_ids, pad_start, 0), axis=-1) + rank
    n_pad = n_blocks * tm
    blk_e = jnp.minimum(jnp.sum(pad_end[None] <= (jnp.arange(n_blocks, dtype=jnp.int32) * tm)[:, None], axis=1),
                        N_EXPERTS - 1).astype(jnp.int32)
    nused = (pad_end[-1] // tm).astype(jnp.int32).reshape(1)

    sample_rows = 8
    idx_p = dest[:, :TP].reshape(TOP_K, TP // SC_SCATTER_ROWS, SC_SCATTER_ROWS).transpose(1, 0, 2)
    idx_s = dest[:, TP:].reshape(TOP_K, n_seq // sample_rows, sample_rows).transpose(1, 0, 2)
    xs3 = _sc_scatter_rows(hp_p.reshape(TP, SLAB, LANES), hp_s.reshape(n_seq, SLAB, LANES), idx_p, idx_s, n_pad)
    ys2 = _ffn_call(blk_e, nused, xs3.reshape(-1, LANES), w_up[0], b_up[0], w_down[0], b_down[0])
    unit = math.lcm(2 * SC_WORKERS * SC_GATHER_ROWS // TOP_K, MIX_TM)
    t_stride = -(-T_all // unit) * unit
    filler = jnp.arange(TOP_K * (t_stride - T_all), dtype=jnp.int32).reshape(TOP_K, t_stride - T_all)
    slot_src = jnp.concatenate([dest, filler], axis=1)
    slot_src = slot_src.reshape(TOP_K * t_stride // SC_GATHER_ROWS, SC_GATHER_ROWS)
    ys4 = _sc_gather_rows(ys2.reshape(-1, SLAB, LANES), slot_src).reshape(-1, LANES)

    gf = g_final[None]
    y_p = _combine_call(ys4, t_stride, 0, xmid_p.reshape(TP, D_MODEL), gate_p, gf, MIX_TM)
    y_s = _combine_call(ys4, t_stride, TP, xmid_s, gate_s, gf, n_seq)

    s_heads = jnp.stack([s_p[:, h * GLA_DK:(h + 1) * GLA_DK, h * GLA_DV:(h + 1) * GLA_DV]
                         for h in range(GLA_HEADS)], axis=1)
    return (y_p.reshape(B, L, D_MODEL), y_s.reshape(n_seq, 1, D_MODEL), s_heads[None],
            k_p.reshape(1, B, WINDOW, SWA_KV_HEADS, SWA_HEAD_DIM),
            v_p.reshape(1, B, WINDOW, SWA_KV_HEADS, SWA_HEAD_DIM),
            st_s[None], ck_s.reshape(1, n_seq, WINDOW, SWA_KV_HEADS, SWA_HEAD_DIM),
            cv_s.reshape(1, n_seq, WINDOW, SWA_KV_HEADS, SWA_HEAD_DIM))
```

```python
import functools
import math

import jax
import jax.numpy as jnp
from jax import lax
from jax.experimental import pallas as pl
from jax.experimental.pallas import tpu as pltpu
from jax.experimental.pallas import tpu_sc as plsc

D_MODEL = 1024
N_META = 16
GLA_HEADS = 4
GLA_DK = 64
GLA_DV = 128
GLA_LOWRANK = 16
GLA_GATE_TAU = 16.0
GLA_CHUNK = 64
SWA_HEADS = 8
SWA_KV_HEADS = 2
SWA_HEAD_DIM = 64
SWA_GROUP = SWA_HEADS // SWA_KV_HEADS
WINDOW = 128
NUM_BUCKETS = 32
MAX_DISTANCE = 128
N_EXPERTS = 32
TOP_K = 4
D_FF = 1024
SWIGLU_ALPHA = 1.702
SWIGLU_LIMIT = 7.0
RMS_EPS = 1e-6

GLA_QK = GLA_HEADS * GLA_DK
GLA_V = GLA_HEADS * GLA_DV
SWA_Q = SWA_HEADS * SWA_HEAD_DIM
SWA_KV = SWA_KV_HEADS * SWA_HEAD_DIM
LANES = 128
C_GQ, C_GK, C_GV, C_GR = 0, GLA_QK, 2 * GLA_QK, 2 * GLA_QK + GLA_V
C_SQ = C_GR + GLA_V
C_SK = C_SQ + SWA_Q
C_SV = C_SK + SWA_KV
C_GA = C_SV + SWA_KV
D_PROJ = C_GA + LANES

MIX_TM = 512
MIX_STREAM_TM = 256
MOE_TM = 512
DEC_SB = 16
VMEM_LIMIT = 56 * 1024 * 1024

F32 = jnp.float32
BF16 = jnp.bfloat16
NEG_INF = float("-inf")


def _dot(a, b):
    return jnp.dot(a, b, preferred_element_type=F32)


def _dot_nt(a, b):
    return lax.dot_general(a, b, (((1,), (1,)), ((), ())), preferred_element_type=F32)


def _split3(x):
    hi = x.astype(BF16)
    r1 = x - hi.astype(F32)
    mid = r1.astype(BF16)
    lo = (r1 - mid.astype(F32)).astype(BF16)
    return hi, mid, lo


def _rms(x, g):
    return x * lax.rsqrt(jnp.mean(x * x, axis=-1, keepdims=True) + RMS_EPS) * g


def _iota(shape, dim):
    return lax.broadcasted_iota(jnp.int32, shape, dim)


HALF_D = D_MODEL // 2
SLAB = HALF_D // LANES


def _pack_pair(lo, hi):
    bl = lax.bitcast_convert_type(lo.astype(BF16).astype(F32), jnp.uint32)
    bh = lax.bitcast_convert_type(hi.astype(BF16).astype(F32), jnp.uint32)
    return lax.bitcast_convert_type(bh | lax.shift_right_logical(bl, jnp.uint32(16)), jnp.int32)


def _unpack_pair(w):
    u = lax.bitcast_convert_type(w, jnp.uint32)
    lo = lax.bitcast_convert_type(lax.shift_left(u, jnp.uint32(16)), F32)
    hi = lax.bitcast_convert_type(u & jnp.uint32(0xFFFF0000), F32)
    return lo, hi


def _store_slabs(ref, x):
    rows = x.shape[0]
    for c in range(SLAB):
        sl = slice(c * LANES, (c + 1) * LANES)
        ref[pl.ds(c, rows, stride=SLAB), :] = _pack_pair(x[:, sl], x[:, HALF_D + c * LANES:HALF_D + (c + 1) * LANES])


def _load_slab_chunk(ref, rows, c):
    return _unpack_pair(ref[pl.ds(c, rows, stride=SLAB), :])


def _drain(steps):
    try:
        while True:
            next(steps)
    except StopIteration as done:
        return done.value


def _project_steps(x, g_mix, w_in_ref, w_a_up, b_a):
    h = _rms(x, g_mix).astype(BF16)
    yield

    def cols(lo, width):
        return _dot(h, w_in_ref[:, lo:lo + width])

    ga = cols(C_GA, LANES).astype(BF16)
    z = _dot(ga, w_a_up) + b_a
    log_a = -(jnp.maximum(-z, 0.0) + jnp.log(1.0 + jnp.exp(-jnp.abs(z)))) / GLA_GATE_TAU
    yield
    gqk = cols(C_GQ, 2 * GLA_QK)
    yield
    gv = cols(C_GV, GLA_V)
    yield
    swa = cols(C_SQ, SWA_Q + 2 * SWA_KV)
    yield
    gr = cols(C_GR, GLA_V)
    yield
    return dict(
        gq=gqk[:, :GLA_QK],
        gk=gqk[:, GLA_QK:],
        gv=gv,
        gr=gr,
        sq=swa[:, :SWA_Q],
        sk=swa[:, SWA_Q:SWA_Q + SWA_KV],
        sv=swa[:, SWA_Q + SWA_KV:],
        log_a=log_a,
    )


def _project(x, g_mix, w_in_ref, w_a_up, b_a):
    return _drain(_project_steps(x, g_mix, w_in_ref, w_a_up, b_a))


def _tail(*args):
    return _drain(_tail_steps(*args))


def _tail_steps(x, o_gla, gr, o_swa, g_gla_out, g_swa_out, w_out_ref, g_ffn, w_r, b_r, get_base):
    tm = x.shape[0]
    gate = gr * jax.nn.sigmoid(gr)
    parts = []
    for h in range(GLA_HEADS):
        sl = slice(h * GLA_DV, (h + 1) * GLA_DV)
        parts.append(_rms(o_gla[:, sl], g_gla_out) * gate[:, sl])
    og = jnp.concatenate(parts, axis=1).astype(BF16)
    yield
    os_ = _rms(o_swa, g_swa_out).astype(BF16)
    x_mid = x + _dot(og, w_out_ref[0:GLA_V])
    yield
    x_mid = x_mid + _dot(os_, w_out_ref[GLA_V:GLA_V + SWA_Q])
    yield
    hp = _rms(x_mid, g_ffn)

    h1 = hp.astype(BF16)
    h2 = (hp - h1.astype(F32)).astype(BF16)
    yield
    la = _dot_nt(w_r, h1)
    lb = _dot_nt(w_r[0:N_EXPERTS], h2)
    logits = la[0:N_EXPERTS] + la[N_EXPERTS:2 * N_EXPERTS] + lb + b_r
    yield

    eidx = _iota((N_EXPERTS, tm), 0)
    vals, idxs, onehots = [], [], []
    l = logits
    for _ in range(TOP_K):
        m = jnp.max(l, axis=0, keepdims=True)
        sel = jnp.min(jnp.where(l == m, eidx, N_EXPERTS), axis=0, keepdims=True)
        oh = eidx == sel
        l = jnp.where(oh, NEG_INF, l)
        vals.append(m)
        idxs.append(sel)
        onehots.append(oh)
    es = [jnp.exp(v - vals[0]) for v in vals]
    denom = es[0] + es[1] + es[2] + es[3]
    gates = [e / denom for e in es]

    ohf = jnp.concatenate([oh.astype(F32) for oh in onehots], axis=0)
    upper = (_iota((tm, tm), 0) < _iota((tm, tm), 1)).astype(BF16)
    prefix = _dot(ohf.astype(BF16), upper)
    yield
    base = get_base()
    ranks = []
    for k in range(TOP_K):
        sl = slice(k * N_EXPERTS, (k + 1) * N_EXPERTS)
        ohk = ohf[sl]
        base_t = jnp.concatenate([base] * (tm // LANES), axis=1)
        ranks.append(jnp.sum(ohk * (prefix[sl] + base_t), axis=0, keepdims=True))
        base = base + jnp.sum(ohk, axis=1, keepdims=True)
    zi = jnp.zeros((8 - TOP_K, tm), jnp.int32)
    zf = jnp.zeros((8 - TOP_K, tm), F32)
    topi = jnp.concatenate(idxs + [zi], axis=0)
    gate8 = jnp.concatenate(gates + [zf], axis=0)
    rank8 = jnp.concatenate([r.astype(jnp.int32) for r in ranks] + [zi], axis=0)
    return x_mid, hp, topi, gate8, rank8, base


def _gla_chunks(p, row0, s_blocks, n_lead_pad):
    tm = p["gq"].shape[0]
    nch = tm // GLA_CHUNK
    log_a = p["log_a"]
    if n_lead_pad:
        rows = row0 + _iota((tm, GLA_QK), 0)
        log_a = jnp.where(rows >= n_lead_pad, log_a, 0.0)
    ri, ci = _iota((tm, tm), 0), _iota((tm, tm), 1)
    tril = ((ri >= ci) & (ri // GLA_CHUNK == ci // GLA_CHUNK)).astype(BF16)
    hi, mid, lo = _split3(log_a)
    b_all = _dot(tril, hi) + _dot(tril, mid) + _dot(tril, lo)
    yield

    c64 = GLA_CHUNK
    kk_mask = (_iota((GLA_QK, GLA_QK), 0) // c64) == (_iota((GLA_QK, GLA_QK), 1) // GLA_DK)
    vv_mask = (_iota((GLA_QK, GLA_V), 0) // c64) == (_iota((GLA_QK, GLA_V), 1) // GLA_DV)
    zero_blk = jnp.zeros((GLA_DK, GLA_DV), BF16)
    causal = (_iota((c64, GLA_QK), 0) >= (_iota((c64, GLA_QK), 1) % c64)).astype(F32)
    zpad_k = jnp.zeros((LANES - c64, GLA_QK), F32)
    zpad_v = jnp.zeros((LANES - c64, GLA_V), BF16)

    outs = []
    for c in range(nch):
        rs = slice(c * c64, (c + 1) * c64)
        b = b_all[rs]
        q, k, v = p["gq"][rs], p["gk"][rs], p["gv"][rs]
        b_last = b[c64 - 1:c64]
        qt = (q * jnp.exp(b)).astype(BF16)
        kt = k * jnp.exp(-b)
        kd = k * jnp.exp(b_last - b)
        vb = v.astype(BF16)
        k_bd = jnp.where(kk_mask, jnp.concatenate([kt] * GLA_HEADS, axis=0), 0.0).astype(BF16)
        a = (_dot_nt(qt, k_bd) * causal).astype(BF16)
        v_bd = jnp.where(vv_mask, jnp.concatenate([vb] * GLA_HEADS, axis=0), jnp.zeros((), BF16))
        s_bd = jnp.concatenate(
            [jnp.concatenate([s_blocks[h].astype(BF16) if g == h else zero_blk for g in range(GLA_HEADS)], axis=1)
             for h in range(GLA_HEADS)], axis=0)
        outs.append(_dot(qt, s_bd) + _dot(a, v_bd))
        kd_t = jnp.transpose(jnp.concatenate([kd, zpad_k], axis=0)).astype(BF16)
        upd = _dot(kd_t, jnp.concatenate([vb, zpad_v], axis=0))
        decay = jnp.exp(jnp.transpose(jnp.broadcast_to(b_last, (LANES, GLA_QK))))
        s_blocks = [s_blocks[h] * decay[h * GLA_DK:(h + 1) * GLA_DK]
                    + upd[h * GLA_DK:(h + 1) * GLA_DK, h * GLA_DV:(h + 1) * GLA_DV] for h in range(GLA_HEADS)]
        yield
    return jnp.concatenate(outs, axis=0), s_blocks


def _swa_block(sq, kcat, vcat, bias_ref, sinks_ref, valid_t):
    half = _iota((1, LANES), 1) < SWA_HEAD_DIM
    top_rows = _iota((LANES, 1), 0) < SWA_HEAD_DIM
    k_roll = pltpu.roll(kcat, SWA_HEAD_DIM, 1)
    v_t = jnp.transpose(vcat)
    zeros_v = jnp.zeros((SWA_HEAD_DIM, 2 * WINDOW), F32)
    cols = []
    for kv in range(SWA_KV_HEADS):
        kk = jnp.where(half, kcat, k_roll) if kv == 0 else jnp.where(half, k_roll, kcat)
        q_parts = []
        for c in (2 * kv, 2 * kv + 1):
            qc = sq[:, c * LANES:(c + 1) * LANES]
            q_parts.append(jnp.where(half, qc, 0.0))
            q_parts.append(jnp.where(half, 0.0, qc))
        q_st = jnp.concatenate(q_parts, axis=0).astype(BF16)
        s = _dot_nt(kk.astype(BF16), q_st) + bias_ref[kv]
        if valid_t is not None:
            s = jnp.where(valid_t, s, NEG_INF)
        sink = jnp.concatenate(
            [jnp.full((1, WINDOW), sinks_ref[kv * SWA_GROUP + g], F32) for g in range(SWA_GROUP)], axis=1)
        m = jnp.maximum(jnp.max(s, axis=0, keepdims=True), sink)
        pr = jnp.exp(s - m)
        inv = 1.0 / (jnp.sum(pr, axis=0, keepdims=True) + jnp.exp(sink - m))
        pb = pr.astype(BF16)
        vk = v_t[kv * SWA_HEAD_DIM:(kv + 1) * SWA_HEAD_DIM]
        vv_t = jnp.concatenate([jnp.concatenate([vk, zeros_v], axis=1),
                                jnp.concatenate([zeros_v, vk], axis=1)], axis=0).astype(BF16)
        for pair in range(SWA_GROUP // 2):
            ce = slice(2 * pair * WINDOW, (2 * pair + 1) * WINDOW)
            co = slice((2 * pair + 1) * WINDOW, (2 * pair + 2) * WINDOW)
            p2_t = jnp.concatenate([pb[:, ce], pb[:, co]], axis=0)
            o2_t = _dot(vv_t, p2_t)
            o2_t = o2_t * jnp.where(top_rows, inv[:, ce], inv[:, co])
            cols.append(jnp.transpose(o2_t))
        yield
    return jnp.concatenate(cols, axis=1)


def _mixer_kernel(sinks_ref, x_ref, s0_ref, k0_ref, v0_ref, base0_ref, bias_ref,
                  g_mix_ref, w_in_ref, w_a_up_ref, b_a_ref, g_gla_ref, g_swa_ref, w_out_ref,
                  g_ffn_ref, w_r_ref, b_r_ref,
                  xmid_ref, hp_ref, topi_ref, gate_ref, rank_ref, sout_ref, kout_ref, vout_ref, cnt_ref,
                  s_scr, k_scr, v_scr, base_scr, *, n_lead_pad, prev_valid_from):
    g_id, j = pl.program_id(0), pl.program_id(1)
    n_streams, tm = x_ref.shape[1], x_ref.shape[2]

    @pl.when(j == 0)
    def _():
        for s in range(n_streams):
            s_scr[s] = s0_ref[...]
            k_scr[s] = k0_ref[...]
            v_scr[s] = v0_ref[...]

    @pl.when((j == 0) & (g_id == 0))
    def _():
        base_scr[...] = base0_ref[...]

    diag = [(slice(h * GLA_DK, (h + 1) * GLA_DK), slice(h * GLA_DV, (h + 1) * GLA_DV)) for h in range(GLA_HEADS)]

    def stream(s):
        x = x_ref[0, s]
        p = yield from _project_steps(x, g_mix_ref[...], w_in_ref, w_a_up_ref[...], b_a_ref[...])
        yield "mix"
        o_gla, s_blocks = yield from _gla_chunks(p, j * tm, [s_scr[s, r, c] for r, c in diag], n_lead_pad)
        for (r, c), blk in zip(diag, s_blocks):
            s_scr[s, r, c] = blk
        o_parts = []
        for sb in range(tm // WINDOW):
            rs = slice(sb * WINDOW, (sb + 1) * WINDOW)
            k_blk, v_blk = p["sk"][rs], p["sv"][rs]
            k_prev = k_scr[s] if sb == 0 else p["sk"][(sb - 1) * WINDOW:sb * WINDOW]
            v_prev = v_scr[s] if sb == 0 else p["sv"][(sb - 1) * WINDOW:sb * WINDOW]
            valid = None
            if sb == 0 and prev_valid_from:
                first = jnp.where(j == 0, prev_valid_from, 0)
                valid = _iota((2 * WINDOW, SWA_GROUP * WINDOW), 0) >= first
            o_parts.append((yield from _swa_block(
                p["sq"][rs], jnp.concatenate([k_prev, k_blk], axis=0),
                jnp.concatenate([v_prev, v_blk], axis=0), bias_ref, sinks_ref, valid)))
        o_swa = jnp.concatenate(o_parts, axis=0)
        k_scr[s] = p["sk"][tm - WINDOW:tm]
        v_scr[s] = p["sv"][tm - WINDOW:tm]
        yield "tail"
        x_mid, hp, topi, gate8, rank8, base = yield from _tail_steps(
            x, o_gla, p["gr"], o_swa, g_gla_ref[...], g_swa_ref[...], w_out_ref,
            g_ffn_ref[...], w_r_ref[...], b_r_ref[...], lambda: base_scr[...])
        base_scr[...] = base
        xmid_ref[0, s] = x_mid
        _store_slabs(hp_ref.at[0, s], hp)
        topi_ref[0, s] = topi
        gate_ref[0, s] = gate8
        rank_ref[0, s] = rank8
        sout_ref[0, s] = s_scr[s]
        kout_ref[0, s] = p["sk"][tm - WINDOW:tm]
        vout_ref[0, s] = p["sv"][tm - WINDOW:tm]
        cnt_ref[...] = base

    def advance(gen, stop):
        try:
            while next(gen) != stop or stop is None:
                pass
            return False
        except StopIteration:
            return True

    def alternate(gen_a, stop_a, gen_b, stop_b):
        done_a = done_b = False
        while not (done_a and done_b):
            if not done_a:
                try:
                    done_a = next(gen_a) == stop_a and stop_a is not None
                except StopIteration:
                    done_a = True
            if not done_b:
                try:
                    done_b = next(gen_b) == stop_b and stop_b is not None
                except StopIteration:
                    done_b = True

    if n_streams == 1:
        advance(stream(0), None)
    else:
        first, second = stream(0), stream(1)
        advance(first, "mix")
        alternate(first, "tail", second, "mix")
        alternate(first, None, second, "tail")
        advance(second, None)


def _full_spec(shape):
    nd = len(shape)
    return pl.BlockSpec(shape, lambda *_: (0,) * nd)


def _mixer_call(x, s0, k0, v0, base0, wts, tm, n_streams, n_lead_pad, prev_valid_from):
    B, L, _ = x.shape
    S = n_streams
    G = B // S
    nj = L // tm
    weight_args = (wts["bias"], wts["g_mix"], wts["w_in"], wts["w_a_up"], wts["b_a"], wts["g_gla"],
                   wts["g_swa"], wts["w_out"], wts["g_ffn"], wts["w_r"], wts["b_r"])
    in_specs = [
        pl.BlockSpec(memory_space=pltpu.SMEM),
        pl.BlockSpec((1, S, tm, D_MODEL), lambda g, j: (g, 0, j, 0)),
        _full_spec(s0.shape), _full_spec(k0.shape), _full_spec(v0.shape), _full_spec(base0.shape),
    ] + [_full_spec(w.shape) for w in weight_args]
    tok_spec = pl.BlockSpec((1, S, 8, tm), lambda g, j: (g, 0, 0, j))

    def per_seq_spec(rows, cols):
        return pl.BlockSpec((1, S, rows, cols), lambda g, j: (g, 0, 0, 0))

    out_specs = [
        pl.BlockSpec((1, S, tm, D_MODEL), lambda g, j: (g, 0, j, 0)),
        pl.BlockSpec((1, S, tm * SLAB, LANES), lambda g, j: (g, 0, j, 0)),
        tok_spec, tok_spec, tok_spec,
        per_seq_spec(GLA_QK, GLA_V), per_seq_spec(WINDOW, SWA_KV), per_seq_spec(WINDOW, SWA_KV),
        _full_spec((N_EXPERTS, LANES)),
    ]
    out_shape = [
        jax.ShapeDtypeStruct((G, S, L, D_MODEL), F32),
        jax.ShapeDtypeStruct((G, S, L * SLAB, LANES), jnp.int32),
        jax.ShapeDtypeStruct((G, S, 8, L), jnp.int32),
        jax.ShapeDtypeStruct((G, S, 8, L), F32),
        jax.ShapeDtypeStruct((G, S, 8, L), jnp.int32),
        jax.ShapeDtypeStruct((G, S, GLA_QK, GLA_V), F32),
        jax.ShapeDtypeStruct((G, S, WINDOW, SWA_KV), F32),
        jax.ShapeDtypeStruct((G, S, WINDOW, SWA_KV), F32),
        jax.ShapeDtypeStruct((N_EXPERTS, LANES), F32),
    ]
    kern = functools.partial(_mixer_kernel, n_lead_pad=n_lead_pad, prev_valid_from=prev_valid_from)
    xmid, hp, topi, gate, rank, s_out, k_out, v_out, cnt = pl.pallas_call(
        kern,
        grid=(G, nj),
        in_specs=in_specs,
        out_specs=out_specs,
        out_shape=out_shape,
        scratch_shapes=[pltpu.VMEM((S, GLA_QK, GLA_V), F32), pltpu.VMEM((S, WINDOW, SWA_KV), F32),
                        pltpu.VMEM((S, WINDOW, SWA_KV), F32), pltpu.VMEM((N_EXPERTS, LANES), F32)],
        compiler_params=pltpu.CompilerParams(dimension_semantics=("arbitrary", "arbitrary"),
                                             vmem_limit_bytes=VMEM_LIMIT),
        name="mixer",
    )(wts["sinks"], x.reshape(G, S, L, D_MODEL), s0, k0, v0, base0, *weight_args)

    def rows8(a):
        return jnp.transpose(a, (2, 0, 1, 3)).reshape(8, B * L)

    return (xmid.reshape(B, L, D_MODEL), hp.reshape(B * L * SLAB, LANES), rows8(topi), rows8(gate), rows8(rank),
            s_out.reshape(B, GLA_QK, GLA_V), k_out.reshape(B, WINDOW, SWA_KV), v_out.reshape(B, WINDOW, SWA_KV), cnt)


def _decode_kernel(sinks_ref, x_ref, st_ref, ck_ref, cv_ref, base0_ref, bias_ref,
                   g_mix_ref, w_in_ref, w_a_up_ref, b_a_ref, g_gla_ref, g_swa_ref, w_out_ref,
                   g_ffn_ref, w_r_ref, b_r_ref,
                   xmid_ref, hp_ref, topi_ref, gate_ref, rank_ref, sto_ref, cko_ref, cvo_ref, cnt_ref,
                   tq_scr, gv_scr, gr_scr, sq_scr, sk_scr, sv_scr, og_scr, os_scr):
    i = pl.program_id(0)
    n_seq = x_ref.shape[0]

    @pl.when(i == 0)
    def _():
        p = _project(x_ref[...], g_mix_ref[...], w_in_ref, w_a_up_ref[...], b_a_ref[...])
        a_hi, a_mid, a_lo = _split3(jnp.transpose(jnp.exp(p["log_a"])))
        tq_scr[...] = jnp.concatenate(
            [a_hi, a_mid, a_lo, jnp.transpose(p["gk"]).astype(BF16), jnp.transpose(p["gq"]).astype(BF16)], axis=0)
        gv_scr[...] = p["gv"]
        gr_scr[...] = p["gr"]
        sq_scr[...] = p["sq"]
        sk_scr[...] = p["sk"]
        sv_scr[...] = p["sv"]

    seq_row = _iota((n_seq, LANES), 0)
    half = _iota((1, LANES), 1) < SWA_HEAD_DIM
    row_id = _iota((WINDOW, SWA_KV), 0)
    head_diag = (_iota((16, SWA_Q), 1) // SWA_HEAD_DIM) == _iota((16, SWA_Q), 0)
    sink_col = jnp.concatenate(
        [jnp.full((1, 1), sinks_ref[h], F32) for h in range(SWA_HEADS)] + [jnp.zeros((8, 1), F32)], axis=0)

    def per_seq(sl, carry):
        s = i * DEC_SB + sl
        pick = (seq_row == s).astype(BF16)
        cols = _dot(tq_scr[...], pick)
        a_c = cols[0:GLA_QK] + cols[GLA_QK:2 * GLA_QK] + cols[2 * GLA_QK:3 * GLA_QK]
        k_c = cols[3 * GLA_QK:4 * GLA_QK]
        q_c = cols[4 * GLA_QK:5 * GLA_QK]
        st = st_ref[sl].reshape(GLA_QK, GLA_DV)
        v_row = gv_scr[pl.ds(s, 1), :]
        v_b = jnp.concatenate(
            [jnp.broadcast_to(v_row[:, h * GLA_DV:(h + 1) * GLA_DV], (GLA_DK, GLA_DV))
             for h in range(GLA_HEADS)], axis=0)
        st_new = a_c * st + k_c * v_b
        sto_ref[sl] = st_new.reshape(GLA_HEADS, GLA_DK, GLA_DV)
        t = q_c * st_new
        og_scr[pl.ds(s, 1), :] = jnp.concatenate(
            [jnp.sum(t[h * GLA_DK:(h + 1) * GLA_DK], axis=0, keepdims=True) for h in range(GLA_HEADS)],
            axis=1)

        k_new = sk_scr[pl.ds(s, 1), :]
        v_new = sv_scr[pl.ds(s, 1), :]
        kn = jnp.where(row_id == WINDOW - 1, k_new, pltpu.roll(ck_ref[sl], WINDOW - 1, 0))
        vn = jnp.where(row_id == WINDOW - 1, v_new, pltpu.roll(cv_ref[sl], WINDOW - 1, 0))
        cko_ref[sl] = kn
        cvo_ref[sl] = vn
        kr, vr = pltpu.roll(kn, SWA_HEAD_DIM, 1), pltpu.roll(vn, SWA_HEAD_DIM, 1)
        k0, k1 = jnp.where(half, kn, kr), jnp.where(half, kr, kn)
        v0, v1 = jnp.where(half, vn, vr), jnp.where(half, vr, vn)
        kw = jnp.concatenate([k0, k0, k1, k1], axis=1).astype(BF16)
        vw = jnp.concatenate([v0, v0, v1, v1], axis=1).astype(BF16)
        q_row = sq_scr[pl.ds(s, 1), :]
        qm = jnp.where(head_diag, jnp.broadcast_to(q_row, (16, SWA_Q)), 0.0).astype(BF16)
        sc = _dot_nt(qm, kw) + bias_ref[...]
        m = jnp.maximum(jnp.max(sc, axis=1, keepdims=True), sink_col)
        pr = jnp.exp(sc - m)
        inv = 1.0 / (jnp.sum(pr, axis=1, keepdims=True) + jnp.exp(sink_col - m))
        ow = _dot(pr.astype(BF16), vw) * inv
        os_scr[pl.ds(s, 1), :] = jnp.sum(jnp.where(head_diag, ow, 0.0), axis=0, keepdims=True)
        return carry

    lax.fori_loop(0, DEC_SB, per_seq, 0, unroll=8)

    @pl.when(i == pl.num_programs(0) - 1)
    def _():
        x_mid, hp, topi, gate8, rank8, base = _tail(
            x_ref[...], og_scr[...], gr_scr[...], os_scr[...], g_gla_ref[...], g_swa_ref[...],
            w_out_ref, g_ffn_ref[...], w_r_ref[...], b_r_ref[...], lambda: base0_ref[...])
        xmid_ref[...] = x_mid
        _store_slabs(hp_ref, hp)
        topi_ref[...] = topi
        gate_ref[...] = gate8
        rank_ref[...] = rank8
        cnt_ref[...] = base


def _decode_call(xs, state, ck, cv, base0, bias_dec, wts):
    n_seq = xs.shape[0]
    nb = n_seq // DEC_SB
    weight_args = (wts["g_mix"], wts["w_in"], wts["w_a_up"], wts["b_a"], wts["g_gla"],
                   wts["g_swa"], wts["w_out"], wts["g_ffn"], wts["w_r"], wts["b_r"])
    in_specs = [
        pl.BlockSpec(memory_space=pltpu.SMEM),
        _full_spec(xs.shape),
        pl.BlockSpec((DEC_SB, GLA_HEADS, GLA_DK, GLA_DV), lambda i: (i, 0, 0, 0)),
        pl.BlockSpec((DEC_SB, WINDOW, SWA_KV), lambda i: (i, 0, 0)),
        pl.BlockSpec((DEC_SB, WINDOW, SWA_KV), lambda i: (i, 0, 0)),
        _full_spec(base0.shape), _full_spec(bias_dec.shape),
    ] + [_full_spec(w.shape) for w in weight_args]
    out_specs = [
        _full_spec((n_seq, D_MODEL)),
        _full_spec((n_seq * SLAB, LANES)),
        _full_spec((8, n_seq)), _full_spec((8, n_seq)), _full_spec((8, n_seq)),
        pl.BlockSpec((DEC_SB, GLA_HEADS, GLA_DK, GLA_DV), lambda i: (i, 0, 0, 0)),
        pl.BlockSpec((DEC_SB, WINDOW, SWA_KV), lambda i: (i, 0, 0)),
        pl.BlockSpec((DEC_SB, WINDOW, SWA_KV), lambda i: (i, 0, 0)),
        _full_spec((N_EXPERTS, LANES)),
    ]
    out_shape = [
        jax.ShapeDtypeStruct((n_seq, D_MODEL), F32),
        jax.ShapeDtypeStruct((n_seq * SLAB, LANES), jnp.int32),
        jax.ShapeDtypeStruct((8, n_seq), jnp.int32),
        jax.ShapeDtypeStruct((8, n_seq), F32),
        jax.ShapeDtypeStruct((8, n_seq), jnp.int32),
        jax.ShapeDtypeStruct(state.shape, F32),
        jax.ShapeDtypeStruct(ck.shape, F32),
        jax.ShapeDtypeStruct(cv.shape, F32),
        jax.ShapeDtypeStruct((N_EXPERTS, LANES), F32),
    ]
    scratch = [pltpu.VMEM((5 * GLA_QK, n_seq), BF16)] + [
        pltpu.VMEM((n_seq, GLA_V), F32), pltpu.VMEM((n_seq, GLA_V), F32), pltpu.VMEM((n_seq, SWA_Q), F32),
        pltpu.VMEM((n_seq, SWA_KV), F32), pltpu.VMEM((n_seq, SWA_KV), F32),
        pltpu.VMEM((n_seq, GLA_V), F32), pltpu.VMEM((n_seq, SWA_Q), F32)]
    return pl.pallas_call(
        _decode_kernel,
        grid=(nb,),
        in_specs=in_specs,
        out_specs=out_specs,
        out_shape=out_shape,
        scratch_shapes=scratch,
        compiler_params=pltpu.CompilerParams(dimension_semantics=("arbitrary",),
                                             vmem_limit_bytes=VMEM_LIMIT),
        name="decode",
    )(wts["sinks"], xs, state, ck, cv, base0, bias_dec, *weight_args)


SC_CORES = 2
SC_SUBCORES = 16
SC_WORKERS = SC_CORES * SC_SUBCORES
SC_SCATTER_ROWS = 64
SC_GATHER_ROWS = 48


def _sc_mesh():
    return plsc.VectorSubcoreMesh(core_axis_name="c", subcore_axis_name="s")


def _sc_worker_id():
    return lax.axis_index("s") * SC_CORES + lax.axis_index("c")


def _sc_scatter_rows(src_p, src_s, idx_p, idx_s, n_out):
    rows = SC_SCATTER_ROWS
    n_chunks = idx_p.shape[0] // SC_WORKERS
    n_s, _, rows_s = idx_s.shape
    assert n_chunks * SC_WORKERS == idx_p.shape[0] and n_chunks % 2 == 0 and n_s <= SC_WORKERS

    @functools.partial(
        pl.kernel, mesh=_sc_mesh(),
        out_type=jax.ShapeDtypeStruct((n_out, SLAB, LANES), jnp.int32),
        scratch_types=[pltpu.VMEM((2, TOP_K, rows), jnp.int32), pltpu.VMEM((2, rows, SLAB, LANES), jnp.int32),
                       pltpu.VMEM((TOP_K, rows_s), jnp.int32), pltpu.VMEM((rows_s, SLAB, LANES), jnp.int32),
                       pltpu.SemaphoreType.DMA((2,)), pltpu.SemaphoreType.DMA((2,))])
    def scatter_rows(srcp_hbm, srcs_hbm, idxp_hbm, idxs_hbm, out_hbm, idx_v, rows_v, idxs_v, rowss_v, lsem, ssem):
        wid = _sc_worker_id()

        def loads(c, b):
            g = wid * n_chunks + c
            return (pltpu.make_async_copy(idxp_hbm.at[g], idx_v.at[b], lsem.at[b]),
                    pltpu.make_async_copy(srcp_hbm.at[pl.ds(pl.multiple_of(g * rows, 8), rows)], rows_v.at[b],
                                          lsem.at[b]))

        def scatters(b):
            return [pltpu.make_async_copy(rows_v.at[b], out_hbm.at[idx_v.at[b, k]], ssem.at[b])
                    for k in range(TOP_K)]

        for d in loads(0, 0):
            d.start()

        @pl.loop(0, n_chunks, step=2)
        def _(c0):
            for b in range(2):
                c = c0 + b
                for d in loads(c, b):
                    d.wait()

                @pl.when(c >= 1)
                def _():
                    for d in scatters(1 - b):
                        d.wait()

                @pl.when(c + 1 < n_chunks)
                def _():
                    for d in loads(c + 1, 1 - b):
                        d.start()

                for d in scatters(b):
                    d.start()

        for d in scatters((n_chunks - 1) % 2):
            d.wait()

        @pl.when(wid < n_s)
        def _():
            pltpu.sync_copy(idxs_hbm.at[wid], idxs_v)
            pltpu.sync_copy(srcs_hbm.at[pl.ds(pl.multiple_of(wid * rows_s, 8), rows_s)], rowss_v)
            for k in range(TOP_K):
                pltpu.sync_copy(rowss_v, out_hbm.at[idxs_v.at[k]])

    return scatter_rows(src_p, src_s, idx_p, idx_s)


def _sc_gather_rows(src3, idx2):
    rows = SC_GATHER_ROWS
    n_chunks = idx2.shape[0] // SC_WORKERS
    assert n_chunks * SC_WORKERS == idx2.shape[0] and idx2.shape[1] == rows and n_chunks % 2 == 0

    @functools.partial(
        pl.kernel, mesh=_sc_mesh(),
        out_type=jax.ShapeDtypeStruct((idx2.shape[0] * rows, SLAB, LANES), jnp.int32),
        scratch_types=[pltpu.VMEM((2, rows), jnp.int32), pltpu.VMEM((2, rows, SLAB, LANES), jnp.int32),
                       pltpu.SemaphoreType.DMA((2,)), pltpu.SemaphoreType.DMA((2,))])
    def gather_rows(src_hbm, idx_hbm, out_hbm, idx_v, rows_v, gsem, wsem):
        wid = _sc_worker_id()

        def gather(b):
            return pltpu.make_async_copy(src_hbm.at[idx_v.at[b]], rows_v.at[b], gsem.at[b])

        def write(c, b):
            base = pl.multiple_of((wid * n_chunks + c) * rows, 8)
            return pltpu.make_async_copy(rows_v.at[b], out_hbm.at[pl.ds(base, rows)], wsem.at[b])

        pltpu.sync_copy(idx_hbm.at[wid * n_chunks], idx_v.at[0])
        gather(0).start()

        @pl.loop(0, n_chunks, step=2)
        def _(c0):
            for b in range(2):
                c = c0 + b

                @pl.when(c + 1 < n_chunks)
                def _():
                    @pl.when(c >= 1)
                    def _():
                        write(c - 1, 1 - b).wait()
                    pltpu.sync_copy(idx_hbm.at[wid * n_chunks + c + 1], idx_v.at[1 - b])
                    gather(1 - b).start()

                gather(b).wait()
                write(c, b).start()

        write(n_chunks - 2, 0).wait()
        write(n_chunks - 1, 1).wait()

    return gather_rows(src3, idx2)


FF_TILE = 256


def _ffn_kernel(blk_e_ref, nused_ref, x_ref, wu_ref, bu_ref, wd_ref, bd_ref, y_ref, xbf, actbf, wu_bf, wd_bf):
    i = pl.program_id(0)
    tm = MOE_TM
    n_tiles = D_FF // FF_TILE

    @pl.when(i < nused_ref[0])
    def _():
        @pl.when((i == 0) | (blk_e_ref[i] != blk_e_ref[jnp.maximum(i - 1, 0)]))
        def _():
            wu_bf[...] = wu_ref[0].astype(BF16)
            wd_bf[...] = wd_ref[0].astype(BF16)

        for c in range(SLAB):
            lo, hi = _load_slab_chunk(x_ref, tm, c)
            xbf[:, c * LANES:(c + 1) * LANES] = lo.astype(BF16)
            xbf[:, HALF_D + c * LANES:HALF_D + (c + 1) * LANES] = hi.astype(BF16)
        for n in range(n_tiles):
            gc = slice(n * FF_TILE, (n + 1) * FF_TILE)
            lc = slice(D_FF + n * FF_TILE, D_FF + (n + 1) * FF_TILE)
            g = jnp.minimum(_dot(xbf[...], wu_bf[:, gc]) + bu_ref[0, :, gc], SWIGLU_LIMIT)
            lin = jnp.clip(_dot(xbf[...], wu_bf[:, lc]) + bu_ref[0, :, lc], -SWIGLU_LIMIT, SWIGLU_LIMIT)
            actbf[:, gc] = (g * jax.nn.sigmoid(SWIGLU_ALPHA * g) * (lin + 1.0)).astype(BF16)
        per_tile = FF_TILE // LANES
        for n in range(n_tiles // 2):
            yl = slice(n * FF_TILE, (n + 1) * FF_TILE)
            yh = slice(HALF_D + n * FF_TILE, HALF_D + (n + 1) * FF_TILE)
            y_lo = _dot(actbf[...], wd_bf[:, yl]) + bd_ref[0, :, yl]
            y_hi = _dot(actbf[...], wd_bf[:, yh]) + bd_ref[0, :, yh]
            for c in range(per_tile):
                sl = slice(c * LANES, (c + 1) * LANES)
                y_ref[pl.ds(n * per_tile + c, tm, stride=SLAB), :] = _pack_pair(y_lo[:, sl], y_hi[:, sl])

    @pl.when(i >= nused_ref[0])
    def _():
        y_ref[...] = jnp.zeros_like(y_ref)


def _ffn_call(blk_e, nused, xs2, w_up, b_up, w_down, b_down):
    n_blocks = blk_e.shape[0]
    tm = MOE_TM
    row_blk = pl.BlockSpec((tm * SLAB, LANES), lambda i, be, nu: (i, 0))
    grid_spec = pltpu.PrefetchScalarGridSpec(
        num_scalar_prefetch=2,
        grid=(n_blocks,),
        in_specs=[
            row_blk,
            pl.BlockSpec((1, D_MODEL, 2 * D_FF), lambda i, be, nu: (be[i], 0, 0)),
            pl.BlockSpec((1, 1, 2 * D_FF), lambda i, be, nu: (be[i], 0, 0)),
            pl.BlockSpec((1, D_FF, D_MODEL), lambda i, be, nu: (be[i], 0, 0)),
            pl.BlockSpec((1, 1, D_MODEL), lambda i, be, nu: (be[i], 0, 0)),
        ],
        out_specs=row_blk,
        scratch_shapes=[pltpu.VMEM((tm, D_MODEL), BF16), pltpu.VMEM((tm, D_FF), BF16),
                        pltpu.VMEM((D_MODEL, 2 * D_FF), BF16), pltpu.VMEM((D_FF, D_MODEL), BF16)],
    )
    return pl.pallas_call(
        _ffn_kernel,
        grid_spec=grid_spec,
        out_shape=jax.ShapeDtypeStruct((n_blocks * tm * SLAB, LANES), jnp.int32),
        compiler_params=pltpu.CompilerParams(dimension_semantics=("arbitrary",),
                                             vmem_limit_bytes=VMEM_LIMIT),
        name="experts",
    )(blk_e, nused, xs2, w_up, b_up.reshape(N_EXPERTS, 1, 2 * D_FF), w_down, b_down.reshape(N_EXPERTS, 1, D_MODEL))


def _combine_kernel(ys0_ref, ys1_ref, ys2_ref, ys3_ref, xmid_ref, gate_ref, g_final_ref, y_ref):
    tm = xmid_ref.shape[0]
    gts = jnp.transpose(jnp.concatenate([gate_ref[...], jnp.zeros((LANES - 8, tm), F32)], axis=0))
    lows, highs = [], []
    for c in range(SLAB):
        acc_lo = xmid_ref[:, c * LANES:(c + 1) * LANES]
        acc_hi = xmid_ref[:, HALF_D + c * LANES:HALF_D + (c + 1) * LANES]
        for k, ys_ref in enumerate((ys0_ref, ys1_ref, ys2_ref, ys3_ref)):
            lo, hi = _load_slab_chunk(ys_ref, tm, c)
            acc_lo = acc_lo + lo * gts[:, k:k + 1]
            acc_hi = acc_hi + hi * gts[:, k:k + 1]
        lows.append(acc_lo)
        highs.append(acc_hi)
    y_ref[...] = _rms(jnp.concatenate(lows + highs, axis=1), g_final_ref[...])


def _combine_call(ys4, t_stride, row0, x_mid, gates, g_final, tm):
    T = x_mid.shape[0]
    blk0 = row0 // tm
    per_k = t_stride // tm
    assert per_k * tm == t_stride and blk0 * tm == row0

    def ys_spec(k):
        return pl.BlockSpec((tm * SLAB, LANES), lambda i: (k * per_k + blk0 + i, 0))

    return pl.pallas_call(
        _combine_kernel,
        grid=(T // tm,),
        in_specs=[
            ys_spec(0), ys_spec(1), ys_spec(2), ys_spec(3),
            pl.BlockSpec((tm, D_MODEL), lambda i: (i, 0)),
            pl.BlockSpec((8, tm), lambda i: (0, i)),
            _full_spec((1, D_MODEL)),
        ],
        out_specs=pl.BlockSpec((tm, D_MODEL), lambda i: (i, 0)),
        out_shape=jax.ShapeDtypeStruct((T, D_MODEL), F32),
        compiler_params=pltpu.CompilerParams(dimension_semantics=("arbitrary",),
                                             vmem_limit_bytes=VMEM_LIMIT),
        name="combine",
    )(ys4, ys4, ys4, ys4, x_mid, gates, g_final)


def _t5_bucket(dist):
    n = jnp.maximum(dist, 0)
    max_exact = NUM_BUCKETS // 2
    nf = jnp.maximum(n, 1).astype(F32)
    large = max_exact + (jnp.log(nf / max_exact) / math.log(MAX_DISTANCE / max_exact)
                         * (NUM_BUCKETS - max_exact)).astype(jnp.int32)
    large = jnp.minimum(large, NUM_BUCKETS - 1)
    return jnp.where(n < max_exact, n, large)


def kernel(x_prompt, x_sample, state_gla, cache_swa_k, cache_swa_v, meta_tokens, rel_bias_table,
           g_mix, w_in, w_a_up, b_a, g_gla_out, g_swa_out, attn_sinks, w_out,
           g_ffn, w_router, b_router, w_up, b_up, w_down, b_down, g_final):
    assert g_mix.shape[0] == 1, "single-layer trunk"
    B, L, _ = x_prompt.shape
    n_seq = x_sample.shape[0]
    TP = B * L
    T_all = TP + n_seq

    wi = w_in[0]
    sizes = (GLA_QK, GLA_QK, GLA_V, GLA_V, GLA_LOWRANK, SWA_Q, SWA_KV, SWA_KV)
    offs = [0]
    for s in sizes:
        offs.append(offs[-1] + s)
    seg = [wi[:, offs[n]:offs[n + 1]] for n in range(8)]
    seg[0] = seg[0] * (GLA_DK ** -0.5)
    seg[5] = seg[5] * (SWA_HEAD_DIM ** -0.5)
    w_in_r = jnp.concatenate(
        seg[0:4] + seg[5:8] + [seg[4], jnp.zeros((D_MODEL, LANES - GLA_LOWRANK), F32)], axis=1).astype(BF16)
    w_a_pad = jnp.concatenate([w_a_up[0], jnp.zeros((LANES - GLA_LOWRANK, GLA_QK), F32)], axis=0).astype(BF16)
    wr_t = jnp.transpose(w_router[0])
    wr_hi = wr_t.astype(BF16)
    wr_lo = (wr_t - wr_hi.astype(F32)).astype(BF16)
    qi = jnp.arange(WINDOW)[:, None]
    kj = jnp.arange(2 * WINDOW)[None, :]
    buckets = jnp.arange(NUM_BUCKETS)
    table = rel_bias_table.astype(F32)
    oh_p = (_t5_bucket(qi - kj + WINDOW)[..., None] == buckets).astype(F32)
    bias_p = jnp.einsum("qkb,bh->hkq", oh_p, table, precision=lax.Precision.HIGHEST)
    in_window = jnp.transpose((kj > qi) & (kj <= qi + WINDOW))
    bias_p = jnp.where(in_window[None], bias_p, NEG_INF)
    bias_p = bias_p.reshape(SWA_KV_HEADS, SWA_GROUP, 2 * WINDOW, WINDOW).transpose(0, 2, 1, 3)
    bias_p = bias_p.reshape(SWA_KV_HEADS, 2 * WINDOW, SWA_GROUP * WINDOW)
    oh_d = (_t5_bucket(WINDOW - 1 - jnp.arange(WINDOW))[:, None] == buckets).astype(F32)
    bias_d = jnp.einsum("rb,bh->hr", oh_d, table, precision=lax.Precision.HIGHEST)
    bias_d = jnp.concatenate([bias_d, jnp.zeros((8, WINDOW), F32)], axis=0)
    wts = dict(
        sinks=attn_sinks[0].astype(F32), bias=bias_p,
        g_mix=g_mix[0][None], w_in=w_in_r, w_a_up=w_a_pad, b_a=b_a[0][None],
        g_gla=g_gla_out[0][None], g_swa=g_swa_out[0][None], w_out=w_out[0].astype(BF16),
        g_ffn=g_ffn[0][None], w_r=jnp.concatenate([wr_hi, wr_lo], axis=0), b_r=b_router[0][:, None],
    )

    x_pre = jnp.concatenate([jnp.zeros((WINDOW - N_META, D_MODEL), F32), meta_tokens.astype(F32)], axis=0)[None]
    zeros_s = jnp.zeros((GLA_QK, GLA_V), F32)
    zeros_kv = jnp.zeros((WINDOW, SWA_KV), F32)
    zeros_b = jnp.zeros((N_EXPERTS, LANES), F32)
    pre = _mixer_call(x_pre, zeros_s, zeros_kv, zeros_kv, zeros_b, wts, WINDOW, 1, WINDOW - N_META, 0)
    s_meta, k_meta, v_meta = pre[5][0], pre[6][0], pre[7][0]

    (xmid_p, hp_p, topi_p, gate_p, rank_p, s_p, k_p, v_p, cnt_p) = _mixer_call(
        x_prompt, s_meta, k_meta, v_meta, zeros_b, wts, MIX_STREAM_TM, 2, 0, WINDOW - N_META)

    (xmid_s, hp_s, topi_s, gate_s, rank_s, st_s, ck_s, cv_s, cnt_all) = _decode_call(
        x_sample[:, 0], state_gla[0], cache_swa_k[0].reshape(n_seq, WINDOW, SWA_KV),
        cache_swa_v[0].reshape(n_seq, WINDOW, SWA_KV), cnt_p, bias_d, wts)

    tm = MOE_TM
    n_slots = T_all * TOP_K
    n_blocks = -(-n_slots // tm) + N_EXPERTS
    top_e = jnp.concatenate([topi_p[:TOP_K], topi_s[:TOP_K]], axis=1)
    rank = jnp.concatenate([rank_p[:TOP_K], rank_s[:TOP_K]], axis=1)
    counts = cnt_all[:, 0].astype(jnp.int32)
    padded = (counts + tm - 1) // tm * tm
    pad_end = jnp.cumsum(padded)
    pad_start = pad_end - padded
    e_ids = jnp.arange(N_EXPERTS, dtype=jnp.int32)
    dest = jnp.sum(jnp.where(top_e[..., None] == e_ids, pad_start, 0), axis=-1) + rank
    n_pad = n_blocks * tm
    blk_e = jnp.minimum(jnp.sum(pad_end[None] <= (jnp.arange(n_blocks, dtype=jnp.int32) * tm)[:, None], axis=1),
                        N_EXPERTS - 1).astype(jnp.int32)
    nused = (pad_end[-1] // tm).astype(jnp.int32).reshape(1)

    sample_rows = 8
    idx_p = dest[:, :TP].reshape(TOP_K, TP // SC_SCATTER_ROWS, SC_SCATTER_ROWS).transpose(1, 0, 2)
    idx_s = dest[:, TP:].reshape(TOP_K, n_seq // sample_rows, sample_rows).transpose(1, 0, 2)
    xs3 = _sc_scatter_rows(hp_p.reshape(TP, SLAB, LANES), hp_s.reshape(n_seq, SLAB, LANES), idx_p, idx_s, n_pad)
    ys2 = _ffn_call(blk_e, nused, xs3.reshape(-1, LANES), w_up[0], b_up[0], w_down[0], b_down[0])
    unit = math.lcm(2 * SC_WORKERS * SC_GATHER_ROWS // TOP_K, MIX_TM)
    t_stride = -(-T_all // unit) * unit
    filler = jnp.arange(TOP_K * (t_stride - T_all), dtype=jnp.int32).reshape(TOP_K, t_stride - T_all)
    slot_src = jnp.concatenate([dest, filler], axis=1)
    slot_src = slot_src.reshape(TOP_K * t_stride // SC_GATHER_ROWS, SC_GATHER_ROWS)
    ys4 = _sc_gather_rows(ys2.reshape(-1, SLAB, LANES), slot_src).reshape(-1, LANES)

    gf = g_final[None]
    y_p = _combine_call(ys4, t_stride, 0, xmid_p.reshape(TP, D_MODEL), gate_p, gf, MIX_TM)
    y_s = _combine_call(ys4, t_stride, TP, xmid_s, gate_s, gf, n_seq)

    s_heads = jnp.stack([s_p[:, h * GLA_DK:(h + 1) * GLA_DK, h * GLA_DV:(h + 1) * GLA_DV]
                         for h in range(GLA_HEADS)], axis=1)
    return (y_p.reshape(B, L, D_MODEL), y_s.reshape(n_seq, 1, D_MODEL), s_heads[None],
            k_p.reshape(1, B, WINDOW, SWA_KV_HEADS, SWA_HEAD_DIM),
            v_p.reshape(1, B, WINDOW, SWA_KV_HEADS, SWA_HEAD_DIM),
            st_s[None], ck_s.reshape(1, n_seq, WINDOW, SWA_KV_HEADS, SWA_HEAD_DIM),
            cv_s.reshape(1, n_seq, WINDOW, SWA_KV_HEADS, SWA_HEAD_DIM))
```

```python
import functools
import math

import jax
import jax.numpy as jnp
from jax import lax
from jax.experimental import pallas as pl
from jax.experimental.pallas import tpu as pltpu
from jax.experimental.pallas import tpu_sc as plsc

D_MODEL = 1024
N_META = 16
GLA_HEADS = 4
GLA_DK = 64
GLA_DV = 128
GLA_LOWRANK = 16
GLA_GATE_TAU = 16.0
GLA_CHUNK = 64
SWA_HEADS = 8
SWA_KV_HEADS = 2
SWA_HEAD_DIM = 64
SWA_GROUP = SWA_HEADS // SWA_KV_HEADS
WINDOW = 128
NUM_BUCKETS = 32
MAX_DISTANCE = 128
N_EXPERTS = 32
TOP_K = 4
D_FF = 1024
SWIGLU_ALPHA = 1.702
SWIGLU_LIMIT = 7.0
RMS_EPS = 1e-6

GLA_QK = GLA_HEADS * GLA_DK
GLA_V = GLA_HEADS * GLA_DV
SWA_Q = SWA_HEADS * SWA_HEAD_DIM
SWA_KV = SWA_KV_HEADS * SWA_HEAD_DIM
LANES = 128
C_GQ, C_GK, C_GV, C_GR = 0, GLA_QK, 2 * GLA_QK, 2 * GLA_QK + GLA_V
C_SQ = C_GR + GLA_V
C_SK = C_SQ + SWA_Q
C_SV = C_SK + SWA_KV
C_GA = C_SV + SWA_KV
D_PROJ = C_GA + LANES

MIX_TM = 512
MIX_STREAM_TM = 512
MOE_TM = 512
DEC_SB = 16
VMEM_LIMIT = 56 * 1024 * 1024

F32 = jnp.float32
BF16 = jnp.bfloat16
NEG_INF = float("-inf")


def _dot(a, b):
    return jnp.dot(a, b, preferred_element_type=F32)


def _dot_nt(a, b):
    return lax.dot_general(a, b, (((1,), (1,)), ((), ())), preferred_element_type=F32)


def _split3(x):
    hi = x.astype(BF16)
    r1 = x - hi.astype(F32)
    mid = r1.astype(BF16)
    lo = (r1 - mid.astype(F32)).astype(BF16)
    return hi, mid, lo


def _rms(x, g):
    return x * lax.rsqrt(jnp.mean(x * x, axis=-1, keepdims=True) + RMS_EPS) * g


def _iota(shape, dim):
    return lax.broadcasted_iota(jnp.int32, shape, dim)


HALF_D = D_MODEL // 2
SLAB = HALF_D // LANES


def _pack_pair(lo, hi):
    bl = lax.bitcast_convert_type(lo.astype(BF16).astype(F32), jnp.uint32)
    bh = lax.bitcast_convert_type(hi.astype(BF16).astype(F32), jnp.uint32)
    return lax.bitcast_convert_type(bh | lax.shift_right_logical(bl, jnp.uint32(16)), jnp.int32)


def _unpack_pair(w):
    u = lax.bitcast_convert_type(w, jnp.uint32)
    lo = lax.bitcast_convert_type(lax.shift_left(u, jnp.uint32(16)), F32)
    hi = lax.bitcast_convert_type(u & jnp.uint32(0xFFFF0000), F32)
    return lo, hi


def _store_slabs(ref, x):
    rows = x.shape[0]
    for c in range(SLAB):
        sl = slice(c * LANES, (c + 1) * LANES)
        ref[pl.ds(c, rows, stride=SLAB), :] = _pack_pair(x[:, sl], x[:, HALF_D + c * LANES:HALF_D + (c + 1) * LANES])


def _load_slab_chunk(ref, rows, c):
    return _unpack_pair(ref[pl.ds(c, rows, stride=SLAB), :])


def _drain(steps):
    try:
        while True:
            next(steps)
    except StopIteration as done:
        return done.value


def _project_steps(x, g_mix, w_in_ref, w_a_up, b_a):
    h = _rms(x, g_mix).astype(BF16)
    yield

    def cols(lo, width, tile=2 * LANES):
        parts = []
        for off in range(0, width, tile):
            parts.append(_dot(h, w_in_ref[:, lo + off:lo + min(off + tile, width)]))
            yield
        return parts[0] if len(parts) == 1 else jnp.concatenate(parts, axis=1)

    ga = (yield from cols(C_GA, LANES)).astype(BF16)
    z = _dot(ga, w_a_up) + b_a
    log_a = -(jnp.maximum(-z, 0.0) + jnp.log(1.0 + jnp.exp(-jnp.abs(z)))) / GLA_GATE_TAU
    gqk = yield from cols(C_GQ, 2 * GLA_QK)
    gv = yield from cols(C_GV, GLA_V)
    swa = yield from cols(C_SQ, SWA_Q + 2 * SWA_KV)
    gr = yield from cols(C_GR, GLA_V)
    return dict(
        gq=gqk[:, :GLA_QK],
        gk=gqk[:, GLA_QK:],
        gv=gv,
        gr=gr,
        sq=swa[:, :SWA_Q],
        sk=swa[:, SWA_Q:SWA_Q + SWA_KV],
        sv=swa[:, SWA_Q + SWA_KV:],
        log_a=log_a,
    )


def _project(x, g_mix, w_in_ref, w_a_up, b_a):
    return _drain(_project_steps(x, g_mix, w_in_ref, w_a_up, b_a))


def _tail(*args):
    return _drain(_tail_steps(*args))


def _tail_steps(x, o_gla, gr, o_swa, g_gla_out, g_swa_out, w_out_ref, g_ffn, w_r, b_r, get_base):
    tm = x.shape[0]
    gate = gr * jax.nn.sigmoid(gr)
    parts = []
    for h in range(GLA_HEADS):
        sl = slice(h * GLA_DV, (h + 1) * GLA_DV)
        parts.append(_rms(o_gla[:, sl], g_gla_out) * gate[:, sl])
    og = jnp.concatenate(parts, axis=1).astype(BF16)
    yield
    os_ = _rms(o_swa, g_swa_out).astype(BF16)
    x_mid = x + _dot(og, w_out_ref[0:GLA_V])
    yield
    x_mid = x_mid + _dot(os_, w_out_ref[GLA_V:GLA_V + SWA_Q])
    yield
    hp = _rms(x_mid, g_ffn)

    h1 = hp.astype(BF16)
    h2 = (hp - h1.astype(F32)).astype(BF16)
    yield
    la = _dot_nt(w_r, h1)
    lb = _dot_nt(w_r[0:N_EXPERTS], h2)
    logits = la[0:N_EXPERTS] + la[N_EXPERTS:2 * N_EXPERTS] + lb + b_r
    yield

    eidx = _iota((N_EXPERTS, tm), 0)
    vals, idxs, onehots = [], [], []
    l = logits
    for _ in range(TOP_K):
        m = jnp.max(l, axis=0, keepdims=True)
        sel = jnp.min(jnp.where(l == m, eidx, N_EXPERTS), axis=0, keepdims=True)
        oh = eidx == sel
        l = jnp.where(oh, NEG_INF, l)
        vals.append(m)
        idxs.append(sel)
        onehots.append(oh)
    es = [jnp.exp(v - vals[0]) for v in vals]
    denom = es[0] + es[1] + es[2] + es[3]
    gates = [e / denom for e in es]

    ohf = jnp.concatenate([oh.astype(F32) for oh in onehots], axis=0)
    upper = (_iota((tm, tm), 0) < _iota((tm, tm), 1)).astype(BF16)
    prefix = _dot(ohf.astype(BF16), upper)
    yield
    base = get_base()
    ranks = []
    for k in range(TOP_K):
        sl = slice(k * N_EXPERTS, (k + 1) * N_EXPERTS)
        ohk = ohf[sl]
        base_t = jnp.concatenate([base] * (tm // LANES), axis=1)
        ranks.append(jnp.sum(ohk * (prefix[sl] + base_t), axis=0, keepdims=True))
        base = base + jnp.sum(ohk, axis=1, keepdims=True)
    zi = jnp.zeros((8 - TOP_K, tm), jnp.int32)
    zf = jnp.zeros((8 - TOP_K, tm), F32)
    topi = jnp.concatenate(idxs + [zi], axis=0)
    gate8 = jnp.concatenate(gates + [zf], axis=0)
    rank8 = jnp.concatenate([r.astype(jnp.int32) for r in ranks] + [zi], axis=0)
    return x_mid, hp, topi, gate8, rank8, base


def _gla_chunks(p, row0, s_blocks, n_lead_pad):
    tm = p["gq"].shape[0]
    nch = tm // GLA_CHUNK
    log_a = p["log_a"]
    if n_lead_pad:
        rows = row0 + _iota((tm, GLA_QK), 0)
        log_a = jnp.where(rows >= n_lead_pad, log_a, 0.0)
    ri, ci = _iota((tm, tm), 0), _iota((tm, tm), 1)
    tril = ((ri >= ci) & (ri // GLA_CHUNK == ci // GLA_CHUNK)).astype(BF16)
    hi, mid, lo = _split3(log_a)
    b_all = _dot(tril, hi) + _dot(tril, mid) + _dot(tril, lo)
    yield

    c64 = GLA_CHUNK
    kk_mask = (_iota((GLA_QK, GLA_QK), 0) // c64) == (_iota((GLA_QK, GLA_QK), 1) // GLA_DK)
    vv_mask = (_iota((GLA_QK, GLA_V), 0) // c64) == (_iota((GLA_QK, GLA_V), 1) // GLA_DV)
    zero_blk = jnp.zeros((GLA_DK, GLA_DV), BF16)
    causal = (_iota((c64, GLA_QK), 0) >= (_iota((c64, GLA_QK), 1) % c64)).astype(F32)
    zpad_k = jnp.zeros((LANES - c64, GLA_QK), F32)
    zpad_v = jnp.zeros((LANES - c64, GLA_V), BF16)

    outs = []
    for c in range(nch):
        rs = slice(c * c64, (c + 1) * c64)
        b = b_all[rs]
        q, k, v = p["gq"][rs], p["gk"][rs], p["gv"][rs]
        b_last = b[c64 - 1:c64]
        qt = (q * jnp.exp(b)).astype(BF16)
        kt = k * jnp.exp(-b)
        kd = k * jnp.exp(b_last - b)
        vb = v.astype(BF16)
        k_bd = jnp.where(kk_mask, jnp.concatenate([kt] * GLA_HEADS, axis=0), 0.0).astype(BF16)
        a = (_dot_nt(qt, k_bd) * causal).astype(BF16)
        v_bd = jnp.where(vv_mask, jnp.concatenate([vb] * GLA_HEADS, axis=0), jnp.zeros((), BF16))
        s_bd = jnp.concatenate(
            [jnp.concatenate([s_blocks[h].astype(BF16) if g == h else zero_blk for g in range(GLA_HEADS)], axis=1)
             for h in range(GLA_HEADS)], axis=0)
        outs.append(_dot(qt, s_bd) + _dot(a, v_bd))
        kd_t = jnp.transpose(jnp.concatenate([kd, zpad_k], axis=0)).astype(BF16)
        upd = _dot(kd_t, jnp.concatenate([vb, zpad_v], axis=0))
        decay = jnp.exp(jnp.transpose(jnp.broadcast_to(b_last, (LANES, GLA_QK))))
        s_blocks = [s_blocks[h] * decay[h * GLA_DK:(h + 1) * GLA_DK]
                    + upd[h * GLA_DK:(h + 1) * GLA_DK, h * GLA_DV:(h + 1) * GLA_DV] for h in range(GLA_HEADS)]
        yield
    return jnp.concatenate(outs, axis=0), s_blocks


def _swa_block(sq, kcat, vcat, bias_ref, sinks_ref, valid_t):
    half = _iota((1, LANES), 1) < SWA_HEAD_DIM
    top_rows = _iota((LANES, 1), 0) < SWA_HEAD_DIM
    k_roll = pltpu.roll(kcat, SWA_HEAD_DIM, 1)
    v_t = jnp.transpose(vcat)
    zeros_v = jnp.zeros((SWA_HEAD_DIM, 2 * WINDOW), F32)
    cols = []
    for kv in range(SWA_KV_HEADS):
        kk = jnp.where(half, kcat, k_roll) if kv == 0 else jnp.where(half, k_roll, kcat)
        q_parts = []
        for c in (2 * kv, 2 * kv + 1):
            qc = sq[:, c * LANES:(c + 1) * LANES]
            q_parts.append(jnp.where(half, qc, 0.0))
            q_parts.append(jnp.where(half, 0.0, qc))
        q_st = jnp.concatenate(q_parts, axis=0).astype(BF16)
        s = _dot_nt(kk.astype(BF16), q_st) + bias_ref[kv]
        if valid_t is not None:
            s = jnp.where(valid_t, s, NEG_INF)
        sink = jnp.concatenate(
            [jnp.full((1, WINDOW), sinks_ref[kv * SWA_GROUP + g], F32) for g in range(SWA_GROUP)], axis=1)
        m = jnp.maximum(jnp.max(s, axis=0, keepdims=True), sink)
        pr = jnp.exp(s - m)
        inv = 1.0 / (jnp.sum(pr, axis=0, keepdims=True) + jnp.exp(sink - m))
        pb = pr.astype(BF16)
        vk = v_t[kv * SWA_HEAD_DIM:(kv + 1) * SWA_HEAD_DIM]
        vv_t = jnp.concatenate([jnp.concatenate([vk, zeros_v], axis=1),
                                jnp.concatenate([zeros_v, vk], axis=1)], axis=0).astype(BF16)
        for pair in range(SWA_GROUP // 2):
            ce = slice(2 * pair * WINDOW, (2 * pair + 1) * WINDOW)
            co = slice((2 * pair + 1) * WINDOW, (2 * pair + 2) * WINDOW)
            p2_t = jnp.concatenate([pb[:, ce], pb[:, co]], axis=0)
            o2_t = _dot(vv_t, p2_t)
            o2_t = o2_t * jnp.where(top_rows, inv[:, ce], inv[:, co])
            cols.append(jnp.transpose(o2_t))
        yield
    return jnp.concatenate(cols, axis=1)


def _mixer_kernel(sinks_ref, x_ref, s0_ref, k0_ref, v0_ref, base0_ref, bias_ref,
                  g_mix_ref, w_in_ref, w_a_up_ref, b_a_ref, g_gla_ref, g_swa_ref, w_out_ref,
                  g_ffn_ref, w_r_ref, b_r_ref,
                  xmid_ref, hp_ref, topi_ref, gate_ref, rank_ref, sout_ref, kout_ref, vout_ref, cnt_ref,
                  s_scr, k_scr, v_scr, base_scr, *, n_lead_pad, prev_valid_from):
    g_id, j = pl.program_id(0), pl.program_id(1)
    n_streams, tm = x_ref.shape[1], x_ref.shape[2]

    @pl.when(j == 0)
    def _():
        for s in range(n_streams):
            s_scr[s] = s0_ref[...]
            k_scr[s] = k0_ref[...]
            v_scr[s] = v0_ref[...]

    @pl.when((j == 0) & (g_id == 0))
    def _():
        base_scr[...] = base0_ref[...]

    diag = [(slice(h * GLA_DK, (h + 1) * GLA_DK), slice(h * GLA_DV, (h + 1) * GLA_DV)) for h in range(GLA_HEADS)]

    def stream(s):
        x = x_ref[0, s]
        p = yield from _project_steps(x, g_mix_ref[...], w_in_ref, w_a_up_ref[...], b_a_ref[...])
        yield "mix"
        o_gla, s_blocks = yield from _gla_chunks(p, j * tm, [s_scr[s, r, c] for r, c in diag], n_lead_pad)
        for (r, c), blk in zip(diag, s_blocks):
            s_scr[s, r, c] = blk
        o_parts = []
        for sb in range(tm // WINDOW):
            rs = slice(sb * WINDOW, (sb + 1) * WINDOW)
            k_blk, v_blk = p["sk"][rs], p["sv"][rs]
            k_prev = k_scr[s] if sb == 0 else p["sk"][(sb - 1) * WINDOW:sb * WINDOW]
            v_prev = v_scr[s] if sb == 0 else p["sv"][(sb - 1) * WINDOW:sb * WINDOW]
            valid = None
            if sb == 0 and prev_valid_from:
                first = jnp.where(j == 0, prev_valid_from, 0)
                valid = _iota((2 * WINDOW, SWA_GROUP * WINDOW), 0) >= first
            o_parts.append((yield from _swa_block(
                p["sq"][rs], jnp.concatenate([k_prev, k_blk], axis=0),
                jnp.concatenate([v_prev, v_blk], axis=0), bias_ref, sinks_ref, valid)))
        o_swa = jnp.concatenate(o_parts, axis=0)
        k_scr[s] = p["sk"][tm - WINDOW:tm]
        v_scr[s] = p["sv"][tm - WINDOW:tm]
        yield "tail"
        x_mid, hp, topi, gate8, rank8, base = yield from _tail_steps(
            x, o_gla, p["gr"], o_swa, g_gla_ref[...], g_swa_ref[...], w_out_ref,
            g_ffn_ref[...], w_r_ref[...], b_r_ref[...], lambda: base_scr[...])
        base_scr[...] = base
        xmid_ref[0, s] = x_mid
        _store_slabs(hp_ref.at[0, s], hp)
        topi_ref[0, s] = topi
        gate_ref[0, s] = gate8
        rank_ref[0, s] = rank8
        sout_ref[0, s] = s_scr[s]
        kout_ref[0, s] = p["sk"][tm - WINDOW:tm]
        vout_ref[0, s] = p["sv"][tm - WINDOW:tm]
        cnt_ref[...] = base

    def advance(gen, stop):
        try:
            while next(gen) != stop or stop is None:
                pass
            return False
        except StopIteration:
            return True

    def alternate(gen_a, stop_a, gen_b, stop_b):
        done_a = done_b = False
        while not (done_a and done_b):
            if not done_a:
                try:
                    done_a = next(gen_a) == stop_a and stop_a is not None
                except StopIteration:
                    done_a = True
            if not done_b:
                try:
                    done_b = next(gen_b) == stop_b and stop_b is not None
                except StopIteration:
                    done_b = True

    if n_streams == 1:
        advance(stream(0), None)
    else:
        first, second = stream(0), stream(1)
        advance(first, "mix")
        alternate(first, "tail", second, "mix")
        alternate(first, None, second, "tail")
        advance(second, None)


def _full_spec(shape):
    nd = len(shape)
    return pl.BlockSpec(shape, lambda *_: (0,) * nd)


def _mixer_call(x, s0, k0, v0, base0, wts, tm, n_streams, n_lead_pad, prev_valid_from):
    B, L, _ = x.shape
    S = n_streams
    G = B // S
    nj = L // tm
    weight_args = (wts["bias"], wts["g_mix"], wts["w_in"], wts["w_a_up"], wts["b_a"], wts["g_gla"],
                   wts["g_swa"], wts["w_out"], wts["g_ffn"], wts["w_r"], wts["b_r"])
    in_specs = [
        pl.BlockSpec(memory_space=pltpu.SMEM),
        pl.BlockSpec((1, S, tm, D_MODEL), lambda g, j: (g, 0, j, 0)),
        _full_spec(s0.shape), _full_spec(k0.shape), _full_spec(v0.shape), _full_spec(base0.shape),
    ] + [_full_spec(w.shape) for w in weight_args]
    tok_spec = pl.BlockSpec((1, S, 8, tm), lambda g, j: (g, 0, 0, j))

    def per_seq_spec(rows, cols):
        return pl.BlockSpec((1, S, rows, cols), lambda g, j: (g, 0, 0, 0))

    out_specs = [
        pl.BlockSpec((1, S, tm, D_MODEL), lambda g, j: (g, 0, j, 0)),
        pl.BlockSpec((1, S, tm * SLAB, LANES), lambda g, j: (g, 0, j, 0)),
        tok_spec, tok_spec, tok_spec,
        per_seq_spec(GLA_QK, GLA_V), per_seq_spec(WINDOW, SWA_KV), per_seq_spec(WINDOW, SWA_KV),
        _full_spec((N_EXPERTS, LANES)),
    ]
    out_shape = [
        jax.ShapeDtypeStruct((G, S, L, D_MODEL), F32),
        jax.ShapeDtypeStruct((G, S, L * SLAB, LANES), jnp.int32),
        jax.ShapeDtypeStruct((G, S, 8, L), jnp.int32),
        jax.ShapeDtypeStruct((G, S, 8, L), F32),
        jax.ShapeDtypeStruct((G, S, 8, L), jnp.int32),
        jax.ShapeDtypeStruct((G, S, GLA_QK, GLA_V), F32),
        jax.ShapeDtypeStruct((G, S, WINDOW, SWA_KV), F32),
        jax.ShapeDtypeStruct((G, S, WINDOW, SWA_KV), F32),
        jax.ShapeDtypeStruct((N_EXPERTS, LANES), F32),
    ]
    kern = functools.partial(_mixer_kernel, n_lead_pad=n_lead_pad, prev_valid_from=prev_valid_from)
    xmid, hp, topi, gate, rank, s_out, k_out, v_out, cnt = pl.pallas_call(
        kern,
        grid=(G, nj),
        in_specs=in_specs,
        out_specs=out_specs,
        out_shape=out_shape,
        scratch_shapes=[pltpu.VMEM((S, GLA_QK, GLA_V), F32), pltpu.VMEM((S, WINDOW, SWA_KV), F32),
                        pltpu.VMEM((S, WINDOW, SWA_KV), F32), pltpu.VMEM((N_EXPERTS, LANES), F32)],
        compiler_params=pltpu.CompilerParams(dimension_semantics=("arbitrary", "arbitrary"),
                                             vmem_limit_bytes=VMEM_LIMIT),
        name="mixer",
    )(wts["sinks"], x.reshape(G, S, L, D_MODEL), s0, k0, v0, base0, *weight_args)

    def rows8(a):
        return jnp.transpose(a, (2, 0, 1, 3)).reshape(8, B * L)

    return (xmid.reshape(B, L, D_MODEL), hp.reshape(B * L * SLAB, LANES), rows8(topi), rows8(gate), rows8(rank),
            s_out.reshape(B, GLA_QK, GLA_V), k_out.reshape(B, WINDOW, SWA_KV), v_out.reshape(B, WINDOW, SWA_KV), cnt)


def _decode_kernel(sinks_ref, x_ref, st_ref, ck_ref, cv_ref, base0_ref, bias_ref,
                   g_mix_ref, w_in_ref, w_a_up_ref, b_a_ref, g_gla_ref, g_swa_ref, w_out_ref,
                   g_ffn_ref, w_r_ref, b_r_ref,
                   xmid_ref, hp_ref, topi_ref, gate_ref, rank_ref, sto_ref, cko_ref, cvo_ref, cnt_ref,
                   tq_scr, gv_scr, gr_scr, sq_scr, sk_scr, sv_scr, og_scr, os_scr):
    i = pl.program_id(0)
    n_seq = x_ref.shape[0]

    @pl.when(i == 0)
    def _():
        p = _project(x_ref[...], g_mix_ref[...], w_in_ref, w_a_up_ref[...], b_a_ref[...])
        a_hi, a_mid, a_lo = _split3(jnp.transpose(jnp.exp(p["log_a"])))
        tq_scr[...] = jnp.concatenate(
            [a_hi, a_mid, a_lo, jnp.transpose(p["gk"]).astype(BF16), jnp.transpose(p["gq"]).astype(BF16)], axis=0)
        gv_scr[...] = p["gv"]
        gr_scr[...] = p["gr"]
        sq_scr[...] = p["sq"]
        sk_scr[...] = p["sk"]
        sv_scr[...] = p["sv"]

    seq_row = _iota((n_seq, LANES), 0)
    half = _iota((1, LANES), 1) < SWA_HEAD_DIM
    row_id = _iota((WINDOW, SWA_KV), 0)
    head_diag = (_iota((16, SWA_Q), 1) // SWA_HEAD_DIM) == _iota((16, SWA_Q), 0)
    sink_col = jnp.concatenate(
        [jnp.full((1, 1), sinks_ref[h], F32) for h in range(SWA_HEADS)] + [jnp.zeros((8, 1), F32)], axis=0)

    def per_seq(sl, carry):
        s = i * DEC_SB + sl
        pick = (seq_row == s).astype(BF16)
        cols = _dot(tq_scr[...], pick)
        a_c = cols[0:GLA_QK] + cols[GLA_QK:2 * GLA_QK] + cols[2 * GLA_QK:3 * GLA_QK]
        k_c = cols[3 * GLA_QK:4 * GLA_QK]
        q_c = cols[4 * GLA_QK:5 * GLA_QK]
        st = st_ref[sl].reshape(GLA_QK, GLA_DV)
        v_row = gv_scr[pl.ds(s, 1), :]
        v_b = jnp.concatenate(
            [jnp.broadcast_to(v_row[:, h * GLA_DV:(h + 1) * GLA_DV], (GLA_DK, GLA_DV))
             for h in range(GLA_HEADS)], axis=0)
        st_new = a_c * st + k_c * v_b
        sto_ref[sl] = st_new.reshape(GLA_HEADS, GLA_DK, GLA_DV)
        t = q_c * st_new
        og_scr[pl.ds(s, 1), :] = jnp.concatenate(
            [jnp.sum(t[h * GLA_DK:(h + 1) * GLA_DK], axis=0, keepdims=True) for h in range(GLA_HEADS)],
            axis=1)

        k_new = sk_scr[pl.ds(s, 1), :]
        v_new = sv_scr[pl.ds(s, 1), :]
        kn = jnp.where(row_id == WINDOW - 1, k_new, pltpu.roll(ck_ref[sl], WINDOW - 1, 0))
        vn = jnp.where(row_id == WINDOW - 1, v_new, pltpu.roll(cv_ref[sl], WINDOW - 1, 0))
        cko_ref[sl] = kn
        cvo_ref[sl] = vn
        kr, vr = pltpu.roll(kn, SWA_HEAD_DIM, 1), pltpu.roll(vn, SWA_HEAD_DIM, 1)
        k0, k1 = jnp.where(half, kn, kr), jnp.where(half, kr, kn)
        v0, v1 = jnp.where(half, vn, vr), jnp.where(half, vr, vn)
        kw = jnp.concatenate([k0, k0, k1, k1], axis=1).astype(BF16)
        vw = jnp.concatenate([v0, v0, v1, v1], axis=1).astype(BF16)
        q_row = sq_scr[pl.ds(s, 1), :]
        qm = jnp.where(head_diag, jnp.broadcast_to(q_row, (16, SWA_Q)), 0.0).astype(BF16)
        sc = _dot_nt(qm, kw) + bias_ref[...]
        m = jnp.maximum(jnp.max(sc, axis=1, keepdims=True), sink_col)
        pr = jnp.exp(sc - m)
        inv = 1.0 / (jnp.sum(pr, axis=1, keepdims=True) + jnp.exp(sink_col - m))
        ow = _dot(pr.astype(BF16), vw) * inv
        os_scr[pl.ds(s, 1), :] = jnp.sum(jnp.where(head_diag, ow, 0.0), axis=0, keepdims=True)
        return carry

    lax.fori_loop(0, DEC_SB, per_seq, 0, unroll=8)

    @pl.when(i == pl.num_programs(0) - 1)
    def _():
        x_mid, hp, topi, gate8, rank8, base = _tail(
            x_ref[...], og_scr[...], gr_scr[...], os_scr[...], g_gla_ref[...], g_swa_ref[...],
            w_out_ref, g_ffn_ref[...], w_r_ref[...], b_r_ref[...], lambda: base0_ref[...])
        xmid_ref[...] = x_mid
        _store_slabs(hp_ref, hp)
        topi_ref[...] = topi
        gate_ref[...] = gate8
        rank_ref[...] = rank8
        cnt_ref[...] = base


def _decode_call(xs, state, ck, cv, base0, bias_dec, wts):
    n_seq = xs.shape[0]
    nb = n_seq // DEC_SB
    weight_args = (wts["g_mix"], wts["w_in"], wts["w_a_up"], wts["b_a"], wts["g_gla"],
                   wts["g_swa"], wts["w_out"], wts["g_ffn"], wts["w_r"], wts["b_r"])
    in_specs = [
        pl.BlockSpec(memory_space=pltpu.SMEM),
        _full_spec(xs.shape),
        pl.BlockSpec((DEC_SB, GLA_HEADS, GLA_DK, GLA_DV), lambda i: (i, 0, 0, 0)),
        pl.BlockSpec((DEC_SB, WINDOW, SWA_KV), lambda i: (i, 0, 0)),
        pl.BlockSpec((DEC_SB, WINDOW, SWA_KV), lambda i: (i, 0, 0)),
        _full_spec(base0.shape), _full_spec(bias_dec.shape),
    ] + [_full_spec(w.shape) for w in weight_args]
    out_specs = [
        _full_spec((n_seq, D_MODEL)),
        _full_spec((n_seq * SLAB, LANES)),
        _full_spec((8, n_seq)), _full_spec((8, n_seq)), _full_spec((8, n_seq)),
        pl.BlockSpec((DEC_SB, GLA_HEADS, GLA_DK, GLA_DV), lambda i: (i, 0, 0, 0)),
        pl.BlockSpec((DEC_SB, WINDOW, SWA_KV), lambda i: (i, 0, 0)),
        pl.BlockSpec((DEC_SB, WINDOW, SWA_KV), lambda i: (i, 0, 0)),
        _full_spec((N_EXPERTS, LANES)),
    ]
    out_shape = [
        jax.ShapeDtypeStruct((n_seq, D_MODEL), F32),
        jax.ShapeDtypeStruct((n_seq * SLAB, LANES), jnp.int32),
        jax.ShapeDtypeStruct((8, n_seq), jnp.int32),
        jax.ShapeDtypeStruct((8, n_seq), F32),
        jax.ShapeDtypeStruct((8, n_seq), jnp.int32),
        jax.ShapeDtypeStruct(state.shape, F32),
        jax.ShapeDtypeStruct(ck.shape, F32),
        jax.ShapeDtypeStruct(cv.shape, F32),
        jax.ShapeDtypeStruct((N_EXPERTS, LANES), F32),
    ]
    scratch = [pltpu.VMEM((5 * GLA_QK, n_seq), BF16)] + [
        pltpu.VMEM((n_seq, GLA_V), F32), pltpu.VMEM((n_seq, GLA_V), F32), pltpu.VMEM((n_seq, SWA_Q), F32),
        pltpu.VMEM((n_seq, SWA_KV), F32), pltpu.VMEM((n_seq, SWA_KV), F32),
        pltpu.VMEM((n_seq, GLA_V), F32), pltpu.VMEM((n_seq, SWA_Q), F32)]
    return pl.pallas_call(
        _decode_kernel,
        grid=(nb,),
        in_specs=in_specs,
        out_specs=out_specs,
        out_shape=out_shape,
        scratch_shapes=scratch,
        compiler_params=pltpu.CompilerParams(dimension_semantics=("arbitrary",),
                                             vmem_limit_bytes=VMEM_LIMIT),
        name="decode",
    )(wts["sinks"], xs, state, ck, cv, base0, bias_dec, *weight_args)


SC_CORES = 2
SC_SUBCORES = 16
SC_WORKERS = SC_CORES * SC_SUBCORES
SC_SCATTER_ROWS = 64
SC_GATHER_ROWS = 48


def _sc_mesh():
    return plsc.VectorSubcoreMesh(core_axis_name="c", subcore_axis_name="s")


def _sc_worker_id():
    return lax.axis_index("s") * SC_CORES + lax.axis_index("c")


def _sc_scatter_rows(src_p, src_s, idx_p, idx_s, n_out):
    rows = SC_SCATTER_ROWS
    n_chunks = idx_p.shape[0] // SC_WORKERS
    n_s, _, rows_s = idx_s.shape
    assert n_chunks * SC_WORKERS == idx_p.shape[0] and n_chunks % 2 == 0 and n_s <= SC_WORKERS

    @functools.partial(
        pl.kernel, mesh=_sc_mesh(),
        out_type=jax.ShapeDtypeStruct((n_out, SLAB, LANES), jnp.int32),
        scratch_types=[pltpu.VMEM((2, TOP_K, rows), jnp.int32), pltpu.VMEM((2, rows, SLAB, LANES), jnp.int32),
                       pltpu.VMEM((TOP_K, rows_s), jnp.int32), pltpu.VMEM((rows_s, SLAB, LANES), jnp.int32),
                       pltpu.SemaphoreType.DMA((2,)), pltpu.SemaphoreType.DMA((2,))])
    def scatter_rows(srcp_hbm, srcs_hbm, idxp_hbm, idxs_hbm, out_hbm, idx_v, rows_v, idxs_v, rowss_v, lsem, ssem):
        wid = _sc_worker_id()

        def loads(c, b):
            g = wid * n_chunks + c
            return (pltpu.make_async_copy(idxp_hbm.at[g], idx_v.at[b], lsem.at[b]),
                    pltpu.make_async_copy(srcp_hbm.at[pl.ds(pl.multiple_of(g * rows, 8), rows)], rows_v.at[b],
                                          lsem.at[b]))

        def scatters(b):
            return [pltpu.make_async_copy(rows_v.at[b], out_hbm.at[idx_v.at[b, k]], ssem.at[b])
                    for k in range(TOP_K)]

        for d in loads(0, 0):
            d.start()

        @pl.loop(0, n_chunks, step=2)
        def _(c0):
            for b in range(2):
                c = c0 + b
                for d in loads(c, b):
                    d.wait()

                @pl.when(c >= 1)
                def _():
                    for d in scatters(1 - b):
                        d.wait()

                @pl.when(c + 1 < n_chunks)
                def _():
                    for d in loads(c + 1, 1 - b):
                        d.start()

                for d in scatters(b):
                    d.start()

        for d in scatters((n_chunks - 1) % 2):
            d.wait()

        @pl.when(wid < n_s)
        def _():
            pltpu.sync_copy(idxs_hbm.at[wid], idxs_v)
            pltpu.sync_copy(srcs_hbm.at[pl.ds(pl.multiple_of(wid * rows_s, 8), rows_s)], rowss_v)
            for k in range(TOP_K):
                pltpu.sync_copy(rowss_v, out_hbm.at[idxs_v.at[k]])

    return scatter_rows(src_p, src_s, idx_p, idx_s)


def _sc_gather_rows(src3, idx2):
    rows = SC_GATHER_ROWS
    n_chunks = idx2.shape[0] // SC_WORKERS
    assert n_chunks * SC_WORKERS == idx2.shape[0] and idx2.shape[1] == rows and n_chunks % 2 == 0

    @functools.partial(
        pl.kernel, mesh=_sc_mesh(),
        out_type=jax.ShapeDtypeStruct((idx2.shape[0] * rows, SLAB, LANES), jnp.int32),
        scratch_types=[pltpu.VMEM((2, rows), jnp.int32), pltpu.VMEM((2, rows, SLAB, LANES), jnp.int32),
                       pltpu.SemaphoreType.DMA((2,)), pltpu.SemaphoreType.DMA((2,))])
    def gather_rows(src_hbm, idx_hbm, out_hbm, idx_v, rows_v, gsem, wsem):
        wid = _sc_worker_id()

        def gather(b):
            return pltpu.make_async_copy(src_hbm.at[idx_v.at[b]], rows_v.at[b], gsem.at[b])

        def write(c, b):
            base = pl.multiple_of((wid * n_chunks + c) * rows, 8)
            return pltpu.make_async_copy(rows_v.at[b], out_hbm.at[pl.ds(base, rows)], wsem.at[b])

        pltpu.sync_copy(idx_hbm.at[wid * n_chunks], idx_v.at[0])
        gather(0).start()

        @pl.loop(0, n_chunks, step=2)
        def _(c0):
            for b in range(2):
                c = c0 + b

                @pl.when(c + 1 < n_chunks)
                def _():
                    @pl.when(c >= 1)
                    def _():
                        write(c - 1, 1 - b).wait()
                    pltpu.sync_copy(idx_hbm.at[wid * n_chunks + c + 1], idx_v.at[1 - b])
                    gather(1 - b).start()

                gather(b).wait()
                write(c, b).start()

        write(n_chunks - 2, 0).wait()
        write(n_chunks - 1, 1).wait()

    return gather_rows(src3, idx2)


FF_TILE = 256


def _ffn_kernel(blk_e_ref, nused_ref, x_ref, wu_ref, bu_ref, wd_ref, bd_ref, y_ref, xbf, actbf, wu_bf, wd_bf):
    i = pl.program_id(0)
    tm = MOE_TM
    n_tiles = D_FF // FF_TILE

    @pl.when(i < nused_ref[0])
    def _():
        @pl.when((i == 0) | (blk_e_ref[i] != blk_e_ref[jnp.maximum(i - 1, 0)]))
        def _():
            wu_bf[...] = wu_ref[0].astype(BF16)
            wd_bf[...] = wd_ref[0].astype(BF16)

        for c in range(SLAB):
            lo, hi = _load_slab_chunk(x_ref, tm, c)
            xbf[:, c * LANES:(c + 1) * LANES] = lo.astype(BF16)
            xbf[:, HALF_D + c * LANES:HALF_D + (c + 1) * LANES] = hi.astype(BF16)
        for n in range(n_tiles):
            gc = slice(n * FF_TILE, (n + 1) * FF_TILE)
            lc = slice(D_FF + n * FF_TILE, D_FF + (n + 1) * FF_TILE)
            g = jnp.minimum(_dot(xbf[...], wu_bf[:, gc]) + bu_ref[0, :, gc], SWIGLU_LIMIT)
            lin = jnp.clip(_dot(xbf[...], wu_bf[:, lc]) + bu_ref[0, :, lc], -SWIGLU_LIMIT, SWIGLU_LIMIT)
            actbf[:, gc] = (g * jax.nn.sigmoid(SWIGLU_ALPHA * g) * (lin + 1.0)).astype(BF16)
        per_tile = FF_TILE // LANES
        for n in range(n_tiles // 2):
            yl = slice(n * FF_TILE, (n + 1) * FF_TILE)
            yh = slice(HALF_D + n * FF_TILE, HALF_D + (n + 1) * FF_TILE)
            y_lo = _dot(actbf[...], wd_bf[:, yl]) + bd_ref[0, :, yl]
            y_hi = _dot(actbf[...], wd_bf[:, yh]) + bd_ref[0, :, yh]
            for c in range(per_tile):
                sl = slice(c * LANES, (c + 1) * LANES)
                y_ref[pl.ds(n * per_tile + c, tm, stride=SLAB), :] = _pack_pair(y_lo[:, sl], y_hi[:, sl])

    @pl.when(i >= nused_ref[0])
    def _():
        y_ref[...] = jnp.zeros_like(y_ref)


def _ffn_call(blk_e, nused, xs2, w_up, b_up, w_down, b_down):
    n_blocks = blk_e.shape[0]
    tm = MOE_TM
    row_blk = pl.BlockSpec((tm * SLAB, LANES), lambda i, be, nu: (i, 0))
    grid_spec = pltpu.PrefetchScalarGridSpec(
        num_scalar_prefetch=2,
        grid=(n_blocks,),
        in_specs=[
            row_blk,
            pl.BlockSpec((1, D_MODEL, 2 * D_FF), lambda i, be, nu: (be[i], 0, 0)),
            pl.BlockSpec((1, 1, 2 * D_FF), lambda i, be, nu: (be[i], 0, 0)),
            pl.BlockSpec((1, D_FF, D_MODEL), lambda i, be, nu: (be[i], 0, 0)),
            pl.BlockSpec((1, 1, D_MODEL), lambda i, be, nu: (be[i], 0, 0)),
        ],
        out_specs=row_blk,
        scratch_shapes=[pltpu.VMEM((tm, D_MODEL), BF16), pltpu.VMEM((tm, D_FF), BF16),
                        pltpu.VMEM((D_MODEL, 2 * D_FF), BF16), pltpu.VMEM((D_FF, D_MODEL), BF16)],
    )
    return pl.pallas_call(
        _ffn_kernel,
        grid_spec=grid_spec,
        out_shape=jax.ShapeDtypeStruct((n_blocks * tm * SLAB, LANES), jnp.int32),
        compiler_params=pltpu.CompilerParams(dimension_semantics=("arbitrary",),
                                             vmem_limit_bytes=VMEM_LIMIT),
        name="experts",
    )(blk_e, nused, xs2, w_up, b_up.reshape(N_EXPERTS, 1, 2 * D_FF), w_down, b_down.reshape(N_EXPERTS, 1, D_MODEL))


def _combine_kernel(ys0_ref, ys1_ref, ys2_ref, ys3_ref, xmid_ref, gate_ref, g_final_ref, y_ref):
    tm = xmid_ref.shape[0]
    gts = jnp.transpose(jnp.concatenate([gate_ref[...], jnp.zeros((LANES - 8, tm), F32)], axis=0))
    lows, highs = [], []
    for c in range(SLAB):
        acc_lo = xmid_ref[:, c * LANES:(c + 1) * LANES]
        acc_hi = xmid_ref[:, HALF_D + c * LANES:HALF_D + (c + 1) * LANES]
        for k, ys_ref in enumerate((ys0_ref, ys1_ref, ys2_ref, ys3_ref)):
            lo, hi = _load_slab_chunk(ys_ref, tm, c)
            acc_lo = acc_lo + lo * gts[:, k:k + 1]
            acc_hi = acc_hi + hi * gts[:, k:k + 1]
        lows.append(acc_lo)
        highs.append(acc_hi)
    y_ref[...] = _rms(jnp.concatenate(lows + highs, axis=1), g_final_ref[...])


def _combine_call(ys4, t_stride, row0, x_mid, gates, g_final, tm):
    T = x_mid.shape[0]
    blk0 = row0 // tm
    per_k = t_stride // tm
    assert per_k * tm == t_stride and blk0 * tm == row0

    def ys_spec(k):
        return pl.BlockSpec((tm * SLAB, LANES), lambda i: (k * per_k + blk0 + i, 0))

    return pl.pallas_call(
        _combine_kernel,
        grid=(T // tm,),
        in_specs=[
            ys_spec(0), ys_spec(1), ys_spec(2), ys_spec(3),
            pl.BlockSpec((tm, D_MODEL), lambda i: (i, 0)),
            pl.BlockSpec((8, tm), lambda i: (0, i)),
            _full_spec((1, D_MODEL)),
        ],
        out_specs=pl.BlockSpec((tm, D_MODEL), lambda i: (i, 0)),
        out_shape=jax.ShapeDtypeStruct((T, D_MODEL), F32),
        compiler_params=pltpu.CompilerParams(dimension_semantics=("arbitrary",),
                                             vmem_limit_bytes=VMEM_LIMIT),
        name="combine",
    )(ys4, ys4, ys4, ys4, x_mid, gates, g_final)


def _t5_bucket(dist):
    n = jnp.maximum(dist, 0)
    max_exact = NUM_BUCKETS // 2
    nf = jnp.maximum(n, 1).astype(F32)
    large = max_exact + (jnp.log(nf / max_exact) / math.log(MAX_DISTANCE / max_exact)
                         * (NUM_BUCKETS - max_exact)).astype(jnp.int32)
    large = jnp.minimum(large, NUM_BUCKETS - 1)
    return jnp.where(n < max_exact, n, large)


def kernel(x_prompt, x_sample, state_gla, cache_swa_k, cache_swa_v, meta_tokens, rel_bias_table,
           g_mix, w_in, w_a_up, b_a, g_gla_out, g_swa_out, attn_sinks, w_out,
           g_ffn, w_router, b_router, w_up, b_up, w_down, b_down, g_final):
    assert g_mix.shape[0] == 1, "single-layer trunk"
    B, L, _ = x_prompt.shape
    n_seq = x_sample.shape[0]
    TP = B * L
    T_all = TP + n_seq

    wi = w_in[0]
    sizes = (GLA_QK, GLA_QK, GLA_V, GLA_V, GLA_LOWRANK, SWA_Q, SWA_KV, SWA_KV)
    offs = [0]
    for s in sizes:
        offs.append(offs[-1] + s)
    seg = [wi[:, offs[n]:offs[n + 1]] for n in range(8)]
    seg[0] = seg[0] * (GLA_DK ** -0.5)
    seg[5] = seg[5] * (SWA_HEAD_DIM ** -0.5)
    w_in_r = jnp.concatenate(
        seg[0:4] + seg[5:8] + [seg[4], jnp.zeros((D_MODEL, LANES - GLA_LOWRANK), F32)], axis=1).astype(BF16)
    w_a_pad = jnp.concatenate([w_a_up[0], jnp.zeros((LANES - GLA_LOWRANK, GLA_QK), F32)], axis=0).astype(BF16)
    wr_t = jnp.transpose(w_router[0])
    wr_hi = wr_t.astype(BF16)
    wr_lo = (wr_t - wr_hi.astype(F32)).astype(BF16)
    qi = jnp.arange(WINDOW)[:, None]
    kj = jnp.arange(2 * WINDOW)[None, :]
    buckets = jnp.arange(NUM_BUCKETS)
    table = rel_bias_table.astype(F32)
    oh_p = (_t5_bucket(qi - kj + WINDOW)[..., None] == buckets).astype(F32)
    bias_p = jnp.einsum("qkb,bh->hkq", oh_p, table, precision=lax.Precision.HIGHEST)
    in_window = jnp.transpose((kj > qi) & (kj <= qi + WINDOW))
    bias_p = jnp.where(in_window[None], bias_p, NEG_INF)
    bias_p = bias_p.reshape(SWA_KV_HEADS, SWA_GROUP, 2 * WINDOW, WINDOW).transpose(0, 2, 1, 3)
    bias_p = bias_p.reshape(SWA_KV_HEADS, 2 * WINDOW, SWA_GROUP * WINDOW)
    oh_d = (_t5_bucket(WINDOW - 1 - jnp.arange(WINDOW))[:, None] == buckets).astype(F32)
    bias_d = jnp.einsum("rb,bh->hr", oh_d, table, precision=lax.Precision.HIGHEST)
    bias_d = jnp.concatenate([bias_d, jnp.zeros((8, WINDOW), F32)], axis=0)
    wts = dict(
        sinks=attn_sinks[0].astype(F32), bias=bias_p,
        g_mix=g_mix[0][None], w_in=w_in_r, w_a_up=w_a_pad, b_a=b_a[0][None],
        g_gla=g_gla_out[0][None], g_swa=g_swa_out[0][None], w_out=w_out[0].astype(BF16),
        g_ffn=g_ffn[0][None], w_r=jnp.concatenate([wr_hi, wr_lo], axis=0), b_r=b_router[0][:, None],
    )

    x_pre = jnp.concatenate([jnp.zeros((WINDOW - N_META, D_MODEL), F32), meta_tokens.astype(F32)], axis=0)[None]
    zeros_s = jnp.zeros((GLA_QK, GLA_V), F32)
    zeros_kv = jnp.zeros((WINDOW, SWA_KV), F32)
    zeros_b = jnp.zeros((N_EXPERTS, LANES), F32)
    pre = _mixer_call(x_pre, zeros_s, zeros_kv, zeros_kv, zeros_b, wts, WINDOW, 1, WINDOW - N_META, 0)
    s_meta, k_meta, v_meta = pre[5][0], pre[6][0], pre[7][0]

    (xmid_p, hp_p, topi_p, gate_p, rank_p, s_p, k_p, v_p, cnt_p) = _mixer_call(
        x_prompt, s_meta, k_meta, v_meta, zeros_b, wts, MIX_STREAM_TM, 2, 0, WINDOW - N_META)

    (xmid_s, hp_s, topi_s, gate_s, rank_s, st_s, ck_s, cv_s, cnt_all) = _decode_call(
        x_sample[:, 0], state_gla[0], cache_swa_k[0].reshape(n_seq, WINDOW, SWA_KV),
        cache_swa_v[0].reshape(n_seq, WINDOW, SWA_KV), cnt_p, bias_d, wts)

    tm = MOE_TM
    n_slots = T_all * TOP_K
    n_blocks = -(-n_slots // tm) + N_EXPERTS
    top_e = jnp.concatenate([topi_p[:TOP_K], topi_s[:TOP_K]], axis=1)
    rank = jnp.concatenate([rank_p[:TOP_K], rank_s[:TOP_K]], axis=1)
    counts = cnt_all[:, 0].astype(jnp.int32)
    padded = (counts + tm - 1) // tm * tm
    pad_end = jnp.cumsum(padded)
    pad_start = pad_end - padded
    e_ids = jnp.arange(N_EXPERTS, dtype=jnp.int32)
    dest = jnp.sum(jnp.where(top_e[..., None] == e_ids, pad_start, 0), axis=-1) + rank
    n_pad = n_blocks * tm
    blk_e = jnp.minimum(jnp.sum(pad_end[None] <= (jnp.arange(n_blocks, dtype=jnp.int32) * tm)[:, None], axis=1),
                        N_EXPERTS - 1).astype(jnp.int32)
    nused = (pad_end[-1] // tm).astype(jnp.int32).reshape(1)

    sample_rows = 8
    idx_p = dest[:, :TP].reshape(TOP_K, TP // SC_SCATTER_ROWS, SC_SCATTER_ROWS).transpose(1, 0, 2)
    idx_s = dest[:, TP:].reshape(TOP_K, n_seq // sample_rows, sample_rows).transpose(1, 0, 2)
    xs3 = _sc_scatter_rows(hp_p.reshape(TP, SLAB, LANES), hp_s.reshape(n_seq, SLAB, LANES), idx_p, idx_s, n_pad)
    ys2 = _ffn_call(blk_e, nused, xs3.reshape(-1, LANES), w_up[0], b_up[0], w_down[0], b_down[0])
    unit = math.lcm(2 * SC_WORKERS * SC_GATHER_ROWS // TOP_K, MIX_TM)
    t_stride = -(-T_all // unit) * unit
    filler = jnp.arange(TOP_K * (t_stride - T_all), dtype=jnp.int32).reshape(TOP_K, t_stride - T_all)
    slot_src = jnp.concatenate([dest, filler], axis=1)
    slot_src = slot_src.reshape(TOP_K * t_stride // SC_GATHER_ROWS, SC_GATHER_ROWS)
    ys4 = _sc_gather_rows(ys2.reshape(-1, SLAB, LANES), slot_src).reshape(-1, LANES)

    gf = g_final[None]
    y_p = _combine_call(ys4, t_stride, 0, xmid_p.reshape(TP, D_MODEL), gate_p, gf, MIX_TM)
    y_s = _combine_call(ys4, t_stride, TP, xmid_s, gate_s, gf, n_seq)

    s_heads = jnp.stack([s_p[:, h * GLA_DK:(h + 1) * GLA_DK, h * GLA_DV:(h + 1) * GLA_DV]
                         for h in range(GLA_HEADS)], axis=1)
    return (y_p.reshape(B, L, D_MODEL), y_s.reshape(n_seq, 1, D_MODEL), s_heads[None],
            k_p.reshape(1, B, WINDOW, SWA_KV_HEADS, SWA_HEAD_DIM),
            v_p.reshape(1, B, WINDOW, SWA_KV_HEADS, SWA_HEAD_DIM),
            st_s[None], ck_s.reshape(1, n_seq, WINDOW, SWA_KV_HEADS, SWA_HEAD_DIM),
            cv_s.reshape(1, n_seq, WINDOW, SWA_KV_HEADS, SWA_HEAD_DIM))
```

```python
import functools
import math

import jax
import jax.numpy as jnp
from jax import lax
from jax.experimental import pallas as pl
from jax.experimental.pallas import tpu as pltpu
from jax.experimental.pallas import tpu_sc as plsc

D_MODEL = 1024
N_META = 16
GLA_HEADS = 4
GLA_DK = 64
GLA_DV = 128
GLA_LOWRANK = 16
GLA_GATE_TAU = 16.0
GLA_CHUNK = 64
SWA_HEADS = 8
SWA_KV_HEADS = 2
SWA_HEAD_DIM = 64
SWA_GROUP = SWA_HEADS // SWA_KV_HEADS
WINDOW = 128
NUM_BUCKETS = 32
MAX_DISTANCE = 128
N_EXPERTS = 32
TOP_K = 4
D_FF = 1024
SWIGLU_ALPHA = 1.702
SWIGLU_LIMIT = 7.0
RMS_EPS = 1e-6

GLA_QK = GLA_HEADS * GLA_DK
GLA_V = GLA_HEADS * GLA_DV
SWA_Q = SWA_HEADS * SWA_HEAD_DIM
SWA_KV = SWA_KV_HEADS * SWA_HEAD_DIM
LANES = 128
C_GQ, C_GK, C_GV, C_GR = 0, GLA_QK, 2 * GLA_QK, 2 * GLA_QK + GLA_V
C_SQ = C_GR + GLA_V
C_SK = C_SQ + SWA_Q
C_SV = C_SK + SWA_KV
C_GA = C_SV + SWA_KV
D_PROJ = C_GA + LANES

MIX_TM = 512
MIX_STREAM_TM = 512
MOE_TM = 512
DEC_SB = 16
VMEM_LIMIT = 56 * 1024 * 1024

F32 = jnp.float32
BF16 = jnp.bfloat16
NEG_INF = float("-inf")


def _dot(a, b):
    return jnp.dot(a, b, preferred_element_type=F32)


def _dot_nt(a, b):
    return lax.dot_general(a, b, (((1,), (1,)), ((), ())), preferred_element_type=F32)


def _split3(x):
    hi = x.astype(BF16)
    r1 = x - hi.astype(F32)
    mid = r1.astype(BF16)
    lo = (r1 - mid.astype(F32)).astype(BF16)
    return hi, mid, lo


def _rms(x, g):
    return x * lax.rsqrt(jnp.mean(x * x, axis=-1, keepdims=True) + RMS_EPS) * g


def _iota(shape, dim):
    return lax.broadcasted_iota(jnp.int32, shape, dim)


HALF_D = D_MODEL // 2
SLAB = HALF_D // LANES


def _pack_pair(lo, hi):
    bl = lax.bitcast_convert_type(lo.astype(BF16).astype(F32), jnp.uint32)
    bh = lax.bitcast_convert_type(hi.astype(BF16).astype(F32), jnp.uint32)
    return lax.bitcast_convert_type(bh | lax.shift_right_logical(bl, jnp.uint32(16)), jnp.int32)


def _unpack_pair(w):
    u = lax.bitcast_convert_type(w, jnp.uint32)
    lo = lax.bitcast_convert_type(lax.shift_left(u, jnp.uint32(16)), F32)
    hi = lax.bitcast_convert_type(u & jnp.uint32(0xFFFF0000), F32)
    return lo, hi


def _store_slabs(ref, x):
    rows = x.shape[0]
    for c in range(SLAB):
        sl = slice(c * LANES, (c + 1) * LANES)
        ref[pl.ds(c, rows, stride=SLAB), :] = _pack_pair(x[:, sl], x[:, HALF_D + c * LANES:HALF_D + (c + 1) * LANES])


def _load_slab_chunk(ref, rows, c):
    return _unpack_pair(ref[pl.ds(c, rows, stride=SLAB), :])


def _drain(steps):
    try:
        while True:
            next(steps)
    except StopIteration as done:
        return done.value


def _project_steps(x, g_mix, w_in_ref, w_a_up, b_a):
    h = _rms(x, g_mix).astype(BF16)
    yield

    def cols(lo, width, tile=2 * LANES):
        parts = []
        for off in range(0, width, tile):
            parts.append(_dot(h, w_in_ref[:, lo + off:lo + min(off + tile, width)]))
            yield
        return parts[0] if len(parts) == 1 else jnp.concatenate(parts, axis=1)

    ga = (yield from cols(C_GA, LANES)).astype(BF16)
    z = _dot(ga, w_a_up) + b_a
    log_a = -(jnp.maximum(-z, 0.0) + jnp.log(1.0 + jnp.exp(-jnp.abs(z)))) / GLA_GATE_TAU
    gqk = yield from cols(C_GQ, 2 * GLA_QK)
    gv = yield from cols(C_GV, GLA_V)
    swa = yield from cols(C_SQ, SWA_Q + 2 * SWA_KV)
    gr = yield from cols(C_GR, GLA_V)
    return dict(
        gq=gqk[:, :GLA_QK],
        gk=gqk[:, GLA_QK:],
        gv=gv,
        gr=gr,
        sq=swa[:, :SWA_Q],
        sk=swa[:, SWA_Q:SWA_Q + SWA_KV],
        sv=swa[:, SWA_Q + SWA_KV:],
        log_a=log_a,
    )


def _project(x, g_mix, w_in_ref, w_a_up, b_a):
    return _drain(_project_steps(x, g_mix, w_in_ref, w_a_up, b_a))


def _tail(*args):
    return _drain(_tail_steps(*args))


def _tail_steps(x, o_gla, gr, o_swa, g_gla_out, g_swa_out, w_out_ref, g_ffn, w_r, b_r, get_base):
    tm = x.shape[0]
    gate = gr * jax.nn.sigmoid(gr)
    parts = []
    for h in range(GLA_HEADS):
        sl = slice(h * GLA_DV, (h + 1) * GLA_DV)
        parts.append(_rms(o_gla[:, sl], g_gla_out) * gate[:, sl])
    og = jnp.concatenate(parts, axis=1).astype(BF16)
    yield
    os_ = _rms(o_swa, g_swa_out).astype(BF16)
    x_mid = x + _dot(og, w_out_ref[0:GLA_V])
    yield
    x_mid = x_mid + _dot(os_, w_out_ref[GLA_V:GLA_V + SWA_Q])
    yield
    hp = _rms(x_mid, g_ffn)

    h1 = hp.astype(BF16)
    h2 = (hp - h1.astype(F32)).astype(BF16)
    yield
    la = _dot_nt(w_r, h1)
    lb = _dot_nt(w_r[0:N_EXPERTS], h2)
    logits = la[0:N_EXPERTS] + la[N_EXPERTS:2 * N_EXPERTS] + lb + b_r
    yield

    eidx = _iota((N_EXPERTS, tm), 0)
    vals, idxs, onehots = [], [], []
    l = logits
    for _ in range(TOP_K):
        m = jnp.max(l, axis=0, keepdims=True)
        sel = jnp.min(jnp.where(l == m, eidx, N_EXPERTS), axis=0, keepdims=True)
        oh = eidx == sel
        l = jnp.where(oh, NEG_INF, l)
        vals.append(m)
        idxs.append(sel)
        onehots.append(oh)
    es = [jnp.exp(v - vals[0]) for v in vals]
    denom = es[0] + es[1] + es[2] + es[3]
    gates = [e / denom for e in es]

    ohf = jnp.concatenate([oh.astype(F32) for oh in onehots], axis=0)
    upper = (_iota((tm, tm), 0) < _iota((tm, tm), 1)).astype(BF16)
    prefix = _dot(ohf.astype(BF16), upper)
    yield
    base = get_base()
    ranks = []
    for k in range(TOP_K):
        sl = slice(k * N_EXPERTS, (k + 1) * N_EXPERTS)
        ohk = ohf[sl]
        base_t = jnp.concatenate([base] * (tm // LANES), axis=1)
        ranks.append(jnp.sum(ohk * (prefix[sl] + base_t), axis=0, keepdims=True))
        base = base + jnp.sum(ohk, axis=1, keepdims=True)
    zi = jnp.zeros((8 - TOP_K, tm), jnp.int32)
    zf = jnp.zeros((8 - TOP_K, tm), F32)
    topi = jnp.concatenate(idxs + [zi], axis=0)
    gate8 = jnp.concatenate(gates + [zf], axis=0)
    rank8 = jnp.concatenate([r.astype(jnp.int32) for r in ranks] + [zi], axis=0)
    return x_mid, hp, topi, gate8, rank8, base


def _gla_chunks(p, row0, s_blocks, n_lead_pad):
    tm = p["gq"].shape[0]
    nch = tm // GLA_CHUNK
    log_a = p["log_a"]
    if n_lead_pad:
        rows = row0 + _iota((tm, GLA_QK), 0)
        log_a = jnp.where(rows >= n_lead_pad, log_a, 0.0)
    ri, ci = _iota((tm, tm), 0), _iota((tm, tm), 1)
    tril = ((ri >= ci) & (ri // GLA_CHUNK == ci // GLA_CHUNK)).astype(BF16)
    hi, mid, lo = _split3(log_a)
    b_all = _dot(tril, hi) + _dot(tril, mid) + _dot(tril, lo)
    yield

    c64 = GLA_CHUNK
    kk_mask = (_iota((GLA_QK, GLA_QK), 0) // c64) == (_iota((GLA_QK, GLA_QK), 1) // GLA_DK)
    vv_mask = (_iota((GLA_QK, GLA_V), 0) // c64) == (_iota((GLA_QK, GLA_V), 1) // GLA_DV)
    zero_blk = jnp.zeros((GLA_DK, GLA_DV), BF16)
    causal = (_iota((c64, GLA_QK), 0) >= (_iota((c64, GLA_QK), 1) % c64)).astype(F32)
    zpad_k = jnp.zeros((LANES - c64, GLA_QK), F32)
    zpad_v = jnp.zeros((LANES - c64, GLA_V), BF16)

    outs = []
    for c in range(nch):
        rs = slice(c * c64, (c + 1) * c64)
        b = b_all[rs]
        q, k, v = p["gq"][rs], p["gk"][rs], p["gv"][rs]
        b_last = b[c64 - 1:c64]
        qt = (q * jnp.exp(b)).astype(BF16)
        kt = k * jnp.exp(-b)
        kd = k * jnp.exp(b_last - b)
        vb = v.astype(BF16)
        k_bd = jnp.where(kk_mask, jnp.concatenate([kt] * GLA_HEADS, axis=0), 0.0).astype(BF16)
        a = (_dot_nt(qt, k_bd) * causal).astype(BF16)
        v_bd = jnp.where(vv_mask, jnp.concatenate([vb] * GLA_HEADS, axis=0), jnp.zeros((), BF16))
        s_bd = jnp.concatenate(
            [jnp.concatenate([s_blocks[h].astype(BF16) if g == h else zero_blk for g in range(GLA_HEADS)], axis=1)
             for h in range(GLA_HEADS)], axis=0)
        outs.append(_dot(qt, s_bd) + _dot(a, v_bd))
        kd_t = jnp.transpose(jnp.concatenate([kd, zpad_k], axis=0)).astype(BF16)
        upd = _dot(kd_t, jnp.concatenate([vb, zpad_v], axis=0))
        decay = jnp.exp(jnp.transpose(jnp.broadcast_to(b_last, (LANES, GLA_QK))))
        s_blocks = [s_blocks[h] * decay[h * GLA_DK:(h + 1) * GLA_DK]
                    + upd[h * GLA_DK:(h + 1) * GLA_DK, h * GLA_DV:(h + 1) * GLA_DV] for h in range(GLA_HEADS)]
        yield
    return jnp.concatenate(outs, axis=0), s_blocks


def _swa_block(sq, kcat, vcat, bias_ref, sinks_ref, valid_t):
    half = _iota((1, LANES), 1) < SWA_HEAD_DIM
    top_rows = _iota((LANES, 1), 0) < SWA_HEAD_DIM
    k_roll = pltpu.roll(kcat, SWA_HEAD_DIM, 1)
    v_t = jnp.transpose(vcat)
    zeros_v = jnp.zeros((SWA_HEAD_DIM, 2 * WINDOW), F32)
    cols = []
    for kv in range(SWA_KV_HEADS):
        kk = jnp.where(half, kcat, k_roll) if kv == 0 else jnp.where(half, k_roll, kcat)
        q_parts = []
        for c in (2 * kv, 2 * kv + 1):
            qc = sq[:, c * LANES:(c + 1) * LANES]
            q_parts.append(jnp.where(half, qc, 0.0))
            q_parts.append(jnp.where(half, 0.0, qc))
        q_st = jnp.concatenate(q_parts, axis=0).astype(BF16)
        s = _dot_nt(kk.astype(BF16), q_st) + bias_ref[kv]
        if valid_t is not None:
            s = jnp.where(valid_t, s, NEG_INF)
        sink = jnp.concatenate(
            [jnp.full((1, WINDOW), sinks_ref[kv * SWA_GROUP + g], F32) for g in range(SWA_GROUP)], axis=1)
        m = jnp.maximum(jnp.max(s, axis=0, keepdims=True), sink)
        pr = jnp.exp(s - m)
        inv = 1.0 / (jnp.sum(pr, axis=0, keepdims=True) + jnp.exp(sink - m))
        pb = pr.astype(BF16)
        vk = v_t[kv * SWA_HEAD_DIM:(kv + 1) * SWA_HEAD_DIM]
        vv_t = jnp.concatenate([jnp.concatenate([vk, zeros_v], axis=1),
                                jnp.concatenate([zeros_v, vk], axis=1)], axis=0).astype(BF16)
        for pair in range(SWA_GROUP // 2):
            ce = slice(2 * pair * WINDOW, (2 * pair + 1) * WINDOW)
            co = slice((2 * pair + 1) * WINDOW, (2 * pair + 2) * WINDOW)
            p2_t = jnp.concatenate([pb[:, ce], pb[:, co]], axis=0)
            o2_t = _dot(vv_t, p2_t)
            o2_t = o2_t * jnp.where(top_rows, inv[:, ce], inv[:, co])
            cols.append(jnp.transpose(o2_t))
        yield
    return jnp.concatenate(cols, axis=1)


def _mixer_kernel(sinks_ref, x_ref, s0_ref, k0_ref, v0_ref, base0_ref, bias_ref,
                  g_mix_ref, w_in_ref, w_a_up_ref, b_a_ref, g_gla_ref, g_swa_ref, w_out_ref,
                  g_ffn_ref, w_r_ref, b_r_ref,
                  xmid_ref, hp_ref, topi_ref, gate_ref, rank_ref, sout_ref, kout_ref, vout_ref, cnt_ref,
                  s_scr, k_scr, v_scr, base_scr, *, n_lead_pad, prev_valid_from):
    g_id, j = pl.program_id(0), pl.program_id(1)
    n_streams, tm = x_ref.shape[1], x_ref.shape[2]

    @pl.when(j == 0)
    def _():
        for s in range(n_streams):
            s_scr[s] = s0_ref[...]
            k_scr[s] = k0_ref[...]
            v_scr[s] = v0_ref[...]

    @pl.when((j == 0) & (g_id == 0))
    def _():
        base_scr[...] = base0_ref[...]

    diag = [(slice(h * GLA_DK, (h + 1) * GLA_DK), slice(h * GLA_DV, (h + 1) * GLA_DV)) for h in range(GLA_HEADS)]

    def stream(s):
        x = x_ref[0, s]
        p = yield from _project_steps(x, g_mix_ref[...], w_in_ref, w_a_up_ref[...], b_a_ref[...])
        yield "mix"
        o_gla, s_blocks = yield from _gla_chunks(p, j * tm, [s_scr[s, r, c] for r, c in diag], n_lead_pad)
        for (r, c), blk in zip(diag, s_blocks):
            s_scr[s, r, c] = blk
        o_parts = []
        for sb in range(tm // WINDOW):
            rs = slice(sb * WINDOW, (sb + 1) * WINDOW)
            k_blk, v_blk = p["sk"][rs], p["sv"][rs]
            k_prev = k_scr[s] if sb == 0 else p["sk"][(sb - 1) * WINDOW:sb * WINDOW]
            v_prev = v_scr[s] if sb == 0 else p["sv"][(sb - 1) * WINDOW:sb * WINDOW]
            valid = None
            if sb == 0 and prev_valid_from:
                first = jnp.where(j == 0, prev_valid_from, 0)
                valid = _iota((2 * WINDOW, SWA_GROUP * WINDOW), 0) >= first
            o_parts.append((yield from _swa_block(
                p["sq"][rs], jnp.concatenate([k_prev, k_blk], axis=0),
                jnp.concatenate([v_prev, v_blk], axis=0), bias_ref, sinks_ref, valid)))
        o_swa = jnp.concatenate(o_parts, axis=0)
        k_scr[s] = p["sk"][tm - WINDOW:tm]
        v_scr[s] = p["sv"][tm - WINDOW:tm]
        yield "tail"
        x_mid, hp, topi, gate8, rank8, base = yield from _tail_steps(
            x, o_gla, p["gr"], o_swa, g_gla_ref[...], g_swa_ref[...], w_out_ref,
            g_ffn_ref[...], w_r_ref[...], b_r_ref[...], lambda: base_scr[...])
        base_scr[...] = base
        xmid_ref[0, s] = x_mid
        _store_slabs(hp_ref.at[0, s], hp)
        topi_ref[0, s] = topi
        gate_ref[0, s] = gate8
        rank_ref[0, s] = rank8
        sout_ref[0, s] = s_scr[s]
        kout_ref[0, s] = p["sk"][tm - WINDOW:tm]
        vout_ref[0, s] = p["sv"][tm - WINDOW:tm]
        cnt_ref[...] = base

    def advance(gen, stop):
        try:
            while next(gen) != stop or stop is None:
                pass
            return False
        except StopIteration:
            return True

    def alternate(gen_a, stop_a, gen_b, stop_b):
        done_a = done_b = False
        while not (done_a and done_b):
            if not done_a:
                try:
                    done_a = next(gen_a) == stop_a and stop_a is not None
                except StopIteration:
                    done_a = True
            if not done_b:
                try:
                    done_b = next(gen_b) == stop_b and stop_b is not None
                except StopIteration:
                    done_b = True

    if n_streams == 1:
        advance(stream(0), None)
    else:
        first, second = stream(0), stream(1)
        advance(first, "mix")
        alternate(first, "tail", second, "mix")
        alternate(first, None, second, "tail")
        advance(second, None)


def _full_spec(shape):
    nd = len(shape)
    return pl.BlockSpec(shape, lambda *_: (0,) * nd)


def _mixer_call(x, s0, k0, v0, base0, wts, tm, n_streams, n_lead_pad, prev_valid_from):
    B, L, _ = x.shape
    S = n_streams
    G = B // S
    nj = L // tm
    weight_args = (wts["bias"], wts["g_mix"], wts["w_in"], wts["w_a_up"], wts["b_a"], wts["g_gla"],
                   wts["g_swa"], wts["w_out"], wts["g_ffn"], wts["w_r"], wts["b_r"])
    in_specs = [
        pl.BlockSpec(memory_space=pltpu.SMEM),
        pl.BlockSpec((1, S, tm, D_MODEL), lambda g, j: (g, 0, j, 0)),
        _full_spec(s0.shape), _full_spec(k0.shape), _full_spec(v0.shape), _full_spec(base0.shape),
    ] + [_full_spec(w.shape) for w in weight_args]
    tok_spec = pl.BlockSpec((1, S, 8, tm), lambda g, j: (g, 0, 0, j))

    def per_seq_spec(rows, cols):
        return pl.BlockSpec((1, S, rows, cols), lambda g, j: (g, 0, 0, 0))

    out_specs = [
        pl.BlockSpec((1, S, tm, D_MODEL), lambda g, j: (g, 0, j, 0)),
        pl.BlockSpec((1, S, tm * SLAB, LANES), lambda g, j: (g, 0, j, 0)),
        tok_spec, tok_spec, tok_spec,
        per_seq_spec(GLA_QK, GLA_V), per_seq_spec(WINDOW, SWA_KV), per_seq_spec(WINDOW, SWA_KV),
        _full_spec((N_EXPERTS, LANES)),
    ]
    out_shape = [
        jax.ShapeDtypeStruct((G, S, L, D_MODEL), F32),
        jax.ShapeDtypeStruct((G, S, L * SLAB, LANES), jnp.int32),
        jax.ShapeDtypeStruct((G, S, 8, L), jnp.int32),
        jax.ShapeDtypeStruct((G, S, 8, L), F32),
        jax.ShapeDtypeStruct((G, S, 8, L), jnp.int32),
        jax.ShapeDtypeStruct((G, S, GLA_QK, GLA_V), F32),
        jax.ShapeDtypeStruct((G, S, WINDOW, SWA_KV), F32),
        jax.ShapeDtypeStruct((G, S, WINDOW, SWA_KV), F32),
        jax.ShapeDtypeStruct((N_EXPERTS, LANES), F32),
    ]
    kern = functools.partial(_mixer_kernel, n_lead_pad=n_lead_pad, prev_valid_from=prev_valid_from)
    xmid, hp, topi, gate, rank, s_out, k_out, v_out, cnt = pl.pallas_call(
        kern,
        grid=(G, nj),
        in_specs=in_specs,
        out_specs=out_specs,
        out_shape=out_shape,
        scratch_shapes=[pltpu.VMEM((S, GLA_QK, GLA_V), F32), pltpu.VMEM((S, WINDOW, SWA_KV), F32),
                        pltpu.VMEM((S, WINDOW, SWA_KV), F32), pltpu.VMEM((N_EXPERTS, LANES), F32)],
        compiler_params=pltpu.CompilerParams(dimension_semantics=("arbitrary", "arbitrary"),
                                             vmem_limit_bytes=VMEM_LIMIT),
        name="mixer",
    )(wts["sinks"], x.reshape(G, S, L, D_MODEL), s0, k0, v0, base0, *weight_args)

    def rows8(a):
        return jnp.transpose(a, (2, 0, 1, 3)).reshape(8, B * L)

    return (xmid.reshape(B, L, D_MODEL), hp.reshape(B * L * SLAB, LANES), rows8(topi), rows8(gate), rows8(rank),
            s_out.reshape(B, GLA_QK, GLA_V), k_out.reshape(B, WINDOW, SWA_KV), v_out.reshape(B, WINDOW, SWA_KV), cnt)


def _decode_kernel(sinks_ref, x_ref, st_ref, ck_ref, cv_ref, base0_ref, bias_ref,
                   g_mix_ref, w_in_ref, w_a_up_ref, b_a_ref, g_gla_ref, g_swa_ref, w_out_ref,
                   g_ffn_ref, w_r_ref, b_r_ref,
                   xmid_ref, hp_ref, topi_ref, gate_ref, rank_ref, sto_ref, cko_ref, cvo_ref, cnt_ref,
                   tq_scr, gv_scr, gr_scr, sq_scr, sk_scr, sv_scr, og_scr, os_scr):
    i = pl.program_id(0)
    n_seq = x_ref.shape[0]

    @pl.when(i == 0)
    def _():
        p = _project(x_ref[...], g_mix_ref[...], w_in_ref, w_a_up_ref[...], b_a_ref[...])
        a_hi, a_mid, a_lo = _split3(jnp.transpose(jnp.exp(p["log_a"])))
        tq_scr[...] = jnp.concatenate(
            [a_hi, a_mid, a_lo, jnp.transpose(p["gk"]).astype(BF16), jnp.transpose(p["gq"]).astype(BF16)], axis=0)
        gv_scr[...] = p["gv"]
        gr_scr[...] = p["gr"]
        sq_scr[...] = p["sq"]
        sk_scr[...] = p["sk"]
        sv_scr[...] = p["sv"]

    seq_row = _iota((n_seq, LANES), 0)
    half = _iota((1, LANES), 1) < SWA_HEAD_DIM
    row_id = _iota((WINDOW, SWA_KV), 0)
    head_diag = (_iota((16, SWA_Q), 1) // SWA_HEAD_DIM) == _iota((16, SWA_Q), 0)
    sink_col = jnp.concatenate(
        [jnp.full((1, 1), sinks_ref[h], F32) for h in range(SWA_HEADS)] + [jnp.zeros((8, 1), F32)], axis=0)

    def per_seq(sl, carry):
        s = i * DEC_SB + sl
        pick = (seq_row == s).astype(BF16)
        cols = _dot(tq_scr[...], pick)
        a_c = cols[0:GLA_QK] + cols[GLA_QK:2 * GLA_QK] + cols[2 * GLA_QK:3 * GLA_QK]
        k_c = cols[3 * GLA_QK:4 * GLA_QK]
        q_c = cols[4 * GLA_QK:5 * GLA_QK]
        st = st_ref[sl].reshape(GLA_QK, GLA_DV)
        v_row = gv_scr[pl.ds(s, 1), :]
        v_b = jnp.concatenate(
            [jnp.broadcast_to(v_row[:, h * GLA_DV:(h + 1) * GLA_DV], (GLA_DK, GLA_DV))
             for h in range(GLA_HEADS)], axis=0)
        st_new = a_c * st + k_c * v_b
        sto_ref[sl] = st_new.reshape(GLA_HEADS, GLA_DK, GLA_DV)
        t = q_c * st_new
        og_scr[pl.ds(s, 1), :] = jnp.concatenate(
            [jnp.sum(t[h * GLA_DK:(h + 1) * GLA_DK], axis=0, keepdims=True) for h in range(GLA_HEADS)],
            axis=1)

        k_new = sk_scr[pl.ds(s, 1), :]
        v_new = sv_scr[pl.ds(s, 1), :]
        kn = jnp.where(row_id == WINDOW - 1, k_new, pltpu.roll(ck_ref[sl], WINDOW - 1, 0))
        vn = jnp.where(row_id == WINDOW - 1, v_new, pltpu.roll(cv_ref[sl], WINDOW - 1, 0))
        cko_ref[sl] = kn
        cvo_ref[sl] = vn
        kr, vr = pltpu.roll(kn, SWA_HEAD_DIM, 1), pltpu.roll(vn, SWA_HEAD_DIM, 1)
        k0, k1 = jnp.where(half, kn, kr), jnp.where(half, kr, kn)
        v0, v1 = jnp.where(half, vn, vr), jnp.where(half, vr, vn)
        kw = jnp.concatenate([k0, k0, k1, k1], axis=1).astype(BF16)
        vw = jnp.concatenate([v0, v0, v1, v1], axis=1).astype(BF16)
        q_row = sq_scr[pl.ds(s, 1), :]
        qm = jnp.where(head_diag, jnp.broadcast_to(q_row, (16, SWA_Q)), 0.0).astype(BF16)
        sc = _dot_nt(qm, kw) + bias_ref[...]
        m = jnp.maximum(jnp.max(sc, axis=1, keepdims=True), sink_col)
        pr = jnp.exp(sc - m)
        inv = 1.0 / (jnp.sum(pr, axis=1, keepdims=True) + jnp.exp(sink_col - m))
        ow = _dot(pr.astype(BF16), vw) * inv
        os_scr[pl.ds(s, 1), :] = jnp.sum(jnp.where(head_diag, ow, 0.0), axis=0, keepdims=True)
        return carry

    lax.fori_loop(0, DEC_SB, per_seq, 0, unroll=8)

    @pl.when(i == pl.num_programs(0) - 1)
    def _():
        x_mid, hp, topi, gate8, rank8, base = _tail(
            x_ref[...], og_scr[...], gr_scr[...], os_scr[...], g_gla_ref[...], g_swa_ref[...],
            w_out_ref, g_ffn_ref[...], w_r_ref[...], b_r_ref[...], lambda: base0_ref[...])
        xmid_ref[...] = x_mid
        _store_slabs(hp_ref, hp)
        topi_ref[...] = topi
        gate_ref[...] = gate8
        rank_ref[...] = rank8
        cnt_ref[...] = base


def _decode_call(xs, state, ck, cv, base0, bias_dec, wts):
    n_seq = xs.shape[0]
    nb = n_seq // DEC_SB
    weight_args = (wts["g_mix"], wts["w_in"], wts["w_a_up"], wts["b_a"], wts["g_gla"],
                   wts["g_swa"], wts["w_out"], wts["g_ffn"], wts["w_r"], wts["b_r"])
    in_specs = [
        pl.BlockSpec(memory_space=pltpu.SMEM),
        _full_spec(xs.shape),
        pl.BlockSpec((DEC_SB, GLA_HEADS, GLA_DK, GLA_DV), lambda i: (i, 0, 0, 0)),
        pl.BlockSpec((DEC_SB, WINDOW, SWA_KV), lambda i: (i, 0, 0)),
        pl.BlockSpec((DEC_SB, WINDOW, SWA_KV), lambda i: (i, 0, 0)),
        _full_spec(base0.shape), _full_spec(bias_dec.shape),
    ] + [_full_spec(w.shape) for w in weight_args]
    out_specs = [
        _full_spec((n_seq, D_MODEL)),
        _full_spec((n_seq * SLAB, LANES)),
        _full_spec((8, n_seq)), _full_spec((8, n_seq)), _full_spec((8, n_seq)),
        pl.BlockSpec((DEC_SB, GLA_HEADS, GLA_DK, GLA_DV), lambda i: (i, 0, 0, 0)),
        pl.BlockSpec((DEC_SB, WINDOW, SWA_KV), lambda i: (i, 0, 0)),
        pl.BlockSpec((DEC_SB, WINDOW, SWA_KV), lambda i: (i, 0, 0)),
        _full_spec((N_EXPERTS, LANES)),
    ]
    out_shape = [
        jax.ShapeDtypeStruct((n_seq, D_MODEL), F32),
        jax.ShapeDtypeStruct((n_seq * SLAB, LANES), jnp.int32),
        jax.ShapeDtypeStruct((8, n_seq), jnp.int32),
        jax.ShapeDtypeStruct((8, n_seq), F32),
        jax.ShapeDtypeStruct((8, n_seq), jnp.int32),
        jax.ShapeDtypeStruct(state.shape, F32),
        jax.ShapeDtypeStruct(ck.shape, F32),
        jax.ShapeDtypeStruct(cv.shape, F32),
        jax.ShapeDtypeStruct((N_EXPERTS, LANES), F32),
    ]
    scratch = [pltpu.VMEM((5 * GLA_QK, n_seq), BF16)] + [
        pltpu.VMEM((n_seq, GLA_V), F32), pltpu.VMEM((n_seq, GLA_V), F32), pltpu.VMEM((n_seq, SWA_Q), F32),
        pltpu.VMEM((n_seq, SWA_KV), F32), pltpu.VMEM((n_seq, SWA_KV), F32),
        pltpu.VMEM((n_seq, GLA_V), F32), pltpu.VMEM((n_seq, SWA_Q), F32)]
    return pl.pallas_call(
        _decode_kernel,
        grid=(nb,),
        in_specs=in_specs,
        out_specs=out_specs,
        out_shape=out_shape,
        scratch_shapes=scratch,
        compiler_params=pltpu.CompilerParams(dimension_semantics=("arbitrary",),
                                             vmem_limit_bytes=VMEM_LIMIT),
        name="decode",
    )(wts["sinks"], xs, state, ck, cv, base0, bias_dec, *weight_args)


SC_CORES = 2
SC_SUBCORES = 16
SC_WORKERS = SC_CORES * SC_SUBCORES
SC_SCATTER_ROWS = 64
SC_GATHER_ROWS = 48


def _sc_mesh():
    return plsc.VectorSubcoreMesh(core_axis_name="c", subcore_axis_name="s")


def _sc_worker_id():
    return lax.axis_index("s") * SC_CORES + lax.axis_index("c")


def _sc_scatter_rows(src_p, src_s, idx_p, idx_s, n_out):
    rows = SC_SCATTER_ROWS
    n_chunks = idx_p.shape[0] // SC_WORKERS
    n_s, _, rows_s = idx_s.shape
    assert n_chunks * SC_WORKERS == idx_p.shape[0] and n_chunks % 2 == 0 and n_s <= SC_WORKERS

    @functools.partial(
        pl.kernel, mesh=_sc_mesh(),
        out_type=jax.ShapeDtypeStruct((n_out, SLAB, LANES), jnp.int32),
        scratch_types=[pltpu.VMEM((2, TOP_K, rows), jnp.int32), pltpu.VMEM((2, rows, SLAB, LANES), jnp.int32),
                       pltpu.VMEM((TOP_K, rows_s), jnp.int32), pltpu.VMEM((rows_s, SLAB, LANES), jnp.int32),
                       pltpu.SemaphoreType.DMA((2,)), pltpu.SemaphoreType.DMA((2,))])
    def scatter_rows(srcp_hbm, srcs_hbm, idxp_hbm, idxs_hbm, out_hbm, idx_v, rows_v, idxs_v, rowss_v, lsem, ssem):
        wid = _sc_worker_id()

        def loads(c, b):
            g = wid * n_chunks + c
            return (pltpu.make_async_copy(idxp_hbm.at[g], idx_v.at[b], lsem.at[b]),
                    pltpu.make_async_copy(srcp_hbm.at[pl.ds(pl.multiple_of(g * rows, 8), rows)], rows_v.at[b],
                                          lsem.at[b]))

        def scatters(b):
            return [pltpu.make_async_copy(rows_v.at[b], out_hbm.at[idx_v.at[b, k]], ssem.at[b])
                    for k in range(TOP_K)]

        for d in loads(0, 0):
            d.start()

        @pl.loop(0, n_chunks, step=2)
        def _(c0):
            for b in range(2):
                c = c0 + b
                for d in loads(c, b):
                    d.wait()

                @pl.when(c >= 1)
                def _():
                    for d in scatters(1 - b):
                        d.wait()

                @pl.when(c + 1 < n_chunks)
                def _():
                    for d in loads(c + 1, 1 - b):
                        d.start()

                for d in scatters(b):
                    d.start()

        for d in scatters((n_chunks - 1) % 2):
            d.wait()

        @pl.when(wid < n_s)
        def _():
            pltpu.sync_copy(idxs_hbm.at[wid], idxs_v)
            pltpu.sync_copy(srcs_hbm.at[pl.ds(pl.multiple_of(wid * rows_s, 8), rows_s)], rowss_v)
            for k in range(TOP_K):
                pltpu.sync_copy(rowss_v, out_hbm.at[idxs_v.at[k]])

    return scatter_rows(src_p, src_s, idx_p, idx_s)


def _sc_gather_rows(src3, idx2):
    rows = SC_GATHER_ROWS
    n_chunks = idx2.shape[0] // SC_WORKERS
    assert n_chunks * SC_WORKERS == idx2.shape[0] and idx2.shape[1] == rows and n_chunks % 2 == 0

    @functools.partial(
        pl.kernel, mesh=_sc_mesh(),
        out_type=jax.ShapeDtypeStruct((idx2.shape[0] * rows, SLAB, LANES), jnp.int32),
        scratch_types=[pltpu.VMEM((2, rows), jnp.int32), pltpu.VMEM((2, rows, SLAB, LANES), jnp.int32),
                       pltpu.SemaphoreType.DMA((2,)), pltpu.SemaphoreType.DMA((2,))])
    def gather_rows(src_hbm, idx_hbm, out_hbm, idx_v, rows_v, gsem, wsem):
        wid = _sc_worker_id()

        def gather(b):
            return pltpu.make_async_copy(src_hbm.at[idx_v.at[b]], rows_v.at[b], gsem.at[b])

        def write(c, b):
            base = pl.multiple_of((wid * n_chunks + c) * rows, 8)
            return pltpu.make_async_copy(rows_v.at[b], out_hbm.at[pl.ds(base, rows)], wsem.at[b])

        pltpu.sync_copy(idx_hbm.at[wid * n_chunks], idx_v.at[0])
        gather(0).start()

        @pl.loop(0, n_chunks, step=2)
        def _(c0):
            for b in range(2):
                c = c0 + b

                @pl.when(c + 1 < n_chunks)
                def _():
                    @pl.when(c >= 1)
                    def _():
                        write(c - 1, 1 - b).wait()
                    pltpu.sync_copy(idx_hbm.at[wid * n_chunks + c + 1], idx_v.at[1 - b])
                    gather(1 - b).start()

                gather(b).wait()
                write(c, b).start()

        write(n_chunks - 2, 0).wait()
        write(n_chunks - 1, 1).wait()

    return gather_rows(src3, idx2)


FF_TILE = 256


FFN_ROW_STEP = 128


def _ffn_kernel(blk_e_ref, nused_ref, rows_ref, x_ref, wu_ref, bu_ref, wd_ref, bd_ref, y_ref,
                xbf, actbf, wu_bf, wd_bf):
    i = pl.program_id(0)
    tm = MOE_TM
    n_tiles = D_FF // FF_TILE
    per_tile = FF_TILE // LANES

    def ffn_rows(rows):
        for c in range(SLAB):
            lo, hi = _load_slab_chunk(x_ref, rows, c)
            xbf[0:rows, c * LANES:(c + 1) * LANES] = lo.astype(BF16)
            xbf[0:rows, HALF_D + c * LANES:HALF_D + (c + 1) * LANES] = hi.astype(BF16)
        x = xbf[0:rows, :]
        for n in range(n_tiles):
            gc = slice(n * FF_TILE, (n + 1) * FF_TILE)
            lc = slice(D_FF + n * FF_TILE, D_FF + (n + 1) * FF_TILE)
            g = jnp.minimum(_dot(x, wu_bf[:, gc]) + bu_ref[0, :, gc], SWIGLU_LIMIT)
            lin = jnp.clip(_dot(x, wu_bf[:, lc]) + bu_ref[0, :, lc], -SWIGLU_LIMIT, SWIGLU_LIMIT)
            actbf[0:rows, gc] = (g * jax.nn.sigmoid(SWIGLU_ALPHA * g) * (lin + 1.0)).astype(BF16)
        act = actbf[0:rows, :]
        for n in range(n_tiles // 2):
            yl = slice(n * FF_TILE, (n + 1) * FF_TILE)
            yh = slice(HALF_D + n * FF_TILE, HALF_D + (n + 1) * FF_TILE)
            y_lo = _dot(act, wd_bf[:, yl]) + bd_ref[0, :, yl]
            y_hi = _dot(act, wd_bf[:, yh]) + bd_ref[0, :, yh]
            for c in range(per_tile):
                sl = slice(c * LANES, (c + 1) * LANES)
                y_ref[pl.ds(n * per_tile + c, rows, stride=SLAB), :] = _pack_pair(y_lo[:, sl], y_hi[:, sl])
        if rows < tm:
            y_ref[rows * SLAB:tm * SLAB, :] = jnp.zeros(((tm - rows) * SLAB, LANES), jnp.int32)

    @pl.when(i < nused_ref[0])
    def _():
        @pl.when((i == 0) | (blk_e_ref[i] != blk_e_ref[jnp.maximum(i - 1, 0)]))
        def _():
            wu_bf[...] = wu_ref[0].astype(BF16)
            wd_bf[...] = wd_ref[0].astype(BF16)

        for rows in range(FFN_ROW_STEP, tm + 1, FFN_ROW_STEP):
            @pl.when(rows_ref[i] == rows)
            def _():
                ffn_rows(rows)

    @pl.when(i >= nused_ref[0])
    def _():
        y_ref[...] = jnp.zeros_like(y_ref)


def _ffn_call(blk_e, nused, blk_rows, xs2, w_up, b_up, w_down, b_down):
    n_blocks = blk_e.shape[0]
    tm = MOE_TM
    row_blk = pl.BlockSpec((tm * SLAB, LANES), lambda i, be, nu, nr: (i, 0))
    grid_spec = pltpu.PrefetchScalarGridSpec(
        num_scalar_prefetch=3,
        grid=(n_blocks,),
        in_specs=[
            row_blk,
            pl.BlockSpec((1, D_MODEL, 2 * D_FF), lambda i, be, nu, nr: (be[i], 0, 0)),
            pl.BlockSpec((1, 1, 2 * D_FF), lambda i, be, nu, nr: (be[i], 0, 0)),
            pl.BlockSpec((1, D_FF, D_MODEL), lambda i, be, nu, nr: (be[i], 0, 0)),
            pl.BlockSpec((1, 1, D_MODEL), lambda i, be, nu, nr: (be[i], 0, 0)),
        ],
        out_specs=row_blk,
        scratch_shapes=[pltpu.VMEM((tm, D_MODEL), BF16), pltpu.VMEM((tm, D_FF), BF16),
                        pltpu.VMEM((D_MODEL, 2 * D_FF), BF16), pltpu.VMEM((D_FF, D_MODEL), BF16)],
    )
    return pl.pallas_call(
        _ffn_kernel,
        grid_spec=grid_spec,
        out_shape=jax.ShapeDtypeStruct((n_blocks * tm * SLAB, LANES), jnp.int32),
        compiler_params=pltpu.CompilerParams(dimension_semantics=("arbitrary",),
                                             vmem_limit_bytes=VMEM_LIMIT),
        name="experts",
    )(blk_e, nused, blk_rows, xs2, w_up, b_up.reshape(N_EXPERTS, 1, 2 * D_FF), w_down,
      b_down.reshape(N_EXPERTS, 1, D_MODEL))


def _combine_kernel(ys0_ref, ys1_ref, ys2_ref, ys3_ref, xmid_ref, gate_ref, g_final_ref, y_ref):
    tm = xmid_ref.shape[0]
    gts = jnp.transpose(jnp.concatenate([gate_ref[...], jnp.zeros((LANES - 8, tm), F32)], axis=0))
    lows, highs = [], []
    for c in range(SLAB):
        acc_lo = xmid_ref[:, c * LANES:(c + 1) * LANES]
        acc_hi = xmid_ref[:, HALF_D + c * LANES:HALF_D + (c + 1) * LANES]
        for k, ys_ref in enumerate((ys0_ref, ys1_ref, ys2_ref, ys3_ref)):
            lo, hi = _load_slab_chunk(ys_ref, tm, c)
            acc_lo = acc_lo + lo * gts[:, k:k + 1]
            acc_hi = acc_hi + hi * gts[:, k:k + 1]
        lows.append(acc_lo)
        highs.append(acc_hi)
    y_ref[...] = _rms(jnp.concatenate(lows + highs, axis=1), g_final_ref[...])


def _combine_call(ys4, t_stride, row0, x_mid, gates, g_final, tm):
    T = x_mid.shape[0]
    blk0 = row0 // tm
    per_k = t_stride // tm
    assert per_k * tm == t_stride and blk0 * tm == row0

    def ys_spec(k):
        return pl.BlockSpec((tm * SLAB, LANES), lambda i: (k * per_k + blk0 + i, 0))

    return pl.pallas_call(
        _combine_kernel,
        grid=(T // tm,),
        in_specs=[
            ys_spec(0), ys_spec(1), ys_spec(2), ys_spec(3),
            pl.BlockSpec((tm, D_MODEL), lambda i: (i, 0)),
            pl.BlockSpec((8, tm), lambda i: (0, i)),
            _full_spec((1, D_MODEL)),
        ],
        out_specs=pl.BlockSpec((tm, D_MODEL), lambda i: (i, 0)),
        out_shape=jax.ShapeDtypeStruct((T, D_MODEL), F32),
        compiler_params=pltpu.CompilerParams(dimension_semantics=("arbitrary",),
                                             vmem_limit_bytes=VMEM_LIMIT),
        name="combine",
    )(ys4, ys4, ys4, ys4, x_mid, gates, g_final)


def _t5_bucket(dist):
    n = jnp.maximum(dist, 0)
    max_exact = NUM_BUCKETS // 2
    nf = jnp.maximum(n, 1).astype(F32)
    large = max_exact + (jnp.log(nf / max_exact) / math.log(MAX_DISTANCE / max_exact)
                         * (NUM_BUCKETS - max_exact)).astype(jnp.int32)
    large = jnp.minimum(large, NUM_BUCKETS - 1)
    return jnp.where(n < max_exact, n, large)


def kernel(x_prompt, x_sample, state_gla, cache_swa_k, cache_swa_v, meta_tokens, rel_bias_table,
           g_mix, w_in, w_a_up, b_a, g_gla_out, g_swa_out, attn_sinks, w_out,
           g_ffn, w_router, b_router, w_up, b_up, w_down, b_down, g_final):
    assert g_mix.shape[0] == 1, "single-layer trunk"
    B, L, _ = x_prompt.shape
    n_seq = x_sample.shape[0]
    TP = B * L
    T_all = TP + n_seq

    wi = w_in[0]
    sizes = (GLA_QK, GLA_QK, GLA_V, GLA_V, GLA_LOWRANK, SWA_Q, SWA_KV, SWA_KV)
    offs = [0]
    for s in sizes:
        offs.append(offs[-1] + s)
    seg = [wi[:, offs[n]:offs[n + 1]] for n in range(8)]
    seg[0] = seg[0] * (GLA_DK ** -0.5)
    seg[5] = seg[5] * (SWA_HEAD_DIM ** -0.5)
    w_in_r = jnp.concatenate(
        seg[0:4] + seg[5:8] + [seg[4], jnp.zeros((D_MODEL, LANES - GLA_LOWRANK), F32)], axis=1).astype(BF16)
    w_a_pad = jnp.concatenate([w_a_up[0], jnp.zeros((LANES - GLA_LOWRANK, GLA_QK), F32)], axis=0).astype(BF16)
    wr_t = jnp.transpose(w_router[0])
    wr_hi = wr_t.astype(BF16)
    wr_lo = (wr_t - wr_hi.astype(F32)).astype(BF16)
    qi = jnp.arange(WINDOW)[:, None]
    kj = jnp.arange(2 * WINDOW)[None, :]
    buckets = jnp.arange(NUM_BUCKETS)
    table = rel_bias_table.astype(F32)
    oh_p = (_t5_bucket(qi - kj + WINDOW)[..., None] == buckets).astype(F32)
    bias_p = jnp.einsum("qkb,bh->hkq", oh_p, table, precision=lax.Precision.HIGHEST)
    in_window = jnp.transpose((kj > qi) & (kj <= qi + WINDOW))
    bias_p = jnp.where(in_window[None], bias_p, NEG_INF)
    bias_p = bias_p.reshape(SWA_KV_HEADS, SWA_GROUP, 2 * WINDOW, WINDOW).transpose(0, 2, 1, 3)
    bias_p = bias_p.reshape(SWA_KV_HEADS, 2 * WINDOW, SWA_GROUP * WINDOW)
    oh_d = (_t5_bucket(WINDOW - 1 - jnp.arange(WINDOW))[:, None] == buckets).astype(F32)
    bias_d = jnp.einsum("rb,bh->hr", oh_d, table, precision=lax.Precision.HIGHEST)
    bias_d = jnp.concatenate([bias_d, jnp.zeros((8, WINDOW), F32)], axis=0)
    wts = dict(
        sinks=attn_sinks[0].astype(F32), bias=bias_p,
        g_mix=g_mix[0][None], w_in=w_in_r, w_a_up=w_a_pad, b_a=b_a[0][None],
        g_gla=g_gla_out[0][None], g_swa=g_swa_out[0][None], w_out=w_out[0].astype(BF16),
        g_ffn=g_ffn[0][None], w_r=jnp.concatenate([wr_hi, wr_lo], axis=0), b_r=b_router[0][:, None],
    )

    x_pre = jnp.concatenate([jnp.zeros((WINDOW - N_META, D_MODEL), F32), meta_tokens.astype(F32)], axis=0)[None]
    zeros_s = jnp.zeros((GLA_QK, GLA_V), F32)
    zeros_kv = jnp.zeros((WINDOW, SWA_KV), F32)
    zeros_b = jnp.zeros((N_EXPERTS, LANES), F32)
    pre = _mixer_call(x_pre, zeros_s, zeros_kv, zeros_kv, zeros_b, wts, WINDOW, 1, WINDOW - N_META, 0)
    s_meta, k_meta, v_meta = pre[5][0], pre[6][0], pre[7][0]

    (xmid_p, hp_p, topi_p, gate_p, rank_p, s_p, k_p, v_p, cnt_p) = _mixer_call(
        x_prompt, s_meta, k_meta, v_meta, zeros_b, wts, MIX_STREAM_TM, 2, 0, WINDOW - N_META)

    (xmid_s, hp_s, topi_s, gate_s, rank_s, st_s, ck_s, cv_s, cnt_all) = _decode_call(
        x_sample[:, 0], state_gla[0], cache_swa_k[0].reshape(n_seq, WINDOW, SWA_KV),
        cache_swa_v[0].reshape(n_seq, WINDOW, SWA_KV), cnt_p, bias_d, wts)

    tm = MOE_TM
    n_slots = T_all * TOP_K
    n_blocks = -(-n_slots // tm) + N_EXPERTS
    top_e = jnp.concatenate([topi_p[:TOP_K], topi_s[:TOP_K]], axis=1)
    rank = jnp.concatenate([rank_p[:TOP_K], rank_s[:TOP_K]], axis=1)
    counts = cnt_all[:, 0].astype(jnp.int32)
    padded = (counts + tm - 1) // tm * tm
    pad_end = jnp.cumsum(padded)
    pad_start = pad_end - padded
    e_ids = jnp.arange(N_EXPERTS, dtype=jnp.int32)
    dest = jnp.sum(jnp.where(top_e[..., None] == e_ids, pad_start, 0), axis=-1) + rank
    n_pad = n_blocks * tm
    blk_e = jnp.minimum(jnp.sum(pad_end[None] <= (jnp.arange(n_blocks, dtype=jnp.int32) * tm)[:, None], axis=1),
                        N_EXPERTS - 1).astype(jnp.int32)
    nused = (pad_end[-1] // tm).astype(jnp.int32).reshape(1)
    blk_first = jnp.arange(n_blocks, dtype=jnp.int32) * tm
    e_onehot = blk_e[:, None] == e_ids[None]
    rows_left = jnp.sum(jnp.where(e_onehot, (pad_start + counts)[None], 0), axis=1) - blk_first
    blk_rows = -(-jnp.clip(rows_left, 1, tm) // FFN_ROW_STEP) * FFN_ROW_STEP

    sample_rows = 8
    idx_p = dest[:, :TP].reshape(TOP_K, TP // SC_SCATTER_ROWS, SC_SCATTER_ROWS).transpose(1, 0, 2)
    idx_s = dest[:, TP:].reshape(TOP_K, n_seq // sample_rows, sample_rows).transpose(1, 0, 2)
    xs3 = _sc_scatter_rows(hp_p.reshape(TP, SLAB, LANES), hp_s.reshape(n_seq, SLAB, LANES), idx_p, idx_s, n_pad)
    ys2 = _ffn_call(blk_e, nused, blk_rows.astype(jnp.int32), xs3.reshape(-1, LANES),
                    w_up[0], b_up[0], w_down[0], b_down[0])
    unit = math.lcm(2 * SC_WORKERS * SC_GATHER_ROWS // TOP_K, MIX_TM)
    t_stride = -(-T_all // unit) * unit
    filler = jnp.arange(TOP_K * (t_stride - T_all), dtype=jnp.int32).reshape(TOP_K, t_stride - T_all)
    slot_src = jnp.concatenate([dest, filler], axis=1)
    slot_src = slot_src.reshape(TOP_K * t_stride // SC_GATHER_ROWS, SC_GATHER_ROWS)
    ys4 = _sc_gather_rows(ys2.reshape(-1, SLAB, LANES), slot_src).reshape(-1, LANES)

    gf = g_final[None]
    y_p = _combine_call(ys4, t_stride, 0, xmid_p.reshape(TP, D_MODEL), gate_p, gf, MIX_TM)
    y_s = _combine_call(ys4, t_stride, TP, xmid_s, gate_s, gf, n_seq)

    s_heads = jnp.stack([s_p[:, h * GLA_DK:(h + 1) * GLA_DK, h * GLA_DV:(h + 1) * GLA_DV]
                         for h in range(GLA_HEADS)], axis=1)
    return (y_p.reshape(B, L, D_MODEL), y_s.reshape(n_seq, 1, D_MODEL), s_heads[None],
            k_p.reshape(1, B, WINDOW, SWA_KV_HEADS, SWA_HEAD_DIM),
            v_p.reshape(1, B, WINDOW, SWA_KV_HEADS, SWA_HEAD_DIM),
            st_s[None], ck_s.reshape(1, n_seq, WINDOW, SWA_KV_HEADS, SWA_HEAD_DIM),
            cv_s.reshape(1, n_seq, WINDOW, SWA_KV_HEADS, SWA_HEAD_DIM))
```

```python
import functools
import math

import jax
import jax.numpy as jnp
from jax import lax
from jax.experimental import pallas as pl
from jax.experimental.pallas import tpu as pltpu
from jax.experimental.pallas import tpu_sc as plsc

D_MODEL = 1024
N_META = 16
GLA_HEADS = 4
GLA_DK = 64
GLA_DV = 128
GLA_LOWRANK = 16
GLA_GATE_TAU = 16.0
GLA_CHUNK = 64
SWA_HEADS = 8
SWA_KV_HEADS = 2
SWA_HEAD_DIM = 64
SWA_GROUP = SWA_HEADS // SWA_KV_HEADS
WINDOW = 128
NUM_BUCKETS = 32
MAX_DISTANCE = 128
N_EXPERTS = 32
TOP_K = 4
D_FF = 1024
SWIGLU_ALPHA = 1.702
SWIGLU_LIMIT = 7.0
RMS_EPS = 1e-6

GLA_QK = GLA_HEADS * GLA_DK
GLA_V = GLA_HEADS * GLA_DV
SWA_Q = SWA_HEADS * SWA_HEAD_DIM
SWA_KV = SWA_KV_HEADS * SWA_HEAD_DIM
LANES = 128
C_GQ, C_GK, C_GV, C_GR = 0, GLA_QK, 2 * GLA_QK, 2 * GLA_QK + GLA_V
C_SQ = C_GR + GLA_V
C_SK = C_SQ + SWA_Q
C_SV = C_SK + SWA_KV
C_GA = C_SV + SWA_KV
D_PROJ = C_GA + LANES

MIX_TM = 512
MIX_STREAM_TM = 512
MOE_TM = 544
DEC_SB = 16
VMEM_LIMIT = 56 * 1024 * 1024

F32 = jnp.float32
BF16 = jnp.bfloat16
NEG_INF = float("-inf")


def _dot(a, b):
    return jnp.dot(a, b, preferred_element_type=F32)


def _dot_nt(a, b):
    return lax.dot_general(a, b, (((1,), (1,)), ((), ())), preferred_element_type=F32)


def _split3(x):
    hi = x.astype(BF16)
    r1 = x - hi.astype(F32)
    mid = r1.astype(BF16)
    lo = (r1 - mid.astype(F32)).astype(BF16)
    return hi, mid, lo


def _rms(x, g):
    return x * lax.rsqrt(jnp.mean(x * x, axis=-1, keepdims=True) + RMS_EPS) * g


def _iota(shape, dim):
    return lax.broadcasted_iota(jnp.int32, shape, dim)


HALF_D = D_MODEL // 2
SLAB = HALF_D // LANES


def _pack_pair(lo, hi):
    bl = lax.bitcast_convert_type(lo.astype(BF16).astype(F32), jnp.uint32)
    bh = lax.bitcast_convert_type(hi.astype(BF16).astype(F32), jnp.uint32)
    return lax.bitcast_convert_type(bh | lax.shift_right_logical(bl, jnp.uint32(16)), jnp.int32)


def _unpack_pair(w):
    u = lax.bitcast_convert_type(w, jnp.uint32)
    lo = lax.bitcast_convert_type(lax.shift_left(u, jnp.uint32(16)), F32)
    hi = lax.bitcast_convert_type(u & jnp.uint32(0xFFFF0000), F32)
    return lo, hi


def _store_slabs(ref, x):
    rows = x.shape[0]
    for c in range(SLAB):
        sl = slice(c * LANES, (c + 1) * LANES)
        ref[pl.ds(c, rows, stride=SLAB), :] = _pack_pair(x[:, sl], x[:, HALF_D + c * LANES:HALF_D + (c + 1) * LANES])


def _load_slab_chunk(ref, rows, c):
    return _unpack_pair(ref[pl.ds(c, rows, stride=SLAB), :])


def _drain(steps):
    try:
        while True:
            next(steps)
    except StopIteration as done:
        return done.value


def _project_steps(x, g_mix, w_in_ref, w_a_up, b_a):
    h = _rms(x, g_mix).astype(BF16)
    yield

    def cols(lo, width, tile=2 * LANES):
        parts = []
        for off in range(0, width, tile):
            parts.append(_dot(h, w_in_ref[:, lo + off:lo + min(off + tile, width)]))
            yield
        return parts[0] if len(parts) == 1 else jnp.concatenate(parts, axis=1)

    ga = (yield from cols(C_GA, LANES)).astype(BF16)
    z = _dot(ga, w_a_up) + b_a
    log_a = -(jnp.maximum(-z, 0.0) + jnp.log(1.0 + jnp.exp(-jnp.abs(z)))) / GLA_GATE_TAU
    gqk = yield from cols(C_GQ, 2 * GLA_QK)
    gv = yield from cols(C_GV, GLA_V)
    swa = yield from cols(C_SQ, SWA_Q + 2 * SWA_KV)
    gr = yield from cols(C_GR, GLA_V)
    return dict(
        gq=gqk[:, :GLA_QK],
        gk=gqk[:, GLA_QK:],
        gv=gv,
        gr=gr,
        sq=swa[:, :SWA_Q],
        sk=swa[:, SWA_Q:SWA_Q + SWA_KV],
        sv=swa[:, SWA_Q + SWA_KV:],
        log_a=log_a,
    )


def _project(x, g_mix, w_in_ref, w_a_up, b_a):
    return _drain(_project_steps(x, g_mix, w_in_ref, w_a_up, b_a))


def _tail(*args):
    return _drain(_tail_steps(*args))


def _tail_steps(x, o_gla, gr, o_swa, g_gla_out, g_swa_out, w_out_ref, g_ffn, w_r, b_r, get_base):
    tm = x.shape[0]
    gate = gr * jax.nn.sigmoid(gr)
    parts = []
    for h in range(GLA_HEADS):
        sl = slice(h * GLA_DV, (h + 1) * GLA_DV)
        parts.append(_rms(o_gla[:, sl], g_gla_out) * gate[:, sl])
    og = jnp.concatenate(parts, axis=1).astype(BF16)
    yield
    os_ = _rms(o_swa, g_swa_out).astype(BF16)
    x_mid = x + _dot(og, w_out_ref[0:GLA_V])
    yield
    x_mid = x_mid + _dot(os_, w_out_ref[GLA_V:GLA_V + SWA_Q])
    yield
    hp = _rms(x_mid, g_ffn)

    h1 = hp.astype(BF16)
    h2 = (hp - h1.astype(F32)).astype(BF16)
    yield
    la = _dot_nt(w_r, h1)
    lb = _dot_nt(w_r[0:N_EXPERTS], h2)
    logits = la[0:N_EXPERTS] + la[N_EXPERTS:2 * N_EXPERTS] + lb + b_r
    yield

    eidx = _iota((N_EXPERTS, tm), 0)
    vals, idxs, onehots = [], [], []
    l = logits
    for _ in range(TOP_K):
        m = jnp.max(l, axis=0, keepdims=True)
        sel = jnp.min(jnp.where(l == m, eidx, N_EXPERTS), axis=0, keepdims=True)
        oh = eidx == sel
        l = jnp.where(oh, NEG_INF, l)
        vals.append(m)
        idxs.append(sel)
        onehots.append(oh)
    es = [jnp.exp(v - vals[0]) for v in vals]
    denom = es[0] + es[1] + es[2] + es[3]
    gates = [e / denom for e in es]

    ohf = jnp.concatenate([oh.astype(F32) for oh in onehots], axis=0)
    upper = (_iota((tm, tm), 0) < _iota((tm, tm), 1)).astype(BF16)
    prefix = _dot(ohf.astype(BF16), upper)
    yield
    base = get_base()
    ranks = []
    for k in range(TOP_K):
        sl = slice(k * N_EXPERTS, (k + 1) * N_EXPERTS)
        ohk = ohf[sl]
        base_t = jnp.concatenate([base] * (tm // LANES), axis=1)
        ranks.append(jnp.sum(ohk * (prefix[sl] + base_t), axis=0, keepdims=True))
        base = base + jnp.sum(ohk, axis=1, keepdims=True)
    zi = jnp.zeros((8 - TOP_K, tm), jnp.int32)
    zf = jnp.zeros((8 - TOP_K, tm), F32)
    topi = jnp.concatenate(idxs + [zi], axis=0)
    gate8 = jnp.concatenate(gates + [zf], axis=0)
    rank8 = jnp.concatenate([r.astype(jnp.int32) for r in ranks] + [zi], axis=0)
    return x_mid, hp, topi, gate8, rank8, base


def _gla_chunks(p, row0, s_blocks, n_lead_pad):
    tm = p["gq"].shape[0]
    nch = tm // GLA_CHUNK
    log_a = p["log_a"]
    if n_lead_pad:
        rows = row0 + _iota((tm, GLA_QK), 0)
        log_a = jnp.where(rows >= n_lead_pad, log_a, 0.0)
    ri, ci = _iota((tm, tm), 0), _iota((tm, tm), 1)
    tril = ((ri >= ci) & (ri // GLA_CHUNK == ci // GLA_CHUNK)).astype(BF16)
    hi, mid, lo = _split3(log_a)
    b_all = _dot(tril, hi) + _dot(tril, mid) + _dot(tril, lo)
    yield

    c64 = GLA_CHUNK
    kk_mask = (_iota((GLA_QK, GLA_QK), 0) // c64) == (_iota((GLA_QK, GLA_QK), 1) // GLA_DK)
    vv_mask = (_iota((GLA_QK, GLA_V), 0) // c64) == (_iota((GLA_QK, GLA_V), 1) // GLA_DV)
    zero_blk = jnp.zeros((GLA_DK, GLA_DV), BF16)
    causal = (_iota((c64, GLA_QK), 0) >= (_iota((c64, GLA_QK), 1) % c64)).astype(F32)
    zpad_k = jnp.zeros((LANES - c64, GLA_QK), F32)
    zpad_v = jnp.zeros((LANES - c64, GLA_V), BF16)

    outs = []
    for c in range(nch):
        rs = slice(c * c64, (c + 1) * c64)
        b = b_all[rs]
        q, k, v = p["gq"][rs], p["gk"][rs], p["gv"][rs]
        b_last = b[c64 - 1:c64]
        qt = (q * jnp.exp(b)).astype(BF16)
        kt = k * jnp.exp(-b)
        kd = k * jnp.exp(b_last - b)
        vb = v.astype(BF16)
        k_bd = jnp.where(kk_mask, jnp.concatenate([kt] * GLA_HEADS, axis=0), 0.0).astype(BF16)
        a = (_dot_nt(qt, k_bd) * causal).astype(BF16)
        v_bd = jnp.where(vv_mask, jnp.concatenate([vb] * GLA_HEADS, axis=0), jnp.zeros((), BF16))
        s_bd = jnp.concatenate(
            [jnp.concatenate([s_blocks[h].astype(BF16) if g == h else zero_blk for g in range(GLA_HEADS)], axis=1)
             for h in range(GLA_HEADS)], axis=0)
        outs.append(_dot(qt, s_bd) + _dot(a, v_bd))
        kd_t = jnp.transpose(jnp.concatenate([kd, zpad_k], axis=0)).astype(BF16)
        upd = _dot(kd_t, jnp.concatenate([vb, zpad_v], axis=0))
        decay = jnp.exp(jnp.transpose(jnp.broadcast_to(b_last, (LANES, GLA_QK))))
        s_blocks = [s_blocks[h] * decay[h * GLA_DK:(h + 1) * GLA_DK]
                    + upd[h * GLA_DK:(h + 1) * GLA_DK, h * GLA_DV:(h + 1) * GLA_DV] for h in range(GLA_HEADS)]
        yield
    return jnp.concatenate(outs, axis=0), s_blocks


def _swa_block(sq, kcat, vcat, bias_ref, sinks_ref, valid_t):
    half = _iota((1, LANES), 1) < SWA_HEAD_DIM
    top_rows = _iota((LANES, 1), 0) < SWA_HEAD_DIM
    k_roll = pltpu.roll(kcat, SWA_HEAD_DIM, 1)
    v_t = jnp.transpose(vcat)
    zeros_v = jnp.zeros((SWA_HEAD_DIM, 2 * WINDOW), F32)
    cols = []
    for kv in range(SWA_KV_HEADS):
        kk = jnp.where(half, kcat, k_roll) if kv == 0 else jnp.where(half, k_roll, kcat)
        q_parts = []
        for c in (2 * kv, 2 * kv + 1):
            qc = sq[:, c * LANES:(c + 1) * LANES]
            q_parts.append(jnp.where(half, qc, 0.0))
            q_parts.append(jnp.where(half, 0.0, qc))
        q_st = jnp.concatenate(q_parts, axis=0).astype(BF16)
        s = _dot_nt(kk.astype(BF16), q_st) + bias_ref[kv]
        if valid_t is not None:
            s = jnp.where(valid_t, s, NEG_INF)
        sink = jnp.concatenate(
            [jnp.full((1, WINDOW), sinks_ref[kv * SWA_GROUP + g], F32) for g in range(SWA_GROUP)], axis=1)
        m = jnp.maximum(jnp.max(s, axis=0, keepdims=True), sink)
        pr = jnp.exp(s - m)
        inv = 1.0 / (jnp.sum(pr, axis=0, keepdims=True) + jnp.exp(sink - m))
        pb = pr.astype(BF16)
        vk = v_t[kv * SWA_HEAD_DIM:(kv + 1) * SWA_HEAD_DIM]
        vv_t = jnp.concatenate([jnp.concatenate([vk, zeros_v], axis=1),
                                jnp.concatenate([zeros_v, vk], axis=1)], axis=0).astype(BF16)
        for pair in range(SWA_GROUP // 2):
            ce = slice(2 * pair * WINDOW, (2 * pair + 1) * WINDOW)
            co = slice((2 * pair + 1) * WINDOW, (2 * pair + 2) * WINDOW)
            p2_t = jnp.concatenate([pb[:, ce], pb[:, co]], axis=0)
            o2_t = _dot(vv_t, p2_t)
            o2_t = o2_t * jnp.where(top_rows, inv[:, ce], inv[:, co])
            cols.append(jnp.transpose(o2_t))
        yield
    return jnp.concatenate(cols, axis=1)


def _mixer_kernel(sinks_ref, x_ref, s0_ref, k0_ref, v0_ref, base0_ref, bias_ref,
                  g_mix_ref, w_in_ref, w_a_up_ref, b_a_ref, g_gla_ref, g_swa_ref, w_out_ref,
                  g_ffn_ref, w_r_ref, b_r_ref,
                  xmid_ref, hp_ref, topi_ref, gate_ref, rank_ref, sout_ref, kout_ref, vout_ref, cnt_ref,
                  s_scr, k_scr, v_scr, base_scr, *, n_lead_pad, prev_valid_from):
    g_id, j = pl.program_id(0), pl.program_id(1)
    n_streams, tm = x_ref.shape[1], x_ref.shape[2]

    @pl.when(j == 0)
    def _():
        for s in range(n_streams):
            s_scr[s] = s0_ref[...]
            k_scr[s] = k0_ref[...]
            v_scr[s] = v0_ref[...]

    @pl.when((j == 0) & (g_id == 0))
    def _():
        base_scr[...] = base0_ref[...]

    diag = [(slice(h * GLA_DK, (h + 1) * GLA_DK), slice(h * GLA_DV, (h + 1) * GLA_DV)) for h in range(GLA_HEADS)]

    def stream(s):
        x = x_ref[0, s]
        p = yield from _project_steps(x, g_mix_ref[...], w_in_ref, w_a_up_ref[...], b_a_ref[...])
        yield "mix"
        o_gla, s_blocks = yield from _gla_chunks(p, j * tm, [s_scr[s, r, c] for r, c in diag], n_lead_pad)
        for (r, c), blk in zip(diag, s_blocks):
            s_scr[s, r, c] = blk
        o_parts = []
        for sb in range(tm // WINDOW):
            rs = slice(sb * WINDOW, (sb + 1) * WINDOW)
            k_blk, v_blk = p["sk"][rs], p["sv"][rs]
            k_prev = k_scr[s] if sb == 0 else p["sk"][(sb - 1) * WINDOW:sb * WINDOW]
            v_prev = v_scr[s] if sb == 0 else p["sv"][(sb - 1) * WINDOW:sb * WINDOW]
            valid = None
            if sb == 0 and prev_valid_from:
                first = jnp.where(j == 0, prev_valid_from, 0)
                valid = _iota((2 * WINDOW, SWA_GROUP * WINDOW), 0) >= first
            o_parts.append((yield from _swa_block(
                p["sq"][rs], jnp.concatenate([k_prev, k_blk], axis=0),
                jnp.concatenate([v_prev, v_blk], axis=0), bias_ref, sinks_ref, valid)))
        o_swa = jnp.concatenate(o_parts, axis=0)
        k_scr[s] = p["sk"][tm - WINDOW:tm]
        v_scr[s] = p["sv"][tm - WINDOW:tm]
        yield "tail"
        x_mid, hp, topi, gate8, rank8, base = yield from _tail_steps(
            x, o_gla, p["gr"], o_swa, g_gla_ref[...], g_swa_ref[...], w_out_ref,
            g_ffn_ref[...], w_r_ref[...], b_r_ref[...], lambda: base_scr[...])
        base_scr[...] = base
        xmid_ref[0, s] = x_mid
        _store_slabs(hp_ref.at[0, s], hp)
        topi_ref[0, s] = topi
        gate_ref[0, s] = gate8
        rank_ref[0, s] = rank8
        sout_ref[0, s] = s_scr[s]
        kout_ref[0, s] = p["sk"][tm - WINDOW:tm]
        vout_ref[0, s] = p["sv"][tm - WINDOW:tm]
        cnt_ref[...] = base

    def advance(gen, stop):
        try:
            while next(gen) != stop or stop is None:
                pass
            return False
        except StopIteration:
            return True

    def alternate(gen_a, stop_a, gen_b, stop_b):
        done_a = done_b = False
        while not (done_a and done_b):
            if not done_a:
                try:
                    done_a = next(gen_a) == stop_a and stop_a is not None
                except StopIteration:
                    done_a = True
            if not done_b:
                try:
                    done_b = next(gen_b) == stop_b and stop_b is not None
                except StopIteration:
                    done_b = True

    if n_streams == 1:
        advance(stream(0), None)
    else:
        first, second = stream(0), stream(1)
        advance(first, "mix")
        alternate(first, "tail", second, "mix")
        alternate(first, None, second, "tail")
        advance(second, None)


def _full_spec(shape):
    nd = len(shape)
    return pl.BlockSpec(shape, lambda *_: (0,) * nd)


def _mixer_call(x, s0, k0, v0, base0, wts, tm, n_streams, n_lead_pad, prev_valid_from):
    B, L, _ = x.shape
    S = n_streams
    G = B // S
    nj = L // tm
    weight_args = (wts["bias"], wts["g_mix"], wts["w_in"], wts["w_a_up"], wts["b_a"], wts["g_gla"],
                   wts["g_swa"], wts["w_out"], wts["g_ffn"], wts["w_r"], wts["b_r"])
    in_specs = [
        pl.BlockSpec(memory_space=pltpu.SMEM),
        pl.BlockSpec((1, S, tm, D_MODEL), lambda g, j: (g, 0, j, 0)),
        _full_spec(s0.shape), _full_spec(k0.shape), _full_spec(v0.shape), _full_spec(base0.shape),
    ] + [_full_spec(w.shape) for w in weight_args]
    tok_spec = pl.BlockSpec((1, S, 8, tm), lambda g, j: (g, 0, 0, j))

    def per_seq_spec(rows, cols):
        return pl.BlockSpec((1, S, rows, cols), lambda g, j: (g, 0, 0, 0))

    out_specs = [
        pl.BlockSpec((1, S, tm, D_MODEL), lambda g, j: (g, 0, j, 0)),
        pl.BlockSpec((1, S, tm * SLAB, LANES), lambda g, j: (g, 0, j, 0)),
        tok_spec, tok_spec, tok_spec,
        per_seq_spec(GLA_QK, GLA_V), per_seq_spec(WINDOW, SWA_KV), per_seq_spec(WINDOW, SWA_KV),
        _full_spec((N_EXPERTS, LANES)),
    ]
    out_shape = [
        jax.ShapeDtypeStruct((G, S, L, D_MODEL), F32),
        jax.ShapeDtypeStruct((G, S, L * SLAB, LANES), jnp.int32),
        jax.ShapeDtypeStruct((G, S, 8, L), jnp.int32),
        jax.ShapeDtypeStruct((G, S, 8, L), F32),
        jax.ShapeDtypeStruct((G, S, 8, L), jnp.int32),
        jax.ShapeDtypeStruct((G, S, GLA_QK, GLA_V), F32),
        jax.ShapeDtypeStruct((G, S, WINDOW, SWA_KV), F32),
        jax.ShapeDtypeStruct((G, S, WINDOW, SWA_KV), F32),
        jax.ShapeDtypeStruct((N_EXPERTS, LANES), F32),
    ]
    kern = functools.partial(_mixer_kernel, n_lead_pad=n_lead_pad, prev_valid_from=prev_valid_from)
    xmid, hp, topi, gate, rank, s_out, k_out, v_out, cnt = pl.pallas_call(
        kern,
        grid=(G, nj),
        in_specs=in_specs,
        out_specs=out_specs,
        out_shape=out_shape,
        scratch_shapes=[pltpu.VMEM((S, GLA_QK, GLA_V), F32), pltpu.VMEM((S, WINDOW, SWA_KV), F32),
                        pltpu.VMEM((S, WINDOW, SWA_KV), F32), pltpu.VMEM((N_EXPERTS, LANES), F32)],
        compiler_params=pltpu.CompilerParams(dimension_semantics=("arbitrary", "arbitrary"),
                                             vmem_limit_bytes=VMEM_LIMIT),
        name="mixer",
    )(wts["sinks"], x.reshape(G, S, L, D_MODEL), s0, k0, v0, base0, *weight_args)

    def rows8(a):
        return jnp.transpose(a, (2, 0, 1, 3)).reshape(8, B * L)

    return (xmid.reshape(B, L, D_MODEL), hp.reshape(B * L * SLAB, LANES), rows8(topi), rows8(gate), rows8(rank),
            s_out.reshape(B, GLA_QK, GLA_V), k_out.reshape(B, WINDOW, SWA_KV), v_out.reshape(B, WINDOW, SWA_KV), cnt)


def _decode_kernel(sinks_ref, x_ref, st_ref, ck_ref, cv_ref, base0_ref, bias_ref,
                   g_mix_ref, w_in_ref, w_a_up_ref, b_a_ref, g_gla_ref, g_swa_ref, w_out_ref,
                   g_ffn_ref, w_r_ref, b_r_ref,
                   xmid_ref, hp_ref, topi_ref, gate_ref, rank_ref, sto_ref, cko_ref, cvo_ref, cnt_ref,
                   tq_scr, gv_scr, gr_scr, sq_scr, sk_scr, sv_scr, og_scr, os_scr):
    i = pl.program_id(0)
    n_seq = x_ref.shape[0]

    @pl.when(i == 0)
    def _():
        p = _project(x_ref[...], g_mix_ref[...], w_in_ref, w_a_up_ref[...], b_a_ref[...])
        a_hi, a_mid, a_lo = _split3(jnp.transpose(jnp.exp(p["log_a"])))
        tq_scr[...] = jnp.concatenate(
            [a_hi, a_mid, a_lo, jnp.transpose(p["gk"]).astype(BF16), jnp.transpose(p["gq"]).astype(BF16)], axis=0)
        gv_scr[...] = p["gv"]
        gr_scr[...] = p["gr"]
        sq_scr[...] = p["sq"]
        sk_scr[...] = p["sk"]
        sv_scr[...] = p["sv"]

    seq_row = _iota((n_seq, LANES), 0)
    half = _iota((1, LANES), 1) < SWA_HEAD_DIM
    row_id = _iota((WINDOW, SWA_KV), 0)
    head_diag = (_iota((16, SWA_Q), 1) // SWA_HEAD_DIM) == _iota((16, SWA_Q), 0)
    sink_col = jnp.concatenate(
        [jnp.full((1, 1), sinks_ref[h], F32) for h in range(SWA_HEADS)] + [jnp.zeros((8, 1), F32)], axis=0)

    def per_seq(sl, carry):
        s = i * DEC_SB + sl
        pick = (seq_row == s).astype(BF16)
        cols = _dot(tq_scr[...], pick)
        a_c = cols[0:GLA_QK] + cols[GLA_QK:2 * GLA_QK] + cols[2 * GLA_QK:3 * GLA_QK]
        k_c = cols[3 * GLA_QK:4 * GLA_QK]
        q_c = cols[4 * GLA_QK:5 * GLA_QK]
        st = st_ref[sl].reshape(GLA_QK, GLA_DV)
        v_row = gv_scr[pl.ds(s, 1), :]
        v_b = jnp.concatenate(
            [jnp.broadcast_to(v_row[:, h * GLA_DV:(h + 1) * GLA_DV], (GLA_DK, GLA_DV))
             for h in range(GLA_HEADS)], axis=0)
        st_new = a_c * st + k_c * v_b
        sto_ref[sl] = st_new.reshape(GLA_HEADS, GLA_DK, GLA_DV)
        t = q_c * st_new
        og_scr[pl.ds(s, 1), :] = jnp.concatenate(
            [jnp.sum(t[h * GLA_DK:(h + 1) * GLA_DK], axis=0, keepdims=True) for h in range(GLA_HEADS)],
            axis=1)

        k_new = sk_scr[pl.ds(s, 1), :]
        v_new = sv_scr[pl.ds(s, 1), :]
        kn = jnp.where(row_id == WINDOW - 1, k_new, pltpu.roll(ck_ref[sl], WINDOW - 1, 0))
        vn = jnp.where(row_id == WINDOW - 1, v_new, pltpu.roll(cv_ref[sl], WINDOW - 1, 0))
        cko_ref[sl] = kn
        cvo_ref[sl] = vn
        kr, vr = pltpu.roll(kn, SWA_HEAD_DIM, 1), pltpu.roll(vn, SWA_HEAD_DIM, 1)
        k0, k1 = jnp.where(half, kn, kr), jnp.where(half, kr, kn)
        v0, v1 = jnp.where(half, vn, vr), jnp.where(half, vr, vn)
        kw = jnp.concatenate([k0, k0, k1, k1], axis=1).astype(BF16)
        vw = jnp.concatenate([v0, v0, v1, v1], axis=1).astype(BF16)
        q_row = sq_scr[pl.ds(s, 1), :]
        qm = jnp.where(head_diag, jnp.broadcast_to(q_row, (16, SWA_Q)), 0.0).astype(BF16)
        sc = _dot_nt(qm, kw) + bias_ref[...]
        m = jnp.maximum(jnp.max(sc, axis=1, keepdims=True), sink_col)
        pr = jnp.exp(sc - m)
        inv = 1.0 / (jnp.sum(pr, axis=1, keepdims=True) + jnp.exp(sink_col - m))
        ow = _dot(pr.astype(BF16), vw) * inv
        os_scr[pl.ds(s, 1), :] = jnp.sum(jnp.where(head_diag, ow, 0.0), axis=0, keepdims=True)
        return carry

    lax.fori_loop(0, DEC_SB, per_seq, 0, unroll=8)

    @pl.when(i == pl.num_programs(0) - 1)
    def _():
        x_mid, hp, topi, gate8, rank8, base = _tail(
            x_ref[...], og_scr[...], gr_scr[...], os_scr[...], g_gla_ref[...], g_swa_ref[...],
            w_out_ref, g_ffn_ref[...], w_r_ref[...], b_r_ref[...], lambda: base0_ref[...])
        xmid_ref[...] = x_mid
        _store_slabs(hp_ref, hp)
        topi_ref[...] = topi
        gate_ref[...] = gate8
        rank_ref[...] = rank8
        cnt_ref[...] = base


def _decode_call(xs, state, ck, cv, base0, bias_dec, wts):
    n_seq = xs.shape[0]
    nb = n_seq // DEC_SB
    weight_args = (wts["g_mix"], wts["w_in"], wts["w_a_up"], wts["b_a"], wts["g_gla"],
                   wts["g_swa"], wts["w_out"], wts["g_ffn"], wts["w_r"], wts["b_r"])
    in_specs = [
        pl.BlockSpec(memory_space=pltpu.SMEM),
        _full_spec(xs.shape),
        pl.BlockSpec((DEC_SB, GLA_HEADS, GLA_DK, GLA_DV), lambda i: (i, 0, 0, 0)),
        pl.BlockSpec((DEC_SB, WINDOW, SWA_KV), lambda i: (i, 0, 0)),
        pl.BlockSpec((DEC_SB, WINDOW, SWA_KV), lambda i: (i, 0, 0)),
        _full_spec(base0.shape), _full_spec(bias_dec.shape),
    ] + [_full_spec(w.shape) for w in weight_args]
    out_specs = [
        _full_spec((n_seq, D_MODEL)),
        _full_spec((n_seq * SLAB, LANES)),
        _full_spec((8, n_seq)), _full_spec((8, n_seq)), _full_spec((8, n_seq)),
        pl.BlockSpec((DEC_SB, GLA_HEADS, GLA_DK, GLA_DV), lambda i: (i, 0, 0, 0)),
        pl.BlockSpec((DEC_SB, WINDOW, SWA_KV), lambda i: (i, 0, 0)),
        pl.BlockSpec((DEC_SB, WINDOW, SWA_KV), lambda i: (i, 0, 0)),
        _full_spec((N_EXPERTS, LANES)),
    ]
    out_shape = [
        jax.ShapeDtypeStruct((n_seq, D_MODEL), F32),
        jax.ShapeDtypeStruct((n_seq * SLAB, LANES), jnp.int32),
        jax.ShapeDtypeStruct((8, n_seq), jnp.int32),
        jax.ShapeDtypeStruct((8, n_seq), F32),
        jax.ShapeDtypeStruct((8, n_seq), jnp.int32),
        jax.ShapeDtypeStruct(state.shape, F32),
        jax.ShapeDtypeStruct(ck.shape, F32),
        jax.ShapeDtypeStruct(cv.shape, F32),
        jax.ShapeDtypeStruct((N_EXPERTS, LANES), F32),
    ]
    scratch = [pltpu.VMEM((5 * GLA_QK, n_seq), BF16)] + [
        pltpu.VMEM((n_seq, GLA_V), F32), pltpu.VMEM((n_seq, GLA_V), F32), pltpu.VMEM((n_seq, SWA_Q), F32),
        pltpu.VMEM((n_seq, SWA_KV), F32), pltpu.VMEM((n_seq, SWA_KV), F32),
        pltpu.VMEM((n_seq, GLA_V), F32), pltpu.VMEM((n_seq, SWA_Q), F32)]
    return pl.pallas_call(
        _decode_kernel,
        grid=(nb,),
        in_specs=in_specs,
        out_specs=out_specs,
        out_shape=out_shape,
        scratch_shapes=scratch,
        compiler_params=pltpu.CompilerParams(dimension_semantics=("arbitrary",),
                                             vmem_limit_bytes=VMEM_LIMIT),
        name="decode",
    )(wts["sinks"], xs, state, ck, cv, base0, bias_dec, *weight_args)


SC_CORES = 2
SC_SUBCORES = 16
SC_WORKERS = SC_CORES * SC_SUBCORES
SC_SCATTER_ROWS = 64
SC_GATHER_ROWS = 48


def _sc_mesh():
    return plsc.VectorSubcoreMesh(core_axis_name="c", subcore_axis_name="s")


def _sc_worker_id():
    return lax.axis_index("s") * SC_CORES + lax.axis_index("c")


def _sc_scatter_rows(src_p, src_s, idx_p, idx_s, n_out):
    rows = SC_SCATTER_ROWS
    n_chunks = idx_p.shape[0] // SC_WORKERS
    n_s, _, rows_s = idx_s.shape
    assert n_chunks * SC_WORKERS == idx_p.shape[0] and n_chunks % 2 == 0 and n_s <= SC_WORKERS

    @functools.partial(
        pl.kernel, mesh=_sc_mesh(),
        out_type=jax.ShapeDtypeStruct((n_out, SLAB, LANES), jnp.int32),
        scratch_types=[pltpu.VMEM((2, TOP_K, rows), jnp.int32), pltpu.VMEM((2, rows, SLAB, LANES), jnp.int32),
                       pltpu.VMEM((TOP_K, rows_s), jnp.int32), pltpu.VMEM((rows_s, SLAB, LANES), jnp.int32),
                       pltpu.SemaphoreType.DMA((2,)), pltpu.SemaphoreType.DMA((2,))])
    def scatter_rows(srcp_hbm, srcs_hbm, idxp_hbm, idxs_hbm, out_hbm, idx_v, rows_v, idxs_v, rowss_v, lsem, ssem):
        wid = _sc_worker_id()

        def loads(c, b):
            g = wid * n_chunks + c
            return (pltpu.make_async_copy(idxp_hbm.at[g], idx_v.at[b], lsem.at[b]),
                    pltpu.make_async_copy(srcp_hbm.at[pl.ds(pl.multiple_of(g * rows, 8), rows)], rows_v.at[b],
                                          lsem.at[b]))

        def scatters(b):
            return [pltpu.make_async_copy(rows_v.at[b], out_hbm.at[idx_v.at[b, k]], ssem.at[b])
                    for k in range(TOP_K)]

        for d in loads(0, 0):
            d.start()

        @pl.loop(0, n_chunks, step=2)
        def _(c0):
            for b in range(2):
                c = c0 + b
                for d in loads(c, b):
                    d.wait()

                @pl.when(c >= 1)
                def _():
                    for d in scatters(1 - b):
                        d.wait()

                @pl.when(c + 1 < n_chunks)
                def _():
                    for d in loads(c + 1, 1 - b):
                        d.start()

                for d in scatters(b):
                    d.start()

        for d in scatters((n_chunks - 1) % 2):
            d.wait()

        @pl.when(wid < n_s)
        def _():
            pltpu.sync_copy(idxs_hbm.at[wid], idxs_v)
            pltpu.sync_copy(srcs_hbm.at[pl.ds(pl.multiple_of(wid * rows_s, 8), rows_s)], rowss_v)
            for k in range(TOP_K):
                pltpu.sync_copy(rowss_v, out_hbm.at[idxs_v.at[k]])

    return scatter_rows(src_p, src_s, idx_p, idx_s)


def _sc_gather_rows(src3, idx2):
    rows = SC_GATHER_ROWS
    n_chunks = idx2.shape[0] // SC_WORKERS
    assert n_chunks * SC_WORKERS == idx2.shape[0] and idx2.shape[1] == rows and n_chunks % 2 == 0

    @functools.partial(
        pl.kernel, mesh=_sc_mesh(),
        out_type=jax.ShapeDtypeStruct((idx2.shape[0] * rows, SLAB, LANES), jnp.int32),
        scratch_types=[pltpu.VMEM((2, rows), jnp.int32), pltpu.VMEM((2, rows, SLAB, LANES), jnp.int32),
                       pltpu.SemaphoreType.DMA((2,)), pltpu.SemaphoreType.DMA((2,))])
    def gather_rows(src_hbm, idx_hbm, out_hbm, idx_v, rows_v, gsem, wsem):
        wid = _sc_worker_id()

        def gather(b):
            return pltpu.make_async_copy(src_hbm.at[idx_v.at[b]], rows_v.at[b], gsem.at[b])

        def write(c, b):
            base = pl.multiple_of((wid * n_chunks + c) * rows, 8)
            return pltpu.make_async_copy(rows_v.at[b], out_hbm.at[pl.ds(base, rows)], wsem.at[b])

        pltpu.sync_copy(idx_hbm.at[wid * n_chunks], idx_v.at[0])
        gather(0).start()

        @pl.loop(0, n_chunks, step=2)
        def _(c0):
            for b in range(2):
                c = c0 + b

                @pl.when(c + 1 < n_chunks)
                def _():
                    @pl.when(c >= 1)
                    def _():
                        write(c - 1, 1 - b).wait()
                    pltpu.sync_copy(idx_hbm.at[wid * n_chunks + c + 1], idx_v.at[1 - b])
                    gather(1 - b).start()

                gather(b).wait()
                write(c, b).start()

        write(n_chunks - 2, 0).wait()
        write(n_chunks - 1, 1).wait()

    return gather_rows(src3, idx2)


FF_TILE = 256


def _ffn_kernel(blk_e_ref, nused_ref, x_ref, wu_ref, bu_ref, wd_ref, bd_ref, y_ref, xbf, actbf, wu_bf, wd_bf):
    i = pl.program_id(0)
    tm = MOE_TM
    n_tiles = D_FF // FF_TILE

    @pl.when(i < nused_ref[0])
    def _():
        @pl.when((i == 0) | (blk_e_ref[i] != blk_e_ref[jnp.maximum(i - 1, 0)]))
        def _():
            wu_bf[...] = wu_ref[0].astype(BF16)
            wd_bf[...] = wd_ref[0].astype(BF16)

        for c in range(SLAB):
            lo, hi = _load_slab_chunk(x_ref, tm, c)
            xbf[:, c * LANES:(c + 1) * LANES] = lo.astype(BF16)
            xbf[:, HALF_D + c * LANES:HALF_D + (c + 1) * LANES] = hi.astype(BF16)
        for n in range(n_tiles):
            gc = slice(n * FF_TILE, (n + 1) * FF_TILE)
            lc = slice(D_FF + n * FF_TILE, D_FF + (n + 1) * FF_TILE)
            g = jnp.minimum(_dot(xbf[...], wu_bf[:, gc]) + bu_ref[0, :, gc], SWIGLU_LIMIT)
            lin = jnp.clip(_dot(xbf[...], wu_bf[:, lc]) + bu_ref[0, :, lc], -SWIGLU_LIMIT, SWIGLU_LIMIT)
            actbf[:, gc] = (g * jax.nn.sigmoid(SWIGLU_ALPHA * g) * (lin + 1.0)).astype(BF16)
        per_tile = FF_TILE // LANES
        for n in range(n_tiles // 2):
            yl = slice(n * FF_TILE, (n + 1) * FF_TILE)
            yh = slice(HALF_D + n * FF_TILE, HALF_D + (n + 1) * FF_TILE)
            y_lo = _dot(actbf[...], wd_bf[:, yl]) + bd_ref[0, :, yl]
            y_hi = _dot(actbf[...], wd_bf[:, yh]) + bd_ref[0, :, yh]
            for c in range(per_tile):
                sl = slice(c * LANES, (c + 1) * LANES)
                y_ref[pl.ds(n * per_tile + c, tm, stride=SLAB), :] = _pack_pair(y_lo[:, sl], y_hi[:, sl])

    @pl.when(i >= nused_ref[0])
    def _():
        y_ref[...] = jnp.zeros_like(y_ref)


def _ffn_call(blk_e, nused, xs2, w_up, b_up, w_down, b_down):
    n_blocks = blk_e.shape[0]
    tm = MOE_TM
    row_blk = pl.BlockSpec((tm * SLAB, LANES), lambda i, be, nu: (i, 0))
    grid_spec = pltpu.PrefetchScalarGridSpec(
        num_scalar_prefetch=2,
        grid=(n_blocks,),
        in_specs=[
            row_blk,
            pl.BlockSpec((1, D_MODEL, 2 * D_FF), lambda i, be, nu: (be[i], 0, 0)),
            pl.BlockSpec((1, 1, 2 * D_FF), lambda i, be, nu: (be[i], 0, 0)),
            pl.BlockSpec((1, D_FF, D_MODEL), lambda i, be, nu: (be[i], 0, 0)),
            pl.BlockSpec((1, 1, D_MODEL), lambda i, be, nu: (be[i], 0, 0)),
        ],
        out_specs=row_blk,
        scratch_shapes=[pltpu.VMEM((tm, D_MODEL), BF16), pltpu.VMEM((tm, D_FF), BF16),
                        pltpu.VMEM((D_MODEL, 2 * D_FF), BF16), pltpu.VMEM((D_FF, D_MODEL), BF16)],
    )
    return pl.pallas_call(
        _ffn_kernel,
        grid_spec=grid_spec,
        out_shape=jax.ShapeDtypeStruct((n_blocks * tm * SLAB, LANES), jnp.int32),
        compiler_params=pltpu.CompilerParams(dimension_semantics=("arbitrary",),
                                             vmem_limit_bytes=VMEM_LIMIT),
        name="experts",
    )(blk_e, nused, xs2, w_up, b_up.reshape(N_EXPERTS, 1, 2 * D_FF), w_down, b_down.reshape(N_EXPERTS, 1, D_MODEL))


def _combine_kernel(ys0_ref, ys1_ref, ys2_ref, ys3_ref, xmid_ref, gate_ref, g_final_ref, y_ref):
    tm = xmid_ref.shape[0]
    gts = jnp.transpose(jnp.concatenate([gate_ref[...], jnp.zeros((LANES - 8, tm), F32)], axis=0))
    lows, highs = [], []
    for c in range(SLAB):
        acc_lo = xmid_ref[:, c * LANES:(c + 1) * LANES]
        acc_hi = xmid_ref[:, HALF_D + c * LANES:HALF_D + (c + 1) * LANES]
        for k, ys_ref in enumerate((ys0_ref, ys1_ref, ys2_ref, ys3_ref)):
            lo, hi = _load_slab_chunk(ys_ref, tm, c)
            acc_lo = acc_lo + lo * gts[:, k:k + 1]
            acc_hi = acc_hi + hi * gts[:, k:k + 1]
        lows.append(acc_lo)
        highs.append(acc_hi)
    y_ref[...] = _rms(jnp.concatenate(lows + highs, axis=1), g_final_ref[...])


def _combine_call(ys4, t_stride, row0, x_mid, gates, g_final, tm):
    T = x_mid.shape[0]
    blk0 = row0 // tm
    per_k = t_stride // tm
    assert per_k * tm == t_stride and blk0 * tm == row0

    def ys_spec(k):
        return pl.BlockSpec((tm * SLAB, LANES), lambda i: (k * per_k + blk0 + i, 0))

    return pl.pallas_call(
        _combine_kernel,
        grid=(T // tm,),
        in_specs=[
            ys_spec(0), ys_spec(1), ys_spec(2), ys_spec(3),
            pl.BlockSpec((tm, D_MODEL), lambda i: (i, 0)),
            pl.BlockSpec((8, tm), lambda i: (0, i)),
            _full_spec((1, D_MODEL)),
        ],
        out_specs=pl.BlockSpec((tm, D_MODEL), lambda i: (i, 0)),
        out_shape=jax.ShapeDtypeStruct((T, D_MODEL), F32),
        compiler_params=pltpu.CompilerParams(dimension_semantics=("arbitrary",),
                                             vmem_limit_bytes=VMEM_LIMIT),
        name="combine",
    )(ys4, ys4, ys4, ys4, x_mid, gates, g_final)


def _t5_bucket(dist):
    n = jnp.maximum(dist, 0)
    max_exact = NUM_BUCKETS // 2
    nf = jnp.maximum(n, 1).astype(F32)
    large = max_exact + (jnp.log(nf / max_exact) / math.log(MAX_DISTANCE / max_exact)
                         * (NUM_BUCKETS - max_exact)).astype(jnp.int32)
    large = jnp.minimum(large, NUM_BUCKETS - 1)
    return jnp.where(n < max_exact, n, large)


def kernel(x_prompt, x_sample, state_gla, cache_swa_k, cache_swa_v, meta_tokens, rel_bias_table,
           g_mix, w_in, w_a_up, b_a, g_gla_out, g_swa_out, attn_sinks, w_out,
           g_ffn, w_router, b_router, w_up, b_up, w_down, b_down, g_final):
    assert g_mix.shape[0] == 1, "single-layer trunk"
    B, L, _ = x_prompt.shape
    n_seq = x_sample.shape[0]
    TP = B * L
    T_all = TP + n_seq

    wi = w_in[0]
    sizes = (GLA_QK, GLA_QK, GLA_V, GLA_V, GLA_LOWRANK, SWA_Q, SWA_KV, SWA_KV)
    offs = [0]
    for s in sizes:
        offs.append(offs[-1] + s)
    seg = [wi[:, offs[n]:offs[n + 1]] for n in range(8)]
    seg[0] = seg[0] * (GLA_DK ** -0.5)
    seg[5] = seg[5] * (SWA_HEAD_DIM ** -0.5)
    w_in_r = jnp.concatenate(
        seg[0:4] + seg[5:8] + [seg[4], jnp.zeros((D_MODEL, LANES - GLA_LOWRANK), F32)], axis=1).astype(BF16)
    w_a_pad = jnp.concatenate([w_a_up[0], jnp.zeros((LANES - GLA_LOWRANK, GLA_QK), F32)], axis=0).astype(BF16)
    wr_t = jnp.transpose(w_router[0])
    wr_hi = wr_t.astype(BF16)
    wr_lo = (wr_t - wr_hi.astype(F32)).astype(BF16)
    qi = jnp.arange(WINDOW)[:, None]
    kj = jnp.arange(2 * WINDOW)[None, :]
    buckets = jnp.arange(NUM_BUCKETS)
    table = rel_bias_table.astype(F32)
    oh_p = (_t5_bucket(qi - kj + WINDOW)[..., None] == buckets).astype(F32)
    bias_p = jnp.einsum("qkb,bh->hkq", oh_p, table, precision=lax.Precision.HIGHEST)
    in_window = jnp.transpose((kj > qi) & (kj <= qi + WINDOW))
    bias_p = jnp.where(in_window[None], bias_p, NEG_INF)
    bias_p = bias_p.reshape(SWA_KV_HEADS, SWA_GROUP, 2 * WINDOW, WINDOW).transpose(0, 2, 1, 3)
    bias_p = bias_p.reshape(SWA_KV_HEADS, 2 * WINDOW, SWA_GROUP * WINDOW)
    oh_d = (_t5_bucket(WINDOW - 1 - jnp.arange(WINDOW))[:, None] == buckets).astype(F32)
    bias_d = jnp.einsum("rb,bh->hr", oh_d, table, precision=lax.Precision.HIGHEST)
    bias_d = jnp.concatenate([bias_d, jnp.zeros((8, WINDOW), F32)], axis=0)
    wts = dict(
        sinks=attn_sinks[0].astype(F32), bias=bias_p,
        g_mix=g_mix[0][None], w_in=w_in_r, w_a_up=w_a_pad, b_a=b_a[0][None],
        g_gla=g_gla_out[0][None], g_swa=g_swa_out[0][None], w_out=w_out[0].astype(BF16),
        g_ffn=g_ffn[0][None], w_r=jnp.concatenate([wr_hi, wr_lo], axis=0), b_r=b_router[0][:, None],
    )

    x_pre = jnp.concatenate([jnp.zeros((WINDOW - N_META, D_MODEL), F32), meta_tokens.astype(F32)], axis=0)[None]
    zeros_s = jnp.zeros((GLA_QK, GLA_V), F32)
    zeros_kv = jnp.zeros((WINDOW, SWA_KV), F32)
    zeros_b = jnp.zeros((N_EXPERTS, LANES), F32)
    pre = _mixer_call(x_pre, zeros_s, zeros_kv, zeros_kv, zeros_b, wts, WINDOW, 1, WINDOW - N_META, 0)
    s_meta, k_meta, v_meta = pre[5][0], pre[6][0], pre[7][0]

    (xmid_p, hp_p, topi_p, gate_p, rank_p, s_p, k_p, v_p, cnt_p) = _mixer_call(
        x_prompt, s_meta, k_meta, v_meta, zeros_b, wts, MIX_STREAM_TM, 2, 0, WINDOW - N_META)

    (xmid_s, hp_s, topi_s, gate_s, rank_s, st_s, ck_s, cv_s, cnt_all) = _decode_call(
        x_sample[:, 0], state_gla[0], cache_swa_k[0].reshape(n_seq, WINDOW, SWA_KV),
        cache_swa_v[0].reshape(n_seq, WINDOW, SWA_KV), cnt_p, bias_d, wts)

    tm = MOE_TM
    n_slots = T_all * TOP_K
    n_blocks = -(-n_slots // tm) + N_EXPERTS
    top_e = jnp.concatenate([topi_p[:TOP_K], topi_s[:TOP_K]], axis=1)
    rank = jnp.concatenate([rank_p[:TOP_K], rank_s[:TOP_K]], axis=1)
    counts = cnt_all[:, 0].astype(jnp.int32)
    padded = (counts + tm - 1) // tm * tm
    pad_end = jnp.cumsum(padded)
    pad_start = pad_end - padded
    e_ids = jnp.arange(N_EXPERTS, dtype=jnp.int32)
    dest = jnp.sum(jnp.where(top_e[..., None] == e_ids, pad_start, 0), axis=-1) + rank
    n_pad = n_blocks * tm
    blk_e = jnp.minimum(jnp.sum(pad_end[None] <= (jnp.arange(n_blocks, dtype=jnp.int32) * tm)[:, None], axis=1),
                        N_EXPERTS - 1).astype(jnp.int32)
    nused = (pad_end[-1] // tm).astype(jnp.int32).reshape(1)

    sample_rows = 8
    idx_p = dest[:, :TP].reshape(TOP_K, TP // SC_SCATTER_ROWS, SC_SCATTER_ROWS).transpose(1, 0, 2)
    idx_s = dest[:, TP:].reshape(TOP_K, n_seq // sample_rows, sample_rows).transpose(1, 0, 2)
    xs3 = _sc_scatter_rows(hp_p.reshape(TP, SLAB, LANES), hp_s.reshape(n_seq, SLAB, LANES), idx_p, idx_s, n_pad)
    ys2 = _ffn_call(blk_e, nused, xs3.reshape(-1, LANES), w_up[0], b_up[0], w_down[0], b_down[0])
    unit = math.lcm(2 * SC_WORKERS * SC_GATHER_ROWS // TOP_K, MIX_TM)
    t_stride = -(-T_all // unit) * unit
    filler = jnp.arange(TOP_K * (t_stride - T_all), dtype=jnp.int32).reshape(TOP_K, t_stride - T_all)
    slot_src = jnp.concatenate([dest, filler], axis=1)
    slot_src = slot_src.reshape(TOP_K * t_stride // SC_GATHER_ROWS, SC_GATHER_ROWS)
    ys4 = _sc_gather_rows(ys2.reshape(-1, SLAB, LANES), slot_src).reshape(-1, LANES)

    gf = g_final[None]
    y_p = _combine_call(ys4, t_stride, 0, xmid_p.reshape(TP, D_MODEL), gate_p, gf, MIX_TM)
    y_s = _combine_call(ys4, t_stride, TP, xmid_s, gate_s, gf, n_seq)

    s_heads = jnp.stack([s_p[:, h * GLA_DK:(h + 1) * GLA_DK, h * GLA_DV:(h + 1) * GLA_DV]
                         for h in range(GLA_HEADS)], axis=1)
    return (y_p.reshape(B, L, D_MODEL), y_s.reshape(n_seq, 1, D_MODEL), s_heads[None],
            k_p.reshape(1, B, WINDOW, SWA_KV_HEADS, SWA_HEAD_DIM),
            v_p.reshape(1, B, WINDOW, SWA_KV_HEADS, SWA_HEAD_DIM),
            st_s[None], ck_s.reshape(1, n_seq, WINDOW, SWA_KV_HEADS, SWA_HEAD_DIM),
            cv_s.reshape(1, n_seq, WINDOW, SWA_KV_HEADS, SWA_HEAD_DIM))
```

```python
import functools
import math

import jax
import jax.numpy as jnp
from jax import lax
from jax.experimental import pallas as pl
from jax.experimental.pallas import tpu as pltpu
from jax.experimental.pallas import tpu_sc as plsc

D_MODEL = 1024
N_META = 16
GLA_HEADS = 4
GLA_DK = 64
GLA_DV = 128
GLA_LOWRANK = 16
GLA_GATE_TAU = 16.0
GLA_CHUNK = 64
SWA_HEADS = 8
SWA_KV_HEADS = 2
SWA_HEAD_DIM = 64
SWA_GROUP = SWA_HEADS // SWA_KV_HEADS
WINDOW = 128
NUM_BUCKETS = 32
MAX_DISTANCE = 128
N_EXPERTS = 32
TOP_K = 4
D_FF = 1024
SWIGLU_ALPHA = 1.702
SWIGLU_LIMIT = 7.0
RMS_EPS = 1e-6

GLA_QK = GLA_HEADS * GLA_DK
GLA_V = GLA_HEADS * GLA_DV
SWA_Q = SWA_HEADS * SWA_HEAD_DIM
SWA_KV = SWA_KV_HEADS * SWA_HEAD_DIM
LANES = 128
C_GQ, C_GK, C_GV, C_GR = 0, GLA_QK, 2 * GLA_QK, 2 * GLA_QK + GLA_V
C_SQ = C_GR + GLA_V
C_SK = C_SQ + SWA_Q
C_SV = C_SK + SWA_KV
C_GA = C_SV + SWA_KV
D_PROJ = C_GA + LANES

MIX_TM = 512
MIX_STREAM_TM = 512
MOE_TM = 1088
DEC_SB = 16
VMEM_LIMIT = 56 * 1024 * 1024

F32 = jnp.float32
BF16 = jnp.bfloat16
NEG_INF = float("-inf")


def _dot(a, b):
    return jnp.dot(a, b, preferred_element_type=F32)


def _dot_nt(a, b):
    return lax.dot_general(a, b, (((1,), (1,)), ((), ())), preferred_element_type=F32)


def _split3(x):
    hi = x.astype(BF16)
    r1 = x - hi.astype(F32)
    mid = r1.astype(BF16)
    lo = (r1 - mid.astype(F32)).astype(BF16)
    return hi, mid, lo


def _rms(x, g):
    return x * lax.rsqrt(jnp.mean(x * x, axis=-1, keepdims=True) + RMS_EPS) * g


def _iota(shape, dim):
    return lax.broadcasted_iota(jnp.int32, shape, dim)


HALF_D = D_MODEL // 2
SLAB = HALF_D // LANES


def _pack_pair(lo, hi):
    bl = lax.bitcast_convert_type(lo.astype(BF16).astype(F32), jnp.uint32)
    bh = lax.bitcast_convert_type(hi.astype(BF16).astype(F32), jnp.uint32)
    return lax.bitcast_convert_type(bh | lax.shift_right_logical(bl, jnp.uint32(16)), jnp.int32)


def _unpack_pair(w):
    u = lax.bitcast_convert_type(w, jnp.uint32)
    lo = lax.bitcast_convert_type(lax.shift_left(u, jnp.uint32(16)), F32)
    hi = lax.bitcast_convert_type(u & jnp.uint32(0xFFFF0000), F32)
    return lo, hi


def _store_slabs(ref, x):
    rows = x.shape[0]
    for c in range(SLAB):
        sl = slice(c * LANES, (c + 1) * LANES)
        ref[pl.ds(c, rows, stride=SLAB), :] = _pack_pair(x[:, sl], x[:, HALF_D + c * LANES:HALF_D + (c + 1) * LANES])


def _load_slab_chunk(ref, rows, c):
    return _unpack_pair(ref[pl.ds(c, rows, stride=SLAB), :])


def _drain(steps):
    try:
        while True:
            next(steps)
    except StopIteration as done:
        return done.value


def _project_steps(x, g_mix, w_in_ref, w_a_up, b_a):
    h = _rms(x, g_mix).astype(BF16)
    yield

    def cols(lo, width, tile=2 * LANES):
        parts = []
        for off in range(0, width, tile):
            parts.append(_dot(h, w_in_ref[:, lo + off:lo + min(off + tile, width)]))
            yield
        return parts[0] if len(parts) == 1 else jnp.concatenate(parts, axis=1)

    ga = (yield from cols(C_GA, LANES)).astype(BF16)
    z = _dot(ga, w_a_up) + b_a
    log_a = -(jnp.maximum(-z, 0.0) + jnp.log(1.0 + jnp.exp(-jnp.abs(z)))) / GLA_GATE_TAU
    gqk = yield from cols(C_GQ, 2 * GLA_QK)
    gv = yield from cols(C_GV, GLA_V)
    swa = yield from cols(C_SQ, SWA_Q + 2 * SWA_KV)
    gr = yield from cols(C_GR, GLA_V)
    return dict(
        gq=gqk[:, :GLA_QK],
        gk=gqk[:, GLA_QK:],
        gv=gv,
        gr=gr,
        sq=swa[:, :SWA_Q],
        sk=swa[:, SWA_Q:SWA_Q + SWA_KV],
        sv=swa[:, SWA_Q + SWA_KV:],
        log_a=log_a,
    )


def _project(x, g_mix, w_in_ref, w_a_up, b_a):
    return _drain(_project_steps(x, g_mix, w_in_ref, w_a_up, b_a))


def _tail(*args):
    return _drain(_tail_steps(*args))


def _tail_steps(x, o_gla, gr, o_swa, g_gla_out, g_swa_out, w_out_ref, g_ffn, w_r, b_r, get_base):
    tm = x.shape[0]
    gate = gr * jax.nn.sigmoid(gr)
    parts = []
    for h in range(GLA_HEADS):
        sl = slice(h * GLA_DV, (h + 1) * GLA_DV)
        parts.append(_rms(o_gla[:, sl], g_gla_out) * gate[:, sl])
    og = jnp.concatenate(parts, axis=1).astype(BF16)
    yield
    os_ = _rms(o_swa, g_swa_out).astype(BF16)
    x_mid = x + _dot(og, w_out_ref[0:GLA_V])
    yield
    x_mid = x_mid + _dot(os_, w_out_ref[GLA_V:GLA_V + SWA_Q])
    yield
    hp = _rms(x_mid, g_ffn)

    h1 = hp.astype(BF16)
    h2 = (hp - h1.astype(F32)).astype(BF16)
    yield
    la = _dot_nt(w_r, h1)
    lb = _dot_nt(w_r[0:N_EXPERTS], h2)
    logits = la[0:N_EXPERTS] + la[N_EXPERTS:2 * N_EXPERTS] + lb + b_r
    yield

    eidx = _iota((N_EXPERTS, tm), 0)
    vals, idxs, onehots = [], [], []
    l = logits
    for _ in range(TOP_K):
        m = jnp.max(l, axis=0, keepdims=True)
        sel = jnp.min(jnp.where(l == m, eidx, N_EXPERTS), axis=0, keepdims=True)
        oh = eidx == sel
        l = jnp.where(oh, NEG_INF, l)
        vals.append(m)
        idxs.append(sel)
        onehots.append(oh)
    es = [jnp.exp(v - vals[0]) for v in vals]
    denom = es[0] + es[1] + es[2] + es[3]
    gates = [e / denom for e in es]

    ohf = jnp.concatenate([oh.astype(F32) for oh in onehots], axis=0)
    upper = (_iota((tm, tm), 0) < _iota((tm, tm), 1)).astype(BF16)
    prefix = _dot(ohf.astype(BF16), upper)
    yield
    base = get_base()
    ranks = []
    for k in range(TOP_K):
        sl = slice(k * N_EXPERTS, (k + 1) * N_EXPERTS)
        ohk = ohf[sl]
        base_t = jnp.concatenate([base] * (tm // LANES), axis=1)
        ranks.append(jnp.sum(ohk * (prefix[sl] + base_t), axis=0, keepdims=True))
        base = base + jnp.sum(ohk, axis=1, keepdims=True)
    zi = jnp.zeros((8 - TOP_K, tm), jnp.int32)
    zf = jnp.zeros((8 - TOP_K, tm), F32)
    topi = jnp.concatenate(idxs + [zi], axis=0)
    gate8 = jnp.concatenate(gates + [zf], axis=0)
    rank8 = jnp.concatenate([r.astype(jnp.int32) for r in ranks] + [zi], axis=0)
    return x_mid, hp, topi, gate8, rank8, base


def _gla_chunks(p, row0, s_blocks, n_lead_pad):
    tm = p["gq"].shape[0]
    nch = tm // GLA_CHUNK
    log_a = p["log_a"]
    if n_lead_pad:
        rows = row0 + _iota((tm, GLA_QK), 0)
        log_a = jnp.where(rows >= n_lead_pad, log_a, 0.0)
    ri, ci = _iota((tm, tm), 0), _iota((tm, tm), 1)
    tril = ((ri >= ci) & (ri // GLA_CHUNK == ci // GLA_CHUNK)).astype(BF16)
    hi, mid, lo = _split3(log_a)
    b_all = _dot(tril, hi) + _dot(tril, mid) + _dot(tril, lo)
    yield

    c64 = GLA_CHUNK
    kk_mask = (_iota((GLA_QK, GLA_QK), 0) // c64) == (_iota((GLA_QK, GLA_QK), 1) // GLA_DK)
    vv_mask = (_iota((GLA_QK, GLA_V), 0) // c64) == (_iota((GLA_QK, GLA_V), 1) // GLA_DV)
    zero_blk = jnp.zeros((GLA_DK, GLA_DV), BF16)
    causal = (_iota((c64, GLA_QK), 0) >= (_iota((c64, GLA_QK), 1) % c64)).astype(F32)
    zpad_k = jnp.zeros((LANES - c64, GLA_QK), F32)
    zpad_v = jnp.zeros((LANES - c64, GLA_V), BF16)

    outs = []
    for c in range(nch):
        rs = slice(c * c64, (c + 1) * c64)
        b = b_all[rs]
        q, k, v = p["gq"][rs], p["gk"][rs], p["gv"][rs]
        b_last = b[c64 - 1:c64]
        qt = (q * jnp.exp(b)).astype(BF16)
        kt = k * jnp.exp(-b)
        kd = k * jnp.exp(b_last - b)
        vb = v.astype(BF16)
        k_bd = jnp.where(kk_mask, jnp.concatenate([kt] * GLA_HEADS, axis=0), 0.0).astype(BF16)
        a = (_dot_nt(qt, k_bd) * causal).astype(BF16)
        v_bd = jnp.where(vv_mask, jnp.concatenate([vb] * GLA_HEADS, axis=0), jnp.zeros((), BF16))
        s_bd = jnp.concatenate(
            [jnp.concatenate([s_blocks[h].astype(BF16) if g == h else zero_blk for g in range(GLA_HEADS)], axis=1)
             for h in range(GLA_HEADS)], axis=0)
        outs.append(_dot(qt, s_bd) + _dot(a, v_bd))
        kd_t = jnp.transpose(jnp.concatenate([kd, zpad_k], axis=0)).astype(BF16)
        upd = _dot(kd_t, jnp.concatenate([vb, zpad_v], axis=0))
        decay = jnp.exp(jnp.transpose(jnp.broadcast_to(b_last, (LANES, GLA_QK))))
        s_blocks = [s_blocks[h] * decay[h * GLA_DK:(h + 1) * GLA_DK]
                    + upd[h * GLA_DK:(h + 1) * GLA_DK, h * GLA_DV:(h + 1) * GLA_DV] for h in range(GLA_HEADS)]
        yield
    return jnp.concatenate(outs, axis=0), s_blocks


def _swa_block(sq, kcat, vcat, bias_ref, sinks_ref, valid_t):
    half = _iota((1, LANES), 1) < SWA_HEAD_DIM
    top_rows = _iota((LANES, 1), 0) < SWA_HEAD_DIM
    k_roll = pltpu.roll(kcat, SWA_HEAD_DIM, 1)
    v_t = jnp.transpose(vcat)
    zeros_v = jnp.zeros((SWA_HEAD_DIM, 2 * WINDOW), F32)
    cols = []
    for kv in range(SWA_KV_HEADS):
        kk = jnp.where(half, kcat, k_roll) if kv == 0 else jnp.where(half, k_roll, kcat)
        q_parts = []
        for c in (2 * kv, 2 * kv + 1):
            qc = sq[:, c * LANES:(c + 1) * LANES]
            q_parts.append(jnp.where(half, qc, 0.0))
            q_parts.append(jnp.where(half, 0.0, qc))
        q_st = jnp.concatenate(q_parts, axis=0).astype(BF16)
        s = _dot_nt(kk.astype(BF16), q_st) + bias_ref[kv]
        if valid_t is not None:
            s = jnp.where(valid_t, s, NEG_INF)
        sink = jnp.concatenate(
            [jnp.full((1, WINDOW), sinks_ref[kv * SWA_GROUP + g], F32) for g in range(SWA_GROUP)], axis=1)
        m = jnp.maximum(jnp.max(s, axis=0, keepdims=True), sink)
        pr = jnp.exp(s - m)
        inv = 1.0 / (jnp.sum(pr, axis=0, keepdims=True) + jnp.exp(sink - m))
        pb = pr.astype(BF16)
        vk = v_t[kv * SWA_HEAD_DIM:(kv + 1) * SWA_HEAD_DIM]
        vv_t = jnp.concatenate([jnp.concatenate([vk, zeros_v], axis=1),
                                jnp.concatenate([zeros_v, vk], axis=1)], axis=0).astype(BF16)
        for pair in range(SWA_GROUP // 2):
            ce = slice(2 * pair * WINDOW, (2 * pair + 1) * WINDOW)
            co = slice((2 * pair + 1) * WINDOW, (2 * pair + 2) * WINDOW)
            p2_t = jnp.concatenate([pb[:, ce], pb[:, co]], axis=0)
            o2_t = _dot(vv_t, p2_t)
            o2_t = o2_t * jnp.where(top_rows, inv[:, ce], inv[:, co])
            cols.append(jnp.transpose(o2_t))
        yield
    return jnp.concatenate(cols, axis=1)


def _mixer_kernel(sinks_ref, x_ref, s0_ref, k0_ref, v0_ref, base0_ref, bias_ref,
                  g_mix_ref, w_in_ref, w_a_up_ref, b_a_ref, g_gla_ref, g_swa_ref, w_out_ref,
                  g_ffn_ref, w_r_ref, b_r_ref,
                  xmid_ref, hp_ref, topi_ref, gate_ref, rank_ref, sout_ref, kout_ref, vout_ref, cnt_ref,
                  s_scr, k_scr, v_scr, base_scr, *, n_lead_pad, prev_valid_from):
    g_id, j = pl.program_id(0), pl.program_id(1)
    n_streams, tm = x_ref.shape[1], x_ref.shape[2]

    @pl.when(j == 0)
    def _():
        for s in range(n_streams):
            s_scr[s] = s0_ref[...]
            k_scr[s] = k0_ref[...]
            v_scr[s] = v0_ref[...]

    @pl.when((j == 0) & (g_id == 0))
    def _():
        base_scr[...] = base0_ref[...]

    diag = [(slice(h * GLA_DK, (h + 1) * GLA_DK), slice(h * GLA_DV, (h + 1) * GLA_DV)) for h in range(GLA_HEADS)]

    def stream(s):
        x = x_ref[0, s]
        p = yield from _project_steps(x, g_mix_ref[...], w_in_ref, w_a_up_ref[...], b_a_ref[...])
        yield "mix"
        o_gla, s_blocks = yield from _gla_chunks(p, j * tm, [s_scr[s, r, c] for r, c in diag], n_lead_pad)
        for (r, c), blk in zip(diag, s_blocks):
            s_scr[s, r, c] = blk
        o_parts = []
        for sb in range(tm // WINDOW):
            rs = slice(sb * WINDOW, (sb + 1) * WINDOW)
            k_blk, v_blk = p["sk"][rs], p["sv"][rs]
            k_prev = k_scr[s] if sb == 0 else p["sk"][(sb - 1) * WINDOW:sb * WINDOW]
            v_prev = v_scr[s] if sb == 0 else p["sv"][(sb - 1) * WINDOW:sb * WINDOW]
            valid = None
            if sb == 0 and prev_valid_from:
                first = jnp.where(j == 0, prev_valid_from, 0)
                valid = _iota((2 * WINDOW, SWA_GROUP * WINDOW), 0) >= first
            o_parts.append((yield from _swa_block(
                p["sq"][rs], jnp.concatenate([k_prev, k_blk], axis=0),
                jnp.concatenate([v_prev, v_blk], axis=0), bias_ref, sinks_ref, valid)))
        o_swa = jnp.concatenate(o_parts, axis=0)
        k_scr[s] = p["sk"][tm - WINDOW:tm]
        v_scr[s] = p["sv"][tm - WINDOW:tm]
        yield "tail"
        x_mid, hp, topi, gate8, rank8, base = yield from _tail_steps(
            x, o_gla, p["gr"], o_swa, g_gla_ref[...], g_swa_ref[...], w_out_ref,
            g_ffn_ref[...], w_r_ref[...], b_r_ref[...], lambda: base_scr[...])
        base_scr[...] = base
        xmid_ref[0, s] = x_mid
        _store_slabs(hp_ref.at[0, s], hp)
        topi_ref[0, s] = topi
        gate_ref[0, s] = gate8
        rank_ref[0, s] = rank8
        sout_ref[0, s] = s_scr[s]
        kout_ref[0, s] = p["sk"][tm - WINDOW:tm]
        vout_ref[0, s] = p["sv"][tm - WINDOW:tm]
        cnt_ref[...] = base

    def advance(gen, stop):
        try:
            while next(gen) != stop or stop is None:
                pass
            return False
        except StopIteration:
            return True

    def alternate(gen_a, stop_a, gen_b, stop_b):
        done_a = done_b = False
        while not (done_a and done_b):
            if not done_a:
                try:
                    done_a = next(gen_a) == stop_a and stop_a is not None
                except StopIteration:
                    done_a = True
            if not done_b:
                try:
                    done_b = next(gen_b) == stop_b and stop_b is not None
                except StopIteration:
                    done_b = True

    if n_streams == 1:
        advance(stream(0), None)
    else:
        first, second = stream(0), stream(1)
        advance(first, "mix")
        alternate(first, "tail", second, "mix")
        alternate(first, None, second, "tail")
        advance(second, None)


def _full_spec(shape):
    nd = len(shape)
    return pl.BlockSpec(shape, lambda *_: (0,) * nd)


def _mixer_call(x, s0, k0, v0, base0, wts, tm, n_streams, n_lead_pad, prev_valid_from):
    B, L, _ = x.shape
    S = n_streams
    G = B // S
    nj = L // tm
    weight_args = (wts["bias"], wts["g_mix"], wts["w_in"], wts["w_a_up"], wts["b_a"], wts["g_gla"],
                   wts["g_swa"], wts["w_out"], wts["g_ffn"], wts["w_r"], wts["b_r"])
    in_specs = [
        pl.BlockSpec(memory_space=pltpu.SMEM),
        pl.BlockSpec((1, S, tm, D_MODEL), lambda g, j: (g, 0, j, 0)),
        _full_spec(s0.shape), _full_spec(k0.shape), _full_spec(v0.shape), _full_spec(base0.shape),
    ] + [_full_spec(w.shape) for w in weight_args]
    tok_spec = pl.BlockSpec((1, S, 8, tm), lambda g, j: (g, 0, 0, j))

    def per_seq_spec(rows, cols):
        return pl.BlockSpec((1, S, rows, cols), lambda g, j: (g, 0, 0, 0))

    out_specs = [
        pl.BlockSpec((1, S, tm, D_MODEL), lambda g, j: (g, 0, j, 0)),
        pl.BlockSpec((1, S, tm * SLAB, LANES), lambda g, j: (g, 0, j, 0)),
        tok_spec, tok_spec, tok_spec,
        per_seq_spec(GLA_QK, GLA_V), per_seq_spec(WINDOW, SWA_KV), per_seq_spec(WINDOW, SWA_KV),
        _full_spec((N_EXPERTS, LANES)),
    ]
    out_shape = [
        jax.ShapeDtypeStruct((G, S, L, D_MODEL), F32),
        jax.ShapeDtypeStruct((G, S, L * SLAB, LANES), jnp.int32),
        jax.ShapeDtypeStruct((G, S, 8, L), jnp.int32),
        jax.ShapeDtypeStruct((G, S, 8, L), F32),
        jax.ShapeDtypeStruct((G, S, 8, L), jnp.int32),
        jax.ShapeDtypeStruct((G, S, GLA_QK, GLA_V), F32),
        jax.ShapeDtypeStruct((G, S, WINDOW, SWA_KV), F32),
        jax.ShapeDtypeStruct((G, S, WINDOW, SWA_KV), F32),
        jax.ShapeDtypeStruct((N_EXPERTS, LANES), F32),
    ]
    kern = functools.partial(_mixer_kernel, n_lead_pad=n_lead_pad, prev_valid_from=prev_valid_from)
    xmid, hp, topi, gate, rank, s_out, k_out, v_out, cnt = pl.pallas_call(
        kern,
        grid=(G, nj),
        in_specs=in_specs,
        out_specs=out_specs,
        out_shape=out_shape,
        scratch_shapes=[pltpu.VMEM((S, GLA_QK, GLA_V), F32), pltpu.VMEM((S, WINDOW, SWA_KV), F32),
                        pltpu.VMEM((S, WINDOW, SWA_KV), F32), pltpu.VMEM((N_EXPERTS, LANES), F32)],
        compiler_params=pltpu.CompilerParams(dimension_semantics=("arbitrary", "arbitrary"),
                                             vmem_limit_bytes=VMEM_LIMIT),
        name="mixer",
    )(wts["sinks"], x.reshape(G, S, L, D_MODEL), s0, k0, v0, base0, *weight_args)

    def rows8(a):
        return jnp.transpose(a, (2, 0, 1, 3)).reshape(8, B * L)

    return (xmid.reshape(B, L, D_MODEL), hp.reshape(B * L * SLAB, LANES), rows8(topi), rows8(gate), rows8(rank),
            s_out.reshape(B, GLA_QK, GLA_V), k_out.reshape(B, WINDOW, SWA_KV), v_out.reshape(B, WINDOW, SWA_KV), cnt)


def _decode_kernel(sinks_ref, x_ref, st_ref, ck_ref, cv_ref, base0_ref, bias_ref,
                   g_mix_ref, w_in_ref, w_a_up_ref, b_a_ref, g_gla_ref, g_swa_ref, w_out_ref,
                   g_ffn_ref, w_r_ref, b_r_ref,
                   xmid_ref, hp_ref, topi_ref, gate_ref, rank_ref, sto_ref, cko_ref, cvo_ref, cnt_ref,
                   tq_scr, gv_scr, gr_scr, sq_scr, sk_scr, sv_scr, og_scr, os_scr):
    i = pl.program_id(0)
    n_seq = x_ref.shape[0]

    @pl.when(i == 0)
    def _():
        p = _project(x_ref[...], g_mix_ref[...], w_in_ref, w_a_up_ref[...], b_a_ref[...])
        a_hi, a_mid, a_lo = _split3(jnp.transpose(jnp.exp(p["log_a"])))
        tq_scr[...] = jnp.concatenate(
            [a_hi, a_mid, a_lo, jnp.transpose(p["gk"]).astype(BF16), jnp.transpose(p["gq"]).astype(BF16)], axis=0)
        gv_scr[...] = p["gv"]
        gr_scr[...] = p["gr"]
        sq_scr[...] = p["sq"]
        sk_scr[...] = p["sk"]
        sv_scr[...] = p["sv"]

    seq_row = _iota((n_seq, LANES), 0)
    half = _iota((1, LANES), 1) < SWA_HEAD_DIM
    row_id = _iota((WINDOW, SWA_KV), 0)
    head_diag = (_iota((16, SWA_Q), 1) // SWA_HEAD_DIM) == _iota((16, SWA_Q), 0)
    sink_col = jnp.concatenate(
        [jnp.full((1, 1), sinks_ref[h], F32) for h in range(SWA_HEADS)] + [jnp.zeros((8, 1), F32)], axis=0)

    def per_seq(sl, carry):
        s = i * DEC_SB + sl
        pick = (seq_row == s).astype(BF16)
        cols = _dot(tq_scr[...], pick)
        a_c = cols[0:GLA_QK] + cols[GLA_QK:2 * GLA_QK] + cols[2 * GLA_QK:3 * GLA_QK]
        k_c = cols[3 * GLA_QK:4 * GLA_QK]
        q_c = cols[4 * GLA_QK:5 * GLA_QK]
        st = st_ref[sl].reshape(GLA_QK, GLA_DV)
        v_row = gv_scr[pl.ds(s, 1), :]
        v_b = jnp.concatenate(
            [jnp.broadcast_to(v_row[:, h * GLA_DV:(h + 1) * GLA_DV], (GLA_DK, GLA_DV))
             for h in range(GLA_HEADS)], axis=0)
        st_new = a_c * st + k_c * v_b
        sto_ref[sl] = st_new.reshape(GLA_HEADS, GLA_DK, GLA_DV)
        t = q_c * st_new
        og_scr[pl.ds(s, 1), :] = jnp.concatenate(
            [jnp.sum(t[h * GLA_DK:(h + 1) * GLA_DK], axis=0, keepdims=True) for h in range(GLA_HEADS)],
            axis=1)

        k_new = sk_scr[pl.ds(s, 1), :]
        v_new = sv_scr[pl.ds(s, 1), :]
        kn = jnp.where(row_id == WINDOW - 1, k_new, pltpu.roll(ck_ref[sl], WINDOW - 1, 0))
        vn = jnp.where(row_id == WINDOW - 1, v_new, pltpu.roll(cv_ref[sl], WINDOW - 1, 0))
        cko_ref[sl] = kn
        cvo_ref[sl] = vn
        kr, vr = pltpu.roll(kn, SWA_HEAD_DIM, 1), pltpu.roll(vn, SWA_HEAD_DIM, 1)
        k0, k1 = jnp.where(half, kn, kr), jnp.where(half, kr, kn)
        v0, v1 = jnp.where(half, vn, vr), jnp.where(half, vr, vn)
        kw = jnp.concatenate([k0, k0, k1, k1], axis=1).astype(BF16)
        vw = jnp.concatenate([v0, v0, v1, v1], axis=1).astype(BF16)
        q_row = sq_scr[pl.ds(s, 1), :]
        qm = jnp.where(head_diag, jnp.broadcast_to(q_row, (16, SWA_Q)), 0.0).astype(BF16)
        sc = _dot_nt(qm, kw) + bias_ref[...]
        m = jnp.maximum(jnp.max(sc, axis=1, keepdims=True), sink_col)
        pr = jnp.exp(sc - m)
        inv = 1.0 / (jnp.sum(pr, axis=1, keepdims=True) + jnp.exp(sink_col - m))
        ow = _dot(pr.astype(BF16), vw) * inv
        os_scr[pl.ds(s, 1), :] = jnp.sum(jnp.where(head_diag, ow, 0.0), axis=0, keepdims=True)
        return carry

    lax.fori_loop(0, DEC_SB, per_seq, 0, unroll=8)

    @pl.when(i == pl.num_programs(0) - 1)
    def _():
        x_mid, hp, topi, gate8, rank8, base = _tail(
            x_ref[...], og_scr[...], gr_scr[...], os_scr[...], g_gla_ref[...], g_swa_ref[...],
            w_out_ref, g_ffn_ref[...], w_r_ref[...], b_r_ref[...], lambda: base0_ref[...])
        xmid_ref[...] = x_mid
        _store_slabs(hp_ref, hp)
        topi_ref[...] = topi
        gate_ref[...] = gate8
        rank_ref[...] = rank8
        cnt_ref[...] = base


def _decode_call(xs, state, ck, cv, base0, bias_dec, wts):
    n_seq = xs.shape[0]
    nb = n_seq // DEC_SB
    weight_args = (wts["g_mix"], wts["w_in"], wts["w_a_up"], wts["b_a"], wts["g_gla"],
                   wts["g_swa"], wts["w_out"], wts["g_ffn"], wts["w_r"], wts["b_r"])
    in_specs = [
        pl.BlockSpec(memory_space=pltpu.SMEM),
        _full_spec(xs.shape),
        pl.BlockSpec((DEC_SB, GLA_HEADS, GLA_DK, GLA_DV), lambda i: (i, 0, 0, 0)),
        pl.BlockSpec((DEC_SB, WINDOW, SWA_KV), lambda i: (i, 0, 0)),
        pl.BlockSpec((DEC_SB, WINDOW, SWA_KV), lambda i: (i, 0, 0)),
        _full_spec(base0.shape), _full_spec(bias_dec.shape),
    ] + [_full_spec(w.shape) for w in weight_args]
    out_specs = [
        _full_spec((n_seq, D_MODEL)),
        _full_spec((n_seq * SLAB, LANES)),
        _full_spec((8, n_seq)), _full_spec((8, n_seq)), _full_spec((8, n_seq)),
        pl.BlockSpec((DEC_SB, GLA_HEADS, GLA_DK, GLA_DV), lambda i: (i, 0, 0, 0)),
        pl.BlockSpec((DEC_SB, WINDOW, SWA_KV), lambda i: (i, 0, 0)),
        pl.BlockSpec((DEC_SB, WINDOW, SWA_KV), lambda i: (i, 0, 0)),
        _full_spec((N_EXPERTS, LANES)),
    ]
    out_shape = [
        jax.ShapeDtypeStruct((n_seq, D_MODEL), F32),
        jax.ShapeDtypeStruct((n_seq * SLAB, LANES), jnp.int32),
        jax.ShapeDtypeStruct((8, n_seq), jnp.int32),
        jax.ShapeDtypeStruct((8, n_seq), F32),
        jax.ShapeDtypeStruct((8, n_seq), jnp.int32),
        jax.ShapeDtypeStruct(state.shape, F32),
        jax.ShapeDtypeStruct(ck.shape, F32),
        jax.ShapeDtypeStruct(cv.shape, F32),
        jax.ShapeDtypeStruct((N_EXPERTS, LANES), F32),
    ]
    scratch = [pltpu.VMEM((5 * GLA_QK, n_seq), BF16)] + [
        pltpu.VMEM((n_seq, GLA_V), F32), pltpu.VMEM((n_seq, GLA_V), F32), pltpu.VMEM((n_seq, SWA_Q), F32),
        pltpu.VMEM((n_seq, SWA_KV), F32), pltpu.VMEM((n_seq, SWA_KV), F32),
        pltpu.VMEM((n_seq, GLA_V), F32), pltpu.VMEM((n_seq, SWA_Q), F32)]
    return pl.pallas_call(
        _decode_kernel,
        grid=(nb,),
        in_specs=in_specs,
        out_specs=out_specs,
        out_shape=out_shape,
        scratch_shapes=scratch,
        compiler_params=pltpu.CompilerParams(dimension_semantics=("arbitrary",),
                                             vmem_limit_bytes=VMEM_LIMIT),
        name="decode",
    )(wts["sinks"], xs, state, ck, cv, base0, bias_dec, *weight_args)


SC_CORES = 2
SC_SUBCORES = 16
SC_WORKERS = SC_CORES * SC_SUBCORES
SC_SCATTER_ROWS = 64
SC_GATHER_ROWS = 48


def _sc_mesh():
    return plsc.VectorSubcoreMesh(core_axis_name="c", subcore_axis_name="s")


def _sc_worker_id():
    return lax.axis_index("s") * SC_CORES + lax.axis_index("c")


def _sc_scatter_rows(src_p, src_s, idx_p, idx_s, n_out):
    rows = SC_SCATTER_ROWS
    n_chunks = idx_p.shape[0] // SC_WORKERS
    n_s, _, rows_s = idx_s.shape
    assert n_chunks * SC_WORKERS == idx_p.shape[0] and n_chunks % 2 == 0 and n_s <= SC_WORKERS

    @functools.partial(
        pl.kernel, mesh=_sc_mesh(),
        out_type=jax.ShapeDtypeStruct((n_out, SLAB, LANES), jnp.int32),
        scratch_types=[pltpu.VMEM((2, TOP_K, rows), jnp.int32), pltpu.VMEM((2, rows, SLAB, LANES), jnp.int32),
                       pltpu.VMEM((TOP_K, rows_s), jnp.int32), pltpu.VMEM((rows_s, SLAB, LANES), jnp.int32),
                       pltpu.SemaphoreType.DMA((2,)), pltpu.SemaphoreType.DMA((2,))])
    def scatter_rows(srcp_hbm, srcs_hbm, idxp_hbm, idxs_hbm, out_hbm, idx_v, rows_v, idxs_v, rowss_v, lsem, ssem):
        wid = _sc_worker_id()

        def loads(c, b):
            g = wid * n_chunks + c
            return (pltpu.make_async_copy(idxp_hbm.at[g], idx_v.at[b], lsem.at[b]),
                    pltpu.make_async_copy(srcp_hbm.at[pl.ds(pl.multiple_of(g * rows, 8), rows)], rows_v.at[b],
                                          lsem.at[b]))

        def scatters(b):
            return [pltpu.make_async_copy(rows_v.at[b], out_hbm.at[idx_v.at[b, k]], ssem.at[b])
                    for k in range(TOP_K)]

        for d in loads(0, 0):
            d.start()

        @pl.loop(0, n_chunks, step=2)
        def _(c0):
            for b in range(2):
                c = c0 + b
                for d in loads(c, b):
                    d.wait()

                @pl.when(c >= 1)
                def _():
                    for d in scatters(1 - b):
                        d.wait()

                @pl.when(c + 1 < n_chunks)
                def _():
                    for d in loads(c + 1, 1 - b):
                        d.start()

                for d in scatters(b):
                    d.start()

        for d in scatters((n_chunks - 1) % 2):
            d.wait()

        @pl.when(wid < n_s)
        def _():
            pltpu.sync_copy(idxs_hbm.at[wid], idxs_v)
            pltpu.sync_copy(srcs_hbm.at[pl.ds(pl.multiple_of(wid * rows_s, 8), rows_s)], rowss_v)
            for k in range(TOP_K):
                pltpu.sync_copy(rowss_v, out_hbm.at[idxs_v.at[k]])

    return scatter_rows(src_p, src_s, idx_p, idx_s)


def _sc_gather_rows(src3, idx2):
    rows = SC_GATHER_ROWS
    n_chunks = idx2.shape[0] // SC_WORKERS
    assert n_chunks * SC_WORKERS == idx2.shape[0] and idx2.shape[1] == rows and n_chunks % 2 == 0

    @functools.partial(
        pl.kernel, mesh=_sc_mesh(),
        out_type=jax.ShapeDtypeStruct((idx2.shape[0] * rows, SLAB, LANES), jnp.int32),
        scratch_types=[pltpu.VMEM((2, rows), jnp.int32), pltpu.VMEM((2, rows, SLAB, LANES), jnp.int32),
                       pltpu.SemaphoreType.DMA((2,)), pltpu.SemaphoreType.DMA((2,))])
    def gather_rows(src_hbm, idx_hbm, out_hbm, idx_v, rows_v, gsem, wsem):
        wid = _sc_worker_id()

        def gather(b):
            return pltpu.make_async_copy(src_hbm.at[idx_v.at[b]], rows_v.at[b], gsem.at[b])

        def write(c, b):
            base = pl.multiple_of((wid * n_chunks + c) * rows, 8)
            return pltpu.make_async_copy(rows_v.at[b], out_hbm.at[pl.ds(base, rows)], wsem.at[b])

        pltpu.sync_copy(idx_hbm.at[wid * n_chunks], idx_v.at[0])
        gather(0).start()

        @pl.loop(0, n_chunks, step=2)
        def _(c0):
            for b in range(2):
                c = c0 + b

                @pl.when(c + 1 < n_chunks)
                def _():
                    @pl.when(c >= 1)
                    def _():
                        write(c - 1, 1 - b).wait()
                    pltpu.sync_copy(idx_hbm.at[wid * n_chunks + c + 1], idx_v.at[1 - b])
                    gather(1 - b).start()

                gather(b).wait()
                write(c, b).start()

        write(n_chunks - 2, 0).wait()
        write(n_chunks - 1, 1).wait()

    return gather_rows(src3, idx2)


FF_TILE = 256


def _ffn_kernel(blk_e_ref, nused_ref, x_ref, wu_ref, bu_ref, wd_ref, bd_ref, y_ref, xbf, actbf, wu_bf, wd_bf):
    i = pl.program_id(0)
    tm = MOE_TM
    n_tiles = D_FF // FF_TILE

    @pl.when(i < nused_ref[0])
    def _():
        @pl.when((i == 0) | (blk_e_ref[i] != blk_e_ref[jnp.maximum(i - 1, 0)]))
        def _():
            wu_bf[...] = wu_ref[0].astype(BF16)
            wd_bf[...] = wd_ref[0].astype(BF16)

        for c in range(SLAB):
            lo, hi = _load_slab_chunk(x_ref, tm, c)
            xbf[:, c * LANES:(c + 1) * LANES] = lo.astype(BF16)
            xbf[:, HALF_D + c * LANES:HALF_D + (c + 1) * LANES] = hi.astype(BF16)
        for n in range(n_tiles):
            gc = slice(n * FF_TILE, (n + 1) * FF_TILE)
            lc = slice(D_FF + n * FF_TILE, D_FF + (n + 1) * FF_TILE)
            g = jnp.minimum(_dot(xbf[...], wu_bf[:, gc]) + bu_ref[0, :, gc], SWIGLU_LIMIT)
            lin = jnp.clip(_dot(xbf[...], wu_bf[:, lc]) + bu_ref[0, :, lc], -SWIGLU_LIMIT, SWIGLU_LIMIT)
            actbf[:, gc] = (g * jax.nn.sigmoid(SWIGLU_ALPHA * g) * (lin + 1.0)).astype(BF16)
        per_tile = FF_TILE // LANES
        for n in range(n_tiles // 2):
            yl = slice(n * FF_TILE, (n + 1) * FF_TILE)
            yh = slice(HALF_D + n * FF_TILE, HALF_D + (n + 1) * FF_TILE)
            y_lo = _dot(actbf[...], wd_bf[:, yl]) + bd_ref[0, :, yl]
            y_hi = _dot(actbf[...], wd_bf[:, yh]) + bd_ref[0, :, yh]
            for c in range(per_tile):
                sl = slice(c * LANES, (c + 1) * LANES)
                y_ref[pl.ds(n * per_tile + c, tm, stride=SLAB), :] = _pack_pair(y_lo[:, sl], y_hi[:, sl])

    @pl.when(i >= nused_ref[0])
    def _():
        y_ref[...] = jnp.zeros_like(y_ref)


def _ffn_call(blk_e, nused, xs2, w_up, b_up, w_down, b_down):
    n_blocks = blk_e.shape[0]
    tm = MOE_TM
    row_blk = pl.BlockSpec((tm * SLAB, LANES), lambda i, be, nu: (i, 0))
    grid_spec = pltpu.PrefetchScalarGridSpec(
        num_scalar_prefetch=2,
        grid=(n_blocks,),
        in_specs=[
            row_blk,
            pl.BlockSpec((1, D_MODEL, 2 * D_FF), lambda i, be, nu: (be[i], 0, 0)),
            pl.BlockSpec((1, 1, 2 * D_FF), lambda i, be, nu: (be[i], 0, 0)),
            pl.BlockSpec((1, D_FF, D_MODEL), lambda i, be, nu: (be[i], 0, 0)),
            pl.BlockSpec((1, 1, D_MODEL), lambda i, be, nu: (be[i], 0, 0)),
        ],
        out_specs=row_blk,
        scratch_shapes=[pltpu.VMEM((tm, D_MODEL), BF16), pltpu.VMEM((tm, D_FF), BF16),
                        pltpu.VMEM((D_MODEL, 2 * D_FF), BF16), pltpu.VMEM((D_FF, D_MODEL), BF16)],
    )
    return pl.pallas_call(
        _ffn_kernel,
        grid_spec=grid_spec,
        out_shape=jax.ShapeDtypeStruct((n_blocks * tm * SLAB, LANES), jnp.int32),
        compiler_params=pltpu.CompilerParams(dimension_semantics=("arbitrary",),
                                             vmem_limit_bytes=VMEM_LIMIT),
        name="experts",
    )(blk_e, nused, xs2, w_up, b_up.reshape(N_EXPERTS, 1, 2 * D_FF), w_down, b_down.reshape(N_EXPERTS, 1, D_MODEL))


def _combine_kernel(ys0_ref, ys1_ref, ys2_ref, ys3_ref, xmid_ref, gate_ref, g_final_ref, y_ref):
    tm = xmid_ref.shape[0]
    gts = jnp.transpose(jnp.concatenate([gate_ref[...], jnp.zeros((LANES - 8, tm), F32)], axis=0))
    lows, highs = [], []
    for c in range(SLAB):
        acc_lo = xmid_ref[:, c * LANES:(c + 1) * LANES]
        acc_hi = xmid_ref[:, HALF_D + c * LANES:HALF_D + (c + 1) * LANES]
        for k, ys_ref in enumerate((ys0_ref, ys1_ref, ys2_ref, ys3_ref)):
            lo, hi = _load_slab_chunk(ys_ref, tm, c)
            acc_lo = acc_lo + lo * gts[:, k:k + 1]
            acc_hi = acc_hi + hi * gts[:, k:k + 1]
        lows.append(acc_lo)
        highs.append(acc_hi)
    y_ref[...] = _rms(jnp.concatenate(lows + highs, axis=1), g_final_ref[...])


def _combine_call(ys4, t_stride, row0, x_mid, gates, g_final, tm):
    T = x_mid.shape[0]
    blk0 = row0 // tm
    per_k = t_stride // tm
    assert per_k * tm == t_stride and blk0 * tm == row0

    def ys_spec(k):
        return pl.BlockSpec((tm * SLAB, LANES), lambda i: (k * per_k + blk0 + i, 0))

    return pl.pallas_call(
        _combine_kernel,
        grid=(T // tm,),
        in_specs=[
            ys_spec(0), ys_spec(1), ys_spec(2), ys_spec(3),
            pl.BlockSpec((tm, D_MODEL), lambda i: (i, 0)),
            pl.BlockSpec((8, tm), lambda i: (0, i)),
            _full_spec((1, D_MODEL)),
        ],
        out_specs=pl.BlockSpec((tm, D_MODEL), lambda i: (i, 0)),
        out_shape=jax.ShapeDtypeStruct((T, D_MODEL), F32),
        compiler_params=pltpu.CompilerParams(dimension_semantics=("arbitrary",),
                                             vmem_limit_bytes=VMEM_LIMIT),
        name="combine",
    )(ys4, ys4, ys4, ys4, x_mid, gates, g_final)


def _t5_bucket(dist):
    n = jnp.maximum(dist, 0)
    max_exact = NUM_BUCKETS // 2
    nf = jnp.maximum(n, 1).astype(F32)
    large = max_exact + (jnp.log(nf / max_exact) / math.log(MAX_DISTANCE / max_exact)
                         * (NUM_BUCKETS - max_exact)).astype(jnp.int32)
    large = jnp.minimum(large, NUM_BUCKETS - 1)
    return jnp.where(n < max_exact, n, large)


def kernel(x_prompt, x_sample, state_gla, cache_swa_k, cache_swa_v, meta_tokens, rel_bias_table,
           g_mix, w_in, w_a_up, b_a, g_gla_out, g_swa_out, attn_sinks, w_out,
           g_ffn, w_router, b_router, w_up, b_up, w_down, b_down, g_final):
    assert g_mix.shape[0] == 1, "single-layer trunk"
    B, L, _ = x_prompt.shape
    n_seq = x_sample.shape[0]
    TP = B * L
    T_all = TP + n_seq

    wi = w_in[0]
    sizes = (GLA_QK, GLA_QK, GLA_V, GLA_V, GLA_LOWRANK, SWA_Q, SWA_KV, SWA_KV)
    offs = [0]
    for s in sizes:
        offs.append(offs[-1] + s)
    seg = [wi[:, offs[n]:offs[n + 1]] for n in range(8)]
    seg[0] = seg[0] * (GLA_DK ** -0.5)
    seg[5] = seg[5] * (SWA_HEAD_DIM ** -0.5)
    w_in_r = jnp.concatenate(
        seg[0:4] + seg[5:8] + [seg[4], jnp.zeros((D_MODEL, LANES - GLA_LOWRANK), F32)], axis=1).astype(BF16)
    w_a_pad = jnp.concatenate([w_a_up[0], jnp.zeros((LANES - GLA_LOWRANK, GLA_QK), F32)], axis=0).astype(BF16)
    wr_t = jnp.transpose(w_router[0])
    wr_hi = wr_t.astype(BF16)
    wr_lo = (wr_t - wr_hi.astype(F32)).astype(BF16)
    qi = jnp.arange(WINDOW)[:, None]
    kj = jnp.arange(2 * WINDOW)[None, :]
    buckets = jnp.arange(NUM_BUCKETS)
    table = rel_bias_table.astype(F32)
    oh_p = (_t5_bucket(qi - kj + WINDOW)[..., None] == buckets).astype(F32)
    bias_p = jnp.einsum("qkb,bh->hkq", oh_p, table, precision=lax.Precision.HIGHEST)
    in_window = jnp.transpose((kj > qi) & (kj <= qi + WINDOW))
    bias_p = jnp.where(in_window[None], bias_p, NEG_INF)
    bias_p = bias_p.reshape(SWA_KV_HEADS, SWA_GROUP, 2 * WINDOW, WINDOW).transpose(0, 2, 1, 3)
    bias_p = bias_p.reshape(SWA_KV_HEADS, 2 * WINDOW, SWA_GROUP * WINDOW)
    oh_d = (_t5_bucket(WINDOW - 1 - jnp.arange(WINDOW))[:, None] == buckets).astype(F32)
    bias_d = jnp.einsum("rb,bh->hr", oh_d, table, precision=lax.Precision.HIGHEST)
    bias_d = jnp.concatenate([bias_d, jnp.zeros((8, WINDOW), F32)], axis=0)
    wts = dict(
        sinks=attn_sinks[0].astype(F32), bias=bias_p,
        g_mix=g_mix[0][None], w_in=w_in_r, w_a_up=w_a_pad, b_a=b_a[0][None],
        g_gla=g_gla_out[0][None], g_swa=g_swa_out[0][None], w_out=w_out[0].astype(BF16),
        g_ffn=g_ffn[0][None], w_r=jnp.concatenate([wr_hi, wr_lo], axis=0), b_r=b_router[0][:, None],
    )

    x_pre = jnp.concatenate([jnp.zeros((WINDOW - N_META, D_MODEL), F32), meta_tokens.astype(F32)], axis=0)[None]
    zeros_s = jnp.zeros((GLA_QK, GLA_V), F32)
    zeros_kv = jnp.zeros((WINDOW, SWA_KV), F32)
    zeros_b = jnp.zeros((N_EXPERTS, LANES), F32)
    pre = _mixer_call(x_pre, zeros_s, zeros_kv, zeros_kv, zeros_b, wts, WINDOW, 1, WINDOW - N_META, 0)
    s_meta, k_meta, v_meta = pre[5][0], pre[6][0], pre[7][0]

    (xmid_p, hp_p, topi_p, gate_p, rank_p, s_p, k_p, v_p, cnt_p) = _mixer_call(
        x_prompt, s_meta, k_meta, v_meta, zeros_b, wts, MIX_STREAM_TM, 2, 0, WINDOW - N_META)

    (xmid_s, hp_s, topi_s, gate_s, rank_s, st_s, ck_s, cv_s, cnt_all) = _decode_call(
        x_sample[:, 0], state_gla[0], cache_swa_k[0].reshape(n_seq, WINDOW, SWA_KV),
        cache_swa_v[0].reshape(n_seq, WINDOW, SWA_KV), cnt_p, bias_d, wts)

    tm = MOE_TM
    n_slots = T_all * TOP_K
    n_blocks = -(-n_slots // tm) + N_EXPERTS
    top_e = jnp.concatenate([topi_p[:TOP_K], topi_s[:TOP_K]], axis=1)
    rank = jnp.concatenate([rank_p[:TOP_K], rank_s[:TOP_K]], axis=1)
    counts = cnt_all[:, 0].astype(jnp.int32)
    padded = (counts + tm - 1) // tm * tm
    pad_end = jnp.cumsum(padded)
    pad_start = pad_end - padded
    e_ids = jnp.arange(N_EXPERTS, dtype=jnp.int32)
    dest = jnp.sum(jnp.where(top_e[..., None] == e_ids, pad_start, 0), axis=-1) + rank
    n_pad = n_blocks * tm
    blk_e = jnp.minimum(jnp.sum(pad_end[None] <= (jnp.arange(n_blocks, dtype=jnp.int32) * tm)[:, None], axis=1),
                        N_EXPERTS - 1).astype(jnp.int32)
    nused = (pad_end[-1] // tm).astype(jnp.int32).reshape(1)

    sample_rows = 8
    idx_p = dest[:, :TP].reshape(TOP_K, TP // SC_SCATTER_ROWS, SC_SCATTER_ROWS).transpose(1, 0, 2)
    idx_s = dest[:, TP:].reshape(TOP_K, n_seq // sample_rows, sample_rows).transpose(1, 0, 2)
    xs3 = _sc_scatter_rows(hp_p.reshape(TP, SLAB, LANES), hp_s.reshape(n_seq, SLAB, LANES), idx_p, idx_s, n_pad)
    ys2 = _ffn_call(blk_e, nused, xs3.reshape(-1, LANES), w_up[0], b_up[0], w_down[0], b_down[0])
    unit = math.lcm(2 * SC_WORKERS * SC_GATHER_ROWS // TOP_K, MIX_TM)
    t_stride = -(-T_all // unit) * unit
    filler = jnp.arange(TOP_K * (t_stride - T_all), dtype=jnp.int32).reshape(TOP_K, t_stride - T_all)
    slot_src = jnp.concatenate([dest, filler], axis=1)
    slot_src = slot_src.reshape(TOP_K * t_stride // SC_GATHER_ROWS, SC_GATHER_ROWS)
    ys4 = _sc_gather_rows(ys2.reshape(-1, SLAB, LANES), slot_src).reshape(-1, LANES)

    gf = g_final[None]
    y_p = _combine_call(ys4, t_stride, 0, xmid_p.reshape(TP, D_MODEL), gate_p, gf, MIX_TM)
    y_s = _combine_call(ys4, t_stride, TP, xmid_s, gate_s, gf, n_seq)

    s_heads = jnp.stack([s_p[:, h * GLA_DK:(h + 1) * GLA_DK, h * GLA_DV:(h + 1) * GLA_DV]
                         for h in range(GLA_HEADS)], axis=1)
    return (y_p.reshape(B, L, D_MODEL), y_s.reshape(n_seq, 1, D_MODEL), s_heads[None],
            k_p.reshape(1, B, WINDOW, SWA_KV_HEADS, SWA_HEAD_DIM),
            v_p.reshape(1, B, WINDOW, SWA_KV_HEADS, SWA_HEAD_DIM),
            st_s[None], ck_s.reshape(1, n_seq, WINDOW, SWA_KV_HEADS, SWA_HEAD_DIM),
            cv_s.reshape(1, n_seq, WINDOW, SWA_KV_HEADS, SWA_HEAD_DIM))
```

```python
import functools
import math

import jax
import jax.numpy as jnp
from jax import lax
from jax.experimental import pallas as pl
from jax.experimental.pallas import tpu as pltpu
from jax.experimental.pallas import tpu_sc as plsc

D_MODEL = 1024
N_META = 16
GLA_HEADS = 4
GLA_DK = 64
GLA_DV = 128
GLA_LOWRANK = 16
GLA_GATE_TAU = 16.0
GLA_CHUNK = 64
SWA_HEADS = 8
SWA_KV_HEADS = 2
SWA_HEAD_DIM = 64
SWA_GROUP = SWA_HEADS // SWA_KV_HEADS
WINDOW = 128
NUM_BUCKETS = 32
MAX_DISTANCE = 128
N_EXPERTS = 32
TOP_K = 4
D_FF = 1024
SWIGLU_ALPHA = 1.702
SWIGLU_LIMIT = 7.0
RMS_EPS = 1e-6

GLA_QK = GLA_HEADS * GLA_DK
GLA_V = GLA_HEADS * GLA_DV
SWA_Q = SWA_HEADS * SWA_HEAD_DIM
SWA_KV = SWA_KV_HEADS * SWA_HEAD_DIM
LANES = 128
C_GQ, C_GK, C_GV, C_GR = 0, GLA_QK, 2 * GLA_QK, 2 * GLA_QK + GLA_V
C_SQ = C_GR + GLA_V
C_SK = C_SQ + SWA_Q
C_SV = C_SK + SWA_KV
C_GA = C_SV + SWA_KV
D_PROJ = C_GA + LANES

MIX_TM = 512
MIX_STREAM_TM = 512
MOE_BLOCKS_PER_EXPERT = 4
DEC_SB = 16
DEC_UNROLL = 8
VMEM_LIMIT = 56 * 1024 * 1024

F32 = jnp.float32
BF16 = jnp.bfloat16
NEG_INF = float("-inf")


def _dot(a, b):
    return jnp.dot(a, b, preferred_element_type=F32)


def _dot_nt(a, b):
    return lax.dot_general(a, b, (((1,), (1,)), ((), ())), preferred_element_type=F32)


def _split3(x):
    hi = x.astype(BF16)
    r1 = x - hi.astype(F32)
    mid = r1.astype(BF16)
    lo = (r1 - mid.astype(F32)).astype(BF16)
    return hi, mid, lo


def _rms(x, g):
    return x * lax.rsqrt(jnp.mean(x * x, axis=-1, keepdims=True) + RMS_EPS) * g


def _iota(shape, dim):
    return lax.broadcasted_iota(jnp.int32, shape, dim)


HALF_D = D_MODEL // 2
SLAB = HALF_D // LANES


def _pack_pair(lo, hi):
    bl = lax.bitcast_convert_type(lo.astype(BF16).astype(F32), jnp.uint32)
    bh = lax.bitcast_convert_type(hi.astype(BF16).astype(F32), jnp.uint32)
    return lax.bitcast_convert_type(bh | lax.shift_right_logical(bl, jnp.uint32(16)), jnp.int32)


def _unpack_pair(w):
    u = lax.bitcast_convert_type(w, jnp.uint32)
    lo = lax.bitcast_convert_type(lax.shift_left(u, jnp.uint32(16)), F32)
    hi = lax.bitcast_convert_type(u & jnp.uint32(0xFFFF0000), F32)
    return lo, hi


def _store_slabs(ref, x):
    rows = x.shape[0]
    for c in range(SLAB):
        sl = slice(c * LANES, (c + 1) * LANES)
        ref[pl.ds(c, rows, stride=SLAB), :] = _pack_pair(x[:, sl], x[:, HALF_D + c * LANES:HALF_D + (c + 1) * LANES])


def _load_slab_chunk(ref, rows, c):
    return _unpack_pair(ref[pl.ds(c, rows, stride=SLAB), :])


def _drain(steps):
    try:
        while True:
            next(steps)
    except StopIteration as done:
        return done.value


def _project_steps(x, g_mix, w_in_ref, w_a_up, b_a):
    h = _rms(x, g_mix).astype(BF16)
    yield

    def cols(lo, width, tile=2 * LANES):
        parts = []
        for off in range(0, width, tile):
            parts.append(_dot(h, w_in_ref[:, lo + off:lo + min(off + tile, width)]))
            yield
        return parts[0] if len(parts) == 1 else jnp.concatenate(parts, axis=1)

    ga = (yield from cols(C_GA, LANES)).astype(BF16)
    z = _dot(ga, w_a_up) + b_a
    log_a = -(jnp.maximum(-z, 0.0) + jnp.log(1.0 + jnp.exp(-jnp.abs(z)))) / GLA_GATE_TAU
    gqk = yield from cols(C_GQ, 2 * GLA_QK)
    gv = yield from cols(C_GV, GLA_V)
    swa = yield from cols(C_SQ, SWA_Q + 2 * SWA_KV)
    gr = yield from cols(C_GR, GLA_V)
    return dict(
        gq=gqk[:, :GLA_QK],
        gk=gqk[:, GLA_QK:],
        gv=gv,
        gr=gr,
        sq=swa[:, :SWA_Q],
        sk=swa[:, SWA_Q:SWA_Q + SWA_KV],
        sv=swa[:, SWA_Q + SWA_KV:],
        log_a=log_a,
    )


def _project(x, g_mix, w_in_ref, w_a_up, b_a):
    return _drain(_project_steps(x, g_mix, w_in_ref, w_a_up, b_a))


def _tail(*args):
    return _drain(_tail_steps(*args))


def _tail_steps(x, o_gla, gr, o_swa, g_gla_out, g_swa_out, w_out_ref, g_ffn, w_r, b_r, get_base):
    tm = x.shape[0]
    gate = gr * jax.nn.sigmoid(gr)
    parts = []
    for h in range(GLA_HEADS):
        sl = slice(h * GLA_DV, (h + 1) * GLA_DV)
        parts.append(_rms(o_gla[:, sl], g_gla_out) * gate[:, sl])
    og = jnp.concatenate(parts, axis=1).astype(BF16)
    yield
    os_ = _rms(o_swa, g_swa_out).astype(BF16)
    x_mid = x + _dot(og, w_out_ref[0:GLA_V])
    yield
    x_mid = x_mid + _dot(os_, w_out_ref[GLA_V:GLA_V + SWA_Q])
    yield
    hp = _rms(x_mid, g_ffn)

    h1 = hp.astype(BF16)
    h2 = (hp - h1.astype(F32)).astype(BF16)
    yield
    la = _dot_nt(w_r, h1)
    lb = _dot_nt(w_r[0:N_EXPERTS], h2)
    logits = la[0:N_EXPERTS] + la[N_EXPERTS:2 * N_EXPERTS] + lb + b_r
    yield

    eidx = _iota((N_EXPERTS, tm), 0)
    vals, idxs, onehots = [], [], []
    l = logits
    for _ in range(TOP_K):
        m = jnp.max(l, axis=0, keepdims=True)
        sel = jnp.min(jnp.where(l == m, eidx, N_EXPERTS), axis=0, keepdims=True)
        oh = eidx == sel
        l = jnp.where(oh, NEG_INF, l)
        vals.append(m)
        idxs.append(sel)
        onehots.append(oh)
    es = [jnp.exp(v - vals[0]) for v in vals]
    denom = es[0] + es[1] + es[2] + es[3]
    gates = [e / denom for e in es]

    ohf = jnp.concatenate([oh.astype(F32) for oh in onehots], axis=0)
    upper = (_iota((tm, tm), 0) < _iota((tm, tm), 1)).astype(BF16)
    prefix = _dot(ohf.astype(BF16), upper)
    yield
    base = get_base()
    ranks = []
    for k in range(TOP_K):
        sl = slice(k * N_EXPERTS, (k + 1) * N_EXPERTS)
        ohk = ohf[sl]
        base_t = jnp.concatenate([base] * (tm // LANES), axis=1)
        ranks.append(jnp.sum(ohk * (prefix[sl] + base_t), axis=0, keepdims=True))
        base = base + jnp.sum(ohk, axis=1, keepdims=True)
    zi = jnp.zeros((8 - TOP_K, tm), jnp.int32)
    zf = jnp.zeros((8 - TOP_K, tm), F32)
    topi = jnp.concatenate(idxs + [zi], axis=0)
    gate8 = jnp.concatenate(gates + [zf], axis=0)
    rank8 = jnp.concatenate([r.astype(jnp.int32) for r in ranks] + [zi], axis=0)
    return x_mid, hp, topi, gate8, rank8, base


def _gla_chunks(p, row0, s_blocks, n_lead_pad):
    tm = p["gq"].shape[0]
    nch = tm // GLA_CHUNK
    log_a = p["log_a"]
    if n_lead_pad:
        rows = row0 + _iota((tm, GLA_QK), 0)
        log_a = jnp.where(rows >= n_lead_pad, log_a, 0.0)
    ri, ci = _iota((tm, tm), 0), _iota((tm, tm), 1)
    tril = ((ri >= ci) & (ri // GLA_CHUNK == ci // GLA_CHUNK)).astype(BF16)
    hi, mid, lo = _split3(log_a)
    b_all = _dot(tril, hi) + _dot(tril, mid) + _dot(tril, lo)
    yield

    c64 = GLA_CHUNK
    kk_mask = (_iota((GLA_QK, GLA_QK), 0) // c64) == (_iota((GLA_QK, GLA_QK), 1) // GLA_DK)
    vv_mask = (_iota((GLA_QK, GLA_V), 0) // c64) == (_iota((GLA_QK, GLA_V), 1) // GLA_DV)
    zero_blk = jnp.zeros((GLA_DK, GLA_DV), BF16)
    causal = (_iota((c64, GLA_QK), 0) >= (_iota((c64, GLA_QK), 1) % c64)).astype(F32)
    zpad_k = jnp.zeros((LANES - c64, GLA_QK), F32)
    zpad_v = jnp.zeros((LANES - c64, GLA_V), BF16)

    outs = []
    for c in range(nch):
        rs = slice(c * c64, (c + 1) * c64)
        b = b_all[rs]
        q, k, v = p["gq"][rs], p["gk"][rs], p["gv"][rs]
        b_last = b[c64 - 1:c64]
        qt = (q * jnp.exp(b)).astype(BF16)
        kt = k * jnp.exp(-b)
        kd = k * jnp.exp(b_last - b)
        vb = v.astype(BF16)
        k_bd = jnp.where(kk_mask, jnp.concatenate([kt] * GLA_HEADS, axis=0), 0.0).astype(BF16)
        a = (_dot_nt(qt, k_bd) * causal).astype(BF16)
        v_bd = jnp.where(vv_mask, jnp.concatenate([vb] * GLA_HEADS, axis=0), jnp.zeros((), BF16))
        s_bd = jnp.concatenate(
            [jnp.concatenate([s_blocks[h].astype(BF16) if g == h else zero_blk for g in range(GLA_HEADS)], axis=1)
             for h in range(GLA_HEADS)], axis=0)
        outs.append(_dot(qt, s_bd) + _dot(a, v_bd))
        kd_t = jnp.transpose(jnp.concatenate([kd, zpad_k], axis=0)).astype(BF16)
        upd = _dot(kd_t, jnp.concatenate([vb, zpad_v], axis=0))
        decay = jnp.exp(jnp.transpose(jnp.broadcast_to(b_last, (LANES, GLA_QK))))
        s_blocks = [s_blocks[h] * decay[h * GLA_DK:(h + 1) * GLA_DK]
                    + upd[h * GLA_DK:(h + 1) * GLA_DK, h * GLA_DV:(h + 1) * GLA_DV] for h in range(GLA_HEADS)]
        yield
    return jnp.concatenate(outs, axis=0), s_blocks


def _swa_block(sq, kcat, vcat, bias_ref, sinks_ref, valid_t):
    half = _iota((1, LANES), 1) < SWA_HEAD_DIM
    top_rows = _iota((LANES, 1), 0) < SWA_HEAD_DIM
    k_roll = pltpu.roll(kcat, SWA_HEAD_DIM, 1)
    v_t = jnp.transpose(vcat)
    zeros_v = jnp.zeros((SWA_HEAD_DIM, 2 * WINDOW), F32)
    cols = []
    for kv in range(SWA_KV_HEADS):
        kk = jnp.where(half, kcat, k_roll) if kv == 0 else jnp.where(half, k_roll, kcat)
        q_parts = []
        for c in (2 * kv, 2 * kv + 1):
            qc = sq[:, c * LANES:(c + 1) * LANES]
            q_parts.append(jnp.where(half, qc, 0.0))
            q_parts.append(jnp.where(half, 0.0, qc))
        q_st = jnp.concatenate(q_parts, axis=0).astype(BF16)
        s = _dot_nt(kk.astype(BF16), q_st) + bias_ref[kv]
        if valid_t is not None:
            s = jnp.where(valid_t, s, NEG_INF)
        sink = jnp.concatenate(
            [jnp.full((1, WINDOW), sinks_ref[kv * SWA_GROUP + g], F32) for g in range(SWA_GROUP)], axis=1)
        m = jnp.maximum(jnp.max(s, axis=0, keepdims=True), sink)
        pr = jnp.exp(s - m)
        inv = 1.0 / (jnp.sum(pr, axis=0, keepdims=True) + jnp.exp(sink - m))
        pb = pr.astype(BF16)
        vk = v_t[kv * SWA_HEAD_DIM:(kv + 1) * SWA_HEAD_DIM]
        vv_t = jnp.concatenate([jnp.concatenate([vk, zeros_v], axis=1),
                                jnp.concatenate([zeros_v, vk], axis=1)], axis=0).astype(BF16)
        for pair in range(SWA_GROUP // 2):
            ce = slice(2 * pair * WINDOW, (2 * pair + 1) * WINDOW)
            co = slice((2 * pair + 1) * WINDOW, (2 * pair + 2) * WINDOW)
            p2_t = jnp.concatenate([pb[:, ce], pb[:, co]], axis=0)
            o2_t = _dot(vv_t, p2_t)
            o2_t = o2_t * jnp.where(top_rows, inv[:, ce], inv[:, co])
            cols.append(jnp.transpose(o2_t))
        yield
    return jnp.concatenate(cols, axis=1)


def _mixer_kernel(sinks_ref, x_ref, s0_ref, k0_ref, v0_ref, base0_ref, bias_ref,
                  g_mix_ref, w_in_ref, w_a_up_ref, b_a_ref, g_gla_ref, g_swa_ref, w_out_ref,
                  g_ffn_ref, w_r_ref, b_r_ref,
                  xmid_ref, hp_ref, topi_ref, gate_ref, rank_ref, sout_ref, kout_ref, vout_ref, cnt_ref,
                  s_scr, k_scr, v_scr, base_scr, *, n_lead_pad, prev_valid_from):
    g_id, j = pl.program_id(0), pl.program_id(1)
    n_streams, tm = x_ref.shape[1], x_ref.shape[2]

    @pl.when(j == 0)
    def _():
        for s in range(n_streams):
            s_scr[s] = s0_ref[...]
            k_scr[s] = k0_ref[...]
            v_scr[s] = v0_ref[...]

    @pl.when((j == 0) & (g_id == 0))
    def _():
        base_scr[...] = base0_ref[...]

    diag = [(slice(h * GLA_DK, (h + 1) * GLA_DK), slice(h * GLA_DV, (h + 1) * GLA_DV)) for h in range(GLA_HEADS)]

    def stream(s):
        x = x_ref[0, s]
        p = yield from _project_steps(x, g_mix_ref[...], w_in_ref, w_a_up_ref[...], b_a_ref[...])
        yield "mix"
        o_gla, s_blocks = yield from _gla_chunks(p, j * tm, [s_scr[s, r, c] for r, c in diag], n_lead_pad)
        for (r, c), blk in zip(diag, s_blocks):
            s_scr[s, r, c] = blk
        o_parts = []
        for sb in range(tm // WINDOW):
            rs = slice(sb * WINDOW, (sb + 1) * WINDOW)
            k_blk, v_blk = p["sk"][rs], p["sv"][rs]
            k_prev = k_scr[s] if sb == 0 else p["sk"][(sb - 1) * WINDOW:sb * WINDOW]
            v_prev = v_scr[s] if sb == 0 else p["sv"][(sb - 1) * WINDOW:sb * WINDOW]
            valid = None
            if sb == 0 and prev_valid_from:
                first = jnp.where(j == 0, prev_valid_from, 0)
                valid = _iota((2 * WINDOW, SWA_GROUP * WINDOW), 0) >= first
            o_parts.append((yield from _swa_block(
                p["sq"][rs], jnp.concatenate([k_prev, k_blk], axis=0),
                jnp.concatenate([v_prev, v_blk], axis=0), bias_ref, sinks_ref, valid)))
        o_swa = jnp.concatenate(o_parts, axis=0)
        k_scr[s] = p["sk"][tm - WINDOW:tm]
        v_scr[s] = p["sv"][tm - WINDOW:tm]
        yield "tail"
        x_mid, hp, topi, gate8, rank8, base = yield from _tail_steps(
            x, o_gla, p["gr"], o_swa, g_gla_ref[...], g_swa_ref[...], w_out_ref,
            g_ffn_ref[...], w_r_ref[...], b_r_ref[...], lambda: base_scr[...])
        base_scr[...] = base
        xmid_ref[0, s] = x_mid
        _store_slabs(hp_ref.at[0, s], hp)
        topi_ref[0, s] = topi
        gate_ref[0, s] = gate8
        rank_ref[0, s] = rank8
        sout_ref[0, s] = s_scr[s]
        kout_ref[0, s] = p["sk"][tm - WINDOW:tm]
        vout_ref[0, s] = p["sv"][tm - WINDOW:tm]
        cnt_ref[...] = base

    def advance(gen, stop):
        try:
            while next(gen) != stop or stop is None:
                pass
            return False
        except StopIteration:
            return True

    def alternate(gen_a, stop_a, gen_b, stop_b):
        done_a = done_b = False
        while not (done_a and done_b):
            if not done_a:
                try:
                    done_a = next(gen_a) == stop_a and stop_a is not None
                except StopIteration:
                    done_a = True
            if not done_b:
                try:
                    done_b = next(gen_b) == stop_b and stop_b is not None
                except StopIteration:
                    done_b = True

    if n_streams == 1:
        advance(stream(0), None)
    else:
        first, second = stream(0), stream(1)
        advance(first, "mix")
        alternate(first, "tail", second, "mix")
        alternate(first, None, second, "tail")
        advance(second, None)


def _full_spec(shape):
    nd = len(shape)
    return pl.BlockSpec(shape, lambda *_: (0,) * nd)


def _mixer_call(x, s0, k0, v0, base0, wts, tm, n_streams, n_lead_pad, prev_valid_from):
    B, L, _ = x.shape
    S = n_streams
    G = B // S
    nj = L // tm
    weight_args = (wts["bias"], wts["g_mix"], wts["w_in"], wts["w_a_up"], wts["b_a"], wts["g_gla"],
                   wts["g_swa"], wts["w_out"], wts["g_ffn"], wts["w_r"], wts["b_r"])
    in_specs = [
        pl.BlockSpec(memory_space=pltpu.SMEM),
        pl.BlockSpec((1, S, tm, D_MODEL), lambda g, j: (g, 0, j, 0)),
        _full_spec(s0.shape), _full_spec(k0.shape), _full_spec(v0.shape), _full_spec(base0.shape),
    ] + [_full_spec(w.shape) for w in weight_args]
    tok_spec = pl.BlockSpec((1, S, 8, tm), lambda g, j: (g, 0, 0, j))

    def per_seq_spec(rows, cols):
        return pl.BlockSpec((1, S, rows, cols), lambda g, j: (g, 0, 0, 0))

    out_specs = [
        pl.BlockSpec((1, S, tm, D_MODEL), lambda g, j: (g, 0, j, 0)),
        pl.BlockSpec((1, S, tm * SLAB, LANES), lambda g, j: (g, 0, j, 0)),
        tok_spec, tok_spec, tok_spec,
        per_seq_spec(GLA_QK, GLA_V), per_seq_spec(WINDOW, SWA_KV), per_seq_spec(WINDOW, SWA_KV),
        _full_spec((N_EXPERTS, LANES)),
    ]
    out_shape = [
        jax.ShapeDtypeStruct((G, S, L, D_MODEL), F32),
        jax.ShapeDtypeStruct((G, S, L * SLAB, LANES), jnp.int32),
        jax.ShapeDtypeStruct((G, S, 8, L), jnp.int32),
        jax.ShapeDtypeStruct((G, S, 8, L), F32),
        jax.ShapeDtypeStruct((G, S, 8, L), jnp.int32),
        jax.ShapeDtypeStruct((G, S, GLA_QK, GLA_V), F32),
        jax.ShapeDtypeStruct((G, S, WINDOW, SWA_KV), F32),
        jax.ShapeDtypeStruct((G, S, WINDOW, SWA_KV), F32),
        jax.ShapeDtypeStruct((N_EXPERTS, LANES), F32),
    ]
    kern = functools.partial(_mixer_kernel, n_lead_pad=n_lead_pad, prev_valid_from=prev_valid_from)
    xmid, hp, topi, gate, rank, s_out, k_out, v_out, cnt = pl.pallas_call(
        kern,
        grid=(G, nj),
        in_specs=in_specs,
        out_specs=out_specs,
        out_shape=out_shape,
        scratch_shapes=[pltpu.VMEM((S, GLA_QK, GLA_V), F32), pltpu.VMEM((S, WINDOW, SWA_KV), F32),
                        pltpu.VMEM((S, WINDOW, SWA_KV), F32), pltpu.VMEM((N_EXPERTS, LANES), F32)],
        compiler_params=pltpu.CompilerParams(dimension_semantics=("arbitrary", "arbitrary"),
                                             vmem_limit_bytes=VMEM_LIMIT),
        name="mixer",
    )(wts["sinks"], x.reshape(G, S, L, D_MODEL), s0, k0, v0, base0, *weight_args)

    def rows8(a):
        return jnp.transpose(a, (2, 0, 1, 3)).reshape(8, B * L)

    return (xmid.reshape(B, L, D_MODEL), hp.reshape(B * L * SLAB, LANES), rows8(topi), rows8(gate), rows8(rank),
            s_out.reshape(B, GLA_QK, GLA_V), k_out.reshape(B, WINDOW, SWA_KV), v_out.reshape(B, WINDOW, SWA_KV), cnt)


def _decode_kernel(sinks_ref, x_ref, st_ref, ck_ref, cv_ref, base0_ref, bias_ref,
                   g_mix_ref, w_in_ref, w_a_up_ref, b_a_ref, g_gla_ref, g_swa_ref, w_out_ref,
                   g_ffn_ref, w_r_ref, b_r_ref,
                   xmid_ref, hp_ref, topi_ref, gate_ref, rank_ref, sto_ref, cko_ref, cvo_ref, cnt_ref,
                   tq_scr, gv_scr, gr_scr, sq_scr, sk_scr, sv_scr, og_scr, os_scr):
    i = pl.program_id(0)
    n_seq = x_ref.shape[0]

    @pl.when(i == 0)
    def _():
        p = _project(x_ref[...], g_mix_ref[...], w_in_ref, w_a_up_ref[...], b_a_ref[...])
        a_hi, a_mid, a_lo = _split3(jnp.transpose(jnp.exp(p["log_a"])))
        tq_scr[...] = jnp.concatenate(
            [a_hi, a_mid, a_lo, jnp.transpose(p["gk"]).astype(BF16), jnp.transpose(p["gq"]).astype(BF16)], axis=0)
        gv_scr[...] = p["gv"]
        gr_scr[...] = p["gr"]
        sq_scr[...] = p["sq"]
        sk_scr[...] = p["sk"]
        sv_scr[...] = p["sv"]

    seq_row = _iota((n_seq, LANES), 0)
    half = _iota((1, LANES), 1) < SWA_HEAD_DIM
    row_id = _iota((WINDOW, SWA_KV), 0)
    head_diag = (_iota((16, SWA_Q), 1) // SWA_HEAD_DIM) == _iota((16, SWA_Q), 0)
    sink_col = jnp.concatenate(
        [jnp.full((1, 1), sinks_ref[h], F32) for h in range(SWA_HEADS)] + [jnp.zeros((8, 1), F32)], axis=0)

    def per_seq(sl, carry):
        s = i * DEC_SB + sl
        pick = (seq_row == s).astype(BF16)
        cols = _dot(tq_scr[...], pick)
        a_c = cols[0:GLA_QK] + cols[GLA_QK:2 * GLA_QK] + cols[2 * GLA_QK:3 * GLA_QK]
        k_c = cols[3 * GLA_QK:4 * GLA_QK]
        q_c = cols[4 * GLA_QK:5 * GLA_QK]
        st = st_ref[sl].reshape(GLA_QK, GLA_DV)
        v_row = gv_scr[pl.ds(s, 1), :]
        v_b = jnp.concatenate(
            [jnp.broadcast_to(v_row[:, h * GLA_DV:(h + 1) * GLA_DV], (GLA_DK, GLA_DV))
             for h in range(GLA_HEADS)], axis=0)
        st_new = a_c * st + k_c * v_b
        sto_ref[sl] = st_new.reshape(GLA_HEADS, GLA_DK, GLA_DV)
        t = q_c * st_new
        og_scr[pl.ds(s, 1), :] = jnp.concatenate(
            [jnp.sum(t[h * GLA_DK:(h + 1) * GLA_DK], axis=0, keepdims=True) for h in range(GLA_HEADS)],
            axis=1)

        k_new = sk_scr[pl.ds(s, 1), :]
        v_new = sv_scr[pl.ds(s, 1), :]
        kn = jnp.where(row_id == WINDOW - 1, k_new, pltpu.roll(ck_ref[sl], WINDOW - 1, 0))
        vn = jnp.where(row_id == WINDOW - 1, v_new, pltpu.roll(cv_ref[sl], WINDOW - 1, 0))
        cko_ref[sl] = kn
        cvo_ref[sl] = vn
        kr, vr = pltpu.roll(kn, SWA_HEAD_DIM, 1), pltpu.roll(vn, SWA_HEAD_DIM, 1)
        k0, k1 = jnp.where(half, kn, kr), jnp.where(half, kr, kn)
        v0, v1 = jnp.where(half, vn, vr), jnp.where(half, vr, vn)
        kw = jnp.concatenate([k0, k0, k1, k1], axis=1).astype(BF16)
        vw = jnp.concatenate([v0, v0, v1, v1], axis=1).astype(BF16)
        q_row = sq_scr[pl.ds(s, 1), :]
        qm = jnp.where(head_diag, jnp.broadcast_to(q_row, (16, SWA_Q)), 0.0).astype(BF16)
        sc = _dot_nt(qm, kw) + bias_ref[...]
        m = jnp.maximum(jnp.max(sc, axis=1, keepdims=True), sink_col)
        pr = jnp.exp(sc - m)
        inv = 1.0 / (jnp.sum(pr, axis=1, keepdims=True) + jnp.exp(sink_col - m))
        ow = _dot(pr.astype(BF16), vw) * inv
        os_scr[pl.ds(s, 1), :] = jnp.sum(jnp.where(head_diag, ow, 0.0), axis=0, keepdims=True)
        return carry

    lax.fori_loop(0, DEC_SB, per_seq, 0, unroll=DEC_UNROLL)

    @pl.when(i == pl.num_programs(0) - 1)
    def _():
        x_mid, hp, topi, gate8, rank8, base = _tail(
            x_ref[...], og_scr[...], gr_scr[...], os_scr[...], g_gla_ref[...], g_swa_ref[...],
            w_out_ref, g_ffn_ref[...], w_r_ref[...], b_r_ref[...], lambda: base0_ref[...])
        xmid_ref[...] = x_mid
        _store_slabs(hp_ref, hp)
        topi_ref[...] = topi
        gate_ref[...] = gate8
        rank_ref[...] = rank8
        cnt_ref[...] = base


def _decode_call(xs, state, ck, cv, base0, bias_dec, wts):
    n_seq = xs.shape[0]
    nb = n_seq // DEC_SB
    weight_args = (wts["g_mix"], wts["w_in"], wts["w_a_up"], wts["b_a"], wts["g_gla"],
                   wts["g_swa"], wts["w_out"], wts["g_ffn"], wts["w_r"], wts["b_r"])
    in_specs = [
        pl.BlockSpec(memory_space=pltpu.SMEM),
        _full_spec(xs.shape),
        pl.BlockSpec((DEC_SB, GLA_HEADS, GLA_DK, GLA_DV), lambda i: (i, 0, 0, 0)),
        pl.BlockSpec((DEC_SB, WINDOW, SWA_KV), lambda i: (i, 0, 0)),
        pl.BlockSpec((DEC_SB, WINDOW, SWA_KV), lambda i: (i, 0, 0)),
        _full_spec(base0.shape), _full_spec(bias_dec.shape),
    ] + [_full_spec(w.shape) for w in weight_args]
    out_specs = [
        _full_spec((n_seq, D_MODEL)),
        _full_spec((n_seq * SLAB, LANES)),
        _full_spec((8, n_seq)), _full_spec((8, n_seq)), _full_spec((8, n_seq)),
        pl.BlockSpec((DEC_SB, GLA_HEADS, GLA_DK, GLA_DV), lambda i: (i, 0, 0, 0)),
        pl.BlockSpec((DEC_SB, WINDOW, SWA_KV), lambda i: (i, 0, 0)),
        pl.BlockSpec((DEC_SB, WINDOW, SWA_KV), lambda i: (i, 0, 0)),
        _full_spec((N_EXPERTS, LANES)),
    ]
    out_shape = [
        jax.ShapeDtypeStruct((n_seq, D_MODEL), F32),
        jax.ShapeDtypeStruct((n_seq * SLAB, LANES), jnp.int32),
        jax.ShapeDtypeStruct((8, n_seq), jnp.int32),
        jax.ShapeDtypeStruct((8, n_seq), F32),
        jax.ShapeDtypeStruct((8, n_seq), jnp.int32),
        jax.ShapeDtypeStruct(state.shape, F32),
        jax.ShapeDtypeStruct(ck.shape, F32),
        jax.ShapeDtypeStruct(cv.shape, F32),
        jax.ShapeDtypeStruct((N_EXPERTS, LANES), F32),
    ]
    scratch = [pltpu.VMEM((5 * GLA_QK, n_seq), BF16)] + [
        pltpu.VMEM((n_seq, GLA_V), F32), pltpu.VMEM((n_seq, GLA_V), F32), pltpu.VMEM((n_seq, SWA_Q), F32),
        pltpu.VMEM((n_seq, SWA_KV), F32), pltpu.VMEM((n_seq, SWA_KV), F32),
        pltpu.VMEM((n_seq, GLA_V), F32), pltpu.VMEM((n_seq, SWA_Q), F32)]
    return pl.pallas_call(
        _decode_kernel,
        grid=(nb,),
        in_specs=in_specs,
        out_specs=out_specs,
        out_shape=out_shape,
        scratch_shapes=scratch,
        compiler_params=pltpu.CompilerParams(dimension_semantics=("arbitrary",),
                                             vmem_limit_bytes=VMEM_LIMIT),
        name="decode",
    )(wts["sinks"], xs, state, ck, cv, base0, bias_dec, *weight_args)


SC_CORES = 2
SC_SUBCORES = 16
SC_WORKERS = SC_CORES * SC_SUBCORES
SC_SCATTER_ROWS = 64
SC_GATHER_ROWS = 96


def _sc_mesh():
    return plsc.VectorSubcoreMesh(core_axis_name="c", subcore_axis_name="s")


def _sc_worker_id():
    return lax.axis_index("s") * SC_CORES + lax.axis_index("c")


def _sc_scatter_rows(src_p, idx_p, n_out):
    rows = SC_SCATTER_ROWS
    n_chunks = idx_p.shape[0] // SC_WORKERS
    assert n_chunks * SC_WORKERS == idx_p.shape[0] and n_chunks % 2 == 0

    @functools.partial(
        pl.kernel, mesh=_sc_mesh(),
        out_type=jax.ShapeDtypeStruct((n_out, SLAB, LANES), jnp.int32),
        scratch_types=[pltpu.VMEM((2, TOP_K, rows), jnp.int32), pltpu.VMEM((2, rows, SLAB, LANES), jnp.int32),
                       pltpu.SemaphoreType.DMA((2,)), pltpu.SemaphoreType.DMA((2,))])
    def scatter_rows(srcp_hbm, idxp_hbm, out_hbm, idx_v, rows_v, lsem, ssem):
        wid = _sc_worker_id()

        def loads(c, b):
            g = wid * n_chunks + c
            return (pltpu.make_async_copy(idxp_hbm.at[g], idx_v.at[b], lsem.at[b]),
                    pltpu.make_async_copy(srcp_hbm.at[pl.ds(pl.multiple_of(g * rows, 8), rows)], rows_v.at[b],
                                          lsem.at[b]))

        def scatters(b):
            return [pltpu.make_async_copy(rows_v.at[b], out_hbm.at[idx_v.at[b, k]], ssem.at[b])
                    for k in range(TOP_K)]

        for d in loads(0, 0):
            d.start()

        @pl.loop(0, n_chunks, step=2)
        def _(c0):
            for b in range(2):
                c = c0 + b
                for d in loads(c, b):
                    d.wait()

                @pl.when(c >= 1)
                def _():
                    for d in scatters(1 - b):
                        d.wait()

                @pl.when(c + 1 < n_chunks)
                def _():
                    for d in loads(c + 1, 1 - b):
                        d.start()

                for d in scatters(b):
                    d.start()

        for d in scatters((n_chunks - 1) % 2):
            d.wait()

    return scatter_rows(src_p, idx_p)


def _place_rows_kernel(dest_ref, xs_in_hbm, rows_ref, xs_hbm, sem):
    del xs_in_hbm
    n = dest_ref.shape[1]

    def row_copy(t, k):
        dst = xs_hbm.at[pl.ds(pl.multiple_of(dest_ref[k, t] * SLAB, SLAB), SLAB)]
        return pltpu.make_async_copy(rows_ref.at[pl.ds(pl.multiple_of(t * SLAB, SLAB), SLAB)], dst, sem)

    def issue(t, c):
        for k in range(TOP_K):
            row_copy(t, k).start()
        return c
    lax.fori_loop(0, n, issue, 0)

    def drain(t, c):
        for k in range(TOP_K):
            row_copy(t, k).wait()
        return c
    lax.fori_loop(0, n, drain, 0)


def _place_rows_call(xs2, rows2, dest):
    return pl.pallas_call(
        _place_rows_kernel,
        grid=(1,),
        in_specs=[pl.BlockSpec(memory_space=pltpu.SMEM), pl.BlockSpec(memory_space=pl.ANY),
                  _full_spec(rows2.shape)],
        out_specs=pl.BlockSpec(memory_space=pl.ANY),
        out_shape=jax.ShapeDtypeStruct(xs2.shape, xs2.dtype),
        scratch_shapes=[pltpu.SemaphoreType.DMA],
        input_output_aliases={1: 0},
        compiler_params=pltpu.CompilerParams(dimension_semantics=("arbitrary",), has_side_effects=True),
        name="place_rows",
    )(dest, xs2, rows2)


def _sc_gather_rows(src3, idx2):
    rows = SC_GATHER_ROWS
    n_chunks = idx2.shape[0] // SC_WORKERS
    assert n_chunks * SC_WORKERS == idx2.shape[0] and idx2.shape[1] == rows and n_chunks % 2 == 0

    @functools.partial(
        pl.kernel, mesh=_sc_mesh(),
        out_type=jax.ShapeDtypeStruct((idx2.shape[0] * rows, SLAB, LANES), jnp.int32),
        scratch_types=[pltpu.VMEM((2, rows), jnp.int32), pltpu.VMEM((2, rows, SLAB, LANES), jnp.int32),
                       pltpu.SemaphoreType.DMA((2,)), pltpu.SemaphoreType.DMA((2,))])
    def gather_rows(src_hbm, idx_hbm, out_hbm, idx_v, rows_v, gsem, wsem):
        wid = _sc_worker_id()

        def gather(b):
            return pltpu.make_async_copy(src_hbm.at[idx_v.at[b]], rows_v.at[b], gsem.at[b])

        def write(c, b):
            base = pl.multiple_of((wid * n_chunks + c) * rows, 8)
            return pltpu.make_async_copy(rows_v.at[b], out_hbm.at[pl.ds(base, rows)], wsem.at[b])

        pltpu.sync_copy(idx_hbm.at[wid * n_chunks], idx_v.at[0])
        gather(0).start()

        @pl.loop(0, n_chunks, step=2)
        def _(c0):
            for b in range(2):
                c = c0 + b

                @pl.when(c + 1 < n_chunks)
                def _():
                    @pl.when(c >= 1)
                    def _():
                        write(c - 1, 1 - b).wait()
                    pltpu.sync_copy(idx_hbm.at[wid * n_chunks + c + 1], idx_v.at[1 - b])
                    gather(1 - b).start()

                gather(b).wait()
                write(c, b).start()

        write(n_chunks - 2, 0).wait()
        write(n_chunks - 1, 1).wait()

    return gather_rows(src3, idx2)


FF_TILE = 256


def _ffn_kernel(blk_e_ref, nused_ref, x_ref, wu_ref, bu_ref, wd_ref, bd_ref, y_ref, xbf, actbf, wu_bf, wd_bf):
    i = pl.program_id(0)
    tm = xbf.shape[0]
    n_tiles = D_FF // FF_TILE

    @pl.when(i < nused_ref[0])
    def _():
        @pl.when((i == 0) | (blk_e_ref[i] != blk_e_ref[jnp.maximum(i - 1, 0)]))
        def _():
            wu_bf[...] = wu_ref[0].astype(BF16)
            wd_bf[...] = wd_ref[0].astype(BF16)

        for c in range(SLAB):
            lo, hi = _load_slab_chunk(x_ref, tm, c)
            xbf[:, c * LANES:(c + 1) * LANES] = lo.astype(BF16)
            xbf[:, HALF_D + c * LANES:HALF_D + (c + 1) * LANES] = hi.astype(BF16)
        for n in range(n_tiles):
            gc = slice(n * FF_TILE, (n + 1) * FF_TILE)
            lc = slice(D_FF + n * FF_TILE, D_FF + (n + 1) * FF_TILE)
            g = jnp.minimum(_dot(xbf[...], wu_bf[:, gc]) + bu_ref[0, :, gc], SWIGLU_LIMIT)
            lin = jnp.clip(_dot(xbf[...], wu_bf[:, lc]) + bu_ref[0, :, lc], -SWIGLU_LIMIT, SWIGLU_LIMIT)
            actbf[:, gc] = (g * jax.nn.sigmoid(SWIGLU_ALPHA * g) * (lin + 1.0)).astype(BF16)
        per_tile = FF_TILE // LANES
        for n in range(n_tiles // 2):
            yl = slice(n * FF_TILE, (n + 1) * FF_TILE)
            yh = slice(HALF_D + n * FF_TILE, HALF_D + (n + 1) * FF_TILE)
            y_lo = _dot(actbf[...], wd_bf[:, yl]) + bd_ref[0, :, yl]
            y_hi = _dot(actbf[...], wd_bf[:, yh]) + bd_ref[0, :, yh]
            for c in range(per_tile):
                sl = slice(c * LANES, (c + 1) * LANES)
                y_ref[pl.ds(n * per_tile + c, tm, stride=SLAB), :] = _pack_pair(y_lo[:, sl], y_hi[:, sl])

    @pl.when(i >= nused_ref[0])
    def _():
        y_ref[...] = jnp.zeros_like(y_ref)


def _ffn_call(blk_e, nused, xs2, w_up, b_up, w_down, b_down, tm):
    n_blocks = blk_e.shape[0]
    row_blk = pl.BlockSpec((tm * SLAB, LANES), lambda i, be, nu: (i, 0))
    grid_spec = pltpu.PrefetchScalarGridSpec(
        num_scalar_prefetch=2,
        grid=(n_blocks,),
        in_specs=[
            pl.BlockSpec((tm * SLAB, LANES), lambda i, be, nu: (jnp.minimum(i, nu[0] - 1), 0)),
            pl.BlockSpec((1, D_MODEL, 2 * D_FF), lambda i, be, nu: (be[i], 0, 0)),
            pl.BlockSpec((1, 1, 2 * D_FF), lambda i, be, nu: (be[i], 0, 0)),
            pl.BlockSpec((1, D_FF, D_MODEL), lambda i, be, nu: (be[i], 0, 0)),
            pl.BlockSpec((1, 1, D_MODEL), lambda i, be, nu: (be[i], 0, 0)),
        ],
        out_specs=row_blk,
        scratch_shapes=[pltpu.VMEM((tm, D_MODEL), BF16), pltpu.VMEM((tm, D_FF), BF16),
                        pltpu.VMEM((D_MODEL, 2 * D_FF), BF16), pltpu.VMEM((D_FF, D_MODEL), BF16)],
    )
    return pl.pallas_call(
        _ffn_kernel,
        grid_spec=grid_spec,
        out_shape=jax.ShapeDtypeStruct((n_blocks * tm * SLAB, LANES), jnp.int32),
        compiler_params=pltpu.CompilerParams(dimension_semantics=("arbitrary",),
                                             vmem_limit_bytes=VMEM_LIMIT),
        name="experts",
    )(blk_e, nused, xs2, w_up, b_up.reshape(N_EXPERTS, 1, 2 * D_FF), w_down, b_down.reshape(N_EXPERTS, 1, D_MODEL))


def _combine_kernel(ys0_ref, ys1_ref, ys2_ref, ys3_ref, xmid_ref, gate_ref, g_final_ref, y_ref):
    tm = xmid_ref.shape[0]
    gts = jnp.transpose(jnp.concatenate([gate_ref[...], jnp.zeros((LANES - 8, tm), F32)], axis=0))
    lows, highs = [], []
    for c in range(SLAB):
        acc_lo = xmid_ref[:, c * LANES:(c + 1) * LANES]
        acc_hi = xmid_ref[:, HALF_D + c * LANES:HALF_D + (c + 1) * LANES]
        for k, ys_ref in enumerate((ys0_ref, ys1_ref, ys2_ref, ys3_ref)):
            lo, hi = _load_slab_chunk(ys_ref, tm, c)
            acc_lo = acc_lo + lo * gts[:, k:k + 1]
            acc_hi = acc_hi + hi * gts[:, k:k + 1]
        lows.append(acc_lo)
        highs.append(acc_hi)
    y_ref[...] = _rms(jnp.concatenate(lows + highs, axis=1), g_final_ref[...])


def _combine_call(ys4, t_stride, row0, x_mid, gates, g_final, tm):
    T = x_mid.shape[0]
    blk0 = row0 // tm
    per_k = t_stride // tm
    assert per_k * tm == t_stride and blk0 * tm == row0

    def ys_spec(k):
        return pl.BlockSpec((tm * SLAB, LANES), lambda i: (k * per_k + blk0 + i, 0))

    return pl.pallas_call(
        _combine_kernel,
        grid=(T // tm,),
        in_specs=[
            ys_spec(0), ys_spec(1), ys_spec(2), ys_spec(3),
            pl.BlockSpec((tm, D_MODEL), lambda i: (i, 0)),
            pl.BlockSpec((8, tm), lambda i: (0, i)),
            _full_spec((1, D_MODEL)),
        ],
        out_specs=pl.BlockSpec((tm, D_MODEL), lambda i: (i, 0)),
        out_shape=jax.ShapeDtypeStruct((T, D_MODEL), F32),
        compiler_params=pltpu.CompilerParams(dimension_semantics=("arbitrary",),
                                             vmem_limit_bytes=VMEM_LIMIT),
        name="combine",
    )(ys4, ys4, ys4, ys4, x_mid, gates, g_final)


def _t5_bucket(dist):
    n = jnp.maximum(dist, 0)
    max_exact = NUM_BUCKETS // 2
    nf = jnp.maximum(n, 1).astype(F32)
    large = max_exact + (jnp.log(nf / max_exact) / math.log(MAX_DISTANCE / max_exact)
                         * (NUM_BUCKETS - max_exact)).astype(jnp.int32)
    large = jnp.minimum(large, NUM_BUCKETS - 1)
    return jnp.where(n < max_exact, n, large)


def kernel(x_prompt, x_sample, state_gla, cache_swa_k, cache_swa_v, meta_tokens, rel_bias_table,
           g_mix, w_in, w_a_up, b_a, g_gla_out, g_swa_out, attn_sinks, w_out,
           g_ffn, w_router, b_router, w_up, b_up, w_down, b_down, g_final):
    assert g_mix.shape[0] == 1, "single-layer trunk"
    B, L, _ = x_prompt.shape
    n_seq = x_sample.shape[0]
    TP = B * L
    T_all = TP + n_seq

    wi = w_in[0]
    sizes = (GLA_QK, GLA_QK, GLA_V, GLA_V, GLA_LOWRANK, SWA_Q, SWA_KV, SWA_KV)
    offs = [0]
    for s in sizes:
        offs.append(offs[-1] + s)
    seg = [wi[:, offs[n]:offs[n + 1]] for n in range(8)]
    seg[0] = seg[0] * (GLA_DK ** -0.5)
    seg[5] = seg[5] * (SWA_HEAD_DIM ** -0.5)
    w_in_r = jnp.concatenate(
        seg[0:4] + seg[5:8] + [seg[4], jnp.zeros((D_MODEL, LANES - GLA_LOWRANK), F32)], axis=1).astype(BF16)
    w_a_pad = jnp.concatenate([w_a_up[0], jnp.zeros((LANES - GLA_LOWRANK, GLA_QK), F32)], axis=0).astype(BF16)
    wr_t = jnp.transpose(w_router[0])
    wr_hi = wr_t.astype(BF16)
    wr_lo = (wr_t - wr_hi.astype(F32)).astype(BF16)
    qi = jnp.arange(WINDOW)[:, None]
    kj = jnp.arange(2 * WINDOW)[None, :]
    buckets = jnp.arange(NUM_BUCKETS)
    table = rel_bias_table.astype(F32)
    oh_p = (_t5_bucket(qi - kj + WINDOW)[..., None] == buckets).astype(F32)
    bias_p = jnp.einsum("qkb,bh->hkq", oh_p, table, precision=lax.Precision.HIGHEST)
    in_window = jnp.transpose((kj > qi) & (kj <= qi + WINDOW))
    bias_p = jnp.where(in_window[None], bias_p, NEG_INF)
    bias_p = bias_p.reshape(SWA_KV_HEADS, SWA_GROUP, 2 * WINDOW, WINDOW).transpose(0, 2, 1, 3)
    bias_p = bias_p.reshape(SWA_KV_HEADS, 2 * WINDOW, SWA_GROUP * WINDOW)
    oh_d = (_t5_bucket(WINDOW - 1 - jnp.arange(WINDOW))[:, None] == buckets).astype(F32)
    bias_d = jnp.einsum("rb,bh->hr", oh_d, table, precision=lax.Precision.HIGHEST)
    bias_d = jnp.concatenate([bias_d, jnp.zeros((8, WINDOW), F32)], axis=0)
    wts = dict(
        sinks=attn_sinks[0].astype(F32), bias=bias_p,
        g_mix=g_mix[0][None], w_in=w_in_r, w_a_up=w_a_pad, b_a=b_a[0][None],
        g_gla=g_gla_out[0][None], g_swa=g_swa_out[0][None], w_out=w_out[0].astype(BF16),
        g_ffn=g_ffn[0][None], w_r=jnp.concatenate([wr_hi, wr_lo], axis=0), b_r=b_router[0][:, None],
    )

    x_pre = jnp.concatenate([jnp.zeros((WINDOW - N_META, D_MODEL), F32), meta_tokens.astype(F32)], axis=0)[None]
    zeros_s = jnp.zeros((GLA_QK, GLA_V), F32)
    zeros_kv = jnp.zeros((WINDOW, SWA_KV), F32)
    zeros_b = jnp.zeros((N_EXPERTS, LANES), F32)
    pre = _mixer_call(x_pre, zeros_s, zeros_kv, zeros_kv, zeros_b, wts, WINDOW, 1, WINDOW - N_META, 0)
    s_meta, k_meta, v_meta = pre[5][0], pre[6][0], pre[7][0]

    (xmid_p, hp_p, topi_p, gate_p, rank_p, s_p, k_p, v_p, cnt_p) = _mixer_call(
        x_prompt, s_meta, k_meta, v_meta, zeros_b, wts, MIX_STREAM_TM, 2, 0, WINDOW - N_META)

    n_slots = T_all * TOP_K
    tm = -(-math.ceil(1.05 * n_slots / N_EXPERTS / MOE_BLOCKS_PER_EXPERT) // 16) * 16
    n_blocks = -(-(TP * TOP_K + N_EXPERTS * n_seq) // tm) + N_EXPERTS
    n_pad = n_blocks * tm
    counts_p = cnt_p[:, 0].astype(jnp.int32)
    padded = (counts_p + n_seq + tm - 1) // tm * tm
    pad_end = jnp.cumsum(padded)
    pad_start = pad_end - padded
    e_ids = jnp.arange(N_EXPERTS, dtype=jnp.int32)

    def sorted_rows(top_e, rank):
        return jnp.sum(jnp.where(top_e[..., None] == e_ids, pad_start, 0), axis=-1) + rank

    blk_e = jnp.minimum(jnp.sum(pad_end[None] <= (jnp.arange(n_blocks, dtype=jnp.int32) * tm)[:, None], axis=1),
                        N_EXPERTS - 1).astype(jnp.int32)
    nused = (pad_end[-1] // tm).astype(jnp.int32).reshape(1)
    dest_p = sorted_rows(topi_p[:TOP_K], rank_p[:TOP_K])
    idx_p = dest_p.reshape(TOP_K, TP // SC_SCATTER_ROWS, SC_SCATTER_ROWS).transpose(1, 0, 2)
    xs3 = _sc_scatter_rows(hp_p.reshape(TP, SLAB, LANES), idx_p, n_pad)

    (xmid_s, hp_s, topi_s, gate_s, rank_s, st_s, ck_s, cv_s, _) = _decode_call(
        x_sample[:, 0], state_gla[0], cache_swa_k[0].reshape(n_seq, WINDOW, SWA_KV),
        cache_swa_v[0].reshape(n_seq, WINDOW, SWA_KV), cnt_p, bias_d, wts)
    dest_s = sorted_rows(topi_s[:TOP_K], rank_s[:TOP_K])
    xs2 = _place_rows_call(xs3.reshape(-1, LANES), hp_s, dest_s)
    dest = jnp.concatenate([dest_p, dest_s], axis=1)

    ys2 = _ffn_call(blk_e, nused, xs2, w_up[0], b_up[0], w_down[0], b_down[0], tm)
    unit = math.lcm(2 * SC_WORKERS * SC_GATHER_ROWS // TOP_K, MIX_TM)
    t_stride = -(-T_all // unit) * unit
    filler = jnp.arange(TOP_K * (t_stride - T_all), dtype=jnp.int32).reshape(TOP_K, t_stride - T_all)
    slot_src = jnp.concatenate([dest, filler], axis=1)
    slot_src = slot_src.reshape(TOP_K * t_stride // SC_GATHER_ROWS, SC_GATHER_ROWS)
    ys4 = _sc_gather_rows(ys2.reshape(-1, SLAB, LANES), slot_src).reshape(-1, LANES)

    gf = g_final[None]
    y_p = _combine_call(ys4, t_stride, 0, xmid_p.reshape(TP, D_MODEL), gate_p, gf, MIX_TM)
    y_s = _combine_call(ys4, t_stride, TP, xmid_s, gate_s, gf, n_seq)

    s_heads = jnp.stack([s_p[:, h * GLA_DK:(h + 1) * GLA_DK, h * GLA_DV:(h + 1) * GLA_DV]
                         for h in range(GLA_HEADS)], axis=1)
    return (y_p.reshape(B, L, D_MODEL), y_s.reshape(n_seq, 1, D_MODEL), s_heads[None],
            k_p.reshape(1, B, WINDOW, SWA_KV_HEADS, SWA_HEAD_DIM),
            v_p.reshape(1, B, WINDOW, SWA_KV_HEADS, SWA_HEAD_DIM),
            st_s[None], ck_s.reshape(1, n_seq, WINDOW, SWA_KV_HEADS, SWA_HEAD_DIM),
            cv_s.reshape(1, n_seq, WINDOW, SWA_KV_HEADS, SWA_HEAD_DIM))
```

```python
import functools
import math

import jax
import jax.numpy as jnp
from jax import lax
from jax.experimental import pallas as pl
from jax.experimental.pallas import tpu as pltpu
from jax.experimental.pallas import tpu_sc as plsc

D_MODEL = 1024
N_META = 16
GLA_HEADS = 4
GLA_DK = 64
GLA_DV = 128
GLA_LOWRANK = 16
GLA_GATE_TAU = 16.0
GLA_CHUNK = 64
SWA_HEADS = 8
SWA_KV_HEADS = 2
SWA_HEAD_DIM = 64
SWA_GROUP = SWA_HEADS // SWA_KV_HEADS
WINDOW = 128
NUM_BUCKETS = 32
MAX_DISTANCE = 128
N_EXPERTS = 32
TOP_K = 4
D_FF = 1024
SWIGLU_ALPHA = 1.702
SWIGLU_LIMIT = 7.0
RMS_EPS = 1e-6

GLA_QK = GLA_HEADS * GLA_DK
GLA_V = GLA_HEADS * GLA_DV
SWA_Q = SWA_HEADS * SWA_HEAD_DIM
SWA_KV = SWA_KV_HEADS * SWA_HEAD_DIM
LANES = 128
C_GQ, C_GK, C_GV, C_GR = 0, GLA_QK, 2 * GLA_QK, 2 * GLA_QK + GLA_V
C_SQ = C_GR + GLA_V
C_SK = C_SQ + SWA_Q
C_SV = C_SK + SWA_KV
C_GA = C_SV + SWA_KV
D_PROJ = C_GA + LANES

MIX_TM = 512
MIX_STREAM_TM = 512
MOE_BLOCKS_PER_EXPERT = 4
DEC_SB = 16
DEC_UNROLL = 8
VMEM_LIMIT = 56 * 1024 * 1024

F32 = jnp.float32
BF16 = jnp.bfloat16
NEG_INF = float("-inf")


def _dot(a, b):
    return jnp.dot(a, b, preferred_element_type=F32)


def _dot_nt(a, b):
    return lax.dot_general(a, b, (((1,), (1,)), ((), ())), preferred_element_type=F32)


def _split3(x):
    hi = x.astype(BF16)
    r1 = x - hi.astype(F32)
    mid = r1.astype(BF16)
    lo = (r1 - mid.astype(F32)).astype(BF16)
    return hi, mid, lo


def _rms(x, g):
    return x * lax.rsqrt(jnp.mean(x * x, axis=-1, keepdims=True) + RMS_EPS) * g


def _iota(shape, dim):
    return lax.broadcasted_iota(jnp.int32, shape, dim)


HALF_D = D_MODEL // 2
SLAB = HALF_D // LANES


def _pack_pair(lo, hi):
    bl = lax.bitcast_convert_type(lo.astype(BF16).astype(F32), jnp.uint32)
    bh = lax.bitcast_convert_type(hi.astype(BF16).astype(F32), jnp.uint32)
    return lax.bitcast_convert_type(bh | lax.shift_right_logical(bl, jnp.uint32(16)), jnp.int32)


def _unpack_pair(w):
    u = lax.bitcast_convert_type(w, jnp.uint32)
    lo = lax.bitcast_convert_type(lax.shift_left(u, jnp.uint32(16)), F32)
    hi = lax.bitcast_convert_type(u & jnp.uint32(0xFFFF0000), F32)
    return lo, hi


def _store_slabs(ref, x):
    rows = x.shape[0]
    for c in range(SLAB):
        sl = slice(c * LANES, (c + 1) * LANES)
        ref[pl.ds(c, rows, stride=SLAB), :] = _pack_pair(x[:, sl], x[:, HALF_D + c * LANES:HALF_D + (c + 1) * LANES])


def _load_slab_chunk(ref, rows, c):
    return _unpack_pair(ref[pl.ds(c, rows, stride=SLAB), :])


def _drain(steps):
    try:
        while True:
            next(steps)
    except StopIteration as done:
        return done.value


def _project_steps(x, g_mix, w_in_ref, w_a_up, b_a):
    h = _rms(x, g_mix).astype(BF16)
    yield

    def cols(lo, width, tile=2 * LANES):
        parts = []
        for off in range(0, width, tile):
            parts.append(_dot(h, w_in_ref[:, lo + off:lo + min(off + tile, width)]))
            yield
        return parts[0] if len(parts) == 1 else jnp.concatenate(parts, axis=1)

    ga = (yield from cols(C_GA, LANES)).astype(BF16)
    z = _dot(ga, w_a_up) + b_a
    log_a = -(jnp.maximum(-z, 0.0) + jnp.log(1.0 + jnp.exp(-jnp.abs(z)))) / GLA_GATE_TAU
    gqk = yield from cols(C_GQ, 2 * GLA_QK)
    gv = yield from cols(C_GV, GLA_V)
    swa = yield from cols(C_SQ, SWA_Q + 2 * SWA_KV)
    gr = yield from cols(C_GR, GLA_V)
    return dict(
        gq=gqk[:, :GLA_QK],
        gk=gqk[:, GLA_QK:],
        gv=gv,
        gr=gr,
        sq=swa[:, :SWA_Q],
        sk=swa[:, SWA_Q:SWA_Q + SWA_KV],
        sv=swa[:, SWA_Q + SWA_KV:],
        log_a=log_a,
    )


def _project(x, g_mix, w_in_ref, w_a_up, b_a):
    return _drain(_project_steps(x, g_mix, w_in_ref, w_a_up, b_a))


def _tail(*args):
    return _drain(_tail_steps(*args))


def _tail_steps(x, o_gla, gr, o_swa, g_gla_out, g_swa_out, w_out_ref, g_ffn, w_r, b_r, get_base):
    tm = x.shape[0]
    gate = gr * jax.nn.sigmoid(gr)
    parts = []
    for h in range(GLA_HEADS):
        sl = slice(h * GLA_DV, (h + 1) * GLA_DV)
        parts.append(_rms(o_gla[:, sl], g_gla_out) * gate[:, sl])
    og = jnp.concatenate(parts, axis=1).astype(BF16)
    yield
    os_ = _rms(o_swa, g_swa_out).astype(BF16)
    x_mid = x + _dot(og, w_out_ref[0:GLA_V])
    yield
    x_mid = x_mid + _dot(os_, w_out_ref[GLA_V:GLA_V + SWA_Q])
    yield
    hp = _rms(x_mid, g_ffn)

    h1 = hp.astype(BF16)
    h2 = (hp - h1.astype(F32)).astype(BF16)
    yield
    la = _dot_nt(w_r, h1)
    lb = _dot_nt(w_r[0:N_EXPERTS], h2)
    logits = la[0:N_EXPERTS] + la[N_EXPERTS:2 * N_EXPERTS] + lb + b_r
    yield

    eidx = _iota((N_EXPERTS, tm), 0)
    vals, idxs, onehots = [], [], []
    l = logits
    for _ in range(TOP_K):
        m = jnp.max(l, axis=0, keepdims=True)
        sel = jnp.min(jnp.where(l == m, eidx, N_EXPERTS), axis=0, keepdims=True)
        oh = eidx == sel
        l = jnp.where(oh, NEG_INF, l)
        vals.append(m)
        idxs.append(sel)
        onehots.append(oh)
    es = [jnp.exp(v - vals[0]) for v in vals]
    denom = es[0] + es[1] + es[2] + es[3]
    gates = [e / denom for e in es]

    ohf = jnp.concatenate([oh.astype(F32) for oh in onehots], axis=0)
    upper = (_iota((tm, tm), 0) < _iota((tm, tm), 1)).astype(BF16)
    prefix = _dot(ohf.astype(BF16), upper)
    yield
    base = get_base()
    ranks = []
    for k in range(TOP_K):
        sl = slice(k * N_EXPERTS, (k + 1) * N_EXPERTS)
        ohk = ohf[sl]
        base_t = jnp.concatenate([base] * (tm // LANES), axis=1)
        ranks.append(jnp.sum(ohk * (prefix[sl] + base_t), axis=0, keepdims=True))
        base = base + jnp.sum(ohk, axis=1, keepdims=True)
    zi = jnp.zeros((8 - TOP_K, tm), jnp.int32)
    zf = jnp.zeros((8 - TOP_K, tm), F32)
    topi = jnp.concatenate(idxs + [zi], axis=0)
    gate8 = jnp.concatenate(gates + [zf], axis=0)
    rank8 = jnp.concatenate([r.astype(jnp.int32) for r in ranks] + [zi], axis=0)
    return x_mid, hp, topi, gate8, rank8, base


def _gla_chunks(p, row0, s_blocks, n_lead_pad):
    tm = p["gq"].shape[0]
    nch = tm // GLA_CHUNK
    log_a = p["log_a"]
    if n_lead_pad:
        rows = row0 + _iota((tm, GLA_QK), 0)
        log_a = jnp.where(rows >= n_lead_pad, log_a, 0.0)
    ri, ci = _iota((tm, tm), 0), _iota((tm, tm), 1)
    tril = ((ri >= ci) & (ri // GLA_CHUNK == ci // GLA_CHUNK)).astype(BF16)
    hi, mid, lo = _split3(log_a)
    b_all = _dot(tril, hi) + _dot(tril, mid) + _dot(tril, lo)
    yield

    c64 = GLA_CHUNK
    kk_mask = (_iota((GLA_QK, GLA_QK), 0) // c64) == (_iota((GLA_QK, GLA_QK), 1) // GLA_DK)
    vv_mask = (_iota((GLA_QK, GLA_V), 0) // c64) == (_iota((GLA_QK, GLA_V), 1) // GLA_DV)
    zero_blk = jnp.zeros((GLA_DK, GLA_DV), BF16)
    causal = (_iota((c64, GLA_QK), 0) >= (_iota((c64, GLA_QK), 1) % c64)).astype(F32)
    zpad_k = jnp.zeros((LANES - c64, GLA_QK), F32)
    zpad_v = jnp.zeros((LANES - c64, GLA_V), BF16)

    outs = []
    for c in range(nch):
        rs = slice(c * c64, (c + 1) * c64)
        b = b_all[rs]
        q, k, v = p["gq"][rs], p["gk"][rs], p["gv"][rs]
        b_last = b[c64 - 1:c64]
        qt = (q * jnp.exp(b)).astype(BF16)
        kt = k * jnp.exp(-b)
        kd = k * jnp.exp(b_last - b)
        vb = v.astype(BF16)
        k_bd = jnp.where(kk_mask, jnp.concatenate([kt] * GLA_HEADS, axis=0), 0.0).astype(BF16)
        a = (_dot_nt(qt, k_bd) * causal).astype(BF16)
        v_bd = jnp.where(vv_mask, jnp.concatenate([vb] * GLA_HEADS, axis=0), jnp.zeros((), BF16))
        s_bd = jnp.concatenate(
            [jnp.concatenate([s_blocks[h].astype(BF16) if g == h else zero_blk for g in range(GLA_HEADS)], axis=1)
             for h in range(GLA_HEADS)], axis=0)
        outs.append(_dot(qt, s_bd) + _dot(a, v_bd))
        kd_t = jnp.transpose(jnp.concatenate([kd, zpad_k], axis=0)).astype(BF16)
        upd = _dot(kd_t, jnp.concatenate([vb, zpad_v], axis=0))
        decay = jnp.exp(jnp.transpose(jnp.broadcast_to(b_last, (LANES, GLA_QK))))
        s_blocks = [s_blocks[h] * decay[h * GLA_DK:(h + 1) * GLA_DK]
                    + upd[h * GLA_DK:(h + 1) * GLA_DK, h * GLA_DV:(h + 1) * GLA_DV] for h in range(GLA_HEADS)]
        yield
    return jnp.concatenate(outs, axis=0), s_blocks


def _swa_block(sq, kcat, vcat, bias_ref, sinks_ref, valid_t):
    half = _iota((1, LANES), 1) < SWA_HEAD_DIM
    top_rows = _iota((LANES, 1), 0) < SWA_HEAD_DIM
    k_roll = pltpu.roll(kcat, SWA_HEAD_DIM, 1)
    v_t = jnp.transpose(vcat)
    zeros_v = jnp.zeros((SWA_HEAD_DIM, 2 * WINDOW), F32)
    cols = []
    for kv in range(SWA_KV_HEADS):
        kk = jnp.where(half, kcat, k_roll) if kv == 0 else jnp.where(half, k_roll, kcat)
        q_parts = []
        for c in (2 * kv, 2 * kv + 1):
            qc = sq[:, c * LANES:(c + 1) * LANES]
            q_parts.append(jnp.where(half, qc, 0.0))
            q_parts.append(jnp.where(half, 0.0, qc))
        q_st = jnp.concatenate(q_parts, axis=0).astype(BF16)
        s = _dot_nt(kk.astype(BF16), q_st) + bias_ref[kv]
        if valid_t is not None:
            s = jnp.where(valid_t, s, NEG_INF)
        sink = jnp.concatenate(
            [jnp.full((1, WINDOW), sinks_ref[kv * SWA_GROUP + g], F32) for g in range(SWA_GROUP)], axis=1)
        m = jnp.maximum(jnp.max(s, axis=0, keepdims=True), sink)
        pr = jnp.exp(s - m)
        inv = 1.0 / (jnp.sum(pr, axis=0, keepdims=True) + jnp.exp(sink - m))
        pb = pr.astype(BF16)
        vk = v_t[kv * SWA_HEAD_DIM:(kv + 1) * SWA_HEAD_DIM]
        vv_t = jnp.concatenate([jnp.concatenate([vk, zeros_v], axis=1),
                                jnp.concatenate([zeros_v, vk], axis=1)], axis=0).astype(BF16)
        for pair in range(SWA_GROUP // 2):
            ce = slice(2 * pair * WINDOW, (2 * pair + 1) * WINDOW)
            co = slice((2 * pair + 1) * WINDOW, (2 * pair + 2) * WINDOW)
            p2_t = jnp.concatenate([pb[:, ce], pb[:, co]], axis=0)
            o2_t = _dot(vv_t, p2_t)
            o2_t = o2_t * jnp.where(top_rows, inv[:, ce], inv[:, co])
            cols.append(jnp.transpose(o2_t))
        yield
    return jnp.concatenate(cols, axis=1)


def _mixer_kernel(sinks_ref, x_ref, s0_ref, k0_ref, v0_ref, base0_ref, bias_ref,
                  g_mix_ref, w_in_ref, w_a_up_ref, b_a_ref, g_gla_ref, g_swa_ref, w_out_ref,
                  g_ffn_ref, w_r_ref, b_r_ref,
                  xmid_ref, hp_ref, topi_ref, gate_ref, rank_ref, sout_ref, kout_ref, vout_ref, cnt_ref,
                  s_scr, k_scr, v_scr, base_scr, *, n_lead_pad, prev_valid_from):
    g_id, j = pl.program_id(0), pl.program_id(1)
    n_streams, tm = x_ref.shape[1], x_ref.shape[2]

    @pl.when(j == 0)
    def _():
        for s in range(n_streams):
            s_scr[s] = s0_ref[...]
            k_scr[s] = k0_ref[...]
            v_scr[s] = v0_ref[...]

    @pl.when((j == 0) & (g_id == 0))
    def _():
        base_scr[...] = base0_ref[...]

    diag = [(slice(h * GLA_DK, (h + 1) * GLA_DK), slice(h * GLA_DV, (h + 1) * GLA_DV)) for h in range(GLA_HEADS)]

    def stream(s):
        x = x_ref[0, s]
        p = yield from _project_steps(x, g_mix_ref[...], w_in_ref, w_a_up_ref[...], b_a_ref[...])
        yield "mix"
        o_gla, s_blocks = yield from _gla_chunks(p, j * tm, [s_scr[s, r, c] for r, c in diag], n_lead_pad)
        for (r, c), blk in zip(diag, s_blocks):
            s_scr[s, r, c] = blk
        o_parts = []
        for sb in range(tm // WINDOW):
            rs = slice(sb * WINDOW, (sb + 1) * WINDOW)
            k_blk, v_blk = p["sk"][rs], p["sv"][rs]
            k_prev = k_scr[s] if sb == 0 else p["sk"][(sb - 1) * WINDOW:sb * WINDOW]
            v_prev = v_scr[s] if sb == 0 else p["sv"][(sb - 1) * WINDOW:sb * WINDOW]
            valid = None
            if sb == 0 and prev_valid_from:
                first = jnp.where(j == 0, prev_valid_from, 0)
                valid = _iota((2 * WINDOW, SWA_GROUP * WINDOW), 0) >= first
            o_parts.append((yield from _swa_block(
                p["sq"][rs], jnp.concatenate([k_prev, k_blk], axis=0),
                jnp.concatenate([v_prev, v_blk], axis=0), bias_ref, sinks_ref, valid)))
        o_swa = jnp.concatenate(o_parts, axis=0)
        k_scr[s] = p["sk"][tm - WINDOW:tm]
        v_scr[s] = p["sv"][tm - WINDOW:tm]
        yield "tail"
        x_mid, hp, topi, gate8, rank8, base = yield from _tail_steps(
            x, o_gla, p["gr"], o_swa, g_gla_ref[...], g_swa_ref[...], w_out_ref,
            g_ffn_ref[...], w_r_ref[...], b_r_ref[...], lambda: base_scr[...])
        base_scr[...] = base
        xmid_ref[0, s] = x_mid
        _store_slabs(hp_ref.at[0, s], hp)
        topi_ref[0, s] = topi
        gate_ref[0, s] = gate8
        rank_ref[0, s] = rank8
        sout_ref[0, s] = s_scr[s]
        kout_ref[0, s] = p["sk"][tm - WINDOW:tm]
        vout_ref[0, s] = p["sv"][tm - WINDOW:tm]
        cnt_ref[...] = base

    def advance(gen, stop):
        try:
            while next(gen) != stop or stop is None:
                pass
            return False
        except StopIteration:
            return True

    def alternate(gen_a, stop_a, gen_b, stop_b):
        done_a = done_b = False
        while not (done_a and done_b):
            if not done_a:
                try:
                    done_a = next(gen_a) == stop_a and stop_a is not None
                except StopIteration:
                    done_a = True
            if not done_b:
                try:
                    done_b = next(gen_b) == stop_b and stop_b is not None
                except StopIteration:
                    done_b = True

    if n_streams == 1:
        advance(stream(0), None)
    else:
        first, second = stream(0), stream(1)
        advance(first, "mix")
        alternate(first, "tail", second, "mix")
        alternate(first, None, second, "tail")
        advance(second, None)


def _full_spec(shape):
    nd = len(shape)
    return pl.BlockSpec(shape, lambda *_: (0,) * nd)


def _mixer_call(x, s0, k0, v0, base0, wts, tm, n_streams, n_lead_pad, prev_valid_from):
    B, L, _ = x.shape
    S = n_streams
    G = B // S
    nj = L // tm
    weight_args = (wts["bias"], wts["g_mix"], wts["w_in"], wts["w_a_up"], wts["b_a"], wts["g_gla"],
                   wts["g_swa"], wts["w_out"], wts["g_ffn"], wts["w_r"], wts["b_r"])
    in_specs = [
        pl.BlockSpec(memory_space=pltpu.SMEM),
        pl.BlockSpec((1, S, tm, D_MODEL), lambda g, j: (g, 0, j, 0)),
        _full_spec(s0.shape), _full_spec(k0.shape), _full_spec(v0.shape), _full_spec(base0.shape),
    ] + [_full_spec(w.shape) for w in weight_args]
    tok_spec = pl.BlockSpec((1, S, 8, tm), lambda g, j: (g, 0, 0, j))

    def per_seq_spec(rows, cols):
        return pl.BlockSpec((1, S, rows, cols), lambda g, j: (g, 0, 0, 0))

    out_specs = [
        pl.BlockSpec((1, S, tm, D_MODEL), lambda g, j: (g, 0, j, 0)),
        pl.BlockSpec((1, S, tm * SLAB, LANES), lambda g, j: (g, 0, j, 0)),
        tok_spec, tok_spec, tok_spec,
        per_seq_spec(GLA_QK, GLA_V), per_seq_spec(WINDOW, SWA_KV), per_seq_spec(WINDOW, SWA_KV),
        _full_spec((N_EXPERTS, LANES)),
    ]
    out_shape = [
        jax.ShapeDtypeStruct((G, S, L, D_MODEL), F32),
        jax.ShapeDtypeStruct((G, S, L * SLAB, LANES), jnp.int32),
        jax.ShapeDtypeStruct((G, S, 8, L), jnp.int32),
        jax.ShapeDtypeStruct((G, S, 8, L), F32),
        jax.ShapeDtypeStruct((G, S, 8, L), jnp.int32),
        jax.ShapeDtypeStruct((G, S, GLA_QK, GLA_V), F32),
        jax.ShapeDtypeStruct((G, S, WINDOW, SWA_KV), F32),
        jax.ShapeDtypeStruct((G, S, WINDOW, SWA_KV), F32),
        jax.ShapeDtypeStruct((N_EXPERTS, LANES), F32),
    ]
    kern = functools.partial(_mixer_kernel, n_lead_pad=n_lead_pad, prev_valid_from=prev_valid_from)
    xmid, hp, topi, gate, rank, s_out, k_out, v_out, cnt = pl.pallas_call(
        kern,
        grid=(G, nj),
        in_specs=in_specs,
        out_specs=out_specs,
        out_shape=out_shape,
        scratch_shapes=[pltpu.VMEM((S, GLA_QK, GLA_V), F32), pltpu.VMEM((S, WINDOW, SWA_KV), F32),
                        pltpu.VMEM((S, WINDOW, SWA_KV), F32), pltpu.VMEM((N_EXPERTS, LANES), F32)],
        compiler_params=pltpu.CompilerParams(dimension_semantics=("arbitrary", "arbitrary"),
                                             vmem_limit_bytes=VMEM_LIMIT),
        name="mixer",
    )(wts["sinks"], x.reshape(G, S, L, D_MODEL), s0, k0, v0, base0, *weight_args)

    def rows8(a):
        return jnp.transpose(a, (2, 0, 1, 3)).reshape(8, B * L)

    return (xmid.reshape(B, L, D_MODEL), hp.reshape(B * L * SLAB, LANES), rows8(topi), rows8(gate), rows8(rank),
            s_out.reshape(B, GLA_QK, GLA_V), k_out.reshape(B, WINDOW, SWA_KV), v_out.reshape(B, WINDOW, SWA_KV), cnt)


def _decode_kernel(sinks_ref, x_ref, st_ref, ck_ref, cv_ref, base0_ref, bias_ref,
                   g_mix_ref, w_in_ref, w_a_up_ref, b_a_ref, g_gla_ref, g_swa_ref, w_out_ref,
                   g_ffn_ref, w_r_ref, b_r_ref,
                   xmid_ref, hp_ref, topi_ref, gate_ref, rank_ref, sto_ref, cko_ref, cvo_ref, cnt_ref,
                   tq_scr, gv_scr, gr_scr, sq_scr, sk_scr, sv_scr, og_scr, os_scr):
    i = pl.program_id(0)
    n_seq = x_ref.shape[0]

    @pl.when(i == 0)
    def _():
        p = _project(x_ref[...], g_mix_ref[...], w_in_ref, w_a_up_ref[...], b_a_ref[...])
        a_hi, a_mid, a_lo = _split3(jnp.transpose(jnp.exp(p["log_a"])))
        tq_scr[...] = jnp.concatenate(
            [a_hi, a_mid, a_lo, jnp.transpose(p["gk"]).astype(BF16), jnp.transpose(p["gq"]).astype(BF16)], axis=0)
        gv_scr[...] = p["gv"]
        gr_scr[...] = p["gr"]
        sq_scr[...] = p["sq"]
        sk_scr[...] = p["sk"]
        sv_scr[...] = p["sv"]

    seq_row = _iota((n_seq, LANES), 0)
    half = _iota((1, LANES), 1) < SWA_HEAD_DIM
    row_id = _iota((WINDOW, SWA_KV), 0)
    head_diag = (_iota((16, SWA_Q), 1) // SWA_HEAD_DIM) == _iota((16, SWA_Q), 0)
    sink_col = jnp.concatenate(
        [jnp.full((1, 1), sinks_ref[h], F32) for h in range(SWA_HEADS)] + [jnp.zeros((8, 1), F32)], axis=0)

    def per_seq(sl, carry):
        s = i * DEC_SB + sl
        pick = (seq_row == s).astype(BF16)
        cols = _dot(tq_scr[...], pick)
        a_c = cols[0:GLA_QK] + cols[GLA_QK:2 * GLA_QK] + cols[2 * GLA_QK:3 * GLA_QK]
        k_c = cols[3 * GLA_QK:4 * GLA_QK]
        q_c = cols[4 * GLA_QK:5 * GLA_QK]
        st = st_ref[sl].reshape(GLA_QK, GLA_DV)
        v_row = gv_scr[pl.ds(s, 1), :]
        v_b = jnp.concatenate(
            [jnp.broadcast_to(v_row[:, h * GLA_DV:(h + 1) * GLA_DV], (GLA_DK, GLA_DV))
             for h in range(GLA_HEADS)], axis=0)
        st_new = a_c * st + k_c * v_b
        sto_ref[sl] = st_new.reshape(GLA_HEADS, GLA_DK, GLA_DV)
        t = q_c * st_new
        og_scr[pl.ds(s, 1), :] = jnp.concatenate(
            [jnp.sum(t[h * GLA_DK:(h + 1) * GLA_DK], axis=0, keepdims=True) for h in range(GLA_HEADS)],
            axis=1)

        k_new = sk_scr[pl.ds(s, 1), :]
        v_new = sv_scr[pl.ds(s, 1), :]
        kn = jnp.where(row_id == WINDOW - 1, k_new, pltpu.roll(ck_ref[sl], WINDOW - 1, 0))
        vn = jnp.where(row_id == WINDOW - 1, v_new, pltpu.roll(cv_ref[sl], WINDOW - 1, 0))
        cko_ref[sl] = kn
        cvo_ref[sl] = vn
        kr, vr = pltpu.roll(kn, SWA_HEAD_DIM, 1), pltpu.roll(vn, SWA_HEAD_DIM, 1)
        k0, k1 = jnp.where(half, kn, kr), jnp.where(half, kr, kn)
        v0, v1 = jnp.where(half, vn, vr), jnp.where(half, vr, vn)
        kw = jnp.concatenate([k0, k0, k1, k1], axis=1).astype(BF16)
        vw = jnp.concatenate([v0, v0, v1, v1], axis=1).astype(BF16)
        q_row = sq_scr[pl.ds(s, 1), :]
        qm = jnp.where(head_diag, jnp.broadcast_to(q_row, (16, SWA_Q)), 0.0).astype(BF16)
        sc = _dot_nt(qm, kw) + bias_ref[...]
        m = jnp.maximum(jnp.max(sc, axis=1, keepdims=True), sink_col)
        pr = jnp.exp(sc - m)
        inv = 1.0 / (jnp.sum(pr, axis=1, keepdims=True) + jnp.exp(sink_col - m))
        ow = _dot(pr.astype(BF16), vw) * inv
        os_scr[pl.ds(s, 1), :] = jnp.sum(jnp.where(head_diag, ow, 0.0), axis=0, keepdims=True)
        return carry

    lax.fori_loop(0, DEC_SB, per_seq, 0, unroll=DEC_UNROLL)

    @pl.when(i == pl.num_programs(0) - 1)
    def _():
        x_mid, hp, topi, gate8, rank8, base = _tail(
            x_ref[...], og_scr[...], gr_scr[...], os_scr[...], g_gla_ref[...], g_swa_ref[...],
            w_out_ref, g_ffn_ref[...], w_r_ref[...], b_r_ref[...], lambda: base0_ref[...])
        xmid_ref[...] = x_mid
        _store_slabs(hp_ref, hp)
        topi_ref[...] = topi
        gate_ref[...] = gate8
        rank_ref[...] = rank8
        cnt_ref[...] = base


def _decode_call(xs, state, ck, cv, base0, bias_dec, wts):
    n_seq = xs.shape[0]
    nb = n_seq // DEC_SB
    weight_args = (wts["g_mix"], wts["w_in"], wts["w_a_up"], wts["b_a"], wts["g_gla"],
                   wts["g_swa"], wts["w_out"], wts["g_ffn"], wts["w_r"], wts["b_r"])
    in_specs = [
        pl.BlockSpec(memory_space=pltpu.SMEM),
        _full_spec(xs.shape),
        pl.BlockSpec((DEC_SB, GLA_HEADS, GLA_DK, GLA_DV), lambda i: (i, 0, 0, 0)),
        pl.BlockSpec((DEC_SB, WINDOW, SWA_KV), lambda i: (i, 0, 0)),
        pl.BlockSpec((DEC_SB, WINDOW, SWA_KV), lambda i: (i, 0, 0)),
        _full_spec(base0.shape), _full_spec(bias_dec.shape),
    ] + [_full_spec(w.shape) for w in weight_args]
    out_specs = [
        _full_spec((n_seq, D_MODEL)),
        _full_spec((n_seq * SLAB, LANES)),
        _full_spec((8, n_seq)), _full_spec((8, n_seq)), _full_spec((8, n_seq)),
        pl.BlockSpec((DEC_SB, GLA_HEADS, GLA_DK, GLA_DV), lambda i: (i, 0, 0, 0)),
        pl.BlockSpec((DEC_SB, WINDOW, SWA_KV), lambda i: (i, 0, 0)),
        pl.BlockSpec((DEC_SB, WINDOW, SWA_KV), lambda i: (i, 0, 0)),
        _full_spec((N_EXPERTS, LANES)),
    ]
    out_shape = [
        jax.ShapeDtypeStruct((n_seq, D_MODEL), F32),
        jax.ShapeDtypeStruct((n_seq * SLAB, LANES), jnp.int32),
        jax.ShapeDtypeStruct((8, n_seq), jnp.int32),
        jax.ShapeDtypeStruct((8, n_seq), F32),
        jax.ShapeDtypeStruct((8, n_seq), jnp.int32),
        jax.ShapeDtypeStruct(state.shape, F32),
        jax.ShapeDtypeStruct(ck.shape, F32),
        jax.ShapeDtypeStruct(cv.shape, F32),
        jax.ShapeDtypeStruct((N_EXPERTS, LANES), F32),
    ]
    scratch = [pltpu.VMEM((5 * GLA_QK, n_seq), BF16)] + [
        pltpu.VMEM((n_seq, GLA_V), F32), pltpu.VMEM((n_seq, GLA_V), F32), pltpu.VMEM((n_seq, SWA_Q), F32),
        pltpu.VMEM((n_seq, SWA_KV), F32), pltpu.VMEM((n_seq, SWA_KV), F32),
        pltpu.VMEM((n_seq, GLA_V), F32), pltpu.VMEM((n_seq, SWA_Q), F32)]
    return pl.pallas_call(
        _decode_kernel,
        grid=(nb,),
        in_specs=in_specs,
        out_specs=out_specs,
        out_shape=out_shape,
        scratch_shapes=scratch,
        compiler_params=pltpu.CompilerParams(dimension_semantics=("arbitrary",),
                                             vmem_limit_bytes=VMEM_LIMIT),
        name="decode",
    )(wts["sinks"], xs, state, ck, cv, base0, bias_dec, *weight_args)


SC_CORES = 2
SC_SUBCORES = 16
SC_WORKERS = SC_CORES * SC_SUBCORES
SC_SCATTER_ROWS = 64
SC_GATHER_ROWS = 96


def _sc_mesh():
    return plsc.VectorSubcoreMesh(core_axis_name="c", subcore_axis_name="s")


def _sc_worker_id():
    return lax.axis_index("s") * SC_CORES + lax.axis_index("c")


def _sc_scatter_rows(src_p, src_s, idx_p, idx_s, n_out):
    rows = SC_SCATTER_ROWS
    n_chunks = idx_p.shape[0] // SC_WORKERS
    n_s, _, rows_s = idx_s.shape
    assert n_chunks * SC_WORKERS == idx_p.shape[0] and n_chunks % 2 == 0 and n_s <= SC_WORKERS

    @functools.partial(
        pl.kernel, mesh=_sc_mesh(),
        out_type=jax.ShapeDtypeStruct((n_out, SLAB, LANES), jnp.int32),
        scratch_types=[pltpu.VMEM((2, TOP_K, rows), jnp.int32), pltpu.VMEM((2, rows, SLAB, LANES), jnp.int32),
                       pltpu.VMEM((TOP_K, rows_s), jnp.int32), pltpu.VMEM((rows_s, SLAB, LANES), jnp.int32),
                       pltpu.SemaphoreType.DMA((2,)), pltpu.SemaphoreType.DMA((2,))])
    def scatter_rows(srcp_hbm, srcs_hbm, idxp_hbm, idxs_hbm, out_hbm, idx_v, rows_v, idxs_v, rowss_v, lsem, ssem):
        wid = _sc_worker_id()

        def loads(c, b):
            g = wid * n_chunks + c
            return (pltpu.make_async_copy(idxp_hbm.at[g], idx_v.at[b], lsem.at[b]),
                    pltpu.make_async_copy(srcp_hbm.at[pl.ds(pl.multiple_of(g * rows, 8), rows)], rows_v.at[b],
                                          lsem.at[b]))

        def scatters(b):
            return [pltpu.make_async_copy(rows_v.at[b], out_hbm.at[idx_v.at[b, k]], ssem.at[b])
                    for k in range(TOP_K)]

        for d in loads(0, 0):
            d.start()

        @pl.loop(0, n_chunks, step=2)
        def _(c0):
            for b in range(2):
                c = c0 + b
                for d in loads(c, b):
                    d.wait()

                @pl.when(c >= 1)
                def _():
                    for d in scatters(1 - b):
                        d.wait()

                @pl.when(c + 1 < n_chunks)
                def _():
                    for d in loads(c + 1, 1 - b):
                        d.start()

                for d in scatters(b):
                    d.start()

        for d in scatters((n_chunks - 1) % 2):
            d.wait()

        @pl.when(wid < n_s)
        def _():
            pltpu.sync_copy(idxs_hbm.at[wid], idxs_v)
            pltpu.sync_copy(srcs_hbm.at[pl.ds(pl.multiple_of(wid * rows_s, 8), rows_s)], rowss_v)
            for k in range(TOP_K):
                pltpu.sync_copy(rowss_v, out_hbm.at[idxs_v.at[k]])

    return scatter_rows(src_p, src_s, idx_p, idx_s)


def _sc_gather_rows(src3, idx2):
    rows = SC_GATHER_ROWS
    n_chunks = idx2.shape[0] // SC_WORKERS
    assert n_chunks * SC_WORKERS == idx2.shape[0] and idx2.shape[1] == rows and n_chunks % 2 == 0

    @functools.partial(
        pl.kernel, mesh=_sc_mesh(),
        out_type=jax.ShapeDtypeStruct((idx2.shape[0] * rows, SLAB, LANES), jnp.int32),
        scratch_types=[pltpu.VMEM((2, rows), jnp.int32), pltpu.VMEM((2, rows, SLAB, LANES), jnp.int32),
                       pltpu.SemaphoreType.DMA((2,)), pltpu.SemaphoreType.DMA((2,))])
    def gather_rows(src_hbm, idx_hbm, out_hbm, idx_v, rows_v, gsem, wsem):
        wid = _sc_worker_id()

        def gather(b):
            return pltpu.make_async_copy(src_hbm.at[idx_v.at[b]], rows_v.at[b], gsem.at[b])

        def write(c, b):
            base = pl.multiple_of((wid * n_chunks + c) * rows, 8)
            return pltpu.make_async_copy(rows_v.at[b], out_hbm.at[pl.ds(base, rows)], wsem.at[b])

        pltpu.sync_copy(idx_hbm.at[wid * n_chunks], idx_v.at[0])
        gather(0).start()

        @pl.loop(0, n_chunks, step=2)
        def _(c0):
            for b in range(2):
                c = c0 + b

                @pl.when(c + 1 < n_chunks)
                def _():
                    @pl.when(c >= 1)
                    def _():
                        write(c - 1, 1 - b).wait()
                    pltpu.sync_copy(idx_hbm.at[wid * n_chunks + c + 1], idx_v.at[1 - b])
                    gather(1 - b).start()

                gather(b).wait()
                write(c, b).start()

        write(n_chunks - 2, 0).wait()
        write(n_chunks - 1, 1).wait()

    return gather_rows(src3, idx2)


FF_TILE = 256


def _ffn_kernel(blk_e_ref, nused_ref, x_ref, wu_ref, bu_ref, wd_ref, bd_ref, y_ref, xbf, actbf, wu_bf, wd_bf):
    i = pl.program_id(0)
    tm = xbf.shape[0]
    n_tiles = D_FF // FF_TILE

    @pl.when(i < nused_ref[0])
    def _():
        @pl.when((i == 0) | (blk_e_ref[i] != blk_e_ref[jnp.maximum(i - 1, 0)]))
        def _():
            wu_bf[...] = wu_ref[0].astype(BF16)
            wd_bf[...] = wd_ref[0].astype(BF16)

        for c in range(SLAB):
            lo, hi = _load_slab_chunk(x_ref, tm, c)
            xbf[:, c * LANES:(c + 1) * LANES] = lo.astype(BF16)
            xbf[:, HALF_D + c * LANES:HALF_D + (c + 1) * LANES] = hi.astype(BF16)
        for n in range(n_tiles):
            gc = slice(n * FF_TILE, (n + 1) * FF_TILE)
            lc = slice(D_FF + n * FF_TILE, D_FF + (n + 1) * FF_TILE)
            g = jnp.minimum(_dot(xbf[...], wu_bf[:, gc]) + bu_ref[0, :, gc], SWIGLU_LIMIT)
            lin = jnp.clip(_dot(xbf[...], wu_bf[:, lc]) + bu_ref[0, :, lc], -SWIGLU_LIMIT, SWIGLU_LIMIT)
            actbf[:, gc] = (g * jax.nn.sigmoid(SWIGLU_ALPHA * g) * (lin + 1.0)).astype(BF16)
        per_tile = FF_TILE // LANES
        for n in range(n_tiles // 2):
            yl = slice(n * FF_TILE, (n + 1) * FF_TILE)
            yh = slice(HALF_D + n * FF_TILE, HALF_D + (n + 1) * FF_TILE)
            y_lo = _dot(actbf[...], wd_bf[:, yl]) + bd_ref[0, :, yl]
            y_hi = _dot(actbf[...], wd_bf[:, yh]) + bd_ref[0, :, yh]
            for c in range(per_tile):
                sl = slice(c * LANES, (c + 1) * LANES)
                y_ref[pl.ds(n * per_tile + c, tm, stride=SLAB), :] = _pack_pair(y_lo[:, sl], y_hi[:, sl])

    @pl.when(i >= nused_ref[0])
    def _():
        y_ref[...] = jnp.zeros_like(y_ref)


def _ffn_call(blk_e, nused, xs2, w_up, b_up, w_down, b_down, tm):
    n_blocks = blk_e.shape[0]
    row_blk = pl.BlockSpec((tm * SLAB, LANES), lambda i, be, nu: (i, 0))
    grid_spec = pltpu.PrefetchScalarGridSpec(
        num_scalar_prefetch=2,
        grid=(n_blocks,),
        in_specs=[
            pl.BlockSpec((tm * SLAB, LANES), lambda i, be, nu: (jnp.minimum(i, nu[0] - 1), 0)),
            pl.BlockSpec((1, D_MODEL, 2 * D_FF), lambda i, be, nu: (be[i], 0, 0)),
            pl.BlockSpec((1, 1, 2 * D_FF), lambda i, be, nu: (be[i], 0, 0)),
            pl.BlockSpec((1, D_FF, D_MODEL), lambda i, be, nu: (be[i], 0, 0)),
            pl.BlockSpec((1, 1, D_MODEL), lambda i, be, nu: (be[i], 0, 0)),
        ],
        out_specs=row_blk,
        scratch_shapes=[pltpu.VMEM((tm, D_MODEL), BF16), pltpu.VMEM((tm, D_FF), BF16),
                        pltpu.VMEM((D_MODEL, 2 * D_FF), BF16), pltpu.VMEM((D_FF, D_MODEL), BF16)],
    )
    return pl.pallas_call(
        _ffn_kernel,
        grid_spec=grid_spec,
        out_shape=jax.ShapeDtypeStruct((n_blocks * tm * SLAB, LANES), jnp.int32),
        compiler_params=pltpu.CompilerParams(dimension_semantics=("arbitrary",),
                                             vmem_limit_bytes=VMEM_LIMIT),
        name="experts",
    )(blk_e, nused, xs2, w_up, b_up.reshape(N_EXPERTS, 1, 2 * D_FF), w_down, b_down.reshape(N_EXPERTS, 1, D_MODEL))


def _combine_kernel(ys0_ref, ys1_ref, ys2_ref, ys3_ref, xmid_ref, gate_ref, g_final_ref, y_ref):
    tm = xmid_ref.shape[0]
    gts = jnp.transpose(jnp.concatenate([gate_ref[...], jnp.zeros((LANES - 8, tm), F32)], axis=0))
    lows, highs = [], []
    for c in range(SLAB):
        acc_lo = xmid_ref[:, c * LANES:(c + 1) * LANES]
        acc_hi = xmid_ref[:, HALF_D + c * LANES:HALF_D + (c + 1) * LANES]
        for k, ys_ref in enumerate((ys0_ref, ys1_ref, ys2_ref, ys3_ref)):
            lo, hi = _load_slab_chunk(ys_ref, tm, c)
            acc_lo = acc_lo + lo * gts[:, k:k + 1]
            acc_hi = acc_hi + hi * gts[:, k:k + 1]
        lows.append(acc_lo)
        highs.append(acc_hi)
    y_ref[...] = _rms(jnp.concatenate(lows + highs, axis=1), g_final_ref[...])


def _combine_call(ys4, t_stride, row0, x_mid, gates, g_final, tm):
    T = x_mid.shape[0]
    blk0 = row0 // tm
    per_k = t_stride // tm
    assert per_k * tm == t_stride and blk0 * tm == row0

    def ys_spec(k):
        return pl.BlockSpec((tm * SLAB, LANES), lambda i: (k * per_k + blk0 + i, 0))

    return pl.pallas_call(
        _combine_kernel,
        grid=(T // tm,),
        in_specs=[
            ys_spec(0), ys_spec(1), ys_spec(2), ys_spec(3),
            pl.BlockSpec((tm, D_MODEL), lambda i: (i, 0)),
            pl.BlockSpec((8, tm), lambda i: (0, i)),
            _full_spec((1, D_MODEL)),
        ],
        out_specs=pl.BlockSpec((tm, D_MODEL), lambda i: (i, 0)),
        out_shape=jax.ShapeDtypeStruct((T, D_MODEL), F32),
        compiler_params=pltpu.CompilerParams(dimension_semantics=("arbitrary",),
                                             vmem_limit_bytes=VMEM_LIMIT),
        name="combine",
    )(ys4, ys4, ys4, ys4, x_mid, gates, g_final)


def _t5_bucket(dist):
    n = jnp.maximum(dist, 0)
    max_exact = NUM_BUCKETS // 2
    nf = jnp.maximum(n, 1).astype(F32)
    large = max_exact + (jnp.log(nf / max_exact) / math.log(MAX_DISTANCE / max_exact)
                         * (NUM_BUCKETS - max_exact)).astype(jnp.int32)
    large = jnp.minimum(large, NUM_BUCKETS - 1)
    return jnp.where(n < max_exact, n, large)


def kernel(x_prompt, x_sample, state_gla, cache_swa_k, cache_swa_v, meta_tokens, rel_bias_table,
           g_mix, w_in, w_a_up, b_a, g_gla_out, g_swa_out, attn_sinks, w_out,
           g_ffn, w_router, b_router, w_up, b_up, w_down, b_down, g_final):
    assert g_mix.shape[0] == 1, "single-layer trunk"
    B, L, _ = x_prompt.shape
    n_seq = x_sample.shape[0]
    TP = B * L
    T_all = TP + n_seq

    wi = w_in[0]
    sizes = (GLA_QK, GLA_QK, GLA_V, GLA_V, GLA_LOWRANK, SWA_Q, SWA_KV, SWA_KV)
    offs = [0]
    for s in sizes:
        offs.append(offs[-1] + s)
    seg = [wi[:, offs[n]:offs[n + 1]] for n in range(8)]
    seg[0] = seg[0] * (GLA_DK ** -0.5)
    seg[5] = seg[5] * (SWA_HEAD_DIM ** -0.5)
    w_in_r = jnp.concatenate(
        seg[0:4] + seg[5:8] + [seg[4], jnp.zeros((D_MODEL, LANES - GLA_LOWRANK), F32)], axis=1).astype(BF16)
    w_a_pad = jnp.concatenate([w_a_up[0], jnp.zeros((LANES - GLA_LOWRANK, GLA_QK), F32)], axis=0).astype(BF16)
    wr_t = jnp.transpose(w_router[0])
    wr_hi = wr_t.astype(BF16)
    wr_lo = (wr_t - wr_hi.astype(F32)).astype(BF16)
    qi = jnp.arange(WINDOW)[:, None]
    kj = jnp.arange(2 * WINDOW)[None, :]
    buckets = jnp.arange(NUM_BUCKETS)
    table = rel_bias_table.astype(F32)
    oh_p = (_t5_bucket(qi - kj + WINDOW)[..., None] == buckets).astype(F32)
    bias_p = jnp.einsum("qkb,bh->hkq", oh_p, table, precision=lax.Precision.HIGHEST)
    in_window = jnp.transpose((kj > qi) & (kj <= qi + WINDOW))
    bias_p = jnp.where(in_window[None], bias_p, NEG_INF)
    bias_p = bias_p.reshape(SWA_KV_HEADS, SWA_GROUP, 2 * WINDOW, WINDOW).transpose(0, 2, 1, 3)
    bias_p = bias_p.reshape(SWA_KV_HEADS, 2 * WINDOW, SWA_GROUP * WINDOW)
    oh_d = (_t5_bucket(WINDOW - 1 - jnp.arange(WINDOW))[:, None] == buckets).astype(F32)
    bias_d = jnp.einsum("rb,bh->hr", oh_d, table, precision=lax.Precision.HIGHEST)
    bias_d = jnp.concatenate([bias_d, jnp.zeros((8, WINDOW), F32)], axis=0)
    wts = dict(
        sinks=attn_sinks[0].astype(F32), bias=bias_p,
        g_mix=g_mix[0][None], w_in=w_in_r, w_a_up=w_a_pad, b_a=b_a[0][None],
        g_gla=g_gla_out[0][None], g_swa=g_swa_out[0][None], w_out=w_out[0].astype(BF16),
        g_ffn=g_ffn[0][None], w_r=jnp.concatenate([wr_hi, wr_lo], axis=0), b_r=b_router[0][:, None],
    )

    x_pre = jnp.concatenate([jnp.zeros((WINDOW - N_META, D_MODEL), F32), meta_tokens.astype(F32)], axis=0)[None]
    zeros_s = jnp.zeros((GLA_QK, GLA_V), F32)
    zeros_kv = jnp.zeros((WINDOW, SWA_KV), F32)
    zeros_b = jnp.zeros((N_EXPERTS, LANES), F32)
    pre = _mixer_call(x_pre, zeros_s, zeros_kv, zeros_kv, zeros_b, wts, WINDOW, 1, WINDOW - N_META, 0)
    s_meta, k_meta, v_meta = pre[5][0], pre[6][0], pre[7][0]

    (xmid_p, hp_p, topi_p, gate_p, rank_p, s_p, k_p, v_p, cnt_p) = _mixer_call(
        x_prompt, s_meta, k_meta, v_meta, zeros_b, wts, MIX_STREAM_TM, 2, 0, WINDOW - N_META)

    (xmid_s, hp_s, topi_s, gate_s, rank_s, st_s, ck_s, cv_s, cnt_all) = _decode_call(
        x_sample[:, 0], state_gla[0], cache_swa_k[0].reshape(n_seq, WINDOW, SWA_KV),
        cache_swa_v[0].reshape(n_seq, WINDOW, SWA_KV), cnt_p, bias_d, wts)

    n_slots = T_all * TOP_K
    tm = -(-math.ceil(1.05 * n_slots / N_EXPERTS / MOE_BLOCKS_PER_EXPERT) // 16) * 16
    n_blocks = -(-n_slots // tm) + N_EXPERTS
    top_e = jnp.concatenate([topi_p[:TOP_K], topi_s[:TOP_K]], axis=1)
    rank = jnp.concatenate([rank_p[:TOP_K], rank_s[:TOP_K]], axis=1)
    counts = cnt_all[:, 0].astype(jnp.int32)
    padded = (counts + tm - 1) // tm * tm
    pad_end = jnp.cumsum(padded)
    pad_start = pad_end - padded
    e_ids = jnp.arange(N_EXPERTS, dtype=jnp.int32)
    dest = jnp.sum(jnp.where(top_e[..., None] == e_ids, pad_start, 0), axis=-1) + rank
    n_pad = n_blocks * tm
    blk_e = jnp.minimum(jnp.sum(pad_end[None] <= (jnp.arange(n_blocks, dtype=jnp.int32) * tm)[:, None], axis=1),
                        N_EXPERTS - 1).astype(jnp.int32)
    nused = (pad_end[-1] // tm).astype(jnp.int32).reshape(1)

    sample_rows = 8
    idx_p = dest[:, :TP].reshape(TOP_K, TP // SC_SCATTER_ROWS, SC_SCATTER_ROWS).transpose(1, 0, 2)
    idx_s = dest[:, TP:].reshape(TOP_K, n_seq // sample_rows, sample_rows).transpose(1, 0, 2)
    xs3 = _sc_scatter_rows(hp_p.reshape(TP, SLAB, LANES), hp_s.reshape(n_seq, SLAB, LANES), idx_p, idx_s, n_pad)
    ys2 = _ffn_call(blk_e, nused, xs3.reshape(-1, LANES), w_up[0], b_up[0], w_down[0], b_down[0], tm)
    unit = math.lcm(2 * SC_WORKERS * SC_GATHER_ROWS // TOP_K, MIX_TM)
    t_stride = -(-T_all // unit) * unit
    filler = jnp.arange(TOP_K * (t_stride - T_all), dtype=jnp.int32).reshape(TOP_K, t_stride - T_all)
    slot_src = jnp.concatenate([dest, filler], axis=1)
    slot_src = slot_src.reshape(TOP_K * t_stride // SC_GATHER_ROWS, SC_GATHER_ROWS)
    ys4 = _sc_gather_rows(ys2.reshape(-1, SLAB, LANES), slot_src).reshape(-1, LANES)

    gf = g_final[None]
    y_p = _combine_call(ys4, t_stride, 0, xmid_p.reshape(TP, D_MODEL), gate_p, gf, MIX_TM)
    y_s = _combine_call(ys4, t_stride, TP, xmid_s, gate_s, gf, n_seq)

    s_heads = jnp.stack([s_p[:, h * GLA_DK:(h + 1) * GLA_DK, h * GLA_DV:(h + 1) * GLA_DV]
                         for h in range(GLA_HEADS)], axis=1)
    return (y_p.reshape(B, L, D_MODEL), y_s.reshape(n_seq, 1, D_MODEL), s_heads[None],
            k_p.reshape(1, B, WINDOW, SWA_KV_HEADS, SWA_HEAD_DIM),
            v_p.reshape(1, B, WINDOW, SWA_KV_HEADS, SWA_HEAD_DIM),
            st_s[None], ck_s.reshape(1, n_seq, WINDOW, SWA_KV_HEADS, SWA_HEAD_DIM),
            cv_s.reshape(1, n_seq, WINDOW, SWA_KV_HEADS, SWA_HEAD_DIM))
```

```python
import functools
import math

import jax
import jax.numpy as jnp
from jax import lax
from jax.experimental import pallas as pl
from jax.experimental.pallas import tpu as pltpu
from jax.experimental.pallas import tpu_sc as plsc

D_MODEL = 1024
N_META = 16
GLA_HEADS = 4
GLA_DK = 64
GLA_DV = 128
GLA_LOWRANK = 16
GLA_GATE_TAU = 16.0
GLA_CHUNK = 64
SWA_HEADS = 8
SWA_KV_HEADS = 2
SWA_HEAD_DIM = 64
SWA_GROUP = SWA_HEADS // SWA_KV_HEADS
WINDOW = 128
NUM_BUCKETS = 32
MAX_DISTANCE = 128
N_EXPERTS = 32
TOP_K = 4
D_FF = 1024
SWIGLU_ALPHA = 1.702
SWIGLU_LIMIT = 7.0
RMS_EPS = 1e-6

GLA_QK = GLA_HEADS * GLA_DK
GLA_V = GLA_HEADS * GLA_DV
SWA_Q = SWA_HEADS * SWA_HEAD_DIM
SWA_KV = SWA_KV_HEADS * SWA_HEAD_DIM
LANES = 128
C_GQ, C_GK, C_GV, C_GR = 0, GLA_QK, 2 * GLA_QK, 2 * GLA_QK + GLA_V
C_SQ = C_GR + GLA_V
C_SK = C_SQ + SWA_Q
C_SV = C_SK + SWA_KV
C_GA = C_SV + SWA_KV
D_PROJ = C_GA + LANES

MIX_TM = 512
MIX_STREAM_TM = 512
MOE_BLOCKS_PER_EXPERT = 2
DEC_SB = 16
DEC_UNROLL = 8
VMEM_LIMIT = 56 * 1024 * 1024

F32 = jnp.float32
BF16 = jnp.bfloat16
NEG_INF = float("-inf")


def _dot(a, b):
    return jnp.dot(a, b, preferred_element_type=F32)


def _dot_nt(a, b):
    return lax.dot_general(a, b, (((1,), (1,)), ((), ())), preferred_element_type=F32)


def _split3(x):
    hi = x.astype(BF16)
    r1 = x - hi.astype(F32)
    mid = r1.astype(BF16)
    lo = (r1 - mid.astype(F32)).astype(BF16)
    return hi, mid, lo


def _rms(x, g):
    return x * lax.rsqrt(jnp.mean(x * x, axis=-1, keepdims=True) + RMS_EPS) * g


def _iota(shape, dim):
    return lax.broadcasted_iota(jnp.int32, shape, dim)


HALF_D = D_MODEL // 2
SLAB = HALF_D // LANES


def _pack_pair(lo, hi):
    bl = lax.bitcast_convert_type(lo.astype(BF16).astype(F32), jnp.uint32)
    bh = lax.bitcast_convert_type(hi.astype(BF16).astype(F32), jnp.uint32)
    return lax.bitcast_convert_type(bh | lax.shift_right_logical(bl, jnp.uint32(16)), jnp.int32)


def _unpack_pair(w):
    u = lax.bitcast_convert_type(w, jnp.uint32)
    lo = lax.bitcast_convert_type(lax.shift_left(u, jnp.uint32(16)), F32)
    hi = lax.bitcast_convert_type(u & jnp.uint32(0xFFFF0000), F32)
    return lo, hi


def _store_slabs(ref, x):
    rows = x.shape[0]
    for c in range(SLAB):
        sl = slice(c * LANES, (c + 1) * LANES)
        ref[pl.ds(c, rows, stride=SLAB), :] = _pack_pair(x[:, sl], x[:, HALF_D + c * LANES:HALF_D + (c + 1) * LANES])


def _load_slab_chunk(ref, rows, c):
    return _unpack_pair(ref[pl.ds(c, rows, stride=SLAB), :])


def _drain(steps):
    try:
        while True:
            next(steps)
    except StopIteration as done:
        return done.value


def _project_steps(x, g_mix, w_in_ref, w_a_up, b_a):
    h = _rms(x, g_mix).astype(BF16)
    yield

    def cols(lo, width, tile=2 * LANES):
        parts = []
        for off in range(0, width, tile):
            parts.append(_dot(h, w_in_ref[:, lo + off:lo + min(off + tile, width)]))
            yield
        return parts[0] if len(parts) == 1 else jnp.concatenate(parts, axis=1)

    ga = (yield from cols(C_GA, LANES)).astype(BF16)
    z = _dot(ga, w_a_up) + b_a
    log_a = -(jnp.maximum(-z, 0.0) + jnp.log(1.0 + jnp.exp(-jnp.abs(z)))) / GLA_GATE_TAU
    gqk = yield from cols(C_GQ, 2 * GLA_QK)
    gv = yield from cols(C_GV, GLA_V)
    swa = yield from cols(C_SQ, SWA_Q + 2 * SWA_KV)
    gr = yield from cols(C_GR, GLA_V)
    return dict(
        gq=gqk[:, :GLA_QK],
        gk=gqk[:, GLA_QK:],
        gv=gv,
        gr=gr,
        sq=swa[:, :SWA_Q],
        sk=swa[:, SWA_Q:SWA_Q + SWA_KV],
        sv=swa[:, SWA_Q + SWA_KV:],
        log_a=log_a,
    )


def _project(x, g_mix, w_in_ref, w_a_up, b_a):
    return _drain(_project_steps(x, g_mix, w_in_ref, w_a_up, b_a))


def _tail(*args):
    return _drain(_tail_steps(*args))


def _tail_steps(x, o_gla, gr, o_swa, g_gla_out, g_swa_out, w_out_ref, g_ffn, w_r, b_r, get_base):
    tm = x.shape[0]
    gate = gr * jax.nn.sigmoid(gr)
    parts = []
    for h in range(GLA_HEADS):
        sl = slice(h * GLA_DV, (h + 1) * GLA_DV)
        parts.append(_rms(o_gla[:, sl], g_gla_out) * gate[:, sl])
    og = jnp.concatenate(parts, axis=1).astype(BF16)
    yield
    os_ = _rms(o_swa, g_swa_out).astype(BF16)
    x_mid = x + _dot(og, w_out_ref[0:GLA_V])
    yield
    x_mid = x_mid + _dot(os_, w_out_ref[GLA_V:GLA_V + SWA_Q])
    yield
    hp = _rms(x_mid, g_ffn)

    h1 = hp.astype(BF16)
    h2 = (hp - h1.astype(F32)).astype(BF16)
    yield
    la = _dot_nt(w_r, h1)
    lb = _dot_nt(w_r[0:N_EXPERTS], h2)
    logits = la[0:N_EXPERTS] + la[N_EXPERTS:2 * N_EXPERTS] + lb + b_r
    yield

    eidx = _iota((N_EXPERTS, tm), 0)
    vals, idxs, onehots = [], [], []
    l = logits
    for _ in range(TOP_K):
        m = jnp.max(l, axis=0, keepdims=True)
        sel = jnp.min(jnp.where(l == m, eidx, N_EXPERTS), axis=0, keepdims=True)
        oh = eidx == sel
        l = jnp.where(oh, NEG_INF, l)
        vals.append(m)
        idxs.append(sel)
        onehots.append(oh)
    es = [jnp.exp(v - vals[0]) for v in vals]
    denom = es[0] + es[1] + es[2] + es[3]
    gates = [e / denom for e in es]

    ohf = jnp.concatenate([oh.astype(F32) for oh in onehots], axis=0)
    upper = (_iota((tm, tm), 0) < _iota((tm, tm), 1)).astype(BF16)
    prefix = _dot(ohf.astype(BF16), upper)
    yield
    base = get_base()
    ranks = []
    for k in range(TOP_K):
        sl = slice(k * N_EXPERTS, (k + 1) * N_EXPERTS)
        ohk = ohf[sl]
        base_t = jnp.concatenate([base] * (tm // LANES), axis=1)
        ranks.append(jnp.sum(ohk * (prefix[sl] + base_t), axis=0, keepdims=True))
        base = base + jnp.sum(ohk, axis=1, keepdims=True)
    zi = jnp.zeros((8 - TOP_K, tm), jnp.int32)
    zf = jnp.zeros((8 - TOP_K, tm), F32)
    topi = jnp.concatenate(idxs + [zi], axis=0)
    gate8 = jnp.concatenate(gates + [zf], axis=0)
    rank8 = jnp.concatenate([r.astype(jnp.int32) for r in ranks] + [zi], axis=0)
    return x_mid, hp, topi, gate8, rank8, base


def _gla_chunks(p, row0, s_blocks, n_lead_pad):
    tm = p["gq"].shape[0]
    nch = tm // GLA_CHUNK
    log_a = p["log_a"]
    if n_lead_pad:
        rows = row0 + _iota((tm, GLA_QK), 0)
        log_a = jnp.where(rows >= n_lead_pad, log_a, 0.0)
    ri, ci = _iota((tm, tm), 0), _iota((tm, tm), 1)
    tril = ((ri >= ci) & (ri // GLA_CHUNK == ci // GLA_CHUNK)).astype(BF16)
    hi, mid, lo = _split3(log_a)
    b_all = _dot(tril, hi) + _dot(tril, mid) + _dot(tril, lo)
    yield

    c64 = GLA_CHUNK
    kk_mask = (_iota((GLA_QK, GLA_QK), 0) // c64) == (_iota((GLA_QK, GLA_QK), 1) // GLA_DK)
    vv_mask = (_iota((GLA_QK, GLA_V), 0) // c64) == (_iota((GLA_QK, GLA_V), 1) // GLA_DV)
    zero_blk = jnp.zeros((GLA_DK, GLA_DV), BF16)
    causal = (_iota((c64, GLA_QK), 0) >= (_iota((c64, GLA_QK), 1) % c64)).astype(F32)
    zpad_k = jnp.zeros((LANES - c64, GLA_QK), F32)
    zpad_v = jnp.zeros((LANES - c64, GLA_V), BF16)

    outs = []
    for c in range(nch):
        rs = slice(c * c64, (c + 1) * c64)
        b = b_all[rs]
        q, k, v = p["gq"][rs], p["gk"][rs], p["gv"][rs]
        b_last = b[c64 - 1:c64]
        qt = (q * jnp.exp(b)).astype(BF16)
        kt = k * jnp.exp(-b)
        kd = k * jnp.exp(b_last - b)
        vb = v.astype(BF16)
        k_bd = jnp.where(kk_mask, jnp.concatenate([kt] * GLA_HEADS, axis=0), 0.0).astype(BF16)
        a = (_dot_nt(qt, k_bd) * causal).astype(BF16)
        v_bd = jnp.where(vv_mask, jnp.concatenate([vb] * GLA_HEADS, axis=0), jnp.zeros((), BF16))
        s_bd = jnp.concatenate(
            [jnp.concatenate([s_blocks[h].astype(BF16) if g == h else zero_blk for g in range(GLA_HEADS)], axis=1)
             for h in range(GLA_HEADS)], axis=0)
        outs.append(_dot(qt, s_bd) + _dot(a, v_bd))
        kd_t = jnp.transpose(jnp.concatenate([kd, zpad_k], axis=0)).astype(BF16)
        upd = _dot(kd_t, jnp.concatenate([vb, zpad_v], axis=0))
        decay = jnp.exp(jnp.transpose(jnp.broadcast_to(b_last, (LANES, GLA_QK))))
        s_blocks = [s_blocks[h] * decay[h * GLA_DK:(h + 1) * GLA_DK]
                    + upd[h * GLA_DK:(h + 1) * GLA_DK, h * GLA_DV:(h + 1) * GLA_DV] for h in range(GLA_HEADS)]
        yield
    return jnp.concatenate(outs, axis=0), s_blocks


def _swa_block(sq, kcat, vcat, bias_ref, sinks_ref, valid_t):
    half = _iota((1, LANES), 1) < SWA_HEAD_DIM
    top_rows = _iota((LANES, 1), 0) < SWA_HEAD_DIM
    k_roll = pltpu.roll(kcat, SWA_HEAD_DIM, 1)
    v_t = jnp.transpose(vcat)
    zeros_v = jnp.zeros((SWA_HEAD_DIM, 2 * WINDOW), F32)
    cols = []
    for kv in range(SWA_KV_HEADS):
        kk = jnp.where(half, kcat, k_roll) if kv == 0 else jnp.where(half, k_roll, kcat)
        q_parts = []
        for c in (2 * kv, 2 * kv + 1):
            qc = sq[:, c * LANES:(c + 1) * LANES]
            q_parts.append(jnp.where(half, qc, 0.0))
            q_parts.append(jnp.where(half, 0.0, qc))
        q_st = jnp.concatenate(q_parts, axis=0).astype(BF16)
        s = _dot_nt(kk.astype(BF16), q_st) + bias_ref[kv]
        if valid_t is not None:
            s = jnp.where(valid_t, s, NEG_INF)
        sink = jnp.concatenate(
            [jnp.full((1, WINDOW), sinks_ref[kv * SWA_GROUP + g], F32) for g in range(SWA_GROUP)], axis=1)
        m = jnp.maximum(jnp.max(s, axis=0, keepdims=True), sink)
        pr = jnp.exp(s - m)
        inv = 1.0 / (jnp.sum(pr, axis=0, keepdims=True) + jnp.exp(sink - m))
        pb = pr.astype(BF16)
        vk = v_t[kv * SWA_HEAD_DIM:(kv + 1) * SWA_HEAD_DIM]
        vv_t = jnp.concatenate([jnp.concatenate([vk, zeros_v], axis=1),
                                jnp.concatenate([zeros_v, vk], axis=1)], axis=0).astype(BF16)
        for pair in range(SWA_GROUP // 2):
            ce = slice(2 * pair * WINDOW, (2 * pair + 1) * WINDOW)
            co = slice((2 * pair + 1) * WINDOW, (2 * pair + 2) * WINDOW)
            p2_t = jnp.concatenate([pb[:, ce], pb[:, co]], axis=0)
            o2_t = _dot(vv_t, p2_t)
            o2_t = o2_t * jnp.where(top_rows, inv[:, ce], inv[:, co])
            cols.append(jnp.transpose(o2_t))
        yield
    return jnp.concatenate(cols, axis=1)


def _mixer_kernel(sinks_ref, x_ref, s0_ref, k0_ref, v0_ref, base0_ref, bias_ref,
                  g_mix_ref, w_in_ref, w_a_up_ref, b_a_ref, g_gla_ref, g_swa_ref, w_out_ref,
                  g_ffn_ref, w_r_ref, b_r_ref,
                  xmid_ref, hp_ref, topi_ref, gate_ref, rank_ref, sout_ref, kout_ref, vout_ref, cnt_ref,
                  s_scr, k_scr, v_scr, base_scr, *, n_lead_pad, prev_valid_from):
    g_id, j = pl.program_id(0), pl.program_id(1)
    n_streams, tm = x_ref.shape[1], x_ref.shape[2]

    @pl.when(j == 0)
    def _():
        for s in range(n_streams):
            s_scr[s] = s0_ref[...]
            k_scr[s] = k0_ref[...]
            v_scr[s] = v0_ref[...]

    @pl.when((j == 0) & (g_id == 0))
    def _():
        base_scr[...] = base0_ref[...]

    diag = [(slice(h * GLA_DK, (h + 1) * GLA_DK), slice(h * GLA_DV, (h + 1) * GLA_DV)) for h in range(GLA_HEADS)]

    def stream(s):
        x = x_ref[0, s]
        p = yield from _project_steps(x, g_mix_ref[...], w_in_ref, w_a_up_ref[...], b_a_ref[...])
        yield "mix"
        o_gla, s_blocks = yield from _gla_chunks(p, j * tm, [s_scr[s, r, c] for r, c in diag], n_lead_pad)
        for (r, c), blk in zip(diag, s_blocks):
            s_scr[s, r, c] = blk
        o_parts = []
        for sb in range(tm // WINDOW):
            rs = slice(sb * WINDOW, (sb + 1) * WINDOW)
            k_blk, v_blk = p["sk"][rs], p["sv"][rs]
            k_prev = k_scr[s] if sb == 0 else p["sk"][(sb - 1) * WINDOW:sb * WINDOW]
            v_prev = v_scr[s] if sb == 0 else p["sv"][(sb - 1) * WINDOW:sb * WINDOW]
            valid = None
            if sb == 0 and prev_valid_from:
                first = jnp.where(j == 0, prev_valid_from, 0)
                valid = _iota((2 * WINDOW, SWA_GROUP * WINDOW), 0) >= first
            o_parts.append((yield from _swa_block(
                p["sq"][rs], jnp.concatenate([k_prev, k_blk], axis=0),
                jnp.concatenate([v_prev, v_blk], axis=0), bias_ref, sinks_ref, valid)))
        o_swa = jnp.concatenate(o_parts, axis=0)
        k_scr[s] = p["sk"][tm - WINDOW:tm]
        v_scr[s] = p["sv"][tm - WINDOW:tm]
        yield "tail"
        x_mid, hp, topi, gate8, rank8, base = yield from _tail_steps(
            x, o_gla, p["gr"], o_swa, g_gla_ref[...], g_swa_ref[...], w_out_ref,
            g_ffn_ref[...], w_r_ref[...], b_r_ref[...], lambda: base_scr[...])
        base_scr[...] = base
        xmid_ref[0, s] = x_mid
        _store_slabs(hp_ref.at[0, s], hp)
        topi_ref[0, s] = topi
        gate_ref[0, s] = gate8
        rank_ref[0, s] = rank8
        sout_ref[0, s] = s_scr[s]
        kout_ref[0, s] = p["sk"][tm - WINDOW:tm]
        vout_ref[0, s] = p["sv"][tm - WINDOW:tm]
        cnt_ref[...] = base

    def advance(gen, stop):
        try:
            while next(gen) != stop or stop is None:
                pass
            return False
        except StopIteration:
            return True

    def alternate(gen_a, stop_a, gen_b, stop_b):
        done_a = done_b = False
        while not (done_a and done_b):
            if not done_a:
                try:
                    done_a = next(gen_a) == stop_a and stop_a is not None
                except StopIteration:
                    done_a = True
            if not done_b:
                try:
                    done_b = next(gen_b) == stop_b and stop_b is not None
                except StopIteration:
                    done_b = True

    if n_streams == 1:
        advance(stream(0), None)
    else:
        first, second = stream(0), stream(1)
        advance(first, "mix")
        alternate(first, "tail", second, "mix")
        alternate(first, None, second, "tail")
        advance(second, None)


def _full_spec(shape):
    nd = len(shape)
    return pl.BlockSpec(shape, lambda *_: (0,) * nd)


def _mixer_call(x, s0, k0, v0, base0, wts, tm, n_streams, n_lead_pad, prev_valid_from):
    B, L, _ = x.shape
    S = n_streams
    G = B // S
    nj = L // tm
    weight_args = (wts["bias"], wts["g_mix"], wts["w_in"], wts["w_a_up"], wts["b_a"], wts["g_gla"],
                   wts["g_swa"], wts["w_out"], wts["g_ffn"], wts["w_r"], wts["b_r"])
    in_specs = [
        pl.BlockSpec(memory_space=pltpu.SMEM),
        pl.BlockSpec((1, S, tm, D_MODEL), lambda g, j: (g, 0, j, 0)),
        _full_spec(s0.shape), _full_spec(k0.shape), _full_spec(v0.shape), _full_spec(base0.shape),
    ] + [_full_spec(w.shape) for w in weight_args]
    tok_spec = pl.BlockSpec((1, S, 8, tm), lambda g, j: (g, 0, 0, j))

    def per_seq_spec(rows, cols):
        return pl.BlockSpec((1, S, rows, cols), lambda g, j: (g, 0, 0, 0))

    out_specs = [
        pl.BlockSpec((1, S, tm, D_MODEL), lambda g, j: (g, 0, j, 0)),
        pl.BlockSpec((1, S, tm * SLAB, LANES), lambda g, j: (g, 0, j, 0)),
        tok_spec, tok_spec, tok_spec,
        per_seq_spec(GLA_QK, GLA_V), per_seq_spec(WINDOW, SWA_KV), per_seq_spec(WINDOW, SWA_KV),
        _full_spec((N_EXPERTS, LANES)),
    ]
    out_shape = [
        jax.ShapeDtypeStruct((G, S, L, D_MODEL), F32),
        jax.ShapeDtypeStruct((G, S, L * SLAB, LANES), jnp.int32),
        jax.ShapeDtypeStruct((G, S, 8, L), jnp.int32),
        jax.ShapeDtypeStruct((G, S, 8, L), F32),
        jax.ShapeDtypeStruct((G, S, 8, L), jnp.int32),
        jax.ShapeDtypeStruct((G, S, GLA_QK, GLA_V), F32),
        jax.ShapeDtypeStruct((G, S, WINDOW, SWA_KV), F32),
        jax.ShapeDtypeStruct((G, S, WINDOW, SWA_KV), F32),
        jax.ShapeDtypeStruct((N_EXPERTS, LANES), F32),
    ]
    kern = functools.partial(_mixer_kernel, n_lead_pad=n_lead_pad, prev_valid_from=prev_valid_from)
    xmid, hp, topi, gate, rank, s_out, k_out, v_out, cnt = pl.pallas_call(
        kern,
        grid=(G, nj),
        in_specs=in_specs,
        out_specs=out_specs,
        out_shape=out_shape,
        scratch_shapes=[pltpu.VMEM((S, GLA_QK, GLA_V), F32), pltpu.VMEM((S, WINDOW, SWA_KV), F32),
                        pltpu.VMEM((S, WINDOW, SWA_KV), F32), pltpu.VMEM((N_EXPERTS, LANES), F32)],
        compiler_params=pltpu.CompilerParams(dimension_semantics=("arbitrary", "arbitrary"),
                                             vmem_limit_bytes=VMEM_LIMIT),
        name="mixer",
    )(wts["sinks"], x.reshape(G, S, L, D_MODEL), s0, k0, v0, base0, *weight_args)

    def rows8(a):
        return jnp.transpose(a, (2, 0, 1, 3)).reshape(8, B * L)

    return (xmid.reshape(B, L, D_MODEL), hp.reshape(B * L * SLAB, LANES), rows8(topi), rows8(gate), rows8(rank),
            s_out.reshape(B, GLA_QK, GLA_V), k_out.reshape(B, WINDOW, SWA_KV), v_out.reshape(B, WINDOW, SWA_KV), cnt)


def _decode_kernel(sinks_ref, x_ref, st_ref, ck_ref, cv_ref, base0_ref, bias_ref,
                   g_mix_ref, w_in_ref, w_a_up_ref, b_a_ref, g_gla_ref, g_swa_ref, w_out_ref,
                   g_ffn_ref, w_r_ref, b_r_ref,
                   xmid_ref, hp_ref, topi_ref, gate_ref, rank_ref, sto_ref, cko_ref, cvo_ref, cnt_ref,
                   tq_scr, gv_scr, gr_scr, sq_scr, sk_scr, sv_scr, og_scr, os_scr):
    i = pl.program_id(0)
    n_seq = x_ref.shape[0]

    @pl.when(i == 0)
    def _():
        p = _project(x_ref[...], g_mix_ref[...], w_in_ref, w_a_up_ref[...], b_a_ref[...])
        a_hi, a_mid, a_lo = _split3(jnp.transpose(jnp.exp(p["log_a"])))
        tq_scr[...] = jnp.concatenate(
            [a_hi, a_mid, a_lo, jnp.transpose(p["gk"]).astype(BF16), jnp.transpose(p["gq"]).astype(BF16)], axis=0)
        gv_scr[...] = p["gv"]
        gr_scr[...] = p["gr"]
        sq_scr[...] = p["sq"]
        sk_scr[...] = p["sk"]
        sv_scr[...] = p["sv"]

    seq_row = _iota((n_seq, LANES), 0)
    half = _iota((1, LANES), 1) < SWA_HEAD_DIM
    row_id = _iota((WINDOW, SWA_KV), 0)
    head_diag = (_iota((16, SWA_Q), 1) // SWA_HEAD_DIM) == _iota((16, SWA_Q), 0)
    sink_col = jnp.concatenate(
        [jnp.full((1, 1), sinks_ref[h], F32) for h in range(SWA_HEADS)] + [jnp.zeros((8, 1), F32)], axis=0)

    def per_seq(sl, carry):
        s = i * DEC_SB + sl
        pick = (seq_row == s).astype(BF16)
        cols = _dot(tq_scr[...], pick)
        a_c = cols[0:GLA_QK] + cols[GLA_QK:2 * GLA_QK] + cols[2 * GLA_QK:3 * GLA_QK]
        k_c = cols[3 * GLA_QK:4 * GLA_QK]
        q_c = cols[4 * GLA_QK:5 * GLA_QK]
        st = st_ref[sl].reshape(GLA_QK, GLA_DV)
        v_row = gv_scr[pl.ds(s, 1), :]
        v_b = jnp.concatenate(
            [jnp.broadcast_to(v_row[:, h * GLA_DV:(h + 1) * GLA_DV], (GLA_DK, GLA_DV))
             for h in range(GLA_HEADS)], axis=0)
        st_new = a_c * st + k_c * v_b
        sto_ref[sl] = st_new.reshape(GLA_HEADS, GLA_DK, GLA_DV)
        t = q_c * st_new
        og_scr[pl.ds(s, 1), :] = jnp.concatenate(
            [jnp.sum(t[h * GLA_DK:(h + 1) * GLA_DK], axis=0, keepdims=True) for h in range(GLA_HEADS)],
            axis=1)

        k_new = sk_scr[pl.ds(s, 1), :]
        v_new = sv_scr[pl.ds(s, 1), :]
        kn = jnp.where(row_id == WINDOW - 1, k_new, pltpu.roll(ck_ref[sl], WINDOW - 1, 0))
        vn = jnp.where(row_id == WINDOW - 1, v_new, pltpu.roll(cv_ref[sl], WINDOW - 1, 0))
        cko_ref[sl] = kn
        cvo_ref[sl] = vn
        kr, vr = pltpu.roll(kn, SWA_HEAD_DIM, 1), pltpu.roll(vn, SWA_HEAD_DIM, 1)
        k0, k1 = jnp.where(half, kn, kr), jnp.where(half, kr, kn)
        v0, v1 = jnp.where(half, vn, vr), jnp.where(half, vr, vn)
        kw = jnp.concatenate([k0, k0, k1, k1], axis=1).astype(BF16)
        vw = jnp.concatenate([v0, v0, v1, v1], axis=1).astype(BF16)
        q_row = sq_scr[pl.ds(s, 1), :]
        qm = jnp.where(head_diag, jnp.broadcast_to(q_row, (16, SWA_Q)), 0.0).astype(BF16)
        sc = _dot_nt(qm, kw) + bias_ref[...]
        m = jnp.maximum(jnp.max(sc, axis=1, keepdims=True), sink_col)
        pr = jnp.exp(sc - m)
        inv = 1.0 / (jnp.sum(pr, axis=1, keepdims=True) + jnp.exp(sink_col - m))
        ow = _dot(pr.astype(BF16), vw) * inv
        os_scr[pl.ds(s, 1), :] = jnp.sum(jnp.where(head_diag, ow, 0.0), axis=0, keepdims=True)
        return carry

    lax.fori_loop(0, DEC_SB, per_seq, 0, unroll=DEC_UNROLL)

    @pl.when(i == pl.num_programs(0) - 1)
    def _():
        x_mid, hp, topi, gate8, rank8, base = _tail(
            x_ref[...], og_scr[...], gr_scr[...], os_scr[...], g_gla_ref[...], g_swa_ref[...],
            w_out_ref, g_ffn_ref[...], w_r_ref[...], b_r_ref[...], lambda: base0_ref[...])
        xmid_ref[...] = x_mid
        _store_slabs(hp_ref, hp)
        topi_ref[...] = topi
        gate_ref[...] = gate8
        rank_ref[...] = rank8
        cnt_ref[...] = base


def _decode_call(xs, state, ck, cv, base0, bias_dec, wts):
    n_seq = xs.shape[0]
    nb = n_seq // DEC_SB
    weight_args = (wts["g_mix"], wts["w_in"], wts["w_a_up"], wts["b_a"], wts["g_gla"],
                   wts["g_swa"], wts["w_out"], wts["g_ffn"], wts["w_r"], wts["b_r"])
    in_specs = [
        pl.BlockSpec(memory_space=pltpu.SMEM),
        _full_spec(xs.shape),
        pl.BlockSpec((DEC_SB, GLA_HEADS, GLA_DK, GLA_DV), lambda i: (i, 0, 0, 0)),
        pl.BlockSpec((DEC_SB, WINDOW, SWA_KV), lambda i: (i, 0, 0)),
        pl.BlockSpec((DEC_SB, WINDOW, SWA_KV), lambda i: (i, 0, 0)),
        _full_spec(base0.shape), _full_spec(bias_dec.shape),
    ] + [_full_spec(w.shape) for w in weight_args]
    out_specs = [
        _full_spec((n_seq, D_MODEL)),
        _full_spec((n_seq * SLAB, LANES)),
        _full_spec((8, n_seq)), _full_spec((8, n_seq)), _full_spec((8, n_seq)),
        pl.BlockSpec((DEC_SB, GLA_HEADS, GLA_DK, GLA_DV), lambda i: (i, 0, 0, 0)),
        pl.BlockSpec((DEC_SB, WINDOW, SWA_KV), lambda i: (i, 0, 0)),
        pl.BlockSpec((DEC_SB, WINDOW, SWA_KV), lambda i: (i, 0, 0)),
        _full_spec((N_EXPERTS, LANES)),
    ]
    out_shape = [
        jax.ShapeDtypeStruct((n_seq, D_MODEL), F32),
        jax.ShapeDtypeStruct((n_seq * SLAB, LANES), jnp.int32),
        jax.ShapeDtypeStruct((8, n_seq), jnp.int32),
        jax.ShapeDtypeStruct((8, n_seq), F32),
        jax.ShapeDtypeStruct((8, n_seq), jnp.int32),
        jax.ShapeDtypeStruct(state.shape, F32),
        jax.ShapeDtypeStruct(ck.shape, F32),
        jax.ShapeDtypeStruct(cv.shape, F32),
        jax.ShapeDtypeStruct((N_EXPERTS, LANES), F32),
    ]
    scratch = [pltpu.VMEM((5 * GLA_QK, n_seq), BF16)] + [
        pltpu.VMEM((n_seq, GLA_V), F32), pltpu.VMEM((n_seq, GLA_V), F32), pltpu.VMEM((n_seq, SWA_Q), F32),
        pltpu.VMEM((n_seq, SWA_KV), F32), pltpu.VMEM((n_seq, SWA_KV), F32),
        pltpu.VMEM((n_seq, GLA_V), F32), pltpu.VMEM((n_seq, SWA_Q), F32)]
    return pl.pallas_call(
        _decode_kernel,
        grid=(nb,),
        in_specs=in_specs,
        out_specs=out_specs,
        out_shape=out_shape,
        scratch_shapes=scratch,
        compiler_params=pltpu.CompilerParams(dimension_semantics=("arbitrary",),
                                             vmem_limit_bytes=VMEM_LIMIT),
        name="decode",
    )(wts["sinks"], xs, state, ck, cv, base0, bias_dec, *weight_args)


SC_CORES = 2
SC_SUBCORES = 16
SC_WORKERS = SC_CORES * SC_SUBCORES
SC_SCATTER_ROWS = 64
SC_GATHER_ROWS = 96


def _sc_mesh():
    return plsc.VectorSubcoreMesh(core_axis_name="c", subcore_axis_name="s")


def _sc_worker_id():
    return lax.axis_index("s") * SC_CORES + lax.axis_index("c")


def _sc_scatter_rows(src_p, src_s, idx_p, idx_s, n_out):
    rows = SC_SCATTER_ROWS
    n_chunks = idx_p.shape[0] // SC_WORKERS
    n_s, _, rows_s = idx_s.shape
    assert n_chunks * SC_WORKERS == idx_p.shape[0] and n_chunks % 2 == 0 and n_s <= SC_WORKERS

    @functools.partial(
        pl.kernel, mesh=_sc_mesh(),
        out_type=jax.ShapeDtypeStruct((n_out, SLAB, LANES), jnp.int32),
        scratch_types=[pltpu.VMEM((2, TOP_K, rows), jnp.int32), pltpu.VMEM((2, rows, SLAB, LANES), jnp.int32),
                       pltpu.VMEM((TOP_K, rows_s), jnp.int32), pltpu.VMEM((rows_s, SLAB, LANES), jnp.int32),
                       pltpu.SemaphoreType.DMA((2,)), pltpu.SemaphoreType.DMA((2,))])
    def scatter_rows(srcp_hbm, srcs_hbm, idxp_hbm, idxs_hbm, out_hbm, idx_v, rows_v, idxs_v, rowss_v, lsem, ssem):
        wid = _sc_worker_id()

        def loads(c, b):
            g = wid * n_chunks + c
            return (pltpu.make_async_copy(idxp_hbm.at[g], idx_v.at[b], lsem.at[b]),
                    pltpu.make_async_copy(srcp_hbm.at[pl.ds(pl.multiple_of(g * rows, 8), rows)], rows_v.at[b],
                                          lsem.at[b]))

        def scatters(b):
            return [pltpu.make_async_copy(rows_v.at[b], out_hbm.at[idx_v.at[b, k]], ssem.at[b])
                    for k in range(TOP_K)]

        for d in loads(0, 0):
            d.start()

        @pl.loop(0, n_chunks, step=2)
        def _(c0):
            for b in range(2):
                c = c0 + b
                for d in loads(c, b):
                    d.wait()

                @pl.when(c >= 1)
                def _():
                    for d in scatters(1 - b):
                        d.wait()

                @pl.when(c + 1 < n_chunks)
                def _():
                    for d in loads(c + 1, 1 - b):
                        d.start()

                for d in scatters(b):
                    d.start()

        for d in scatters((n_chunks - 1) % 2):
            d.wait()

        @pl.when(wid < n_s)
        def _():
            pltpu.sync_copy(idxs_hbm.at[wid], idxs_v)
            pltpu.sync_copy(srcs_hbm.at[pl.ds(pl.multiple_of(wid * rows_s, 8), rows_s)], rowss_v)
            for k in range(TOP_K):
                pltpu.sync_copy(rowss_v, out_hbm.at[idxs_v.at[k]])

    return scatter_rows(src_p, src_s, idx_p, idx_s)


def _sc_gather_rows(src3, idx2):
    rows = SC_GATHER_ROWS
    n_chunks = idx2.shape[0] // SC_WORKERS
    assert n_chunks * SC_WORKERS == idx2.shape[0] and idx2.shape[1] == rows and n_chunks % 2 == 0

    @functools.partial(
        pl.kernel, mesh=_sc_mesh(),
        out_type=jax.ShapeDtypeStruct((idx2.shape[0] * rows, SLAB, LANES), jnp.int32),
        scratch_types=[pltpu.VMEM((2, rows), jnp.int32), pltpu.VMEM((2, rows, SLAB, LANES), jnp.int32),
                       pltpu.SemaphoreType.DMA((2,)), pltpu.SemaphoreType.DMA((2,))])
    def gather_rows(src_hbm, idx_hbm, out_hbm, idx_v, rows_v, gsem, wsem):
        wid = _sc_worker_id()

        def gather(b):
            return pltpu.make_async_copy(src_hbm.at[idx_v.at[b]], rows_v.at[b], gsem.at[b])

        def write(c, b):
            base = pl.multiple_of((wid * n_chunks + c) * rows, 8)
            return pltpu.make_async_copy(rows_v.at[b], out_hbm.at[pl.ds(base, rows)], wsem.at[b])

        pltpu.sync_copy(idx_hbm.at[wid * n_chunks], idx_v.at[0])
        gather(0).start()

        @pl.loop(0, n_chunks, step=2)
        def _(c0):
            for b in range(2):
                c = c0 + b

                @pl.when(c + 1 < n_chunks)
                def _():
                    @pl.when(c >= 1)
                    def _():
                        write(c - 1, 1 - b).wait()
                    pltpu.sync_copy(idx_hbm.at[wid * n_chunks + c + 1], idx_v.at[1 - b])
                    gather(1 - b).start()

                gather(b).wait()
                write(c, b).start()

        write(n_chunks - 2, 0).wait()
        write(n_chunks - 1, 1).wait()

    return gather_rows(src3, idx2)


FF_TILE = 256


def _ffn_kernel(blk_e_ref, nused_ref, next_e_ref, x_ref, wu_hbm, bu_ref, wd_hbm, bd_ref, y_ref,
                xbf, actbf, wu_f32, wd_f32, wu_bf, wd_bf, wsem):
    i = pl.program_id(0)
    tm = xbf.shape[0]
    n_tiles = D_FF // FF_TILE

    def weight_copies(e):
        return (pltpu.make_async_copy(wu_hbm.at[e], wu_f32, wsem.at[0]),
                pltpu.make_async_copy(wd_hbm.at[e], wd_f32, wsem.at[1]))

    @pl.when(i < nused_ref[0])
    def _():
        @pl.when(i == 0)
        def _():
            for copy in weight_copies(blk_e_ref[0]):
                copy.start()

        @pl.when((i == 0) | (blk_e_ref[i] != blk_e_ref[jnp.maximum(i - 1, 0)]))
        def _():
            for copy in weight_copies(blk_e_ref[i]):
                copy.wait()
            wu_bf[...] = wu_f32[...].astype(BF16)
            wd_bf[...] = wd_f32[...].astype(BF16)

            @pl.when(next_e_ref[i] >= 0)
            def _():
                for copy in weight_copies(next_e_ref[i]):
                    copy.start()

        for c in range(SLAB):
            lo, hi = _load_slab_chunk(x_ref, tm, c)
            xbf[:, c * LANES:(c + 1) * LANES] = lo.astype(BF16)
            xbf[:, HALF_D + c * LANES:HALF_D + (c + 1) * LANES] = hi.astype(BF16)
        for n in range(n_tiles):
            gc = slice(n * FF_TILE, (n + 1) * FF_TILE)
            lc = slice(D_FF + n * FF_TILE, D_FF + (n + 1) * FF_TILE)
            g = jnp.minimum(_dot(xbf[...], wu_bf[:, gc]) + bu_ref[0, :, gc], SWIGLU_LIMIT)
            lin = jnp.clip(_dot(xbf[...], wu_bf[:, lc]) + bu_ref[0, :, lc], -SWIGLU_LIMIT, SWIGLU_LIMIT)
            actbf[:, gc] = (g * jax.nn.sigmoid(SWIGLU_ALPHA * g) * (lin + 1.0)).astype(BF16)
        per_tile = FF_TILE // LANES
        for n in range(n_tiles // 2):
            yl = slice(n * FF_TILE, (n + 1) * FF_TILE)
            yh = slice(HALF_D + n * FF_TILE, HALF_D + (n + 1) * FF_TILE)
            y_lo = _dot(actbf[...], wd_bf[:, yl]) + bd_ref[0, :, yl]
            y_hi = _dot(actbf[...], wd_bf[:, yh]) + bd_ref[0, :, yh]
            for c in range(per_tile):
                sl = slice(c * LANES, (c + 1) * LANES)
                y_ref[pl.ds(n * per_tile + c, tm, stride=SLAB), :] = _pack_pair(y_lo[:, sl], y_hi[:, sl])

    @pl.when(i >= nused_ref[0])
    def _():
        y_ref[...] = jnp.zeros_like(y_ref)


def _ffn_call(blk_e, nused, next_e, xs2, w_up, b_up, w_down, b_down, tm):
    n_blocks = blk_e.shape[0]
    row_blk = pl.BlockSpec((tm * SLAB, LANES), lambda i, be, nu, ne: (i, 0))
    grid_spec = pltpu.PrefetchScalarGridSpec(
        num_scalar_prefetch=3,
        grid=(n_blocks,),
        in_specs=[
            pl.BlockSpec((tm * SLAB, LANES), lambda i, be, nu, ne: (jnp.minimum(i, nu[0] - 1), 0)),
            pl.BlockSpec(memory_space=pl.ANY),
            pl.BlockSpec((1, 1, 2 * D_FF), lambda i, be, nu, ne: (be[i], 0, 0)),
            pl.BlockSpec(memory_space=pl.ANY),
            pl.BlockSpec((1, 1, D_MODEL), lambda i, be, nu, ne: (be[i], 0, 0)),
        ],
        out_specs=row_blk,
        scratch_shapes=[pltpu.VMEM((tm, D_MODEL), BF16), pltpu.VMEM((tm, D_FF), BF16),
                        pltpu.VMEM((D_MODEL, 2 * D_FF), F32), pltpu.VMEM((D_FF, D_MODEL), F32),
                        pltpu.VMEM((D_MODEL, 2 * D_FF), BF16), pltpu.VMEM((D_FF, D_MODEL), BF16),
                        pltpu.SemaphoreType.DMA((2,))],
    )
    return pl.pallas_call(
        _ffn_kernel,
        grid_spec=grid_spec,
        out_shape=jax.ShapeDtypeStruct((n_blocks * tm * SLAB, LANES), jnp.int32),
        compiler_params=pltpu.CompilerParams(dimension_semantics=("arbitrary",),
                                             vmem_limit_bytes=VMEM_LIMIT),
        name="experts",
    )(blk_e, nused, next_e, xs2, w_up, b_up.reshape(N_EXPERTS, 1, 2 * D_FF), w_down,
      b_down.reshape(N_EXPERTS, 1, D_MODEL))


def _combine_kernel(ys0_ref, ys1_ref, ys2_ref, ys3_ref, xmid_ref, gate_ref, g_final_ref, y_ref):
    tm = xmid_ref.shape[0]
    gts = jnp.transpose(jnp.concatenate([gate_ref[...], jnp.zeros((LANES - 8, tm), F32)], axis=0))
    lows, highs = [], []
    for c in range(SLAB):
        acc_lo = xmid_ref[:, c * LANES:(c + 1) * LANES]
        acc_hi = xmid_ref[:, HALF_D + c * LANES:HALF_D + (c + 1) * LANES]
        for k, ys_ref in enumerate((ys0_ref, ys1_ref, ys2_ref, ys3_ref)):
            lo, hi = _load_slab_chunk(ys_ref, tm, c)
            acc_lo = acc_lo + lo * gts[:, k:k + 1]
            acc_hi = acc_hi + hi * gts[:, k:k + 1]
        lows.append(acc_lo)
        highs.append(acc_hi)
    y_ref[...] = _rms(jnp.concatenate(lows + highs, axis=1), g_final_ref[...])


def _combine_call(ys4, t_stride, row0, x_mid, gates, g_final, tm):
    T = x_mid.shape[0]
    blk0 = row0 // tm
    per_k = t_stride // tm
    assert per_k * tm == t_stride and blk0 * tm == row0

    def ys_spec(k):
        return pl.BlockSpec((tm * SLAB, LANES), lambda i: (k * per_k + blk0 + i, 0))

    return pl.pallas_call(
        _combine_kernel,
        grid=(T // tm,),
        in_specs=[
            ys_spec(0), ys_spec(1), ys_spec(2), ys_spec(3),
            pl.BlockSpec((tm, D_MODEL), lambda i: (i, 0)),
            pl.BlockSpec((8, tm), lambda i: (0, i)),
            _full_spec((1, D_MODEL)),
        ],
        out_specs=pl.BlockSpec((tm, D_MODEL), lambda i: (i, 0)),
        out_shape=jax.ShapeDtypeStruct((T, D_MODEL), F32),
        compiler_params=pltpu.CompilerParams(dimension_semantics=("arbitrary",),
                                             vmem_limit_bytes=VMEM_LIMIT),
        name="combine",
    )(ys4, ys4, ys4, ys4, x_mid, gates, g_final)


def _t5_bucket(dist):
    n = jnp.maximum(dist, 0)
    max_exact = NUM_BUCKETS // 2
    nf = jnp.maximum(n, 1).astype(F32)
    large = max_exact + (jnp.log(nf / max_exact) / math.log(MAX_DISTANCE / max_exact)
                         * (NUM_BUCKETS - max_exact)).astype(jnp.int32)
    large = jnp.minimum(large, NUM_BUCKETS - 1)
    return jnp.where(n < max_exact, n, large)


def kernel(x_prompt, x_sample, state_gla, cache_swa_k, cache_swa_v, meta_tokens, rel_bias_table,
           g_mix, w_in, w_a_up, b_a, g_gla_out, g_swa_out, attn_sinks, w_out,
           g_ffn, w_router, b_router, w_up, b_up, w_down, b_down, g_final):
    assert g_mix.shape[0] == 1, "single-layer trunk"
    B, L, _ = x_prompt.shape
    n_seq = x_sample.shape[0]
    TP = B * L
    T_all = TP + n_seq

    wi = w_in[0]
    sizes = (GLA_QK, GLA_QK, GLA_V, GLA_V, GLA_LOWRANK, SWA_Q, SWA_KV, SWA_KV)
    offs = [0]
    for s in sizes:
        offs.append(offs[-1] + s)
    seg = [wi[:, offs[n]:offs[n + 1]] for n in range(8)]
    seg[0] = seg[0] * (GLA_DK ** -0.5)
    seg[5] = seg[5] * (SWA_HEAD_DIM ** -0.5)
    w_in_r = jnp.concatenate(
        seg[0:4] + seg[5:8] + [seg[4], jnp.zeros((D_MODEL, LANES - GLA_LOWRANK), F32)], axis=1).astype(BF16)
    w_a_pad = jnp.concatenate([w_a_up[0], jnp.zeros((LANES - GLA_LOWRANK, GLA_QK), F32)], axis=0).astype(BF16)
    wr_t = jnp.transpose(w_router[0])
    wr_hi = wr_t.astype(BF16)
    wr_lo = (wr_t - wr_hi.astype(F32)).astype(BF16)
    qi = jnp.arange(WINDOW)[:, None]
    kj = jnp.arange(2 * WINDOW)[None, :]
    buckets = jnp.arange(NUM_BUCKETS)
    table = rel_bias_table.astype(F32)
    oh_p = (_t5_bucket(qi - kj + WINDOW)[..., None] == buckets).astype(F32)
    bias_p = jnp.einsum("qkb,bh->hkq", oh_p, table, precision=lax.Precision.HIGHEST)
    in_window = jnp.transpose((kj > qi) & (kj <= qi + WINDOW))
    bias_p = jnp.where(in_window[None], bias_p, NEG_INF)
    bias_p = bias_p.reshape(SWA_KV_HEADS, SWA_GROUP, 2 * WINDOW, WINDOW).transpose(0, 2, 1, 3)
    bias_p = bias_p.reshape(SWA_KV_HEADS, 2 * WINDOW, SWA_GROUP * WINDOW)
    oh_d = (_t5_bucket(WINDOW - 1 - jnp.arange(WINDOW))[:, None] == buckets).astype(F32)
    bias_d = jnp.einsum("rb,bh->hr", oh_d, table, precision=lax.Precision.HIGHEST)
    bias_d = jnp.concatenate([bias_d, jnp.zeros((8, WINDOW), F32)], axis=0)
    wts = dict(
        sinks=attn_sinks[0].astype(F32), bias=bias_p,
        g_mix=g_mix[0][None], w_in=w_in_r, w_a_up=w_a_pad, b_a=b_a[0][None],
        g_gla=g_gla_out[0][None], g_swa=g_swa_out[0][None], w_out=w_out[0].astype(BF16),
        g_ffn=g_ffn[0][None], w_r=jnp.concatenate([wr_hi, wr_lo], axis=0), b_r=b_router[0][:, None],
    )

    x_pre = jnp.concatenate([jnp.zeros((WINDOW - N_META, D_MODEL), F32), meta_tokens.astype(F32)], axis=0)[None]
    zeros_s = jnp.zeros((GLA_QK, GLA_V), F32)
    zeros_kv = jnp.zeros((WINDOW, SWA_KV), F32)
    zeros_b = jnp.zeros((N_EXPERTS, LANES), F32)
    pre = _mixer_call(x_pre, zeros_s, zeros_kv, zeros_kv, zeros_b, wts, WINDOW, 1, WINDOW - N_META, 0)
    s_meta, k_meta, v_meta = pre[5][0], pre[6][0], pre[7][0]

    (xmid_p, hp_p, topi_p, gate_p, rank_p, s_p, k_p, v_p, cnt_p) = _mixer_call(
        x_prompt, s_meta, k_meta, v_meta, zeros_b, wts, MIX_STREAM_TM, 2, 0, WINDOW - N_META)

    (xmid_s, hp_s, topi_s, gate_s, rank_s, st_s, ck_s, cv_s, cnt_all) = _decode_call(
        x_sample[:, 0], state_gla[0], cache_swa_k[0].reshape(n_seq, WINDOW, SWA_KV),
        cache_swa_v[0].reshape(n_seq, WINDOW, SWA_KV), cnt_p, bias_d, wts)

    n_slots = T_all * TOP_K
    tm = -(-math.ceil(1.05 * n_slots / N_EXPERTS / MOE_BLOCKS_PER_EXPERT) // 16) * 16
    n_blocks = -(-n_slots // tm) + N_EXPERTS
    top_e = jnp.concatenate([topi_p[:TOP_K], topi_s[:TOP_K]], axis=1)
    rank = jnp.concatenate([rank_p[:TOP_K], rank_s[:TOP_K]], axis=1)
    counts = cnt_all[:, 0].astype(jnp.int32)
    padded = (counts + tm - 1) // tm * tm
    pad_end = jnp.cumsum(padded)
    pad_start = pad_end - padded
    e_ids = jnp.arange(N_EXPERTS, dtype=jnp.int32)
    dest = jnp.sum(jnp.where(top_e[..., None] == e_ids, pad_start, 0), axis=-1) + rank
    n_pad = n_blocks * tm
    blk_e = jnp.minimum(jnp.sum(pad_end[None] <= (jnp.arange(n_blocks, dtype=jnp.int32) * tm)[:, None], axis=1),
                        N_EXPERTS - 1).astype(jnp.int32)
    nused = (pad_end[-1] // tm).astype(jnp.int32).reshape(1)
    later_nonempty = (e_ids[None] > e_ids[:, None]) & (counts > 0)[None]
    next_nonempty = jnp.min(jnp.where(later_nonempty, e_ids[None], N_EXPERTS), axis=1)
    next_nonempty = jnp.where(next_nonempty == N_EXPERTS, -1, next_nonempty)
    next_e = jnp.sum(jnp.where(blk_e[:, None] == e_ids[None], next_nonempty[None], 0), axis=1).astype(jnp.int32)

    sample_rows = 8
    idx_p = dest[:, :TP].reshape(TOP_K, TP // SC_SCATTER_ROWS, SC_SCATTER_ROWS).transpose(1, 0, 2)
    idx_s = dest[:, TP:].reshape(TOP_K, n_seq // sample_rows, sample_rows).transpose(1, 0, 2)
    xs3 = _sc_scatter_rows(hp_p.reshape(TP, SLAB, LANES), hp_s.reshape(n_seq, SLAB, LANES), idx_p, idx_s, n_pad)
    ys2 = _ffn_call(blk_e, nused, next_e, xs3.reshape(-1, LANES), w_up[0], b_up[0], w_down[0], b_down[0], tm)
    unit = math.lcm(2 * SC_WORKERS * SC_GATHER_ROWS // TOP_K, MIX_TM)
    t_stride = -(-T_all // unit) * unit
    filler = jnp.arange(TOP_K * (t_stride - T_all), dtype=jnp.int32).reshape(TOP_K, t_stride - T_all)
    slot_src = jnp.concatenate([dest, filler], axis=1)
    slot_src = slot_src.reshape(TOP_K * t_stride // SC_GATHER_ROWS, SC_GATHER_ROWS)
    ys4 = _sc_gather_rows(ys2.reshape(-1, SLAB, LANES), slot_src).reshape(-1, LANES)

    gf = g_final[None]
    y_p = _combine_call(ys4, t_stride, 0, xmid_p.reshape(TP, D_MODEL), gate_p, gf, MIX_TM)
    y_s = _combine_call(ys4, t_stride, TP, xmid_s, gate_s, gf, n_seq)

    s_heads = jnp.stack([s_p[:, h * GLA_DK:(h + 1) * GLA_DK, h * GLA_DV:(h + 1) * GLA_DV]
                         for h in range(GLA_HEADS)], axis=1)
    return (y_p.reshape(B, L, D_MODEL), y_s.reshape(n_seq, 1, D_MODEL), s_heads[None],
            k_p.reshape(1, B, WINDOW, SWA_KV_HEADS, SWA_HEAD_DIM),
            v_p.reshape(1, B, WINDOW, SWA_KV_HEADS, SWA_HEAD_DIM),
            st_s[None], ck_s.reshape(1, n_seq, WINDOW, SWA_KV_HEADS, SWA_HEAD_DIM),
            cv_s.reshape(1, n_seq, WINDOW, SWA_KV_HEADS, SWA_HEAD_DIM))
```

```python
import functools
import math

import jax
import jax.numpy as jnp
from jax import lax
from jax.experimental import pallas as pl
from jax.experimental.pallas import tpu as pltpu
from jax.experimental.pallas import tpu_sc as plsc

D_MODEL = 1024
N_META = 16
GLA_HEADS = 4
GLA_DK = 64
GLA_DV = 128
GLA_LOWRANK = 16
GLA_GATE_TAU = 16.0
GLA_CHUNK = 64
SWA_HEADS = 8
SWA_KV_HEADS = 2
SWA_HEAD_DIM = 64
SWA_GROUP = SWA_HEADS // SWA_KV_HEADS
WINDOW = 128
NUM_BUCKETS = 32
MAX_DISTANCE = 128
N_EXPERTS = 32
TOP_K = 4
D_FF = 1024
SWIGLU_ALPHA = 1.702
SWIGLU_LIMIT = 7.0
RMS_EPS = 1e-6

GLA_QK = GLA_HEADS * GLA_DK
GLA_V = GLA_HEADS * GLA_DV
SWA_Q = SWA_HEADS * SWA_HEAD_DIM
SWA_KV = SWA_KV_HEADS * SWA_HEAD_DIM
LANES = 128
C_GQ, C_GK, C_GV, C_GR = 0, GLA_QK, 2 * GLA_QK, 2 * GLA_QK + GLA_V
C_SQ = C_GR + GLA_V
C_SK = C_SQ + SWA_Q
C_SV = C_SK + SWA_KV
C_GA = C_SV + SWA_KV
D_PROJ = C_GA + LANES

MIX_TM = 512
MIX_STREAM_TM = 512
MOE_BLOCKS_PER_EXPERT = 4
DEC_SB = 16
DEC_UNROLL = 8
VMEM_LIMIT = 56 * 1024 * 1024

F32 = jnp.float32
BF16 = jnp.bfloat16
NEG_INF = float("-inf")


def _dot(a, b):
    return jnp.dot(a, b, preferred_element_type=F32)


def _dot_nt(a, b):
    return lax.dot_general(a, b, (((1,), (1,)), ((), ())), preferred_element_type=F32)


def _split3(x):
    hi = x.astype(BF16)
    r1 = x - hi.astype(F32)
    mid = r1.astype(BF16)
    lo = (r1 - mid.astype(F32)).astype(BF16)
    return hi, mid, lo


def _rms(x, g):
    return x * lax.rsqrt(jnp.mean(x * x, axis=-1, keepdims=True) + RMS_EPS) * g


def _iota(shape, dim):
    return lax.broadcasted_iota(jnp.int32, shape, dim)


HALF_D = D_MODEL // 2
SLAB = HALF_D // LANES


def _pack_pair(lo, hi):
    bl = lax.bitcast_convert_type(lo.astype(BF16).astype(F32), jnp.uint32)
    bh = lax.bitcast_convert_type(hi.astype(BF16).astype(F32), jnp.uint32)
    return lax.bitcast_convert_type(bh | lax.shift_right_logical(bl, jnp.uint32(16)), jnp.int32)


def _unpack_pair(w):
    u = lax.bitcast_convert_type(w, jnp.uint32)
    lo = lax.bitcast_convert_type(lax.shift_left(u, jnp.uint32(16)), F32)
    hi = lax.bitcast_convert_type(u & jnp.uint32(0xFFFF0000), F32)
    return lo, hi


def _store_slabs(ref, x):
    rows = x.shape[0]
    for c in range(SLAB):
        sl = slice(c * LANES, (c + 1) * LANES)
        ref[pl.ds(c, rows, stride=SLAB), :] = _pack_pair(x[:, sl], x[:, HALF_D + c * LANES:HALF_D + (c + 1) * LANES])


def _load_slab_chunk(ref, rows, c):
    return _unpack_pair(ref[pl.ds(c, rows, stride=SLAB), :])


def _drain(steps):
    try:
        while True:
            next(steps)
    except StopIteration as done:
        return done.value


def _project_steps(x, g_mix, w_in_ref, w_a_up, b_a):
    h = _rms(x, g_mix).astype(BF16)
    yield

    def cols(lo, width, tile=2 * LANES):
        parts = []
        for off in range(0, width, tile):
            parts.append(_dot(h, w_in_ref[:, lo + off:lo + min(off + tile, width)]))
            yield
        return parts[0] if len(parts) == 1 else jnp.concatenate(parts, axis=1)

    ga = (yield from cols(C_GA, LANES)).astype(BF16)
    z = _dot(ga, w_a_up) + b_a
    log_a = -(jnp.maximum(-z, 0.0) + jnp.log(1.0 + jnp.exp(-jnp.abs(z)))) / GLA_GATE_TAU
    gqk = yield from cols(C_GQ, 2 * GLA_QK)
    gv = yield from cols(C_GV, GLA_V)
    swa = yield from cols(C_SQ, SWA_Q + 2 * SWA_KV)
    gr = yield from cols(C_GR, GLA_V)
    return dict(
        gq=gqk[:, :GLA_QK],
        gk=gqk[:, GLA_QK:],
        gv=gv,
        gr=gr,
        sq=swa[:, :SWA_Q],
        sk=swa[:, SWA_Q:SWA_Q + SWA_KV],
        sv=swa[:, SWA_Q + SWA_KV:],
        log_a=log_a,
    )


def _project(x, g_mix, w_in_ref, w_a_up, b_a):
    return _drain(_project_steps(x, g_mix, w_in_ref, w_a_up, b_a))


def _tail(*args):
    return _drain(_tail_steps(*args))


def _tail_steps(x, o_gla, gr, o_swa, g_gla_out, g_swa_out, w_out_ref, g_ffn, w_r, b_r, get_base):
    tm = x.shape[0]
    gate = gr * jax.nn.sigmoid(gr)
    parts = []
    for h in range(GLA_HEADS):
        sl = slice(h * GLA_DV, (h + 1) * GLA_DV)
        parts.append(_rms(o_gla[:, sl], g_gla_out) * gate[:, sl])
    og = jnp.concatenate(parts, axis=1).astype(BF16)
    yield
    os_ = _rms(o_swa, g_swa_out).astype(BF16)
    x_mid = x + _dot(og, w_out_ref[0:GLA_V])
    yield
    x_mid = x_mid + _dot(os_, w_out_ref[GLA_V:GLA_V + SWA_Q])
    yield
    hp = _rms(x_mid, g_ffn)

    h1 = hp.astype(BF16)
    h2 = (hp - h1.astype(F32)).astype(BF16)
    yield
    la = _dot_nt(w_r, h1)
    lb = _dot_nt(w_r[0:N_EXPERTS], h2)
    logits = la[0:N_EXPERTS] + la[N_EXPERTS:2 * N_EXPERTS] + lb + b_r
    yield

    eidx = _iota((N_EXPERTS, tm), 0)
    vals, idxs, onehots = [], [], []
    l = logits
    for _ in range(TOP_K):
        m = jnp.max(l, axis=0, keepdims=True)
        sel = jnp.min(jnp.where(l == m, eidx, N_EXPERTS), axis=0, keepdims=True)
        oh = eidx == sel
        l = jnp.where(oh, NEG_INF, l)
        vals.append(m)
        idxs.append(sel)
        onehots.append(oh)
    es = [jnp.exp(v - vals[0]) for v in vals]
    denom = es[0] + es[1] + es[2] + es[3]
    gates = [e / denom for e in es]

    ohf = jnp.concatenate([oh.astype(F32) for oh in onehots], axis=0)
    upper = (_iota((tm, tm), 0) < _iota((tm, tm), 1)).astype(BF16)
    prefix = _dot(ohf.astype(BF16), upper)
    yield
    base = get_base()
    ranks = []
    for k in range(TOP_K):
        sl = slice(k * N_EXPERTS, (k + 1) * N_EXPERTS)
        ohk = ohf[sl]
        base_t = jnp.concatenate([base] * (tm // LANES), axis=1)
        ranks.append(jnp.sum(ohk * (prefix[sl] + base_t), axis=0, keepdims=True))
        base = base + jnp.sum(ohk, axis=1, keepdims=True)
    zi = jnp.zeros((8 - TOP_K, tm), jnp.int32)
    zf = jnp.zeros((8 - TOP_K, tm), F32)
    topi = jnp.concatenate(idxs + [zi], axis=0)
    gate8 = jnp.concatenate(gates + [zf], axis=0)
    rank8 = jnp.concatenate([r.astype(jnp.int32) for r in ranks] + [zi], axis=0)
    return x_mid, hp, topi, gate8, rank8, base


def _gla_chunks(p, row0, s_blocks, n_lead_pad):
    tm = p["gq"].shape[0]
    nch = tm // GLA_CHUNK
    log_a = p["log_a"]
    if n_lead_pad:
        rows = row0 + _iota((tm, GLA_QK), 0)
        log_a = jnp.where(rows >= n_lead_pad, log_a, 0.0)
    ri, ci = _iota((tm, tm), 0), _iota((tm, tm), 1)
    tril = ((ri >= ci) & (ri // GLA_CHUNK == ci // GLA_CHUNK)).astype(BF16)
    hi, mid, lo = _split3(log_a)
    b_all = _dot(tril, hi) + _dot(tril, mid) + _dot(tril, lo)
    yield

    c64 = GLA_CHUNK
    kk_mask = (_iota((GLA_QK, GLA_QK), 0) // c64) == (_iota((GLA_QK, GLA_QK), 1) // GLA_DK)
    vv_mask = (_iota((GLA_QK, GLA_V), 0) // c64) == (_iota((GLA_QK, GLA_V), 1) // GLA_DV)
    zero_blk = jnp.zeros((GLA_DK, GLA_DV), BF16)
    causal = (_iota((c64, GLA_QK), 0) >= (_iota((c64, GLA_QK), 1) % c64)).astype(F32)
    zpad_k = jnp.zeros((LANES - c64, GLA_QK), F32)
    zpad_v = jnp.zeros((LANES - c64, GLA_V), BF16)

    outs = []
    for c in range(nch):
        rs = slice(c * c64, (c + 1) * c64)
        b = b_all[rs]
        q, k, v = p["gq"][rs], p["gk"][rs], p["gv"][rs]
        b_last = b[c64 - 1:c64]
        qt = (q * jnp.exp(b)).astype(BF16)
        kt = k * jnp.exp(-b)
        kd = k * jnp.exp(b_last - b)
        vb = v.astype(BF16)
        k_bd = jnp.where(kk_mask, jnp.concatenate([kt] * GLA_HEADS, axis=0), 0.0).astype(BF16)
        a = (_dot_nt(qt, k_bd) * causal).astype(BF16)
        v_bd = jnp.where(vv_mask, jnp.concatenate([vb] * GLA_HEADS, axis=0), jnp.zeros((), BF16))
        s_bd = jnp.concatenate(
            [jnp.concatenate([s_blocks[h].astype(BF16) if g == h else zero_blk for g in range(GLA_HEADS)], axis=1)
             for h in range(GLA_HEADS)], axis=0)
        outs.append(_dot(qt, s_bd) + _dot(a, v_bd))
        kd_t = jnp.transpose(jnp.concatenate([kd, zpad_k], axis=0)).astype(BF16)
        upd = _dot(kd_t, jnp.concatenate([vb, zpad_v], axis=0))
        decay = jnp.exp(jnp.transpose(jnp.broadcast_to(b_last, (LANES, GLA_QK))))
        s_blocks = [s_blocks[h] * decay[h * GLA_DK:(h + 1) * GLA_DK]
                    + upd[h * GLA_DK:(h + 1) * GLA_DK, h * GLA_DV:(h + 1) * GLA_DV] for h in range(GLA_HEADS)]
        yield
    return jnp.concatenate(outs, axis=0), s_blocks


def _swa_block(sq, kcat, vcat, bias_ref, sinks_ref, valid_t):
    half = _iota((1, LANES), 1) < SWA_HEAD_DIM
    top_rows = _iota((LANES, 1), 0) < SWA_HEAD_DIM
    k_roll = pltpu.roll(kcat, SWA_HEAD_DIM, 1)
    v_t = jnp.transpose(vcat)
    zeros_v = jnp.zeros((SWA_HEAD_DIM, 2 * WINDOW), F32)
    cols = []
    for kv in range(SWA_KV_HEADS):
        kk = jnp.where(half, kcat, k_roll) if kv == 0 else jnp.where(half, k_roll, kcat)
        q_parts = []
        for c in (2 * kv, 2 * kv + 1):
            qc = sq[:, c * LANES:(c + 1) * LANES]
            q_parts.append(jnp.where(half, qc, 0.0))
            q_parts.append(jnp.where(half, 0.0, qc))
        q_st = jnp.concatenate(q_parts, axis=0).astype(BF16)
        s = _dot_nt(kk.astype(BF16), q_st) + bias_ref[kv]
        if valid_t is not None:
            s = jnp.where(valid_t, s, NEG_INF)
        sink = jnp.concatenate(
            [jnp.full((1, WINDOW), sinks_ref[kv * SWA_GROUP + g], F32) for g in range(SWA_GROUP)], axis=1)
        m = jnp.maximum(jnp.max(s, axis=0, keepdims=True), sink)
        pr = jnp.exp(s - m)
        inv = 1.0 / (jnp.sum(pr, axis=0, keepdims=True) + jnp.exp(sink - m))
        pb = pr.astype(BF16)
        vk = v_t[kv * SWA_HEAD_DIM:(kv + 1) * SWA_HEAD_DIM]
        vv_t = jnp.concatenate([jnp.concatenate([vk, zeros_v], axis=1),
                                jnp.concatenate([zeros_v, vk], axis=1)], axis=0).astype(BF16)
        for pair in range(SWA_GROUP // 2):
            ce = slice(2 * pair * WINDOW, (2 * pair + 1) * WINDOW)
            co = slice((2 * pair + 1) * WINDOW, (2 * pair + 2) * WINDOW)
            p2_t = jnp.concatenate([pb[:, ce], pb[:, co]], axis=0)
            o2_t = _dot(vv_t, p2_t)
            o2_t = o2_t * jnp.where(top_rows, inv[:, ce], inv[:, co])
            cols.append(jnp.transpose(o2_t))
        yield
    return jnp.concatenate(cols, axis=1)


def _mixer_kernel(sinks_ref, x_ref, s0_ref, k0_ref, v0_ref, base0_ref, bias_ref,
                  g_mix_ref, w_in_ref, w_a_up_ref, b_a_ref, g_gla_ref, g_swa_ref, w_out_ref,
                  g_ffn_ref, w_r_ref, b_r_ref,
                  xmid_ref, hp_ref, topi_ref, gate_ref, rank_ref, sout_ref, kout_ref, vout_ref, cnt_ref,
                  s_scr, k_scr, v_scr, base_scr, *, n_lead_pad, prev_valid_from):
    g_id, j = pl.program_id(0), pl.program_id(1)
    n_streams, tm = x_ref.shape[1], x_ref.shape[2]

    @pl.when(j == 0)
    def _():
        for s in range(n_streams):
            s_scr[s] = s0_ref[...]
            k_scr[s] = k0_ref[...]
            v_scr[s] = v0_ref[...]

    @pl.when((j == 0) & (g_id == 0))
    def _():
        base_scr[...] = base0_ref[...]

    diag = [(slice(h * GLA_DK, (h + 1) * GLA_DK), slice(h * GLA_DV, (h + 1) * GLA_DV)) for h in range(GLA_HEADS)]

    def stream(s):
        x = x_ref[0, s]
        p = yield from _project_steps(x, g_mix_ref[...], w_in_ref, w_a_up_ref[...], b_a_ref[...])
        yield "mix"
        o_gla, s_blocks = yield from _gla_chunks(p, j * tm, [s_scr[s, r, c] for r, c in diag], n_lead_pad)
        for (r, c), blk in zip(diag, s_blocks):
            s_scr[s, r, c] = blk
        o_parts = []
        for sb in range(tm // WINDOW):
            rs = slice(sb * WINDOW, (sb + 1) * WINDOW)
            k_blk, v_blk = p["sk"][rs], p["sv"][rs]
            k_prev = k_scr[s] if sb == 0 else p["sk"][(sb - 1) * WINDOW:sb * WINDOW]
            v_prev = v_scr[s] if sb == 0 else p["sv"][(sb - 1) * WINDOW:sb * WINDOW]
            valid = None
            if sb == 0 and prev_valid_from:
                first = jnp.where(j == 0, prev_valid_from, 0)
                valid = _iota((2 * WINDOW, SWA_GROUP * WINDOW), 0) >= first
            o_parts.append((yield from _swa_block(
                p["sq"][rs], jnp.concatenate([k_prev, k_blk], axis=0),
                jnp.concatenate([v_prev, v_blk], axis=0), bias_ref, sinks_ref, valid)))
        o_swa = jnp.concatenate(o_parts, axis=0)
        k_scr[s] = p["sk"][tm - WINDOW:tm]
        v_scr[s] = p["sv"][tm - WINDOW:tm]
        yield "tail"
        x_mid, hp, topi, gate8, rank8, base = yield from _tail_steps(
            x, o_gla, p["gr"], o_swa, g_gla_ref[...], g_swa_ref[...], w_out_ref,
            g_ffn_ref[...], w_r_ref[...], b_r_ref[...], lambda: base_scr[...])
        base_scr[...] = base
        xmid_ref[0, s] = x_mid
        _store_slabs(hp_ref.at[0, s], hp)
        topi_ref[0, s] = topi
        gate_ref[0, s] = gate8
        rank_ref[0, s] = rank8
        sout_ref[0, s] = s_scr[s]
        kout_ref[0, s] = p["sk"][tm - WINDOW:tm]
        vout_ref[0, s] = p["sv"][tm - WINDOW:tm]
        cnt_ref[...] = base

    def advance(gen, stop):
        try:
            while next(gen) != stop or stop is None:
                pass
            return False
        except StopIteration:
            return True

    def alternate(gen_a, stop_a, gen_b, stop_b):
        done_a = done_b = False
        while not (done_a and done_b):
            if not done_a:
                try:
                    done_a = next(gen_a) == stop_a and stop_a is not None
                except StopIteration:
                    done_a = True
            if not done_b:
                try:
                    done_b = next(gen_b) == stop_b and stop_b is not None
                except StopIteration:
                    done_b = True

    if n_streams == 1:
        advance(stream(0), None)
    else:
        first, second = stream(0), stream(1)
        advance(first, "mix")
        alternate(first, "tail", second, "mix")
        alternate(first, None, second, "tail")
        advance(second, None)


def _full_spec(shape):
    nd = len(shape)
    return pl.BlockSpec(shape, lambda *_: (0,) * nd)


def _mixer_call(x, s0, k0, v0, base0, wts, tm, n_streams, n_lead_pad, prev_valid_from):
    B, L, _ = x.shape
    S = n_streams
    G = B // S
    nj = L // tm
    weight_args = (wts["bias"], wts["g_mix"], wts["w_in"], wts["w_a_up"], wts["b_a"], wts["g_gla"],
                   wts["g_swa"], wts["w_out"], wts["g_ffn"], wts["w_r"], wts["b_r"])
    in_specs = [
        pl.BlockSpec(memory_space=pltpu.SMEM),
        pl.BlockSpec((1, S, tm, D_MODEL), lambda g, j: (g, 0, j, 0)),
        _full_spec(s0.shape), _full_spec(k0.shape), _full_spec(v0.shape), _full_spec(base0.shape),
    ] + [_full_spec(w.shape) for w in weight_args]
    tok_spec = pl.BlockSpec((1, S, 8, tm), lambda g, j: (g, 0, 0, j))

    def per_seq_spec(rows, cols):
        return pl.BlockSpec((1, S, rows, cols), lambda g, j: (g, 0, 0, 0))

    out_specs = [
        pl.BlockSpec((1, S, tm, D_MODEL), lambda g, j: (g, 0, j, 0)),
        pl.BlockSpec((1, S, tm * SLAB, LANES), lambda g, j: (g, 0, j, 0)),
        tok_spec, tok_spec, tok_spec,
        per_seq_spec(GLA_QK, GLA_V), per_seq_spec(WINDOW, SWA_KV), per_seq_spec(WINDOW, SWA_KV),
        _full_spec((N_EXPERTS, LANES)),
    ]
    out_shape = [
        jax.ShapeDtypeStruct((G, S, L, D_MODEL), F32),
        jax.ShapeDtypeStruct((G, S, L * SLAB, LANES), jnp.int32),
        jax.ShapeDtypeStruct((G, S, 8, L), jnp.int32),
        jax.ShapeDtypeStruct((G, S, 8, L), F32),
        jax.ShapeDtypeStruct((G, S, 8, L), jnp.int32),
        jax.ShapeDtypeStruct((G, S, GLA_QK, GLA_V), F32),
        jax.ShapeDtypeStruct((G, S, WINDOW, SWA_KV), F32),
        jax.ShapeDtypeStruct((G, S, WINDOW, SWA_KV), F32),
        jax.ShapeDtypeStruct((N_EXPERTS, LANES), F32),
    ]
    kern = functools.partial(_mixer_kernel, n_lead_pad=n_lead_pad, prev_valid_from=prev_valid_from)
    xmid, hp, topi, gate, rank, s_out, k_out, v_out, cnt = pl.pallas_call(
        kern,
        grid=(G, nj),
        in_specs=in_specs,
        out_specs=out_specs,
        out_shape=out_shape,
        scratch_shapes=[pltpu.VMEM((S, GLA_QK, GLA_V), F32), pltpu.VMEM((S, WINDOW, SWA_KV), F32),
                        pltpu.VMEM((S, WINDOW, SWA_KV), F32), pltpu.VMEM((N_EXPERTS, LANES), F32)],
        compiler_params=pltpu.CompilerParams(dimension_semantics=("arbitrary", "arbitrary"),
                                             vmem_limit_bytes=VMEM_LIMIT),
        name="mixer",
    )(wts["sinks"], x.reshape(G, S, L, D_MODEL), s0, k0, v0, base0, *weight_args)

    def rows8(a):
        return jnp.transpose(a, (2, 0, 1, 3)).reshape(8, B * L)

    return (xmid.reshape(B, L, D_MODEL), hp.reshape(B * L * SLAB, LANES), rows8(topi), rows8(gate), rows8(rank),
            s_out.reshape(B, GLA_QK, GLA_V), k_out.reshape(B, WINDOW, SWA_KV), v_out.reshape(B, WINDOW, SWA_KV), cnt)


def _decode_kernel(sinks_ref, x_ref, st_ref, ck_ref, cv_ref, base0_ref, bias_ref,
                   g_mix_ref, w_in_ref, w_a_up_ref, b_a_ref, g_gla_ref, g_swa_ref, w_out_ref,
                   g_ffn_ref, w_r_ref, b_r_ref,
                   xmid_ref, hp_ref, topi_ref, gate_ref, rank_ref, sto_ref, cko_ref, cvo_ref, cnt_ref,
                   tq_scr, gv_scr, gr_scr, sq_scr, sk_scr, sv_scr, og_scr, os_scr):
    i = pl.program_id(0)
    n_seq = x_ref.shape[0]

    @pl.when(i == 0)
    def _():
        p = _project(x_ref[...], g_mix_ref[...], w_in_ref, w_a_up_ref[...], b_a_ref[...])
        a_hi, a_mid, a_lo = _split3(jnp.transpose(jnp.exp(p["log_a"])))
        tq_scr[...] = jnp.concatenate(
            [a_hi, a_mid, a_lo, jnp.transpose(p["gk"]).astype(BF16), jnp.transpose(p["gq"]).astype(BF16)], axis=0)
        gv_scr[...] = p["gv"]
        gr_scr[...] = p["gr"]
        sq_scr[...] = p["sq"]
        sk_scr[...] = p["sk"]
        sv_scr[...] = p["sv"]

    seq_row = _iota((n_seq, LANES), 0)
    half = _iota((1, LANES), 1) < SWA_HEAD_DIM
    row_id = _iota((WINDOW, SWA_KV), 0)
    head_diag = (_iota((16, SWA_Q), 1) // SWA_HEAD_DIM) == _iota((16, SWA_Q), 0)
    sink_col = jnp.concatenate(
        [jnp.full((1, 1), sinks_ref[h], F32) for h in range(SWA_HEADS)] + [jnp.zeros((8, 1), F32)], axis=0)

    def per_seq(sl, carry):
        s = i * DEC_SB + sl
        pick = (seq_row == s).astype(BF16)
        cols = _dot(tq_scr[...], pick)
        a_c = cols[0:GLA_QK] + cols[GLA_QK:2 * GLA_QK] + cols[2 * GLA_QK:3 * GLA_QK]
        k_c = cols[3 * GLA_QK:4 * GLA_QK]
        q_c = cols[4 * GLA_QK:5 * GLA_QK]
        st = st_ref[sl].reshape(GLA_QK, GLA_DV)
        v_row = gv_scr[pl.ds(s, 1), :]
        v_b = jnp.concatenate(
            [jnp.broadcast_to(v_row[:, h * GLA_DV:(h + 1) * GLA_DV], (GLA_DK, GLA_DV))
             for h in range(GLA_HEADS)], axis=0)
        st_new = a_c * st + k_c * v_b
        sto_ref[sl] = st_new.reshape(GLA_HEADS, GLA_DK, GLA_DV)
        t = q_c * st_new
        og_scr[pl.ds(s, 1), :] = jnp.concatenate(
            [jnp.sum(t[h * GLA_DK:(h + 1) * GLA_DK], axis=0, keepdims=True) for h in range(GLA_HEADS)],
            axis=1)

        k_new = sk_scr[pl.ds(s, 1), :]
        v_new = sv_scr[pl.ds(s, 1), :]
        kn = jnp.where(row_id == WINDOW - 1, k_new, pltpu.roll(ck_ref[sl], WINDOW - 1, 0))
        vn = jnp.where(row_id == WINDOW - 1, v_new, pltpu.roll(cv_ref[sl], WINDOW - 1, 0))
        cko_ref[sl] = kn
        cvo_ref[sl] = vn
        kr, vr = pltpu.roll(kn, SWA_HEAD_DIM, 1), pltpu.roll(vn, SWA_HEAD_DIM, 1)
        k0, k1 = jnp.where(half, kn, kr), jnp.where(half, kr, kn)
        v0, v1 = jnp.where(half, vn, vr), jnp.where(half, vr, vn)
        kw = jnp.concatenate([k0, k0, k1, k1], axis=1).astype(BF16)
        vw = jnp.concatenate([v0, v0, v1, v1], axis=1).astype(BF16)
        q_row = sq_scr[pl.ds(s, 1), :]
        qm = jnp.where(head_diag, jnp.broadcast_to(q_row, (16, SWA_Q)), 0.0).astype(BF16)
        sc = _dot_nt(qm, kw) + bias_ref[...]
        m = jnp.maximum(jnp.max(sc, axis=1, keepdims=True), sink_col)
        pr = jnp.exp(sc - m)
        inv = 1.0 / (jnp.sum(pr, axis=1, keepdims=True) + jnp.exp(sink_col - m))
        ow = _dot(pr.astype(BF16), vw) * inv
        os_scr[pl.ds(s, 1), :] = jnp.sum(jnp.where(head_diag, ow, 0.0), axis=0, keepdims=True)
        return carry

    lax.fori_loop(0, DEC_SB, per_seq, 0, unroll=DEC_UNROLL)

    @pl.when(i == pl.num_programs(0) - 1)
    def _():
        x_mid, hp, topi, gate8, rank8, base = _tail(
            x_ref[...], og_scr[...], gr_scr[...], os_scr[...], g_gla_ref[...], g_swa_ref[...],
            w_out_ref, g_ffn_ref[...], w_r_ref[...], b_r_ref[...], lambda: base0_ref[...])
        xmid_ref[...] = x_mid
        _store_slabs(hp_ref, hp)
        topi_ref[...] = topi
        gate_ref[...] = gate8
        rank_ref[...] = rank8
        cnt_ref[...] = base


def _decode_call(xs, state, ck, cv, base0, bias_dec, wts):
    n_seq = xs.shape[0]
    nb = n_seq // DEC_SB
    weight_args = (wts["g_mix"], wts["w_in"], wts["w_a_up"], wts["b_a"], wts["g_gla"],
                   wts["g_swa"], wts["w_out"], wts["g_ffn"], wts["w_r"], wts["b_r"])
    in_specs = [
        pl.BlockSpec(memory_space=pltpu.SMEM),
        _full_spec(xs.shape),
        pl.BlockSpec((DEC_SB, GLA_HEADS, GLA_DK, GLA_DV), lambda i: (i, 0, 0, 0)),
        pl.BlockSpec((DEC_SB, WINDOW, SWA_KV), lambda i: (i, 0, 0)),
        pl.BlockSpec((DEC_SB, WINDOW, SWA_KV), lambda i: (i, 0, 0)),
        _full_spec(base0.shape), _full_spec(bias_dec.shape),
    ] + [_full_spec(w.shape) for w in weight_args]
    out_specs = [
        _full_spec((n_seq, D_MODEL)),
        _full_spec((n_seq * SLAB, LANES)),
        _full_spec((8, n_seq)), _full_spec((8, n_seq)), _full_spec((8, n_seq)),
        pl.BlockSpec((DEC_SB, GLA_HEADS, GLA_DK, GLA_DV), lambda i: (i, 0, 0, 0)),
        pl.BlockSpec((DEC_SB, WINDOW, SWA_KV), lambda i: (i, 0, 0)),
        pl.BlockSpec((DEC_SB, WINDOW, SWA_KV), lambda i: (i, 0, 0)),
        _full_spec((N_EXPERTS, LANES)),
    ]
    out_shape = [
        jax.ShapeDtypeStruct((n_seq, D_MODEL), F32),
        jax.ShapeDtypeStruct((n_seq * SLAB, LANES), jnp.int32),
        jax.ShapeDtypeStruct((8, n_seq), jnp.int32),
        jax.ShapeDtypeStruct((8, n_seq), F32),
        jax.ShapeDtypeStruct((8, n_seq), jnp.int32),
        jax.ShapeDtypeStruct(state.shape, F32),
        jax.ShapeDtypeStruct(ck.shape, F32),
        jax.ShapeDtypeStruct(cv.shape, F32),
        jax.ShapeDtypeStruct((N_EXPERTS, LANES), F32),
    ]
    scratch = [pltpu.VMEM((5 * GLA_QK, n_seq), BF16)] + [
        pltpu.VMEM((n_seq, GLA_V), F32), pltpu.VMEM((n_seq, GLA_V), F32), pltpu.VMEM((n_seq, SWA_Q), F32),
        pltpu.VMEM((n_seq, SWA_KV), F32), pltpu.VMEM((n_seq, SWA_KV), F32),
        pltpu.VMEM((n_seq, GLA_V), F32), pltpu.VMEM((n_seq, SWA_Q), F32)]
    return pl.pallas_call(
        _decode_kernel,
        grid=(nb,),
        in_specs=in_specs,
        out_specs=out_specs,
        out_shape=out_shape,
        scratch_shapes=scratch,
        compiler_params=pltpu.CompilerParams(dimension_semantics=("arbitrary",),
                                             vmem_limit_bytes=VMEM_LIMIT),
        name="decode",
    )(wts["sinks"], xs, state, ck, cv, base0, bias_dec, *weight_args)


SC_CORES = 2
SC_SUBCORES = 16
SC_WORKERS = SC_CORES * SC_SUBCORES
SC_SCATTER_ROWS = 64
SC_GATHER_ROWS = 96


def _sc_mesh():
    return plsc.VectorSubcoreMesh(core_axis_name="c", subcore_axis_name="s")


def _sc_worker_id():
    return lax.axis_index("s") * SC_CORES + lax.axis_index("c")


def _sc_scatter_rows(src_p, src_s, idx_p, idx_s, n_out):
    rows = SC_SCATTER_ROWS
    n_chunks = idx_p.shape[0] // SC_WORKERS
    n_s, _, rows_s = idx_s.shape
    assert n_chunks * SC_WORKERS == idx_p.shape[0] and n_chunks % 2 == 0 and n_s <= SC_WORKERS

    @functools.partial(
        pl.kernel, mesh=_sc_mesh(),
        out_type=jax.ShapeDtypeStruct((n_out, SLAB, LANES), jnp.int32),
        scratch_types=[pltpu.VMEM((2, TOP_K, rows), jnp.int32), pltpu.VMEM((2, rows, SLAB, LANES), jnp.int32),
                       pltpu.VMEM((TOP_K, rows_s), jnp.int32), pltpu.VMEM((rows_s, SLAB, LANES), jnp.int32),
                       pltpu.SemaphoreType.DMA((2,)), pltpu.SemaphoreType.DMA((2,))])
    def scatter_rows(srcp_hbm, srcs_hbm, idxp_hbm, idxs_hbm, out_hbm, idx_v, rows_v, idxs_v, rowss_v, lsem, ssem):
        wid = _sc_worker_id()

        def loads(c, b):
            g = wid * n_chunks + c
            return (pltpu.make_async_copy(idxp_hbm.at[g], idx_v.at[b], lsem.at[b]),
                    pltpu.make_async_copy(srcp_hbm.at[pl.ds(pl.multiple_of(g * rows, 8), rows)], rows_v.at[b],
                                          lsem.at[b]))

        def scatters(b):
            return [pltpu.make_async_copy(rows_v.at[b], out_hbm.at[idx_v.at[b, k]], ssem.at[b])
                    for k in range(TOP_K)]

        for d in loads(0, 0):
            d.start()

        @pl.loop(0, n_chunks, step=2)
        def _(c0):
            for b in range(2):
                c = c0 + b
                for d in loads(c, b):
                    d.wait()

                @pl.when(c >= 1)
                def _():
                    for d in scatters(1 - b):
                        d.wait()

                @pl.when(c + 1 < n_chunks)
                def _():
                    for d in loads(c + 1, 1 - b):
                        d.start()

                for d in scatters(b):
                    d.start()

        for d in scatters((n_chunks - 1) % 2):
            d.wait()

        @pl.when(wid < n_s)
        def _():
            pltpu.sync_copy(idxs_hbm.at[wid], idxs_v)
            pltpu.sync_copy(srcs_hbm.at[pl.ds(pl.multiple_of(wid * rows_s, 8), rows_s)], rowss_v)
            for k in range(TOP_K):
                pltpu.sync_copy(rowss_v, out_hbm.at[idxs_v.at[k]])

    return scatter_rows(src_p, src_s, idx_p, idx_s)


def _sc_gather_rows(src3, idx2):
    rows = SC_GATHER_ROWS
    n_chunks = idx2.shape[0] // SC_WORKERS
    assert n_chunks * SC_WORKERS == idx2.shape[0] and idx2.shape[1] == rows and n_chunks % 2 == 0

    @functools.partial(
        pl.kernel, mesh=_sc_mesh(),
        out_type=jax.ShapeDtypeStruct((idx2.shape[0] * rows, SLAB, LANES), jnp.int32),
        scratch_types=[pltpu.VMEM((2, rows), jnp.int32), pltpu.VMEM((2, rows, SLAB, LANES), jnp.int32),
                       pltpu.SemaphoreType.DMA((2,)), pltpu.SemaphoreType.DMA((2,))])
    def gather_rows(src_hbm, idx_hbm, out_hbm, idx_v, rows_v, gsem, wsem):
        wid = _sc_worker_id()

        def gather(b):
            return pltpu.make_async_copy(src_hbm.at[idx_v.at[b]], rows_v.at[b], gsem.at[b])

        def write(c, b):
            base = pl.multiple_of((wid * n_chunks + c) * rows, 8)
            return pltpu.make_async_copy(rows_v.at[b], out_hbm.at[pl.ds(base, rows)], wsem.at[b])

        pltpu.sync_copy(idx_hbm.at[wid * n_chunks], idx_v.at[0])
        gather(0).start()

        @pl.loop(0, n_chunks, step=2)
        def _(c0):
            for b in range(2):
                c = c0 + b

                @pl.when(c + 1 < n_chunks)
                def _():
                    @pl.when(c >= 1)
                    def _():
                        write(c - 1, 1 - b).wait()
                    pltpu.sync_copy(idx_hbm.at[wid * n_chunks + c + 1], idx_v.at[1 - b])
                    gather(1 - b).start()

                gather(b).wait()
                write(c, b).start()

        write(n_chunks - 2, 0).wait()
        write(n_chunks - 1, 1).wait()

    return gather_rows(src3, idx2)


FF_TILE = 256


def _ffn_kernel(blk_e_ref, nused_ref, next_e_ref, slot_ref, x_ref, wu_hbm, bu_ref, wd_hbm, bd_ref, y_ref,
                xbf, actbf, wu_f32, wd_f32, wsem):
    i = pl.program_id(0)
    tm = xbf.shape[0]
    n_tiles = D_FF // FF_TILE
    slot = slot_ref[i]

    def weight_copies(e, s):
        return (pltpu.make_async_copy(wu_hbm.at[e], wu_f32.at[s], wsem.at[s, 0]),
                pltpu.make_async_copy(wd_hbm.at[e], wd_f32.at[s], wsem.at[s, 1]))

    @pl.when(i < nused_ref[0])
    def _():
        @pl.when(i == 0)
        def _():
            for copy in weight_copies(blk_e_ref[0], slot):
                copy.start()

        @pl.when((i == 0) | (blk_e_ref[i] != blk_e_ref[jnp.maximum(i - 1, 0)]))
        def _():
            for copy in weight_copies(blk_e_ref[i], slot):
                copy.wait()

            @pl.when(next_e_ref[i] >= 0)
            def _():
                for copy in weight_copies(next_e_ref[i], 1 - slot):
                    copy.start()

        for c in range(SLAB):
            lo, hi = _load_slab_chunk(x_ref, tm, c)
            xbf[:, c * LANES:(c + 1) * LANES] = lo.astype(BF16)
            xbf[:, HALF_D + c * LANES:HALF_D + (c + 1) * LANES] = hi.astype(BF16)
        for n in range(n_tiles):
            gc = slice(n * FF_TILE, (n + 1) * FF_TILE)
            lc = slice(D_FF + n * FF_TILE, D_FF + (n + 1) * FF_TILE)
            g = jnp.minimum(_dot(xbf[...], wu_f32[slot, :, gc].astype(BF16)) + bu_ref[0, :, gc], SWIGLU_LIMIT)
            lin = jnp.clip(_dot(xbf[...], wu_f32[slot, :, lc].astype(BF16)) + bu_ref[0, :, lc],
                           -SWIGLU_LIMIT, SWIGLU_LIMIT)
            actbf[:, gc] = (g * jax.nn.sigmoid(SWIGLU_ALPHA * g) * (lin + 1.0)).astype(BF16)
        per_tile = FF_TILE // LANES
        for n in range(n_tiles // 2):
            yl = slice(n * FF_TILE, (n + 1) * FF_TILE)
            yh = slice(HALF_D + n * FF_TILE, HALF_D + (n + 1) * FF_TILE)
            y_lo = _dot(actbf[...], wd_f32[slot, :, yl].astype(BF16)) + bd_ref[0, :, yl]
            y_hi = _dot(actbf[...], wd_f32[slot, :, yh].astype(BF16)) + bd_ref[0, :, yh]
            for c in range(per_tile):
                sl = slice(c * LANES, (c + 1) * LANES)
                y_ref[pl.ds(n * per_tile + c, tm, stride=SLAB), :] = _pack_pair(y_lo[:, sl], y_hi[:, sl])

    @pl.when(i >= nused_ref[0])
    def _():
        y_ref[...] = jnp.zeros_like(y_ref)


def _ffn_call(blk_e, nused, next_e, slot, xs2, w_up, b_up, w_down, b_down, tm):
    n_blocks = blk_e.shape[0]
    row_blk = pl.BlockSpec((tm * SLAB, LANES), lambda i, be, nu, ne, sl: (i, 0))
    grid_spec = pltpu.PrefetchScalarGridSpec(
        num_scalar_prefetch=4,
        grid=(n_blocks,),
        in_specs=[
            pl.BlockSpec((tm * SLAB, LANES), lambda i, be, nu, ne, sl: (jnp.minimum(i, nu[0] - 1), 0)),
            pl.BlockSpec(memory_space=pl.ANY),
            pl.BlockSpec((1, 1, 2 * D_FF), lambda i, be, nu, ne, sl: (be[i], 0, 0)),
            pl.BlockSpec(memory_space=pl.ANY),
            pl.BlockSpec((1, 1, D_MODEL), lambda i, be, nu, ne, sl: (be[i], 0, 0)),
        ],
        out_specs=row_blk,
        scratch_shapes=[pltpu.VMEM((tm, D_MODEL), BF16), pltpu.VMEM((tm, D_FF), BF16),
                        pltpu.VMEM((2, D_MODEL, 2 * D_FF), F32), pltpu.VMEM((2, D_FF, D_MODEL), F32),
                        pltpu.SemaphoreType.DMA((2, 2))],
    )
    return pl.pallas_call(
        _ffn_kernel,
        grid_spec=grid_spec,
        out_shape=jax.ShapeDtypeStruct((n_blocks * tm * SLAB, LANES), jnp.int32),
        compiler_params=pltpu.CompilerParams(dimension_semantics=("arbitrary",),
                                             vmem_limit_bytes=VMEM_LIMIT),
        name="experts",
    )(blk_e, nused, next_e, slot, xs2, w_up, b_up.reshape(N_EXPERTS, 1, 2 * D_FF), w_down,
      b_down.reshape(N_EXPERTS, 1, D_MODEL))


def _combine_kernel(ys0_ref, ys1_ref, ys2_ref, ys3_ref, xmid_ref, gate_ref, g_final_ref, y_ref):
    tm = xmid_ref.shape[0]
    gts = jnp.transpose(jnp.concatenate([gate_ref[...], jnp.zeros((LANES - 8, tm), F32)], axis=0))
    lows, highs = [], []
    for c in range(SLAB):
        acc_lo = xmid_ref[:, c * LANES:(c + 1) * LANES]
        acc_hi = xmid_ref[:, HALF_D + c * LANES:HALF_D + (c + 1) * LANES]
        for k, ys_ref in enumerate((ys0_ref, ys1_ref, ys2_ref, ys3_ref)):
            lo, hi = _load_slab_chunk(ys_ref, tm, c)
            acc_lo = acc_lo + lo * gts[:, k:k + 1]
            acc_hi = acc_hi + hi * gts[:, k:k + 1]
        lows.append(acc_lo)
        highs.append(acc_hi)
    y_ref[...] = _rms(jnp.concatenate(lows + highs, axis=1), g_final_ref[...])


def _combine_call(ys4, t_stride, row0, x_mid, gates, g_final, tm):
    T = x_mid.shape[0]
    blk0 = row0 // tm
    per_k = t_stride // tm
    assert per_k * tm == t_stride and blk0 * tm == row0

    def ys_spec(k):
        return pl.BlockSpec((tm * SLAB, LANES), lambda i: (k * per_k + blk0 + i, 0))

    return pl.pallas_call(
        _combine_kernel,
        grid=(T // tm,),
        in_specs=[
            ys_spec(0), ys_spec(1), ys_spec(2), ys_spec(3),
            pl.BlockSpec((tm, D_MODEL), lambda i: (i, 0)),
            pl.BlockSpec((8, tm), lambda i: (0, i)),
            _full_spec((1, D_MODEL)),
        ],
        out_specs=pl.BlockSpec((tm, D_MODEL), lambda i: (i, 0)),
        out_shape=jax.ShapeDtypeStruct((T, D_MODEL), F32),
        compiler_params=pltpu.CompilerParams(dimension_semantics=("arbitrary",),
                                             vmem_limit_bytes=VMEM_LIMIT),
        name="combine",
    )(ys4, ys4, ys4, ys4, x_mid, gates, g_final)


def _t5_bucket(dist):
    n = jnp.maximum(dist, 0)
    max_exact = NUM_BUCKETS // 2
    nf = jnp.maximum(n, 1).astype(F32)
    large = max_exact + (jnp.log(nf / max_exact) / math.log(MAX_DISTANCE / max_exact)
                         * (NUM_BUCKETS - max_exact)).astype(jnp.int32)
    large = jnp.minimum(large, NUM_BUCKETS - 1)
    return jnp.where(n < max_exact, n, large)


def kernel(x_prompt, x_sample, state_gla, cache_swa_k, cache_swa_v, meta_tokens, rel_bias_table,
           g_mix, w_in, w_a_up, b_a, g_gla_out, g_swa_out, attn_sinks, w_out,
           g_ffn, w_router, b_router, w_up, b_up, w_down, b_down, g_final):
    assert g_mix.shape[0] == 1, "single-layer trunk"
    B, L, _ = x_prompt.shape
    n_seq = x_sample.shape[0]
    TP = B * L
    T_all = TP + n_seq

    wi = w_in[0]
    sizes = (GLA_QK, GLA_QK, GLA_V, GLA_V, GLA_LOWRANK, SWA_Q, SWA_KV, SWA_KV)
    offs = [0]
    for s in sizes:
        offs.append(offs[-1] + s)
    seg = [wi[:, offs[n]:offs[n + 1]] for n in range(8)]
    seg[0] = seg[0] * (GLA_DK ** -0.5)
    seg[5] = seg[5] * (SWA_HEAD_DIM ** -0.5)
    w_in_r = jnp.concatenate(
        seg[0:4] + seg[5:8] + [seg[4], jnp.zeros((D_MODEL, LANES - GLA_LOWRANK), F32)], axis=1).astype(BF16)
    w_a_pad = jnp.concatenate([w_a_up[0], jnp.zeros((LANES - GLA_LOWRANK, GLA_QK), F32)], axis=0).astype(BF16)
    wr_t = jnp.transpose(w_router[0])
    wr_hi = wr_t.astype(BF16)
    wr_lo = (wr_t - wr_hi.astype(F32)).astype(BF16)
    qi = jnp.arange(WINDOW)[:, None]
    kj = jnp.arange(2 * WINDOW)[None, :]
    buckets = jnp.arange(NUM_BUCKETS)
    table = rel_bias_table.astype(F32)
    oh_p = (_t5_bucket(qi - kj + WINDOW)[..., None] == buckets).astype(F32)
    bias_p = jnp.einsum("qkb,bh->hkq", oh_p, table, precision=lax.Precision.HIGHEST)
    in_window = jnp.transpose((kj > qi) & (kj <= qi + WINDOW))
    bias_p = jnp.where(in_window[None], bias_p, NEG_INF)
    bias_p = bias_p.reshape(SWA_KV_HEADS, SWA_GROUP, 2 * WINDOW, WINDOW).transpose(0, 2, 1, 3)
    bias_p = bias_p.reshape(SWA_KV_HEADS, 2 * WINDOW, SWA_GROUP * WINDOW)
    oh_d = (_t5_bucket(WINDOW - 1 - jnp.arange(WINDOW))[:, None] == buckets).astype(F32)
    bias_d = jnp.einsum("rb,bh->hr", oh_d, table, precision=lax.Precision.HIGHEST)
    bias_d = jnp.concatenate([bias_d, jnp.zeros((8, WINDOW), F32)], axis=0)
    wts = dict(
        sinks=attn_sinks[0].astype(F32), bias=bias_p,
        g_mix=g_mix[0][None], w_in=w_in_r, w_a_up=w_a_pad, b_a=b_a[0][None],
        g_gla=g_gla_out[0][None], g_swa=g_swa_out[0][None], w_out=w_out[0].astype(BF16),
        g_ffn=g_ffn[0][None], w_r=jnp.concatenate([wr_hi, wr_lo], axis=0), b_r=b_router[0][:, None],
    )

    x_pre = jnp.concatenate([jnp.zeros((WINDOW - N_META, D_MODEL), F32), meta_tokens.astype(F32)], axis=0)[None]
    zeros_s = jnp.zeros((GLA_QK, GLA_V), F32)
    zeros_kv = jnp.zeros((WINDOW, SWA_KV), F32)
    zeros_b = jnp.zeros((N_EXPERTS, LANES), F32)
    pre = _mixer_call(x_pre, zeros_s, zeros_kv, zeros_kv, zeros_b, wts, WINDOW, 1, WINDOW - N_META, 0)
    s_meta, k_meta, v_meta = pre[5][0], pre[6][0], pre[7][0]

    (xmid_p, hp_p, topi_p, gate_p, rank_p, s_p, k_p, v_p, cnt_p) = _mixer_call(
        x_prompt, s_meta, k_meta, v_meta, zeros_b, wts, MIX_STREAM_TM, 2, 0, WINDOW - N_META)

    (xmid_s, hp_s, topi_s, gate_s, rank_s, st_s, ck_s, cv_s, cnt_all) = _decode_call(
        x_sample[:, 0], state_gla[0], cache_swa_k[0].reshape(n_seq, WINDOW, SWA_KV),
        cache_swa_v[0].reshape(n_seq, WINDOW, SWA_KV), cnt_p, bias_d, wts)

    n_slots = T_all * TOP_K
    tm = -(-math.ceil(1.05 * n_slots / N_EXPERTS / MOE_BLOCKS_PER_EXPERT) // 16) * 16
    n_blocks = -(-n_slots // tm) + N_EXPERTS
    top_e = jnp.concatenate([topi_p[:TOP_K], topi_s[:TOP_K]], axis=1)
    rank = jnp.concatenate([rank_p[:TOP_K], rank_s[:TOP_K]], axis=1)
    counts = cnt_all[:, 0].astype(jnp.int32)
    padded = (counts + tm - 1) // tm * tm
    pad_end = jnp.cumsum(padded)
    pad_start = pad_end - padded
    e_ids = jnp.arange(N_EXPERTS, dtype=jnp.int32)
    dest = jnp.sum(jnp.where(top_e[..., None] == e_ids, pad_start, 0), axis=-1) + rank
    n_pad = n_blocks * tm
    blk_e = jnp.minimum(jnp.sum(pad_end[None] <= (jnp.arange(n_blocks, dtype=jnp.int32) * tm)[:, None], axis=1),
                        N_EXPERTS - 1).astype(jnp.int32)
    nused = (pad_end[-1] // tm).astype(jnp.int32).reshape(1)
    later_nonempty = (e_ids[None] > e_ids[:, None]) & (counts > 0)[None]
    next_nonempty = jnp.min(jnp.where(later_nonempty, e_ids[None], N_EXPERTS), axis=1)
    next_nonempty = jnp.where(next_nonempty == N_EXPERTS, -1, next_nonempty)
    next_e = jnp.sum(jnp.where(blk_e[:, None] == e_ids[None], next_nonempty[None], 0), axis=1).astype(jnp.int32)
    ordinal = jnp.cumsum((counts > 0).astype(jnp.int32)) - 1
    w_slot = (jnp.sum(jnp.where(blk_e[:, None] == e_ids[None], ordinal[None], 0), axis=1) % 2).astype(jnp.int32)

    sample_rows = 8
    idx_p = dest[:, :TP].reshape(TOP_K, TP // SC_SCATTER_ROWS, SC_SCATTER_ROWS).transpose(1, 0, 2)
    idx_s = dest[:, TP:].reshape(TOP_K, n_seq // sample_rows, sample_rows).transpose(1, 0, 2)
    xs3 = _sc_scatter_rows(hp_p.reshape(TP, SLAB, LANES), hp_s.reshape(n_seq, SLAB, LANES), idx_p, idx_s, n_pad)
    ys2 = _ffn_call(blk_e, nused, next_e, w_slot, xs3.reshape(-1, LANES), w_up[0], b_up[0], w_down[0], b_down[0], tm)
    unit = math.lcm(2 * SC_WORKERS * SC_GATHER_ROWS // TOP_K, MIX_TM)
    t_stride = -(-T_all // unit) * unit
    filler = jnp.arange(TOP_K * (t_stride - T_all), dtype=jnp.int32).reshape(TOP_K, t_stride - T_all)
    slot_src = jnp.concatenate([dest, filler], axis=1)
    slot_src = slot_src.reshape(TOP_K * t_stride // SC_GATHER_ROWS, SC_GATHER_ROWS)
    ys4 = _sc_gather_rows(ys2.reshape(-1, SLAB, LANES), slot_src).reshape(-1, LANES)

    gf = g_final[None]
    y_p = _combine_call(ys4, t_stride, 0, xmid_p.reshape(TP, D_MODEL), gate_p, gf, MIX_TM)
    y_s = _combine_call(ys4, t_stride, TP, xmid_s, gate_s, gf, n_seq)

    s_heads = jnp.stack([s_p[:, h * GLA_DK:(h + 1) * GLA_DK, h * GLA_DV:(h + 1) * GLA_DV]
                         for h in range(GLA_HEADS)], axis=1)
    return (y_p.reshape(B, L, D_MODEL), y_s.reshape(n_seq, 1, D_MODEL), s_heads[None],
            k_p.reshape(1, B, WINDOW, SWA_KV_HEADS, SWA_HEAD_DIM),
            v_p.reshape(1, B, WINDOW, SWA_KV_HEADS, SWA_HEAD_DIM),
            st_s[None], ck_s.reshape(1, n_seq, WINDOW, SWA_KV_HEADS, SWA_HEAD_DIM),
            cv_s.reshape(1, n_seq, WINDOW, SWA_KV_HEADS, SWA_HEAD_DIM))
```
